```python
import jax, jax.numpy as jnp
from jax import lax
import numpy as np

D_MODEL = 1024
BATCH = 8
SEQ = 8192
DEPTH = 1

GDN_H = 4
GDN_DK = 128
GDN_DV = 128
GDN_W = GDN_H * GDN_DV
CONV_W = 4
CHUNK = 64
SB_H = 4
SB_D = 128
SB_W = SB_H * SB_D
Q_BLOCK = 128
MEM_H = 4
MEM_D = 128
MEM_W = MEM_H * MEM_D
N_MEM = 256
N_BRANCH = 3
D_FF = 4 * D_MODEL
EPS = 1e-6
IN_SIZES = (3 * GDN_W, GDN_W, GDN_H, GDN_H, 3 * SB_W, MEM_W, N_BRANCH * D_MODEL)
D_IN = sum(IN_SIZES)

kernel_name = "hybrid_gdn_stickbreak_memxattn_gated_merge"


def rms_norm(x, g):
    xf = x.astype(jnp.float32)
    y = xf * lax.rsqrt(jnp.mean(xf * xf, axis=-1, keepdims=True) + EPS)
    return (y * g.astype(jnp.float32)).astype(x.dtype)


def l2norm(x):
    return x * lax.rsqrt(jnp.sum(x * x, axis=-1, keepdims=True) + EPS)


def causal_dwconv(x, w):
    c = x.shape[-1]
    return lax.conv_general_dilated(
        x, w[:, None, :].astype(x.dtype), window_strides=(1,),
        padding=[(w.shape[0] - 1, 0)], dimension_numbers=("NWC", "WIO", "NWC"),
        feature_group_count=c)


def gated_delta_rule_chunked(q, k, v, g, beta):
    b, s, h, dk = q.shape
    dv = v.shape[-1]
    n = s // CHUNK

    def to_chunks(t):
        return jnp.moveaxis(t.reshape(b, n, CHUNK, h, *t.shape[3:]), 3, 1)

    qc, kc, vc, gc, bc = (to_chunks(t) for t in (q, k, v, g, beta))
    gc = jnp.cumsum(gc, axis=-1)
    idx = jnp.arange(CHUNK)
    causal = idx[:, None] >= idx[None, :]
    strict = idx[:, None] > idx[None, :]
    decay = jnp.exp(jnp.where(causal, gc[..., :, None] - gc[..., None, :], -jnp.inf))
    kb = kc * bc[..., None]
    vb = vc * bc[..., None]
    lower = jnp.where(strict, jnp.einsum("bhncd,bhnsd->bhncs", kb, kc) * decay, 0.0)
    eye = jnp.eye(CHUNK, dtype=jnp.float32)
    t_inv = lax.linalg.triangular_solve(eye + lower, jnp.broadcast_to(eye, lower.shape),
                                        left_side=True, lower=True, unit_diagonal=True)
    u = jnp.einsum("bhncs,bhnsv->bhncv", t_inv, vb)
    w = jnp.einsum("bhncs,bhnsk->bhnck", t_inv, kb * jnp.exp(gc)[..., None])
    a_qk = jnp.where(causal, jnp.einsum("bhncd,bhnsd->bhncs", qc, kc) * decay, 0.0)
    q_dec = qc * jnp.exp(gc)[..., None]
    k_dec = kc * jnp.exp(gc[..., -1:] - gc)[..., None]
    g_last = jnp.exp(gc[..., -1])
    xs = tuple(jnp.moveaxis(t, 2, 0) for t in (u, w, a_qk, q_dec, k_dec, g_last))

    def step(state, inp):
        u_i, w_i, a_i, qd_i, kd_i, gl_i = inp
        v_new = u_i - jnp.einsum("bhck,bhkv->bhcv", w_i, state)
        o_i = (jnp.einsum("bhck,bhkv->bhcv", qd_i, state)
               + jnp.einsum("bhcs,bhsv->bhcv", a_i, v_new))
        state = state * gl_i[..., None, None] + jnp.einsum("bhck,bhcv->bhkv", kd_i, v_new)
        return state, o_i

    s0 = jnp.zeros((b, h, dk, dv), jnp.float32)
    _, o = lax.scan(step, s0, xs)
    o = jnp.moveaxis(jnp.moveaxis(o, 0, 2), 1, 3)
    return o.reshape(b, s, h, dv)


def stick_breaking_attention(q, k, v):
    s = q.shape[2]
    scale = q.shape[-1] ** -0.5
    outs = []
    for blk in range(s // Q_BLOCK):
        q0 = blk * Q_BLOCK
        kl = q0 + Q_BLOCK
        z = jnp.einsum("bhqd,bhkd->bhqk", q[:, :, q0:kl], k[:, :, :kl]) * scale
        mask = jnp.arange(kl)[None, :] < (q0 + jnp.arange(Q_BLOCK))[:, None]
        log_1mb = jnp.where(mask, jax.nn.log_sigmoid(-z), 0.0)
        between = lax.cumsum(log_1mb, axis=3, reverse=True) - log_1mb
        att = jnp.where(mask, jnp.exp(jax.nn.log_sigmoid(z) + between), 0.0)
        outs.append(jnp.einsum("bhqk,bhkd->bhqd", att, v[:, :, :kl]))
    return jnp.concatenate(outs, axis=2)


def _fwd_setup_inputs(seed: int = 0) -> dict:
    key = jax.random.key(seed)
    ks = jax.random.split(key, 24)
    f32 = jnp.float32
    nrm = lambda k, shape, scale: jax.random.normal(k, shape, f32) * scale
    gain = lambda k, n: 1.0 + 0.02 * jax.random.normal(k, (DEPTH, n), f32)
    dt = jnp.exp(jax.random.uniform(ks[5], (DEPTH, GDN_H), f32, np.log(1e-3), np.log(1e-1)))
    return {
        "x": nrm(ks[0], (BATCH, SEQ, D_MODEL), 1.0),
        "mem": nrm(ks[1], (BATCH, N_MEM, D_MODEL), 1.0),
        "norm1_g": gain(ks[2], D_MODEL),
        "w_in": nrm(ks[3], (DEPTH, D_MODEL, D_IN), D_MODEL ** -0.5),
        "conv_w": nrm(ks[4], (DEPTH, CONV_W, 3 * GDN_W), CONV_W ** -0.5),
        "a_log": jnp.log(jax.random.uniform(ks[6], (DEPTH, GDN_H), f32, 1.0, 16.0)),
        "dt_bias": jnp.log(jnp.expm1(dt)),
        "gdn_norm_g": gain(ks[7], GDN_DV),
        "sb_q_norm_g": gain(ks[8], SB_D),
        "sb_k_norm_g": gain(ks[9], SB_D),
        "mem_norm_g": gain(ks[10], D_MODEL),
        "w_mem_kv": nrm(ks[11], (DEPTH, D_MODEL, 2 * MEM_W), D_MODEL ** -0.5),
        "mem_q_norm_g": gain(ks[12], MEM_D),
        "mem_k_norm_g": gain(ks[13], MEM_D),
        "w_br_gdn": nrm(ks[14], (DEPTH, GDN_W, D_MODEL), GDN_W ** -0.5),
        "w_br_sb": nrm(ks[15], (DEPTH, SB_W, D_MODEL), SB_W ** -0.5),
        "w_br_mem": nrm(ks[16], (DEPTH, MEM_W, D_MODEL), MEM_W ** -0.5),
        "w_o": nrm(ks[17], (DEPTH, D_MODEL, D_MODEL), D_MODEL ** -0.5),
        "norm2_g": gain(ks[18], D_MODEL),
        "w_up": nrm(ks[19], (DEPTH, D_MODEL, D_FF), D_MODEL ** -0.5),
        "w_down": nrm(ks[20], (DEPTH, D_FF, D_MODEL), D_FF ** -0.5),
    }


def _fwd_reference(x, mem, norm1_g, w_in, conv_w, a_log, dt_bias, gdn_norm_g, sb_q_norm_g, sb_k_norm_g,
              mem_norm_g, w_mem_kv, mem_q_norm_g, mem_k_norm_g, w_br_gdn, w_br_sb, w_br_mem, w_o,
              norm2_g, w_up, w_down):
    b, s, _ = x.shape
    f32 = jnp.float32
    splits = np.cumsum(IN_SIZES)[:-1].tolist()
    for l in range(DEPTH):
        h = rms_norm(x, norm1_g[l])
        proj = h @ w_in[l]
        gdn_qkv, gdn_z, gdn_a, gdn_b, sb_qkv, mem_q, gate_logits = jnp.split(proj, splits, axis=-1)

        gdn_qkv = jax.nn.silu(causal_dwconv(gdn_qkv, conv_w[l])).astype(f32)
        gq, gk, gv = jnp.split(gdn_qkv, 3, axis=-1)
        gq = l2norm(gq.reshape(b, s, GDN_H, GDN_DK)) * (GDN_DK ** -0.5)
        gk = l2norm(gk.reshape(b, s, GDN_H, GDN_DK))
        gv = gv.reshape(b, s, GDN_H, GDN_DV)
        beta = jax.nn.sigmoid(gdn_b.astype(f32))
        g = -jnp.exp(a_log[l].astype(f32)) * jax.nn.softplus(gdn_a.astype(f32) + dt_bias[l].astype(f32))
        o_gdn = gated_delta_rule_chunked(gq, gk, gv, g, beta)
        o_gdn = rms_norm(o_gdn, gdn_norm_g[l]) * jax.nn.silu(gdn_z.astype(f32).reshape(b, s, GDN_H, GDN_DV))
        y_gdn = o_gdn.reshape(b, s, GDN_W).astype(x.dtype) @ w_br_gdn[l]

        sq, sk, sv = jnp.split(sb_qkv, 3, axis=-1)
        sq = rms_norm(sq.reshape(b, s, SB_H, SB_D), sb_q_norm_g[l])
        sk = rms_norm(sk.reshape(b, s, SB_H, SB_D), sb_k_norm_g[l])
        sv = sv.reshape(b, s, SB_H, SB_D)
        to_bhsd = lambda t: jnp.transpose(t, (0, 2, 1, 3)).astype(f32)
        o_sb = stick_breaking_attention(to_bhsd(sq), to_bhsd(sk), to_bhsd(sv))
        y_sb = jnp.transpose(o_sb, (0, 2, 1, 3)).reshape(b, s, SB_W).astype(x.dtype) @ w_br_sb[l]

        kv = rms_norm(mem, mem_norm_g[l]) @ w_mem_kv[l]
        km, vm = jnp.split(kv, 2, axis=-1)
        km = rms_norm(km.reshape(b, N_MEM, MEM_H, MEM_D), mem_k_norm_g[l]).astype(f32)
        vm = vm.reshape(b, N_MEM, MEM_H, MEM_D).astype(f32)
        qm = rms_norm(mem_q.reshape(b, s, MEM_H, MEM_D), mem_q_norm_g[l]).astype(f32)
        p = jax.nn.softmax(jnp.einsum("bshd,bmhd->bhsm", qm, km) * (MEM_D ** -0.5), axis=-1)
        o_mem = jnp.einsum("bhsm,bmhd->bshd", p, vm).reshape(b, s, MEM_W).astype(x.dtype)
        y_mem = o_mem @ w_br_mem[l]

        g_gdn, g_sb, g_mem = jnp.split(jax.nn.sigmoid(gate_logits), N_BRANCH, axis=-1)
        mix = g_gdn * y_gdn + g_sb * y_sb + g_mem * y_mem
        x = x + mix @ w_o[l]

        h2 = rms_norm(x, norm2_g[l])
        x = x + jnp.square(jax.nn.relu(h2 @ w_up[l])) @ w_down[l]
    return x


import jax as _jax
import jax.numpy as _jnp

TWIN_FORMAT = 'train_step'
FWD_PARAMS = ['x', 'mem', 'norm1_g', 'w_in', 'conv_w', 'a_log', 'dt_bias', 'gdn_norm_g', 'sb_q_norm_g', 'sb_k_norm_g', 'mem_norm_g', 'w_mem_kv', 'mem_q_norm_g', 'mem_k_norm_g', 'w_br_gdn', 'w_br_sb', 'w_br_mem', 'w_o', 'norm2_g', 'w_up', 'w_down']
TWIN_WEIGHTS = ['norm1_g', 'w_in', 'conv_w', 'a_log', 'dt_bias', 'gdn_norm_g', 'sb_q_norm_g', 'sb_k_norm_g', 'mem_norm_g', 'w_mem_kv', 'mem_q_norm_g', 'mem_k_norm_g', 'w_br_gdn', 'w_br_sb', 'w_br_mem', 'w_o', 'norm2_g', 'w_up', 'w_down']
TWIN_DIFF_INPUT = 'x'
TWIN_INPUTS = ['x', 'mem', 'norm1_g', 'w_in', 'conv_w', 'a_log', 'dt_bias', 'gdn_norm_g', 'sb_q_norm_g', 'sb_k_norm_g', 'mem_norm_g', 'w_mem_kv', 'mem_q_norm_g', 'mem_k_norm_g', 'w_br_gdn', 'w_br_sb', 'w_br_mem', 'w_o', 'norm2_g', 'w_up', 'w_down', 'loss_target', 'm_norm1_g', 'm_w_in', 'm_conv_w', 'm_a_log', 'm_dt_bias', 'm_gdn_norm_g', 'm_sb_q_norm_g', 'm_sb_k_norm_g', 'm_mem_norm_g', 'm_w_mem_kv', 'm_mem_q_norm_g', 'm_mem_k_norm_g', 'm_w_br_gdn', 'm_w_br_sb', 'm_w_br_mem', 'm_w_o', 'm_norm2_g', 'm_w_up', 'm_w_down', 'v_norm1_g', 'v_w_in', 'v_conv_w', 'v_a_log', 'v_dt_bias', 'v_gdn_norm_g', 'v_sb_q_norm_g', 'v_sb_k_norm_g', 'v_mem_norm_g', 'v_w_mem_kv', 'v_mem_q_norm_g', 'v_mem_k_norm_g', 'v_w_br_gdn', 'v_w_br_sb', 'v_w_br_mem', 'v_w_o', 'v_norm2_g', 'v_w_up', 'v_w_down']
TWIN_OUTPUTS = ['loss', 'grad_x', 'grad_norm1_g', 'grad_w_in', 'grad_conv_w', 'grad_a_log', 'grad_dt_bias', 'grad_gdn_norm_g', 'grad_sb_q_norm_g', 'grad_sb_k_norm_g', 'grad_mem_norm_g', 'grad_w_mem_kv', 'grad_mem_q_norm_g', 'grad_mem_k_norm_g', 'grad_w_br_gdn', 'grad_w_br_sb', 'grad_w_br_mem', 'grad_w_o', 'grad_norm2_g', 'grad_w_up', 'grad_w_down', 'delta_norm1_g', 'delta_w_in', 'delta_conv_w', 'delta_a_log', 'delta_dt_bias', 'delta_gdn_norm_g', 'delta_sb_q_norm_g', 'delta_sb_k_norm_g', 'delta_mem_norm_g', 'delta_w_mem_kv', 'delta_mem_q_norm_g', 'delta_mem_k_norm_g', 'delta_w_br_gdn', 'delta_w_br_sb', 'delta_w_br_mem', 'delta_w_o', 'delta_norm2_g', 'delta_w_up', 'delta_w_down', 'new_m_norm1_g', 'new_m_w_in', 'new_m_conv_w', 'new_m_a_log', 'new_m_dt_bias', 'new_m_gdn_norm_g', 'new_m_sb_q_norm_g', 'new_m_sb_k_norm_g', 'new_m_mem_norm_g', 'new_m_w_mem_kv', 'new_m_mem_q_norm_g', 'new_m_mem_k_norm_g', 'new_m_w_br_gdn', 'new_m_w_br_sb', 'new_m_w_br_mem', 'new_m_w_o', 'new_m_norm2_g', 'new_m_w_up', 'new_m_w_down', 'new_v_norm1_g', 'new_v_w_in', 'new_v_conv_w', 'new_v_a_log', 'new_v_dt_bias', 'new_v_gdn_norm_g', 'new_v_sb_q_norm_g', 'new_v_sb_k_norm_g', 'new_v_mem_norm_g', 'new_v_w_mem_kv', 'new_v_mem_q_norm_g', 'new_v_mem_k_norm_g', 'new_v_w_br_gdn', 'new_v_w_br_sb', 'new_v_w_br_mem', 'new_v_w_o', 'new_v_norm2_g', 'new_v_w_up', 'new_v_w_down']
TWIN_LEAF_KINDS = {'loss': 'loss', 'grad_x': 'grad_x', 'grad_norm1_g': 'grad_w', 'grad_w_in': 'grad_w', 'grad_conv_w': 'grad_w', 'grad_a_log': 'grad_w', 'grad_dt_bias': 'grad_w', 'grad_gdn_norm_g': 'grad_w', 'grad_sb_q_norm_g': 'grad_w', 'grad_sb_k_norm_g': 'grad_w', 'grad_mem_norm_g': 'grad_w', 'grad_w_mem_kv': 'grad_w', 'grad_mem_q_norm_g': 'grad_w', 'grad_mem_k_norm_g': 'grad_w', 'grad_w_br_gdn': 'grad_w', 'grad_w_br_sb': 'grad_w', 'grad_w_br_mem': 'grad_w', 'grad_w_o': 'grad_w', 'grad_norm2_g': 'grad_w', 'grad_w_up': 'grad_w', 'grad_w_down': 'grad_w', 'delta_norm1_g': 'delta_w', 'delta_w_in': 'delta_w', 'delta_conv_w': 'delta_w', 'delta_a_log': 'delta_w', 'delta_dt_bias': 'delta_w', 'delta_gdn_norm_g': 'delta_w', 'delta_sb_q_norm_g': 'delta_w', 'delta_sb_k_norm_g': 'delta_w', 'delta_mem_norm_g': 'delta_w', 'delta_w_mem_kv': 'delta_w', 'delta_mem_q_norm_g': 'delta_w', 'delta_mem_k_norm_g': 'delta_w', 'delta_w_br_gdn': 'delta_w', 'delta_w_br_sb': 'delta_w', 'delta_w_br_mem': 'delta_w', 'delta_w_o': 'delta_w', 'delta_norm2_g': 'delta_w', 'delta_w_up': 'delta_w', 'delta_w_down': 'delta_w', 'new_m_norm1_g': 'new_m', 'new_m_w_in': 'new_m', 'new_m_conv_w': 'new_m', 'new_m_a_log': 'new_m', 'new_m_dt_bias': 'new_m', 'new_m_gdn_norm_g': 'new_m', 'new_m_sb_q_norm_g': 'new_m', 'new_m_sb_k_norm_g': 'new_m', 'new_m_mem_norm_g': 'new_m', 'new_m_w_mem_kv': 'new_m', 'new_m_mem_q_norm_g': 'new_m', 'new_m_mem_k_norm_g': 'new_m', 'new_m_w_br_gdn': 'new_m', 'new_m_w_br_sb': 'new_m', 'new_m_w_br_mem': 'new_m', 'new_m_w_o': 'new_m', 'new_m_norm2_g': 'new_m', 'new_m_w_up': 'new_m', 'new_m_w_down': 'new_m', 'new_v_norm1_g': 'new_v', 'new_v_w_in': 'new_v', 'new_v_conv_w': 'new_v', 'new_v_a_log': 'new_v', 'new_v_dt_bias': 'new_v', 'new_v_gdn_norm_g': 'new_v', 'new_v_sb_q_norm_g': 'new_v', 'new_v_sb_k_norm_g': 'new_v', 'new_v_mem_norm_g': 'new_v', 'new_v_w_mem_kv': 'new_v', 'new_v_mem_q_norm_g': 'new_v', 'new_v_mem_k_norm_g': 'new_v', 'new_v_w_br_gdn': 'new_v', 'new_v_w_br_sb': 'new_v', 'new_v_w_br_mem': 'new_v', 'new_v_w_o': 'new_v', 'new_v_norm2_g': 'new_v', 'new_v_w_up': 'new_v', 'new_v_w_down': 'new_v'}


def _forward(args):
    return _fwd_reference(*[args[k] for k in FWD_PARAMS])


def _output_shape():
    def fwd():
        inp = _fwd_setup_inputs(0)
        return _fwd_reference(*[inp[k] for k in FWD_PARAMS])
    out = _jax.eval_shape(fwd)
    return out.shape, out.dtype

N_MICROBATCH = 1
ADAM_LR = 0.001
ADAM_B1 = 0.9
ADAM_B2 = 0.999
ADAM_EPS = 1e-08
ADAM_WD = 0.01
ADAM_STEP = 10
PER_EXAMPLE_BATCH_AXIS = {'x': 0, 'mem': 0, 'loss_target': 0}
SHARED_INPUTS = []
_WEIGHT_DTYPES = {'norm1_g': _jnp.float32, 'w_in': _jnp.float32, 'conv_w': _jnp.float32, 'a_log': _jnp.float32, 'dt_bias': _jnp.float32, 'gdn_norm_g': _jnp.float32, 'sb_q_norm_g': _jnp.float32, 'sb_k_norm_g': _jnp.float32, 'mem_norm_g': _jnp.float32, 'w_mem_kv': _jnp.float32, 'mem_q_norm_g': _jnp.float32, 'mem_k_norm_g': _jnp.float32, 'w_br_gdn': _jnp.float32, 'w_br_sb': _jnp.float32, 'w_br_mem': _jnp.float32, 'w_o': _jnp.float32, 'norm2_g': _jnp.float32, 'w_up': _jnp.float32, 'w_down': _jnp.float32}
MOMENT_SCALE = {'norm1_g': 1.639488e+01, 'w_in': 3.535434e-01, 'conv_w': 1.516063e+00, 'a_log': 6.015954e+01, 'dt_bias': 5.780115e+01, 'gdn_norm_g': 5.196291e+01, 'sb_q_norm_g': 9.485308e+00, 'sb_k_norm_g': 9.479232e+00, 'mem_norm_g': 5.897166e-01, 'w_mem_kv': 5.726209e-01, 'mem_q_norm_g': 1.415108e+00, 'mem_k_norm_g': 1.423678e+00, 'w_br_gdn': 2.840447e+00, 'w_br_sb': 5.984441e-01, 'w_br_mem': 6.167417e-01, 'w_o': 2.932370e+00, 'norm2_g': 1.907518e+02, 'w_up': 1.868082e+00, 'w_down': 1.595201e+01}


def _to_microbatches(a, axis):
    t = _jnp.moveaxis(a, axis, 0)
    t = t.reshape((N_MICROBATCH, t.shape[0] // N_MICROBATCH) + t.shape[1:])
    return _jnp.moveaxis(t, 1, axis + 1)


def setup_inputs(seed: int = 0) -> dict:
    inp = _fwd_setup_inputs(seed)
    key = _jax.random.fold_in(_jax.random.key(seed), 7919)
    shape, _ = _output_shape()
    out = dict(inp)
    out["loss_target"] = _jax.random.normal(_jax.random.fold_in(key, 0), shape, _jnp.float32)
    for i, name in enumerate(TWIN_WEIGHTS):
        w = inp[name].astype(_jnp.float32)
        if MOMENT_SCALE is None:
            s = _jnp.sqrt(_jnp.mean(_jnp.square(w)) + 1e-30)
        else:
            s = MOMENT_SCALE[name]
        km, kv = _jax.random.split(_jax.random.fold_in(key, i + 1))
        out[name] = w
        out["m_" + name] = s * _jax.random.normal(km, w.shape, _jnp.float32)
        out["v_" + name] = (s * s) * _jax.random.uniform(kv, w.shape, _jnp.float32, 0.5, 1.5)
    if N_MICROBATCH > 1:
        for name, axis in PER_EXAMPLE_BATCH_AXIS.items():
            out[name] = _to_microbatches(out[name], axis)
    return {'x': out['x'], 'mem': out['mem'], 'norm1_g': out['norm1_g'], 'w_in': out['w_in'], 'conv_w': out['conv_w'], 'a_log': out['a_log'], 'dt_bias': out['dt_bias'], 'gdn_norm_g': out['gdn_norm_g'], 'sb_q_norm_g': out['sb_q_norm_g'], 'sb_k_norm_g': out['sb_k_norm_g'], 'mem_norm_g': out['mem_norm_g'], 'w_mem_kv': out['w_mem_kv'], 'mem_q_norm_g': out['mem_q_norm_g'], 'mem_k_norm_g': out['mem_k_norm_g'], 'w_br_gdn': out['w_br_gdn'], 'w_br_sb': out['w_br_sb'], 'w_br_mem': out['w_br_mem'], 'w_o': out['w_o'], 'norm2_g': out['norm2_g'], 'w_up': out['w_up'], 'w_down': out['w_down'], 'loss_target': out['loss_target'], 'm_norm1_g': out['m_norm1_g'], 'm_w_in': out['m_w_in'], 'm_conv_w': out['m_conv_w'], 'm_a_log': out['m_a_log'], 'm_dt_bias': out['m_dt_bias'], 'm_gdn_norm_g': out['m_gdn_norm_g'], 'm_sb_q_norm_g': out['m_sb_q_norm_g'], 'm_sb_k_norm_g': out['m_sb_k_norm_g'], 'm_mem_norm_g': out['m_mem_norm_g'], 'm_w_mem_kv': out['m_w_mem_kv'], 'm_mem_q_norm_g': out['m_mem_q_norm_g'], 'm_mem_k_norm_g': out['m_mem_k_norm_g'], 'm_w_br_gdn': out['m_w_br_gdn'], 'm_w_br_sb': out['m_w_br_sb'], 'm_w_br_mem': out['m_w_br_mem'], 'm_w_o': out['m_w_o'], 'm_norm2_g': out['m_norm2_g'], 'm_w_up': out['m_w_up'], 'm_w_down': out['m_w_down'], 'v_norm1_g': out['v_norm1_g'], 'v_w_in': out['v_w_in'], 'v_conv_w': out['v_conv_w'], 'v_a_log': out['v_a_log'], 'v_dt_bias': out['v_dt_bias'], 'v_gdn_norm_g': out['v_gdn_norm_g'], 'v_sb_q_norm_g': out['v_sb_q_norm_g'], 'v_sb_k_norm_g': out['v_sb_k_norm_g'], 'v_mem_norm_g': out['v_mem_norm_g'], 'v_w_mem_kv': out['v_w_mem_kv'], 'v_mem_q_norm_g': out['v_mem_q_norm_g'], 'v_mem_k_norm_g': out['v_mem_k_norm_g'], 'v_w_br_gdn': out['v_w_br_gdn'], 'v_w_br_sb': out['v_w_br_sb'], 'v_w_br_mem': out['v_w_br_mem'], 'v_w_o': out['v_w_o'], 'v_norm2_g': out['v_norm2_g'], 'v_w_up': out['v_w_up'], 'v_w_down': out['v_w_down']}


def _loss(weights, diff, rest, loss_target):
    with _jax.named_scope("forward"):
        args = {**rest, TWIN_DIFF_INPUT: diff, **{k: w.astype(_WEIGHT_DTYPES[k]) for k, w in weights.items()}}
        y = _forward(args)
    with _jax.named_scope("loss_head"):
        err = _jnp.square(y.astype(_jnp.float32) - loss_target)
        return 0.5 * _jnp.sum(_jnp.mean(err, axis=-1)) if err.ndim else 0.5 * err


def _adamw(w, g, m, v):
    m = ADAM_B1 * m + (1.0 - ADAM_B1) * g
    v = ADAM_B2 * v + (1.0 - ADAM_B2) * _jnp.square(g)
    m_hat = m / (1.0 - ADAM_B1 ** ADAM_STEP)
    v_hat = v / (1.0 - ADAM_B2 ** ADAM_STEP)
    delta = -ADAM_LR * (m_hat / (_jnp.sqrt(v_hat) + ADAM_EPS) + ADAM_WD * w)
    return delta, m, v


def reference(x, mem, norm1_g, w_in, conv_w, a_log, dt_bias, gdn_norm_g, sb_q_norm_g, sb_k_norm_g, mem_norm_g, w_mem_kv, mem_q_norm_g, mem_k_norm_g, w_br_gdn, w_br_sb, w_br_mem, w_o, norm2_g, w_up, w_down, loss_target, m_norm1_g, m_w_in, m_conv_w, m_a_log, m_dt_bias, m_gdn_norm_g, m_sb_q_norm_g, m_sb_k_norm_g, m_mem_norm_g, m_w_mem_kv, m_mem_q_norm_g, m_mem_k_norm_g, m_w_br_gdn, m_w_br_sb, m_w_br_mem, m_w_o, m_norm2_g, m_w_up, m_w_down, v_norm1_g, v_w_in, v_conv_w, v_a_log, v_dt_bias, v_gdn_norm_g, v_sb_q_norm_g, v_sb_k_norm_g, v_mem_norm_g, v_w_mem_kv, v_mem_q_norm_g, v_mem_k_norm_g, v_w_br_gdn, v_w_br_sb, v_w_br_mem, v_w_o, v_norm2_g, v_w_up, v_w_down):
    given = dict(x=x, mem=mem, norm1_g=norm1_g, w_in=w_in, conv_w=conv_w, a_log=a_log, dt_bias=dt_bias, gdn_norm_g=gdn_norm_g, sb_q_norm_g=sb_q_norm_g, sb_k_norm_g=sb_k_norm_g, mem_norm_g=mem_norm_g, w_mem_kv=w_mem_kv, mem_q_norm_g=mem_q_norm_g, mem_k_norm_g=mem_k_norm_g, w_br_gdn=w_br_gdn, w_br_sb=w_br_sb, w_br_mem=w_br_mem, w_o=w_o, norm2_g=norm2_g, w_up=w_up, w_down=w_down, loss_target=loss_target, m_norm1_g=m_norm1_g, m_w_in=m_w_in, m_conv_w=m_conv_w, m_a_log=m_a_log, m_dt_bias=m_dt_bias, m_gdn_norm_g=m_gdn_norm_g, m_sb_q_norm_g=m_sb_q_norm_g, m_sb_k_norm_g=m_sb_k_norm_g, m_mem_norm_g=m_mem_norm_g, m_w_mem_kv=m_w_mem_kv, m_mem_q_norm_g=m_mem_q_norm_g, m_mem_k_norm_g=m_mem_k_norm_g, m_w_br_gdn=m_w_br_gdn, m_w_br_sb=m_w_br_sb, m_w_br_mem=m_w_br_mem, m_w_o=m_w_o, m_norm2_g=m_norm2_g, m_w_up=m_w_up, m_w_down=m_w_down, v_norm1_g=v_norm1_g, v_w_in=v_w_in, v_conv_w=v_conv_w, v_a_log=v_a_log, v_dt_bias=v_dt_bias, v_gdn_norm_g=v_gdn_norm_g, v_sb_q_norm_g=v_sb_q_norm_g, v_sb_k_norm_g=v_sb_k_norm_g, v_mem_norm_g=v_mem_norm_g, v_w_mem_kv=v_w_mem_kv, v_mem_q_norm_g=v_mem_q_norm_g, v_mem_k_norm_g=v_mem_k_norm_g, v_w_br_gdn=v_w_br_gdn, v_w_br_sb=v_w_br_sb, v_w_br_mem=v_w_br_mem, v_w_o=v_w_o, v_norm2_g=v_norm2_g, v_w_up=v_w_up, v_w_down=v_w_down)
    weights = {n: given[n] for n in TWIN_WEIGHTS}
    shared = {n: given[n] for n in SHARED_INPUTS}
    per_example = {n: given[n] for n in ['x', 'mem']}
    grad_fn = _jax.value_and_grad(_loss, argnums=(0, 1))

    def one_microbatch(ex, loss_target):
        ex = dict(ex)
        diff = ex.pop(TWIN_DIFF_INPUT)
        return grad_fn(weights, diff, {**shared, **ex}, loss_target)

    if N_MICROBATCH == 1:
        loss, (grad_w, grad_x) = one_microbatch(per_example, given["loss_target"])
    else:
        def body(carry, xs):
            loss_sum, grad_sum = carry
            l_k, (gw_k, gx_k) = one_microbatch(xs[0], xs[1])
            with _jax.named_scope("update"):
                return (loss_sum + l_k, _jax.tree.map(_jnp.add, grad_sum, gw_k)), gx_k

        init = (_jnp.zeros((), _jnp.float32), _jax.tree.map(_jnp.zeros_like, weights))
        (loss, grad_w), grad_x = _jax.lax.scan(body, init, (per_example, given["loss_target"]))
    with _jax.named_scope("update"):
        delta_w, new_m, new_v = {}, {}, {}
        for n in TWIN_WEIGHTS:
            delta_w[n], new_m[n], new_v[n] = _adamw(weights[n], grad_w[n], given["m_" + n], given["v_" + n])
    return (loss, grad_x, *[grad_w[n] for n in TWIN_WEIGHTS], *[delta_w[n] for n in TWIN_WEIGHTS],
            *[new_m[n] for n in TWIN_WEIGHTS], *[new_v[n] for n in TWIN_WEIGHTS])
```

```python
import functools

import jax
import jax.numpy as jnp
import numpy as np
from jax import lax
from jax.experimental import pallas as pl
from jax.experimental.pallas import tpu as pltpu

F32 = jnp.float32
BF16 = jnp.bfloat16
MESH = pl.DeviceIdType.MESH

D_MODEL = 1024
N_HEAD = 4
D_HEAD = 128
BR_W = N_HEAD * D_HEAD
CONV_TAPS = 4
GDN_CHUNK = 64
INV_BLOCK = 16
N_MEM = 256
D_FF = 4 * D_MODEL
EPS = 1e-6
LANES = 128
PROJ_W = 7168
GATE_OFF = 4096
SB_OFF = 2048
MEMQ_OFF = 3584
Z_OFF = 1536

ADAM_LR, ADAM_B1, ADAM_B2, ADAM_EPS, ADAM_WD, ADAM_STEP = 0.001, 0.9, 0.999, 1e-08, 0.01, 10

TM = 512
TK_TOK = 512
G1_TM = 256
SB_BLK = 512
SB_W = 256
VMEM_LIMIT = 48 << 20

BIG = (("w_in", (1024, 1794), 1), ("w_mem_kv", (256, 1024), 0), ("w_br_gdn", (512, 256), 1),
       ("w_br_sb", (512, 256), 1), ("w_br_mem", (512, 256), 1), ("w_o", (256, 1024), 0),
       ("w_up", (1024, 1024), 1), ("w_down", (1024, 1024), 0))
BIG_ELEMS = sum(s[0] * s[1] for _, s, _ in BIG)
PACK_ROWS = 38912
HALF_ROWS = PACK_ROWS // 2
ROW_BLK = 1024
SMALL = (("norm1_g", 1024), ("mem_norm_g", 1024), ("norm2_g", 1024), ("gdn_norm_g", 128), ("sb_q_norm_g", 128),
         ("sb_k_norm_g", 128), ("mem_q_norm_g", 128), ("mem_k_norm_g", 128), ("a_log", 4), ("dt_bias", 4))
SMALL_ROWS = 32
CONV_ROWS = 48
SPACK_ROWS = SMALL_ROWS + CONV_ROWS
APACK_ROWS = SMALL_ROWS + 16

WEIGHTS = ("norm1_g", "w_in", "conv_w", "a_log", "dt_bias", "gdn_norm_g", "sb_q_norm_g", "sb_k_norm_g",
           "mem_norm_g", "w_mem_kv", "mem_q_norm_g", "mem_k_norm_g", "w_br_gdn", "w_br_sb", "w_br_mem", "w_o",
           "norm2_g", "w_up", "w_down")


def _cp(*sem):
    return pltpu.CompilerParams(dimension_semantics=sem if sem else None, vmem_limit_bytes=VMEM_LIMIT)


_NN = (((1,), (0,)), ((), ()))
_NT = (((1,), (1,)), ((), ()))
_TN = (((0,), (0,)), ((), ()))


def _dot(a, b, dims=_NN):
    return lax.dot_general(a.astype(BF16), b.astype(BF16), dims, preferred_element_type=F32)


def _dot_nt(a, b):
    return _dot(a, b, _NT)


def _dot_tn(a, b):
    return _dot(a, b, _TN)


def _dotf(a, b, dims=_NN):
    return lax.dot_general(a, b, dims, precision=lax.Precision.HIGHEST, preferred_element_type=F32)


def _sigmoid(v):
    return 1.0 / (1.0 + jnp.exp(-v))


def _softplus(v):
    return jnp.maximum(v, 0.0) + jnp.log(1.0 + jnp.exp(-jnp.abs(v)))


def _iota(shape, dim):
    return lax.broadcasted_iota(jnp.int32, shape, dim)


def _hs(h):
    return slice(h * D_HEAD, (h + 1) * D_HEAD)


def _rowspec(tm, w, col=0):
    return pl.BlockSpec((tm, w), lambda i: (i, col))


def _full(shape):
    return pl.BlockSpec(shape, lambda *_: (0,) * len(shape))


def _accum(ref, first, val):
    @pl.when(first)
    def _():
        ref[...] = val

    @pl.when(jnp.logical_not(first))
    def _():
        ref[...] += val


def _mm(a, b, mode, out_dtype, name, *, tm=None, tn=None, tk=None, a_fn=None, extra=None, epi=None):
    if mode == "tn":
        (K, M), N = a.shape, b.shape[1]
    else:
        (M, K), N = a.shape, (b.shape[0] if mode == "nt" else b.shape[1])
    tm = min(tm or (1024 if mode == "tn" else TM), M)
    tn = min(tn or 1024, N)
    tk = min(tk or (TK_TOK if mode == "tn" else 1024), K)
    nm, nn, nk = M // tm, N // tn, K // tk
    assert nm * tm == M and nn * tn == N and nk * tk == K, (name, a.shape, b.shape)
    if mode == "tn":
        a_spec = pl.BlockSpec((tk, tm), lambda i, j, k: (k, i))
    else:
        a_spec = pl.BlockSpec((tm, tk), lambda i, j, k: (i, k))
    if mode == "nt":
        b_spec = pl.BlockSpec((tn, tk), lambda i, j, k: (j, k))
    else:
        b_spec = pl.BlockSpec((tk, tn), lambda i, j, k: (k, j))
    dims = {"nn": _NN, "nt": _NT, "tn": _TN}[mode]
    o_spec = pl.BlockSpec((tm, tn), lambda i, j, k: (i, j))
    has_extra = extra is not None

    def body(*refs):
        a_ref, b_ref = refs[0], refs[1]
        e_ref = refs[2] if has_extra else None
        o_ref = refs[2 + has_extra]
        av = a_ref[...]
        if a_fn is not None:
            av = a_fn(av)
        p = lax.dot_general(av, b_ref[...], dims, preferred_element_type=F32)

        def finish(acc):
            if epi is not None:
                acc = epi(acc, e_ref[...] if has_extra else None)
            o_ref[...] = acc.astype(out_dtype)

        if nk == 1:
            finish(p)
        else:
            acc_ref = refs[3 + has_extra]
            k = pl.program_id(2)
            _accum(acc_ref, k == 0, p)

            @pl.when(k == nk - 1)
            def _():
                finish(acc_ref[...])

    ins = [a, b] + ([extra] if has_extra else [])
    in_specs = [a_spec, b_spec] + ([o_spec] if has_extra else [])
    return pl.pallas_call(
        body, out_shape=jax.ShapeDtypeStruct((M, N), out_dtype), grid=(nm, nn, nk), in_specs=in_specs,
        out_specs=o_spec, scratch_shapes=[pltpu.VMEM((tm, tn), F32)] if nk > 1 else [], name=name,
        compiler_params=_cp("parallel", "parallel", "arbitrary"))(*ins)


def _relu2(u):
    r = jnp.maximum(u.astype(F32), 0.0)
    return (r * r).astype(BF16)


def _epi_add(acc, e):
    return acc + e.astype(F32)


def _epi_drelu2(acc, u):
    return acc * (2.0 * jnp.maximum(u.astype(F32), 0.0))


def _rms_fwd(x, g, name):
    T, dm = x.shape
    tm = min(TM, T)

    def body(x_ref, g_ref, h_ref):
        xv = x_ref[...]
        r = lax.rsqrt(jnp.mean(xv * xv, axis=-1, keepdims=True) + EPS)
        h_ref[...] = (xv * r * g_ref[...]).astype(BF16)

    return pl.pallas_call(
        body, out_shape=jax.ShapeDtypeStruct((T, dm), BF16), grid=(T // tm,),
        in_specs=[_rowspec(tm, dm), _full((1, dm))], out_specs=_rowspec(tm, dm), name=name,
        compiler_params=_cp("parallel"))(x, g)


def _rms_bwd(dh, x, g, resid, name):
    T, dm = x.shape
    tm = min(TM, T)

    def body(dh_ref, x_ref, g_ref, res_ref, dx_ref, dxb_ref, dg_ref):
        i = pl.program_id(0)
        xv = x_ref[...]
        r = lax.rsqrt(jnp.mean(xv * xv, axis=-1, keepdims=True) + EPS)
        y = xv * r
        dhv = dh_ref[...].astype(F32)
        dy = dhv * g_ref[...]
        dx = res_ref[...] + r * (dy - y * jnp.mean(dy * y, axis=-1, keepdims=True))
        dx_ref[...] = dx
        dxb_ref[...] = dx.astype(BF16)
        _accum(dg_ref, i == 0, jnp.sum(dhv * y, axis=0, keepdims=True))

    return pl.pallas_call(
        body,
        out_shape=(jax.ShapeDtypeStruct((T, dm), F32), jax.ShapeDtypeStruct((T, dm), BF16),
                   jax.ShapeDtypeStruct((1, dm), F32)),
        grid=(T // tm,),
        in_specs=[_rowspec(tm, dm), _rowspec(tm, dm), _full((1, dm)), _rowspec(tm, dm)],
        out_specs=(_rowspec(tm, dm), _rowspec(tm, dm), _full((1, dm))), name=name,
        compiler_params=_cp("arbitrary"))(dh, x, g, resid)


def _conv_tile(x_ref, halo_ref, w_ref, xpad, tm):
    i = pl.program_id(0)
    halo = halo_ref[...].astype(F32)[8:16]
    xpad[0:8, :] = jnp.where(i > 0, halo, 0.0)
    xpad[8:, :] = x_ref[...].astype(F32)
    w = w_ref[...]
    xc = w[0:1] * xpad[5:5 + tm, :]
    for j in range(1, CONV_TAPS):
        xc = xc + w[j:j + 1] * xpad[5 + j:5 + j + tm, :]
    return xc


def _gate_terms(ab_ref, av_ref):
    abv = ab_ref[...]
    av = av_ref[...]
    pre = abv + av[1:2]
    ea = jnp.exp(av[0:1])
    g = -ea * _softplus(pre)
    return abv, pre, ea, g


def _gdn_pre(proj, conv_w, ab, avec):
    T = proj.shape[0]
    tm = min(G1_TM, T)
    cw = 3 * BR_W

    def body(x_ref, halo_ref, w_ref, ab_ref, av_ref, q_ref, k_ref, v_ref, gb_ref, xpad):
        xc = _conv_tile(x_ref, halo_ref, w_ref, xpad, tm)
        y = xc * _sigmoid(xc)
        for h in range(N_HEAD):
            for off, ref, scale in ((0, q_ref, D_HEAD ** -0.5), (BR_W, k_ref, 1.0)):
                yh = y[:, off + h * D_HEAD:off + (h + 1) * D_HEAD]
                r = lax.rsqrt(jnp.sum(yh * yh, axis=-1, keepdims=True) + EPS)
                ref[:, _hs(h)] = yh * (r * scale)
        v_ref[...] = y[:, 2 * BR_W:]
        abv, _, _, g = _gate_terms(ab_ref, av_ref)
        lane = _iota((tm, LANES), 1)
        gb_ref[...] = jnp.where(lane < N_HEAD, g, jnp.where(lane < 2 * N_HEAD, _sigmoid(abv), 0.0))

    hb = tm // 16
    return pl.pallas_call(
        body,
        out_shape=(jax.ShapeDtypeStruct((T, BR_W), F32),) * 3 + (jax.ShapeDtypeStruct((T, LANES), F32),),
        grid=(T // tm,),
        in_specs=[_rowspec(tm, cw), pl.BlockSpec((16, cw), lambda i: (jnp.maximum(i * hb - 1, 0), 0)),
                  _full((CONV_TAPS, cw)), _rowspec(tm, LANES), _full((2, LANES))],
        out_specs=(_rowspec(tm, BR_W),) * 3 + (_rowspec(tm, LANES),),
        scratch_shapes=[pltpu.VMEM((tm + 8, cw), F32)], name="gdn_pre",
        compiler_params=_cp("parallel"))(proj, proj, conv_w, ab, avec)


def _gdn_pre_bwd(proj, conv_w, ab, avec, dq, dk, dv, dgb):
    T = proj.shape[0]
    tm = min(G1_TM, T)
    cw = 3 * BR_W

    def body(x_ref, halo_ref, w_ref, ab_ref, av_ref, dq_ref, dk_ref, dv_ref, dgb_ref,
             dxc_ref, dab_ref, dcw_ref, dav_ref, xpad):
        i = pl.program_id(0)

        @pl.when(i == 0)
        def _():
            dcw_ref[...] = jnp.zeros_like(dcw_ref)
            dav_ref[...] = jnp.zeros_like(dav_ref)

        xc_all = _conv_tile(x_ref, halo_ref, w_ref, xpad, tm)
        for s in range(cw // D_HEAD):
            cs = slice(s * D_HEAD, (s + 1) * D_HEAD)
            xc = xc_all[:, cs]
            sg = _sigmoid(xc)
            yh = xc * sg
            h = s % N_HEAD
            if s < 2 * N_HEAD:
                dref, scale = (dq_ref, D_HEAD ** -0.5) if s < N_HEAD else (dk_ref, 1.0)
                r = lax.rsqrt(jnp.sum(yh * yh, axis=-1, keepdims=True) + EPS)
                yn = yh * r
                dn = dref[:, _hs(h)]
                dy = (scale * r) * (dn - yn * jnp.sum(yn * dn, axis=-1, keepdims=True))
            else:
                dy = dv_ref[:, _hs(h)]
            dxc = dy * (sg * (1.0 + xc * (1.0 - sg)))
            dxc_ref[:, cs] = dxc.astype(BF16)
            for j in range(CONV_TAPS):
                dcw_ref[j:j + 1, cs] += jnp.sum(dxc * xpad[5 + j:5 + j + tm, cs], axis=0, keepdims=True)

        abv, pre, ea, g = _gate_terms(ab_ref, av_ref)
        dgbv = dgb_ref[...]
        lane = _iota((tm, LANES), 1)
        is_a = lane < N_HEAD
        da = jnp.where(is_a, dgbv * (-ea) * _sigmoid(pre), 0.0)
        bs = _sigmoid(abv)
        db = jnp.where(jnp.logical_and(lane >= N_HEAD, lane < 2 * N_HEAD), dgbv * bs * (1.0 - bs), 0.0)
        dab_ref[...] = (da + db).astype(BF16)
        dav_ref[0:1, :] += jnp.sum(jnp.where(is_a, dgbv * g, 0.0), axis=0, keepdims=True)
        dav_ref[1:2, :] += jnp.sum(da, axis=0, keepdims=True)

    hb = tm // 16
    return pl.pallas_call(
        body,
        out_shape=(jax.ShapeDtypeStruct((T, cw), BF16), jax.ShapeDtypeStruct((T, LANES), BF16),
                   jax.ShapeDtypeStruct((CONV_TAPS, cw), F32), jax.ShapeDtypeStruct((2, LANES), F32)),
        grid=(T // tm,),
        in_specs=[_rowspec(tm, cw), pl.BlockSpec((16, cw), lambda i: (jnp.maximum(i * hb - 1, 0), 0)),
                  _full((CONV_TAPS, cw)), _rowspec(tm, LANES), _full((2, LANES)),
                  _rowspec(tm, BR_W), _rowspec(tm, BR_W), _rowspec(tm, BR_W), _rowspec(tm, LANES)],
        out_specs=(_rowspec(tm, cw), _rowspec(tm, LANES), _full((CONV_TAPS, cw)), _full((2, LANES))),
        scratch_shapes=[pltpu.VMEM((tm + 8, cw), F32)], name="gdn_pre_bwd",
        compiler_params=_cp("arbitrary"))(proj, proj, conv_w, ab, avec, dq, dk, dv, dgb)


def _conv_bwd(dxc, conv_w):
    T, cw = dxc.shape
    tm = min(G1_TM, T)
    nt = T // tm
    hb = tm // 16

    def body(d_ref, halo_ref, w_ref, dx_ref, xpad):
        i = pl.program_id(0)
        xpad[0:tm, :] = d_ref[...].astype(F32)
        xpad[tm:, :] = jnp.where(i < nt - 1, halo_ref[...].astype(F32)[0:8], 0.0)
        w = w_ref[...]
        dx = w[3:4] * xpad[0:tm, :]
        for j in range(CONV_TAPS - 1):
            dx = dx + w[j:j + 1] * xpad[3 - j:3 - j + tm, :]
        dx_ref[...] = dx.astype(BF16)

    return pl.pallas_call(
        body, out_shape=jax.ShapeDtypeStruct((T, cw), BF16), grid=(nt,),
        in_specs=[_rowspec(tm, cw), pl.BlockSpec((16, cw), lambda i: (jnp.minimum((i + 1) * hb, T // 16 - 1), 0)),
                  _full((CONV_TAPS, cw))],
        out_specs=_rowspec(tm, cw), scratch_shapes=[pltpu.VMEM((tm + 8, cw), F32)], name="conv_bwd",
        compiler_params=_cp("parallel"))(dxc, dxc, conv_w)


def _chunk_consts():
    C = GDN_CHUNK
    row, col = _iota((C, C), 0), _iota((C, C), 1)
    return row, col, row >= col, row > col


def _chunk_decay(gbv, incl):
    c_all = _dotf(incl.astype(F32), gbv)
    c_t = jnp.concatenate([c_all, jnp.zeros_like(c_all)], axis=0).T[:, :GDN_CHUNK]
    return c_all, c_t


def _head_decay(c_all, c_t, gbv, incl, h):
    C = GDN_CHUNK
    c_col = c_all[:, h:h + 1]
    c_row = c_t[h:h + 1, :]
    gam = jnp.exp(jnp.where(incl, c_col - c_row, -1e30))
    c_last = c_all[C - 1:C, h:h + 1]
    return gam, jnp.exp(c_col), jnp.exp(c_last - c_col), jnp.exp(c_last), gbv[:, N_HEAD + h:N_HEAD + h + 1]


def _unit_lower_inverse(m, row, col):
    bi, bj = row // INV_BLOCK, col // INV_BLOCK
    eye = (row == col).astype(F32)
    n = jnp.where(bi == bj, -m, 0.0)
    inv = eye + n
    size = 2
    while size < INV_BLOCK:
        n = _dotf(n, n)
        inv = inv + _dotf(inv, n)
        size *= 2
    width = 2
    while width * INV_BLOCK <= GDN_CHUNK:
        e = jnp.where(jnp.logical_and(bi // width == bj // width, bi // (width // 2) > bj // (width // 2)), m, 0.0)
        inv = inv - _dotf(_dotf(inv, e), inv)
        width *= 2
    return inv


def _gdn_fwd(q, k, v, gb, proj, gnorm):
    T = q.shape[0]
    C = GDN_CHUNK
    nc = T // C
    zcol = Z_OFF // BR_W

    def body(q_ref, k_ref, v_ref, gb_ref, z_ref, gn_ref, og_ref, oraw_ref, sh_ref, ti_ref, vn_ref, s_ref):
        @pl.when(pl.program_id(0) == 0)
        def _():
            s_ref[...] = jnp.zeros_like(s_ref)

        row, col, incl, strict = _chunk_consts()
        gbv = gb_ref[...]
        c_all, c_t = _chunk_decay(gbv, incl)
        for h in range(N_HEAD):
            gam, gcol, dcol, glast, bcol = _head_decay(c_all, c_t, gbv, incl, h)
            Q, K, V = q_ref[:, _hs(h)], k_ref[:, _hs(h)], v_ref[:, _hs(h)]
            S = s_ref[h]
            sh_ref[0, h] = S
            m = jnp.where(strict, _dot_nt(K * bcol, K) * gam, 0.0)
            tinv = _unit_lower_inverse(m, row, col)
            R = bcol * (V - gcol * _dot(K, S))
            vn = _dot(tinv, R)
            P = _dot_nt(Q, K) * gam
            O = gcol * _dot(Q, S) + _dot(P, vn)
            s_ref[h] = glast * S + _dot_tn(K * dcol, vn)
            ti_ref[0, h] = tinv
            vn_ref[:, _hs(h)] = vn
            oraw_ref[:, _hs(h)] = O
            rr = lax.rsqrt(jnp.mean(O * O, axis=-1, keepdims=True) + EPS)
            zz = z_ref[:, _hs(h)].astype(F32)
            og_ref[:, _hs(h)] = (O * rr * gn_ref[...] * (zz * _sigmoid(zz))).astype(BF16)

    cspec = lambda w, cb=0: pl.BlockSpec((C, w), lambda n: (n, cb))
    hist = lambda a, b: pl.BlockSpec((1, N_HEAD, a, b), lambda n: (n, 0, 0, 0))
    return pl.pallas_call(
        body,
        out_shape=(jax.ShapeDtypeStruct((T, BR_W), BF16), jax.ShapeDtypeStruct((T, BR_W), F32),
                   jax.ShapeDtypeStruct((nc, N_HEAD, D_HEAD, D_HEAD), F32), jax.ShapeDtypeStruct((nc, N_HEAD, C, C), F32),
                   jax.ShapeDtypeStruct((T, BR_W), F32)),
        grid=(nc,),
        in_specs=[cspec(BR_W), cspec(BR_W), cspec(BR_W), cspec(LANES), cspec(BR_W, zcol), _full((1, D_HEAD))],
        out_specs=(cspec(BR_W), cspec(BR_W), hist(D_HEAD, D_HEAD), hist(C, C), cspec(BR_W)),
        scratch_shapes=[pltpu.VMEM((N_HEAD, D_HEAD, D_HEAD), F32)], name="gdn_chunk_fwd",
        compiler_params=_cp("arbitrary"))(q, k, v, gb, proj, gnorm)


def _gdn_bwd(q, k, v, gb, proj, gnorm, oraw, shist, tinv_all, vn_all, dog):
    T = q.shape[0]
    C = GDN_CHUNK
    nc = T // C
    zcol = Z_OFF // BR_W

    def body(q_ref, k_ref, v_ref, gb_ref, z_ref, gn_ref, oraw_ref, sh_ref, ti_ref, vn_ref, dog_ref,
             dq_ref, dk_ref, dv_ref, dgb_ref, dz_ref, dgn_ref, ds_ref):
        @pl.when(pl.program_id(0) == 0)
        def _():
            ds_ref[...] = jnp.zeros_like(ds_ref)
            dgn_ref[...] = jnp.zeros_like(dgn_ref)

        row, col, incl, strict = _chunk_consts()
        gbv = gb_ref[...]
        c_all, c_t = _chunk_decay(gbv, incl)
        lane = _iota((C, LANES), 1)
        rowl = _iota((C, LANES), 0)
        ones = jnp.ones((C, LANES), F32)
        dc_all = jnp.zeros((C, LANES), F32)
        db_all = jnp.zeros((C, LANES), F32)
        dgn = jnp.zeros((1, D_HEAD), F32)
        gn = gn_ref[...]
        for h in range(N_HEAD):
            gam, gcol, dcol, glast, bcol = _head_decay(c_all, c_t, gbv, incl, h)
            Q, K, V = q_ref[:, _hs(h)], k_ref[:, _hs(h)], v_ref[:, _hs(h)]
            O = oraw_ref[:, _hs(h)]
            zz = z_ref[:, _hs(h)].astype(F32)
            dogv = dog_ref[:, _hs(h)].astype(F32)
            rr = lax.rsqrt(jnp.mean(O * O, axis=-1, keepdims=True) + EPS)
            on = O * rr
            sg = _sigmoid(zz)
            dz_ref[:, _hs(h)] = (dogv * on * gn * (sg * (1.0 + zz * (1.0 - sg)))).astype(BF16)
            dyn = dogv * (zz * sg)
            dgn = dgn + jnp.sum(dyn * on, axis=0, keepdims=True)
            dyv = dyn * gn
            dO = rr * (dyv - on * jnp.mean(dyv * on, axis=-1, keepdims=True))
            S = sh_ref[0, h]
            tinv = ti_ref[0, h]
            vn = vn_ref[:, _hs(h)]
            dSn = ds_ref[h]
            Kb = K * bcol
            M = jnp.where(strict, _dot_nt(Kb, K) * gam, 0.0)
            P = _dot_nt(Q, K) * gam
            KS = _dot(K, S)
            QS = _dot(Q, S)
            dvn = _dot_tn(P, dO) + _dot(K * dcol, dSn)
            dR = _dot_tn(tinv, dvn)
            dP = jnp.where(incl, _dot_nt(dO, vn), 0.0)
            dM = jnp.where(strict, -_dot_nt(dR, vn), 0.0)
            dPG = dP * gam
            dMG = dM * gam
            E = _dot_nt(vn, dSn)
            dKb = _dot(dMG, K)
            bg = bcol * gcol
            dq_ref[:, _hs(h)] = gcol * _dot_nt(dO, S) + _dot(dPG, K)
            dk_ref[:, _hs(h)] = (_dot_tn(dPG, Q) + _dot_tn(dMG, Kb) + bcol * dKb - bg * _dot_nt(dR, S) + dcol * E)
            dv_ref[:, _hs(h)] = bcol * dR
            dbeta = (jnp.sum(dKb * K, axis=-1, keepdims=True)
                     + jnp.sum(dR * (V - gcol * KS), axis=-1, keepdims=True))
            ds_ref[h] = glast * dSn + _dot_tn(gcol * Q, dO) - _dot_tn(bg * K, dR)
            X = dP * P + dM * M
            ddel = jnp.sum(K * E, axis=-1, keepdims=True) * dcol
            colsum = _dotf(X, ones, _TN)[:, 0:1]
            dc = (jnp.sum(X, axis=-1, keepdims=True) - colsum + gcol * jnp.sum(dO * QS, axis=-1, keepdims=True)
                  - bg * jnp.sum(dR * KS, axis=-1, keepdims=True) - ddel)
            last = jnp.sum(ddel, axis=0, keepdims=True) + glast * jnp.sum(jnp.sum(dSn * S, axis=-1, keepdims=True),
                                                                           axis=0, keepdims=True)
            dc_all = dc_all + jnp.where(lane == h, dc + jnp.where(rowl == C - 1, last, 0.0), 0.0)
            db_all = db_all + jnp.where(lane == N_HEAD + h, dbeta, 0.0)
        dgb_ref[...] = _dotf((col >= row).astype(F32), dc_all) + db_all
        dgn_ref[...] += dgn

    cspec = lambda w, cb=0: pl.BlockSpec((C, w), lambda n: (nc - 1 - n, cb))
    hist = lambda a, b: pl.BlockSpec((1, N_HEAD, a, b), lambda n: (nc - 1 - n, 0, 0, 0))
    return pl.pallas_call(
        body,
        out_shape=(jax.ShapeDtypeStruct((T, BR_W), F32),) * 3 + (
            jax.ShapeDtypeStruct((T, LANES), F32), jax.ShapeDtypeStruct((T, BR_W), BF16),
            jax.ShapeDtypeStruct((1, D_HEAD), F32)),
        grid=(nc,),
        in_specs=[cspec(BR_W), cspec(BR_W), cspec(BR_W), cspec(LANES), cspec(BR_W, zcol), _full((1, D_HEAD)),
                  cspec(BR_W), hist(D_HEAD, D_HEAD), hist(C, C), cspec(BR_W), cspec(BR_W)],
        out_specs=(cspec(BR_W), cspec(BR_W), cspec(BR_W), cspec(LANES), cspec(BR_W), _full((1, D_HEAD))),
        scratch_shapes=[pltpu.VMEM((N_HEAD, D_HEAD, D_HEAD), F32)], name="gdn_chunk_bwd",
        compiler_params=_cp("arbitrary"))(q, k, v, gb, proj, gnorm, oraw, shist, tinv_all, vn_all, dog)


def _sb_pre(proj, gq, gk):
    T = proj.shape[0]
    tm = min(TM, T)

    def body(x_ref, gq_ref, gk_ref, q_ref, k_ref, v_ref):
        for h in range(N_HEAD):
            for off, g_ref, ref in ((0, gq_ref, q_ref), (BR_W, gk_ref, k_ref)):
                xh = x_ref[:, off + h * D_HEAD:off + (h + 1) * D_HEAD].astype(F32)
                r = lax.rsqrt(jnp.mean(xh * xh, axis=-1, keepdims=True) + EPS)
                ref[:, _hs(h)] = (xh * r * g_ref[...]).astype(BF16)
        v_ref[...] = x_ref[:, 2 * BR_W:]

    return pl.pallas_call(
        body, out_shape=(jax.ShapeDtypeStruct((T, BR_W), BF16),) * 3, grid=(T // tm,),
        in_specs=[_rowspec(tm, 3 * BR_W), _full((1, D_HEAD)), _full((1, D_HEAD))],
        out_specs=(_rowspec(tm, BR_W),) * 3, name="sb_pre", compiler_params=_cp("parallel"))(proj, gq, gk)


def _sb_pre_bwd(sbx, gq, gk, dq, dk, dv):
    T = sbx.shape[0]
    tm = min(TM, T)

    def body(x_ref, gq_ref, gk_ref, dq_ref, dk_ref, dv_ref, dx_ref, dgq_ref, dgk_ref):
        i = pl.program_id(0)

        @pl.when(i == 0)
        def _():
            dgq_ref[...] = jnp.zeros_like(dgq_ref)
            dgk_ref[...] = jnp.zeros_like(dgk_ref)

        for off, g_ref, d_ref, dg_ref in ((0, gq_ref, dq_ref, dgq_ref), (BR_W, gk_ref, dk_ref, dgk_ref)):
            dg = jnp.zeros((1, D_HEAD), F32)
            for h in range(N_HEAD):
                cs = slice(off + h * D_HEAD, off + (h + 1) * D_HEAD)
                xh = x_ref[:, cs].astype(F32)
                r = lax.rsqrt(jnp.mean(xh * xh, axis=-1, keepdims=True) + EPS)
                y = xh * r
                dn = d_ref[:, _hs(h)]
                dg = dg + jnp.sum(dn * y, axis=0, keepdims=True)
                dy = dn * g_ref[...]
                dx_ref[:, cs] = (r * (dy - y * jnp.mean(dy * y, axis=-1, keepdims=True))).astype(BF16)
            dg_ref[...] += dg
        dx_ref[:, 2 * BR_W:] = dv_ref[...].astype(BF16)

    return pl.pallas_call(
        body,
        out_shape=(jax.ShapeDtypeStruct((T, 3 * BR_W), BF16), jax.ShapeDtypeStruct((1, D_HEAD), F32),
                   jax.ShapeDtypeStruct((1, D_HEAD), F32)),
        grid=(T // tm,),
        in_specs=[_rowspec(tm, 3 * BR_W), _full((1, D_HEAD)), _full((1, D_HEAD)),
                  _rowspec(tm, BR_W), _rowspec(tm, BR_W), _rowspec(tm, BR_W)],
        out_specs=(_rowspec(tm, 3 * BR_W), _full((1, D_HEAD)), _full((1, D_HEAD))), name="sb_pre_bwd",
        compiler_params=_cp("arbitrary"))(sbx, gq, gk, dq, dk, dv)


def _split_sum(x, tri):
    hi = x.astype(BF16)
    lo = (x - hi.astype(F32)).astype(BF16)
    return (lax.dot_general(hi, tri, _NN, preferred_element_type=F32)
            + lax.dot_general(lo, tri, _NN, preferred_element_type=F32))


def _sb_scores(q_ref, k_ref, diag):
    blk = q_ref.shape[0]
    z = _dot_nt(q_ref[...], k_ref[...]) * (D_HEAD ** -0.5)
    mask = jnp.logical_or(_iota((blk, blk), 1) < _iota((blk, blk), 0), jnp.logical_not(diag))
    return z, jnp.where(mask, -_softplus(z), 0.0), mask


def _sb_fwd(sq, sk, sv):
    T = sq.shape[0]
    blk = min(SB_BLK, T)
    w = min(SB_W, blk)
    nb, nsub = T // blk, blk // w

    def body(q_ref, k_ref, v_ref, o_ref, lt_ref, acc_ref, r_ref):
        qi, kk = pl.program_id(1), pl.program_id(2)

        @pl.when(kk == 0)
        def _():
            acc_ref[...] = jnp.zeros_like(acc_ref)
            r_ref[...] = jnp.zeros_like(r_ref)

        @pl.when(kk <= qi)
        def _():
            z, l, mask = _sb_scores(q_ref, k_ref, kk == 0)
            after = (_iota((w, w), 0) > _iota((w, w), 1)).astype(BF16)
            r = r_ref[...]
            acc = acc_ref[...]
            for sb in reversed(range(nsub)):
                cs = slice(sb * w, (sb + 1) * w)
                ls = l[:, cs]
                log_a = z[:, cs] + ls + _split_sum(ls, after) + r
                a = jnp.where(mask[:, cs], jnp.exp(log_a), 0.0)
                acc = acc + _dot(a, v_ref[cs, :])
                r = r + jnp.sum(ls, axis=-1, keepdims=True)
            acc_ref[...] = acc
            r_ref[...] = r

        @pl.when(kk == qi)
        def _():
            o_ref[...] = acc_ref[...].astype(BF16)
            lt_ref[0] = r_ref[...]

    qspec = pl.BlockSpec((blk, D_HEAD), lambda h, i, j: (i, h))
    kspec = pl.BlockSpec((blk, D_HEAD), lambda h, i, j: (jnp.maximum(i - j, 0), h))
    return pl.pallas_call(
        body,
        out_shape=(jax.ShapeDtypeStruct((T, BR_W), BF16), jax.ShapeDtypeStruct((N_HEAD, T, 1), F32)),
        grid=(N_HEAD, nb, nb), in_specs=[qspec, kspec, kspec],
        out_specs=(qspec, pl.BlockSpec((1, blk, 1), lambda h, i, j: (h, i, 0))),
        scratch_shapes=[pltpu.VMEM((blk, D_HEAD), F32), pltpu.VMEM((blk, 1), F32)], name="sb_fwd",
        compiler_params=_cp("parallel", "parallel", "arbitrary"))(sq, sk, sv)


def _sb_bwd(sq, sk, sv, ltot, do):
    T = sq.shape[0]
    blk = min(SB_BLK, T)
    w = min(SB_W, blk)
    nb, nsub = T // blk, blk // w
    scale = D_HEAD ** -0.5

    def body(q_ref, k_ref, v_ref, lt_ref, do_ref, dq_ref, dk_ref, dv_ref, acc_ref, p_ref, g_ref):
        qi, kj = pl.program_id(1), pl.program_id(2)

        @pl.when(jnp.logical_and(qi == 0, kj == 0))
        def _():
            dk_ref[...] = jnp.zeros_like(dk_ref)
            dv_ref[...] = jnp.zeros_like(dv_ref)

        @pl.when(kj == 0)
        def _():
            acc_ref[...] = jnp.zeros_like(acc_ref)
            p_ref[...] = jnp.zeros_like(p_ref)
            g_ref[...] = jnp.zeros_like(g_ref)

        @pl.when(kj <= qi)
        def _():
            z, l, mask = _sb_scores(q_ref, k_ref, kj == qi)
            d_a = _dot_nt(do_ref[...], v_ref[...])
            upto = (_iota((w, w), 0) <= _iota((w, w), 1)).astype(BF16)
            before = (_iota((w, w), 0) < _iota((w, w), 1)).astype(BF16)
            ltv = lt_ref[0]
            p = p_ref[...]
            hg = g_ref[...]
            acc = acc_ref[...]
            base = pl.multiple_of(kj * blk, blk)
            for sb in range(nsub):
                cs = slice(sb * w, (sb + 1) * w)
                ls, zs = l[:, cs], z[:, cs]
                log_a = zs + ls + (ltv - p - _split_sum(ls, upto))
                a = jnp.where(mask[:, cs], jnp.exp(log_a), 0.0)
                g = a * d_a[:, cs]
                hm = hg + _split_sum(g, before)
                sig = jnp.exp(zs + ls)
                dz = jnp.where(mask[:, cs], (g * (1.0 - sig) - sig * hm) * scale, 0.0).astype(BF16)
                rows = pl.ds(base + sb * w, w)
                dv_ref[rows, :] += _dot_tn(a, do_ref[...])
                dk_ref[rows, :] += _dot_tn(dz, q_ref[...])
                acc = acc + _dot(dz, k_ref[cs, :])
                p = p + jnp.sum(ls, axis=-1, keepdims=True)
                hg = hg + jnp.sum(g, axis=-1, keepdims=True)
            acc_ref[...] = acc
            p_ref[...] = p
            g_ref[...] = hg

        @pl.when(kj == qi)
        def _():
            dq_ref[...] = acc_ref[...]

    qspec = pl.BlockSpec((blk, D_HEAD), lambda h, i, j: (i, h))
    kspec = pl.BlockSpec((blk, D_HEAD), lambda h, i, j: (jnp.minimum(j, i), h))
    full = pl.BlockSpec((T, D_HEAD), lambda h, i, j: (0, h))
    return pl.pallas_call(
        body, out_shape=(jax.ShapeDtypeStruct((T, BR_W), F32),) * 3, grid=(N_HEAD, nb, nb),
        in_specs=[qspec, kspec, kspec, pl.BlockSpec((1, blk, 1), lambda h, i, j: (h, i, 0)), qspec],
        out_specs=(qspec, full, full),
        scratch_shapes=[pltpu.VMEM((blk, D_HEAD), F32), pltpu.VMEM((blk, 1), F32), pltpu.VMEM((blk, 1), F32)],
        name="sb_bwd", compiler_params=_cp("arbitrary", "arbitrary", "arbitrary"))(sq, sk, sv, ltot, do)


def _mem_kv(mem, gm, w_kv, gk):
    def body(mem_ref, gm_ref, w_ref, gk_ref, mn_ref, kv_ref, kh_ref, vm_ref):
        mv = mem_ref[...]
        r = lax.rsqrt(jnp.mean(mv * mv, axis=-1, keepdims=True) + EPS)
        mn = (mv * r * gm_ref[...]).astype(BF16)
        mn_ref[...] = mn
        kv = lax.dot_general(mn, w_ref[...], _NN, preferred_element_type=F32)
        kv_ref[...] = kv
        for h in range(N_HEAD):
            kh = kv[:, _hs(h)]
            rk = lax.rsqrt(jnp.mean(kh * kh, axis=-1, keepdims=True) + EPS)
            kh_ref[:, _hs(h)] = (kh * rk * gk_ref[...]).astype(BF16)
        vm_ref[...] = kv[:, BR_W:].astype(BF16)

    return pl.pallas_call(
        body,
        out_shape=(jax.ShapeDtypeStruct((N_MEM, D_MODEL), BF16), jax.ShapeDtypeStruct((N_MEM, 2 * BR_W), F32),
                   jax.ShapeDtypeStruct((N_MEM, BR_W), BF16), jax.ShapeDtypeStruct((N_MEM, BR_W), BF16)),
        name="mem_kv", compiler_params=_cp())(mem, gm, w_kv, gk)


def _mem_q(x_ref, gq_ref, h):
    xh = x_ref[:, _hs(h)].astype(F32)
    r = lax.rsqrt(jnp.mean(xh * xh, axis=-1, keepdims=True) + EPS)
    return r, xh * r


def _mem_probs(qn, kh):
    s = _dot_nt(qn, kh) * (D_HEAD ** -0.5)
    e = jnp.exp(s - jnp.max(s, axis=-1, keepdims=True))
    return e / jnp.sum(e, axis=-1, keepdims=True)


def _mem_fwd(proj, kh, vm, gq):
    T = proj.shape[0]
    tm = min(TM, T)

    def body(x_ref, kh_ref, vm_ref, gq_ref, o_ref):
        for h in range(N_HEAD):
            _, y = _mem_q(x_ref, gq_ref, h)
            p = _mem_probs((y * gq_ref[...]).astype(BF16), kh_ref[:, _hs(h)])
            o_ref[:, _hs(h)] = _dot(p, vm_ref[:, _hs(h)]).astype(BF16)

    return pl.pallas_call(
        body, out_shape=jax.ShapeDtypeStruct((T, BR_W), BF16), grid=(T // tm,),
        in_specs=[_rowspec(tm, BR_W, MEMQ_OFF // BR_W), _full((N_MEM, BR_W)), _full((N_MEM, BR_W)),
                  _full((1, D_HEAD))],
        out_specs=_rowspec(tm, BR_W), name="mem_fwd", compiler_params=_cp("parallel"))(proj, kh, vm, gq)


def _mem_bwd(proj, kh, vm, gq, do):
    T = proj.shape[0]
    tm = min(TM, T)

    def body(x_ref, kh_ref, vm_ref, gq_ref, do_ref, dx_ref, dkh_ref, dvm_ref, dgq_ref):
        i = pl.program_id(0)

        @pl.when(i == 0)
        def _():
            dkh_ref[...] = jnp.zeros_like(dkh_ref)
            dvm_ref[...] = jnp.zeros_like(dvm_ref)
            dgq_ref[...] = jnp.zeros_like(dgq_ref)

        dg = jnp.zeros((1, D_HEAD), F32)
        for h in range(N_HEAD):
            r, y = _mem_q(x_ref, gq_ref, h)
            qn = (y * gq_ref[...]).astype(BF16)
            p = _mem_probs(qn, kh_ref[:, _hs(h)])
            dov = do_ref[:, _hs(h)]
            dp = _dot_nt(dov, vm_ref[:, _hs(h)])
            ds = p * (dp - jnp.sum(dp * p, axis=-1, keepdims=True)) * (D_HEAD ** -0.5)
            dqn = _dot(ds, kh_ref[:, _hs(h)])
            dkh_ref[:, _hs(h)] += _dot_tn(ds, qn)
            dvm_ref[:, _hs(h)] += _dot_tn(p, dov)
            dg = dg + jnp.sum(dqn * y, axis=0, keepdims=True)
            dy = dqn * gq_ref[...]
            dx_ref[:, _hs(h)] = (r * (dy - y * jnp.mean(dy * y, axis=-1, keepdims=True))).astype(BF16)
        dgq_ref[...] += dg

    return pl.pallas_call(
        body,
        out_shape=(jax.ShapeDtypeStruct((T, BR_W), BF16), jax.ShapeDtypeStruct((N_MEM, BR_W), F32),
                   jax.ShapeDtypeStruct((N_MEM, BR_W), F32), jax.ShapeDtypeStruct((1, D_HEAD), F32)),
        grid=(T // tm,),
        in_specs=[_rowspec(tm, BR_W, MEMQ_OFF // BR_W), _full((N_MEM, BR_W)), _full((N_MEM, BR_W)),
                  _full((1, D_HEAD)), _rowspec(tm, BR_W)],
        out_specs=(_rowspec(tm, BR_W), _full((N_MEM, BR_W)), _full((N_MEM, BR_W)), _full((1, D_HEAD))),
        name="mem_bwd", compiler_params=_cp("arbitrary"))(proj, kh, vm, gq, do)


def _mem_kv_bwd(mem, gm, w_kv, gk, kv, mn, dkh, dvm):
    def body(mem_ref, gm_ref, w_ref, gk_ref, kv_ref, mn_ref, dkh_ref, dvm_ref, dw_ref, dgm_ref, dgk_ref, dkv_ref):
        dgk = jnp.zeros((1, D_HEAD), F32)
        for h in range(N_HEAD):
            kh = kv_ref[:, _hs(h)]
            r = lax.rsqrt(jnp.mean(kh * kh, axis=-1, keepdims=True) + EPS)
            y = kh * r
            dn = dkh_ref[:, _hs(h)]
            dgk = dgk + jnp.sum(dn * y, axis=0, keepdims=True)
            dy = dn * gk_ref[...]
            dkv_ref[:, _hs(h)] = (r * (dy - y * jnp.mean(dy * y, axis=-1, keepdims=True))).astype(BF16)
        dkv_ref[:, BR_W:] = dvm_ref[...].astype(BF16)
        dgk_ref[...] = dgk
        dkv = dkv_ref[...]
        dw_ref[...] = lax.dot_general(mn_ref[...], dkv, _TN, preferred_element_type=F32)
        dmn = lax.dot_general(dkv, w_ref[...], _NT, preferred_element_type=F32)
        mv = mem_ref[...]
        memn = mv * lax.rsqrt(jnp.mean(mv * mv, axis=-1, keepdims=True) + EPS)
        dgm_ref[...] = jnp.sum(dmn * memn, axis=0, keepdims=True)

    return pl.pallas_call(
        body,
        out_shape=(jax.ShapeDtypeStruct((D_MODEL, 2 * BR_W), F32), jax.ShapeDtypeStruct((1, D_MODEL), F32),
                   jax.ShapeDtypeStruct((1, D_HEAD), F32)),
        scratch_shapes=[pltpu.VMEM((N_MEM, 2 * BR_W), BF16)], name="mem_kv_bwd",
        compiler_params=_cp())(mem, gm, w_kv, gk, kv, mn, dkh, dvm)


def _merge_fwd(og, osb, om, proj, wg, ws, wm):
    T = og.shape[0]
    tm = min(TM, T)

    def body(og_ref, os_ref, om_ref, gl_ref, wg_ref, ws_ref, wm_ref, mix_ref, yg_ref, ys_ref, ym_ref):
        mix = jnp.zeros((tm, D_MODEL), F32)
        for b, (o_ref, w_ref, y_ref) in enumerate(((og_ref, wg_ref, yg_ref), (os_ref, ws_ref, ys_ref),
                                                   (om_ref, wm_ref, ym_ref))):
            y = lax.dot_general(o_ref[...], w_ref[...], _NN, preferred_element_type=F32)
            y_ref[...] = y.astype(BF16)
            gate = _sigmoid(gl_ref[:, b * D_MODEL:(b + 1) * D_MODEL].astype(F32))
            mix = mix + gate * y
        mix_ref[...] = mix.astype(BF16)

    br = _rowspec(tm, BR_W)
    wspec = _full((BR_W, D_MODEL))
    out = _rowspec(tm, D_MODEL)
    return pl.pallas_call(
        body, out_shape=(jax.ShapeDtypeStruct((T, D_MODEL), BF16),) * 4, grid=(T // tm,),
        in_specs=[br, br, br, pl.BlockSpec((tm, 3 * D_MODEL), lambda i: (i, 0)), wspec, wspec, wspec],
        out_specs=(out,) * 4, name="merge_fwd", compiler_params=_cp("parallel"))(og, osb, om, proj, wg, ws, wm)


def _merge_bwd(dmix, gates, ys, os_, ws):
    T = dmix.shape[0]
    tm = min(TM, T)

    def body(dmix_ref, gl_ref, y0, y1, y2, o0, o1, o2, w0, w1, w2, dgl_ref, do0, do1, do2, dw0, dw1, dw2):
        i = pl.program_id(0)
        dm = dmix_ref[...].astype(F32)
        for b, (y_ref, o_ref, w_ref, do_ref, dw_ref) in enumerate(((y0, o0, w0, do0, dw0), (y1, o1, w1, do1, dw1),
                                                                   (y2, o2, w2, do2, dw2))):
            cs = slice(b * D_MODEL, (b + 1) * D_MODEL)
            gate = _sigmoid(gl_ref[:, cs].astype(F32))
            dgl_ref[:, cs] = (dm * y_ref[...].astype(F32) * gate * (1.0 - gate)).astype(BF16)
            dy = (gate * dm).astype(BF16)
            do_ref[...] = lax.dot_general(dy, w_ref[...], _NT, preferred_element_type=F32).astype(BF16)
            _accum(dw_ref, i == 0, lax.dot_general(o_ref[...], dy, _TN, preferred_element_type=F32))

    br = _rowspec(tm, BR_W)
    wide = _rowspec(tm, D_MODEL)
    wspec = _full((BR_W, D_MODEL))
    gspec = pl.BlockSpec((tm, 3 * D_MODEL), lambda i: (i, 0))
    return pl.pallas_call(
        body,
        out_shape=(jax.ShapeDtypeStruct((T, 3 * D_MODEL), BF16),) + (jax.ShapeDtypeStruct((T, BR_W), BF16),) * 3
        + (jax.ShapeDtypeStruct((BR_W, D_MODEL), F32),) * 3,
        grid=(T // tm,),
        in_specs=[wide, gspec, wide, wide, wide, br, br, br, wspec, wspec, wspec],
        out_specs=(gspec, br, br, br, wspec, wspec, wspec), name="merge_bwd",
        compiler_params=_cp("arbitrary"))(dmix, gates, *ys, *os_, *ws)


def _loss(y, tgt):
    T, dm = y.shape
    tm = min(TM, T)

    def body(y_ref, t_ref, dy_ref, dyb_ref, sq_ref):
        err = y_ref[...] - t_ref[...]
        dy = err * (1.0 / dm)
        dy_ref[...] = dy
        dyb_ref[...] = dy.astype(BF16)
        _accum(sq_ref, pl.program_id(0) == 0, jnp.sum(err * err, axis=0, keepdims=True))

    return pl.pallas_call(
        body,
        out_shape=(jax.ShapeDtypeStruct((T, dm), F32), jax.ShapeDtypeStruct((T, dm), BF16),
                   jax.ShapeDtypeStruct((1, dm), F32)),
        grid=(T // tm,), in_specs=[_rowspec(tm, dm), _rowspec(tm, dm)],
        out_specs=(_rowspec(tm, dm), _rowspec(tm, dm), _full((1, dm))), name="loss",
        compiler_params=_cp("arbitrary"))(y, tgt)


def _local_step(x, mem, tgt, W, P):
    w_in = W["w_in"]
    w_main = jnp.concatenate([w_in[:, :SB_OFF], w_in[:, SB_OFF + 8:]], axis=1)
    w_ab = jnp.pad(w_in[:, SB_OFF:SB_OFF + 8], ((0, 0), (0, LANES - 8)))
    avec = jnp.pad(jnp.concatenate([P["a_log"], P["dt_bias"]], axis=0), ((0, 0), (0, LANES - N_HEAD)))
    wbr = (W["w_br_gdn"], W["w_br_sb"], W["w_br_mem"])

    h = _rms_fwd(x, P["norm1_g"], "rms1")
    proj = _mm(h, w_main, "nn", BF16, "in_proj")
    ab = _mm(h, w_ab, "nn", F32, "in_proj_ab")
    q, k, v, gb = _gdn_pre(proj, P["conv_w"], ab, avec)
    og, oraw, shist, tinv, vn = _gdn_fwd(q, k, v, gb, proj, P["gdn_norm_g"])
    sbx = proj[:, SB_OFF:SB_OFF + 3 * BR_W]
    sq, sk, sv = _sb_pre(sbx, P["sb_q_norm_g"], P["sb_k_norm_g"])
    osb, ltot = _sb_fwd(sq, sk, sv)
    mn, kv, kh, vm = _mem_kv(mem, P["mem_norm_g"], W["w_mem_kv"], P["mem_k_norm_g"])
    om = _mem_fwd(proj, kh, vm, P["mem_q_norm_g"])
    gates = proj[:, GATE_OFF:]
    mix, yg, ys, ym = _merge_fwd(og, osb, om, gates, *wbr)
    x1 = _mm(mix, W["w_o"], "nn", F32, "out_proj", extra=x, epi=_epi_add)
    h2 = _rms_fwd(x1, P["norm2_g"], "rms2")
    u = _mm(h2, W["w_up"], "nn", BF16, "mlp_up")
    y = _mm(u, W["w_down"], "nn", F32, "mlp_down", a_fn=_relu2, extra=x1, epi=_epi_add)
    dy, dyb, sq_err = _loss(y, tgt)

    G = {}
    du = _mm(dyb, W["w_down"], "nt", BF16, "d_mlp_act", extra=u, epi=_epi_drelu2)
    G["w_down"] = _mm(u, dyb, "tn", F32, "dw_down", a_fn=_relu2)
    G["w_up"] = _mm(h2, du, "tn", F32, "dw_up")
    dh2 = _mm(du, W["w_up"], "nt", F32, "d_h2")
    dx1, dx1b, G["norm2_g"] = _rms_bwd(dh2, x1, P["norm2_g"], dy, "rms2_bwd")
    dmix = _mm(dx1b, W["w_o"], "nt", BF16, "d_mix")
    G["w_o"] = _mm(mix, dx1b, "tn", F32, "dw_o")
    dgates, dog, dosb, dom, G["w_br_gdn"], G["w_br_sb"], G["w_br_mem"] = _merge_bwd(
        dmix, gates, (yg, ys, ym), (og, osb, om), wbr)
    dq, dk, dv, dgb, dz, G["gdn_norm_g"] = _gdn_bwd(q, k, v, gb, proj, P["gdn_norm_g"], oraw, shist, tinv, vn, dog)
    dxc, dab, G["conv_w"], dav = _gdn_pre_bwd(proj, P["conv_w"], ab, avec, dq, dk, dv, dgb)
    dqkv = _conv_bwd(dxc, P["conv_w"])
    G["a_log"], G["dt_bias"] = dav[0:1, :N_HEAD], dav[1:2, :N_HEAD]
    dsq, dsk, dsv = _sb_bwd(sq, sk, sv, ltot, dosb)
    dsb, G["sb_q_norm_g"], G["sb_k_norm_g"] = _sb_pre_bwd(sbx, P["sb_q_norm_g"], P["sb_k_norm_g"], dsq, dsk, dsv)
    dmemq, dkh, dvm, G["mem_q_norm_g"] = _mem_bwd(proj, kh, vm, P["mem_q_norm_g"], dom)
    G["w_mem_kv"], G["mem_norm_g"], G["mem_k_norm_g"] = _mem_kv_bwd(
        mem, P["mem_norm_g"], W["w_mem_kv"], P["mem_k_norm_g"], kv, mn, dkh, dvm)
    dproj = jnp.concatenate([dqkv, dz, dsb, dmemq, dgates], axis=1)
    dw_main = _mm(h, dproj, "tn", F32, "dw_in")
    dw_ab = _mm(h, dab, "tn", F32, "dw_in_ab")
    G["w_in"] = jnp.concatenate([dw_main[:, :SB_OFF], dw_ab[:, :8], dw_main[:, SB_OFF:]], axis=1)
    dh = _mm(dproj, w_main, "nt", F32, "d_h")
    dh = _mm(dab, w_ab, "nt", F32, "d_h_ab", extra=dh, epi=_epi_add)
    dx, _, G["norm1_g"] = _rms_bwd(dh, x, P["norm1_g"], dx1, "rms1_bwd")
    return sq_err, dx, G


HBM = pl.BlockSpec(memory_space=pl.ANY)


def _comm(name, ins, out_shapes, plan):
    n_in, n_out = len(ins), len(out_shapes)
    probe = plan([None] * n_in, [None] * n_out, 0, 0, 0, dry=True)
    n_copy = probe

    def body(*refs):
        in_refs, out_refs = refs[:n_in], refs[n_in:n_in + n_out]
        send_sems, recv_sems = refs[n_in + n_out:]
        x, y, c = lax.axis_index("x"), lax.axis_index("y"), lax.axis_index("c")
        copies = []
        for k, (src, dst, dev) in enumerate(plan(in_refs, out_refs, x, y, c, dry=False)):
            if dev is None:
                cp = pltpu.make_async_copy(src, dst, send_sems.at[k])
            else:
                cp = pltpu.make_async_remote_copy(src_ref=src, dst_ref=dst, send_sem=send_sems.at[k],
                                                  recv_sem=recv_sems.at[k], device_id=dev, device_id_type=MESH)
            cp.start()
            copies.append(cp)
        for cp in copies:
            cp.wait()

    return pl.pallas_call(
        body, out_shape=tuple(out_shapes), in_specs=[HBM] * n_in, out_specs=tuple([HBM] * n_out),
        scratch_shapes=[pltpu.SemaphoreType.DMA((n_copy,)), pltpu.SemaphoreType.DMA((n_copy,))], name=name)(*ins)


def _other_chips(x, y):
    return ((1 - x, y), (x, 1 - y), (1 - x, 1 - y))


def _gather_chips(parts, name):
    def plan(ins, outs, x, y, c, dry):
        if dry:
            return 4 * len(ins)
        me = 2 * x + y
        copies = []
        for src, dst in zip(ins, outs):
            copies.append((src, dst.at[me], None))
            copies += [(src, dst.at[me], (px, py, c)) for px, py in _other_chips(x, y)]
        return copies

    return _comm(name, parts, [jax.ShapeDtypeStruct((4,) + p.shape, p.dtype) for p in parts], plan)


def _swap_halves(a, name):
    def plan(ins, outs, x, y, c, dry):
        if dry:
            return 1
        return [(ins[0].at[1 - c], outs[0], (x, y, 1 - c))]

    return _comm(name, [a], [jax.ShapeDtypeStruct(a.shape[1:], a.dtype)], plan)[0]


def _scatter_chips(a, name):
    def plan(ins, outs, x, y, c, dry):
        if dry:
            return 4
        me = 2 * x + y
        copies = [(ins[0].at[me], outs[0].at[me], None)]
        copies += [(ins[0].at[2 * px + py], outs[0].at[me], (px, py, c)) for px, py in _other_chips(x, y)]
        return copies

    return _comm(name, [a], [jax.ShapeDtypeStruct(a.shape, a.dtype)], plan)[0]


def _join_halves(a, name):
    def plan(ins, outs, x, y, c, dry):
        if dry:
            return 2
        return [(ins[0], outs[0].at[c], None), (ins[0], outs[0].at[c], (x, y, 1 - c))]

    return _comm(name, [a], [jax.ShapeDtypeStruct((2,) + a.shape, a.dtype)], plan)[0]


def _gather_all(a, name):
    def plan(ins, outs, x, y, c, dry):
        if dry:
            return 8
        me = 4 * x + 2 * y + c
        copies = [(ins[0], outs[0].at[me], None)]
        for f in range(1, 8):
            peer = (1 - x if f & 4 else x, 1 - y if f & 2 else y, 1 - c if f & 1 else c)
            copies.append((ins[0], outs[0].at[me], peer))
        return copies

    return _comm(name, [a], [jax.ShapeDtypeStruct((8,) + a.shape, a.dtype)], plan)[0]


def _sum_slots(a, name, extra=None):
    n, R, _ = a.shape
    rb = min(ROW_BLK, R)

    def body(*refs):
        a_ref, o_ref = refs[0], refs[-1]
        acc = a_ref[0]
        for s in range(1, n):
            acc = acc + a_ref[s]
        if extra is not None:
            acc = acc + refs[1][...]
        o_ref[...] = acc

    ins = [a] + ([extra] if extra is not None else [])
    in_specs = [pl.BlockSpec((n, rb, LANES), lambda i: (0, i, 0))] + ([_rowspec(rb, LANES)] if extra is not None else [])
    return pl.pallas_call(
        body, out_shape=jax.ShapeDtypeStruct((R, LANES), F32), grid=(R // rb,), in_specs=in_specs,
        out_specs=_rowspec(rb, LANES), name=name, compiler_params=_cp("parallel"))(*ins)


def _adamw(w, g, m, v, name):
    R = w.shape[0]
    rb = min(ROW_BLK, R)
    c1 = 1.0 - ADAM_B1 ** ADAM_STEP
    c2 = 1.0 - ADAM_B2 ** ADAM_STEP

    def body(w_ref, g_ref, m_ref, v_ref, d_ref, nm_ref, nv_ref):
        gv = g_ref[...]
        nm = ADAM_B1 * m_ref[...] + (1.0 - ADAM_B1) * gv
        nv = ADAM_B2 * v_ref[...] + (1.0 - ADAM_B2) * (gv * gv)
        d_ref[...] = -ADAM_LR * ((nm / c1) / (jnp.sqrt(nv / c2) + ADAM_EPS) + ADAM_WD * w_ref[...])
        nm_ref[...] = nm
        nv_ref[...] = nv

    spec = _rowspec(rb, LANES)
    return pl.pallas_call(
        body, out_shape=(jax.ShapeDtypeStruct((R, LANES), F32),) * 3, grid=(R // rb,), in_specs=[spec] * 4,
        out_specs=(spec,) * 3, name=name, compiler_params=_cp("parallel"))(w, g, m, v)


def _pack_rows(parts, rows, dtype):
    flat = jnp.concatenate([p.reshape(-1).astype(dtype) for p in parts])
    return jnp.pad(flat, (0, rows * LANES - flat.shape[0])).reshape(rows, LANES)


def _unpack_rows(pack, shapes):
    flat = pack.reshape(-1)
    out, off = [], 0
    for shape in shapes:
        n = int(np.prod(shape))
        out.append(flat[off:off + n].reshape(shape))
        off += n
    return out


def _shard_of(full, shape, axis, j):
    return full[j * shape[0]:(j + 1) * shape[0], :] if axis == 0 else full[:, j * shape[1]:(j + 1) * shape[1]]


def _small_rows(n):
    return max(n // LANES, 1)


def _pack_small(vals):
    rows = []
    for name, n in SMALL:
        r = _small_rows(n)
        rows.append(jnp.pad(vals[name].reshape(-1), (0, r * LANES - n)).reshape(r, LANES))
    flat = jnp.concatenate(rows, axis=0)
    return jnp.pad(flat, ((0, SMALL_ROWS - flat.shape[0]), (0, 0)))


def _unpack_small(pack):
    out, r0 = {}, 0
    for name, n in SMALL:
        r = _small_rows(n)
        out[name] = pack[r0:r0 + r].reshape(-1)[:n]
        r0 += r
    return out


def kernel(x, mem, norm1_g, w_in, conv_w, a_log, dt_bias, gdn_norm_g, sb_q_norm_g, sb_k_norm_g, mem_norm_g, w_mem_kv, mem_q_norm_g, mem_k_norm_g, w_br_gdn, w_br_sb, w_br_mem, w_o, norm2_g, w_up, w_down, loss_target, m_norm1_g, m_w_in, m_conv_w, m_a_log, m_dt_bias, m_gdn_norm_g, m_sb_q_norm_g, m_sb_k_norm_g, m_mem_norm_g, m_w_mem_kv, m_mem_q_norm_g, m_mem_k_norm_g, m_w_br_gdn, m_w_br_sb, m_w_br_mem, m_w_o, m_norm2_g, m_w_up, m_w_down, v_norm1_g, v_w_in, v_conv_w, v_a_log, v_dt_bias, v_gdn_norm_g, v_sb_q_norm_g, v_sb_k_norm_g, v_mem_norm_g, v_w_mem_kv, v_mem_q_norm_g, v_mem_k_norm_g, v_w_br_gdn, v_w_br_sb, v_w_br_mem, v_w_o, v_norm2_g, v_w_up, v_w_down):
    wd = dict(norm1_g=norm1_g, w_in=w_in, conv_w=conv_w, a_log=a_log, dt_bias=dt_bias, gdn_norm_g=gdn_norm_g,
              sb_q_norm_g=sb_q_norm_g, sb_k_norm_g=sb_k_norm_g, mem_norm_g=mem_norm_g, w_mem_kv=w_mem_kv,
              mem_q_norm_g=mem_q_norm_g, mem_k_norm_g=mem_k_norm_g, w_br_gdn=w_br_gdn, w_br_sb=w_br_sb,
              w_br_mem=w_br_mem, w_o=w_o, norm2_g=norm2_g, w_up=w_up, w_down=w_down)
    md = dict(norm1_g=m_norm1_g, w_in=m_w_in, conv_w=m_conv_w, a_log=m_a_log, dt_bias=m_dt_bias,
              gdn_norm_g=m_gdn_norm_g, sb_q_norm_g=m_sb_q_norm_g, sb_k_norm_g=m_sb_k_norm_g,
              mem_norm_g=m_mem_norm_g, w_mem_kv=m_w_mem_kv, mem_q_norm_g=m_mem_q_norm_g,
              mem_k_norm_g=m_mem_k_norm_g, w_br_gdn=m_w_br_gdn, w_br_sb=m_w_br_sb, w_br_mem=m_w_br_mem, w_o=m_w_o,
              norm2_g=m_norm2_g, w_up=m_w_up, w_down=m_w_down)
    vd = dict(norm1_g=v_norm1_g, w_in=v_w_in, conv_w=v_conv_w, a_log=v_a_log, dt_bias=v_dt_bias,
              gdn_norm_g=v_gdn_norm_g, sb_q_norm_g=v_sb_q_norm_g, sb_k_norm_g=v_sb_k_norm_g,
              mem_norm_g=v_mem_norm_g, w_mem_kv=v_w_mem_kv, mem_q_norm_g=v_mem_q_norm_g,
              mem_k_norm_g=v_mem_k_norm_g, w_br_gdn=v_w_br_gdn, w_br_sb=v_w_br_sb, w_br_mem=v_w_br_mem, w_o=v_w_o,
              norm2_g=v_norm2_g, w_up=v_w_up, w_down=v_w_down)
    wd, md, vd = ({n: a[0] for n, a in d.items()} for d in (wd, md, vd))
    chip = 2 * lax.axis_index("x") + lax.axis_index("y")
    shard_shapes = [s for _, s, _ in BIG]
    conv_shard = wd["conv_w"].shape

    wpack = _pack_rows([wd[n] for n, _, _ in BIG], PACK_ROWS, BF16)
    cpack = _pack_rows([wd["conv_w"]], 16, F32)
    wall, call = _gather_chips([wpack, cpack], "gather_weights")
    W = {}
    for (name, shape, axis), blk in zip(BIG, _unpack_rows_stacked(wall, shard_shapes)):
        W[name] = blk.reshape(4 * shape[0], shape[1]) if axis == 0 else blk.transpose(1, 0, 2).reshape(shape[0], 4 * shape[1])
    conv_full = call.reshape(4, -1)[:, :conv_shard[0] * conv_shard[1]].reshape((4,) + conv_shard)
    P = {n: wd[n].reshape(1, -1) for n, _ in SMALL}
    P["conv_w"] = conv_full.transpose(1, 0, 2).reshape(conv_shard[0], 4 * conv_shard[1])

    sq_err, grad_x, G = _local_step(x[0], mem[0], loss_target[0], W, P)
    loss = lax.psum(0.5 / D_MODEL * jnp.sum(sq_err), ("x", "y", "c"))

    gpack = jnp.stack([_pack_rows([_shard_of(G[n], s, a, j) for n, s, a in BIG], PACK_ROWS, F32) for j in range(4)])
    gpack = gpack.reshape(4, 2, HALF_ROWS, LANES).transpose(1, 0, 2, 3)
    from_core = _swap_halves(gpack, "grad_swap_cores").reshape(4 * HALF_ROWS, LANES)
    c = lax.axis_index("c")
    mine = lax.dynamic_index_in_dim(gpack, c, axis=0, keepdims=False).reshape(1, 4 * HALF_ROWS, LANES)
    pair = _sum_slots(mine, "grad_pair_sum", extra=from_core).reshape(4, HALF_ROWS, LANES)
    by_chip = _scatter_chips(pair, "grad_to_owner")
    half = _sum_slots(by_chip, "grad_chip_sum")
    g_big = _join_halves(half, "grad_join_cores").reshape(PACK_ROWS, LANES)

    spack = jnp.concatenate([_pack_small(G), G["conv_w"].reshape(CONV_ROWS, LANES)], axis=0)
    g_small = _sum_slots(_gather_all(spack, "gather_small_grads"), "small_grad_sum")
    g_conv_full = g_small[SMALL_ROWS:].reshape(conv_shard[0], 4 * conv_shard[1])
    g_conv = lax.dynamic_slice_in_dim(g_conv_full, chip * conv_shard[1], conv_shard[1], axis=1)

    pack_big = lambda d: _pack_rows([d[n] for n, _, _ in BIG], PACK_ROWS, F32)
    d_big, m_big, v_big = _adamw(pack_big(wd), g_big, pack_big(md), pack_big(vd), "adamw_big")
    pack_sm = lambda d: jnp.concatenate([_pack_small(d), _pack_rows([d["conv_w"]], APACK_ROWS - SMALL_ROWS, F32)], axis=0)
    g_sm = jnp.concatenate([g_small[:SMALL_ROWS], _pack_rows([g_conv], APACK_ROWS - SMALL_ROWS, F32)], axis=0)
    d_sm, m_sm, v_sm = _adamw(pack_sm(wd), g_sm, pack_sm(md), pack_sm(vd), "adamw_small")

    def unpack(big, small):
        out = dict(zip([n for n, _, _ in BIG], _unpack_rows(big, shard_shapes)))
        out.update(_unpack_small(small[:SMALL_ROWS]))
        out["conv_w"] = small[SMALL_ROWS:].reshape(-1)[:conv_shard[0] * conv_shard[1]].reshape(conv_shard)
        return [out[n][None] for n in WEIGHTS]

    return (loss, grad_x[None], *unpack(g_big, g_sm), *unpack(d_big, d_sm), *unpack(m_big, m_sm),
            *unpack(v_big, v_sm))


def _unpack_rows_stacked(pack, shapes):
    flat = pack.reshape(4, -1)
    out, off = [], 0
    for shape in shapes:
        n = int(np.prod(shape))
        out.append(flat[:, off:off + n].reshape((4,) + tuple(shape)))
        off += n
    return out
```

```python
import functools

import jax
import jax.numpy as jnp
import numpy as np
from jax import lax
from jax.experimental import pallas as pl
from jax.experimental.pallas import tpu as pltpu

F32 = jnp.float32
BF16 = jnp.bfloat16
MESH = pl.DeviceIdType.MESH

D_MODEL = 1024
N_HEAD = 4
D_HEAD = 128
BR_W = N_HEAD * D_HEAD
CONV_TAPS = 4
GDN_CHUNK = 64
INV_BLOCK = 16
INV_CHUNKS = 4
N_MEM = 256
D_FF = 4 * D_MODEL
EPS = 1e-6
LANES = 128
PROJ_W = 7168
GATE_OFF = 4096
SB_OFF = 2048
MEMQ_OFF = 3584
Z_OFF = 1536

ADAM_LR, ADAM_B1, ADAM_B2, ADAM_EPS, ADAM_WD, ADAM_STEP = 0.001, 0.9, 0.999, 1e-08, 0.01, 10

TM = 512
TK_TOK = 512
G1_TM = 256
SB_BLK = 512
SB_W = 256
VMEM_LIMIT = 48 << 20

BIG = (("w_in", (1024, 1794), 1), ("w_mem_kv", (256, 1024), 0), ("w_br_gdn", (512, 256), 1),
       ("w_br_sb", (512, 256), 1), ("w_br_mem", (512, 256), 1), ("w_o", (256, 1024), 0),
       ("w_up", (1024, 1024), 1), ("w_down", (1024, 1024), 0))
COL_SHARDED = tuple(n for n, _, a in BIG if a == 1)
GATE_COL = GATE_OFF // D_MODEL
ROW_BLK = 1024
ADAM_ROWS = 128
SMALL = (("norm1_g", 1024), ("mem_norm_g", 1024), ("norm2_g", 1024), ("gdn_norm_g", 128), ("sb_q_norm_g", 128),
         ("sb_k_norm_g", 128), ("mem_q_norm_g", 128), ("mem_k_norm_g", 128), ("a_log", 4), ("dt_bias", 4))
SMALL_ROWS = 32
CONV_ROWS = 48
SPACK_ROWS = SMALL_ROWS + CONV_ROWS
APACK_ROWS = SMALL_ROWS + 16

WEIGHTS = ("norm1_g", "w_in", "conv_w", "a_log", "dt_bias", "gdn_norm_g", "sb_q_norm_g", "sb_k_norm_g",
           "mem_norm_g", "w_mem_kv", "mem_q_norm_g", "mem_k_norm_g", "w_br_gdn", "w_br_sb", "w_br_mem", "w_o",
           "norm2_g", "w_up", "w_down")


def _cp(*sem):
    return pltpu.CompilerParams(dimension_semantics=sem if sem else None, vmem_limit_bytes=VMEM_LIMIT)


_NN = (((1,), (0,)), ((), ()))
_NT = (((1,), (1,)), ((), ()))
_TN = (((0,), (0,)), ((), ()))


def _dot(a, b, dims=_NN):
    return lax.dot_general(a.astype(BF16), b.astype(BF16), dims, preferred_element_type=F32)


def _dot_nt(a, b):
    return _dot(a, b, _NT)


def _dot_tn(a, b):
    return _dot(a, b, _TN)


def _dotf(a, b, dims=_NN):
    return lax.dot_general(a, b, dims, precision=lax.Precision.HIGHEST, preferred_element_type=F32)


def _sigmoid(v):
    return 1.0 / (1.0 + jnp.exp(-v))


def _softplus(v):
    return jnp.maximum(v, 0.0) + jnp.log(1.0 + jnp.exp(-jnp.abs(v)))


def _iota(shape, dim):
    return lax.broadcasted_iota(jnp.int32, shape, dim)


def _hs(h):
    return slice(h * D_HEAD, (h + 1) * D_HEAD)


def _rowspec(tm, w, col=0):
    return pl.BlockSpec((tm, w), lambda i: (i, col))


def _full(shape):
    return pl.BlockSpec(shape, lambda *_: (0,) * len(shape))


def _accum(ref, first, val):
    @pl.when(first)
    def _():
        ref[...] = val

    @pl.when(jnp.logical_not(first))
    def _():
        ref[...] += val


def _mm(a, b, mode, out_dtype, name, *, tm=None, tn=None, tk=None, a_fn=None, extra=None, epi=None):
    if mode == "tn":
        (K, M), N = a.shape, b.shape[1]
    else:
        (M, K), N = a.shape, (b.shape[0] if mode == "nt" else b.shape[1])
    tm = min(tm or (1024 if mode == "tn" else TM), M)
    tn = min(tn or 1024, N)
    tk = min(tk or (TK_TOK if mode == "tn" else 1024), K)
    nm, nn, nk = M // tm, N // tn, K // tk
    assert nm * tm == M and nn * tn == N and nk * tk == K, (name, a.shape, b.shape)
    if mode == "tn":
        a_spec = pl.BlockSpec((tk, tm), lambda i, j, k: (k, i))
    else:
        a_spec = pl.BlockSpec((tm, tk), lambda i, j, k: (i, k))
    if mode == "nt":
        b_spec = pl.BlockSpec((tn, tk), lambda i, j, k: (j, k))
    else:
        b_spec = pl.BlockSpec((tk, tn), lambda i, j, k: (k, j))
    dims = {"nn": _NN, "nt": _NT, "tn": _TN}[mode]
    o_spec = pl.BlockSpec((tm, tn), lambda i, j, k: (i, j))
    has_extra = extra is not None

    def body(*refs):
        a_ref, b_ref = refs[0], refs[1]
        e_ref = refs[2] if has_extra else None
        o_ref = refs[2 + has_extra]
        av = a_ref[...]
        if a_fn is not None:
            av = a_fn(av)
        p = lax.dot_general(av, b_ref[...], dims, preferred_element_type=F32)

        def finish(acc):
            if epi is not None:
                acc = epi(acc, e_ref[...] if has_extra else None)
            o_ref[...] = acc.astype(out_dtype)

        if nk == 1:
            finish(p)
        else:
            acc_ref = refs[3 + has_extra]
            k = pl.program_id(2)
            _accum(acc_ref, k == 0, p)

            @pl.when(k == nk - 1)
            def _():
                finish(acc_ref[...])

    ins = [a, b] + ([extra] if has_extra else [])
    in_specs = [a_spec, b_spec] + ([o_spec] if has_extra else [])
    return pl.pallas_call(
        body, out_shape=jax.ShapeDtypeStruct((M, N), out_dtype), grid=(nm, nn, nk), in_specs=in_specs,
        out_specs=o_spec, scratch_shapes=[pltpu.VMEM((tm, tn), F32)] if nk > 1 else [], name=name,
        compiler_params=_cp("parallel", "parallel", "arbitrary"))(*ins)


def _relu2(u):
    r = jnp.maximum(u.astype(F32), 0.0)
    return (r * r).astype(BF16)


def _epi_add(acc, e):
    return acc + e.astype(F32)


def _epi_drelu2(acc, u):
    return acc * (2.0 * jnp.maximum(u.astype(F32), 0.0))


def _rms_fwd(x, g, name):
    T, dm = x.shape
    tm = min(TM, T)

    def body(x_ref, g_ref, h_ref):
        xv = x_ref[...]
        r = lax.rsqrt(jnp.mean(xv * xv, axis=-1, keepdims=True) + EPS)
        h_ref[...] = (xv * r * g_ref[...]).astype(BF16)

    return pl.pallas_call(
        body, out_shape=jax.ShapeDtypeStruct((T, dm), BF16), grid=(T // tm,),
        in_specs=[_rowspec(tm, dm), _full((1, dm))], out_specs=_rowspec(tm, dm), name=name,
        compiler_params=_cp("parallel"))(x, g)


def _rms_bwd(dh, x, g, resid, name):
    T, dm = x.shape
    tm = min(TM, T)

    def body(dh_ref, x_ref, g_ref, res_ref, dx_ref, dxb_ref, dg_ref):
        i = pl.program_id(0)
        xv = x_ref[...]
        r = lax.rsqrt(jnp.mean(xv * xv, axis=-1, keepdims=True) + EPS)
        y = xv * r
        dhv = dh_ref[...].astype(F32)
        dy = dhv * g_ref[...]
        dx = res_ref[...] + r * (dy - y * jnp.mean(dy * y, axis=-1, keepdims=True))
        dx_ref[...] = dx
        dxb_ref[...] = dx.astype(BF16)
        _accum(dg_ref, i == 0, jnp.sum(dhv * y, axis=0, keepdims=True))

    return pl.pallas_call(
        body,
        out_shape=(jax.ShapeDtypeStruct((T, dm), F32), jax.ShapeDtypeStruct((T, dm), BF16),
                   jax.ShapeDtypeStruct((1, dm), F32)),
        grid=(T // tm,),
        in_specs=[_rowspec(tm, dm), _rowspec(tm, dm), _full((1, dm)), _rowspec(tm, dm)],
        out_specs=(_rowspec(tm, dm), _rowspec(tm, dm), _full((1, dm))), name=name,
        compiler_params=_cp("arbitrary"))(dh, x, g, resid)


def _conv_tile(x_ref, halo_ref, w_ref, xpad, tm):
    i = pl.program_id(0)
    halo = halo_ref[...].astype(F32)[8:16]
    xpad[0:8, :] = jnp.where(i > 0, halo, 0.0)
    xpad[8:, :] = x_ref[...].astype(F32)
    w = w_ref[...]
    xc = w[0:1] * xpad[5:5 + tm, :]
    for j in range(1, CONV_TAPS):
        xc = xc + w[j:j + 1] * xpad[5 + j:5 + j + tm, :]
    return xc


def _gate_terms(ab_ref, av_ref):
    abv = ab_ref[...]
    av = av_ref[...]
    pre = abv + av[1:2]
    ea = jnp.exp(av[0:1])
    g = -ea * _softplus(pre)
    return abv, pre, ea, g


def _gdn_pre(proj, conv_w, ab, avec):
    T = proj.shape[0]
    tm = min(G1_TM, T)
    cw = 3 * BR_W

    def body(x_ref, halo_ref, w_ref, ab_ref, av_ref, q_ref, k_ref, v_ref, gb_ref, xpad):
        xc = _conv_tile(x_ref, halo_ref, w_ref, xpad, tm)
        y = xc * _sigmoid(xc)
        for h in range(N_HEAD):
            for off, ref, scale in ((0, q_ref, D_HEAD ** -0.5), (BR_W, k_ref, 1.0)):
                yh = y[:, off + h * D_HEAD:off + (h + 1) * D_HEAD]
                r = lax.rsqrt(jnp.sum(yh * yh, axis=-1, keepdims=True) + EPS)
                ref[:, _hs(h)] = yh * (r * scale)
        v_ref[...] = y[:, 2 * BR_W:]
        abv, _, _, g = _gate_terms(ab_ref, av_ref)
        lane = _iota((tm, LANES), 1)
        gb_ref[...] = jnp.where(lane < N_HEAD, g, jnp.where(lane < 2 * N_HEAD, _sigmoid(abv), 0.0))

    hb = tm // 16
    return pl.pallas_call(
        body,
        out_shape=(jax.ShapeDtypeStruct((T, BR_W), F32),) * 3 + (jax.ShapeDtypeStruct((T, LANES), F32),),
        grid=(T // tm,),
        in_specs=[_rowspec(tm, cw), pl.BlockSpec((16, cw), lambda i: (jnp.maximum(i * hb - 1, 0), 0)),
                  _full((CONV_TAPS, cw)), _rowspec(tm, LANES), _full((2, LANES))],
        out_specs=(_rowspec(tm, BR_W),) * 3 + (_rowspec(tm, LANES),),
        scratch_shapes=[pltpu.VMEM((tm + 8, cw), F32)], name="gdn_pre",
        compiler_params=_cp("parallel"))(proj, proj, conv_w, ab, avec)


def _gdn_pre_bwd(proj, conv_w, ab, avec, dq, dk, dv, dgb):
    T = proj.shape[0]
    tm = min(G1_TM, T)
    cw = 3 * BR_W

    def body(x_ref, halo_ref, w_ref, ab_ref, av_ref, dq_ref, dk_ref, dv_ref, dgb_ref,
             dxc_ref, dab_ref, dcw_ref, dav_ref, xpad):
        i = pl.program_id(0)

        @pl.when(i == 0)
        def _():
            dcw_ref[...] = jnp.zeros_like(dcw_ref)
            dav_ref[...] = jnp.zeros_like(dav_ref)

        xc_all = _conv_tile(x_ref, halo_ref, w_ref, xpad, tm)
        for s in range(cw // D_HEAD):
            cs = slice(s * D_HEAD, (s + 1) * D_HEAD)
            xc = xc_all[:, cs]
            sg = _sigmoid(xc)
            yh = xc * sg
            h = s % N_HEAD
            if s < 2 * N_HEAD:
                dref, scale = (dq_ref, D_HEAD ** -0.5) if s < N_HEAD else (dk_ref, 1.0)
                r = lax.rsqrt(jnp.sum(yh * yh, axis=-1, keepdims=True) + EPS)
                yn = yh * r
                dn = dref[:, _hs(h)]
                dy = (scale * r) * (dn - yn * jnp.sum(yn * dn, axis=-1, keepdims=True))
            else:
                dy = dv_ref[:, _hs(h)]
            dxc = dy * (sg * (1.0 + xc * (1.0 - sg)))
            dxc_ref[:, cs] = dxc.astype(BF16)
            for j in range(CONV_TAPS):
                dcw_ref[j:j + 1, cs] += jnp.sum(dxc * xpad[5 + j:5 + j + tm, cs], axis=0, keepdims=True)

        abv, pre, ea, g = _gate_terms(ab_ref, av_ref)
        dgbv = dgb_ref[...]
        lane = _iota((tm, LANES), 1)
        is_a = lane < N_HEAD
        da = jnp.where(is_a, dgbv * (-ea) * _sigmoid(pre), 0.0)
        bs = _sigmoid(abv)
        db = jnp.where(jnp.logical_and(lane >= N_HEAD, lane < 2 * N_HEAD), dgbv * bs * (1.0 - bs), 0.0)
        dab_ref[...] = (da + db).astype(BF16)
        dav_ref[0:1, :] += jnp.sum(jnp.where(is_a, dgbv * g, 0.0), axis=0, keepdims=True)
        dav_ref[1:2, :] += jnp.sum(da, axis=0, keepdims=True)

    hb = tm // 16
    return pl.pallas_call(
        body,
        out_shape=(jax.ShapeDtypeStruct((T, cw), BF16), jax.ShapeDtypeStruct((T, LANES), BF16),
                   jax.ShapeDtypeStruct((CONV_TAPS, cw), F32), jax.ShapeDtypeStruct((2, LANES), F32)),
        grid=(T // tm,),
        in_specs=[_rowspec(tm, cw), pl.BlockSpec((16, cw), lambda i: (jnp.maximum(i * hb - 1, 0), 0)),
                  _full((CONV_TAPS, cw)), _rowspec(tm, LANES), _full((2, LANES)),
                  _rowspec(tm, BR_W), _rowspec(tm, BR_W), _rowspec(tm, BR_W), _rowspec(tm, LANES)],
        out_specs=(_rowspec(tm, cw), _rowspec(tm, LANES), _full((CONV_TAPS, cw)), _full((2, LANES))),
        scratch_shapes=[pltpu.VMEM((tm + 8, cw), F32)], name="gdn_pre_bwd",
        compiler_params=_cp("arbitrary"))(proj, proj, conv_w, ab, avec, dq, dk, dv, dgb)


def _conv_bwd(dxc, conv_w):
    T, cw = dxc.shape
    tm = min(G1_TM, T)
    nt = T // tm
    hb = tm // 16

    def body(d_ref, halo_ref, w_ref, dx_ref, xpad):
        i = pl.program_id(0)
        xpad[0:tm, :] = d_ref[...].astype(F32)
        xpad[tm:, :] = jnp.where(i < nt - 1, halo_ref[...].astype(F32)[0:8], 0.0)
        w = w_ref[...]
        dx = w[3:4] * xpad[0:tm, :]
        for j in range(CONV_TAPS - 1):
            dx = dx + w[j:j + 1] * xpad[3 - j:3 - j + tm, :]
        dx_ref[...] = dx.astype(BF16)

    return pl.pallas_call(
        body, out_shape=jax.ShapeDtypeStruct((T, cw), BF16), grid=(nt,),
        in_specs=[_rowspec(tm, cw), pl.BlockSpec((16, cw), lambda i: (jnp.minimum((i + 1) * hb, T // 16 - 1), 0)),
                  _full((CONV_TAPS, cw))],
        out_specs=_rowspec(tm, cw), scratch_shapes=[pltpu.VMEM((tm + 8, cw), F32)], name="conv_bwd",
        compiler_params=_cp("parallel"))(dxc, dxc, conv_w)


def _chunk_consts():
    C = GDN_CHUNK
    row, col = _iota((C, C), 0), _iota((C, C), 1)
    return row, col, row >= col, row > col


def _chunk_decay(gbv, incl):
    c_all = _dotf(incl.astype(F32), gbv)
    c_t = jnp.concatenate([c_all, jnp.zeros_like(c_all)], axis=0).T[:, :GDN_CHUNK]
    return c_all, c_t


def _head_decay(c_all, c_t, gbv, incl, h):
    C = GDN_CHUNK
    c_col = c_all[:, h:h + 1]
    c_row = c_t[h:h + 1, :]
    gam = jnp.exp(jnp.where(incl, c_col - c_row, -1e30))
    c_last = c_all[C - 1:C, h:h + 1]
    return gam, jnp.exp(c_col), jnp.exp(c_last - c_col), jnp.exp(c_last), gbv[:, N_HEAD + h:N_HEAD + h + 1]


def _split_bf16(x):
    hi = x.astype(BF16)
    return hi, (x - hi.astype(F32)).astype(BF16)


def _dot3(a, b):
    ah, al = _split_bf16(a)
    bh, bl = _split_bf16(b)
    d = lambda u, v: lax.dot_general(u, v, _NN, preferred_element_type=F32)
    return d(ah, bh) + (d(ah, bl) + d(al, bh))


def _unit_lower_inverses(ms, row, col):
    bi, bj = row // INV_BLOCK, col // INV_BLOCK
    eye = (row == col).astype(F32)
    ns = [jnp.where(bi == bj, -m, 0.0) for m in ms]
    invs = [eye + n for n in ns]
    size = 2
    while size < INV_BLOCK:
        ns = [_dot3(n, n) for n in ns]
        invs = [inv + _dot3(inv, n) for inv, n in zip(invs, ns)]
        size *= 2
    width = 2
    while width * INV_BLOCK <= GDN_CHUNK:
        sel = jnp.logical_and(bi // width == bj // width, bi // (width // 2) > bj // (width // 2))
        ts = [_dot3(inv, jnp.where(sel, m, 0.0)) for inv, m in zip(invs, ms)]
        invs = [inv - _dot3(t, inv) for inv, t in zip(invs, ts)]
        width *= 2
    return invs


def _gdn_inv(k, gb):
    T = k.shape[0]
    C = GDN_CHUNK
    per = min(INV_CHUNKS, T // C)
    rows = per * C

    def body(k_ref, gb_ref, ti_ref):
        row, col, incl, strict = _chunk_consts()
        ms = []
        for ci in range(per):
            rs = slice(ci * C, (ci + 1) * C)
            gbv = gb_ref[rs, :]
            c_all, c_t = _chunk_decay(gbv, incl)
            for h in range(N_HEAD):
                gam, _, _, _, bcol = _head_decay(c_all, c_t, gbv, incl, h)
                K = k_ref[rs, _hs(h)]
                ms.append(jnp.where(strict, _dot_nt(K * bcol, K) * gam, 0.0))
        for i, inv in enumerate(_unit_lower_inverses(ms, row, col)):
            ti_ref[i // N_HEAD, i % N_HEAD] = inv

    return pl.pallas_call(
        body, out_shape=jax.ShapeDtypeStruct((T // C, N_HEAD, C, C), F32), grid=(T // rows,),
        in_specs=[_rowspec(rows, BR_W), _rowspec(rows, LANES)],
        out_specs=pl.BlockSpec((per, N_HEAD, C, C), lambda i: (i, 0, 0, 0)), name="gdn_inv",
        compiler_params=_cp("parallel"))(k, gb)


def _gdn_fwd(q, k, v, gb, proj, gnorm, tinv_all):
    T = q.shape[0]
    C = GDN_CHUNK
    nc = T // C
    zcol = Z_OFF // BR_W
    heads = range(N_HEAD)

    def body(q_ref, k_ref, v_ref, gb_ref, z_ref, gn_ref, ti_ref, og_ref, oraw_ref, sh_ref, vn_ref, s_ref):
        @pl.when(pl.program_id(0) == 0)
        def _():
            s_ref[...] = jnp.zeros_like(s_ref)

        _, _, incl, _ = _chunk_consts()
        gbv = gb_ref[...]
        c_all, c_t = _chunk_decay(gbv, incl)
        dec = [_head_decay(c_all, c_t, gbv, incl, h) for h in heads]
        gam, gcol, dcol, glast, bcol = ([d[i] for d in dec] for i in range(5))
        Q = [q_ref[:, _hs(h)] for h in heads]
        K = [k_ref[:, _hs(h)] for h in heads]
        V = [v_ref[:, _hs(h)] for h in heads]
        S = [s_ref[h] for h in heads]
        Sb = [s.astype(BF16) for s in S]
        KS = [_dot(K[h], Sb[h]) for h in heads]
        QS = [_dot(Q[h], Sb[h]) for h in heads]
        P = [_dot_nt(Q[h], K[h]) * gam[h] for h in heads]
        R = [bcol[h] * (V[h] - gcol[h] * KS[h]) for h in heads]
        vn = [_dot(ti_ref[0, h], R[h]) for h in heads]
        O = [gcol[h] * QS[h] + _dot(P[h], vn[h]) for h in heads]
        Sn = [glast[h] * S[h] + _dot_tn(K[h] * dcol[h], vn[h]) for h in heads]
        for h in heads:
            sh_ref[0, h] = S[h]
            s_ref[h] = Sn[h]
            vn_ref[:, _hs(h)] = vn[h]
            oraw_ref[:, _hs(h)] = O[h]
            rr = lax.rsqrt(jnp.mean(O[h] * O[h], axis=-1, keepdims=True) + EPS)
            zz = z_ref[:, _hs(h)].astype(F32)
            og_ref[:, _hs(h)] = (O[h] * rr * gn_ref[...] * (zz * _sigmoid(zz))).astype(BF16)

    cspec = lambda w, cb=0: pl.BlockSpec((C, w), lambda n: (n, cb))
    hist = lambda a, b: pl.BlockSpec((1, N_HEAD, a, b), lambda n: (n, 0, 0, 0))
    return pl.pallas_call(
        body,
        out_shape=(jax.ShapeDtypeStruct((T, BR_W), BF16), jax.ShapeDtypeStruct((T, BR_W), F32),
                   jax.ShapeDtypeStruct((nc, N_HEAD, D_HEAD, D_HEAD), F32), jax.ShapeDtypeStruct((T, BR_W), F32)),
        grid=(nc,),
        in_specs=[cspec(BR_W), cspec(BR_W), cspec(BR_W), cspec(LANES), cspec(BR_W, zcol), _full((1, D_HEAD)),
                  hist(C, C)],
        out_specs=(cspec(BR_W), cspec(BR_W), hist(D_HEAD, D_HEAD), cspec(BR_W)),
        scratch_shapes=[pltpu.VMEM((N_HEAD, D_HEAD, D_HEAD), F32)], name="gdn_chunk_fwd",
        compiler_params=_cp("arbitrary"))(q, k, v, gb, proj, gnorm, tinv_all)


def _gdn_bwd(q, k, v, gb, proj, gnorm, oraw, shist, tinv_all, vn_all, dog):
    T = q.shape[0]
    C = GDN_CHUNK
    nc = T // C
    zcol = Z_OFF // BR_W

    def body(q_ref, k_ref, v_ref, gb_ref, z_ref, gn_ref, oraw_ref, sh_ref, ti_ref, vn_ref, dog_ref,
             dq_ref, dk_ref, dv_ref, dgb_ref, dz_ref, dgn_ref, ds_ref):
        @pl.when(pl.program_id(0) == 0)
        def _():
            ds_ref[...] = jnp.zeros_like(ds_ref)
            dgn_ref[...] = jnp.zeros_like(dgn_ref)

        row, col, incl, strict = _chunk_consts()
        gbv = gb_ref[...]
        c_all, c_t = _chunk_decay(gbv, incl)
        lane = _iota((C, LANES), 1)
        rowl = _iota((C, LANES), 0)
        ones = jnp.ones((C, LANES), F32)
        gn = gn_ref[...]
        heads = range(N_HEAD)
        rsum = lambda a: jnp.sum(a, axis=-1, keepdims=True)
        dec = [_head_decay(c_all, c_t, gbv, incl, h) for h in heads]
        gam, gcol, dcol, glast, bcol = ([d[i] for d in dec] for i in range(5))
        Q = [q_ref[:, _hs(h)] for h in heads]
        K = [k_ref[:, _hs(h)] for h in heads]
        V = [v_ref[:, _hs(h)] for h in heads]
        dgn = jnp.zeros((1, D_HEAD), F32)
        dO = []
        for h in heads:
            O = oraw_ref[:, _hs(h)]
            zz = z_ref[:, _hs(h)].astype(F32)
            dogv = dog_ref[:, _hs(h)].astype(F32)
            rr = lax.rsqrt(jnp.mean(O * O, axis=-1, keepdims=True) + EPS)
            on = O * rr
            sg = _sigmoid(zz)
            dz_ref[:, _hs(h)] = (dogv * on * gn * (sg * (1.0 + zz * (1.0 - sg)))).astype(BF16)
            dyn = dogv * (zz * sg)
            dgn = dgn + jnp.sum(dyn * on, axis=0, keepdims=True)
            dyv = dyn * gn
            dO.append((rr * (dyv - on * jnp.mean(dyv * on, axis=-1, keepdims=True))).astype(BF16))
        S = [sh_ref[0, h] for h in heads]
        Sb = [s.astype(BF16) for s in S]
        tinv = [ti_ref[0, h].astype(BF16) for h in heads]
        vn = [vn_ref[:, _hs(h)] for h in heads]
        vnb = [a.astype(BF16) for a in vn]
        dSn = [ds_ref[h] for h in heads]
        dSb = [a.astype(BF16) for a in dSn]
        Kb = [K[h] * bcol[h] for h in heads]
        M = [jnp.where(strict, _dot_nt(Kb[h], K[h]) * gam[h], 0.0) for h in heads]
        P = [_dot_nt(Q[h], K[h]) * gam[h] for h in heads]
        KS = [_dot(K[h], Sb[h]) for h in heads]
        QS = [_dot(Q[h], Sb[h]) for h in heads]
        dvn = [_dot_tn(P[h], dO[h]) + _dot(K[h] * dcol[h], dSb[h]) for h in heads]
        dR = [_dot_tn(tinv[h], dvn[h]) for h in heads]
        dRb = [a.astype(BF16) for a in dR]
        dP = [jnp.where(incl, _dot_nt(dO[h], vnb[h]), 0.0) for h in heads]
        dM = [jnp.where(strict, -_dot_nt(dRb[h], vnb[h]), 0.0) for h in heads]
        dPG = [(dP[h] * gam[h]).astype(BF16) for h in heads]
        dMG = [(dM[h] * gam[h]).astype(BF16) for h in heads]
        E = [_dot_nt(vnb[h], dSb[h]) for h in heads]
        dKb = [_dot(dMG[h], K[h]) for h in heads]
        bg = [bcol[h] * gcol[h] for h in heads]
        dc_all = jnp.zeros((C, LANES), F32)
        db_all = jnp.zeros((C, LANES), F32)
        for h in heads:
            dq_ref[:, _hs(h)] = gcol[h] * _dot_nt(dO[h], Sb[h]) + _dot(dPG[h], K[h])
            dk_ref[:, _hs(h)] = (_dot_tn(dPG[h], Q[h]) + _dot_tn(dMG[h], Kb[h]) + bcol[h] * dKb[h]
                                 - bg[h] * _dot_nt(dRb[h], Sb[h]) + dcol[h] * E[h])
            dv_ref[:, _hs(h)] = bcol[h] * dR[h]
            ds_ref[h] = glast[h] * dSn[h] + _dot_tn(gcol[h] * Q[h], dO[h]) - _dot_tn(bg[h] * K[h], dRb[h])
            dbeta = rsum(dKb[h] * K[h]) + rsum(dR[h] * (V[h] - gcol[h] * KS[h]))
            X = dP[h] * P[h] + dM[h] * M[h]
            ddel = rsum(K[h] * E[h]) * dcol[h]
            colsum = _dotf(X, ones, _TN)[:, 0:1]
            dc = (rsum(X) - colsum + gcol[h] * rsum(dO[h].astype(F32) * QS[h]) - bg[h] * rsum(dR[h] * KS[h]) - ddel)
            last = jnp.sum(ddel, axis=0, keepdims=True) + glast[h] * jnp.sum(rsum(dSn[h] * S[h]), axis=0, keepdims=True)
            dc_all = dc_all + jnp.where(lane == h, dc + jnp.where(rowl == C - 1, last, 0.0), 0.0)
            db_all = db_all + jnp.where(lane == N_HEAD + h, dbeta, 0.0)
        dgb_ref[...] = _dotf((col >= row).astype(F32), dc_all) + db_all
        dgn_ref[...] += dgn

    cspec = lambda w, cb=0: pl.BlockSpec((C, w), lambda n: (nc - 1 - n, cb))
    hist = lambda a, b: pl.BlockSpec((1, N_HEAD, a, b), lambda n: (nc - 1 - n, 0, 0, 0))
    return pl.pallas_call(
        body,
        out_shape=(jax.ShapeDtypeStruct((T, BR_W), F32),) * 3 + (
            jax.ShapeDtypeStruct((T, LANES), F32), jax.ShapeDtypeStruct((T, BR_W), BF16),
            jax.ShapeDtypeStruct((1, D_HEAD), F32)),
        grid=(nc,),
        in_specs=[cspec(BR_W), cspec(BR_W), cspec(BR_W), cspec(LANES), cspec(BR_W, zcol), _full((1, D_HEAD)),
                  cspec(BR_W), hist(D_HEAD, D_HEAD), hist(C, C), cspec(BR_W), cspec(BR_W)],
        out_specs=(cspec(BR_W), cspec(BR_W), cspec(BR_W), cspec(LANES), cspec(BR_W), _full((1, D_HEAD))),
        scratch_shapes=[pltpu.VMEM((N_HEAD, D_HEAD, D_HEAD), F32)], name="gdn_chunk_bwd",
        compiler_params=_cp("arbitrary"))(q, k, v, gb, proj, gnorm, oraw, shist, tinv_all, vn_all, dog)


SB_COL = SB_OFF // BR_W
SB_SCALE = D_HEAD ** -0.5


def _sb_pre(proj, gq, gk):
    T = proj.shape[0]
    tm = min(TM, T)

    def body(xq_ref, xk_ref, xv_ref, gq_ref, gk_ref, q_ref, k_ref, v_ref):
        for h in range(N_HEAD):
            for x_ref, g_ref, ref, scale in ((xq_ref, gq_ref, q_ref, SB_SCALE), (xk_ref, gk_ref, k_ref, 1.0)):
                xh = x_ref[:, _hs(h)].astype(F32)
                r = lax.rsqrt(jnp.mean(xh * xh, axis=-1, keepdims=True) + EPS)
                ref[:, _hs(h)] = (xh * (r * scale) * g_ref[...]).astype(BF16)
        v_ref[...] = xv_ref[...]

    return pl.pallas_call(
        body, out_shape=(jax.ShapeDtypeStruct((T, BR_W), BF16),) * 3, grid=(T // tm,),
        in_specs=[_rowspec(tm, BR_W, SB_COL), _rowspec(tm, BR_W, SB_COL + 1), _rowspec(tm, BR_W, SB_COL + 2),
                  _full((1, D_HEAD)), _full((1, D_HEAD))],
        out_specs=(_rowspec(tm, BR_W),) * 3, name="sb_pre", compiler_params=_cp("parallel"))(proj, proj, proj, gq, gk)


def _sb_pre_bwd(proj, gq, gk, dq, dk, dv):
    T = proj.shape[0]
    tm = min(TM, T)

    def body(xq_ref, xk_ref, gq_ref, gk_ref, dq_ref, dk_ref, dv_ref, dx_ref, dgq_ref, dgk_ref):
        i = pl.program_id(0)

        @pl.when(i == 0)
        def _():
            dgq_ref[...] = jnp.zeros_like(dgq_ref)
            dgk_ref[...] = jnp.zeros_like(dgk_ref)

        for off, x_ref, g_ref, d_ref, dg_ref, scale in ((0, xq_ref, gq_ref, dq_ref, dgq_ref, SB_SCALE),
                                                        (BR_W, xk_ref, gk_ref, dk_ref, dgk_ref, 1.0)):
            dg = jnp.zeros((1, D_HEAD), F32)
            for h in range(N_HEAD):
                xh = x_ref[:, _hs(h)].astype(F32)
                r = lax.rsqrt(jnp.mean(xh * xh, axis=-1, keepdims=True) + EPS)
                y = xh * r
                dn = d_ref[:, _hs(h)] * scale
                dg = dg + jnp.sum(dn * y, axis=0, keepdims=True)
                dy = dn * g_ref[...]
                dx_ref[:, off + h * D_HEAD:off + (h + 1) * D_HEAD] = (
                    r * (dy - y * jnp.mean(dy * y, axis=-1, keepdims=True))).astype(BF16)
            dg_ref[...] += dg
        dx_ref[:, 2 * BR_W:] = dv_ref[...].astype(BF16)

    return pl.pallas_call(
        body,
        out_shape=(jax.ShapeDtypeStruct((T, 3 * BR_W), BF16), jax.ShapeDtypeStruct((1, D_HEAD), F32),
                   jax.ShapeDtypeStruct((1, D_HEAD), F32)),
        grid=(T // tm,),
        in_specs=[_rowspec(tm, BR_W, SB_COL), _rowspec(tm, BR_W, SB_COL + 1), _full((1, D_HEAD)), _full((1, D_HEAD)),
                  _rowspec(tm, BR_W), _rowspec(tm, BR_W), _rowspec(tm, BR_W)],
        out_specs=(_rowspec(tm, 3 * BR_W), _full((1, D_HEAD)), _full((1, D_HEAD))), name="sb_pre_bwd",
        compiler_params=_cp("arbitrary"))(proj, proj, gq, gk, dq, dk, dv)


def _split_sum(x, tri):
    hi = x.astype(BF16)
    lo = (x - hi.astype(F32)).astype(BF16)
    return (lax.dot_general(hi, tri, _NN, preferred_element_type=F32)
            + lax.dot_general(lo, tri, _NN, preferred_element_type=F32))


def _sb_scores(q_ref, k_ref, diag):
    blk = q_ref.shape[0]
    z = _dot_nt(q_ref[...], k_ref[...])
    zc = jnp.minimum(z, 30.0)
    sp = jnp.log(1.0 + jnp.exp(zc)) + (z - zc)
    if not diag:
        return z, sp, None
    mask = _iota((blk, blk), 1) < _iota((blk, blk), 0)
    return z, jnp.where(mask, sp, 0.0), mask


def _sb_fwd(sq, sk, sv):
    T = sq.shape[0]
    blk = min(SB_BLK, T)
    w = min(SB_W, blk)
    nb, nsub = T // blk, blk // w

    def body(q_ref, k_ref, v_ref, o_ref, lt_ref, acc_ref, r_ref):
        qi, kk = pl.program_id(1), pl.program_id(2)

        @pl.when(kk == 0)
        def _():
            acc_ref[...] = jnp.zeros_like(acc_ref)
            r_ref[...] = jnp.zeros_like(r_ref)

        def block(diag):
            z, sp, mask = _sb_scores(q_ref, k_ref, diag)
            after = (_iota((w, w), 0) > _iota((w, w), 1)).astype(BF16)
            r = r_ref[...]
            acc = acc_ref[...]
            for sb in reversed(range(nsub)):
                cs = slice(sb * w, (sb + 1) * w)
                sps = sp[:, cs]
                later = _split_sum(sps, after)
                a = jnp.exp(z[:, cs] - sps - later - r)
                if diag:
                    a = jnp.where(mask[:, cs], a, 0.0)
                acc = acc + _dot(a, v_ref[cs, :])
                r = r + (later[:, 0:1] + sps[:, 0:1])
            acc_ref[...] = acc
            r_ref[...] = r

        pl.when(kk == 0)(functools.partial(block, True))
        pl.when(jnp.logical_and(kk > 0, kk <= qi))(functools.partial(block, False))

        @pl.when(kk == qi)
        def _():
            o_ref[...] = acc_ref[...].astype(BF16)
            lt_ref[0] = r_ref[...]

    qspec = pl.BlockSpec((blk, D_HEAD), lambda h, i, j: (i, h))
    kspec = pl.BlockSpec((blk, D_HEAD), lambda h, i, j: (jnp.maximum(i - j, 0), h))
    return pl.pallas_call(
        body,
        out_shape=(jax.ShapeDtypeStruct((T, BR_W), BF16), jax.ShapeDtypeStruct((N_HEAD, T, 1), F32)),
        grid=(N_HEAD, nb, nb), in_specs=[qspec, kspec, kspec],
        out_specs=(qspec, pl.BlockSpec((1, blk, 1), lambda h, i, j: (h, i, 0))),
        scratch_shapes=[pltpu.VMEM((blk, D_HEAD), F32), pltpu.VMEM((blk, 1), F32)], name="sb_fwd",
        compiler_params=_cp("parallel", "parallel", "arbitrary"))(sq, sk, sv)


def _sb_bwd(sq, sk, sv, ltot, do):
    T = sq.shape[0]
    blk = min(SB_BLK, T)
    w = min(SB_W, blk)
    nb, nsub = T // blk, blk // w

    def body(q_ref, k_ref, v_ref, lt_ref, do_ref, dq_ref, dk_ref, dv_ref, acc_ref, p_ref, g_ref):
        qi, kj = pl.program_id(1), pl.program_id(2)

        @pl.when(jnp.logical_and(qi == 0, kj == 0))
        def _():
            dk_ref[...] = jnp.zeros_like(dk_ref)
            dv_ref[...] = jnp.zeros_like(dv_ref)

        @pl.when(kj == 0)
        def _():
            acc_ref[...] = jnp.zeros_like(acc_ref)
            p_ref[...] = lt_ref[0]
            g_ref[...] = jnp.zeros_like(g_ref)

        def block(diag):
            z, sp, mask = _sb_scores(q_ref, k_ref, diag)
            d_a = _dot_nt(do_ref[...], v_ref[...])
            upto = (_iota((w, w), 0) <= _iota((w, w), 1)).astype(BF16)
            before = (_iota((w, w), 0) < _iota((w, w), 1)).astype(BF16)
            rest = p_ref[...]
            hg = g_ref[...]
            acc = acc_ref[...]
            base = pl.multiple_of(kj * blk, blk)
            for sb in range(nsub):
                cs = slice(sb * w, (sb + 1) * w)
                sps, zs = sp[:, cs], z[:, cs]
                upto_s = _split_sum(sps, upto)
                a = jnp.exp(zs - sps - (rest - upto_s))
                if diag:
                    a = jnp.where(mask[:, cs], a, 0.0)
                g = a * d_a[:, cs]
                g_before = _split_sum(g, before)
                sig = jnp.exp(zs - sps)
                dz = g - sig * (g + (hg + g_before))
                if diag:
                    dz = jnp.where(mask[:, cs], dz, 0.0)
                dz = dz.astype(BF16)
                rows = pl.ds(base + sb * w, w)
                dv_ref[rows, :] += _dot_tn(a, do_ref[...])
                dk_ref[rows, :] += _dot_tn(dz, q_ref[...])
                acc = acc + _dot(dz, k_ref[cs, :])
                rest = rest - upto_s[:, w - 1:w]
                hg = hg + (g_before[:, w - 1:w] + g[:, w - 1:w])
            acc_ref[...] = acc
            p_ref[...] = rest
            g_ref[...] = hg

        pl.when(kj == qi)(functools.partial(block, True))
        pl.when(kj < qi)(functools.partial(block, False))

        @pl.when(kj == qi)
        def _():
            dq_ref[...] = acc_ref[...]

    qspec = pl.BlockSpec((blk, D_HEAD), lambda h, i, j: (i, h))
    kspec = pl.BlockSpec((blk, D_HEAD), lambda h, i, j: (jnp.minimum(j, i), h))
    full = pl.BlockSpec((T, D_HEAD), lambda h, i, j: (0, h))
    return pl.pallas_call(
        body, out_shape=(jax.ShapeDtypeStruct((T, BR_W), F32),) * 3, grid=(N_HEAD, nb, nb),
        in_specs=[qspec, kspec, kspec, pl.BlockSpec((1, blk, 1), lambda h, i, j: (h, i, 0)), qspec],
        out_specs=(qspec, full, full),
        scratch_shapes=[pltpu.VMEM((blk, D_HEAD), F32), pltpu.VMEM((blk, 1), F32), pltpu.VMEM((blk, 1), F32)],
        name="sb_bwd", compiler_params=_cp("arbitrary", "arbitrary", "arbitrary"))(sq, sk, sv, ltot, do)


def _mem_kv(mem, gm, w_kv, gk):
    def body(mem_ref, gm_ref, w_ref, gk_ref, mn_ref, kv_ref, kh_ref, vm_ref):
        mv = mem_ref[...]
        r = lax.rsqrt(jnp.mean(mv * mv, axis=-1, keepdims=True) + EPS)
        mn = (mv * r * gm_ref[...]).astype(BF16)
        mn_ref[...] = mn
        kv = lax.dot_general(mn, w_ref[...], _NN, preferred_element_type=F32)
        kv_ref[...] = kv
        for h in range(N_HEAD):
            kh = kv[:, _hs(h)]
            rk = lax.rsqrt(jnp.mean(kh * kh, axis=-1, keepdims=True) + EPS)
            kh_ref[:, _hs(h)] = (kh * rk * gk_ref[...]).astype(BF16)
        vm_ref[...] = kv[:, BR_W:].astype(BF16)

    return pl.pallas_call(
        body,
        out_shape=(jax.ShapeDtypeStruct((N_MEM, D_MODEL), BF16), jax.ShapeDtypeStruct((N_MEM, 2 * BR_W), F32),
                   jax.ShapeDtypeStruct((N_MEM, BR_W), BF16), jax.ShapeDtypeStruct((N_MEM, BR_W), BF16)),
        name="mem_kv", compiler_params=_cp())(mem, gm, w_kv, gk)


def _mem_q(x_ref, gq_ref, h):
    xh = x_ref[:, _hs(h)].astype(F32)
    r = lax.rsqrt(jnp.mean(xh * xh, axis=-1, keepdims=True) + EPS)
    return r, xh * r


def _mem_probs(qn, kh):
    s = _dot_nt(qn, kh) * (D_HEAD ** -0.5)
    e = jnp.exp(s - jnp.max(s, axis=-1, keepdims=True))
    return e / jnp.sum(e, axis=-1, keepdims=True)


def _mem_fwd(proj, kh, vm, gq):
    T = proj.shape[0]
    tm = min(TM, T)

    def body(x_ref, kh_ref, vm_ref, gq_ref, o_ref):
        for h in range(N_HEAD):
            _, y = _mem_q(x_ref, gq_ref, h)
            p = _mem_probs((y * gq_ref[...]).astype(BF16), kh_ref[:, _hs(h)])
            o_ref[:, _hs(h)] = _dot(p, vm_ref[:, _hs(h)]).astype(BF16)

    return pl.pallas_call(
        body, out_shape=jax.ShapeDtypeStruct((T, BR_W), BF16), grid=(T // tm,),
        in_specs=[_rowspec(tm, BR_W, MEMQ_OFF // BR_W), _full((N_MEM, BR_W)), _full((N_MEM, BR_W)),
                  _full((1, D_HEAD))],
        out_specs=_rowspec(tm, BR_W), name="mem_fwd", compiler_params=_cp("parallel"))(proj, kh, vm, gq)


def _mem_bwd(proj, kh, vm, gq, do):
    T = proj.shape[0]
    tm = min(TM, T)

    def body(x_ref, kh_ref, vm_ref, gq_ref, do_ref, dx_ref, dkh_ref, dvm_ref, dgq_ref):
        i = pl.program_id(0)

        @pl.when(i == 0)
        def _():
            dkh_ref[...] = jnp.zeros_like(dkh_ref)
            dvm_ref[...] = jnp.zeros_like(dvm_ref)
            dgq_ref[...] = jnp.zeros_like(dgq_ref)

        dg = jnp.zeros((1, D_HEAD), F32)
        for h in range(N_HEAD):
            r, y = _mem_q(x_ref, gq_ref, h)
            qn = (y * gq_ref[...]).astype(BF16)
            p = _mem_probs(qn, kh_ref[:, _hs(h)])
            dov = do_ref[:, _hs(h)]
            dp = _dot_nt(dov, vm_ref[:, _hs(h)])
            ds = p * (dp - jnp.sum(dp * p, axis=-1, keepdims=True)) * (D_HEAD ** -0.5)
            dqn = _dot(ds, kh_ref[:, _hs(h)])
            dkh_ref[:, _hs(h)] += _dot_tn(ds, qn)
            dvm_ref[:, _hs(h)] += _dot_tn(p, dov)
            dg = dg + jnp.sum(dqn * y, axis=0, keepdims=True)
            dy = dqn * gq_ref[...]
            dx_ref[:, _hs(h)] = (r * (dy - y * jnp.mean(dy * y, axis=-1, keepdims=True))).astype(BF16)
        dgq_ref[...] += dg

    return pl.pallas_call(
        body,
        out_shape=(jax.ShapeDtypeStruct((T, BR_W), BF16), jax.ShapeDtypeStruct((N_MEM, BR_W), F32),
                   jax.ShapeDtypeStruct((N_MEM, BR_W), F32), jax.ShapeDtypeStruct((1, D_HEAD), F32)),
        grid=(T // tm,),
        in_specs=[_rowspec(tm, BR_W, MEMQ_OFF // BR_W), _full((N_MEM, BR_W)), _full((N_MEM, BR_W)),
                  _full((1, D_HEAD)), _rowspec(tm, BR_W)],
        out_specs=(_rowspec(tm, BR_W), _full((N_MEM, BR_W)), _full((N_MEM, BR_W)), _full((1, D_HEAD))),
        name="mem_bwd", compiler_params=_cp("arbitrary"))(proj, kh, vm, gq, do)


def _mem_kv_bwd(mem, gm, w_kv, gk, kv, mn, dkh, dvm):
    def body(mem_ref, gm_ref, w_ref, gk_ref, kv_ref, mn_ref, dkh_ref, dvm_ref, dw_ref, dgm_ref, dgk_ref, dkv_ref):
        dgk = jnp.zeros((1, D_HEAD), F32)
        for h in range(N_HEAD):
            kh = kv_ref[:, _hs(h)]
            r = lax.rsqrt(jnp.mean(kh * kh, axis=-1, keepdims=True) + EPS)
            y = kh * r
            dn = dkh_ref[:, _hs(h)]
            dgk = dgk + jnp.sum(dn * y, axis=0, keepdims=True)
            dy = dn * gk_ref[...]
            dkv_ref[:, _hs(h)] = (r * (dy - y * jnp.mean(dy * y, axis=-1, keepdims=True))).astype(BF16)
        dkv_ref[:, BR_W:] = dvm_ref[...].astype(BF16)
        dgk_ref[...] = dgk
        dkv = dkv_ref[...]
        dw_ref[...] = lax.dot_general(mn_ref[...], dkv, _TN, preferred_element_type=F32)
        dmn = lax.dot_general(dkv, w_ref[...], _NT, preferred_element_type=F32)
        mv = mem_ref[...]
        memn = mv * lax.rsqrt(jnp.mean(mv * mv, axis=-1, keepdims=True) + EPS)
        dgm_ref[...] = jnp.sum(dmn * memn, axis=0, keepdims=True)

    return pl.pallas_call(
        body,
        out_shape=(jax.ShapeDtypeStruct((D_MODEL, 2 * BR_W), F32), jax.ShapeDtypeStruct((1, D_MODEL), F32),
                   jax.ShapeDtypeStruct((1, D_HEAD), F32)),
        scratch_shapes=[pltpu.VMEM((N_MEM, 2 * BR_W), BF16)], name="mem_kv_bwd",
        compiler_params=_cp())(mem, gm, w_kv, gk, kv, mn, dkh, dvm)


def _merge_fwd(og, osb, om, proj, wg, ws, wm):
    T = og.shape[0]
    tm = min(TM, T)

    def body(og_ref, os_ref, om_ref, g0, g1, g2, wg_ref, ws_ref, wm_ref, mix_ref, yg_ref, ys_ref, ym_ref):
        mix = jnp.zeros((tm, D_MODEL), F32)
        for o_ref, gl_ref, w_ref, y_ref in ((og_ref, g0, wg_ref, yg_ref), (os_ref, g1, ws_ref, ys_ref),
                                            (om_ref, g2, wm_ref, ym_ref)):
            y = lax.dot_general(o_ref[...], w_ref[...], _NN, preferred_element_type=F32)
            y_ref[...] = y.astype(BF16)
            mix = mix + _sigmoid(gl_ref[...].astype(F32)) * y
        mix_ref[...] = mix.astype(BF16)

    br = _rowspec(tm, BR_W)
    wspec = _full((BR_W, D_MODEL))
    out = _rowspec(tm, D_MODEL)
    gates = [_rowspec(tm, D_MODEL, GATE_COL + b) for b in range(3)]
    return pl.pallas_call(
        body, out_shape=(jax.ShapeDtypeStruct((T, D_MODEL), BF16),) * 4, grid=(T // tm,),
        in_specs=[br, br, br, *gates, wspec, wspec, wspec],
        out_specs=(out,) * 4, name="merge_fwd",
        compiler_params=_cp("parallel"))(og, osb, om, proj, proj, proj, wg, ws, wm)


def _merge_bwd(dmix, proj, ys, os_, ws):
    T = dmix.shape[0]
    tm = min(TM, T)

    def body(dmix_ref, g0, g1, g2, y0, y1, y2, o0, o1, o2, w0, w1, w2, dgl_ref, do0, do1, do2, dw0, dw1, dw2):
        i = pl.program_id(0)
        dm = dmix_ref[...].astype(F32)
        for b, (gl_ref, y_ref, o_ref, w_ref, do_ref, dw_ref) in enumerate((
                (g0, y0, o0, w0, do0, dw0), (g1, y1, o1, w1, do1, dw1), (g2, y2, o2, w2, do2, dw2))):
            gate = _sigmoid(gl_ref[...].astype(F32))
            dgl_ref[:, b * D_MODEL:(b + 1) * D_MODEL] = (dm * y_ref[...].astype(F32) * gate * (1.0 - gate)).astype(BF16)
            dy = (gate * dm).astype(BF16)
            do_ref[...] = lax.dot_general(dy, w_ref[...], _NT, preferred_element_type=F32).astype(BF16)
            _accum(dw_ref, i == 0, lax.dot_general(dy, o_ref[...], _TN, preferred_element_type=F32))

    br = _rowspec(tm, BR_W)
    wide = _rowspec(tm, D_MODEL)
    wspec = _full((BR_W, D_MODEL))
    wtspec = _full((D_MODEL, BR_W))
    gates = [_rowspec(tm, D_MODEL, GATE_COL + b) for b in range(3)]
    return pl.pallas_call(
        body,
        out_shape=(jax.ShapeDtypeStruct((T, 3 * D_MODEL), BF16),) + (jax.ShapeDtypeStruct((T, BR_W), BF16),) * 3
        + (jax.ShapeDtypeStruct((D_MODEL, BR_W), F32),) * 3,
        grid=(T // tm,),
        in_specs=[wide, *gates, wide, wide, wide, br, br, br, wspec, wspec, wspec],
        out_specs=(_rowspec(tm, 3 * D_MODEL), br, br, br, wtspec, wtspec, wtspec), name="merge_bwd",
        compiler_params=_cp("arbitrary"))(dmix, proj, proj, proj, *ys, *os_, *ws)


def _loss(y, tgt):
    T, dm = y.shape
    tm = min(TM, T)

    def body(y_ref, t_ref, dy_ref, dyb_ref, sq_ref):
        err = y_ref[...] - t_ref[...]
        dy = err * (1.0 / dm)
        dy_ref[...] = dy
        dyb_ref[...] = dy.astype(BF16)
        _accum(sq_ref, pl.program_id(0) == 0, jnp.sum(err * err, axis=0, keepdims=True))

    return pl.pallas_call(
        body,
        out_shape=(jax.ShapeDtypeStruct((T, dm), F32), jax.ShapeDtypeStruct((T, dm), BF16),
                   jax.ShapeDtypeStruct((1, dm), F32)),
        grid=(T // tm,), in_specs=[_rowspec(tm, dm), _rowspec(tm, dm)],
        out_specs=(_rowspec(tm, dm), _rowspec(tm, dm), _full((1, dm))), name="loss",
        compiler_params=_cp("arbitrary"))(y, tgt)


def _local_step(x, mem, tgt, W, P):
    w_in = W["w_in"]
    w_main = jnp.concatenate([w_in[:, :SB_OFF], w_in[:, SB_OFF + 8:]], axis=1)
    w_ab = jnp.pad(w_in[:, SB_OFF:SB_OFF + 8], ((0, 0), (0, LANES - 8)))
    avec = jnp.pad(jnp.concatenate([P["a_log"], P["dt_bias"]], axis=0), ((0, 0), (0, LANES - N_HEAD)))
    wbr = (W["w_br_gdn"], W["w_br_sb"], W["w_br_mem"])

    h = _rms_fwd(x, P["norm1_g"], "rms1")
    proj = _mm(h, w_main, "nn", BF16, "in_proj")
    ab = _mm(h, w_ab, "nn", F32, "in_proj_ab")
    q, k, v, gb = _gdn_pre(proj, P["conv_w"], ab, avec)
    tinv = _gdn_inv(k, gb)
    og, oraw, shist, vn = _gdn_fwd(q, k, v, gb, proj, P["gdn_norm_g"], tinv)
    sq, sk, sv = _sb_pre(proj, P["sb_q_norm_g"], P["sb_k_norm_g"])
    osb, ltot = _sb_fwd(sq, sk, sv)
    mn, kv, kh, vm = _mem_kv(mem, P["mem_norm_g"], W["w_mem_kv"], P["mem_k_norm_g"])
    om = _mem_fwd(proj, kh, vm, P["mem_q_norm_g"])
    mix, yg, ys, ym = _merge_fwd(og, osb, om, proj, *wbr)
    x1 = _mm(mix, W["w_o"], "nn", F32, "out_proj", extra=x, epi=_epi_add)
    h2 = _rms_fwd(x1, P["norm2_g"], "rms2")
    u = _mm(h2, W["w_up"], "nn", BF16, "mlp_up")
    y = _mm(u, W["w_down"], "nn", F32, "mlp_down", a_fn=_relu2, extra=x1, epi=_epi_add)
    dy, dyb, sq_err = _loss(y, tgt)

    G = {}
    du = _mm(dyb, W["w_down"], "nt", BF16, "d_mlp_act", extra=u, epi=_epi_drelu2)
    G["w_down"] = _mm(u, dyb, "tn", F32, "dw_down", a_fn=_relu2)
    G["w_up"] = _mm(du, h2, "tn", F32, "dw_up")
    dh2 = _mm(du, W["w_up"], "nt", F32, "d_h2")
    dx1, dx1b, G["norm2_g"] = _rms_bwd(dh2, x1, P["norm2_g"], dy, "rms2_bwd")
    dmix = _mm(dx1b, W["w_o"], "nt", BF16, "d_mix")
    G["w_o"] = _mm(mix, dx1b, "tn", F32, "dw_o")
    dgates, dog, dosb, dom, G["w_br_gdn"], G["w_br_sb"], G["w_br_mem"] = _merge_bwd(
        dmix, proj, (yg, ys, ym), (og, osb, om), wbr)
    dq, dk, dv, dgb, dz, G["gdn_norm_g"] = _gdn_bwd(q, k, v, gb, proj, P["gdn_norm_g"], oraw, shist, tinv, vn, dog)
    dxc, dab, G["conv_w"], dav = _gdn_pre_bwd(proj, P["conv_w"], ab, avec, dq, dk, dv, dgb)
    dqkv = _conv_bwd(dxc, P["conv_w"])
    G["a_log"], G["dt_bias"] = dav[0:1, :N_HEAD], dav[1:2, :N_HEAD]
    dsq, dsk, dsv = _sb_bwd(sq, sk, sv, ltot, dosb)
    dsb, G["sb_q_norm_g"], G["sb_k_norm_g"] = _sb_pre_bwd(proj, P["sb_q_norm_g"], P["sb_k_norm_g"], dsq, dsk, dsv)
    dmemq, dkh, dvm, G["mem_q_norm_g"] = _mem_bwd(proj, kh, vm, P["mem_q_norm_g"], dom)
    G["w_mem_kv"], G["mem_norm_g"], G["mem_k_norm_g"] = _mem_kv_bwd(
        mem, P["mem_norm_g"], W["w_mem_kv"], P["mem_k_norm_g"], kv, mn, dkh, dvm)
    dproj = jnp.concatenate([dqkv, dz, dsb, dmemq, dgates], axis=1)
    dw_main = _mm(dproj, h, "tn", F32, "dw_in")
    dw_ab = _mm(dab, h, "tn", F32, "dw_in_ab")
    G["w_in"] = jnp.concatenate([dw_main[:SB_OFF], dw_ab[:8], dw_main[SB_OFF:]], axis=0)
    dh = _mm(dproj, w_main, "nt", F32, "d_h")
    dh = _mm(dab, w_ab, "nt", F32, "d_h_ab", extra=dh, epi=_epi_add)
    dx, _, G["norm1_g"] = _rms_bwd(dh, x, P["norm1_g"], dx1, "rms1_bwd")
    return sq_err, dx, G


HBM = pl.BlockSpec(memory_space=pl.ANY)


def _comm(name, ins, out_shapes, plan):
    n_in, n_out = len(ins), len(out_shapes)
    probe = plan([None] * n_in, [None] * n_out, 0, 0, 0, dry=True)
    n_copy = probe

    def body(*refs):
        in_refs, out_refs = refs[:n_in], refs[n_in:n_in + n_out]
        send_sems, recv_sems = refs[n_in + n_out:]
        x, y, c = lax.axis_index("x"), lax.axis_index("y"), lax.axis_index("c")
        copies = []
        for k, (src, dst, dev) in enumerate(plan(in_refs, out_refs, x, y, c, dry=False)):
            if dev is None:
                cp = pltpu.make_async_copy(src, dst, send_sems.at[k])
            else:
                cp = pltpu.make_async_remote_copy(src_ref=src, dst_ref=dst, send_sem=send_sems.at[k],
                                                  recv_sem=recv_sems.at[k], device_id=dev, device_id_type=MESH)
            cp.start()
            copies.append(cp)
        for cp in copies:
            cp.wait()

    return pl.pallas_call(
        body, out_shape=tuple(out_shapes), in_specs=[HBM] * n_in, out_specs=tuple([HBM] * n_out),
        scratch_shapes=[pltpu.SemaphoreType.DMA((n_copy,)), pltpu.SemaphoreType.DMA((n_copy,))], name=name)(*ins)


def _other_chips(x, y):
    return ((1 - x, y), (x, 1 - y), (1 - x, 1 - y))


def _gather_weights(parts, conv):
    n = len(parts)
    n_copy = 7 * n + 4

    def body(*refs):
        ins, cin = refs[:n], refs[n]
        outs, cout = refs[n + 1:2 * n + 1], refs[2 * n + 1]
        send, recv = refs[2 * n + 2:]
        x, y, c = lax.axis_index("x"), lax.axis_index("y"), lax.axis_index("c")
        me = 2 * x + y
        chips = _other_chips(x, y)

        def remote(src, dst, k, dev):
            return pltpu.make_async_remote_copy(src_ref=src, dst_ref=dst, send_sem=send.at[k], recv_sem=recv.at[k],
                                                device_id=dev, device_id_type=MESH)

        def my_half(p):
            hr = ins[p].shape[0] // 2
            return pl.ds(pl.multiple_of(c * hr, 16), hr)

        local, sent = [], []
        for p in range(n):
            cp = pltpu.make_async_copy(ins[p], outs[p].at[me], send.at[7 * p])
            cp.start()
            local.append(cp)
            for f, (px, py) in enumerate(chips):
                cp = remote(ins[p].at[my_half(p)], outs[p].at[me, my_half(p)], 7 * p + 1 + f, (px, py, c))
                cp.start()
                sent.append(cp)
        cp = pltpu.make_async_copy(cin, cout.at[me], send.at[7 * n])
        cp.start()
        local.append(cp)
        direct = []
        for f, (px, py) in enumerate(chips):
            cp = remote(cin, cout.at[me], 7 * n + 1 + f, (px, py, c))
            cp.start()
            direct.append(cp)
        passed = []
        for p in range(n):
            for f, (px, py) in enumerate(chips):
                landed = outs[p].at[2 * px + py, my_half(p)]
                remote(landed, landed, 7 * p + 1 + f, (px, py, c)).wait_recv()
                cp = remote(landed, landed, 7 * p + 4 + f, (x, y, 1 - c))
                cp.start()
                passed.append(cp)
        for cp in sent:
            cp.wait_send()
        for cp in passed + direct + local:
            cp.wait()

    shapes = [jax.ShapeDtypeStruct((4,) + p.shape, p.dtype) for p in parts + [conv]]
    res = pl.pallas_call(
        body, out_shape=tuple(shapes), in_specs=[HBM] * (n + 1), out_specs=tuple([HBM] * (n + 1)),
        scratch_shapes=[pltpu.SemaphoreType.DMA((n_copy,)), pltpu.SemaphoreType.DMA((n_copy,))],
        name="gather_weights")(*parts, conv)
    return res[:n], res[n]


def _swap_halves(slabs):
    n = len(slabs)

    def body(*refs):
        ins, outs = refs[:n], refs[n:2 * n]
        send, recv = refs[2 * n:]
        x, y, c = lax.axis_index("x"), lax.axis_index("y"), lax.axis_index("c")
        other = (x, y, 1 - c)
        for p in range(n):
            for j in range(4):
                pltpu.make_async_remote_copy(src_ref=ins[p].at[j, 1 - c], dst_ref=outs[p].at[j], send_sem=send.at[p],
                                             recv_sem=recv.at[p], device_id=other, device_id_type=MESH).start()
        for p in range(n):
            pltpu.make_async_remote_copy(src_ref=outs[p], dst_ref=outs[p], send_sem=send.at[p], recv_sem=recv.at[p],
                                         device_id=other, device_id_type=MESH).wait()

    shapes = [jax.ShapeDtypeStruct((4,) + s.shape[2:], s.dtype) for s in slabs]
    return pl.pallas_call(
        body, out_shape=tuple(shapes), in_specs=[HBM] * n, out_specs=tuple([HBM] * n),
        scratch_shapes=[pltpu.SemaphoreType.DMA((n,)), pltpu.SemaphoreType.DMA((n,))], name="grad_swap_cores")(*slabs)


def _scatter_chips(pairs):
    def plan(ins, outs, x, y, c, dry):
        if dry:
            return 4 * len(ins)
        me = 2 * x + y
        copies = []
        for src, dst in zip(ins, outs):
            copies.append((src.at[me], dst.at[me], None))
            copies += [(src.at[2 * px + py], dst.at[me], (px, py, c)) for px, py in _other_chips(x, y)]
        return copies

    return _comm("grad_to_owner", pairs, [jax.ShapeDtypeStruct(a.shape, a.dtype) for a in pairs], plan)


def _join_halves(halves):
    def plan(ins, outs, x, y, c, dry):
        if dry:
            return 2 * len(ins)
        copies = []
        for src, dst in zip(ins, outs):
            copies += [(src, dst.at[c], None), (src, dst.at[c], (x, y, 1 - c))]
        return copies

    return _comm("grad_join_cores", halves, [jax.ShapeDtypeStruct((2,) + a.shape, a.dtype) for a in halves], plan)


def _gather_all(a, name):
    def plan(ins, outs, x, y, c, dry):
        if dry:
            return 8
        me = 4 * x + 2 * y + c
        copies = [(ins[0], outs[0].at[me], None)]
        for f in range(1, 8):
            peer = (1 - x if f & 4 else x, 1 - y if f & 2 else y, 1 - c if f & 1 else c)
            copies.append((ins[0], outs[0].at[me], peer))
        return copies

    return _comm(name, [a], [jax.ShapeDtypeStruct((8,) + a.shape, a.dtype)], plan)[0]


def _sum_slots(a, name, extra=None):
    n, R, _ = a.shape
    rb = min(ROW_BLK, R)

    def body(*refs):
        a_ref, o_ref = refs[0], refs[-1]
        acc = a_ref[0]
        for s in range(1, n):
            acc = acc + a_ref[s]
        if extra is not None:
            acc = acc + refs[1][...]
        o_ref[...] = acc

    ins = [a] + ([extra] if extra is not None else [])
    in_specs = [pl.BlockSpec((n, rb, LANES), lambda i: (0, i, 0))] + ([_rowspec(rb, LANES)] if extra is not None else [])
    return pl.pallas_call(
        body, out_shape=jax.ShapeDtypeStruct((R, LANES), F32), grid=(R // rb,), in_specs=in_specs,
        out_specs=_rowspec(rb, LANES), name=name, compiler_params=_cp("parallel"))(*ins)


def _pair_sum(slab, theirs, core, name):
    _, _, hr, C = slab.shape

    def body(c_ref, a_ref, b_ref, o_ref):
        o_ref[...] = (a_ref[...] + b_ref[...]).astype(BF16)

    return pl.pallas_call(
        body, out_shape=jax.ShapeDtypeStruct((4, hr, C), BF16),
        grid_spec=pltpu.PrefetchScalarGridSpec(
            num_scalar_prefetch=1, grid=(4,),
            in_specs=[pl.BlockSpec((None, None, hr, C), lambda j, c_ref: (j, c_ref[0], 0, 0)),
                      pl.BlockSpec((None, hr, C), lambda j, c_ref: (j, 0, 0))],
            out_specs=pl.BlockSpec((None, hr, C), lambda j, c_ref: (j, 0, 0))),
        name=name, compiler_params=_cp("parallel"))(core, slab, theirs)


def _chip_sum(a, name):
    _, hr, C = a.shape

    def body(a_ref, o_ref):
        acc = a_ref[0].astype(F32)
        for s in range(1, 4):
            acc = acc + a_ref[s].astype(F32)
        o_ref[...] = acc

    return pl.pallas_call(
        body, out_shape=jax.ShapeDtypeStruct((hr, C), F32), grid=(1,), in_specs=[_full((4, hr, C))],
        out_specs=_full((hr, C)), name=name, compiler_params=_cp("arbitrary"))(a)


def _adamw(w, g, m, v, name):
    R, C = w.shape
    rb = min(ADAM_ROWS, R)
    c1 = 1.0 - ADAM_B1 ** ADAM_STEP
    c2 = 1.0 - ADAM_B2 ** ADAM_STEP

    def body(w_ref, g_ref, m_ref, v_ref, d_ref, nm_ref, nv_ref):
        gv = g_ref[...]
        nm = ADAM_B1 * m_ref[...] + (1.0 - ADAM_B1) * gv
        nv = ADAM_B2 * v_ref[...] + (1.0 - ADAM_B2) * (gv * gv)
        d_ref[...] = -ADAM_LR * ((nm / c1) / (jnp.sqrt(nv / c2) + ADAM_EPS) + ADAM_WD * w_ref[...])
        nm_ref[...] = nm
        nv_ref[...] = nv

    spec = _rowspec(rb, C)
    return pl.pallas_call(
        body, out_shape=(jax.ShapeDtypeStruct((R, C), F32),) * 3, grid=(R // rb,), in_specs=[spec] * 4,
        out_specs=(spec,) * 3, name=name, compiler_params=_cp("parallel"))(w, g, m, v)


def _pack_rows(parts, rows, dtype):
    flat = jnp.concatenate([p.reshape(-1).astype(dtype) for p in parts])
    return jnp.pad(flat, (0, rows * LANES - flat.shape[0])).reshape(rows, LANES)


def _small_rows(n):
    return max(n // LANES, 1)


def _pack_small(vals):
    rows = []
    for name, n in SMALL:
        r = _small_rows(n)
        rows.append(jnp.pad(vals[name].reshape(-1), (0, r * LANES - n)).reshape(r, LANES))
    flat = jnp.concatenate(rows, axis=0)
    return jnp.pad(flat, ((0, SMALL_ROWS - flat.shape[0]), (0, 0)))


def _unpack_small(pack):
    out, r0 = {}, 0
    for name, n in SMALL:
        r = _small_rows(n)
        out[name] = pack[r0:r0 + r].reshape(-1)[:n]
        r0 += r
    return out


def kernel(x, mem, norm1_g, w_in, conv_w, a_log, dt_bias, gdn_norm_g, sb_q_norm_g, sb_k_norm_g, mem_norm_g, w_mem_kv, mem_q_norm_g, mem_k_norm_g, w_br_gdn, w_br_sb, w_br_mem, w_o, norm2_g, w_up, w_down, loss_target, m_norm1_g, m_w_in, m_conv_w, m_a_log, m_dt_bias, m_gdn_norm_g, m_sb_q_norm_g, m_sb_k_norm_g, m_mem_norm_g, m_w_mem_kv, m_mem_q_norm_g, m_mem_k_norm_g, m_w_br_gdn, m_w_br_sb, m_w_br_mem, m_w_o, m_norm2_g, m_w_up, m_w_down, v_norm1_g, v_w_in, v_conv_w, v_a_log, v_dt_bias, v_gdn_norm_g, v_sb_q_norm_g, v_sb_k_norm_g, v_mem_norm_g, v_w_mem_kv, v_mem_q_norm_g, v_mem_k_norm_g, v_w_br_gdn, v_w_br_sb, v_w_br_mem, v_w_o, v_norm2_g, v_w_up, v_w_down):
    wd = dict(norm1_g=norm1_g, w_in=w_in, conv_w=conv_w, a_log=a_log, dt_bias=dt_bias, gdn_norm_g=gdn_norm_g,
              sb_q_norm_g=sb_q_norm_g, sb_k_norm_g=sb_k_norm_g, mem_norm_g=mem_norm_g, w_mem_kv=w_mem_kv,
              mem_q_norm_g=mem_q_norm_g, mem_k_norm_g=mem_k_norm_g, w_br_gdn=w_br_gdn, w_br_sb=w_br_sb,
              w_br_mem=w_br_mem, w_o=w_o, norm2_g=norm2_g, w_up=w_up, w_down=w_down)
    md = dict(norm1_g=m_norm1_g, w_in=m_w_in, conv_w=m_conv_w, a_log=m_a_log, dt_bias=m_dt_bias,
              gdn_norm_g=m_gdn_norm_g, sb_q_norm_g=m_sb_q_norm_g, sb_k_norm_g=m_sb_k_norm_g,
              mem_norm_g=m_mem_norm_g, w_mem_kv=m_w_mem_kv, mem_q_norm_g=m_mem_q_norm_g,
              mem_k_norm_g=m_mem_k_norm_g, w_br_gdn=m_w_br_gdn, w_br_sb=m_w_br_sb, w_br_mem=m_w_br_mem, w_o=m_w_o,
              norm2_g=m_norm2_g, w_up=m_w_up, w_down=m_w_down)
    vd = dict(norm1_g=v_norm1_g, w_in=v_w_in, conv_w=v_conv_w, a_log=v_a_log, dt_bias=v_dt_bias,
              gdn_norm_g=v_gdn_norm_g, sb_q_norm_g=v_sb_q_norm_g, sb_k_norm_g=v_sb_k_norm_g,
              mem_norm_g=v_mem_norm_g, w_mem_kv=v_w_mem_kv, mem_q_norm_g=v_mem_q_norm_g,
              mem_k_norm_g=v_mem_k_norm_g, w_br_gdn=v_w_br_gdn, w_br_sb=v_w_br_sb, w_br_mem=v_w_br_mem, w_o=v_w_o,
              norm2_g=v_norm2_g, w_up=v_w_up, w_down=v_w_down)
    wd, md, vd = ({n: a[0] for n, a in d.items()} for d in (wd, md, vd))
    chip = 2 * lax.axis_index("x") + lax.axis_index("y")
    core = lax.axis_index("c").astype(jnp.int32).reshape(1)
    conv_shard = wd["conv_w"].shape

    gathered, conv_all = _gather_weights([wd[n].astype(BF16) for n, _, _ in BIG], wd["conv_w"])
    W = {}
    for (name, shape, axis), blk in zip(BIG, gathered):
        W[name] = blk.reshape(4 * shape[0], shape[1]) if axis == 0 else blk.transpose(1, 0, 2).reshape(shape[0], 4 * shape[1])
    P = {n: wd[n].reshape(1, -1) for n, _ in SMALL}
    P["conv_w"] = conv_all.transpose(1, 0, 2).reshape(conv_shard[0], 4 * conv_shard[1])

    sq_err, grad_x, G = _local_step(x[0], mem[0], loss_target[0], W, P)
    loss = lax.psum(0.5 / D_MODEL * jnp.sum(sq_err), ("x", "y", "c"))

    slabs = []
    for name, (r, cc), axis in BIG:
        g = G[name]
        if axis == 0:
            slabs.append(g.reshape(4, 2, r // 2, cc))
        else:
            rows = _slab_rows(cc)
            g = jnp.pad(g.reshape(4, cc, r), ((0, 0), (0, rows - cc), (0, 0)))
            slabs.append(g.reshape(4, 2, rows // 2, r))
    theirs = _swap_halves(slabs)
    pairs = [_pair_sum(s, t, core, "pair_sum_" + n) for s, t, (n, _, _) in zip(slabs, theirs, BIG)]
    by_chip = _scatter_chips(pairs)
    halves = [_chip_sum(a, "chip_sum_" + n) for a, (n, _, _) in zip(by_chip, BIG)]
    g_big = {}
    for (name, (r, cc), axis), both in zip(BIG, _join_halves(halves)):
        full = both.reshape(-1, both.shape[-1])
        g_big[name] = full if axis == 0 else full[:cc].T

    spack = jnp.concatenate([_pack_small(G), G["conv_w"].reshape(CONV_ROWS, LANES)], axis=0)
    g_small = _sum_slots(_gather_all(spack, "gather_small_grads"), "small_grad_sum")
    g_conv_full = g_small[SMALL_ROWS:].reshape(conv_shard[0], 4 * conv_shard[1])
    g_conv = lax.dynamic_slice_in_dim(g_conv_full, chip * conv_shard[1], conv_shard[1], axis=1)

    grads, deltas, new_m, new_v = dict(g_big), {}, {}, {}
    for name, _, _ in BIG:
        deltas[name], new_m[name], new_v[name] = _adamw(wd[name], g_big[name], md[name], vd[name], "adamw_" + name)
    pack_sm = lambda d: jnp.concatenate([_pack_small(d), _pack_rows([d["conv_w"]], APACK_ROWS - SMALL_ROWS, F32)], axis=0)
    g_sm = jnp.concatenate([g_small[:SMALL_ROWS], _pack_rows([g_conv], APACK_ROWS - SMALL_ROWS, F32)], axis=0)
    small = (g_sm,) + _adamw(pack_sm(wd), g_sm, pack_sm(md), pack_sm(vd), "adamw_small")
    for out, pack in zip((grads, deltas, new_m, new_v), small):
        out.update(_unpack_small(pack[:SMALL_ROWS]))
        out["conv_w"] = pack[SMALL_ROWS:].reshape(-1)[:conv_shard[0] * conv_shard[1]].reshape(conv_shard)

    return (loss, grad_x[None], *[d[n][None] for d in (grads, deltas, new_m, new_v) for n in WEIGHTS])


def _slab_rows(n):
    return -(-n // 32) * 32
```

```python
import functools

import jax
import jax.numpy as jnp
import numpy as np
from jax import lax
from jax.experimental import pallas as pl
from jax.experimental.pallas import tpu as pltpu

F32 = jnp.float32
BF16 = jnp.bfloat16
MESH = pl.DeviceIdType.MESH

D_MODEL = 1024
N_HEAD = 4
D_HEAD = 128
BR_W = N_HEAD * D_HEAD
CONV_TAPS = 4
GDN_CHUNK = 64
INV_BLOCK = 16
INV_CHUNKS = 4
N_MEM = 256
D_FF = 4 * D_MODEL
EPS = 1e-6
LANES = 128
PROJ_W = 7168
GATE_OFF = 4096
SB_OFF = 2048
MEMQ_OFF = 3584
Z_OFF = 1536

ADAM_LR, ADAM_B1, ADAM_B2, ADAM_EPS, ADAM_WD, ADAM_STEP = 0.001, 0.9, 0.999, 1e-08, 0.01, 10

TM = 512
TK_TOK = 512
G1_TM = 256
SB_BLK = 512
SB_W = 256
VMEM_LIMIT = 48 << 20

BIG = (("w_in", (1024, 1794), 1), ("w_mem_kv", (256, 1024), 0), ("w_br_gdn", (512, 256), 1),
       ("w_br_sb", (512, 256), 1), ("w_br_mem", (512, 256), 1), ("w_o", (256, 1024), 0),
       ("w_up", (1024, 1024), 1), ("w_down", (1024, 1024), 0))
COL_SHARDED = tuple(n for n, _, a in BIG if a == 1)
GATE_COL = GATE_OFF // D_MODEL
ROW_BLK = 1024
ADAM_ROWS = 128
SMALL = (("norm1_g", 1024), ("mem_norm_g", 1024), ("norm2_g", 1024), ("gdn_norm_g", 128), ("sb_q_norm_g", 128),
         ("sb_k_norm_g", 128), ("mem_q_norm_g", 128), ("mem_k_norm_g", 128), ("a_log", 4), ("dt_bias", 4))
SMALL_ROWS = 32
CONV_ROWS = 48
SPACK_ROWS = SMALL_ROWS + CONV_ROWS
APACK_ROWS = SMALL_ROWS + 16

WEIGHTS = ("norm1_g", "w_in", "conv_w", "a_log", "dt_bias", "gdn_norm_g", "sb_q_norm_g", "sb_k_norm_g",
           "mem_norm_g", "w_mem_kv", "mem_q_norm_g", "mem_k_norm_g", "w_br_gdn", "w_br_sb", "w_br_mem", "w_o",
           "norm2_g", "w_up", "w_down")


def _cp(*sem):
    return pltpu.CompilerParams(dimension_semantics=sem if sem else None, vmem_limit_bytes=VMEM_LIMIT)


_NN = (((1,), (0,)), ((), ()))
_NT = (((1,), (1,)), ((), ()))
_TN = (((0,), (0,)), ((), ()))


def _dot(a, b, dims=_NN):
    return lax.dot_general(a.astype(BF16), b.astype(BF16), dims, preferred_element_type=F32)


def _dot_nt(a, b):
    return _dot(a, b, _NT)


def _dot_tn(a, b):
    return _dot(a, b, _TN)


def _dotf(a, b, dims=_NN):
    return lax.dot_general(a, b, dims, precision=lax.Precision.HIGHEST, preferred_element_type=F32)


def _sigmoid(v):
    return 1.0 / (1.0 + jnp.exp(-v))


def _softplus(v):
    return jnp.maximum(v, 0.0) + jnp.log(1.0 + jnp.exp(-jnp.abs(v)))


def _iota(shape, dim):
    return lax.broadcasted_iota(jnp.int32, shape, dim)


def _hs(h):
    return slice(h * D_HEAD, (h + 1) * D_HEAD)


def _rowspec(tm, w, col=0):
    return pl.BlockSpec((tm, w), lambda i: (i, col))


def _full(shape):
    return pl.BlockSpec(shape, lambda *_: (0,) * len(shape))


def _accum(ref, first, val):
    @pl.when(first)
    def _():
        ref[...] = val

    @pl.when(jnp.logical_not(first))
    def _():
        ref[...] += val


def _mm(a, b, mode, out_dtype, name, *, tm=None, tn=None, tk=None, a_fn=None, extra=None, epi=None):
    if mode == "tn":
        (K, M), N = a.shape, b.shape[1]
    else:
        (M, K), N = a.shape, (b.shape[0] if mode == "nt" else b.shape[1])
    tm = min(tm or (1024 if mode == "tn" else TM), M)
    tn = min(tn or 1024, N)
    tk = min(tk or (TK_TOK if mode == "tn" else 1024), K)
    nm, nn, nk = M // tm, N // tn, K // tk
    assert nm * tm == M and nn * tn == N and nk * tk == K, (name, a.shape, b.shape)
    if mode == "tn":
        a_spec = pl.BlockSpec((tk, tm), lambda i, j, k: (k, i))
    else:
        a_spec = pl.BlockSpec((tm, tk), lambda i, j, k: (i, k))
    if mode == "nt":
        b_spec = pl.BlockSpec((tn, tk), lambda i, j, k: (j, k))
    else:
        b_spec = pl.BlockSpec((tk, tn), lambda i, j, k: (k, j))
    dims = {"nn": _NN, "nt": _NT, "tn": _TN}[mode]
    o_spec = pl.BlockSpec((tm, tn), lambda i, j, k: (i, j))
    has_extra = extra is not None

    def body(*refs):
        a_ref, b_ref = refs[0], refs[1]
        e_ref = refs[2] if has_extra else None
        o_ref = refs[2 + has_extra]
        av = a_ref[...]
        if a_fn is not None:
            av = a_fn(av)
        p = lax.dot_general(av, b_ref[...], dims, preferred_element_type=F32)

        def finish(acc):
            if epi is not None:
                acc = epi(acc, e_ref[...] if has_extra else None)
            o_ref[...] = acc.astype(out_dtype)

        if nk == 1:
            finish(p)
        else:
            acc_ref = refs[3 + has_extra]
            k = pl.program_id(2)
            _accum(acc_ref, k == 0, p)

            @pl.when(k == nk - 1)
            def _():
                finish(acc_ref[...])

    ins = [a, b] + ([extra] if has_extra else [])
    in_specs = [a_spec, b_spec] + ([o_spec] if has_extra else [])
    return pl.pallas_call(
        body, out_shape=jax.ShapeDtypeStruct((M, N), out_dtype), grid=(nm, nn, nk), in_specs=in_specs,
        out_specs=o_spec, scratch_shapes=[pltpu.VMEM((tm, tn), F32)] if nk > 1 else [], name=name,
        compiler_params=_cp("parallel", "parallel", "arbitrary"))(*ins)


def _relu2(u):
    r = jnp.maximum(u.astype(F32), 0.0)
    return (r * r).astype(BF16)


def _epi_add(acc, e):
    return acc + e.astype(F32)


def _epi_drelu2(acc, u):
    return acc * (2.0 * jnp.maximum(u.astype(F32), 0.0))


def _rms_fwd(x, g, name):
    T, dm = x.shape
    tm = min(TM, T)

    def body(x_ref, g_ref, h_ref):
        xv = x_ref[...]
        r = lax.rsqrt(jnp.mean(xv * xv, axis=-1, keepdims=True) + EPS)
        h_ref[...] = (xv * r * g_ref[...]).astype(BF16)

    return pl.pallas_call(
        body, out_shape=jax.ShapeDtypeStruct((T, dm), BF16), grid=(T // tm,),
        in_specs=[_rowspec(tm, dm), _full((1, dm))], out_specs=_rowspec(tm, dm), name=name,
        compiler_params=_cp("parallel"))(x, g)


def _rms_bwd(dh, x, g, resid, name):
    T, dm = x.shape
    tm = min(TM, T)

    def body(dh_ref, x_ref, g_ref, res_ref, dx_ref, dxb_ref, dg_ref):
        i = pl.program_id(0)
        xv = x_ref[...]
        r = lax.rsqrt(jnp.mean(xv * xv, axis=-1, keepdims=True) + EPS)
        y = xv * r
        dhv = dh_ref[...].astype(F32)
        dy = dhv * g_ref[...]
        dx = res_ref[...] + r * (dy - y * jnp.mean(dy * y, axis=-1, keepdims=True))
        dx_ref[...] = dx
        dxb_ref[...] = dx.astype(BF16)
        _accum(dg_ref, i == 0, jnp.sum(dhv * y, axis=0, keepdims=True))

    return pl.pallas_call(
        body,
        out_shape=(jax.ShapeDtypeStruct((T, dm), F32), jax.ShapeDtypeStruct((T, dm), BF16),
                   jax.ShapeDtypeStruct((1, dm), F32)),
        grid=(T // tm,),
        in_specs=[_rowspec(tm, dm), _rowspec(tm, dm), _full((1, dm)), _rowspec(tm, dm)],
        out_specs=(_rowspec(tm, dm), _rowspec(tm, dm), _full((1, dm))), name=name,
        compiler_params=_cp("arbitrary"))(dh, x, g, resid)


def _conv_tile(x_ref, halo_ref, w_ref, xpad, tm):
    i = pl.program_id(0)
    halo = halo_ref[...].astype(F32)[8:16]
    xpad[0:8, :] = jnp.where(i > 0, halo, 0.0)
    xpad[8:, :] = x_ref[...].astype(F32)
    w = w_ref[...]
    xc = w[0:1] * xpad[5:5 + tm, :]
    for j in range(1, CONV_TAPS):
        xc = xc + w[j:j + 1] * xpad[5 + j:5 + j + tm, :]
    return xc


def _gate_terms(ab_ref, av_ref):
    abv = ab_ref[...]
    av = av_ref[...]
    pre = abv + av[1:2]
    ea = jnp.exp(av[0:1])
    g = -ea * _softplus(pre)
    return abv, pre, ea, g


def _gdn_pre(proj, conv_w, ab, avec):
    T = proj.shape[0]
    tm = min(G1_TM, T)
    cw = 3 * BR_W

    def body(x_ref, halo_ref, w_ref, ab_ref, av_ref, q_ref, k_ref, v_ref, gb_ref, xpad):
        xc = _conv_tile(x_ref, halo_ref, w_ref, xpad, tm)
        y = xc * _sigmoid(xc)
        for h in range(N_HEAD):
            for off, ref, scale in ((0, q_ref, D_HEAD ** -0.5), (BR_W, k_ref, 1.0)):
                yh = y[:, off + h * D_HEAD:off + (h + 1) * D_HEAD]
                r = lax.rsqrt(jnp.sum(yh * yh, axis=-1, keepdims=True) + EPS)
                ref[:, _hs(h)] = yh * (r * scale)
        v_ref[...] = y[:, 2 * BR_W:]
        abv, _, _, g = _gate_terms(ab_ref, av_ref)
        lane = _iota((tm, LANES), 1)
        gb_ref[...] = jnp.where(lane < N_HEAD, g, jnp.where(lane < 2 * N_HEAD, _sigmoid(abv), 0.0))

    hb = tm // 16
    return pl.pallas_call(
        body,
        out_shape=(jax.ShapeDtypeStruct((T, BR_W), F32),) * 3 + (jax.ShapeDtypeStruct((T, LANES), F32),),
        grid=(T // tm,),
        in_specs=[_rowspec(tm, cw), pl.BlockSpec((16, cw), lambda i: (jnp.maximum(i * hb - 1, 0), 0)),
                  _full((CONV_TAPS, cw)), _rowspec(tm, LANES), _full((2, LANES))],
        out_specs=(_rowspec(tm, BR_W),) * 3 + (_rowspec(tm, LANES),),
        scratch_shapes=[pltpu.VMEM((tm + 8, cw), F32)], name="gdn_pre",
        compiler_params=_cp("parallel"))(proj, proj, conv_w, ab, avec)


def _gdn_pre_bwd(proj, conv_w, ab, avec, dq, dk, dv, dgb):
    T = proj.shape[0]
    tm = min(G1_TM, T)
    cw = 3 * BR_W

    def body(x_ref, halo_ref, w_ref, ab_ref, av_ref, dq_ref, dk_ref, dv_ref, dgb_ref,
             dxc_ref, dab_ref, dcw_ref, dav_ref, xpad):
        i = pl.program_id(0)

        @pl.when(i == 0)
        def _():
            dcw_ref[...] = jnp.zeros_like(dcw_ref)
            dav_ref[...] = jnp.zeros_like(dav_ref)

        xc_all = _conv_tile(x_ref, halo_ref, w_ref, xpad, tm)
        for s in range(cw // D_HEAD):
            cs = slice(s * D_HEAD, (s + 1) * D_HEAD)
            xc = xc_all[:, cs]
            sg = _sigmoid(xc)
            yh = xc * sg
            h = s % N_HEAD
            if s < 2 * N_HEAD:
                dref, scale = (dq_ref, D_HEAD ** -0.5) if s < N_HEAD else (dk_ref, 1.0)
                r = lax.rsqrt(jnp.sum(yh * yh, axis=-1, keepdims=True) + EPS)
                yn = yh * r
                dn = dref[:, _hs(h)]
                dy = (scale * r) * (dn - yn * jnp.sum(yn * dn, axis=-1, keepdims=True))
            else:
                dy = dv_ref[:, _hs(h)]
            dxc = dy * (sg * (1.0 + xc * (1.0 - sg)))
            dxc_ref[:, cs] = dxc.astype(BF16)
            for j in range(CONV_TAPS):
                dcw_ref[j:j + 1, cs] += jnp.sum(dxc * xpad[5 + j:5 + j + tm, cs], axis=0, keepdims=True)

        abv, pre, ea, g = _gate_terms(ab_ref, av_ref)
        dgbv = dgb_ref[...]
        lane = _iota((tm, LANES), 1)
        is_a = lane < N_HEAD
        da = jnp.where(is_a, dgbv * (-ea) * _sigmoid(pre), 0.0)
        bs = _sigmoid(abv)
        db = jnp.where(jnp.logical_and(lane >= N_HEAD, lane < 2 * N_HEAD), dgbv * bs * (1.0 - bs), 0.0)
        dab_ref[...] = (da + db).astype(BF16)
        dav_ref[0:1, :] += jnp.sum(jnp.where(is_a, dgbv * g, 0.0), axis=0, keepdims=True)
        dav_ref[1:2, :] += jnp.sum(da, axis=0, keepdims=True)

    hb = tm // 16
    return pl.pallas_call(
        body,
        out_shape=(jax.ShapeDtypeStruct((T, cw), BF16), jax.ShapeDtypeStruct((T, LANES), BF16),
                   jax.ShapeDtypeStruct((CONV_TAPS, cw), F32), jax.ShapeDtypeStruct((2, LANES), F32)),
        grid=(T // tm,),
        in_specs=[_rowspec(tm, cw), pl.BlockSpec((16, cw), lambda i: (jnp.maximum(i * hb - 1, 0), 0)),
                  _full((CONV_TAPS, cw)), _rowspec(tm, LANES), _full((2, LANES)),
                  _rowspec(tm, BR_W), _rowspec(tm, BR_W), _rowspec(tm, BR_W), _rowspec(tm, LANES)],
        out_specs=(_rowspec(tm, cw), _rowspec(tm, LANES), _full((CONV_TAPS, cw)), _full((2, LANES))),
        scratch_shapes=[pltpu.VMEM((tm + 8, cw), F32)], name="gdn_pre_bwd",
        compiler_params=_cp("arbitrary"))(proj, proj, conv_w, ab, avec, dq, dk, dv, dgb)


def _conv_bwd(dxc, conv_w):
    T, cw = dxc.shape
    tm = min(G1_TM, T)
    nt = T // tm
    hb = tm // 16

    def body(d_ref, halo_ref, w_ref, dx_ref, xpad):
        i = pl.program_id(0)
        xpad[0:tm, :] = d_ref[...].astype(F32)
        xpad[tm:, :] = jnp.where(i < nt - 1, halo_ref[...].astype(F32)[0:8], 0.0)
        w = w_ref[...]
        dx = w[3:4] * xpad[0:tm, :]
        for j in range(CONV_TAPS - 1):
            dx = dx + w[j:j + 1] * xpad[3 - j:3 - j + tm, :]
        dx_ref[...] = dx.astype(BF16)

    return pl.pallas_call(
        body, out_shape=jax.ShapeDtypeStruct((T, cw), BF16), grid=(nt,),
        in_specs=[_rowspec(tm, cw), pl.BlockSpec((16, cw), lambda i: (jnp.minimum((i + 1) * hb, T // 16 - 1), 0)),
                  _full((CONV_TAPS, cw))],
        out_specs=_rowspec(tm, cw), scratch_shapes=[pltpu.VMEM((tm + 8, cw), F32)], name="conv_bwd",
        compiler_params=_cp("parallel"))(dxc, dxc, conv_w)


def _chunk_consts():
    C = GDN_CHUNK
    row, col = _iota((C, C), 0), _iota((C, C), 1)
    return row, col, row >= col, row > col


def _chunk_decay(gbv, incl):
    c_all = _dotf(incl.astype(F32), gbv)
    c_t = jnp.concatenate([c_all, jnp.zeros_like(c_all)], axis=0).T[:, :GDN_CHUNK]
    return c_all, c_t


def _head_decay(c_all, c_t, gbv, incl, h):
    C = GDN_CHUNK
    c_col = c_all[:, h:h + 1]
    c_row = c_t[h:h + 1, :]
    gam = jnp.exp(jnp.where(incl, c_col - c_row, -1e30))
    c_last = c_all[C - 1:C, h:h + 1]
    return gam, jnp.exp(c_col), jnp.exp(c_last - c_col), jnp.exp(c_last), gbv[:, N_HEAD + h:N_HEAD + h + 1]


def _split_bf16(x):
    hi = x.astype(BF16)
    return hi, (x - hi.astype(F32)).astype(BF16)


def _dot3(a, b):
    ah, al = _split_bf16(a)
    bh, bl = _split_bf16(b)
    d = lambda u, v: lax.dot_general(u, v, _NN, preferred_element_type=F32)
    return d(ah, bh) + (d(ah, bl) + d(al, bh))


def _unit_lower_inverses(ms, row, col):
    bi, bj = row // INV_BLOCK, col // INV_BLOCK
    eye = (row == col).astype(F32)
    ns = [jnp.where(bi == bj, -m, 0.0) for m in ms]
    invs = [eye + n for n in ns]
    size = 2
    while size < INV_BLOCK:
        ns = [_dot3(n, n) for n in ns]
        invs = [inv + _dot3(inv, n) for inv, n in zip(invs, ns)]
        size *= 2
    width = 2
    while width * INV_BLOCK <= GDN_CHUNK:
        sel = jnp.logical_and(bi // width == bj // width, bi // (width // 2) > bj // (width // 2))
        ts = [_dot3(inv, jnp.where(sel, m, 0.0)) for inv, m in zip(invs, ms)]
        invs = [inv - _dot3(t, inv) for inv, t in zip(invs, ts)]
        width *= 2
    return invs


def _gdn_inv(k, gb):
    T = k.shape[0]
    C = GDN_CHUNK
    per = min(INV_CHUNKS, T // C)
    rows = per * C

    def body(k_ref, gb_ref, ti_ref):
        row, col, incl, strict = _chunk_consts()
        ms = []
        for ci in range(per):
            rs = slice(ci * C, (ci + 1) * C)
            gbv = gb_ref[rs, :]
            c_all, c_t = _chunk_decay(gbv, incl)
            for h in range(N_HEAD):
                gam, _, _, _, bcol = _head_decay(c_all, c_t, gbv, incl, h)
                K = k_ref[rs, _hs(h)]
                ms.append(jnp.where(strict, _dot_nt(K * bcol, K) * gam, 0.0))
        for i, inv in enumerate(_unit_lower_inverses(ms, row, col)):
            ti_ref[i // N_HEAD, i % N_HEAD] = inv

    return pl.pallas_call(
        body, out_shape=jax.ShapeDtypeStruct((T // C, N_HEAD, C, C), F32), grid=(T // rows,),
        in_specs=[_rowspec(rows, BR_W), _rowspec(rows, LANES)],
        out_specs=pl.BlockSpec((per, N_HEAD, C, C), lambda i: (i, 0, 0, 0)), name="gdn_inv",
        compiler_params=_cp("parallel"))(k, gb)


def _gdn_fwd(q, k, v, gb, proj, gnorm, tinv_all):
    T = q.shape[0]
    C = GDN_CHUNK
    nc = T // C
    zcol = Z_OFF // BR_W
    heads = range(N_HEAD)

    def body(q_ref, k_ref, v_ref, gb_ref, z_ref, gn_ref, ti_ref, og_ref, oraw_ref, sh_ref, vn_ref, s_ref):
        @pl.when(pl.program_id(0) == 0)
        def _():
            s_ref[...] = jnp.zeros_like(s_ref)

        _, _, incl, _ = _chunk_consts()
        gbv = gb_ref[...]
        c_all, c_t = _chunk_decay(gbv, incl)
        dec = [_head_decay(c_all, c_t, gbv, incl, h) for h in heads]
        gam, gcol, dcol, glast, bcol = ([d[i] for d in dec] for i in range(5))
        Q = [q_ref[:, _hs(h)] for h in heads]
        K = [k_ref[:, _hs(h)] for h in heads]
        V = [v_ref[:, _hs(h)] for h in heads]
        S = [s_ref[h] for h in heads]
        Sb = [s.astype(BF16) for s in S]
        KS = [_dot(K[h], Sb[h]) for h in heads]
        QS = [_dot(Q[h], Sb[h]) for h in heads]
        P = [_dot_nt(Q[h], K[h]) * gam[h] for h in heads]
        R = [bcol[h] * (V[h] - gcol[h] * KS[h]) for h in heads]
        vn = [_dot(ti_ref[0, h], R[h]) for h in heads]
        O = [gcol[h] * QS[h] + _dot(P[h], vn[h]) for h in heads]
        Sn = [glast[h] * S[h] + _dot_tn(K[h] * dcol[h], vn[h]) for h in heads]
        for h in heads:
            sh_ref[0, h] = S[h]
            s_ref[h] = Sn[h]
            vn_ref[:, _hs(h)] = vn[h]
            oraw_ref[:, _hs(h)] = O[h]
            rr = lax.rsqrt(jnp.mean(O[h] * O[h], axis=-1, keepdims=True) + EPS)
            zz = z_ref[:, _hs(h)].astype(F32)
            og_ref[:, _hs(h)] = (O[h] * rr * gn_ref[...] * (zz * _sigmoid(zz))).astype(BF16)

    cspec = lambda w, cb=0: pl.BlockSpec((C, w), lambda n: (n, cb))
    hist = lambda a, b: pl.BlockSpec((1, N_HEAD, a, b), lambda n: (n, 0, 0, 0))
    return pl.pallas_call(
        body,
        out_shape=(jax.ShapeDtypeStruct((T, BR_W), BF16), jax.ShapeDtypeStruct((T, BR_W), F32),
                   jax.ShapeDtypeStruct((nc, N_HEAD, D_HEAD, D_HEAD), F32), jax.ShapeDtypeStruct((T, BR_W), F32)),
        grid=(nc,),
        in_specs=[cspec(BR_W), cspec(BR_W), cspec(BR_W), cspec(LANES), cspec(BR_W, zcol), _full((1, D_HEAD)),
                  hist(C, C)],
        out_specs=(cspec(BR_W), cspec(BR_W), hist(D_HEAD, D_HEAD), cspec(BR_W)),
        scratch_shapes=[pltpu.VMEM((N_HEAD, D_HEAD, D_HEAD), F32)], name="gdn_chunk_fwd",
        compiler_params=_cp("arbitrary"))(q, k, v, gb, proj, gnorm, tinv_all)


def _gdn_bwd(q, k, v, gb, proj, gnorm, oraw, shist, tinv_all, vn_all, dog):
    T = q.shape[0]
    C = GDN_CHUNK
    nc = T // C
    zcol = Z_OFF // BR_W

    def body(q_ref, k_ref, v_ref, gb_ref, z_ref, gn_ref, oraw_ref, sh_ref, ti_ref, vn_ref, dog_ref,
             dq_ref, dk_ref, dv_ref, dgb_ref, dz_ref, dgn_ref, ds_ref):
        @pl.when(pl.program_id(0) == 0)
        def _():
            ds_ref[...] = jnp.zeros_like(ds_ref)
            dgn_ref[...] = jnp.zeros_like(dgn_ref)

        row, col, incl, strict = _chunk_consts()
        gbv = gb_ref[...]
        c_all, c_t = _chunk_decay(gbv, incl)
        lane = _iota((C, LANES), 1)
        rowl = _iota((C, LANES), 0)
        ones = jnp.ones((C, LANES), F32)
        gn = gn_ref[...]
        heads = range(N_HEAD)
        rsum = lambda a: jnp.sum(a, axis=-1, keepdims=True)
        dec = [_head_decay(c_all, c_t, gbv, incl, h) for h in heads]
        gam, gcol, dcol, glast, bcol = ([d[i] for d in dec] for i in range(5))
        Q = [q_ref[:, _hs(h)] for h in heads]
        K = [k_ref[:, _hs(h)] for h in heads]
        V = [v_ref[:, _hs(h)] for h in heads]
        dgn = jnp.zeros((1, D_HEAD), F32)
        dO = []
        for h in heads:
            O = oraw_ref[:, _hs(h)]
            zz = z_ref[:, _hs(h)].astype(F32)
            dogv = dog_ref[:, _hs(h)].astype(F32)
            rr = lax.rsqrt(jnp.mean(O * O, axis=-1, keepdims=True) + EPS)
            on = O * rr
            sg = _sigmoid(zz)
            dz_ref[:, _hs(h)] = (dogv * on * gn * (sg * (1.0 + zz * (1.0 - sg)))).astype(BF16)
            dyn = dogv * (zz * sg)
            dgn = dgn + jnp.sum(dyn * on, axis=0, keepdims=True)
            dyv = dyn * gn
            dO.append((rr * (dyv - on * jnp.mean(dyv * on, axis=-1, keepdims=True))).astype(BF16))
        S = [sh_ref[0, h] for h in heads]
        Sb = [s.astype(BF16) for s in S]
        tinv = [ti_ref[0, h].astype(BF16) for h in heads]
        vn = [vn_ref[:, _hs(h)] for h in heads]
        vnb = [a.astype(BF16) for a in vn]
        dSn = [ds_ref[h] for h in heads]
        dSb = [a.astype(BF16) for a in dSn]
        Kb = [K[h] * bcol[h] for h in heads]
        M = [jnp.where(strict, _dot_nt(Kb[h], K[h]) * gam[h], 0.0) for h in heads]
        P = [_dot_nt(Q[h], K[h]) * gam[h] for h in heads]
        KS = [_dot(K[h], Sb[h]) for h in heads]
        QS = [_dot(Q[h], Sb[h]) for h in heads]
        dvn = [_dot_tn(P[h], dO[h]) + _dot(K[h] * dcol[h], dSb[h]) for h in heads]
        dR = [_dot_tn(tinv[h], dvn[h]) for h in heads]
        dRb = [a.astype(BF16) for a in dR]
        dP = [jnp.where(incl, _dot_nt(dO[h], vnb[h]), 0.0) for h in heads]
        dM = [jnp.where(strict, -_dot_nt(dRb[h], vnb[h]), 0.0) for h in heads]
        dPG = [(dP[h] * gam[h]).astype(BF16) for h in heads]
        dMG = [(dM[h] * gam[h]).astype(BF16) for h in heads]
        E = [_dot_nt(vnb[h], dSb[h]) for h in heads]
        dKb = [_dot(dMG[h], K[h]) for h in heads]
        bg = [bcol[h] * gcol[h] for h in heads]
        dc_all = jnp.zeros((C, LANES), F32)
        db_all = jnp.zeros((C, LANES), F32)
        for h in heads:
            dq_ref[:, _hs(h)] = gcol[h] * _dot_nt(dO[h], Sb[h]) + _dot(dPG[h], K[h])
            dk_ref[:, _hs(h)] = (_dot_tn(dPG[h], Q[h]) + _dot_tn(dMG[h], Kb[h]) + bcol[h] * dKb[h]
                                 - bg[h] * _dot_nt(dRb[h], Sb[h]) + dcol[h] * E[h])
            dv_ref[:, _hs(h)] = bcol[h] * dR[h]
            ds_ref[h] = glast[h] * dSn[h] + _dot_tn(gcol[h] * Q[h], dO[h]) - _dot_tn(bg[h] * K[h], dRb[h])
            dbeta = rsum(dKb[h] * K[h]) + rsum(dR[h] * (V[h] - gcol[h] * KS[h]))
            X = dP[h] * P[h] + dM[h] * M[h]
            ddel = rsum(K[h] * E[h]) * dcol[h]
            colsum = _dotf(X, ones, _TN)[:, 0:1]
            dc = (rsum(X) - colsum + gcol[h] * rsum(dO[h].astype(F32) * QS[h]) - bg[h] * rsum(dR[h] * KS[h]) - ddel)
            last = jnp.sum(ddel, axis=0, keepdims=True) + glast[h] * jnp.sum(rsum(dSn[h] * S[h]), axis=0, keepdims=True)
            dc_all = dc_all + jnp.where(lane == h, dc + jnp.where(rowl == C - 1, last, 0.0), 0.0)
            db_all = db_all + jnp.where(lane == N_HEAD + h, dbeta, 0.0)
        dgb_ref[...] = _dotf((col >= row).astype(F32), dc_all) + db_all
        dgn_ref[...] += dgn

    cspec = lambda w, cb=0: pl.BlockSpec((C, w), lambda n: (nc - 1 - n, cb))
    hist = lambda a, b: pl.BlockSpec((1, N_HEAD, a, b), lambda n: (nc - 1 - n, 0, 0, 0))
    return pl.pallas_call(
        body,
        out_shape=(jax.ShapeDtypeStruct((T, BR_W), F32),) * 3 + (
            jax.ShapeDtypeStruct((T, LANES), F32), jax.ShapeDtypeStruct((T, BR_W), BF16),
            jax.ShapeDtypeStruct((1, D_HEAD), F32)),
        grid=(nc,),
        in_specs=[cspec(BR_W), cspec(BR_W), cspec(BR_W), cspec(LANES), cspec(BR_W, zcol), _full((1, D_HEAD)),
                  cspec(BR_W), hist(D_HEAD, D_HEAD), hist(C, C), cspec(BR_W), cspec(BR_W)],
        out_specs=(cspec(BR_W), cspec(BR_W), cspec(BR_W), cspec(LANES), cspec(BR_W), _full((1, D_HEAD))),
        scratch_shapes=[pltpu.VMEM((N_HEAD, D_HEAD, D_HEAD), F32)], name="gdn_chunk_bwd",
        compiler_params=_cp("arbitrary"))(q, k, v, gb, proj, gnorm, oraw, shist, tinv_all, vn_all, dog)


SB_COL = SB_OFF // BR_W
SB_SCALE = D_HEAD ** -0.5


def _sb_pre(proj, gq, gk):
    T = proj.shape[0]
    tm = min(TM, T)

    def body(xq_ref, xk_ref, xv_ref, gq_ref, gk_ref, q_ref, k_ref, v_ref):
        for h in range(N_HEAD):
            for x_ref, g_ref, ref, scale in ((xq_ref, gq_ref, q_ref, SB_SCALE), (xk_ref, gk_ref, k_ref, 1.0)):
                xh = x_ref[:, _hs(h)].astype(F32)
                r = lax.rsqrt(jnp.mean(xh * xh, axis=-1, keepdims=True) + EPS)
                ref[:, _hs(h)] = (xh * (r * scale) * g_ref[...]).astype(BF16)
        v_ref[...] = xv_ref[...]

    return pl.pallas_call(
        body, out_shape=(jax.ShapeDtypeStruct((T, BR_W), BF16),) * 3, grid=(T // tm,),
        in_specs=[_rowspec(tm, BR_W, SB_COL), _rowspec(tm, BR_W, SB_COL + 1), _rowspec(tm, BR_W, SB_COL + 2),
                  _full((1, D_HEAD)), _full((1, D_HEAD))],
        out_specs=(_rowspec(tm, BR_W),) * 3, name="sb_pre", compiler_params=_cp("parallel"))(proj, proj, proj, gq, gk)


def _sb_pre_bwd(proj, gq, gk, dq, dk, dv):
    T = proj.shape[0]
    tm = min(TM, T)

    def body(xq_ref, xk_ref, gq_ref, gk_ref, dq_ref, dk_ref, dv_ref, dx_ref, dgq_ref, dgk_ref):
        i = pl.program_id(0)

        @pl.when(i == 0)
        def _():
            dgq_ref[...] = jnp.zeros_like(dgq_ref)
            dgk_ref[...] = jnp.zeros_like(dgk_ref)

        for off, x_ref, g_ref, d_ref, dg_ref, scale in ((0, xq_ref, gq_ref, dq_ref, dgq_ref, SB_SCALE),
                                                        (BR_W, xk_ref, gk_ref, dk_ref, dgk_ref, 1.0)):
            dg = jnp.zeros((1, D_HEAD), F32)
            for h in range(N_HEAD):
                xh = x_ref[:, _hs(h)].astype(F32)
                r = lax.rsqrt(jnp.mean(xh * xh, axis=-1, keepdims=True) + EPS)
                y = xh * r
                dn = d_ref[:, _hs(h)] * scale
                dg = dg + jnp.sum(dn * y, axis=0, keepdims=True)
                dy = dn * g_ref[...]
                dx_ref[:, off + h * D_HEAD:off + (h + 1) * D_HEAD] = (
                    r * (dy - y * jnp.mean(dy * y, axis=-1, keepdims=True))).astype(BF16)
            dg_ref[...] += dg
        dx_ref[:, 2 * BR_W:] = dv_ref[...].astype(BF16)

    return pl.pallas_call(
        body,
        out_shape=(jax.ShapeDtypeStruct((T, 3 * BR_W), BF16), jax.ShapeDtypeStruct((1, D_HEAD), F32),
                   jax.ShapeDtypeStruct((1, D_HEAD), F32)),
        grid=(T // tm,),
        in_specs=[_rowspec(tm, BR_W, SB_COL), _rowspec(tm, BR_W, SB_COL + 1), _full((1, D_HEAD)), _full((1, D_HEAD)),
                  _rowspec(tm, BR_W), _rowspec(tm, BR_W), _rowspec(tm, BR_W)],
        out_specs=(_rowspec(tm, 3 * BR_W), _full((1, D_HEAD)), _full((1, D_HEAD))), name="sb_pre_bwd",
        compiler_params=_cp("arbitrary"))(proj, proj, gq, gk, dq, dk, dv)


def _sb_scores(q_ref, k_ref, diag):
    blk = q_ref.shape[0]
    z = _dot_nt(q_ref[...], k_ref[...])
    zc = jnp.minimum(z, 30.0)
    sp = jnp.log(1.0 + jnp.exp(zc)) + (z - zc)
    if not diag:
        return z, sp, None
    mask = _iota((blk, blk), 1) < _iota((blk, blk), 0)
    return z, jnp.where(mask, sp, 0.0), mask


def _sb_fwd(sq, sk, sv):
    T = sq.shape[0]
    blk = min(SB_BLK, T)
    w = min(SB_W, blk)
    nb, nsub = T // blk, blk // w

    def body(q_ref, k_ref, v_ref, o_ref, lt_ref, acc_ref, r_ref):
        qi, kk = pl.program_id(1), pl.program_id(2)

        @pl.when(kk == 0)
        def _():
            acc_ref[...] = jnp.zeros_like(acc_ref)
            r_ref[...] = jnp.zeros_like(r_ref)

        def block(diag):
            z, sp, mask = _sb_scores(q_ref, k_ref, diag)
            after = (_iota((w, w), 0) > _iota((w, w), 1)).astype(BF16)
            r = r_ref[...]
            acc = acc_ref[...]
            for sb in reversed(range(nsub)):
                cs = slice(sb * w, (sb + 1) * w)
                sps = sp[:, cs]
                a = jnp.exp(z[:, cs] - sps - _dot(sps, after) - r)
                if diag:
                    a = jnp.where(mask[:, cs], a, 0.0)
                acc = acc + _dot(a, v_ref[cs, :])
                r = r + jnp.sum(sps, axis=-1, keepdims=True)
            acc_ref[...] = acc
            r_ref[...] = r

        pl.when(kk == 0)(functools.partial(block, True))
        pl.when(jnp.logical_and(kk > 0, kk <= qi))(functools.partial(block, False))

        @pl.when(kk == qi)
        def _():
            o_ref[...] = acc_ref[...].astype(BF16)
            lt_ref[0] = r_ref[...]

    qspec = pl.BlockSpec((blk, D_HEAD), lambda h, i, j: (i, h))
    kspec = pl.BlockSpec((blk, D_HEAD), lambda h, i, j: (jnp.maximum(i - j, 0), h))
    return pl.pallas_call(
        body,
        out_shape=(jax.ShapeDtypeStruct((T, BR_W), BF16), jax.ShapeDtypeStruct((N_HEAD, T, 1), F32)),
        grid=(N_HEAD, nb, nb), in_specs=[qspec, kspec, kspec],
        out_specs=(qspec, pl.BlockSpec((1, blk, 1), lambda h, i, j: (h, i, 0))),
        scratch_shapes=[pltpu.VMEM((blk, D_HEAD), F32), pltpu.VMEM((blk, 1), F32)], name="sb_fwd",
        compiler_params=_cp("parallel", "parallel", "arbitrary"))(sq, sk, sv)


def _sb_bwd(sq, sk, sv, ltot, do):
    T = sq.shape[0]
    blk = min(SB_BLK, T)
    w = min(SB_W, blk)
    nb, nsub = T // blk, blk // w

    def body(q_ref, k_ref, v_ref, lt_ref, do_ref, dq_ref, dk_ref, dv_ref, acc_ref, p_ref, g_ref):
        qi, kj = pl.program_id(1), pl.program_id(2)

        @pl.when(jnp.logical_and(qi == 0, kj == 0))
        def _():
            dk_ref[...] = jnp.zeros_like(dk_ref)
            dv_ref[...] = jnp.zeros_like(dv_ref)

        @pl.when(kj == 0)
        def _():
            acc_ref[...] = jnp.zeros_like(acc_ref)
            p_ref[...] = lt_ref[0]
            g_ref[...] = jnp.zeros_like(g_ref)

        def block(diag):
            z, sp, mask = _sb_scores(q_ref, k_ref, diag)
            d_a = _dot_nt(do_ref[...], v_ref[...])
            after = (_iota((w, w), 0) > _iota((w, w), 1)).astype(BF16)
            before = (_iota((w, w), 0) < _iota((w, w), 1)).astype(BF16)
            rest = p_ref[...]
            hg = g_ref[...]
            acc = acc_ref[...]
            base = pl.multiple_of(kj * blk, blk)
            for sb in range(nsub):
                cs = slice(sb * w, (sb + 1) * w)
                sps, zs = sp[:, cs], z[:, cs]
                rest = rest - jnp.sum(sps, axis=-1, keepdims=True)
                a = jnp.exp(zs - sps - _dot(sps, after) - rest)
                if diag:
                    a = jnp.where(mask[:, cs], a, 0.0)
                g = a * d_a[:, cs]
                sig = jnp.exp(zs - sps)
                dz = g - sig * (g + (hg + _dot(g, before)))
                if diag:
                    dz = jnp.where(mask[:, cs], dz, 0.0)
                dz = dz.astype(BF16)
                rows = pl.ds(base + sb * w, w)
                dv_ref[rows, :] += _dot_tn(a, do_ref[...])
                dk_ref[rows, :] += _dot_tn(dz, q_ref[...])
                acc = acc + _dot(dz, k_ref[cs, :])
                hg = hg + jnp.sum(g, axis=-1, keepdims=True)
            acc_ref[...] = acc
            p_ref[...] = rest
            g_ref[...] = hg

        pl.when(kj == qi)(functools.partial(block, True))
        pl.when(kj < qi)(functools.partial(block, False))

        @pl.when(kj == qi)
        def _():
            dq_ref[...] = acc_ref[...]

    qspec = pl.BlockSpec((blk, D_HEAD), lambda h, i, j: (i, h))
    kspec = pl.BlockSpec((blk, D_HEAD), lambda h, i, j: (jnp.minimum(j, i), h))
    full = pl.BlockSpec((T, D_HEAD), lambda h, i, j: (0, h))
    return pl.pallas_call(
        body, out_shape=(jax.ShapeDtypeStruct((T, BR_W), F32),) * 3, grid=(N_HEAD, nb, nb),
        in_specs=[qspec, kspec, kspec, pl.BlockSpec((1, blk, 1), lambda h, i, j: (h, i, 0)), qspec],
        out_specs=(qspec, full, full),
        scratch_shapes=[pltpu.VMEM((blk, D_HEAD), F32), pltpu.VMEM((blk, 1), F32), pltpu.VMEM((blk, 1), F32)],
        name="sb_bwd", compiler_params=_cp("arbitrary", "arbitrary", "arbitrary"))(sq, sk, sv, ltot, do)


def _mem_kv(mem, gm, w_kv, gk):
    def body(mem_ref, gm_ref, w_ref, gk_ref, mn_ref, kv_ref, kh_ref, vm_ref):
        mv = mem_ref[...]
        r = lax.rsqrt(jnp.mean(mv * mv, axis=-1, keepdims=True) + EPS)
        mn = (mv * r * gm_ref[...]).astype(BF16)
        mn_ref[...] = mn
        kv = lax.dot_general(mn, w_ref[...], _NN, preferred_element_type=F32)
        kv_ref[...] = kv
        for h in range(N_HEAD):
            kh = kv[:, _hs(h)]
            rk = lax.rsqrt(jnp.mean(kh * kh, axis=-1, keepdims=True) + EPS)
            kh_ref[:, _hs(h)] = (kh * rk * gk_ref[...]).astype(BF16)
        vm_ref[...] = kv[:, BR_W:].astype(BF16)

    return pl.pallas_call(
        body,
        out_shape=(jax.ShapeDtypeStruct((N_MEM, D_MODEL), BF16), jax.ShapeDtypeStruct((N_MEM, 2 * BR_W), F32),
                   jax.ShapeDtypeStruct((N_MEM, BR_W), BF16), jax.ShapeDtypeStruct((N_MEM, BR_W), BF16)),
        name="mem_kv", compiler_params=_cp())(mem, gm, w_kv, gk)


def _mem_q(x_ref, gq_ref, h):
    xh = x_ref[:, _hs(h)].astype(F32)
    r = lax.rsqrt(jnp.mean(xh * xh, axis=-1, keepdims=True) + EPS)
    return r, xh * r


def _mem_probs(qn, kh):
    s = _dot_nt(qn, kh) * (D_HEAD ** -0.5)
    e = jnp.exp(s - jnp.max(s, axis=-1, keepdims=True))
    return e / jnp.sum(e, axis=-1, keepdims=True)


def _mem_fwd(proj, kh, vm, gq):
    T = proj.shape[0]
    tm = min(TM, T)

    def body(x_ref, kh_ref, vm_ref, gq_ref, o_ref):
        for h in range(N_HEAD):
            _, y = _mem_q(x_ref, gq_ref, h)
            p = _mem_probs((y * gq_ref[...]).astype(BF16), kh_ref[:, _hs(h)])
            o_ref[:, _hs(h)] = _dot(p, vm_ref[:, _hs(h)]).astype(BF16)

    return pl.pallas_call(
        body, out_shape=jax.ShapeDtypeStruct((T, BR_W), BF16), grid=(T // tm,),
        in_specs=[_rowspec(tm, BR_W, MEMQ_OFF // BR_W), _full((N_MEM, BR_W)), _full((N_MEM, BR_W)),
                  _full((1, D_HEAD))],
        out_specs=_rowspec(tm, BR_W), name="mem_fwd", compiler_params=_cp("parallel"))(proj, kh, vm, gq)


def _mem_bwd(proj, kh, vm, gq, do):
    T = proj.shape[0]
    tm = min(TM, T)

    def body(x_ref, kh_ref, vm_ref, gq_ref, do_ref, dx_ref, dkh_ref, dvm_ref, dgq_ref):
        i = pl.program_id(0)

        @pl.when(i == 0)
        def _():
            dkh_ref[...] = jnp.zeros_like(dkh_ref)
            dvm_ref[...] = jnp.zeros_like(dvm_ref)
            dgq_ref[...] = jnp.zeros_like(dgq_ref)

        dg = jnp.zeros((1, D_HEAD), F32)
        for h in range(N_HEAD):
            r, y = _mem_q(x_ref, gq_ref, h)
            qn = (y * gq_ref[...]).astype(BF16)
            p = _mem_probs(qn, kh_ref[:, _hs(h)])
            dov = do_ref[:, _hs(h)]
            dp = _dot_nt(dov, vm_ref[:, _hs(h)])
            ds = p * (dp - jnp.sum(dp * p, axis=-1, keepdims=True)) * (D_HEAD ** -0.5)
            dqn = _dot(ds, kh_ref[:, _hs(h)])
            dkh_ref[:, _hs(h)] += _dot_tn(ds, qn)
            dvm_ref[:, _hs(h)] += _dot_tn(p, dov)
            dg = dg + jnp.sum(dqn * y, axis=0, keepdims=True)
            dy = dqn * gq_ref[...]
            dx_ref[:, _hs(h)] = (r * (dy - y * jnp.mean(dy * y, axis=-1, keepdims=True))).astype(BF16)
        dgq_ref[...] += dg

    return pl.pallas_call(
        body,
        out_shape=(jax.ShapeDtypeStruct((T, BR_W), BF16), jax.ShapeDtypeStruct((N_MEM, BR_W), F32),
                   jax.ShapeDtypeStruct((N_MEM, BR_W), F32), jax.ShapeDtypeStruct((1, D_HEAD), F32)),
        grid=(T // tm,),
        in_specs=[_rowspec(tm, BR_W, MEMQ_OFF // BR_W), _full((N_MEM, BR_W)), _full((N_MEM, BR_W)),
                  _full((1, D_HEAD)), _rowspec(tm, BR_W)],
        out_specs=(_rowspec(tm, BR_W), _full((N_MEM, BR_W)), _full((N_MEM, BR_W)), _full((1, D_HEAD))),
        name="mem_bwd", compiler_params=_cp("arbitrary"))(proj, kh, vm, gq, do)


def _mem_kv_bwd(mem, gm, w_kv, gk, kv, mn, dkh, dvm):
    def body(mem_ref, gm_ref, w_ref, gk_ref, kv_ref, mn_ref, dkh_ref, dvm_ref, dw_ref, dgm_ref, dgk_ref, dkv_ref):
        dgk = jnp.zeros((1, D_HEAD), F32)
        for h in range(N_HEAD):
            kh = kv_ref[:, _hs(h)]
            r = lax.rsqrt(jnp.mean(kh * kh, axis=-1, keepdims=True) + EPS)
            y = kh * r
            dn = dkh_ref[:, _hs(h)]
            dgk = dgk + jnp.sum(dn * y, axis=0, keepdims=True)
            dy = dn * gk_ref[...]
            dkv_ref[:, _hs(h)] = (r * (dy - y * jnp.mean(dy * y, axis=-1, keepdims=True))).astype(BF16)
        dkv_ref[:, BR_W:] = dvm_ref[...].astype(BF16)
        dgk_ref[...] = dgk
        dkv = dkv_ref[...]
        dw_ref[...] = lax.dot_general(mn_ref[...], dkv, _TN, preferred_element_type=F32)
        dmn = lax.dot_general(dkv, w_ref[...], _NT, preferred_element_type=F32)
        mv = mem_ref[...]
        memn = mv * lax.rsqrt(jnp.mean(mv * mv, axis=-1, keepdims=True) + EPS)
        dgm_ref[...] = jnp.sum(dmn * memn, axis=0, keepdims=True)

    return pl.pallas_call(
        body,
        out_shape=(jax.ShapeDtypeStruct((D_MODEL, 2 * BR_W), F32), jax.ShapeDtypeStruct((1, D_MODEL), F32),
                   jax.ShapeDtypeStruct((1, D_HEAD), F32)),
        scratch_shapes=[pltpu.VMEM((N_MEM, 2 * BR_W), BF16)], name="mem_kv_bwd",
        compiler_params=_cp())(mem, gm, w_kv, gk, kv, mn, dkh, dvm)


def _merge_fwd(og, osb, om, proj, wg, ws, wm):
    T = og.shape[0]
    tm = min(TM, T)

    def body(og_ref, os_ref, om_ref, g0, g1, g2, wg_ref, ws_ref, wm_ref, mix_ref, yg_ref, ys_ref, ym_ref):
        mix = jnp.zeros((tm, D_MODEL), F32)
        for o_ref, gl_ref, w_ref, y_ref in ((og_ref, g0, wg_ref, yg_ref), (os_ref, g1, ws_ref, ys_ref),
                                            (om_ref, g2, wm_ref, ym_ref)):
            y = lax.dot_general(o_ref[...], w_ref[...], _NN, preferred_element_type=F32)
            y_ref[...] = y.astype(BF16)
            mix = mix + _sigmoid(gl_ref[...].astype(F32)) * y
        mix_ref[...] = mix.astype(BF16)

    br = _rowspec(tm, BR_W)
    wspec = _full((BR_W, D_MODEL))
    out = _rowspec(tm, D_MODEL)
    gates = [_rowspec(tm, D_MODEL, GATE_COL + b) for b in range(3)]
    return pl.pallas_call(
        body, out_shape=(jax.ShapeDtypeStruct((T, D_MODEL), BF16),) * 4, grid=(T // tm,),
        in_specs=[br, br, br, *gates, wspec, wspec, wspec],
        out_specs=(out,) * 4, name="merge_fwd",
        compiler_params=_cp("parallel"))(og, osb, om, proj, proj, proj, wg, ws, wm)


def _merge_bwd(dmix, proj, ys, os_, ws):
    T = dmix.shape[0]
    tm = min(TM, T)

    def body(dmix_ref, g0, g1, g2, y0, y1, y2, o0, o1, o2, w0, w1, w2, dgl_ref, do0, do1, do2, dw0, dw1, dw2):
        i = pl.program_id(0)
        dm = dmix_ref[...].astype(F32)
        for b, (gl_ref, y_ref, o_ref, w_ref, do_ref, dw_ref) in enumerate((
                (g0, y0, o0, w0, do0, dw0), (g1, y1, o1, w1, do1, dw1), (g2, y2, o2, w2, do2, dw2))):
            gate = _sigmoid(gl_ref[...].astype(F32))
            dgl_ref[:, b * D_MODEL:(b + 1) * D_MODEL] = (dm * y_ref[...].astype(F32) * gate * (1.0 - gate)).astype(BF16)
            dy = (gate * dm).astype(BF16)
            do_ref[...] = lax.dot_general(dy, w_ref[...], _NT, preferred_element_type=F32).astype(BF16)
            _accum(dw_ref, i == 0, lax.dot_general(dy, o_ref[...], _TN, preferred_element_type=F32))

    br = _rowspec(tm, BR_W)
    wide = _rowspec(tm, D_MODEL)
    wspec = _full((BR_W, D_MODEL))
    wtspec = _full((D_MODEL, BR_W))
    gates = [_rowspec(tm, D_MODEL, GATE_COL + b) for b in range(3)]
    return pl.pallas_call(
        body,
        out_shape=(jax.ShapeDtypeStruct((T, 3 * D_MODEL), BF16),) + (jax.ShapeDtypeStruct((T, BR_W), BF16),) * 3
        + (jax.ShapeDtypeStruct((D_MODEL, BR_W), F32),) * 3,
        grid=(T // tm,),
        in_specs=[wide, *gates, wide, wide, wide, br, br, br, wspec, wspec, wspec],
        out_specs=(_rowspec(tm, 3 * D_MODEL), br, br, br, wtspec, wtspec, wtspec), name="merge_bwd",
        compiler_params=_cp("arbitrary"))(dmix, proj, proj, proj, *ys, *os_, *ws)


def _loss(y, tgt):
    T, dm = y.shape
    tm = min(TM, T)

    def body(y_ref, t_ref, dy_ref, dyb_ref, sq_ref):
        err = y_ref[...] - t_ref[...]
        dy = err * (1.0 / dm)
        dy_ref[...] = dy
        dyb_ref[...] = dy.astype(BF16)
        _accum(sq_ref, pl.program_id(0) == 0, jnp.sum(err * err, axis=0, keepdims=True))

    return pl.pallas_call(
        body,
        out_shape=(jax.ShapeDtypeStruct((T, dm), F32), jax.ShapeDtypeStruct((T, dm), BF16),
                   jax.ShapeDtypeStruct((1, dm), F32)),
        grid=(T // tm,), in_specs=[_rowspec(tm, dm), _rowspec(tm, dm)],
        out_specs=(_rowspec(tm, dm), _rowspec(tm, dm), _full((1, dm))), name="loss",
        compiler_params=_cp("arbitrary"))(y, tgt)


def _local_step(x, mem, tgt, W, P):
    w_in = W["w_in"]
    w_main = jnp.concatenate([w_in[:, :SB_OFF], w_in[:, SB_OFF + 8:]], axis=1)
    w_ab = jnp.pad(w_in[:, SB_OFF:SB_OFF + 8], ((0, 0), (0, LANES - 8)))
    avec = jnp.pad(jnp.concatenate([P["a_log"], P["dt_bias"]], axis=0), ((0, 0), (0, LANES - N_HEAD)))
    wbr = (W["w_br_gdn"], W["w_br_sb"], W["w_br_mem"])

    h = _rms_fwd(x, P["norm1_g"], "rms1")
    proj = _mm(h, w_main, "nn", BF16, "in_proj")
    ab = _mm(h, w_ab, "nn", F32, "in_proj_ab")
    q, k, v, gb = _gdn_pre(proj, P["conv_w"], ab, avec)
    tinv = _gdn_inv(k, gb)
    og, oraw, shist, vn = _gdn_fwd(q, k, v, gb, proj, P["gdn_norm_g"], tinv)
    sq, sk, sv = _sb_pre(proj, P["sb_q_norm_g"], P["sb_k_norm_g"])
    osb, ltot = _sb_fwd(sq, sk, sv)
    mn, kv, kh, vm = _mem_kv(mem, P["mem_norm_g"], W["w_mem_kv"], P["mem_k_norm_g"])
    om = _mem_fwd(proj, kh, vm, P["mem_q_norm_g"])
    mix, yg, ys, ym = _merge_fwd(og, osb, om, proj, *wbr)
    x1 = _mm(mix, W["w_o"], "nn", F32, "out_proj", extra=x, epi=_epi_add)
    h2 = _rms_fwd(x1, P["norm2_g"], "rms2")
    u = _mm(h2, W["w_up"], "nn", BF16, "mlp_up")
    y = _mm(u, W["w_down"], "nn", F32, "mlp_down", a_fn=_relu2, extra=x1, epi=_epi_add)
    dy, dyb, sq_err = _loss(y, tgt)

    G = {}
    du = _mm(dyb, W["w_down"], "nt", BF16, "d_mlp_act", extra=u, epi=_epi_drelu2)
    G["w_down"] = _mm(u, dyb, "tn", F32, "dw_down", a_fn=_relu2)
    G["w_up"] = _mm(du, h2, "tn", F32, "dw_up")
    dh2 = _mm(du, W["w_up"], "nt", F32, "d_h2")
    dx1, dx1b, G["norm2_g"] = _rms_bwd(dh2, x1, P["norm2_g"], dy, "rms2_bwd")
    dmix = _mm(dx1b, W["w_o"], "nt", BF16, "d_mix")
    G["w_o"] = _mm(mix, dx1b, "tn", F32, "dw_o")
    dgates, dog, dosb, dom, G["w_br_gdn"], G["w_br_sb"], G["w_br_mem"] = _merge_bwd(
        dmix, proj, (yg, ys, ym), (og, osb, om), wbr)
    dq, dk, dv, dgb, dz, G["gdn_norm_g"] = _gdn_bwd(q, k, v, gb, proj, P["gdn_norm_g"], oraw, shist, tinv, vn, dog)
    dxc, dab, G["conv_w"], dav = _gdn_pre_bwd(proj, P["conv_w"], ab, avec, dq, dk, dv, dgb)
    dqkv = _conv_bwd(dxc, P["conv_w"])
    G["a_log"], G["dt_bias"] = dav[0:1, :N_HEAD], dav[1:2, :N_HEAD]
    dsq, dsk, dsv = _sb_bwd(sq, sk, sv, ltot, dosb)
    dsb, G["sb_q_norm_g"], G["sb_k_norm_g"] = _sb_pre_bwd(proj, P["sb_q_norm_g"], P["sb_k_norm_g"], dsq, dsk, dsv)
    dmemq, dkh, dvm, G["mem_q_norm_g"] = _mem_bwd(proj, kh, vm, P["mem_q_norm_g"], dom)
    G["w_mem_kv"], G["mem_norm_g"], G["mem_k_norm_g"] = _mem_kv_bwd(
        mem, P["mem_norm_g"], W["w_mem_kv"], P["mem_k_norm_g"], kv, mn, dkh, dvm)
    dproj = jnp.concatenate([dqkv, dz, dsb, dmemq, dgates], axis=1)
    dw_main = _mm(dproj, h, "tn", F32, "dw_in")
    dw_ab = _mm(dab, h, "tn", F32, "dw_in_ab")
    G["w_in"] = jnp.concatenate([dw_main[:SB_OFF], dw_ab[:8], dw_main[SB_OFF:]], axis=0)
    dh = _mm(dproj, w_main, "nt", F32, "d_h")
    dh = _mm(dab, w_ab, "nt", F32, "d_h_ab", extra=dh, epi=_epi_add)
    dx, _, G["norm1_g"] = _rms_bwd(dh, x, P["norm1_g"], dx1, "rms1_bwd")
    return sq_err, dx, G


HBM = pl.BlockSpec(memory_space=pl.ANY)


def _comm(name, ins, out_shapes, plan):
    n_in, n_out = len(ins), len(out_shapes)
    probe = plan([None] * n_in, [None] * n_out, 0, 0, 0, dry=True)
    n_copy = probe

    def body(*refs):
        in_refs, out_refs = refs[:n_in], refs[n_in:n_in + n_out]
        send_sems, recv_sems = refs[n_in + n_out:]
        x, y, c = lax.axis_index("x"), lax.axis_index("y"), lax.axis_index("c")
        copies = []
        for k, (src, dst, dev) in enumerate(plan(in_refs, out_refs, x, y, c, dry=False)):
            if dev is None:
                cp = pltpu.make_async_copy(src, dst, send_sems.at[k])
            else:
                cp = pltpu.make_async_remote_copy(src_ref=src, dst_ref=dst, send_sem=send_sems.at[k],
                                                  recv_sem=recv_sems.at[k], device_id=dev, device_id_type=MESH)
            cp.start()
            copies.append(cp)
        for cp in copies:
            cp.wait()

    return pl.pallas_call(
        body, out_shape=tuple(out_shapes), in_specs=[HBM] * n_in, out_specs=tuple([HBM] * n_out),
        scratch_shapes=[pltpu.SemaphoreType.DMA((n_copy,)), pltpu.SemaphoreType.DMA((n_copy,))], name=name)(*ins)


def _other_chips(x, y):
    return ((1 - x, y), (x, 1 - y), (1 - x, 1 - y))


def _gather_weights(parts, conv):
    n = len(parts)
    n_copy = 6 * n + 3

    def body(*refs):
        ins, cin = refs[:n], refs[n]
        outs, cout = refs[n + 1:2 * n + 1], refs[2 * n + 1]
        send, recv = refs[2 * n + 2:]
        x, y, c = lax.axis_index("x"), lax.axis_index("y"), lax.axis_index("c")
        me = 2 * x + y
        chips = _other_chips(x, y)

        def remote(src, dst, k, dev):
            return pltpu.make_async_remote_copy(src_ref=src, dst_ref=dst, send_sem=send.at[k], recv_sem=recv.at[k],
                                                device_id=dev, device_id_type=MESH)

        def my_half(p):
            hr = ins[p].shape[0] // 2
            return pl.ds(pl.multiple_of(c * hr, 16), hr)

        sent = []
        for p in range(n):
            for f, (px, py) in enumerate(chips):
                cp = remote(ins[p].at[my_half(p)], outs[p].at[me, my_half(p)], 6 * p + f, (px, py, c))
                cp.start()
                sent.append(cp)
        direct = []
        for f, (px, py) in enumerate(chips):
            cp = remote(cin, cout.at[me], 6 * n + f, (px, py, c))
            cp.start()
            direct.append(cp)
        passed = []
        for p in range(n):
            for f, (px, py) in enumerate(chips):
                landed = outs[p].at[2 * px + py, my_half(p)]
                remote(landed, landed, 6 * p + f, (px, py, c)).wait_recv()
                cp = remote(landed, landed, 6 * p + 3 + f, (x, y, 1 - c))
                cp.start()
                passed.append(cp)
        for cp in sent:
            cp.wait_send()
        for cp in passed + direct:
            cp.wait()

    shapes = [jax.ShapeDtypeStruct((4,) + p.shape, p.dtype) for p in parts + [conv]]
    res = pl.pallas_call(
        body, out_shape=tuple(shapes), in_specs=[HBM] * (n + 1), out_specs=tuple([HBM] * (n + 1)),
        scratch_shapes=[pltpu.SemaphoreType.DMA((n_copy,)), pltpu.SemaphoreType.DMA((n_copy,))],
        name="gather_weights")(*parts, conv)
    chip = 2 * lax.axis_index("x") + lax.axis_index("y")
    res = [lax.dynamic_update_index_in_dim(r, a, chip, 0) for r, a in zip(res, parts + [conv])]
    return res[:n], res[n]


def _swap_halves(slabs):
    n = len(slabs)

    def body(*refs):
        ins, outs = refs[:n], refs[n:2 * n]
        send, recv = refs[2 * n:]
        x, y, c = lax.axis_index("x"), lax.axis_index("y"), lax.axis_index("c")
        other = (x, y, 1 - c)
        for p in range(n):
            for j in range(4):
                pltpu.make_async_remote_copy(src_ref=ins[p].at[j, 1 - c], dst_ref=outs[p].at[j], send_sem=send.at[p],
                                             recv_sem=recv.at[p], device_id=other, device_id_type=MESH).start()
        for p in range(n):
            pltpu.make_async_remote_copy(src_ref=outs[p], dst_ref=outs[p], send_sem=send.at[p], recv_sem=recv.at[p],
                                         device_id=other, device_id_type=MESH).wait()

    shapes = [jax.ShapeDtypeStruct((4,) + s.shape[2:], s.dtype) for s in slabs]
    return pl.pallas_call(
        body, out_shape=tuple(shapes), in_specs=[HBM] * n, out_specs=tuple([HBM] * n),
        scratch_shapes=[pltpu.SemaphoreType.DMA((n,)), pltpu.SemaphoreType.DMA((n,))], name="grad_swap_cores")(*slabs)


def _scatter_chips(pairs):
    def plan(ins, outs, x, y, c, dry):
        if dry:
            return 3 * len(ins)
        me = 2 * x + y
        copies = []
        for src, dst in zip(ins, outs):
            copies += [(src.at[2 * px + py], dst.at[me], (px, py, c)) for px, py in _other_chips(x, y)]
        return copies

    return _comm("grad_to_owner", pairs, [jax.ShapeDtypeStruct(a.shape, a.dtype) for a in pairs], plan)


def _join_halves(both):
    n = len(both)

    def body(*refs):
        bufs = refs[n:2 * n]
        send, recv = refs[2 * n:]
        x, y, c = lax.axis_index("x"), lax.axis_index("y"), lax.axis_index("c")
        copies = []
        for p in range(n):
            cp = pltpu.make_async_remote_copy(src_ref=bufs[p].at[c], dst_ref=bufs[p].at[c], send_sem=send.at[p],
                                              recv_sem=recv.at[p], device_id=(x, y, 1 - c), device_id_type=MESH)
            cp.start()
            copies.append(cp)
        for cp in copies:
            cp.wait()

    return pl.pallas_call(
        body, out_shape=tuple(jax.ShapeDtypeStruct(a.shape, a.dtype) for a in both), in_specs=[HBM] * n,
        out_specs=tuple([HBM] * n), input_output_aliases={p: p for p in range(n)},
        scratch_shapes=[pltpu.SemaphoreType.DMA((n,)), pltpu.SemaphoreType.DMA((n,))], name="grad_join_cores")(*both)


def _gather_all(a, name):
    def plan(ins, outs, x, y, c, dry):
        if dry:
            return 8
        me = 4 * x + 2 * y + c
        copies = [(ins[0], outs[0].at[me], None)]
        for f in range(1, 8):
            peer = (1 - x if f & 4 else x, 1 - y if f & 2 else y, 1 - c if f & 1 else c)
            copies.append((ins[0], outs[0].at[me], peer))
        return copies

    return _comm(name, [a], [jax.ShapeDtypeStruct((8,) + a.shape, a.dtype)], plan)[0]


def _sum_slots(a, name, extra=None):
    n, R, _ = a.shape
    rb = min(ROW_BLK, R)

    def body(*refs):
        a_ref, o_ref = refs[0], refs[-1]
        acc = a_ref[0]
        for s in range(1, n):
            acc = acc + a_ref[s]
        if extra is not None:
            acc = acc + refs[1][...]
        o_ref[...] = acc

    ins = [a] + ([extra] if extra is not None else [])
    in_specs = [pl.BlockSpec((n, rb, LANES), lambda i: (0, i, 0))] + ([_rowspec(rb, LANES)] if extra is not None else [])
    return pl.pallas_call(
        body, out_shape=jax.ShapeDtypeStruct((R, LANES), F32), grid=(R // rb,), in_specs=in_specs,
        out_specs=_rowspec(rb, LANES), name=name, compiler_params=_cp("parallel"))(*ins)


def _pair_sum(slab, theirs, core, name):
    _, _, hr, C = slab.shape

    def body(c_ref, a_ref, b_ref, o_ref):
        o_ref[...] = (a_ref[...] + b_ref[...]).astype(BF16)

    return pl.pallas_call(
        body, out_shape=jax.ShapeDtypeStruct((4, hr, C), BF16),
        grid_spec=pltpu.PrefetchScalarGridSpec(
            num_scalar_prefetch=1, grid=(4,),
            in_specs=[pl.BlockSpec((None, None, hr, C), lambda j, c_ref: (j, c_ref[0], 0, 0)),
                      pl.BlockSpec((None, hr, C), lambda j, c_ref: (j, 0, 0))],
            out_specs=pl.BlockSpec((None, hr, C), lambda j, c_ref: (j, 0, 0))),
        name=name, compiler_params=_cp("parallel"))(core, slab, theirs)


def _chip_sum(recv, pairs, where, name):
    _, hr, C = recv.shape

    def body(w_ref, r_ref, p_ref, o_ref):
        me = w_ref[0]
        o_ref[...] = jnp.zeros_like(o_ref)
        for s in range(4):
            @pl.when(me == s)
            def _():
                o_ref[...] += p_ref[...].astype(F32)

            @pl.when(me != s)
            def _():
                o_ref[...] += r_ref[s].astype(F32)

    return pl.pallas_call(
        body, out_shape=jax.ShapeDtypeStruct((2, hr, C), F32),
        grid_spec=pltpu.PrefetchScalarGridSpec(
            num_scalar_prefetch=1, grid=(1,),
            in_specs=[pl.BlockSpec((4, hr, C), lambda i, w_ref: (0, 0, 0)),
                      pl.BlockSpec((None, hr, C), lambda i, w_ref: (w_ref[0], 0, 0))],
            out_specs=pl.BlockSpec((None, hr, C), lambda i, w_ref: (w_ref[1], 0, 0))),
        name=name, compiler_params=_cp("arbitrary"))(where, recv, pairs)


def _adamw(w, g, m, v, name):
    R, C = w.shape
    rb = min(ADAM_ROWS, R)
    c1 = 1.0 - ADAM_B1 ** ADAM_STEP
    c2 = 1.0 - ADAM_B2 ** ADAM_STEP

    def body(w_ref, g_ref, m_ref, v_ref, d_ref, nm_ref, nv_ref):
        gv = g_ref[...]
        nm = ADAM_B1 * m_ref[...] + (1.0 - ADAM_B1) * gv
        nv = ADAM_B2 * v_ref[...] + (1.0 - ADAM_B2) * (gv * gv)
        d_ref[...] = -ADAM_LR * ((nm / c1) / (jnp.sqrt(nv / c2) + ADAM_EPS) + ADAM_WD * w_ref[...])
        nm_ref[...] = nm
        nv_ref[...] = nv

    spec = _rowspec(rb, C)
    return pl.pallas_call(
        body, out_shape=(jax.ShapeDtypeStruct((R, C), F32),) * 3, grid=(R // rb,), in_specs=[spec] * 4,
        out_specs=(spec,) * 3, name=name, compiler_params=_cp("parallel"))(w, g, m, v)


def _pack_rows(parts, rows, dtype):
    flat = jnp.concatenate([p.reshape(-1).astype(dtype) for p in parts])
    return jnp.pad(flat, (0, rows * LANES - flat.shape[0])).reshape(rows, LANES)


def _small_rows(n):
    return max(n // LANES, 1)


def _pack_small(vals):
    rows = []
    for name, n in SMALL:
        r = _small_rows(n)
        rows.append(jnp.pad(vals[name].reshape(-1), (0, r * LANES - n)).reshape(r, LANES))
    flat = jnp.concatenate(rows, axis=0)
    return jnp.pad(flat, ((0, SMALL_ROWS - flat.shape[0]), (0, 0)))


def _unpack_small(pack):
    out, r0 = {}, 0
    for name, n in SMALL:
        r = _small_rows(n)
        out[name] = pack[r0:r0 + r].reshape(-1)[:n]
        r0 += r
    return out


def kernel(x, mem, norm1_g, w_in, conv_w, a_log, dt_bias, gdn_norm_g, sb_q_norm_g, sb_k_norm_g, mem_norm_g, w_mem_kv, mem_q_norm_g, mem_k_norm_g, w_br_gdn, w_br_sb, w_br_mem, w_o, norm2_g, w_up, w_down, loss_target, m_norm1_g, m_w_in, m_conv_w, m_a_log, m_dt_bias, m_gdn_norm_g, m_sb_q_norm_g, m_sb_k_norm_g, m_mem_norm_g, m_w_mem_kv, m_mem_q_norm_g, m_mem_k_norm_g, m_w_br_gdn, m_w_br_sb, m_w_br_mem, m_w_o, m_norm2_g, m_w_up, m_w_down, v_norm1_g, v_w_in, v_conv_w, v_a_log, v_dt_bias, v_gdn_norm_g, v_sb_q_norm_g, v_sb_k_norm_g, v_mem_norm_g, v_w_mem_kv, v_mem_q_norm_g, v_mem_k_norm_g, v_w_br_gdn, v_w_br_sb, v_w_br_mem, v_w_o, v_norm2_g, v_w_up, v_w_down):
    wd = dict(norm1_g=norm1_g, w_in=w_in, conv_w=conv_w, a_log=a_log, dt_bias=dt_bias, gdn_norm_g=gdn_norm_g,
              sb_q_norm_g=sb_q_norm_g, sb_k_norm_g=sb_k_norm_g, mem_norm_g=mem_norm_g, w_mem_kv=w_mem_kv,
              mem_q_norm_g=mem_q_norm_g, mem_k_norm_g=mem_k_norm_g, w_br_gdn=w_br_gdn, w_br_sb=w_br_sb,
              w_br_mem=w_br_mem, w_o=w_o, norm2_g=norm2_g, w_up=w_up, w_down=w_down)
    md = dict(norm1_g=m_norm1_g, w_in=m_w_in, conv_w=m_conv_w, a_log=m_a_log, dt_bias=m_dt_bias,
              gdn_norm_g=m_gdn_norm_g, sb_q_norm_g=m_sb_q_norm_g, sb_k_norm_g=m_sb_k_norm_g,
              mem_norm_g=m_mem_norm_g, w_mem_kv=m_w_mem_kv, mem_q_norm_g=m_mem_q_norm_g,
              mem_k_norm_g=m_mem_k_norm_g, w_br_gdn=m_w_br_gdn, w_br_sb=m_w_br_sb, w_br_mem=m_w_br_mem, w_o=m_w_o,
              norm2_g=m_norm2_g, w_up=m_w_up, w_down=m_w_down)
    vd = dict(norm1_g=v_norm1_g, w_in=v_w_in, conv_w=v_conv_w, a_log=v_a_log, dt_bias=v_dt_bias,
              gdn_norm_g=v_gdn_norm_g, sb_q_norm_g=v_sb_q_norm_g, sb_k_norm_g=v_sb_k_norm_g,
              mem_norm_g=v_mem_norm_g, w_mem_kv=v_w_mem_kv, mem_q_norm_g=v_mem_q_norm_g,
              mem_k_norm_g=v_mem_k_norm_g, w_br_gdn=v_w_br_gdn, w_br_sb=v_w_br_sb, w_br_mem=v_w_br_mem, w_o=v_w_o,
              norm2_g=v_norm2_g, w_up=v_w_up, w_down=v_w_down)
    wd, md, vd = ({n: a[0] for n, a in d.items()} for d in (wd, md, vd))
    chip = 2 * lax.axis_index("x") + lax.axis_index("y")
    core = lax.axis_index("c").astype(jnp.int32).reshape(1)
    conv_shard = wd["conv_w"].shape

    gathered, conv_all = _gather_weights([wd[n].astype(BF16) for n, _, _ in BIG], wd["conv_w"])
    W = {}
    for (name, shape, axis), blk in zip(BIG, gathered):
        W[name] = blk.reshape(4 * shape[0], shape[1]) if axis == 0 else blk.transpose(1, 0, 2).reshape(shape[0], 4 * shape[1])
    P = {n: wd[n].reshape(1, -1) for n, _ in SMALL}
    P["conv_w"] = conv_all.transpose(1, 0, 2).reshape(conv_shard[0], 4 * conv_shard[1])

    sq_err, grad_x, G = _local_step(x[0], mem[0], loss_target[0], W, P)
    loss = lax.psum(0.5 / D_MODEL * jnp.sum(sq_err), ("x", "y", "c"))

    slabs = []
    for name, (r, cc), axis in BIG:
        g = G[name]
        if axis == 0:
            slabs.append(g.reshape(4, 2, r // 2, cc))
        else:
            rows = _slab_rows(cc)
            g = jnp.pad(g.reshape(4, cc, r), ((0, 0), (0, rows - cc), (0, 0)))
            slabs.append(g.reshape(4, 2, rows // 2, r))
    theirs = _swap_halves(slabs)
    pairs = [_pair_sum(s, t, core, "pair_sum_" + n) for s, t, (n, _, _) in zip(slabs, theirs, BIG)]
    by_chip = _scatter_chips(pairs)
    where = jnp.concatenate([chip.astype(jnp.int32).reshape(1), core])
    halves = [_chip_sum(a, p, where, "chip_sum_" + n) for a, p, (n, _, _) in zip(by_chip, pairs, BIG)]
    g_big = {}
    for (name, (r, cc), axis), both in zip(BIG, _join_halves(halves)):
        full = both.reshape(-1, both.shape[-1])
        g_big[name] = full if axis == 0 else full[:cc].T

    spack = jnp.concatenate([_pack_small(G), G["conv_w"].reshape(CONV_ROWS, LANES)], axis=0)
    g_small = _sum_slots(_gather_all(spack, "gather_small_grads"), "small_grad_sum")
    g_conv_full = g_small[SMALL_ROWS:].reshape(conv_shard[0], 4 * conv_shard[1])
    g_conv = lax.dynamic_slice_in_dim(g_conv_full, chip * conv_shard[1], conv_shard[1], axis=1)

    grads, deltas, new_m, new_v = dict(g_big), {}, {}, {}
    for name, _, _ in BIG:
        deltas[name], new_m[name], new_v[name] = _adamw(wd[name], g_big[name], md[name], vd[name], "adamw_" + name)
    pack_sm = lambda d: jnp.concatenate([_pack_small(d), _pack_rows([d["conv_w"]], APACK_ROWS - SMALL_ROWS, F32)], axis=0)
    g_sm = jnp.concatenate([g_small[:SMALL_ROWS], _pack_rows([g_conv], APACK_ROWS - SMALL_ROWS, F32)], axis=0)
    small = (g_sm,) + _adamw(pack_sm(wd), g_sm, pack_sm(md), pack_sm(vd), "adamw_small")
    for out, pack in zip((grads, deltas, new_m, new_v), small):
        out.update(_unpack_small(pack[:SMALL_ROWS]))
        out["conv_w"] = pack[SMALL_ROWS:].reshape(-1)[:conv_shard[0] * conv_shard[1]].reshape(conv_shard)

    return (loss, grad_x[None], *[d[n][None] for d in (grads, deltas, new_m, new_v) for n in WEIGHTS])


def _slab_rows(n):
    return -(-n // 32) * 32
```

```python
import functools

import jax
import jax.numpy as jnp
import numpy as np
from jax import lax
from jax.experimental import pallas as pl
from jax.experimental.pallas import tpu as pltpu

F32 = jnp.float32
BF16 = jnp.bfloat16
MESH = pl.DeviceIdType.MESH

D_MODEL = 1024
N_HEAD = 4
D_HEAD = 128
BR_W = N_HEAD * D_HEAD
CONV_TAPS = 4
GDN_CHUNK = 64
INV_BLOCK = 16
INV_CHUNKS = 4
N_MEM = 256
D_FF = 4 * D_MODEL
EPS = 1e-6
LANES = 128
PROJ_W = 7168
GATE_OFF = 4096
SB_OFF = 2048
MEMQ_OFF = 3584
Z_OFF = 1536

ADAM_LR, ADAM_B1, ADAM_B2, ADAM_EPS, ADAM_WD, ADAM_STEP = 0.001, 0.9, 0.999, 1e-08, 0.01, 10

TM = 512
MM_TM = 1024
TK_TOK = 1024
GDN_STEP_CHUNKS = 4
GDN_BWD_STEP_CHUNKS = 1
G1_TM = 256
SB_BLK_Q = 512
SB_BLK = 512
SB_W = 256
VMEM_LIMIT = 48 << 20

BIG = (("w_in", (1024, 1794), 1), ("w_mem_kv", (256, 1024), 0), ("w_br_gdn", (512, 256), 1),
       ("w_br_sb", (512, 256), 1), ("w_br_mem", (512, 256), 1), ("w_o", (256, 1024), 0),
       ("w_up", (1024, 1024), 1), ("w_down", (1024, 1024), 0))
COL_SHARDED = tuple(n for n, _, a in BIG if a == 1)
GATE_COL = GATE_OFF // D_MODEL
ROW_BLK = 1024
ADAM_ROWS = 128
SMALL = (("norm1_g", 1024), ("mem_norm_g", 1024), ("norm2_g", 1024), ("gdn_norm_g", 128), ("sb_q_norm_g", 128),
         ("sb_k_norm_g", 128), ("mem_q_norm_g", 128), ("mem_k_norm_g", 128), ("a_log", 4), ("dt_bias", 4))
SMALL_ROWS = 32
CONV_ROWS = 48
SPACK_ROWS = SMALL_ROWS + CONV_ROWS
APACK_ROWS = SMALL_ROWS + 16

WEIGHTS = ("norm1_g", "w_in", "conv_w", "a_log", "dt_bias", "gdn_norm_g", "sb_q_norm_g", "sb_k_norm_g",
           "mem_norm_g", "w_mem_kv", "mem_q_norm_g", "mem_k_norm_g", "w_br_gdn", "w_br_sb", "w_br_mem", "w_o",
           "norm2_g", "w_up", "w_down")


def _cp(*sem):
    return pltpu.CompilerParams(dimension_semantics=sem if sem else None, vmem_limit_bytes=VMEM_LIMIT)


_NN = (((1,), (0,)), ((), ()))
_NT = (((1,), (1,)), ((), ()))
_TN = (((0,), (0,)), ((), ()))


def _dot(a, b, dims=_NN):
    return lax.dot_general(a.astype(BF16), b.astype(BF16), dims, preferred_element_type=F32)


def _dot_nt(a, b):
    return _dot(a, b, _NT)


def _dot_tn(a, b):
    return _dot(a, b, _TN)


def _dotf(a, b, dims=_NN):
    return lax.dot_general(a, b, dims, precision=lax.Precision.HIGHEST, preferred_element_type=F32)


def _sigmoid(v):
    return 0.5 * jnp.tanh(0.5 * v) + 0.5


def _softplus(v):
    return jnp.maximum(v, 0.0) + jnp.log(1.0 + jnp.exp(-jnp.abs(v)))


def _iota(shape, dim):
    return lax.broadcasted_iota(jnp.int32, shape, dim)


def _hs(h):
    return slice(h * D_HEAD, (h + 1) * D_HEAD)


def _rowspec(tm, w, col=0):
    return pl.BlockSpec((tm, w), lambda i: (i, col))


def _full(shape):
    return pl.BlockSpec(shape, lambda *_: (0,) * len(shape))


def _accum(ref, first, val):
    @pl.when(first)
    def _():
        ref[...] = val

    @pl.when(jnp.logical_not(first))
    def _():
        ref[...] += val


def _mm(a, b, mode, out_dtype, name, *, tm=None, tn=None, tk=None, a_fn=None, extra=None, epi=None):
    if mode == "tn":
        (K, M), N = a.shape, b.shape[1]
    else:
        (M, K), N = a.shape, (b.shape[0] if mode == "nt" else b.shape[1])
    tm = min(tm or (1024 if mode == "tn" else MM_TM), M)
    tn = min(tn or 1024, N)
    tk = min(tk or (TK_TOK if mode == "tn" else 1024), K)
    nm, nn, nk = M // tm, N // tn, K // tk
    assert nm * tm == M and nn * tn == N and nk * tk == K, (name, a.shape, b.shape)
    if mode == "tn":
        a_spec = pl.BlockSpec((tk, tm), lambda i, j, k: (k, i))
    else:
        a_spec = pl.BlockSpec((tm, tk), lambda i, j, k: (i, k))
    if mode == "nt":
        b_spec = pl.BlockSpec((tn, tk), lambda i, j, k: (j, k))
    else:
        b_spec = pl.BlockSpec((tk, tn), lambda i, j, k: (k, j))
    dims = {"nn": _NN, "nt": _NT, "tn": _TN}[mode]
    o_spec = pl.BlockSpec((tm, tn), lambda i, j, k: (i, j))
    has_extra = extra is not None

    def body(*refs):
        a_ref, b_ref = refs[0], refs[1]
        e_ref = refs[2] if has_extra else None
        o_ref = refs[2 + has_extra]
        av = a_ref[...]
        if a_fn is not None:
            av = a_fn(av)
        p = lax.dot_general(av, b_ref[...], dims, preferred_element_type=F32)

        def finish(acc):
            if epi is not None:
                acc = epi(acc, e_ref[...] if has_extra else None)
            o_ref[...] = acc.astype(out_dtype)

        if nk == 1:
            finish(p)
        else:
            acc_ref = refs[3 + has_extra]
            k = pl.program_id(2)
            _accum(acc_ref, k == 0, p)

            @pl.when(k == nk - 1)
            def _():
                finish(acc_ref[...])

    ins = [a, b] + ([extra] if has_extra else [])
    in_specs = [a_spec, b_spec] + ([o_spec] if has_extra else [])
    return pl.pallas_call(
        body, out_shape=jax.ShapeDtypeStruct((M, N), out_dtype), grid=(nm, nn, nk), in_specs=in_specs,
        out_specs=o_spec, scratch_shapes=[pltpu.VMEM((tm, tn), F32)] if nk > 1 else [], name=name,
        compiler_params=_cp("parallel", "parallel", "arbitrary"))(*ins)


def _relu2(u):
    r = jnp.maximum(u.astype(F32), 0.0)
    return (r * r).astype(BF16)


def _epi_add(acc, e):
    return acc + e.astype(F32)


def _epi_drelu2(acc, u):
    return acc * (2.0 * jnp.maximum(u.astype(F32), 0.0))


def _rms_fwd(x, g, name):
    T, dm = x.shape
    tm = min(TM, T)

    def body(x_ref, g_ref, h_ref):
        xv = x_ref[...]
        r = lax.rsqrt(jnp.mean(xv * xv, axis=-1, keepdims=True) + EPS)
        h_ref[...] = (xv * r * g_ref[...]).astype(BF16)

    return pl.pallas_call(
        body, out_shape=jax.ShapeDtypeStruct((T, dm), BF16), grid=(T // tm,),
        in_specs=[_rowspec(tm, dm), _full((1, dm))], out_specs=_rowspec(tm, dm), name=name,
        compiler_params=_cp("parallel"))(x, g)


def _rms_bwd(dh, x, g, resid, name):
    T, dm = x.shape
    tm = min(TM, T)

    def body(dh_ref, x_ref, g_ref, res_ref, dx_ref, dxb_ref, dg_ref):
        i = pl.program_id(0)
        xv = x_ref[...]
        r = lax.rsqrt(jnp.mean(xv * xv, axis=-1, keepdims=True) + EPS)
        y = xv * r
        dhv = dh_ref[...].astype(F32)
        dy = dhv * g_ref[...]
        dx = res_ref[...] + r * (dy - y * jnp.mean(dy * y, axis=-1, keepdims=True))
        dx_ref[...] = dx
        dxb_ref[...] = dx.astype(BF16)
        _accum(dg_ref, i == 0, jnp.sum(dhv * y, axis=0, keepdims=True))

    return pl.pallas_call(
        body,
        out_shape=(jax.ShapeDtypeStruct((T, dm), F32), jax.ShapeDtypeStruct((T, dm), BF16),
                   jax.ShapeDtypeStruct((1, dm), F32)),
        grid=(T // tm,),
        in_specs=[_rowspec(tm, dm), _rowspec(tm, dm), _full((1, dm)), _rowspec(tm, dm)],
        out_specs=(_rowspec(tm, dm), _rowspec(tm, dm), _full((1, dm))), name=name,
        compiler_params=_cp("arbitrary"))(dh, x, g, resid)


def _conv_tile(x_ref, halo_ref, w_ref, xpad, tm):
    i = pl.program_id(0)
    halo = halo_ref[...].astype(F32)[8:16]
    xpad[0:8, :] = jnp.where(i > 0, halo, 0.0)
    xpad[8:, :] = x_ref[...].astype(F32)
    w = w_ref[...]
    xc = w[0:1] * xpad[5:5 + tm, :]
    for j in range(1, CONV_TAPS):
        xc = xc + w[j:j + 1] * xpad[5 + j:5 + j + tm, :]
    return xc


def _gate_terms(ab_ref, av_ref):
    abv = ab_ref[...]
    av = av_ref[...]
    pre = abv + av[1:2]
    ea = jnp.exp(av[0:1])
    g = -ea * _softplus(pre)
    return abv, pre, ea, g


def _gdn_pre(proj, conv_w, ab, avec):
    T = proj.shape[0]
    tm = min(G1_TM, T)
    cw = 3 * BR_W

    def body(x_ref, halo_ref, w_ref, ab_ref, av_ref, q_ref, k_ref, v_ref, gb_ref, xpad):
        xc = _conv_tile(x_ref, halo_ref, w_ref, xpad, tm)
        y = xc * _sigmoid(xc)
        for h in range(N_HEAD):
            for off, ref, scale in ((0, q_ref, D_HEAD ** -0.5), (BR_W, k_ref, 1.0)):
                yh = y[:, off + h * D_HEAD:off + (h + 1) * D_HEAD]
                r = lax.rsqrt(jnp.sum(yh * yh, axis=-1, keepdims=True) + EPS)
                ref[:, _hs(h)] = yh * (r * scale)
        v_ref[...] = y[:, 2 * BR_W:]
        abv, _, _, g = _gate_terms(ab_ref, av_ref)
        lane = _iota((tm, LANES), 1)
        gb_ref[...] = jnp.where(lane < N_HEAD, g, jnp.where(lane < 2 * N_HEAD, _sigmoid(abv), 0.0))

    hb = tm // 16
    return pl.pallas_call(
        body,
        out_shape=(jax.ShapeDtypeStruct((T, BR_W), F32),) * 3 + (jax.ShapeDtypeStruct((T, LANES), F32),),
        grid=(T // tm,),
        in_specs=[_rowspec(tm, cw), pl.BlockSpec((16, cw), lambda i: (jnp.maximum(i * hb - 1, 0), 0)),
                  _full((CONV_TAPS, cw)), _rowspec(tm, LANES), _full((2, LANES))],
        out_specs=(_rowspec(tm, BR_W),) * 3 + (_rowspec(tm, LANES),),
        scratch_shapes=[pltpu.VMEM((tm + 8, cw), F32)], name="gdn_pre",
        compiler_params=_cp("parallel"))(proj, proj, conv_w, ab, avec)


def _gdn_pre_bwd(proj, conv_w, ab, avec, dq, dk, dv, dgb):
    T = proj.shape[0]
    tm = min(G1_TM, T)
    cw = 3 * BR_W

    def body(x_ref, halo_ref, w_ref, ab_ref, av_ref, dq_ref, dk_ref, dv_ref, dgb_ref,
             dxc_ref, dab_ref, dcw_ref, dav_ref, xpad):
        i = pl.program_id(0)

        @pl.when(i == 0)
        def _():
            dcw_ref[...] = jnp.zeros_like(dcw_ref)
            dav_ref[...] = jnp.zeros_like(dav_ref)

        xc_all = _conv_tile(x_ref, halo_ref, w_ref, xpad, tm)
        for s in range(cw // D_HEAD):
            cs = slice(s * D_HEAD, (s + 1) * D_HEAD)
            xc = xc_all[:, cs]
            sg = _sigmoid(xc)
            yh = xc * sg
            h = s % N_HEAD
            if s < 2 * N_HEAD:
                dref, scale = (dq_ref, D_HEAD ** -0.5) if s < N_HEAD else (dk_ref, 1.0)
                r = lax.rsqrt(jnp.sum(yh * yh, axis=-1, keepdims=True) + EPS)
                yn = yh * r
                dn = dref[:, _hs(h)]
                dy = (scale * r) * (dn - yn * jnp.sum(yn * dn, axis=-1, keepdims=True))
            else:
                dy = dv_ref[:, _hs(h)]
            dxc = dy * (sg * (1.0 + xc * (1.0 - sg)))
            dxc_ref[:, cs] = dxc.astype(BF16)
            for j in range(CONV_TAPS):
                dcw_ref[j:j + 1, cs] += jnp.sum(dxc * xpad[5 + j:5 + j + tm, cs], axis=0, keepdims=True)

        abv, pre, ea, g = _gate_terms(ab_ref, av_ref)
        dgbv = dgb_ref[...]
        lane = _iota((tm, LANES), 1)
        is_a = lane < N_HEAD
        da = jnp.where(is_a, dgbv * (-ea) * _sigmoid(pre), 0.0)
        bs = _sigmoid(abv)
        db = jnp.where(jnp.logical_and(lane >= N_HEAD, lane < 2 * N_HEAD), dgbv * bs * (1.0 - bs), 0.0)
        dab_ref[...] = (da + db).astype(BF16)
        dav_ref[0:1, :] += jnp.sum(jnp.where(is_a, dgbv * g, 0.0), axis=0, keepdims=True)
        dav_ref[1:2, :] += jnp.sum(da, axis=0, keepdims=True)

    hb = tm // 16
    return pl.pallas_call(
        body,
        out_shape=(jax.ShapeDtypeStruct((T, cw), BF16), jax.ShapeDtypeStruct((T, LANES), BF16),
                   jax.ShapeDtypeStruct((CONV_TAPS, cw), F32), jax.ShapeDtypeStruct((2, LANES), F32)),
        grid=(T // tm,),
        in_specs=[_rowspec(tm, cw), pl.BlockSpec((16, cw), lambda i: (jnp.maximum(i * hb - 1, 0), 0)),
                  _full((CONV_TAPS, cw)), _rowspec(tm, LANES), _full((2, LANES)),
                  _rowspec(tm, BR_W), _rowspec(tm, BR_W), _rowspec(tm, BR_W), _rowspec(tm, LANES)],
        out_specs=(_rowspec(tm, cw), _rowspec(tm, LANES), _full((CONV_TAPS, cw)), _full((2, LANES))),
        scratch_shapes=[pltpu.VMEM((tm + 8, cw), F32)], name="gdn_pre_bwd",
        compiler_params=_cp("arbitrary"))(proj, proj, conv_w, ab, avec, dq, dk, dv, dgb)


def _conv_bwd(dxc, conv_w):
    T, cw = dxc.shape
    tm = min(G1_TM, T)
    nt = T // tm
    hb = tm // 16

    def body(d_ref, halo_ref, w_ref, dx_ref, xpad):
        i = pl.program_id(0)
        xpad[0:tm, :] = d_ref[...].astype(F32)
        xpad[tm:, :] = jnp.where(i < nt - 1, halo_ref[...].astype(F32)[0:8], 0.0)
        w = w_ref[...]
        dx = w[3:4] * xpad[0:tm, :]
        for j in range(CONV_TAPS - 1):
            dx = dx + w[j:j + 1] * xpad[3 - j:3 - j + tm, :]
        dx_ref[...] = dx.astype(BF16)

    return pl.pallas_call(
        body, out_shape=jax.ShapeDtypeStruct((T, cw), BF16), grid=(nt,),
        in_specs=[_rowspec(tm, cw), pl.BlockSpec((16, cw), lambda i: (jnp.minimum((i + 1) * hb, T // 16 - 1), 0)),
                  _full((CONV_TAPS, cw))],
        out_specs=_rowspec(tm, cw), scratch_shapes=[pltpu.VMEM((tm + 8, cw), F32)], name="conv_bwd",
        compiler_params=_cp("parallel"))(dxc, dxc, conv_w)


def _chunk_consts():
    C = GDN_CHUNK
    row, col = _iota((C, C), 0), _iota((C, C), 1)
    return row, col, row >= col, row > col


def _chunk_decay(gbv, incl):
    c_all = _dotf(incl.astype(F32), gbv)
    c_t = jnp.concatenate([c_all, jnp.zeros_like(c_all)], axis=0).T[:, :GDN_CHUNK]
    return c_all, c_t


def _head_decay(c_all, c_t, gbv, incl, h):
    C = GDN_CHUNK
    c_col = c_all[:, h:h + 1]
    c_row = c_t[h:h + 1, :]
    gam = jnp.exp(jnp.where(incl, c_col - c_row, -1e30))
    c_last = c_all[C - 1:C, h:h + 1]
    return gam, jnp.exp(c_col), jnp.exp(c_last - c_col), jnp.exp(c_last), gbv[:, N_HEAD + h:N_HEAD + h + 1]


def _split_bf16(x):
    hi = x.astype(BF16)
    return hi, (x - hi.astype(F32)).astype(BF16)


def _dot3(a, b):
    ah, al = _split_bf16(a)
    bh, bl = _split_bf16(b)
    d = lambda u, v: lax.dot_general(u, v, _NN, preferred_element_type=F32)
    return d(ah, bh) + (d(ah, bl) + d(al, bh))


def _unit_lower_inverses(ms, row, col):
    bi, bj = row // INV_BLOCK, col // INV_BLOCK
    eye = (row == col).astype(F32)
    ns = [jnp.where(bi == bj, -m, 0.0) for m in ms]
    invs = [eye + n for n in ns]
    size = 2
    while size < INV_BLOCK:
        ns = [_dot3(n, n) for n in ns]
        invs = [inv + _dot3(inv, n) for inv, n in zip(invs, ns)]
        size *= 2
    width = 2
    while width * INV_BLOCK <= GDN_CHUNK:
        sel = jnp.logical_and(bi // width == bj // width, bi // (width // 2) > bj // (width // 2))
        ts = [_dot3(inv, jnp.where(sel, m, 0.0)) for inv, m in zip(invs, ms)]
        invs = [inv - _dot3(t, inv) for inv, t in zip(invs, ts)]
        width *= 2
    return invs


def _gdn_inv(k, gb):
    T = k.shape[0]
    C = GDN_CHUNK
    per = min(INV_CHUNKS, T // C)
    rows = per * C

    def body(k_ref, gb_ref, ti_ref):
        row, col, incl, strict = _chunk_consts()
        ms = []
        for ci in range(per):
            rs = slice(ci * C, (ci + 1) * C)
            gbv = gb_ref[rs, :]
            c_all, c_t = _chunk_decay(gbv, incl)
            for h in range(N_HEAD):
                gam, _, _, _, bcol = _head_decay(c_all, c_t, gbv, incl, h)
                K = k_ref[rs, _hs(h)]
                ms.append(jnp.where(strict, _dot_nt(K * bcol, K) * gam, 0.0))
        for i, inv in enumerate(_unit_lower_inverses(ms, row, col)):
            ti_ref[i // N_HEAD, i % N_HEAD] = inv

    return pl.pallas_call(
        body, out_shape=jax.ShapeDtypeStruct((T // C, N_HEAD, C, C), F32), grid=(T // rows,),
        in_specs=[_rowspec(rows, BR_W), _rowspec(rows, LANES)],
        out_specs=pl.BlockSpec((per, N_HEAD, C, C), lambda i: (i, 0, 0, 0)), name="gdn_inv",
        compiler_params=_cp("parallel"))(k, gb)


def _gdn_fwd(q, k, v, gb, proj, gnorm, tinv_all):
    T = q.shape[0]
    C = GDN_CHUNK
    nc = T // C
    per = min(GDN_STEP_CHUNKS, nc)
    zcol = Z_OFF // BR_W
    heads = range(N_HEAD)

    def body(q_ref, k_ref, v_ref, gb_ref, z_ref, gn_ref, ti_ref, og_ref, oraw_ref, sh_ref, vn_ref, s_ref):
        @pl.when(pl.program_id(0) == 0)
        def _():
            s_ref[...] = jnp.zeros_like(s_ref)

        _, _, incl, _ = _chunk_consts()
        S = [s_ref[h] for h in heads]
        for ci in range(per):
            rs = slice(ci * C, (ci + 1) * C)
            gbv = gb_ref[rs, :]
            c_all, c_t = _chunk_decay(gbv, incl)
            dec = [_head_decay(c_all, c_t, gbv, incl, h) for h in heads]
            gam, gcol, dcol, glast, bcol = ([d[i] for d in dec] for i in range(5))
            Q = [q_ref[rs, _hs(h)] for h in heads]
            K = [k_ref[rs, _hs(h)] for h in heads]
            V = [v_ref[rs, _hs(h)] for h in heads]
            Sb = [s.astype(BF16) for s in S]
            KS = [_dot(K[h], Sb[h]) for h in heads]
            QS = [_dot(Q[h], Sb[h]) for h in heads]
            P = [_dot_nt(Q[h], K[h]) * gam[h] for h in heads]
            R = [bcol[h] * (V[h] - gcol[h] * KS[h]) for h in heads]
            vn = [_dot(ti_ref[ci, h], R[h]) for h in heads]
            O = [gcol[h] * QS[h] + _dot(P[h], vn[h]) for h in heads]
            Sn = [glast[h] * S[h] + _dot_tn(K[h] * dcol[h], vn[h]) for h in heads]
            for h in heads:
                sh_ref[ci, h] = S[h]
                vn_ref[rs, _hs(h)] = vn[h]
                oraw_ref[rs, _hs(h)] = O[h]
                rr = lax.rsqrt(jnp.mean(O[h] * O[h], axis=-1, keepdims=True) + EPS)
                zz = z_ref[rs, _hs(h)].astype(F32)
                og_ref[rs, _hs(h)] = (O[h] * rr * gn_ref[...] * (zz * _sigmoid(zz))).astype(BF16)
            S = Sn
        for h in heads:
            s_ref[h] = S[h]

    cspec = lambda w, cb=0: pl.BlockSpec((per * C, w), lambda n: (n, cb))
    hist = lambda a, b: pl.BlockSpec((per, N_HEAD, a, b), lambda n: (n, 0, 0, 0))
    return pl.pallas_call(
        body,
        out_shape=(jax.ShapeDtypeStruct((T, BR_W), BF16), jax.ShapeDtypeStruct((T, BR_W), F32),
                   jax.ShapeDtypeStruct((nc, N_HEAD, D_HEAD, D_HEAD), F32), jax.ShapeDtypeStruct((T, BR_W), F32)),
        grid=(nc // per,),
        in_specs=[cspec(BR_W), cspec(BR_W), cspec(BR_W), cspec(LANES), cspec(BR_W, zcol), _full((1, D_HEAD)),
                  hist(C, C)],
        out_specs=(cspec(BR_W), cspec(BR_W), hist(D_HEAD, D_HEAD), cspec(BR_W)),
        scratch_shapes=[pltpu.VMEM((N_HEAD, D_HEAD, D_HEAD), F32)], name="gdn_chunk_fwd",
        compiler_params=_cp("arbitrary"))(q, k, v, gb, proj, gnorm, tinv_all)


def _gdn_bwd(q, k, v, gb, proj, gnorm, oraw, shist, tinv_all, vn_all, dog):
    T = q.shape[0]
    C = GDN_CHUNK
    nc = T // C
    per = min(GDN_BWD_STEP_CHUNKS, nc)
    zcol = Z_OFF // BR_W

    def body(q_ref, k_ref, v_ref, gb_ref, z_ref, gn_ref, oraw_ref, sh_ref, ti_ref, vn_ref, dog_ref,
             dq_ref, dk_ref, dv_ref, dgb_ref, dz_ref, dgn_ref, ds_ref):
        @pl.when(pl.program_id(0) == 0)
        def _():
            ds_ref[...] = jnp.zeros_like(ds_ref)
            dgn_ref[...] = jnp.zeros_like(dgn_ref)

        row, col, incl, strict = _chunk_consts()
        lane = _iota((C, LANES), 1)
        rowl = _iota((C, LANES), 0)
        ones = jnp.ones((C, LANES), F32)
        upper = (col >= row).astype(F32)
        gn = gn_ref[...]
        heads = range(N_HEAD)
        rsum = lambda a: jnp.sum(a, axis=-1, keepdims=True)
        dgn = jnp.zeros((1, D_HEAD), F32)
        dSn = [ds_ref[h] for h in heads]
        for ci in reversed(range(per)):
            rs = slice(ci * C, (ci + 1) * C)
            gbv = gb_ref[rs, :]
            c_all, c_t = _chunk_decay(gbv, incl)
            dec = [_head_decay(c_all, c_t, gbv, incl, h) for h in heads]
            gam, gcol, dcol, glast, bcol = ([d[i] for d in dec] for i in range(5))
            Q = [q_ref[rs, _hs(h)] for h in heads]
            K = [k_ref[rs, _hs(h)] for h in heads]
            V = [v_ref[rs, _hs(h)] for h in heads]
            dO = []
            for h in heads:
                O = oraw_ref[rs, _hs(h)]
                zz = z_ref[rs, _hs(h)].astype(F32)
                dogv = dog_ref[rs, _hs(h)].astype(F32)
                rr = lax.rsqrt(jnp.mean(O * O, axis=-1, keepdims=True) + EPS)
                on = O * rr
                sg = _sigmoid(zz)
                dz_ref[rs, _hs(h)] = (dogv * on * gn * (sg * (1.0 + zz * (1.0 - sg)))).astype(BF16)
                dyn = dogv * (zz * sg)
                dgn = dgn + jnp.sum(dyn * on, axis=0, keepdims=True)
                dyv = dyn * gn
                dO.append((rr * (dyv - on * jnp.mean(dyv * on, axis=-1, keepdims=True))).astype(BF16))
            S = [sh_ref[ci, h] for h in heads]
            Sb = [s.astype(BF16) for s in S]
            tinv = [ti_ref[ci, h].astype(BF16) for h in heads]
            vn = [vn_ref[rs, _hs(h)] for h in heads]
            vnb = [a.astype(BF16) for a in vn]
            dSb = [a.astype(BF16) for a in dSn]
            Kb = [K[h] * bcol[h] for h in heads]
            M = [jnp.where(strict, _dot_nt(Kb[h], K[h]) * gam[h], 0.0) for h in heads]
            P = [_dot_nt(Q[h], K[h]) * gam[h] for h in heads]
            KS = [_dot(K[h], Sb[h]) for h in heads]
            QS = [_dot(Q[h], Sb[h]) for h in heads]
            dvn = [_dot_tn(P[h], dO[h]) + _dot(K[h] * dcol[h], dSb[h]) for h in heads]
            dR = [_dot_tn(tinv[h], dvn[h]) for h in heads]
            dRb = [a.astype(BF16) for a in dR]
            bg = [bcol[h] * gcol[h] for h in heads]
            dS_new = [glast[h] * dSn[h] + _dot_tn(gcol[h] * Q[h], dO[h]) - _dot_tn(bg[h] * K[h], dRb[h])
                      for h in heads]
            dP = [jnp.where(incl, _dot_nt(dO[h], vnb[h]), 0.0) for h in heads]
            dM = [jnp.where(strict, -_dot_nt(dRb[h], vnb[h]), 0.0) for h in heads]
            dPG = [(dP[h] * gam[h]).astype(BF16) for h in heads]
            dMG = [(dM[h] * gam[h]).astype(BF16) for h in heads]
            E = [_dot_nt(vnb[h], dSb[h]) for h in heads]
            dKb = [_dot(dMG[h], K[h]) for h in heads]
            dc_all = jnp.zeros((C, LANES), F32)
            db_all = jnp.zeros((C, LANES), F32)
            for h in heads:
                dq_ref[rs, _hs(h)] = gcol[h] * _dot_nt(dO[h], Sb[h]) + _dot(dPG[h], K[h])
                dk_ref[rs, _hs(h)] = (_dot_tn(dPG[h], Q[h]) + _dot_tn(dMG[h], Kb[h]) + bcol[h] * dKb[h]
                                      - bg[h] * _dot_nt(dRb[h], Sb[h]) + dcol[h] * E[h])
                dv_ref[rs, _hs(h)] = bcol[h] * dR[h]
                dbeta = rsum(dKb[h] * K[h]) + rsum(dR[h] * (V[h] - gcol[h] * KS[h]))
                X = dP[h] * P[h] + dM[h] * M[h]
                ddel = rsum(K[h] * E[h]) * dcol[h]
                colsum = _dotf(X, ones, _TN)[:, 0:1]
                dc = (rsum(X) - colsum + gcol[h] * rsum(dO[h].astype(F32) * QS[h]) - bg[h] * rsum(dR[h] * KS[h])
                      - ddel)
                last = (jnp.sum(ddel, axis=0, keepdims=True)
                        + glast[h] * jnp.sum(rsum(dSn[h] * S[h]), axis=0, keepdims=True))
                dc_all = dc_all + jnp.where(lane == h, dc + jnp.where(rowl == C - 1, last, 0.0), 0.0)
                db_all = db_all + jnp.where(lane == N_HEAD + h, dbeta, 0.0)
            dgb_ref[rs, :] = _dotf(upper, dc_all) + db_all
            dSn = dS_new
        for h in heads:
            ds_ref[h] = dSn[h]
        dgn_ref[...] += dgn

    nb = nc // per
    cspec = lambda w, cb=0: pl.BlockSpec((per * C, w), lambda n: (nb - 1 - n, cb))
    hist = lambda a, b: pl.BlockSpec((per, N_HEAD, a, b), lambda n: (nb - 1 - n, 0, 0, 0))
    return pl.pallas_call(
        body,
        out_shape=(jax.ShapeDtypeStruct((T, BR_W), F32),) * 3 + (
            jax.ShapeDtypeStruct((T, LANES), F32), jax.ShapeDtypeStruct((T, BR_W), BF16),
            jax.ShapeDtypeStruct((1, D_HEAD), F32)),
        grid=(nb,),
        in_specs=[cspec(BR_W), cspec(BR_W), cspec(BR_W), cspec(LANES), cspec(BR_W, zcol), _full((1, D_HEAD)),
                  cspec(BR_W), hist(D_HEAD, D_HEAD), hist(C, C), cspec(BR_W), cspec(BR_W)],
        out_specs=(cspec(BR_W), cspec(BR_W), cspec(BR_W), cspec(LANES), cspec(BR_W), _full((1, D_HEAD))),
        scratch_shapes=[pltpu.VMEM((N_HEAD, D_HEAD, D_HEAD), F32)], name="gdn_chunk_bwd",
        compiler_params=_cp("arbitrary"))(q, k, v, gb, proj, gnorm, oraw, shist, tinv_all, vn_all, dog)


SB_COL = SB_OFF // BR_W
SB_SCALE = D_HEAD ** -0.5


def _sb_pre(proj, gq, gk):
    T = proj.shape[0]
    tm = min(TM, T)

    def body(xq_ref, xk_ref, xv_ref, gq_ref, gk_ref, q_ref, k_ref, v_ref):
        for h in range(N_HEAD):
            for x_ref, g_ref, ref, scale in ((xq_ref, gq_ref, q_ref, SB_SCALE), (xk_ref, gk_ref, k_ref, 1.0)):
                xh = x_ref[:, _hs(h)].astype(F32)
                r = lax.rsqrt(jnp.mean(xh * xh, axis=-1, keepdims=True) + EPS)
                ref[:, _hs(h)] = (xh * (r * scale) * g_ref[...]).astype(BF16)
        v_ref[...] = xv_ref[...]

    return pl.pallas_call(
        body, out_shape=(jax.ShapeDtypeStruct((T, BR_W), BF16),) * 3, grid=(T // tm,),
        in_specs=[_rowspec(tm, BR_W, SB_COL), _rowspec(tm, BR_W, SB_COL + 1), _rowspec(tm, BR_W, SB_COL + 2),
                  _full((1, D_HEAD)), _full((1, D_HEAD))],
        out_specs=(_rowspec(tm, BR_W),) * 3, name="sb_pre", compiler_params=_cp("parallel"))(proj, proj, proj, gq, gk)


def _sb_pre_bwd(proj, gq, gk, dq, dk, dv):
    T = proj.shape[0]
    tm = min(TM, T)

    def body(xq_ref, xk_ref, gq_ref, gk_ref, dq_ref, dk_ref, dv_ref, dx_ref, dgq_ref, dgk_ref):
        i = pl.program_id(0)

        @pl.when(i == 0)
        def _():
            dgq_ref[...] = jnp.zeros_like(dgq_ref)
            dgk_ref[...] = jnp.zeros_like(dgk_ref)

        for off, x_ref, g_ref, d_ref, dg_ref, scale in ((0, xq_ref, gq_ref, dq_ref, dgq_ref, SB_SCALE),
                                                        (BR_W, xk_ref, gk_ref, dk_ref, dgk_ref, 1.0)):
            dg = jnp.zeros((1, D_HEAD), F32)
            for h in range(N_HEAD):
                xh = x_ref[:, _hs(h)].astype(F32)
                r = lax.rsqrt(jnp.mean(xh * xh, axis=-1, keepdims=True) + EPS)
                y = xh * r
                dn = d_ref[:, _hs(h)] * scale
                dg = dg + jnp.sum(dn * y, axis=0, keepdims=True)
                dy = dn * g_ref[...]
                dx_ref[:, off + h * D_HEAD:off + (h + 1) * D_HEAD] = (
                    r * (dy - y * jnp.mean(dy * y, axis=-1, keepdims=True))).astype(BF16)
            dg_ref[...] += dg
        dx_ref[:, 2 * BR_W:] = dv_ref[...].astype(BF16)

    return pl.pallas_call(
        body,
        out_shape=(jax.ShapeDtypeStruct((T, 3 * BR_W), BF16), jax.ShapeDtypeStruct((1, D_HEAD), F32),
                   jax.ShapeDtypeStruct((1, D_HEAD), F32)),
        grid=(T // tm,),
        in_specs=[_rowspec(tm, BR_W, SB_COL), _rowspec(tm, BR_W, SB_COL + 1), _full((1, D_HEAD)), _full((1, D_HEAD)),
                  _rowspec(tm, BR_W), _rowspec(tm, BR_W), _rowspec(tm, BR_W)],
        out_specs=(_rowspec(tm, 3 * BR_W), _full((1, D_HEAD)), _full((1, D_HEAD))), name="sb_pre_bwd",
        compiler_params=_cp("arbitrary"))(proj, proj, gq, gk, dq, dk, dv)


def _sb_scores(q_ref, k_ref, lead):
    z = _dot_nt(q_ref[...], k_ref[...])
    zc = jnp.minimum(z, 30.0)
    sp = jnp.log(1.0 + jnp.exp(zc)) + (z - zc)
    if lead is None:
        return z, sp, None
    mask = _iota(z.shape, 1) - _iota(z.shape, 0) < -lead
    return z, jnp.where(mask, sp, 0.0), mask


def _sb_blocks(T):
    bq = min(SB_BLK_Q, T)
    bk = min(SB_BLK, bq)
    return bq, bk, min(SB_W, bk), T // bq, bq // bk


def _sb_fwd(sq, sk, sv):
    T = sq.shape[0]
    bq, bk, w, nq, ratio = _sb_blocks(T)
    nsub = bk // w

    def body(qi_ref, kj_ref, q_ref, k_ref, v_ref, o_ref, lt_ref, acc_ref, r_ref):
        t = pl.program_id(1)
        qi, kj = qi_ref[t], kj_ref[t]

        @pl.when(kj == ratio * qi + ratio - 1)
        def _():
            acc_ref[...] = jnp.zeros_like(acc_ref)
            r_ref[...] = jnp.zeros_like(r_ref)

        def block(masked):
            z, sp, mask = _sb_scores(q_ref, k_ref, kj * bk - qi * bq if masked else None)
            after = (_iota((w, w), 0) > _iota((w, w), 1)).astype(BF16)
            r = r_ref[...]
            acc = acc_ref[...]
            for sb in reversed(range(nsub)):
                cs = slice(sb * w, (sb + 1) * w)
                sps = sp[:, cs]
                a = jnp.exp(z[:, cs] - sps - _dot(sps, after) - r)
                if masked:
                    a = jnp.where(mask[:, cs], a, 0.0)
                acc = acc + _dot(a, v_ref[cs, :])
                r = r + jnp.sum(sps, axis=-1, keepdims=True)
            acc_ref[...] = acc
            r_ref[...] = r

        pl.when(kj >= ratio * qi)(functools.partial(block, True))
        pl.when(kj < ratio * qi)(functools.partial(block, False))

        @pl.when(kj == 0)
        def _():
            o_ref[...] = acc_ref[...].astype(BF16)
            lt_ref[0] = r_ref[...]

    pairs = [(i, j) for i in range(nq) for j in range(ratio * (i + 1) - 1, -1, -1)]
    qi_tab, kj_tab = (jnp.asarray(np.array(c, np.int32)) for c in zip(*pairs))
    qspec = pl.BlockSpec((bq, D_HEAD), lambda h, t, qi, kj: (qi[t], h))
    kspec = pl.BlockSpec((bk, D_HEAD), lambda h, t, qi, kj: (kj[t], h))
    return pl.pallas_call(
        body,
        out_shape=(jax.ShapeDtypeStruct((T, BR_W), BF16), jax.ShapeDtypeStruct((N_HEAD, T, 1), F32)),
        grid_spec=pltpu.PrefetchScalarGridSpec(
            num_scalar_prefetch=2, grid=(N_HEAD, len(pairs)), in_specs=[qspec, kspec, kspec],
            out_specs=(qspec, pl.BlockSpec((1, bq, 1), lambda h, t, qi, kj: (h, qi[t], 0))),
            scratch_shapes=[pltpu.VMEM((bq, D_HEAD), F32), pltpu.VMEM((bq, 1), F32)]),
        name="sb_fwd", compiler_params=_cp("parallel", "arbitrary"))(qi_tab, kj_tab, sq, sk, sv)


def _sb_bwd(sq, sk, sv, ltot, do):
    T = sq.shape[0]
    bq, bk, w, nq, ratio = _sb_blocks(T)
    nsub = bk // w

    def body(qi_ref, kj_ref, q_ref, k_ref, v_ref, lt_ref, do_ref, dq_ref, dk_ref, dv_ref, acc_ref, p_ref, g_ref):
        t = pl.program_id(1)
        qi, kj = qi_ref[t], kj_ref[t]

        @pl.when(t == 0)
        def _():
            dk_ref[...] = jnp.zeros_like(dk_ref)
            dv_ref[...] = jnp.zeros_like(dv_ref)

        @pl.when(kj == 0)
        def _():
            acc_ref[...] = jnp.zeros_like(acc_ref)
            p_ref[...] = lt_ref[0]
            g_ref[...] = jnp.zeros_like(g_ref)

        def block(masked):
            z, sp, mask = _sb_scores(q_ref, k_ref, kj * bk - qi * bq if masked else None)
            d_a = _dot_nt(do_ref[...], v_ref[...])
            after = (_iota((w, w), 0) > _iota((w, w), 1)).astype(BF16)
            before = (_iota((w, w), 0) < _iota((w, w), 1)).astype(BF16)
            rest = p_ref[...]
            hg = g_ref[...]
            acc = acc_ref[...]
            base = pl.multiple_of(kj * bk, bk)
            for sb in range(nsub):
                cs = slice(sb * w, (sb + 1) * w)
                sps, zs = sp[:, cs], z[:, cs]
                rest = rest - jnp.sum(sps, axis=-1, keepdims=True)
                a = jnp.exp(zs - sps - _dot(sps, after) - rest)
                if masked:
                    a = jnp.where(mask[:, cs], a, 0.0)
                g = a * d_a[:, cs]
                sig = jnp.exp(zs - sps)
                dz = g - sig * (g + (hg + _dot(g, before)))
                if masked:
                    dz = jnp.where(mask[:, cs], dz, 0.0)
                dz = dz.astype(BF16)
                rows = pl.ds(base + sb * w, w)
                dv_ref[rows, :] += _dot_tn(a, do_ref[...])
                dk_ref[rows, :] += _dot_tn(dz, q_ref[...])
                acc = acc + _dot(dz, k_ref[cs, :])
                hg = hg + jnp.sum(g, axis=-1, keepdims=True)
            acc_ref[...] = acc
            p_ref[...] = rest
            g_ref[...] = hg

        pl.when(kj >= ratio * qi)(functools.partial(block, True))
        pl.when(kj < ratio * qi)(functools.partial(block, False))

        @pl.when(kj == ratio * qi + ratio - 1)
        def _():
            dq_ref[...] = acc_ref[...]

    pairs = [(i, j) for i in range(nq) for j in range(ratio * (i + 1))]
    qi_tab, kj_tab = (jnp.asarray(np.array(c, np.int32)) for c in zip(*pairs))
    qspec = pl.BlockSpec((bq, D_HEAD), lambda h, t, qi, kj: (qi[t], h))
    kspec = pl.BlockSpec((bk, D_HEAD), lambda h, t, qi, kj: (kj[t], h))
    full = pl.BlockSpec((T, D_HEAD), lambda h, t, qi, kj: (0, h))
    return pl.pallas_call(
        body, out_shape=(jax.ShapeDtypeStruct((T, BR_W), F32),) * 3,
        grid_spec=pltpu.PrefetchScalarGridSpec(
            num_scalar_prefetch=2, grid=(N_HEAD, len(pairs)),
            in_specs=[qspec, kspec, kspec, pl.BlockSpec((1, bq, 1), lambda h, t, qi, kj: (h, qi[t], 0)), qspec],
            out_specs=(qspec, full, full),
            scratch_shapes=[pltpu.VMEM((bq, D_HEAD), F32), pltpu.VMEM((bq, 1), F32), pltpu.VMEM((bq, 1), F32)]),
        name="sb_bwd", compiler_params=_cp("arbitrary", "arbitrary"))(qi_tab, kj_tab, sq, sk, sv, ltot, do)


def _mem_kv(mem, gm, w_kv, gk):
    def body(mem_ref, gm_ref, w_ref, gk_ref, mn_ref, kv_ref, kh_ref, vm_ref):
        mv = mem_ref[...]
        r = lax.rsqrt(jnp.mean(mv * mv, axis=-1, keepdims=True) + EPS)
        mn = (mv * r * gm_ref[...]).astype(BF16)
        mn_ref[...] = mn
        kv = lax.dot_general(mn, w_ref[...], _NN, preferred_element_type=F32)
        kv_ref[...] = kv
        for h in range(N_HEAD):
            kh = kv[:, _hs(h)]
            rk = lax.rsqrt(jnp.mean(kh * kh, axis=-1, keepdims=True) + EPS)
            kh_ref[:, _hs(h)] = (kh * rk * gk_ref[...]).astype(BF16)
        vm_ref[...] = kv[:, BR_W:].astype(BF16)

    return pl.pallas_call(
        body,
        out_shape=(jax.ShapeDtypeStruct((N_MEM, D_MODEL), BF16), jax.ShapeDtypeStruct((N_MEM, 2 * BR_W), F32),
                   jax.ShapeDtypeStruct((N_MEM, BR_W), BF16), jax.ShapeDtypeStruct((N_MEM, BR_W), BF16)),
        name="mem_kv", compiler_params=_cp())(mem, gm, w_kv, gk)


def _mem_q(x_ref, gq_ref, h):
    xh = x_ref[:, _hs(h)].astype(F32)
    r = lax.rsqrt(jnp.mean(xh * xh, axis=-1, keepdims=True) + EPS)
    return r, xh * r


def _mem_probs(qn, kh):
    s = _dot_nt(qn, kh) * (D_HEAD ** -0.5)
    e = jnp.exp(s - jnp.max(s, axis=-1, keepdims=True))
    return e / jnp.sum(e, axis=-1, keepdims=True)


def _mem_fwd(proj, kh, vm, gq):
    T = proj.shape[0]
    tm = min(TM, T)

    def body(x_ref, kh_ref, vm_ref, gq_ref, o_ref):
        for h in range(N_HEAD):
            _, y = _mem_q(x_ref, gq_ref, h)
            p = _mem_probs((y * gq_ref[...]).astype(BF16), kh_ref[:, _hs(h)])
            o_ref[:, _hs(h)] = _dot(p, vm_ref[:, _hs(h)]).astype(BF16)

    return pl.pallas_call(
        body, out_shape=jax.ShapeDtypeStruct((T, BR_W), BF16), grid=(T // tm,),
        in_specs=[_rowspec(tm, BR_W, MEMQ_OFF // BR_W), _full((N_MEM, BR_W)), _full((N_MEM, BR_W)),
                  _full((1, D_HEAD))],
        out_specs=_rowspec(tm, BR_W), name="mem_fwd", compiler_params=_cp("parallel"))(proj, kh, vm, gq)


def _mem_bwd(proj, kh, vm, gq, do):
    T = proj.shape[0]
    tm = min(TM, T)

    def body(x_ref, kh_ref, vm_ref, gq_ref, do_ref, dx_ref, dkh_ref, dvm_ref, dgq_ref):
        i = pl.program_id(0)

        @pl.when(i == 0)
        def _():
            dkh_ref[...] = jnp.zeros_like(dkh_ref)
            dvm_ref[...] = jnp.zeros_like(dvm_ref)
            dgq_ref[...] = jnp.zeros_like(dgq_ref)

        dg = jnp.zeros((1, D_HEAD), F32)
        for h in range(N_HEAD):
            r, y = _mem_q(x_ref, gq_ref, h)
            qn = (y * gq_ref[...]).astype(BF16)
            p = _mem_probs(qn, kh_ref[:, _hs(h)])
            dov = do_ref[:, _hs(h)]
            dp = _dot_nt(dov, vm_ref[:, _hs(h)])
            ds = p * (dp - jnp.sum(dp * p, axis=-1, keepdims=True)) * (D_HEAD ** -0.5)
            dqn = _dot(ds, kh_ref[:, _hs(h)])
            dkh_ref[:, _hs(h)] += _dot_tn(ds, qn)
            dvm_ref[:, _hs(h)] += _dot_tn(p, dov)
            dg = dg + jnp.sum(dqn * y, axis=0, keepdims=True)
            dy = dqn * gq_ref[...]
            dx_ref[:, _hs(h)] = (r * (dy - y * jnp.mean(dy * y, axis=-1, keepdims=True))).astype(BF16)
        dgq_ref[...] += dg

    return pl.pallas_call(
        body,
        out_shape=(jax.ShapeDtypeStruct((T, BR_W), BF16), jax.ShapeDtypeStruct((N_MEM, BR_W), F32),
                   jax.ShapeDtypeStruct((N_MEM, BR_W), F32), jax.ShapeDtypeStruct((1, D_HEAD), F32)),
        grid=(T // tm,),
        in_specs=[_rowspec(tm, BR_W, MEMQ_OFF // BR_W), _full((N_MEM, BR_W)), _full((N_MEM, BR_W)),
                  _full((1, D_HEAD)), _rowspec(tm, BR_W)],
        out_specs=(_rowspec(tm, BR_W), _full((N_MEM, BR_W)), _full((N_MEM, BR_W)), _full((1, D_HEAD))),
        name="mem_bwd", compiler_params=_cp("arbitrary"))(proj, kh, vm, gq, do)


def _mem_kv_bwd(mem, gm, w_kv, gk, kv, mn, dkh, dvm):
    def body(mem_ref, gm_ref, w_ref, gk_ref, kv_ref, mn_ref, dkh_ref, dvm_ref, dw_ref, dgm_ref, dgk_ref, dkv_ref):
        dgk = jnp.zeros((1, D_HEAD), F32)
        for h in range(N_HEAD):
            kh = kv_ref[:, _hs(h)]
            r = lax.rsqrt(jnp.mean(kh * kh, axis=-1, keepdims=True) + EPS)
            y = kh * r
            dn = dkh_ref[:, _hs(h)]
            dgk = dgk + jnp.sum(dn * y, axis=0, keepdims=True)
            dy = dn * gk_ref[...]
            dkv_ref[:, _hs(h)] = (r * (dy - y * jnp.mean(dy * y, axis=-1, keepdims=True))).astype(BF16)
        dkv_ref[:, BR_W:] = dvm_ref[...].astype(BF16)
        dgk_ref[...] = dgk
        dkv = dkv_ref[...]
        dw_ref[...] = lax.dot_general(mn_ref[...], dkv, _TN, preferred_element_type=F32)
        dmn = lax.dot_general(dkv, w_ref[...], _NT, preferred_element_type=F32)
        mv = mem_ref[...]
        memn = mv * lax.rsqrt(jnp.mean(mv * mv, axis=-1, keepdims=True) + EPS)
        dgm_ref[...] = jnp.sum(dmn * memn, axis=0, keepdims=True)

    return pl.pallas_call(
        body,
        out_shape=(jax.ShapeDtypeStruct((D_MODEL, 2 * BR_W), F32), jax.ShapeDtypeStruct((1, D_MODEL), F32),
                   jax.ShapeDtypeStruct((1, D_HEAD), F32)),
        scratch_shapes=[pltpu.VMEM((N_MEM, 2 * BR_W), BF16)], name="mem_kv_bwd",
        compiler_params=_cp())(mem, gm, w_kv, gk, kv, mn, dkh, dvm)


def _merge_fwd(og, osb, om, proj, wg, ws, wm):
    T = og.shape[0]
    tm = min(TM, T)

    def body(og_ref, os_ref, om_ref, g0, g1, g2, wg_ref, ws_ref, wm_ref, mix_ref, yg_ref, ys_ref, ym_ref):
        mix = jnp.zeros((tm, D_MODEL), F32)
        for o_ref, gl_ref, w_ref, y_ref in ((og_ref, g0, wg_ref, yg_ref), (os_ref, g1, ws_ref, ys_ref),
                                            (om_ref, g2, wm_ref, ym_ref)):
            y = lax.dot_general(o_ref[...], w_ref[...], _NN, preferred_element_type=F32)
            y_ref[...] = y.astype(BF16)
            mix = mix + _sigmoid(gl_ref[...].astype(F32)) * y
        mix_ref[...] = mix.astype(BF16)

    br = _rowspec(tm, BR_W)
    wspec = _full((BR_W, D_MODEL))
    out = _rowspec(tm, D_MODEL)
    gates = [_rowspec(tm, D_MODEL, GATE_COL + b) for b in range(3)]
    return pl.pallas_call(
        body, out_shape=(jax.ShapeDtypeStruct((T, D_MODEL), BF16),) * 4, grid=(T // tm,),
        in_specs=[br, br, br, *gates, wspec, wspec, wspec],
        out_specs=(out,) * 4, name="merge_fwd",
        compiler_params=_cp("parallel"))(og, osb, om, proj, proj, proj, wg, ws, wm)


def _merge_bwd(dmix, proj, ys, os_, ws):
    T = dmix.shape[0]
    tm = min(TM, T)

    def body(dmix_ref, g0, g1, g2, y0, y1, y2, o0, o1, o2, w0, w1, w2, dgl_ref, do0, do1, do2, dw0, dw1, dw2):
        i = pl.program_id(0)
        dm = dmix_ref[...].astype(F32)
        for b, (gl_ref, y_ref, o_ref, w_ref, do_ref, dw_ref) in enumerate((
                (g0, y0, o0, w0, do0, dw0), (g1, y1, o1, w1, do1, dw1), (g2, y2, o2, w2, do2, dw2))):
            gate = _sigmoid(gl_ref[...].astype(F32))
            dgl_ref[:, b * D_MODEL:(b + 1) * D_MODEL] = (dm * y_ref[...].astype(F32) * gate * (1.0 - gate)).astype(BF16)
            dy = (gate * dm).astype(BF16)
            do_ref[...] = lax.dot_general(dy, w_ref[...], _NT, preferred_element_type=F32).astype(BF16)
            _accum(dw_ref, i == 0, lax.dot_general(dy, o_ref[...], _TN, preferred_element_type=F32))

    br = _rowspec(tm, BR_W)
    wide = _rowspec(tm, D_MODEL)
    wspec = _full((BR_W, D_MODEL))
    wtspec = _full((D_MODEL, BR_W))
    gates = [_rowspec(tm, D_MODEL, GATE_COL + b) for b in range(3)]
    return pl.pallas_call(
        body,
        out_shape=(jax.ShapeDtypeStruct((T, 3 * D_MODEL), BF16),) + (jax.ShapeDtypeStruct((T, BR_W), BF16),) * 3
        + (jax.ShapeDtypeStruct((D_MODEL, BR_W), F32),) * 3,
        grid=(T // tm,),
        in_specs=[wide, *gates, wide, wide, wide, br, br, br, wspec, wspec, wspec],
        out_specs=(_rowspec(tm, 3 * D_MODEL), br, br, br, wtspec, wtspec, wtspec), name="merge_bwd",
        compiler_params=_cp("arbitrary"))(dmix, proj, proj, proj, *ys, *os_, *ws)


def _loss(y, tgt):
    T, dm = y.shape
    tm = min(TM, T)

    def body(y_ref, t_ref, dy_ref, dyb_ref, sq_ref):
        err = y_ref[...] - t_ref[...]
        dy = err * (1.0 / dm)
        dy_ref[...] = dy
        dyb_ref[...] = dy.astype(BF16)
        _accum(sq_ref, pl.program_id(0) == 0, jnp.sum(err * err, axis=0, keepdims=True))

    return pl.pallas_call(
        body,
        out_shape=(jax.ShapeDtypeStruct((T, dm), F32), jax.ShapeDtypeStruct((T, dm), BF16),
                   jax.ShapeDtypeStruct((1, dm), F32)),
        grid=(T // tm,), in_specs=[_rowspec(tm, dm), _rowspec(tm, dm)],
        out_specs=(_rowspec(tm, dm), _rowspec(tm, dm), _full((1, dm))), name="loss",
        compiler_params=_cp("arbitrary"))(y, tgt)


def _local_step(x, mem, tgt, W, P):
    w_in = W["w_in"]
    w_main = jnp.concatenate([w_in[:, :SB_OFF], w_in[:, SB_OFF + 8:]], axis=1)
    w_ab = jnp.pad(w_in[:, SB_OFF:SB_OFF + 8], ((0, 0), (0, LANES - 8)))
    avec = jnp.pad(jnp.concatenate([P["a_log"], P["dt_bias"]], axis=0), ((0, 0), (0, LANES - N_HEAD)))
    wbr = (W["w_br_gdn"], W["w_br_sb"], W["w_br_mem"])

    h = _rms_fwd(x, P["norm1_g"], "rms1")
    proj = _mm(h, w_main, "nn", BF16, "in_proj")
    ab = _mm(h, w_ab, "nn", F32, "in_proj_ab")
    q, k, v, gb = _gdn_pre(proj, P["conv_w"], ab, avec)
    tinv = _gdn_inv(k, gb)
    og, oraw, shist, vn = _gdn_fwd(q, k, v, gb, proj, P["gdn_norm_g"], tinv)
    sq, sk, sv = _sb_pre(proj, P["sb_q_norm_g"], P["sb_k_norm_g"])
    osb, ltot = _sb_fwd(sq, sk, sv)
    mn, kv, kh, vm = _mem_kv(mem, P["mem_norm_g"], W["w_mem_kv"], P["mem_k_norm_g"])
    om = _mem_fwd(proj, kh, vm, P["mem_q_norm_g"])
    mix, yg, ys, ym = _merge_fwd(og, osb, om, proj, *wbr)
    x1 = _mm(mix, W["w_o"], "nn", F32, "out_proj", extra=x, epi=_epi_add)
    h2 = _rms_fwd(x1, P["norm2_g"], "rms2")
    u = _mm(h2, W["w_up"], "nn", BF16, "mlp_up")
    y = _mm(u, W["w_down"], "nn", F32, "mlp_down", a_fn=_relu2, extra=x1, epi=_epi_add)
    dy, dyb, sq_err = _loss(y, tgt)

    G = {}
    du = _mm(dyb, W["w_down"], "nt", BF16, "d_mlp_act", extra=u, epi=_epi_drelu2)
    G["w_down"] = _mm(u, dyb, "tn", F32, "dw_down", a_fn=_relu2)
    G["w_up"] = _mm(du, h2, "tn", F32, "dw_up")
    dh2 = _mm(du, W["w_up"], "nt", F32, "d_h2")
    dx1, dx1b, G["norm2_g"] = _rms_bwd(dh2, x1, P["norm2_g"], dy, "rms2_bwd")
    dmix = _mm(dx1b, W["w_o"], "nt", BF16, "d_mix")
    G["w_o"] = _mm(mix, dx1b, "tn", F32, "dw_o")
    dgates, dog, dosb, dom, G["w_br_gdn"], G["w_br_sb"], G["w_br_mem"] = _merge_bwd(
        dmix, proj, (yg, ys, ym), (og, osb, om), wbr)
    dq, dk, dv, dgb, dz, G["gdn_norm_g"] = _gdn_bwd(q, k, v, gb, proj, P["gdn_norm_g"], oraw, shist, tinv, vn, dog)
    dxc, dab, G["conv_w"], dav = _gdn_pre_bwd(proj, P["conv_w"], ab, avec, dq, dk, dv, dgb)
    dqkv = _conv_bwd(dxc, P["conv_w"])
    G["a_log"], G["dt_bias"] = dav[0:1, :N_HEAD], dav[1:2, :N_HEAD]
    dsq, dsk, dsv = _sb_bwd(sq, sk, sv, ltot, dosb)
    dsb, G["sb_q_norm_g"], G["sb_k_norm_g"] = _sb_pre_bwd(proj, P["sb_q_norm_g"], P["sb_k_norm_g"], dsq, dsk, dsv)
    dmemq, dkh, dvm, G["mem_q_norm_g"] = _mem_bwd(proj, kh, vm, P["mem_q_norm_g"], dom)
    G["w_mem_kv"], G["mem_norm_g"], G["mem_k_norm_g"] = _mem_kv_bwd(
        mem, P["mem_norm_g"], W["w_mem_kv"], P["mem_k_norm_g"], kv, mn, dkh, dvm)
    dproj = jnp.concatenate([dqkv, dz, dsb, dmemq, dgates], axis=1)
    dw_main = _mm(dproj, h, "tn", F32, "dw_in")
    dw_ab = _mm(dab, h, "tn", F32, "dw_in_ab")
    G["w_in"] = jnp.concatenate([dw_main[:SB_OFF], dw_ab[:8], dw_main[SB_OFF:]], axis=0)
    dh = _mm(dproj, w_main, "nt", F32, "d_h")
    dh = _mm(dab, w_ab, "nt", F32, "d_h_ab", extra=dh, epi=_epi_add)
    dx, _, G["norm1_g"] = _rms_bwd(dh, x, P["norm1_g"], dx1, "rms1_bwd")
    return sq_err, dx, G


HBM = pl.BlockSpec(memory_space=pl.ANY)


def _comm(name, ins, out_shapes, plan):
    n_in, n_out = len(ins), len(out_shapes)
    probe = plan([None] * n_in, [None] * n_out, 0, 0, 0, dry=True)
    n_copy = probe

    def body(*refs):
        in_refs, out_refs = refs[:n_in], refs[n_in:n_in + n_out]
        send_sems, recv_sems = refs[n_in + n_out:]
        x, y, c = lax.axis_index("x"), lax.axis_index("y"), lax.axis_index("c")
        copies = []
        for k, (src, dst, dev) in enumerate(plan(in_refs, out_refs, x, y, c, dry=False)):
            if dev is None:
                cp = pltpu.make_async_copy(src, dst, send_sems.at[k])
            else:
                cp = pltpu.make_async_remote_copy(src_ref=src, dst_ref=dst, send_sem=send_sems.at[k],
                                                  recv_sem=recv_sems.at[k], device_id=dev, device_id_type=MESH)
            cp.start()
            copies.append(cp)
        for cp in copies:
            cp.wait()

    return pl.pallas_call(
        body, out_shape=tuple(out_shapes), in_specs=[HBM] * n_in, out_specs=tuple([HBM] * n_out),
        scratch_shapes=[pltpu.SemaphoreType.DMA((n_copy,)), pltpu.SemaphoreType.DMA((n_copy,))], name=name)(*ins)


def _other_chips(x, y):
    return ((1 - x, y), (x, 1 - y), (1 - x, 1 - y))


def _gather_weights(parts, conv):
    n = len(parts)
    n_copy = 6 * n + 3

    def body(*refs):
        ins, cin = refs[:n], refs[n]
        outs, cout = refs[n + 1:2 * n + 1], refs[2 * n + 1]
        send, recv, local_sems = refs[2 * n + 2:2 * n + 5]
        staged = refs[2 * n + 5:]
        x, y, c = lax.axis_index("x"), lax.axis_index("y"), lax.axis_index("c")
        me = 2 * x + y
        chips = _other_chips(x, y)
        loads = [pltpu.make_async_copy(src, buf, local_sems.at[2 * p])
                 for p, (src, buf) in enumerate(zip(refs[:n + 1], staged))]
        for cp in loads:
            cp.start()

        def remote(src, dst, k, dev):
            return pltpu.make_async_remote_copy(src_ref=src, dst_ref=dst, send_sem=send.at[k], recv_sem=recv.at[k],
                                                device_id=dev, device_id_type=MESH)

        def my_half(p):
            hr = ins[p].shape[0] // 2
            return pl.ds(pl.multiple_of(c * hr, 16), hr)

        sent = []
        for p in range(n):
            for f, (px, py) in enumerate(chips):
                cp = remote(ins[p].at[my_half(p)], outs[p].at[me, my_half(p)], 6 * p + f, (px, py, c))
                cp.start()
                sent.append(cp)
        direct = []
        for f, (px, py) in enumerate(chips):
            cp = remote(cin, cout.at[me], 6 * n + f, (px, py, c))
            cp.start()
            direct.append(cp)
        stores = []
        for p, (buf, dst) in enumerate(zip(staged, refs[n + 1:2 * n + 2])):
            loads[p].wait()
            cp = pltpu.make_async_copy(buf, dst.at[me], local_sems.at[2 * p + 1])
            cp.start()
            stores.append(cp)
        passed = []
        for p in range(n):
            for f, (px, py) in enumerate(chips):
                landed = outs[p].at[2 * px + py, my_half(p)]
                remote(landed, landed, 6 * p + f, (px, py, c)).wait_recv()
                cp = remote(landed, landed, 6 * p + 3 + f, (x, y, 1 - c))
                cp.start()
                passed.append(cp)
        for cp in sent:
            cp.wait_send()
        for cp in passed + direct + stores:
            cp.wait()

    every = parts + [conv]
    shapes = [jax.ShapeDtypeStruct((4,) + p.shape, p.dtype) for p in every]
    res = pl.pallas_call(
        body, out_shape=tuple(shapes), in_specs=[HBM] * (n + 1), out_specs=tuple([HBM] * (n + 1)),
        scratch_shapes=[pltpu.SemaphoreType.DMA((n_copy,)), pltpu.SemaphoreType.DMA((n_copy,)),
                        pltpu.SemaphoreType.DMA((2 * (n + 1),))] + [pltpu.VMEM(p.shape, p.dtype) for p in every],
        name="gather_weights", compiler_params=_cp())(*every)
    return res[:n], res[n]


def _swap_halves(slabs):
    n = len(slabs)

    def body(*refs):
        ins, outs = refs[:n], refs[n:2 * n]
        send, recv = refs[2 * n:]
        x, y, c = lax.axis_index("x"), lax.axis_index("y"), lax.axis_index("c")
        other = (x, y, 1 - c)
        for p in range(n):
            for j in range(4):
                pltpu.make_async_remote_copy(src_ref=ins[p].at[j, 1 - c], dst_ref=outs[p].at[j], send_sem=send.at[p],
                                             recv_sem=recv.at[p], device_id=other, device_id_type=MESH).start()
        for p in range(n):
            pltpu.make_async_remote_copy(src_ref=outs[p], dst_ref=outs[p], send_sem=send.at[p], recv_sem=recv.at[p],
                                         device_id=other, device_id_type=MESH).wait()

    shapes = [jax.ShapeDtypeStruct((4,) + s.shape[2:], s.dtype) for s in slabs]
    return pl.pallas_call(
        body, out_shape=tuple(shapes), in_specs=[HBM] * n, out_specs=tuple([HBM] * n),
        scratch_shapes=[pltpu.SemaphoreType.DMA((n,)), pltpu.SemaphoreType.DMA((n,))], name="grad_swap_cores")(*slabs)


def _scatter_chips(pairs):
    def plan(ins, outs, x, y, c, dry):
        if dry:
            return 3 * len(ins)
        me = 2 * x + y
        copies = []
        for src, dst in zip(ins, outs):
            copies += [(src.at[2 * px + py], dst.at[me], (px, py, c)) for px, py in _other_chips(x, y)]
        return copies

    return _comm("grad_to_owner", pairs, [jax.ShapeDtypeStruct(a.shape, a.dtype) for a in pairs], plan)


def _join_halves(both):
    n = len(both)

    def body(*refs):
        bufs = refs[n:2 * n]
        send, recv = refs[2 * n:]
        x, y, c = lax.axis_index("x"), lax.axis_index("y"), lax.axis_index("c")
        copies = []
        for p in range(n):
            cp = pltpu.make_async_remote_copy(src_ref=bufs[p].at[c], dst_ref=bufs[p].at[c], send_sem=send.at[p],
                                              recv_sem=recv.at[p], device_id=(x, y, 1 - c), device_id_type=MESH)
            cp.start()
            copies.append(cp)
        for cp in copies:
            cp.wait()

    return pl.pallas_call(
        body, out_shape=tuple(jax.ShapeDtypeStruct(a.shape, a.dtype) for a in both), in_specs=[HBM] * n,
        out_specs=tuple([HBM] * n), input_output_aliases={p: p for p in range(n)},
        scratch_shapes=[pltpu.SemaphoreType.DMA((n,)), pltpu.SemaphoreType.DMA((n,))], name="grad_join_cores")(*both)


def _gather_all(a, name):
    def plan(ins, outs, x, y, c, dry):
        if dry:
            return 8
        me = 4 * x + 2 * y + c
        copies = [(ins[0], outs[0].at[me], None)]
        for f in range(1, 8):
            peer = (1 - x if f & 4 else x, 1 - y if f & 2 else y, 1 - c if f & 1 else c)
            copies.append((ins[0], outs[0].at[me], peer))
        return copies

    return _comm(name, [a], [jax.ShapeDtypeStruct((8,) + a.shape, a.dtype)], plan)[0]


def _sum_slots(a, name, extra=None):
    n, R, _ = a.shape
    rb = min(ROW_BLK, R)

    def body(*refs):
        a_ref, o_ref = refs[0], refs[-1]
        acc = a_ref[0]
        for s in range(1, n):
            acc = acc + a_ref[s]
        if extra is not None:
            acc = acc + refs[1][...]
        o_ref[...] = acc

    ins = [a] + ([extra] if extra is not None else [])
    in_specs = [pl.BlockSpec((n, rb, LANES), lambda i: (0, i, 0))] + ([_rowspec(rb, LANES)] if extra is not None else [])
    return pl.pallas_call(
        body, out_shape=jax.ShapeDtypeStruct((R, LANES), F32), grid=(R // rb,), in_specs=in_specs,
        out_specs=_rowspec(rb, LANES), name=name, compiler_params=_cp("parallel"))(*ins)


def _pair_sum(slab, theirs, core, name):
    _, _, hr, C = slab.shape

    def body(c_ref, a_ref, b_ref, o_ref):
        o_ref[...] = (a_ref[...] + b_ref[...]).astype(BF16)

    return pl.pallas_call(
        body, out_shape=jax.ShapeDtypeStruct((4, hr, C), BF16),
        grid_spec=pltpu.PrefetchScalarGridSpec(
            num_scalar_prefetch=1, grid=(4,),
            in_specs=[pl.BlockSpec((None, None, hr, C), lambda j, c_ref: (j, c_ref[0], 0, 0)),
                      pl.BlockSpec((None, hr, C), lambda j, c_ref: (j, 0, 0))],
            out_specs=pl.BlockSpec((None, hr, C), lambda j, c_ref: (j, 0, 0))),
        name=name, compiler_params=_cp("parallel"))(core, slab, theirs)


def _chip_sum(recv, pairs, where, name):
    _, hr, C = recv.shape

    def body(w_ref, r_ref, p_ref, o_ref):
        me = w_ref[0]
        o_ref[...] = jnp.zeros_like(o_ref)
        for s in range(4):
            @pl.when(me == s)
            def _():
                o_ref[...] += p_ref[...].astype(F32)

            @pl.when(me != s)
            def _():
                o_ref[...] += r_ref[s].astype(F32)

    return pl.pallas_call(
        body, out_shape=jax.ShapeDtypeStruct((2, hr, C), F32),
        grid_spec=pltpu.PrefetchScalarGridSpec(
            num_scalar_prefetch=1, grid=(1,),
            in_specs=[pl.BlockSpec((4, hr, C), lambda i, w_ref: (0, 0, 0)),
                      pl.BlockSpec((None, hr, C), lambda i, w_ref: (w_ref[0], 0, 0))],
            out_specs=pl.BlockSpec((None, hr, C), lambda i, w_ref: (w_ref[1], 0, 0))),
        name=name, compiler_params=_cp("arbitrary"))(where, recv, pairs)


def _adamw(w, g, m, v, name):
    R, C = w.shape
    rb = min(ADAM_ROWS, R)
    c1 = 1.0 - ADAM_B1 ** ADAM_STEP
    c2 = 1.0 - ADAM_B2 ** ADAM_STEP

    def body(w_ref, g_ref, m_ref, v_ref, d_ref, nm_ref, nv_ref):
        gv = g_ref[...]
        nm = ADAM_B1 * m_ref[...] + (1.0 - ADAM_B1) * gv
        nv = ADAM_B2 * v_ref[...] + (1.0 - ADAM_B2) * (gv * gv)
        d_ref[...] = -ADAM_LR * ((nm / c1) / (jnp.sqrt(nv / c2) + ADAM_EPS) + ADAM_WD * w_ref[...])
        nm_ref[...] = nm
        nv_ref[...] = nv

    spec = _rowspec(rb, C)
    return pl.pallas_call(
        body, out_shape=(jax.ShapeDtypeStruct((R, C), F32),) * 3, grid=(R // rb,), in_specs=[spec] * 4,
        out_specs=(spec,) * 3, name=name, compiler_params=_cp("parallel"))(w, g, m, v)


def _pack_rows(parts, rows, dtype):
    flat = jnp.concatenate([p.reshape(-1).astype(dtype) for p in parts])
    return jnp.pad(flat, (0, rows * LANES - flat.shape[0])).reshape(rows, LANES)


def _small_rows(n):
    return max(n // LANES, 1)


def _pack_small(vals):
    rows = []
    for name, n in SMALL:
        r = _small_rows(n)
        rows.append(jnp.pad(vals[name].reshape(-1), (0, r * LANES - n)).reshape(r, LANES))
    flat = jnp.concatenate(rows, axis=0)
    return jnp.pad(flat, ((0, SMALL_ROWS - flat.shape[0]), (0, 0)))


def _unpack_small(pack):
    out, r0 = {}, 0
    for name, n in SMALL:
        r = _small_rows(n)
        out[name] = pack[r0:r0 + r].reshape(-1)[:n]
        r0 += r
    return out


def kernel(x, mem, norm1_g, w_in, conv_w, a_log, dt_bias, gdn_norm_g, sb_q_norm_g, sb_k_norm_g, mem_norm_g, w_mem_kv, mem_q_norm_g, mem_k_norm_g, w_br_gdn, w_br_sb, w_br_mem, w_o, norm2_g, w_up, w_down, loss_target, m_norm1_g, m_w_in, m_conv_w, m_a_log, m_dt_bias, m_gdn_norm_g, m_sb_q_norm_g, m_sb_k_norm_g, m_mem_norm_g, m_w_mem_kv, m_mem_q_norm_g, m_mem_k_norm_g, m_w_br_gdn, m_w_br_sb, m_w_br_mem, m_w_o, m_norm2_g, m_w_up, m_w_down, v_norm1_g, v_w_in, v_conv_w, v_a_log, v_dt_bias, v_gdn_norm_g, v_sb_q_norm_g, v_sb_k_norm_g, v_mem_norm_g, v_w_mem_kv, v_mem_q_norm_g, v_mem_k_norm_g, v_w_br_gdn, v_w_br_sb, v_w_br_mem, v_w_o, v_norm2_g, v_w_up, v_w_down):
    wd = dict(norm1_g=norm1_g, w_in=w_in, conv_w=conv_w, a_log=a_log, dt_bias=dt_bias, gdn_norm_g=gdn_norm_g,
              sb_q_norm_g=sb_q_norm_g, sb_k_norm_g=sb_k_norm_g, mem_norm_g=mem_norm_g, w_mem_kv=w_mem_kv,
              mem_q_norm_g=mem_q_norm_g, mem_k_norm_g=mem_k_norm_g, w_br_gdn=w_br_gdn, w_br_sb=w_br_sb,
              w_br_mem=w_br_mem, w_o=w_o, norm2_g=norm2_g, w_up=w_up, w_down=w_down)
    md = dict(norm1_g=m_norm1_g, w_in=m_w_in, conv_w=m_conv_w, a_log=m_a_log, dt_bias=m_dt_bias,
              gdn_norm_g=m_gdn_norm_g, sb_q_norm_g=m_sb_q_norm_g, sb_k_norm_g=m_sb_k_norm_g,
              mem_norm_g=m_mem_norm_g, w_mem_kv=m_w_mem_kv, mem_q_norm_g=m_mem_q_norm_g,
              mem_k_norm_g=m_mem_k_norm_g, w_br_gdn=m_w_br_gdn, w_br_sb=m_w_br_sb, w_br_mem=m_w_br_mem, w_o=m_w_o,
              norm2_g=m_norm2_g, w_up=m_w_up, w_down=m_w_down)
    vd = dict(norm1_g=v_norm1_g, w_in=v_w_in, conv_w=v_conv_w, a_log=v_a_log, dt_bias=v_dt_bias,
              gdn_norm_g=v_gdn_norm_g, sb_q_norm_g=v_sb_q_norm_g, sb_k_norm_g=v_sb_k_norm_g,
              mem_norm_g=v_mem_norm_g, w_mem_kv=v_w_mem_kv, mem_q_norm_g=v_mem_q_norm_g,
              mem_k_norm_g=v_mem_k_norm_g, w_br_gdn=v_w_br_gdn, w_br_sb=v_w_br_sb, w_br_mem=v_w_br_mem, w_o=v_w_o,
              norm2_g=v_norm2_g, w_up=v_w_up, w_down=v_w_down)
    wd, md, vd = ({n: a[0] for n, a in d.items()} for d in (wd, md, vd))
    chip = 2 * lax.axis_index("x") + lax.axis_index("y")
    core = lax.axis_index("c").astype(jnp.int32).reshape(1)
    conv_shard = wd["conv_w"].shape

    gathered, conv_all = _gather_weights([wd[n].astype(BF16) for n, _, _ in BIG], wd["conv_w"])
    W = {}
    for (name, shape, axis), blk in zip(BIG, gathered):
        W[name] = blk.reshape(4 * shape[0], shape[1]) if axis == 0 else blk.transpose(1, 0, 2).reshape(shape[0], 4 * shape[1])
    P = {n: wd[n].reshape(1, -1) for n, _ in SMALL}
    P["conv_w"] = conv_all.transpose(1, 0, 2).reshape(conv_shard[0], 4 * conv_shard[1])

    sq_err, grad_x, G = _local_step(x[0], mem[0], loss_target[0], W, P)
    loss = lax.psum(0.5 / D_MODEL * jnp.sum(sq_err), ("x", "y", "c"))

    slabs = []
    for name, (r, cc), axis in BIG:
        g = G[name]
        if axis == 0:
            slabs.append(g.reshape(4, 2, r // 2, cc))
        else:
            rows = _slab_rows(cc)
            g = jnp.pad(g.reshape(4, cc, r), ((0, 0), (0, rows - cc), (0, 0)))
            slabs.append(g.reshape(4, 2, rows // 2, r))
    theirs = _swap_halves(slabs)
    pairs = [_pair_sum(s, t, core, "pair_sum_" + n) for s, t, (n, _, _) in zip(slabs, theirs, BIG)]
    by_chip = _scatter_chips(pairs)
    where = jnp.concatenate([chip.astype(jnp.int32).reshape(1), core])
    halves = [_chip_sum(a, p, where, "chip_sum_" + n) for a, p, (n, _, _) in zip(by_chip, pairs, BIG)]
    g_big = {}
    for (name, (r, cc), axis), both in zip(BIG, _join_halves(halves)):
        full = both.reshape(-1, both.shape[-1])
        g_big[name] = full if axis == 0 else full[:cc].T

    spack = jnp.concatenate([_pack_small(G), G["conv_w"].reshape(CONV_ROWS, LANES)], axis=0)
    g_small = _sum_slots(_gather_all(spack, "gather_small_grads"), "small_grad_sum")
    g_conv_full = g_small[SMALL_ROWS:].reshape(conv_shard[0], 4 * conv_shard[1])
    g_conv = lax.dynamic_slice_in_dim(g_conv_full, chip * conv_shard[1], conv_shard[1], axis=1)

    grads, deltas, new_m, new_v = dict(g_big), {}, {}, {}
    for name, _, _ in BIG:
        deltas[name], new_m[name], new_v[name] = _adamw(wd[name], g_big[name], md[name], vd[name], "adamw_" + name)
    pack_sm = lambda d: jnp.concatenate([_pack_small(d), _pack_rows([d["conv_w"]], APACK_ROWS - SMALL_ROWS, F32)], axis=0)
    g_sm = jnp.concatenate([g_small[:SMALL_ROWS], _pack_rows([g_conv], APACK_ROWS - SMALL_ROWS, F32)], axis=0)
    small = (g_sm,) + _adamw(pack_sm(wd), g_sm, pack_sm(md), pack_sm(vd), "adamw_small")
    for out, pack in zip((grads, deltas, new_m, new_v), small):
        out.update(_unpack_small(pack[:SMALL_ROWS]))
        out["conv_w"] = pack[SMALL_ROWS:].reshape(-1)[:conv_shard[0] * conv_shard[1]].reshape(conv_shard)

    return (loss, grad_x[None], *[d[n][None] for d in (grads, deltas, new_m, new_v) for n in WEIGHTS])


def _slab_rows(n):
    return -(-n // 32) * 32
```

```python
import functools

import jax
import jax.numpy as jnp
import numpy as np
from jax import lax
from jax.experimental import pallas as pl
from jax.experimental.pallas import tpu as pltpu

F32 = jnp.float32
BF16 = jnp.bfloat16
MESH = pl.DeviceIdType.MESH

D_MODEL = 1024
N_HEAD = 4
D_HEAD = 128
BR_W = N_HEAD * D_HEAD
CONV_TAPS = 4
GDN_CHUNK = 64
INV_BLOCK = 16
INV_CHUNKS = 4
N_MEM = 256
D_FF = 4 * D_MODEL
EPS = 1e-6
LANES = 128
PROJ_W = 7168
GATE_OFF = 4096
SB_OFF = 2048
MEMQ_OFF = 3584
Z_OFF = 1536

ADAM_LR, ADAM_B1, ADAM_B2, ADAM_EPS, ADAM_WD, ADAM_STEP = 0.001, 0.9, 0.999, 1e-08, 0.01, 10

TM = 512
MM_TM = 1024
TK_TOK = 1024
GDN_STEP_CHUNKS = 4
GDN_BWD_STEP_CHUNKS = 1
G1_TM = 256
SB_BLK_Q = 512
SB_BLK = 512
SB_W = 256
VMEM_LIMIT = 48 << 20

BIG = (("w_in", (1024, 1794), 1), ("w_mem_kv", (256, 1024), 0), ("w_br_gdn", (512, 256), 1),
       ("w_br_sb", (512, 256), 1), ("w_br_mem", (512, 256), 1), ("w_o", (256, 1024), 0),
       ("w_up", (1024, 1024), 1), ("w_down", (1024, 1024), 0))
COL_SHARDED = tuple(n for n, _, a in BIG if a == 1)
GATE_COL = GATE_OFF // D_MODEL
ROW_BLK = 1024
ADAM_ROWS = 128
SMALL = (("norm1_g", 1024), ("mem_norm_g", 1024), ("norm2_g", 1024), ("gdn_norm_g", 128), ("sb_q_norm_g", 128),
         ("sb_k_norm_g", 128), ("mem_q_norm_g", 128), ("mem_k_norm_g", 128), ("a_log", 4), ("dt_bias", 4))
SMALL_ROWS = 32
CONV_ROWS = 48
SPACK_ROWS = SMALL_ROWS + CONV_ROWS
APACK_ROWS = SMALL_ROWS + 16

WEIGHTS = ("norm1_g", "w_in", "conv_w", "a_log", "dt_bias", "gdn_norm_g", "sb_q_norm_g", "sb_k_norm_g",
           "mem_norm_g", "w_mem_kv", "mem_q_norm_g", "mem_k_norm_g", "w_br_gdn", "w_br_sb", "w_br_mem", "w_o",
           "norm2_g", "w_up", "w_down")


def _cp(*sem):
    return pltpu.CompilerParams(dimension_semantics=sem if sem else None, vmem_limit_bytes=VMEM_LIMIT)


HBM = pl.BlockSpec(memory_space=pl.ANY)

_NN = (((1,), (0,)), ((), ()))
_NT = (((1,), (1,)), ((), ()))
_TN = (((0,), (0,)), ((), ()))


def _dot(a, b, dims=_NN):
    return lax.dot_general(a.astype(BF16), b.astype(BF16), dims, preferred_element_type=F32)


def _dot_nt(a, b):
    return _dot(a, b, _NT)


def _dot_tn(a, b):
    return _dot(a, b, _TN)


def _dotf(a, b, dims=_NN):
    return lax.dot_general(a, b, dims, precision=lax.Precision.HIGHEST, preferred_element_type=F32)


def _sigmoid(v):
    return 0.5 * jnp.tanh(0.5 * v) + 0.5


def _softplus(v):
    return jnp.maximum(v, 0.0) + jnp.log(1.0 + jnp.exp(-jnp.abs(v)))


def _iota(shape, dim):
    return lax.broadcasted_iota(jnp.int32, shape, dim)


def _hs(h):
    return slice(h * D_HEAD, (h + 1) * D_HEAD)


def _rowspec(tm, w, col=0):
    return pl.BlockSpec((tm, w), lambda i: (i, col))


def _full(shape):
    return pl.BlockSpec(shape, lambda *_: (0,) * len(shape))


def _accum(ref, first, val):
    @pl.when(first)
    def _():
        ref[...] = val

    @pl.when(jnp.logical_not(first))
    def _():
        ref[...] += val


class _Hosted:
    def __init__(self, ins, out_shapes, n_sems, start, finish, mid=None, scratch=()):
        self.ins, self.out_shapes, self.n_sems = list(ins), list(out_shapes), n_sems
        self.start, self.mid, self.finish, self.scratch = start, mid, finish, list(scratch)


def _mm(a, b, mode, out_dtype, name, *, tm=None, tn=None, tk=None, a_fn=None, extra=None, epi=None, comm=None):
    if mode == "tn":
        (K, M), N = a.shape, b.shape[1]
    else:
        (M, K), N = a.shape, (b.shape[0] if mode == "nt" else b.shape[1])
    tm = min(tm or (1024 if mode == "tn" else MM_TM), M)
    tn = min(tn or 1024, N)
    tk = min(tk or (TK_TOK if mode == "tn" else 1024), K)
    nm, nn, nk = M // tm, N // tn, K // tk
    assert nm * tm == M and nn * tn == N and nk * tk == K, (name, a.shape, b.shape)
    if mode == "tn":
        a_spec = pl.BlockSpec((tk, tm), lambda i, j, k: (k, i))
    else:
        a_spec = pl.BlockSpec((tm, tk), lambda i, j, k: (i, k))
    if mode == "nt":
        b_spec = pl.BlockSpec((tn, tk), lambda i, j, k: (j, k))
    else:
        b_spec = pl.BlockSpec((tk, tn), lambda i, j, k: (k, j))
    dims = {"nn": _NN, "nt": _NT, "tn": _TN}[mode]
    o_spec = pl.BlockSpec((tm, tn), lambda i, j, k: (i, j))
    has_extra = extra is not None

    n_ci, n_co = (len(comm.ins), len(comm.out_shapes)) if comm else (0, 0)
    n_in = 2 + has_extra + n_ci
    steps = nm * nn * nk

    def body(*refs):
        a_ref, b_ref = refs[0], refs[1]
        e_ref = refs[2] if has_extra else None
        o_ref = refs[n_in]
        scratch = refs[n_in + 1 + n_co:]
        if comm:
            step = (pl.program_id(0) * nn + pl.program_id(1)) * nk + pl.program_id(2)
            cargs = (refs[2 + has_extra:n_in], refs[n_in + 1:n_in + 1 + n_co], scratch[nk > 1], scratch[(nk > 1) + 1],
                     scratch[(nk > 1) + 2:])
            pl.when(step == 0)(lambda: comm.start(*cargs))
            if comm.mid is not None:
                pl.when(step == steps // 2)(lambda: comm.mid(*cargs))
        av = a_ref[...]
        if a_fn is not None:
            av = a_fn(av)
        p = lax.dot_general(av, b_ref[...], dims, preferred_element_type=F32)

        def finish(acc):
            if epi is not None:
                acc = epi(acc, e_ref[...] if has_extra else None)
            o_ref[...] = acc.astype(out_dtype)

        if nk == 1:
            finish(p)
        else:
            acc_ref = scratch[0]
            k = pl.program_id(2)
            _accum(acc_ref, k == 0, p)

            @pl.when(k == nk - 1)
            def _():
                finish(acc_ref[...])

        if comm:
            pl.when(step == steps - 1)(lambda: comm.finish(*cargs))

    ins = [a, b] + ([extra] if has_extra else [])
    in_specs = [a_spec, b_spec] + ([o_spec] if has_extra else [])
    scratch_shapes = [pltpu.VMEM((tm, tn), F32)] if nk > 1 else []
    main = jax.ShapeDtypeStruct((M, N), out_dtype)
    if not comm:
        return pl.pallas_call(
            body, out_shape=main, grid=(nm, nn, nk), in_specs=in_specs, out_specs=o_spec,
            scratch_shapes=scratch_shapes, name=name, compiler_params=_cp("parallel", "parallel", "arbitrary"))(*ins)
    sems = [pltpu.SemaphoreType.DMA((comm.n_sems,)), pltpu.SemaphoreType.DMA((comm.n_sems,))]
    res = pl.pallas_call(
        body, out_shape=(main, *comm.out_shapes), grid=(nm, nn, nk), in_specs=in_specs + [HBM] * n_ci,
        out_specs=(o_spec, *[HBM] * n_co), scratch_shapes=scratch_shapes + sems + comm.scratch, name=name,
        compiler_params=_cp("arbitrary", "arbitrary", "arbitrary"))(*ins, *comm.ins)
    return res[0], list(res[1:])


def _relu2(u):
    r = jnp.maximum(u.astype(F32), 0.0)
    return (r * r).astype(BF16)


def _epi_add(acc, e):
    return acc + e.astype(F32)


def _epi_drelu2(acc, u):
    return acc * (2.0 * jnp.maximum(u.astype(F32), 0.0))


def _rms_fwd(x, g, name):
    T, dm = x.shape
    tm = min(TM, T)

    def body(x_ref, g_ref, h_ref):
        xv = x_ref[...]
        r = lax.rsqrt(jnp.mean(xv * xv, axis=-1, keepdims=True) + EPS)
        h_ref[...] = (xv * r * g_ref[...]).astype(BF16)

    return pl.pallas_call(
        body, out_shape=jax.ShapeDtypeStruct((T, dm), BF16), grid=(T // tm,),
        in_specs=[_rowspec(tm, dm), _full((1, dm))], out_specs=_rowspec(tm, dm), name=name,
        compiler_params=_cp("parallel"))(x, g)


def _rms_bwd(dh, x, g, resid, name):
    T, dm = x.shape
    tm = min(TM, T)

    def body(dh_ref, x_ref, g_ref, res_ref, dx_ref, dxb_ref, dg_ref):
        i = pl.program_id(0)
        xv = x_ref[...]
        r = lax.rsqrt(jnp.mean(xv * xv, axis=-1, keepdims=True) + EPS)
        y = xv * r
        dhv = dh_ref[...].astype(F32)
        dy = dhv * g_ref[...]
        dx = res_ref[...] + r * (dy - y * jnp.mean(dy * y, axis=-1, keepdims=True))
        dx_ref[...] = dx
        dxb_ref[...] = dx.astype(BF16)
        _accum(dg_ref, i == 0, jnp.sum(dhv * y, axis=0, keepdims=True))

    return pl.pallas_call(
        body,
        out_shape=(jax.ShapeDtypeStruct((T, dm), F32), jax.ShapeDtypeStruct((T, dm), BF16),
                   jax.ShapeDtypeStruct((1, dm), F32)),
        grid=(T // tm,),
        in_specs=[_rowspec(tm, dm), _rowspec(tm, dm), _full((1, dm)), _rowspec(tm, dm)],
        out_specs=(_rowspec(tm, dm), _rowspec(tm, dm), _full((1, dm))), name=name,
        compiler_params=_cp("arbitrary"))(dh, x, g, resid)


def _conv_tile(x_ref, halo_ref, w_ref, xpad, tm):
    i = pl.program_id(0)
    halo = halo_ref[...].astype(F32)[8:16]
    xpad[0:8, :] = jnp.where(i > 0, halo, 0.0)
    xpad[8:, :] = x_ref[...].astype(F32)
    w = w_ref[...]
    xc = w[0:1] * xpad[5:5 + tm, :]
    for j in range(1, CONV_TAPS):
        xc = xc + w[j:j + 1] * xpad[5 + j:5 + j + tm, :]
    return xc


def _gate_terms(ab_ref, av_ref):
    abv = ab_ref[...]
    av = av_ref[...]
    pre = abv + av[1:2]
    ea = jnp.exp(av[0:1])
    g = -ea * _softplus(pre)
    return abv, pre, ea, g


def _gdn_pre(proj, conv_w, ab, avec):
    T = proj.shape[0]
    tm = min(G1_TM, T)
    cw = 3 * BR_W

    def body(x_ref, halo_ref, w_ref, ab_ref, av_ref, q_ref, k_ref, v_ref, gb_ref, xpad):
        xc = _conv_tile(x_ref, halo_ref, w_ref, xpad, tm)
        y = xc * _sigmoid(xc)
        for h in range(N_HEAD):
            for off, ref, scale in ((0, q_ref, D_HEAD ** -0.5), (BR_W, k_ref, 1.0)):
                yh = y[:, off + h * D_HEAD:off + (h + 1) * D_HEAD]
                r = lax.rsqrt(jnp.sum(yh * yh, axis=-1, keepdims=True) + EPS)
                ref[:, _hs(h)] = yh * (r * scale)
        v_ref[...] = y[:, 2 * BR_W:]
        abv, _, _, g = _gate_terms(ab_ref, av_ref)
        lane = _iota((tm, LANES), 1)
        gb_ref[...] = jnp.where(lane < N_HEAD, g, jnp.where(lane < 2 * N_HEAD, _sigmoid(abv), 0.0))

    hb = tm // 16
    return pl.pallas_call(
        body,
        out_shape=(jax.ShapeDtypeStruct((T, BR_W), F32),) * 3 + (jax.ShapeDtypeStruct((T, LANES), F32),),
        grid=(T // tm,),
        in_specs=[_rowspec(tm, cw), pl.BlockSpec((16, cw), lambda i: (jnp.maximum(i * hb - 1, 0), 0)),
                  _full((CONV_TAPS, cw)), _rowspec(tm, LANES), _full((2, LANES))],
        out_specs=(_rowspec(tm, BR_W),) * 3 + (_rowspec(tm, LANES),),
        scratch_shapes=[pltpu.VMEM((tm + 8, cw), F32)], name="gdn_pre",
        compiler_params=_cp("parallel"))(proj, proj, conv_w, ab, avec)


def _gdn_pre_bwd(proj, conv_w, ab, avec, dq, dk, dv, dgb):
    T = proj.shape[0]
    tm = min(G1_TM, T)
    cw = 3 * BR_W

    def body(x_ref, halo_ref, w_ref, ab_ref, av_ref, dq_ref, dk_ref, dv_ref, dgb_ref,
             dxc_ref, dab_ref, dcw_ref, dav_ref, xpad):
        i = pl.program_id(0)

        @pl.when(i == 0)
        def _():
            dcw_ref[...] = jnp.zeros_like(dcw_ref)
            dav_ref[...] = jnp.zeros_like(dav_ref)

        xc_all = _conv_tile(x_ref, halo_ref, w_ref, xpad, tm)
        for s in range(cw // D_HEAD):
            cs = slice(s * D_HEAD, (s + 1) * D_HEAD)
            xc = xc_all[:, cs]
            sg = _sigmoid(xc)
            yh = xc * sg
            h = s % N_HEAD
            if s < 2 * N_HEAD:
                dref, scale = (dq_ref, D_HEAD ** -0.5) if s < N_HEAD else (dk_ref, 1.0)
                r = lax.rsqrt(jnp.sum(yh * yh, axis=-1, keepdims=True) + EPS)
                yn = yh * r
                dn = dref[:, _hs(h)]
                dy = (scale * r) * (dn - yn * jnp.sum(yn * dn, axis=-1, keepdims=True))
            else:
                dy = dv_ref[:, _hs(h)]
            dxc = dy * (sg * (1.0 + xc * (1.0 - sg)))
            dxc_ref[:, cs] = dxc.astype(BF16)
            for j in range(CONV_TAPS):
                dcw_ref[j:j + 1, cs] += jnp.sum(dxc * xpad[5 + j:5 + j + tm, cs], axis=0, keepdims=True)

        abv, pre, ea, g = _gate_terms(ab_ref, av_ref)
        dgbv = dgb_ref[...]
        lane = _iota((tm, LANES), 1)
        is_a = lane < N_HEAD
        da = jnp.where(is_a, dgbv * (-ea) * _sigmoid(pre), 0.0)
        bs = _sigmoid(abv)
        db = jnp.where(jnp.logical_and(lane >= N_HEAD, lane < 2 * N_HEAD), dgbv * bs * (1.0 - bs), 0.0)
        dab_ref[...] = (da + db).astype(BF16)
        dav_ref[0:1, :] += jnp.sum(jnp.where(is_a, dgbv * g, 0.0), axis=0, keepdims=True)
        dav_ref[1:2, :] += jnp.sum(da, axis=0, keepdims=True)

    hb = tm // 16
    return pl.pallas_call(
        body,
        out_shape=(jax.ShapeDtypeStruct((T, cw), BF16), jax.ShapeDtypeStruct((T, LANES), BF16),
                   jax.ShapeDtypeStruct((CONV_TAPS, cw), F32), jax.ShapeDtypeStruct((2, LANES), F32)),
        grid=(T // tm,),
        in_specs=[_rowspec(tm, cw), pl.BlockSpec((16, cw), lambda i: (jnp.maximum(i * hb - 1, 0), 0)),
                  _full((CONV_TAPS, cw)), _rowspec(tm, LANES), _full((2, LANES)),
                  _rowspec(tm, BR_W), _rowspec(tm, BR_W), _rowspec(tm, BR_W), _rowspec(tm, LANES)],
        out_specs=(_rowspec(tm, cw), _rowspec(tm, LANES), _full((CONV_TAPS, cw)), _full((2, LANES))),
        scratch_shapes=[pltpu.VMEM((tm + 8, cw), F32)], name="gdn_pre_bwd",
        compiler_params=_cp("arbitrary"))(proj, proj, conv_w, ab, avec, dq, dk, dv, dgb)


def _conv_bwd(dxc, conv_w):
    T, cw = dxc.shape
    tm = min(G1_TM, T)
    nt = T // tm
    hb = tm // 16

    def body(d_ref, halo_ref, w_ref, dx_ref, xpad):
        i = pl.program_id(0)
        xpad[0:tm, :] = d_ref[...].astype(F32)
        xpad[tm:, :] = jnp.where(i < nt - 1, halo_ref[...].astype(F32)[0:8], 0.0)
        w = w_ref[...]
        dx = w[3:4] * xpad[0:tm, :]
        for j in range(CONV_TAPS - 1):
            dx = dx + w[j:j + 1] * xpad[3 - j:3 - j + tm, :]
        dx_ref[...] = dx.astype(BF16)

    return pl.pallas_call(
        body, out_shape=jax.ShapeDtypeStruct((T, cw), BF16), grid=(nt,),
        in_specs=[_rowspec(tm, cw), pl.BlockSpec((16, cw), lambda i: (jnp.minimum((i + 1) * hb, T // 16 - 1), 0)),
                  _full((CONV_TAPS, cw))],
        out_specs=_rowspec(tm, cw), scratch_shapes=[pltpu.VMEM((tm + 8, cw), F32)], name="conv_bwd",
        compiler_params=_cp("parallel"))(dxc, dxc, conv_w)


def _chunk_consts():
    C = GDN_CHUNK
    row, col = _iota((C, C), 0), _iota((C, C), 1)
    return row, col, row >= col, row > col


def _chunk_decay(gbv, incl):
    c_all = _dotf(incl.astype(F32), gbv)
    c_t = jnp.concatenate([c_all, jnp.zeros_like(c_all)], axis=0).T[:, :GDN_CHUNK]
    return c_all, c_t


def _head_decay(c_all, c_t, gbv, incl, h):
    C = GDN_CHUNK
    c_col = c_all[:, h:h + 1]
    c_row = c_t[h:h + 1, :]
    gam = jnp.exp(jnp.where(incl, c_col - c_row, -1e30))
    c_last = c_all[C - 1:C, h:h + 1]
    return gam, jnp.exp(c_col), jnp.exp(c_last - c_col), jnp.exp(c_last), gbv[:, N_HEAD + h:N_HEAD + h + 1]


def _split_bf16(x):
    hi = x.astype(BF16)
    return hi, (x - hi.astype(F32)).astype(BF16)


def _dot3(a, b):
    ah, al = _split_bf16(a)
    bh, bl = _split_bf16(b)
    d = lambda u, v: lax.dot_general(u, v, _NN, preferred_element_type=F32)
    return d(ah, bh) + (d(ah, bl) + d(al, bh))


def _unit_lower_inverses(ms, row, col):
    bi, bj = row // INV_BLOCK, col // INV_BLOCK
    eye = (row == col).astype(F32)
    ns = [jnp.where(bi == bj, -m, 0.0) for m in ms]
    invs = [eye + n for n in ns]
    size = 2
    while size < INV_BLOCK:
        ns = [_dot3(n, n) for n in ns]
        invs = [inv + _dot3(inv, n) for inv, n in zip(invs, ns)]
        size *= 2
    width = 2
    while width * INV_BLOCK <= GDN_CHUNK:
        sel = jnp.logical_and(bi // width == bj // width, bi // (width // 2) > bj // (width // 2))
        ts = [_dot3(inv, jnp.where(sel, m, 0.0)) for inv, m in zip(invs, ms)]
        invs = [inv - _dot3(t, inv) for inv, t in zip(invs, ts)]
        width *= 2
    return invs


def _gdn_inv(k, gb):
    T = k.shape[0]
    C = GDN_CHUNK
    per = min(INV_CHUNKS, T // C)
    rows = per * C

    def body(k_ref, gb_ref, ti_ref):
        row, col, incl, strict = _chunk_consts()
        ms = []
        for ci in range(per):
            rs = slice(ci * C, (ci + 1) * C)
            gbv = gb_ref[rs, :]
            c_all, c_t = _chunk_decay(gbv, incl)
            for h in range(N_HEAD):
                gam, _, _, _, bcol = _head_decay(c_all, c_t, gbv, incl, h)
                K = k_ref[rs, _hs(h)]
                ms.append(jnp.where(strict, _dot_nt(K * bcol, K) * gam, 0.0))
        for i, inv in enumerate(_unit_lower_inverses(ms, row, col)):
            ti_ref[i // N_HEAD, i % N_HEAD] = inv

    return pl.pallas_call(
        body, out_shape=jax.ShapeDtypeStruct((T // C, N_HEAD, C, C), F32), grid=(T // rows,),
        in_specs=[_rowspec(rows, BR_W), _rowspec(rows, LANES)],
        out_specs=pl.BlockSpec((per, N_HEAD, C, C), lambda i: (i, 0, 0, 0)), name="gdn_inv",
        compiler_params=_cp("parallel"))(k, gb)


def _gdn_fwd(q, k, v, gb, proj, gnorm, tinv_all):
    T = q.shape[0]
    C = GDN_CHUNK
    nc = T // C
    per = min(GDN_STEP_CHUNKS, nc)
    zcol = Z_OFF // BR_W
    heads = range(N_HEAD)

    def body(q_ref, k_ref, v_ref, gb_ref, z_ref, gn_ref, ti_ref, og_ref, oraw_ref, sh_ref, vn_ref, s_ref):
        @pl.when(pl.program_id(0) == 0)
        def _():
            s_ref[...] = jnp.zeros_like(s_ref)

        _, _, incl, _ = _chunk_consts()
        S = [s_ref[h] for h in heads]
        for ci in range(per):
            rs = slice(ci * C, (ci + 1) * C)
            gbv = gb_ref[rs, :]
            c_all, c_t = _chunk_decay(gbv, incl)
            dec = [_head_decay(c_all, c_t, gbv, incl, h) for h in heads]
            gam, gcol, dcol, glast, bcol = ([d[i] for d in dec] for i in range(5))
            Q = [q_ref[rs, _hs(h)] for h in heads]
            K = [k_ref[rs, _hs(h)] for h in heads]
            V = [v_ref[rs, _hs(h)] for h in heads]
            Sb = [s.astype(BF16) for s in S]
            KS = [_dot(K[h], Sb[h]) for h in heads]
            QS = [_dot(Q[h], Sb[h]) for h in heads]
            P = [_dot_nt(Q[h], K[h]) * gam[h] for h in heads]
            R = [bcol[h] * (V[h] - gcol[h] * KS[h]) for h in heads]
            vn = [_dot(ti_ref[ci, h], R[h]) for h in heads]
            O = [gcol[h] * QS[h] + _dot(P[h], vn[h]) for h in heads]
            Sn = [glast[h] * S[h] + _dot_tn(K[h] * dcol[h], vn[h]) for h in heads]
            for h in heads:
                sh_ref[ci, h] = S[h]
                vn_ref[rs, _hs(h)] = vn[h]
                oraw_ref[rs, _hs(h)] = O[h]
                rr = lax.rsqrt(jnp.mean(O[h] * O[h], axis=-1, keepdims=True) + EPS)
                zz = z_ref[rs, _hs(h)].astype(F32)
                og_ref[rs, _hs(h)] = (O[h] * rr * gn_ref[...] * (zz * _sigmoid(zz))).astype(BF16)
            S = Sn
        for h in heads:
            s_ref[h] = S[h]

    cspec = lambda w, cb=0: pl.BlockSpec((per * C, w), lambda n: (n, cb))
    hist = lambda a, b: pl.BlockSpec((per, N_HEAD, a, b), lambda n: (n, 0, 0, 0))
    return pl.pallas_call(
        body,
        out_shape=(jax.ShapeDtypeStruct((T, BR_W), BF16), jax.ShapeDtypeStruct((T, BR_W), F32),
                   jax.ShapeDtypeStruct((nc, N_HEAD, D_HEAD, D_HEAD), F32), jax.ShapeDtypeStruct((T, BR_W), F32)),
        grid=(nc // per,),
        in_specs=[cspec(BR_W), cspec(BR_W), cspec(BR_W), cspec(LANES), cspec(BR_W, zcol), _full((1, D_HEAD)),
                  hist(C, C)],
        out_specs=(cspec(BR_W), cspec(BR_W), hist(D_HEAD, D_HEAD), cspec(BR_W)),
        scratch_shapes=[pltpu.VMEM((N_HEAD, D_HEAD, D_HEAD), F32)], name="gdn_chunk_fwd",
        compiler_params=_cp("arbitrary"))(q, k, v, gb, proj, gnorm, tinv_all)


def _gdn_bwd(q, k, v, gb, proj, gnorm, oraw, shist, tinv_all, vn_all, dog):
    T = q.shape[0]
    C = GDN_CHUNK
    nc = T // C
    per = min(GDN_BWD_STEP_CHUNKS, nc)
    zcol = Z_OFF // BR_W

    def body(q_ref, k_ref, v_ref, gb_ref, z_ref, gn_ref, oraw_ref, sh_ref, ti_ref, vn_ref, dog_ref,
             dq_ref, dk_ref, dv_ref, dgb_ref, dz_ref, dgn_ref, ds_ref):
        @pl.when(pl.program_id(0) == 0)
        def _():
            ds_ref[...] = jnp.zeros_like(ds_ref)
            dgn_ref[...] = jnp.zeros_like(dgn_ref)

        row, col, incl, strict = _chunk_consts()
        lane = _iota((C, LANES), 1)
        rowl = _iota((C, LANES), 0)
        ones = jnp.ones((C, LANES), F32)
        upper = (col >= row).astype(F32)
        gn = gn_ref[...]
        heads = range(N_HEAD)
        rsum = lambda a: jnp.sum(a, axis=-1, keepdims=True)
        dgn = jnp.zeros((1, D_HEAD), F32)
        dSn = [ds_ref[h] for h in heads]
        for ci in reversed(range(per)):
            rs = slice(ci * C, (ci + 1) * C)
            gbv = gb_ref[rs, :]
            c_all, c_t = _chunk_decay(gbv, incl)
            dec = [_head_decay(c_all, c_t, gbv, incl, h) for h in heads]
            gam, gcol, dcol, glast, bcol = ([d[i] for d in dec] for i in range(5))
            Q = [q_ref[rs, _hs(h)] for h in heads]
            K = [k_ref[rs, _hs(h)] for h in heads]
            V = [v_ref[rs, _hs(h)] for h in heads]
            dO = []
            for h in heads:
                O = oraw_ref[rs, _hs(h)]
                zz = z_ref[rs, _hs(h)].astype(F32)
                dogv = dog_ref[rs, _hs(h)].astype(F32)
                rr = lax.rsqrt(jnp.mean(O * O, axis=-1, keepdims=True) + EPS)
                on = O * rr
                sg = _sigmoid(zz)
                dz_ref[rs, _hs(h)] = (dogv * on * gn * (sg * (1.0 + zz * (1.0 - sg)))).astype(BF16)
                dyn = dogv * (zz * sg)
                dgn = dgn + jnp.sum(dyn * on, axis=0, keepdims=True)
                dyv = dyn * gn
                dO.append((rr * (dyv - on * jnp.mean(dyv * on, axis=-1, keepdims=True))).astype(BF16))
            S = [sh_ref[ci, h] for h in heads]
            Sb = [s.astype(BF16) for s in S]
            tinv = [ti_ref[ci, h].astype(BF16) for h in heads]
            vn = [vn_ref[rs, _hs(h)] for h in heads]
            vnb = [a.astype(BF16) for a in vn]
            dSb = [a.astype(BF16) for a in dSn]
            Kb = [K[h] * bcol[h] for h in heads]
            M = [jnp.where(strict, _dot_nt(Kb[h], K[h]) * gam[h], 0.0) for h in heads]
            P = [_dot_nt(Q[h], K[h]) * gam[h] for h in heads]
            KS = [_dot(K[h], Sb[h]) for h in heads]
            QS = [_dot(Q[h], Sb[h]) for h in heads]
            dvn = [_dot_tn(P[h], dO[h]) + _dot(K[h] * dcol[h], dSb[h]) for h in heads]
            dR = [_dot_tn(tinv[h], dvn[h]) for h in heads]
            dRb = [a.astype(BF16) for a in dR]
            bg = [bcol[h] * gcol[h] for h in heads]
            dS_new = [glast[h] * dSn[h] + _dot_tn(gcol[h] * Q[h], dO[h]) - _dot_tn(bg[h] * K[h], dRb[h])
                      for h in heads]
            dP = [jnp.where(incl, _dot_nt(dO[h], vnb[h]), 0.0) for h in heads]
            dM = [jnp.where(strict, -_dot_nt(dRb[h], vnb[h]), 0.0) for h in heads]
            dPG = [(dP[h] * gam[h]).astype(BF16) for h in heads]
            dMG = [(dM[h] * gam[h]).astype(BF16) for h in heads]
            E = [_dot_nt(vnb[h], dSb[h]) for h in heads]
            dKb = [_dot(dMG[h], K[h]) for h in heads]
            dc_all = jnp.zeros((C, LANES), F32)
            db_all = jnp.zeros((C, LANES), F32)
            for h in heads:
                dq_ref[rs, _hs(h)] = gcol[h] * _dot_nt(dO[h], Sb[h]) + _dot(dPG[h], K[h])
                dk_ref[rs, _hs(h)] = (_dot_tn(dPG[h], Q[h]) + _dot_tn(dMG[h], Kb[h]) + bcol[h] * dKb[h]
                                      - bg[h] * _dot_nt(dRb[h], Sb[h]) + dcol[h] * E[h])
                dv_ref[rs, _hs(h)] = bcol[h] * dR[h]
                dbeta = rsum(dKb[h] * K[h]) + rsum(dR[h] * (V[h] - gcol[h] * KS[h]))
                X = dP[h] * P[h] + dM[h] * M[h]
                ddel = rsum(K[h] * E[h]) * dcol[h]
                colsum = _dotf(X, ones, _TN)[:, 0:1]
                dc = (rsum(X) - colsum + gcol[h] * rsum(dO[h].astype(F32) * QS[h]) - bg[h] * rsum(dR[h] * KS[h])
                      - ddel)
                last = (jnp.sum(ddel, axis=0, keepdims=True)
                        + glast[h] * jnp.sum(rsum(dSn[h] * S[h]), axis=0, keepdims=True))
                dc_all = dc_all + jnp.where(lane == h, dc + jnp.where(rowl == C - 1, last, 0.0), 0.0)
                db_all = db_all + jnp.where(lane == N_HEAD + h, dbeta, 0.0)
            dgb_ref[rs, :] = _dotf(upper, dc_all) + db_all
            dSn = dS_new
        for h in heads:
            ds_ref[h] = dSn[h]
        dgn_ref[...] += dgn

    nb = nc // per
    cspec = lambda w, cb=0: pl.BlockSpec((per * C, w), lambda n: (nb - 1 - n, cb))
    hist = lambda a, b: pl.BlockSpec((per, N_HEAD, a, b), lambda n: (nb - 1 - n, 0, 0, 0))
    return pl.pallas_call(
        body,
        out_shape=(jax.ShapeDtypeStruct((T, BR_W), F32),) * 3 + (
            jax.ShapeDtypeStruct((T, LANES), F32), jax.ShapeDtypeStruct((T, BR_W), BF16),
            jax.ShapeDtypeStruct((1, D_HEAD), F32)),
        grid=(nb,),
        in_specs=[cspec(BR_W), cspec(BR_W), cspec(BR_W), cspec(LANES), cspec(BR_W, zcol), _full((1, D_HEAD)),
                  cspec(BR_W), hist(D_HEAD, D_HEAD), hist(C, C), cspec(BR_W), cspec(BR_W)],
        out_specs=(cspec(BR_W), cspec(BR_W), cspec(BR_W), cspec(LANES), cspec(BR_W), _full((1, D_HEAD))),
        scratch_shapes=[pltpu.VMEM((N_HEAD, D_HEAD, D_HEAD), F32)], name="gdn_chunk_bwd",
        compiler_params=_cp("arbitrary"))(q, k, v, gb, proj, gnorm, oraw, shist, tinv_all, vn_all, dog)


SB_COL = SB_OFF // BR_W
SB_SCALE = D_HEAD ** -0.5


def _sb_pre(proj, gq, gk):
    T = proj.shape[0]
    tm = min(TM, T)

    def body(xq_ref, xk_ref, xv_ref, gq_ref, gk_ref, q_ref, k_ref, v_ref):
        for h in range(N_HEAD):
            for x_ref, g_ref, ref, scale in ((xq_ref, gq_ref, q_ref, SB_SCALE), (xk_ref, gk_ref, k_ref, 1.0)):
                xh = x_ref[:, _hs(h)].astype(F32)
                r = lax.rsqrt(jnp.mean(xh * xh, axis=-1, keepdims=True) + EPS)
                ref[:, _hs(h)] = (xh * (r * scale) * g_ref[...]).astype(BF16)
        v_ref[...] = xv_ref[...]

    return pl.pallas_call(
        body, out_shape=(jax.ShapeDtypeStruct((T, BR_W), BF16),) * 3, grid=(T // tm,),
        in_specs=[_rowspec(tm, BR_W, SB_COL), _rowspec(tm, BR_W, SB_COL + 1), _rowspec(tm, BR_W, SB_COL + 2),
                  _full((1, D_HEAD)), _full((1, D_HEAD))],
        out_specs=(_rowspec(tm, BR_W),) * 3, name="sb_pre", compiler_params=_cp("parallel"))(proj, proj, proj, gq, gk)


def _sb_pre_bwd(proj, gq, gk, dq, dk, dv):
    T = proj.shape[0]
    tm = min(TM, T)

    def body(xq_ref, xk_ref, gq_ref, gk_ref, dq_ref, dk_ref, dv_ref, dx_ref, dgq_ref, dgk_ref):
        i = pl.program_id(0)

        @pl.when(i == 0)
        def _():
            dgq_ref[...] = jnp.zeros_like(dgq_ref)
            dgk_ref[...] = jnp.zeros_like(dgk_ref)

        for off, x_ref, g_ref, d_ref, dg_ref, scale in ((0, xq_ref, gq_ref, dq_ref, dgq_ref, SB_SCALE),
                                                        (BR_W, xk_ref, gk_ref, dk_ref, dgk_ref, 1.0)):
            dg = jnp.zeros((1, D_HEAD), F32)
            for h in range(N_HEAD):
                xh = x_ref[:, _hs(h)].astype(F32)
                r = lax.rsqrt(jnp.mean(xh * xh, axis=-1, keepdims=True) + EPS)
                y = xh * r
                dn = d_ref[:, _hs(h)] * scale
                dg = dg + jnp.sum(dn * y, axis=0, keepdims=True)
                dy = dn * g_ref[...]
                dx_ref[:, off + h * D_HEAD:off + (h + 1) * D_HEAD] = (
                    r * (dy - y * jnp.mean(dy * y, axis=-1, keepdims=True))).astype(BF16)
            dg_ref[...] += dg
        dx_ref[:, 2 * BR_W:] = dv_ref[...].astype(BF16)

    return pl.pallas_call(
        body,
        out_shape=(jax.ShapeDtypeStruct((T, 3 * BR_W), BF16), jax.ShapeDtypeStruct((1, D_HEAD), F32),
                   jax.ShapeDtypeStruct((1, D_HEAD), F32)),
        grid=(T // tm,),
        in_specs=[_rowspec(tm, BR_W, SB_COL), _rowspec(tm, BR_W, SB_COL + 1), _full((1, D_HEAD)), _full((1, D_HEAD)),
                  _rowspec(tm, BR_W), _rowspec(tm, BR_W), _rowspec(tm, BR_W)],
        out_specs=(_rowspec(tm, 3 * BR_W), _full((1, D_HEAD)), _full((1, D_HEAD))), name="sb_pre_bwd",
        compiler_params=_cp("arbitrary"))(proj, proj, gq, gk, dq, dk, dv)


def _sb_scores(q_ref, k_ref, lead):
    z = _dot_nt(q_ref[...], k_ref[...])
    zc = jnp.minimum(z, 30.0)
    sp = jnp.log(1.0 + jnp.exp(zc)) + (z - zc)
    if lead is None:
        return z, sp, None
    mask = _iota(z.shape, 1) - _iota(z.shape, 0) < -lead
    return z, jnp.where(mask, sp, 0.0), mask


def _sb_blocks(T):
    bq = min(SB_BLK_Q, T)
    bk = min(SB_BLK, bq)
    return bq, bk, min(SB_W, bk), T // bq, bq // bk


def _sb_fwd(sq, sk, sv):
    T = sq.shape[0]
    bq, bk, w, nq, ratio = _sb_blocks(T)
    nsub = bk // w

    def body(qi_ref, kj_ref, q_ref, k_ref, v_ref, o_ref, lt_ref, acc_ref, r_ref):
        t = pl.program_id(1)
        qi, kj = qi_ref[t], kj_ref[t]

        @pl.when(kj == ratio * qi + ratio - 1)
        def _():
            acc_ref[...] = jnp.zeros_like(acc_ref)
            r_ref[...] = jnp.zeros_like(r_ref)

        def block(masked):
            z, sp, mask = _sb_scores(q_ref, k_ref, kj * bk - qi * bq if masked else None)
            after = (_iota((w, w), 0) > _iota((w, w), 1)).astype(BF16)
            r = r_ref[...]
            acc = acc_ref[...]
            for sb in reversed(range(nsub)):
                cs = slice(sb * w, (sb + 1) * w)
                sps = sp[:, cs]
                a = jnp.exp(z[:, cs] - sps - _dot(sps, after) - r)
                if masked:
                    a = jnp.where(mask[:, cs], a, 0.0)
                acc = acc + _dot(a, v_ref[cs, :])
                r = r + jnp.sum(sps, axis=-1, keepdims=True)
            acc_ref[...] = acc
            r_ref[...] = r

        pl.when(kj >= ratio * qi)(functools.partial(block, True))
        pl.when(kj < ratio * qi)(functools.partial(block, False))

        @pl.when(kj == 0)
        def _():
            o_ref[...] = acc_ref[...].astype(BF16)
            lt_ref[0] = r_ref[...]

    pairs = [(i, j) for i in range(nq) for j in range(ratio * (i + 1) - 1, -1, -1)]
    qi_tab, kj_tab = (jnp.asarray(np.array(c, np.int32)) for c in zip(*pairs))
    qspec = pl.BlockSpec((bq, D_HEAD), lambda h, t, qi, kj: (qi[t], h))
    kspec = pl.BlockSpec((bk, D_HEAD), lambda h, t, qi, kj: (kj[t], h))
    return pl.pallas_call(
        body,
        out_shape=(jax.ShapeDtypeStruct((T, BR_W), BF16), jax.ShapeDtypeStruct((N_HEAD, T, 1), F32)),
        grid_spec=pltpu.PrefetchScalarGridSpec(
            num_scalar_prefetch=2, grid=(N_HEAD, len(pairs)), in_specs=[qspec, kspec, kspec],
            out_specs=(qspec, pl.BlockSpec((1, bq, 1), lambda h, t, qi, kj: (h, qi[t], 0))),
            scratch_shapes=[pltpu.VMEM((bq, D_HEAD), F32), pltpu.VMEM((bq, 1), F32)]),
        name="sb_fwd", compiler_params=_cp("parallel", "arbitrary"))(qi_tab, kj_tab, sq, sk, sv)


def _sb_bwd(sq, sk, sv, ltot, do):
    T = sq.shape[0]
    bq, bk, w, nq, ratio = _sb_blocks(T)
    nsub = bk // w

    def body(qi_ref, kj_ref, q_ref, k_ref, v_ref, lt_ref, do_ref, dq_ref, dk_ref, dv_ref, acc_ref, p_ref, g_ref):
        t = pl.program_id(1)
        qi, kj = qi_ref[t], kj_ref[t]

        @pl.when(t == 0)
        def _():
            dk_ref[...] = jnp.zeros_like(dk_ref)
            dv_ref[...] = jnp.zeros_like(dv_ref)

        @pl.when(kj == 0)
        def _():
            acc_ref[...] = jnp.zeros_like(acc_ref)
            p_ref[...] = lt_ref[0]
            g_ref[...] = jnp.zeros_like(g_ref)

        def block(masked):
            z, sp, mask = _sb_scores(q_ref, k_ref, kj * bk - qi * bq if masked else None)
            d_a = _dot_nt(do_ref[...], v_ref[...])
            after = (_iota((w, w), 0) > _iota((w, w), 1)).astype(BF16)
            before = (_iota((w, w), 0) < _iota((w, w), 1)).astype(BF16)
            rest = p_ref[...]
            hg = g_ref[...]
            acc = acc_ref[...]
            base = pl.multiple_of(kj * bk, bk)
            for sb in range(nsub):
                cs = slice(sb * w, (sb + 1) * w)
                sps, zs = sp[:, cs], z[:, cs]
                rest = rest - jnp.sum(sps, axis=-1, keepdims=True)
                a = jnp.exp(zs - sps - _dot(sps, after) - rest)
                if masked:
                    a = jnp.where(mask[:, cs], a, 0.0)
                g = a * d_a[:, cs]
                sig = jnp.exp(zs - sps)
                dz = g - sig * (g + (hg + _dot(g, before)))
                if masked:
                    dz = jnp.where(mask[:, cs], dz, 0.0)
                dz = dz.astype(BF16)
                rows = pl.ds(base + sb * w, w)
                dv_ref[rows, :] += _dot_tn(a, do_ref[...])
                dk_ref[rows, :] += _dot_tn(dz, q_ref[...])
                acc = acc + _dot(dz, k_ref[cs, :])
                hg = hg + jnp.sum(g, axis=-1, keepdims=True)
            acc_ref[...] = acc
            p_ref[...] = rest
            g_ref[...] = hg

        pl.when(kj >= ratio * qi)(functools.partial(block, True))
        pl.when(kj < ratio * qi)(functools.partial(block, False))

        @pl.when(kj == ratio * qi + ratio - 1)
        def _():
            dq_ref[...] = acc_ref[...]

    pairs = [(i, j) for i in range(nq) for j in range(ratio * (i + 1))]
    qi_tab, kj_tab = (jnp.asarray(np.array(c, np.int32)) for c in zip(*pairs))
    qspec = pl.BlockSpec((bq, D_HEAD), lambda h, t, qi, kj: (qi[t], h))
    kspec = pl.BlockSpec((bk, D_HEAD), lambda h, t, qi, kj: (kj[t], h))
    full = pl.BlockSpec((T, D_HEAD), lambda h, t, qi, kj: (0, h))
    return pl.pallas_call(
        body, out_shape=(jax.ShapeDtypeStruct((T, BR_W), F32),) * 3,
        grid_spec=pltpu.PrefetchScalarGridSpec(
            num_scalar_prefetch=2, grid=(N_HEAD, len(pairs)),
            in_specs=[qspec, kspec, kspec, pl.BlockSpec((1, bq, 1), lambda h, t, qi, kj: (h, qi[t], 0)), qspec],
            out_specs=(qspec, full, full),
            scratch_shapes=[pltpu.VMEM((bq, D_HEAD), F32), pltpu.VMEM((bq, 1), F32), pltpu.VMEM((bq, 1), F32)]),
        name="sb_bwd", compiler_params=_cp("arbitrary", "arbitrary"))(qi_tab, kj_tab, sq, sk, sv, ltot, do)


def _mem_kv(mem, gm, w_kv, gk):
    def body(mem_ref, gm_ref, w_ref, gk_ref, mn_ref, kv_ref, kh_ref, vm_ref):
        mv = mem_ref[...]
        r = lax.rsqrt(jnp.mean(mv * mv, axis=-1, keepdims=True) + EPS)
        mn = (mv * r * gm_ref[...]).astype(BF16)
        mn_ref[...] = mn
        kv = lax.dot_general(mn, w_ref[...], _NN, preferred_element_type=F32)
        kv_ref[...] = kv
        for h in range(N_HEAD):
            kh = kv[:, _hs(h)]
            rk = lax.rsqrt(jnp.mean(kh * kh, axis=-1, keepdims=True) + EPS)
            kh_ref[:, _hs(h)] = (kh * rk * gk_ref[...]).astype(BF16)
        vm_ref[...] = kv[:, BR_W:].astype(BF16)

    return pl.pallas_call(
        body,
        out_shape=(jax.ShapeDtypeStruct((N_MEM, D_MODEL), BF16), jax.ShapeDtypeStruct((N_MEM, 2 * BR_W), F32),
                   jax.ShapeDtypeStruct((N_MEM, BR_W), BF16), jax.ShapeDtypeStruct((N_MEM, BR_W), BF16)),
        name="mem_kv", compiler_params=_cp())(mem, gm, w_kv, gk)


def _mem_q(x_ref, gq_ref, h):
    xh = x_ref[:, _hs(h)].astype(F32)
    r = lax.rsqrt(jnp.mean(xh * xh, axis=-1, keepdims=True) + EPS)
    return r, xh * r


def _mem_probs(qn, kh):
    s = _dot_nt(qn, kh) * (D_HEAD ** -0.5)
    e = jnp.exp(s - jnp.max(s, axis=-1, keepdims=True))
    return e / jnp.sum(e, axis=-1, keepdims=True)


def _mem_fwd(proj, kh, vm, gq):
    T = proj.shape[0]
    tm = min(TM, T)

    def body(x_ref, kh_ref, vm_ref, gq_ref, o_ref):
        for h in range(N_HEAD):
            _, y = _mem_q(x_ref, gq_ref, h)
            p = _mem_probs((y * gq_ref[...]).astype(BF16), kh_ref[:, _hs(h)])
            o_ref[:, _hs(h)] = _dot(p, vm_ref[:, _hs(h)]).astype(BF16)

    return pl.pallas_call(
        body, out_shape=jax.ShapeDtypeStruct((T, BR_W), BF16), grid=(T // tm,),
        in_specs=[_rowspec(tm, BR_W, MEMQ_OFF // BR_W), _full((N_MEM, BR_W)), _full((N_MEM, BR_W)),
                  _full((1, D_HEAD))],
        out_specs=_rowspec(tm, BR_W), name="mem_fwd", compiler_params=_cp("parallel"))(proj, kh, vm, gq)


def _mem_bwd(proj, kh, vm, gq, do):
    T = proj.shape[0]
    tm = min(TM, T)

    def body(x_ref, kh_ref, vm_ref, gq_ref, do_ref, dx_ref, dkh_ref, dvm_ref, dgq_ref):
        i = pl.program_id(0)

        @pl.when(i == 0)
        def _():
            dkh_ref[...] = jnp.zeros_like(dkh_ref)
            dvm_ref[...] = jnp.zeros_like(dvm_ref)
            dgq_ref[...] = jnp.zeros_like(dgq_ref)

        dg = jnp.zeros((1, D_HEAD), F32)
        for h in range(N_HEAD):
            r, y = _mem_q(x_ref, gq_ref, h)
            qn = (y * gq_ref[...]).astype(BF16)
            p = _mem_probs(qn, kh_ref[:, _hs(h)])
            dov = do_ref[:, _hs(h)]
            dp = _dot_nt(dov, vm_ref[:, _hs(h)])
            ds = p * (dp - jnp.sum(dp * p, axis=-1, keepdims=True)) * (D_HEAD ** -0.5)
            dqn = _dot(ds, kh_ref[:, _hs(h)])
            dkh_ref[:, _hs(h)] += _dot_tn(ds, qn)
            dvm_ref[:, _hs(h)] += _dot_tn(p, dov)
            dg = dg + jnp.sum(dqn * y, axis=0, keepdims=True)
            dy = dqn * gq_ref[...]
            dx_ref[:, _hs(h)] = (r * (dy - y * jnp.mean(dy * y, axis=-1, keepdims=True))).astype(BF16)
        dgq_ref[...] += dg

    return pl.pallas_call(
        body,
        out_shape=(jax.ShapeDtypeStruct((T, BR_W), BF16), jax.ShapeDtypeStruct((N_MEM, BR_W), F32),
                   jax.ShapeDtypeStruct((N_MEM, BR_W), F32), jax.ShapeDtypeStruct((1, D_HEAD), F32)),
        grid=(T // tm,),
        in_specs=[_rowspec(tm, BR_W, MEMQ_OFF // BR_W), _full((N_MEM, BR_W)), _full((N_MEM, BR_W)),
                  _full((1, D_HEAD)), _rowspec(tm, BR_W)],
        out_specs=(_rowspec(tm, BR_W), _full((N_MEM, BR_W)), _full((N_MEM, BR_W)), _full((1, D_HEAD))),
        name="mem_bwd", compiler_params=_cp("arbitrary"))(proj, kh, vm, gq, do)


def _mem_kv_bwd(mem, gm, w_kv, gk, kv, mn, dkh, dvm):
    def body(mem_ref, gm_ref, w_ref, gk_ref, kv_ref, mn_ref, dkh_ref, dvm_ref, dw_ref, dgm_ref, dgk_ref, dkv_ref):
        dgk = jnp.zeros((1, D_HEAD), F32)
        for h in range(N_HEAD):
            kh = kv_ref[:, _hs(h)]
            r = lax.rsqrt(jnp.mean(kh * kh, axis=-1, keepdims=True) + EPS)
            y = kh * r
            dn = dkh_ref[:, _hs(h)]
            dgk = dgk + jnp.sum(dn * y, axis=0, keepdims=True)
            dy = dn * gk_ref[...]
            dkv_ref[:, _hs(h)] = (r * (dy - y * jnp.mean(dy * y, axis=-1, keepdims=True))).astype(BF16)
        dkv_ref[:, BR_W:] = dvm_ref[...].astype(BF16)
        dgk_ref[...] = dgk
        dkv = dkv_ref[...]
        dw_ref[...] = lax.dot_general(mn_ref[...], dkv, _TN, preferred_element_type=F32)
        dmn = lax.dot_general(dkv, w_ref[...], _NT, preferred_element_type=F32)
        mv = mem_ref[...]
        memn = mv * lax.rsqrt(jnp.mean(mv * mv, axis=-1, keepdims=True) + EPS)
        dgm_ref[...] = jnp.sum(dmn * memn, axis=0, keepdims=True)

    return pl.pallas_call(
        body,
        out_shape=(jax.ShapeDtypeStruct((D_MODEL, 2 * BR_W), F32), jax.ShapeDtypeStruct((1, D_MODEL), F32),
                   jax.ShapeDtypeStruct((1, D_HEAD), F32)),
        scratch_shapes=[pltpu.VMEM((N_MEM, 2 * BR_W), BF16)], name="mem_kv_bwd",
        compiler_params=_cp())(mem, gm, w_kv, gk, kv, mn, dkh, dvm)


def _merge_fwd(og, osb, om, proj, wg, ws, wm):
    T = og.shape[0]
    tm = min(TM, T)

    def body(og_ref, os_ref, om_ref, g0, g1, g2, wg_ref, ws_ref, wm_ref, mix_ref, yg_ref, ys_ref, ym_ref):
        mix = jnp.zeros((tm, D_MODEL), F32)
        for o_ref, gl_ref, w_ref, y_ref in ((og_ref, g0, wg_ref, yg_ref), (os_ref, g1, ws_ref, ys_ref),
                                            (om_ref, g2, wm_ref, ym_ref)):
            y = lax.dot_general(o_ref[...], w_ref[...], _NN, preferred_element_type=F32)
            y_ref[...] = y.astype(BF16)
            mix = mix + _sigmoid(gl_ref[...].astype(F32)) * y
        mix_ref[...] = mix.astype(BF16)

    br = _rowspec(tm, BR_W)
    wspec = _full((BR_W, D_MODEL))
    out = _rowspec(tm, D_MODEL)
    gates = [_rowspec(tm, D_MODEL, GATE_COL + b) for b in range(3)]
    return pl.pallas_call(
        body, out_shape=(jax.ShapeDtypeStruct((T, D_MODEL), BF16),) * 4, grid=(T // tm,),
        in_specs=[br, br, br, *gates, wspec, wspec, wspec],
        out_specs=(out,) * 4, name="merge_fwd",
        compiler_params=_cp("parallel"))(og, osb, om, proj, proj, proj, wg, ws, wm)


def _merge_bwd(dmix, proj, ys, os_, ws):
    T = dmix.shape[0]
    tm = min(TM, T)

    def body(dmix_ref, g0, g1, g2, y0, y1, y2, o0, o1, o2, w0, w1, w2, dgl_ref, do0, do1, do2, dw0, dw1, dw2):
        i = pl.program_id(0)
        dm = dmix_ref[...].astype(F32)
        for b, (gl_ref, y_ref, o_ref, w_ref, do_ref, dw_ref) in enumerate((
                (g0, y0, o0, w0, do0, dw0), (g1, y1, o1, w1, do1, dw1), (g2, y2, o2, w2, do2, dw2))):
            gate = _sigmoid(gl_ref[...].astype(F32))
            dgl_ref[:, b * D_MODEL:(b + 1) * D_MODEL] = (dm * y_ref[...].astype(F32) * gate * (1.0 - gate)).astype(BF16)
            dy = (gate * dm).astype(BF16)
            do_ref[...] = lax.dot_general(dy, w_ref[...], _NT, preferred_element_type=F32).astype(BF16)
            _accum(dw_ref, i == 0, lax.dot_general(dy, o_ref[...], _TN, preferred_element_type=F32))

    br = _rowspec(tm, BR_W)
    wide = _rowspec(tm, D_MODEL)
    wspec = _full((BR_W, D_MODEL))
    wtspec = _full((D_MODEL, BR_W))
    gates = [_rowspec(tm, D_MODEL, GATE_COL + b) for b in range(3)]
    return pl.pallas_call(
        body,
        out_shape=(jax.ShapeDtypeStruct((T, 3 * D_MODEL), BF16),) + (jax.ShapeDtypeStruct((T, BR_W), BF16),) * 3
        + (jax.ShapeDtypeStruct((D_MODEL, BR_W), F32),) * 3,
        grid=(T // tm,),
        in_specs=[wide, *gates, wide, wide, wide, br, br, br, wspec, wspec, wspec],
        out_specs=(_rowspec(tm, 3 * D_MODEL), br, br, br, wtspec, wtspec, wtspec), name="merge_bwd",
        compiler_params=_cp("arbitrary"))(dmix, proj, proj, proj, *ys, *os_, *ws)


def _loss(y, tgt):
    T, dm = y.shape
    tm = min(TM, T)

    def body(y_ref, t_ref, dy_ref, dyb_ref, sq_ref):
        err = y_ref[...] - t_ref[...]
        dy = err * (1.0 / dm)
        dy_ref[...] = dy
        dyb_ref[...] = dy.astype(BF16)
        _accum(sq_ref, pl.program_id(0) == 0, jnp.sum(err * err, axis=0, keepdims=True))

    return pl.pallas_call(
        body,
        out_shape=(jax.ShapeDtypeStruct((T, dm), F32), jax.ShapeDtypeStruct((T, dm), BF16),
                   jax.ShapeDtypeStruct((1, dm), F32)),
        grid=(T // tm,), in_specs=[_rowspec(tm, dm), _rowspec(tm, dm)],
        out_specs=(_rowspec(tm, dm), _rowspec(tm, dm), _full((1, dm))), name="loss",
        compiler_params=_cp("arbitrary"))(y, tgt)


def _local_step(x, mem, tgt, W, P, dist=None):
    w_in = W["w_in"]
    w_main = jnp.concatenate([w_in[:, :SB_OFF], w_in[:, SB_OFF + 8:]], axis=1)
    w_ab = jnp.pad(w_in[:, SB_OFF:SB_OFF + 8], ((0, 0), (0, LANES - 8)))
    avec = jnp.pad(jnp.concatenate([P["a_log"], P["dt_bias"]], axis=0), ((0, 0), (0, LANES - N_HEAD)))

    h = _rms_fwd(x, P["norm1_g"], "rms1")
    if dist is None:
        proj = _mm(h, w_main, "nn", BF16, "in_proj")
    else:
        proj, gathered = _mm(h, w_main, "nn", BF16, "in_proj", comm=dist.gather_rest())
        rest, conv_w = dist.weights_from(gathered)
        W, P = {**W, **rest}, {**P, "conv_w": conv_w}
    wbr = (W["w_br_gdn"], W["w_br_sb"], W["w_br_mem"])
    ab = _mm(h, w_ab, "nn", F32, "in_proj_ab")
    q, k, v, gb = _gdn_pre(proj, P["conv_w"], ab, avec)
    tinv = _gdn_inv(k, gb)
    og, oraw, shist, vn = _gdn_fwd(q, k, v, gb, proj, P["gdn_norm_g"], tinv)
    sq, sk, sv = _sb_pre(proj, P["sb_q_norm_g"], P["sb_k_norm_g"])
    osb, ltot = _sb_fwd(sq, sk, sv)
    mn, kv, kh, vm = _mem_kv(mem, P["mem_norm_g"], W["w_mem_kv"], P["mem_k_norm_g"])
    om = _mem_fwd(proj, kh, vm, P["mem_q_norm_g"])
    mix, yg, ys, ym = _merge_fwd(og, osb, om, proj, *wbr)
    x1 = _mm(mix, W["w_o"], "nn", F32, "out_proj", extra=x, epi=_epi_add)
    h2 = _rms_fwd(x1, P["norm2_g"], "rms2")
    u = _mm(h2, W["w_up"], "nn", BF16, "mlp_up")
    y = _mm(u, W["w_down"], "nn", F32, "mlp_down", a_fn=_relu2, extra=x1, epi=_epi_add)
    dy, dyb, sq_err = _loss(y, tgt)

    G = {}
    du = _mm(dyb, W["w_down"], "nt", BF16, "d_mlp_act", extra=u, epi=_epi_drelu2)
    G["w_down"] = _mm(u, dyb, "tn", F32, "dw_down", a_fn=_relu2)
    G["w_up"] = _mm(du, h2, "tn", F32, "dw_up")
    dh2 = _mm(du, W["w_up"], "nt", F32, "d_h2")
    dx1, dx1b, G["norm2_g"] = _rms_bwd(dh2, x1, P["norm2_g"], dy, "rms2_bwd")
    dmix = _mm(dx1b, W["w_o"], "nt", BF16, "d_mix")
    G["w_o"] = _mm(mix, dx1b, "tn", F32, "dw_o")
    dgates, dog, dosb, dom, G["w_br_gdn"], G["w_br_sb"], G["w_br_mem"] = _merge_bwd(
        dmix, proj, (yg, ys, ym), (og, osb, om), wbr)
    dq, dk, dv, dgb, dz, G["gdn_norm_g"] = _gdn_bwd(q, k, v, gb, proj, P["gdn_norm_g"], oraw, shist, tinv, vn, dog)
    dxc, dab, G["conv_w"], dav = _gdn_pre_bwd(proj, P["conv_w"], ab, avec, dq, dk, dv, dgb)
    dqkv = _conv_bwd(dxc, P["conv_w"])
    G["a_log"], G["dt_bias"] = dav[0:1, :N_HEAD], dav[1:2, :N_HEAD]
    dsq, dsk, dsv = _sb_bwd(sq, sk, sv, ltot, dosb)
    dsb, G["sb_q_norm_g"], G["sb_k_norm_g"] = _sb_pre_bwd(proj, P["sb_q_norm_g"], P["sb_k_norm_g"], dsq, dsk, dsv)
    dmemq, dkh, dvm, G["mem_q_norm_g"] = _mem_bwd(proj, kh, vm, P["mem_q_norm_g"], dom)
    G["w_mem_kv"], G["mem_norm_g"], G["mem_k_norm_g"] = _mem_kv_bwd(
        mem, P["mem_norm_g"], W["w_mem_kv"], P["mem_k_norm_g"], kv, mn, dkh, dvm)
    dproj = jnp.concatenate([dqkv, dz, dsb, dmemq, dgates], axis=1)
    dw_ab = _mm(dab, h, "tn", F32, "dw_in_ab")
    if dist is None:
        dw_main = _mm(dproj, h, "tn", F32, "dw_in")
    else:
        early = [n for n, _, _ in BIG if n != "w_in"]
        dw_main, landed = _mm(dproj, h, "tn", F32, "dw_in", comm=dist.scatter(G, early, "early"))
        dist.collect(early, landed)
    G["w_in"] = jnp.concatenate([dw_main[:SB_OFF], dw_ab[:8], dw_main[SB_OFF:]], axis=0)
    if dist is None:
        dh = _mm(dproj, w_main, "nt", F32, "d_h")
    else:
        dh, landed = _mm(dproj, w_main, "nt", F32, "d_h", comm=dist.scatter(G, ["w_in"], "late"))
        dist.collect(["w_in"], landed)
    dh = _mm(dab, w_ab, "nt", F32, "d_h_ab", extra=dh, epi=_epi_add)
    dx, _, G["norm1_g"] = _rms_bwd(dh, x, P["norm1_g"], dx1, "rms1_bwd")
    return sq_err, dx, G


def _comm(name, ins, out_shapes, plan):
    n_in, n_out = len(ins), len(out_shapes)
    probe = plan([None] * n_in, [None] * n_out, 0, 0, 0, dry=True)
    n_copy = probe

    def body(*refs):
        in_refs, out_refs = refs[:n_in], refs[n_in:n_in + n_out]
        send_sems, recv_sems = refs[n_in + n_out:]
        x, y, c = lax.axis_index("x"), lax.axis_index("y"), lax.axis_index("c")
        copies = []
        for k, (src, dst, dev) in enumerate(plan(in_refs, out_refs, x, y, c, dry=False)):
            if dev is None:
                cp = pltpu.make_async_copy(src, dst, send_sems.at[k])
            else:
                cp = pltpu.make_async_remote_copy(src_ref=src, dst_ref=dst, send_sem=send_sems.at[k],
                                                  recv_sem=recv_sems.at[k], device_id=dev, device_id_type=MESH)
            cp.start()
            copies.append(cp)
        for cp in copies:
            cp.wait()

    return pl.pallas_call(
        body, out_shape=tuple(out_shapes), in_specs=[HBM] * n_in, out_specs=tuple([HBM] * n_out),
        scratch_shapes=[pltpu.SemaphoreType.DMA((n_copy,)), pltpu.SemaphoreType.DMA((n_copy,))], name=name)(*ins)


def _other_chips(x, y):
    return ((1 - x, y), (x, 1 - y), (1 - x, 1 - y))


def _gather_plan(parts, direct=()):
    n, every = len(parts), list(parts) + list(direct)

    def copies(ins, outs, send, recv, scratch):
        x, y, c = lax.axis_index("x"), lax.axis_index("y"), lax.axis_index("c")
        me = 2 * x + y
        chips = _other_chips(x, y)
        local_sems, staged = scratch[0], scratch[1:]

        def remote(src, dst, k, dev):
            return pltpu.make_async_remote_copy(src_ref=src, dst_ref=dst, send_sem=send.at[k], recv_sem=recv.at[k],
                                                device_id=dev, device_id_type=MESH)

        def half(p, ci):
            hr = ins[p].shape[0] // 2
            return pl.ds(pl.multiple_of(ci * hr, 16), hr)

        sent = [remote(ins[p].at[half(p, c)], outs[p].at[me, half(p, c)], 6 * p + f, (px, py, c))
                for p in range(n) for f, (px, py) in enumerate(chips)]
        sent += [remote(ins[p], outs[p].at[me], 6 * n + 3 * (p - n) + f, (px, py, c))
                 for p in range(n, len(every)) for f, (px, py) in enumerate(chips)]
        landed = [outs[p].at[2 * px + py, half(p, c)] for p in range(n) for px, py in chips]
        passed = [remote(landed[3 * p + f], landed[3 * p + f], 6 * p + 3 + f, (x, y, 1 - c))
                  for p in range(n) for f in range(3)]
        loads = [pltpu.make_async_copy(ins[p], staged[p], local_sems.at[2 * p]) for p in range(len(every))]
        stores = [pltpu.make_async_copy(staged[p], outs[p].at[me], local_sems.at[2 * p + 1]) for p in range(len(every))]
        return sent, passed, loads, stores

    def start(*refs):
        sent, _, loads, _ = copies(*refs)
        for cp in loads + sent:
            cp.start()

    def mid(*refs):
        sent, passed, loads, stores = copies(*refs)
        for ld, st in zip(loads, stores):
            ld.wait()
            st.start()
        for p in range(n):
            for f in range(3):
                sent[3 * p + f].wait_recv()
                passed[3 * p + f].start()

    def finish(*refs):
        sent, passed, _, stores = copies(*refs)
        for cp in sent[:3 * n]:
            cp.wait_send()
        for cp in passed + sent[3 * n:] + stores:
            cp.wait()

    return _Hosted(every, [jax.ShapeDtypeStruct((4,) + p.shape, p.dtype) for p in every], 6 * n + 3 * len(direct),
                   start, finish, mid,
                   [pltpu.SemaphoreType.DMA((2 * len(every),))] + [pltpu.VMEM(p.shape, p.dtype) for p in every])


def _scatter_plan(pairs):
    def copies(ins, outs, send, recv, scratch):
        x, y, c = lax.axis_index("x"), lax.axis_index("y"), lax.axis_index("c")
        me = 2 * x + y
        return [pltpu.make_async_remote_copy(src_ref=src.at[2 * px + py], dst_ref=dst.at[me], send_sem=send.at[3 * p + f],
                                             recv_sem=recv.at[3 * p + f], device_id=(px, py, c), device_id_type=MESH)
                for p, (src, dst) in enumerate(zip(ins, outs)) for f, (px, py) in enumerate(_other_chips(x, y))]

    def start(*refs):
        for cp in copies(*refs):
            cp.start()

    def finish(*refs):
        for cp in copies(*refs):
            cp.wait()

    return _Hosted(pairs, [jax.ShapeDtypeStruct(a.shape, a.dtype) for a in pairs], 3 * len(pairs), start, finish)


def _run_hosted(comm, name):
    n_in, n_out = len(comm.ins), len(comm.out_shapes)

    def body(*refs):
        args = (refs[:n_in], refs[n_in:n_in + n_out], refs[n_in + n_out], refs[n_in + n_out + 1], refs[n_in + n_out + 2:])
        comm.start(*args)
        if comm.mid is not None:
            comm.mid(*args)
        comm.finish(*args)

    sems = [pltpu.SemaphoreType.DMA((comm.n_sems,)), pltpu.SemaphoreType.DMA((comm.n_sems,))]
    return list(pl.pallas_call(
        body, out_shape=tuple(comm.out_shapes), in_specs=[HBM] * n_in, out_specs=tuple([HBM] * n_out),
        scratch_shapes=sems + comm.scratch, name=name, compiler_params=_cp())(*comm.ins))


def _swap_halves(slabs, name):
    n = len(slabs)

    def body(*refs):
        ins, outs = refs[:n], refs[n:2 * n]
        send, recv = refs[2 * n:]
        x, y, c = lax.axis_index("x"), lax.axis_index("y"), lax.axis_index("c")
        other = (x, y, 1 - c)
        for p in range(n):
            for j in range(4):
                pltpu.make_async_remote_copy(src_ref=ins[p].at[j, 1 - c], dst_ref=outs[p].at[j], send_sem=send.at[p],
                                             recv_sem=recv.at[p], device_id=other, device_id_type=MESH).start()
        for p in range(n):
            pltpu.make_async_remote_copy(src_ref=outs[p], dst_ref=outs[p], send_sem=send.at[p], recv_sem=recv.at[p],
                                         device_id=other, device_id_type=MESH).wait()

    shapes = [jax.ShapeDtypeStruct((4,) + s.shape[2:], s.dtype) for s in slabs]
    return pl.pallas_call(
        body, out_shape=tuple(shapes), in_specs=[HBM] * n, out_specs=tuple([HBM] * n),
        scratch_shapes=[pltpu.SemaphoreType.DMA((n,)), pltpu.SemaphoreType.DMA((n,))], name=name)(*slabs)


def _join_halves(both):
    n = len(both)

    def body(*refs):
        bufs = refs[n:2 * n]
        send, recv = refs[2 * n:]
        x, y, c = lax.axis_index("x"), lax.axis_index("y"), lax.axis_index("c")
        copies = []
        for p in range(n):
            cp = pltpu.make_async_remote_copy(src_ref=bufs[p].at[c], dst_ref=bufs[p].at[c], send_sem=send.at[p],
                                              recv_sem=recv.at[p], device_id=(x, y, 1 - c), device_id_type=MESH)
            cp.start()
            copies.append(cp)
        for cp in copies:
            cp.wait()

    return pl.pallas_call(
        body, out_shape=tuple(jax.ShapeDtypeStruct(a.shape, a.dtype) for a in both), in_specs=[HBM] * n,
        out_specs=tuple([HBM] * n), input_output_aliases={p: p for p in range(n)},
        scratch_shapes=[pltpu.SemaphoreType.DMA((n,)), pltpu.SemaphoreType.DMA((n,))], name="grad_join_cores")(*both)


def _gather_all(a, name):
    def plan(ins, outs, x, y, c, dry):
        if dry:
            return 8
        me = 4 * x + 2 * y + c
        copies = [(ins[0], outs[0].at[me], None)]
        for f in range(1, 8):
            peer = (1 - x if f & 4 else x, 1 - y if f & 2 else y, 1 - c if f & 1 else c)
            copies.append((ins[0], outs[0].at[me], peer))
        return copies

    return _comm(name, [a], [jax.ShapeDtypeStruct((8,) + a.shape, a.dtype)], plan)[0]


def _sum_slots(a, name, extra=None):
    n, R, _ = a.shape
    rb = min(ROW_BLK, R)

    def body(*refs):
        a_ref, o_ref = refs[0], refs[-1]
        acc = a_ref[0]
        for s in range(1, n):
            acc = acc + a_ref[s]
        if extra is not None:
            acc = acc + refs[1][...]
        o_ref[...] = acc

    ins = [a] + ([extra] if extra is not None else [])
    in_specs = [pl.BlockSpec((n, rb, LANES), lambda i: (0, i, 0))] + ([_rowspec(rb, LANES)] if extra is not None else [])
    return pl.pallas_call(
        body, out_shape=jax.ShapeDtypeStruct((R, LANES), F32), grid=(R // rb,), in_specs=in_specs,
        out_specs=_rowspec(rb, LANES), name=name, compiler_params=_cp("parallel"))(*ins)


def _pair_sum(slab, theirs, core, name):
    _, _, hr, C = slab.shape

    def body(c_ref, a_ref, b_ref, o_ref):
        o_ref[...] = (a_ref[...] + b_ref[...]).astype(BF16)

    return pl.pallas_call(
        body, out_shape=jax.ShapeDtypeStruct((4, hr, C), BF16),
        grid_spec=pltpu.PrefetchScalarGridSpec(
            num_scalar_prefetch=1, grid=(4,),
            in_specs=[pl.BlockSpec((None, None, hr, C), lambda j, c_ref: (j, c_ref[0], 0, 0)),
                      pl.BlockSpec((None, hr, C), lambda j, c_ref: (j, 0, 0))],
            out_specs=pl.BlockSpec((None, hr, C), lambda j, c_ref: (j, 0, 0))),
        name=name, compiler_params=_cp("parallel"))(core, slab, theirs)


def _chip_sum(recv, pairs, where, name):
    _, hr, C = recv.shape

    def body(w_ref, r_ref, p_ref, o_ref):
        me = w_ref[0]
        o_ref[...] = jnp.zeros_like(o_ref)
        for s in range(4):
            @pl.when(me == s)
            def _():
                o_ref[...] += p_ref[...].astype(F32)

            @pl.when(me != s)
            def _():
                o_ref[...] += r_ref[s].astype(F32)

    return pl.pallas_call(
        body, out_shape=jax.ShapeDtypeStruct((2, hr, C), F32),
        grid_spec=pltpu.PrefetchScalarGridSpec(
            num_scalar_prefetch=1, grid=(1,),
            in_specs=[pl.BlockSpec((4, hr, C), lambda i, w_ref: (0, 0, 0)),
                      pl.BlockSpec((None, hr, C), lambda i, w_ref: (w_ref[0], 0, 0))],
            out_specs=pl.BlockSpec((None, hr, C), lambda i, w_ref: (w_ref[1], 0, 0))),
        name=name, compiler_params=_cp("arbitrary"))(where, recv, pairs)


def _adamw(w, g, m, v, name):
    R, C = w.shape
    rb = min(ADAM_ROWS, R)
    c1 = 1.0 - ADAM_B1 ** ADAM_STEP
    c2 = 1.0 - ADAM_B2 ** ADAM_STEP

    def body(w_ref, g_ref, m_ref, v_ref, d_ref, nm_ref, nv_ref):
        gv = g_ref[...]
        nm = ADAM_B1 * m_ref[...] + (1.0 - ADAM_B1) * gv
        nv = ADAM_B2 * v_ref[...] + (1.0 - ADAM_B2) * (gv * gv)
        d_ref[...] = -ADAM_LR * ((nm / c1) / (jnp.sqrt(nv / c2) + ADAM_EPS) + ADAM_WD * w_ref[...])
        nm_ref[...] = nm
        nv_ref[...] = nv

    spec = _rowspec(rb, C)
    return pl.pallas_call(
        body, out_shape=(jax.ShapeDtypeStruct((R, C), F32),) * 3, grid=(R // rb,), in_specs=[spec] * 4,
        out_specs=(spec,) * 3, name=name, compiler_params=_cp("parallel"))(w, g, m, v)


class _Dist:
    def __init__(self, shards):
        self.shards = shards
        self.chip = 2 * lax.axis_index("x") + lax.axis_index("y")
        self.where = jnp.stack([self.chip, lax.axis_index("c")]).astype(jnp.int32)
        self.pairs, self.landed = {}, {}

    @staticmethod
    def _unshard(name, blk):
        _, (r, cc), axis = next(b for b in BIG if b[0] == name)
        return blk.reshape(4 * r, cc) if axis == 0 else blk.transpose(1, 0, 2).reshape(r, 4 * cc)

    def gather_first(self):
        got = _run_hosted(_gather_plan([self.shards["w_in"].astype(BF16)]), "gather_w_in")
        return self._unshard("w_in", got[0])

    def gather_rest(self):
        rest = [self.shards[n].astype(BF16) for n, _, _ in BIG if n != "w_in"]
        return _gather_plan(rest, [self.shards["conv_w"]])

    def weights_from(self, gathered):
        names = [n for n, _, _ in BIG if n != "w_in"]
        conv = gathered[-1]
        taps, width = conv.shape[1:]
        return ({n: self._unshard(n, g) for n, g in zip(names, gathered)},
                conv.transpose(1, 0, 2).reshape(taps, 4 * width))

    def scatter(self, G, names, tag):
        slabs = []
        for name, (r, cc), axis in BIG:
            if name not in names:
                continue
            g = G[name]
            if axis == 0:
                slabs.append(g.reshape(4, 2, r // 2, cc))
            else:
                rows = _slab_rows(cc)
                g = jnp.pad(g.reshape(4, cc, r), ((0, 0), (0, rows - cc), (0, 0)))
                slabs.append(g.reshape(4, 2, rows // 2, r))
        theirs = _swap_halves(slabs, "grad_swap_cores_" + tag)
        pairs = [_pair_sum(s, t, self.where[1:], "pair_sum_" + n) for s, t, n in zip(slabs, theirs, names)]
        self.pairs.update(zip(names, pairs))
        return _scatter_plan(pairs)

    def collect(self, names, landed):
        self.landed.update(zip(names, landed))

    def finish(self):
        names = [n for n, _, _ in BIG]
        halves = [_chip_sum(self.landed[n], self.pairs[n], self.where, "chip_sum_" + n) for n in names]
        out = {}
        for (name, (r, cc), axis), both in zip(BIG, _join_halves(halves)):
            full = both.reshape(-1, both.shape[-1])
            out[name] = full if axis == 0 else full[:cc].T
        return out


def _pack_rows(parts, rows, dtype):
    flat = jnp.concatenate([p.reshape(-1).astype(dtype) for p in parts])
    return jnp.pad(flat, (0, rows * LANES - flat.shape[0])).reshape(rows, LANES)


def _small_rows(n):
    return max(n // LANES, 1)


def _pack_small(vals):
    rows = []
    for name, n in SMALL:
        r = _small_rows(n)
        rows.append(jnp.pad(vals[name].reshape(-1), (0, r * LANES - n)).reshape(r, LANES))
    flat = jnp.concatenate(rows, axis=0)
    return jnp.pad(flat, ((0, SMALL_ROWS - flat.shape[0]), (0, 0)))


def _unpack_small(pack):
    out, r0 = {}, 0
    for name, n in SMALL:
        r = _small_rows(n)
        out[name] = pack[r0:r0 + r].reshape(-1)[:n]
        r0 += r
    return out


def kernel(x, mem, norm1_g, w_in, conv_w, a_log, dt_bias, gdn_norm_g, sb_q_norm_g, sb_k_norm_g, mem_norm_g, w_mem_kv, mem_q_norm_g, mem_k_norm_g, w_br_gdn, w_br_sb, w_br_mem, w_o, norm2_g, w_up, w_down, loss_target, m_norm1_g, m_w_in, m_conv_w, m_a_log, m_dt_bias, m_gdn_norm_g, m_sb_q_norm_g, m_sb_k_norm_g, m_mem_norm_g, m_w_mem_kv, m_mem_q_norm_g, m_mem_k_norm_g, m_w_br_gdn, m_w_br_sb, m_w_br_mem, m_w_o, m_norm2_g, m_w_up, m_w_down, v_norm1_g, v_w_in, v_conv_w, v_a_log, v_dt_bias, v_gdn_norm_g, v_sb_q_norm_g, v_sb_k_norm_g, v_mem_norm_g, v_w_mem_kv, v_mem_q_norm_g, v_mem_k_norm_g, v_w_br_gdn, v_w_br_sb, v_w_br_mem, v_w_o, v_norm2_g, v_w_up, v_w_down):
    wd = dict(norm1_g=norm1_g, w_in=w_in, conv_w=conv_w, a_log=a_log, dt_bias=dt_bias, gdn_norm_g=gdn_norm_g,
              sb_q_norm_g=sb_q_norm_g, sb_k_norm_g=sb_k_norm_g, mem_norm_g=mem_norm_g, w_mem_kv=w_mem_kv,
              mem_q_norm_g=mem_q_norm_g, mem_k_norm_g=mem_k_norm_g, w_br_gdn=w_br_gdn, w_br_sb=w_br_sb,
              w_br_mem=w_br_mem, w_o=w_o, norm2_g=norm2_g, w_up=w_up, w_down=w_down)
    md = dict(norm1_g=m_norm1_g, w_in=m_w_in, conv_w=m_conv_w, a_log=m_a_log, dt_bias=m_dt_bias,
              gdn_norm_g=m_gdn_norm_g, sb_q_norm_g=m_sb_q_norm_g, sb_k_norm_g=m_sb_k_norm_g,
              mem_norm_g=m_mem_norm_g, w_mem_kv=m_w_mem_kv, mem_q_norm_g=m_mem_q_norm_g,
              mem_k_norm_g=m_mem_k_norm_g, w_br_gdn=m_w_br_gdn, w_br_sb=m_w_br_sb, w_br_mem=m_w_br_mem, w_o=m_w_o,
              norm2_g=m_norm2_g, w_up=m_w_up, w_down=m_w_down)
    vd = dict(norm1_g=v_norm1_g, w_in=v_w_in, conv_w=v_conv_w, a_log=v_a_log, dt_bias=v_dt_bias,
              gdn_norm_g=v_gdn_norm_g, sb_q_norm_g=v_sb_q_norm_g, sb_k_norm_g=v_sb_k_norm_g,
              mem_norm_g=v_mem_norm_g, w_mem_kv=v_w_mem_kv, mem_q_norm_g=v_mem_q_norm_g,
              mem_k_norm_g=v_mem_k_norm_g, w_br_gdn=v_w_br_gdn, w_br_sb=v_w_br_sb, w_br_mem=v_w_br_mem, w_o=v_w_o,
              norm2_g=v_norm2_g, w_up=v_w_up, w_down=v_w_down)
    wd, md, vd = ({n: a[0] for n, a in d.items()} for d in (wd, md, vd))
    chip = 2 * lax.axis_index("x") + lax.axis_index("y")
    conv_shard = wd["conv_w"].shape

    dist = _Dist(wd)
    W = {"w_in": dist.gather_first()}
    P = {n: wd[n].reshape(1, -1) for n, _ in SMALL}

    sq_err, grad_x, G = _local_step(x[0], mem[0], loss_target[0], W, P, dist)
    loss = lax.psum(0.5 / D_MODEL * jnp.sum(sq_err), ("x", "y", "c"))

    g_big = dist.finish()

    spack = jnp.concatenate([_pack_small(G), G["conv_w"].reshape(CONV_ROWS, LANES)], axis=0)
    g_small = _sum_slots(_gather_all(spack, "gather_small_grads"), "small_grad_sum")
    g_conv_full = g_small[SMALL_ROWS:].reshape(conv_shard[0], 4 * conv_shard[1])
    g_conv = lax.dynamic_slice_in_dim(g_conv_full, chip * conv_shard[1], conv_shard[1], axis=1)

    grads, deltas, new_m, new_v = dict(g_big), {}, {}, {}
    for name, _, _ in BIG:
        deltas[name], new_m[name], new_v[name] = _adamw(wd[name], g_big[name], md[name], vd[name], "adamw_" + name)
    pack_sm = lambda d: jnp.concatenate([_pack_small(d), _pack_rows([d["conv_w"]], APACK_ROWS - SMALL_ROWS, F32)], axis=0)
    g_sm = jnp.concatenate([g_small[:SMALL_ROWS], _pack_rows([g_conv], APACK_ROWS - SMALL_ROWS, F32)], axis=0)
    small = (g_sm,) + _adamw(pack_sm(wd), g_sm, pack_sm(md), pack_sm(vd), "adamw_small")
    for out, pack in zip((grads, deltas, new_m, new_v), small):
        out.update(_unpack_small(pack[:SMALL_ROWS]))
        out["conv_w"] = pack[SMALL_ROWS:].reshape(-1)[:conv_shard[0] * conv_shard[1]].reshape(conv_shard)

    return (loss, grad_x[None], *[d[n][None] for d in (grads, deltas, new_m, new_v) for n in WEIGHTS])


def _slab_rows(n):
    return -(-n // 32) * 32
```

```python
import functools

import jax
import jax.numpy as jnp
import numpy as np
from jax import lax
from jax.experimental import pallas as pl
from jax.experimental.pallas import tpu as pltpu

F32 = jnp.float32
BF16 = jnp.bfloat16
MESH = pl.DeviceIdType.MESH

D_MODEL = 1024
N_HEAD = 4
D_HEAD = 128
BR_W = N_HEAD * D_HEAD
CONV_TAPS = 4
GDN_CHUNK = 64
INV_BLOCK = 16
INV_CHUNKS = 4
N_MEM = 256
D_FF = 4 * D_MODEL
EPS = 1e-6
LANES = 128
PROJ_W = 7168
GATE_OFF = 4096
SB_OFF = 2048
MEMQ_OFF = 3584
Z_OFF = 1536

ADAM_LR, ADAM_B1, ADAM_B2, ADAM_EPS, ADAM_WD, ADAM_STEP = 0.001, 0.9, 0.999, 1e-08, 0.01, 10

TM = 512
MM_TM = 1024
TK_TOK = 1024
GDN_STEP_CHUNKS = 4
GDN_BWD_STEP_CHUNKS = 1
G1_TM = 256
SB_BLK_Q = 512
SB_BLK = 512
SB_W = 256
SB_DEAD = 120.0
VMEM_LIMIT = 48 << 20

BIG = (("w_in", (1024, 1794), 1), ("w_mem_kv", (256, 1024), 0), ("w_br_gdn", (512, 256), 1),
       ("w_br_sb", (512, 256), 1), ("w_br_mem", (512, 256), 1), ("w_o", (256, 1024), 0),
       ("w_up", (1024, 1024), 1), ("w_down", (1024, 1024), 0))
COL_SHARDED = tuple(n for n, _, a in BIG if a == 1)
GATE_COL = GATE_OFF // D_MODEL
ROW_BLK = 1024
ADAM_ROWS = 128
SMALL = (("norm1_g", 1024), ("mem_norm_g", 1024), ("norm2_g", 1024), ("gdn_norm_g", 128), ("sb_q_norm_g", 128),
         ("sb_k_norm_g", 128), ("mem_q_norm_g", 128), ("mem_k_norm_g", 128), ("a_log", 4), ("dt_bias", 4))
SMALL_ROWS = 32
CONV_ROWS = 48
SPACK_ROWS = SMALL_ROWS + CONV_ROWS
APACK_ROWS = SMALL_ROWS + 16

WEIGHTS = ("norm1_g", "w_in", "conv_w", "a_log", "dt_bias", "gdn_norm_g", "sb_q_norm_g", "sb_k_norm_g",
           "mem_norm_g", "w_mem_kv", "mem_q_norm_g", "mem_k_norm_g", "w_br_gdn", "w_br_sb", "w_br_mem", "w_o",
           "norm2_g", "w_up", "w_down")


def _cp(*sem):
    return pltpu.CompilerParams(dimension_semantics=sem if sem else None, vmem_limit_bytes=VMEM_LIMIT)


HBM = pl.BlockSpec(memory_space=pl.ANY)

_NN = (((1,), (0,)), ((), ()))
_NT = (((1,), (1,)), ((), ()))
_TN = (((0,), (0,)), ((), ()))


def _dot(a, b, dims=_NN):
    return lax.dot_general(a.astype(BF16), b.astype(BF16), dims, preferred_element_type=F32)


def _dot_nt(a, b):
    return _dot(a, b, _NT)


def _dot_tn(a, b):
    return _dot(a, b, _TN)


def _dotf(a, b, dims=_NN):
    return lax.dot_general(a, b, dims, precision=lax.Precision.HIGHEST, preferred_element_type=F32)


def _sigmoid(v):
    return 0.5 * jnp.tanh(0.5 * v) + 0.5


def _softplus(v):
    return jnp.maximum(v, 0.0) + jnp.log(1.0 + jnp.exp(-jnp.abs(v)))


def _iota(shape, dim):
    return lax.broadcasted_iota(jnp.int32, shape, dim)


def _hs(h):
    return slice(h * D_HEAD, (h + 1) * D_HEAD)


def _rowspec(tm, w, col=0):
    return pl.BlockSpec((tm, w), lambda i: (i, col))


def _full(shape):
    return pl.BlockSpec(shape, lambda *_: (0,) * len(shape))


def _accum(ref, first, val):
    @pl.when(first)
    def _():
        ref[...] = val

    @pl.when(jnp.logical_not(first))
    def _():
        ref[...] += val


class _Hosted:
    def __init__(self, ins, out_shapes, n_sems, start, finish, mid=None, scratch=()):
        self.ins, self.out_shapes, self.n_sems = list(ins), list(out_shapes), n_sems
        self.start, self.mid, self.finish, self.scratch = start, mid, finish, list(scratch)


def _mm(a, b, mode, out_dtype, name, *, tm=None, tn=None, tk=None, a_fn=None, extra=None, epi=None, comm=None):
    if mode == "tn":
        (K, M), N = a.shape, b.shape[1]
    else:
        (M, K), N = a.shape, (b.shape[0] if mode == "nt" else b.shape[1])
    tm = min(tm or (1024 if mode == "tn" else MM_TM), M)
    tn = min(tn or 1024, N)
    tk = min(tk or (TK_TOK if mode == "tn" else 1024), K)
    nm, nn, nk = M // tm, N // tn, K // tk
    assert nm * tm == M and nn * tn == N and nk * tk == K, (name, a.shape, b.shape)
    if mode == "tn":
        a_spec = pl.BlockSpec((tk, tm), lambda i, j, k: (k, i))
    else:
        a_spec = pl.BlockSpec((tm, tk), lambda i, j, k: (i, k))
    if mode == "nt":
        b_spec = pl.BlockSpec((tn, tk), lambda i, j, k: (j, k))
    else:
        b_spec = pl.BlockSpec((tk, tn), lambda i, j, k: (k, j))
    dims = {"nn": _NN, "nt": _NT, "tn": _TN}[mode]
    o_spec = pl.BlockSpec((tm, tn), lambda i, j, k: (i, j))
    has_extra = extra is not None

    n_ci, n_co = (len(comm.ins), len(comm.out_shapes)) if comm else (0, 0)
    n_in = 2 + has_extra + n_ci
    steps = nm * nn * nk

    def body(*refs):
        a_ref, b_ref = refs[0], refs[1]
        e_ref = refs[2] if has_extra else None
        o_ref = refs[n_in]
        scratch = refs[n_in + 1 + n_co:]
        if comm:
            step = (pl.program_id(0) * nn + pl.program_id(1)) * nk + pl.program_id(2)
            cargs = (refs[2 + has_extra:n_in], refs[n_in + 1:n_in + 1 + n_co], scratch[nk > 1], scratch[(nk > 1) + 1],
                     scratch[(nk > 1) + 2:])
            pl.when(step == 0)(lambda: comm.start(*cargs))
            if comm.mid is not None:
                pl.when(step == (steps * 7) // 8)(lambda: comm.mid(*cargs))
        av = a_ref[...]
        if a_fn is not None:
            av = a_fn(av)
        p = lax.dot_general(av, b_ref[...], dims, preferred_element_type=F32)

        def finish(acc):
            if epi is not None:
                acc = epi(acc, e_ref[...] if has_extra else None)
            o_ref[...] = acc.astype(out_dtype)

        if nk == 1:
            finish(p)
        else:
            acc_ref = scratch[0]
            k = pl.program_id(2)
            _accum(acc_ref, k == 0, p)

            @pl.when(k == nk - 1)
            def _():
                finish(acc_ref[...])

        if comm:
            pl.when(step == steps - 1)(lambda: comm.finish(*cargs))

    ins = [a, b] + ([extra] if has_extra else [])
    in_specs = [a_spec, b_spec] + ([o_spec] if has_extra else [])
    scratch_shapes = [pltpu.VMEM((tm, tn), F32)] if nk > 1 else []
    main = jax.ShapeDtypeStruct((M, N), out_dtype)
    if not comm:
        return pl.pallas_call(
            body, out_shape=main, grid=(nm, nn, nk), in_specs=in_specs, out_specs=o_spec,
            scratch_shapes=scratch_shapes, name=name, compiler_params=_cp("parallel", "parallel", "arbitrary"))(*ins)
    sems = [pltpu.SemaphoreType.DMA((comm.n_sems,)), pltpu.SemaphoreType.DMA((comm.n_sems,))]
    res = pl.pallas_call(
        body, out_shape=(main, *comm.out_shapes), grid=(nm, nn, nk), in_specs=in_specs + [HBM] * n_ci,
        out_specs=(o_spec, *[HBM] * n_co), scratch_shapes=scratch_shapes + sems + comm.scratch, name=name,
        compiler_params=_cp("arbitrary", "arbitrary", "arbitrary"))(*ins, *comm.ins)
    return res[0], list(res[1:])


def _relu2(u):
    r = jnp.maximum(u.astype(F32), 0.0)
    return (r * r).astype(BF16)


def _epi_add(acc, e):
    return acc + e.astype(F32)


def _epi_drelu2(acc, u):
    return acc * (2.0 * jnp.maximum(u.astype(F32), 0.0))


def _rms_fwd(x, g, name):
    T, dm = x.shape
    tm = min(TM, T)

    def body(x_ref, g_ref, h_ref):
        xv = x_ref[...]
        r = lax.rsqrt(jnp.mean(xv * xv, axis=-1, keepdims=True) + EPS)
        h_ref[...] = (xv * r * g_ref[...]).astype(BF16)

    return pl.pallas_call(
        body, out_shape=jax.ShapeDtypeStruct((T, dm), BF16), grid=(T // tm,),
        in_specs=[_rowspec(tm, dm), _full((1, dm))], out_specs=_rowspec(tm, dm), name=name,
        compiler_params=_cp("parallel"))(x, g)


def _rms_bwd(dh, x, g, resid, name):
    T, dm = x.shape
    tm = min(TM, T)

    def body(dh_ref, x_ref, g_ref, res_ref, dx_ref, dxb_ref, dg_ref):
        i = pl.program_id(0)
        xv = x_ref[...]
        r = lax.rsqrt(jnp.mean(xv * xv, axis=-1, keepdims=True) + EPS)
        y = xv * r
        dhv = dh_ref[...].astype(F32)
        dy = dhv * g_ref[...]
        dx = res_ref[...] + r * (dy - y * jnp.mean(dy * y, axis=-1, keepdims=True))
        dx_ref[...] = dx
        dxb_ref[...] = dx.astype(BF16)
        _accum(dg_ref, i == 0, jnp.sum(dhv * y, axis=0, keepdims=True))

    return pl.pallas_call(
        body,
        out_shape=(jax.ShapeDtypeStruct((T, dm), F32), jax.ShapeDtypeStruct((T, dm), BF16),
                   jax.ShapeDtypeStruct((1, dm), F32)),
        grid=(T // tm,),
        in_specs=[_rowspec(tm, dm), _rowspec(tm, dm), _full((1, dm)), _rowspec(tm, dm)],
        out_specs=(_rowspec(tm, dm), _rowspec(tm, dm), _full((1, dm))), name=name,
        compiler_params=_cp("arbitrary"))(dh, x, g, resid)


def _conv_tile(x_ref, halo_ref, w_ref, xpad, tm):
    i = pl.program_id(0)
    halo = halo_ref[...].astype(F32)[8:16]
    xpad[0:8, :] = jnp.where(i > 0, halo, 0.0)
    xpad[8:, :] = x_ref[...].astype(F32)
    w = w_ref[...]
    xc = w[0:1] * xpad[5:5 + tm, :]
    for j in range(1, CONV_TAPS):
        xc = xc + w[j:j + 1] * xpad[5 + j:5 + j + tm, :]
    return xc


def _gate_terms(ab_ref, av_ref):
    abv = ab_ref[...]
    av = av_ref[...]
    pre = abv + av[1:2]
    ea = jnp.exp(av[0:1])
    g = -ea * _softplus(pre)
    return abv, pre, ea, g


def _gdn_pre(proj, conv_w, ab, avec):
    T = proj.shape[0]
    tm = min(G1_TM, T)
    cw = 3 * BR_W

    def body(x_ref, halo_ref, w_ref, ab_ref, av_ref, q_ref, k_ref, v_ref, gb_ref, xpad):
        xc = _conv_tile(x_ref, halo_ref, w_ref, xpad, tm)
        y = xc * _sigmoid(xc)
        for h in range(N_HEAD):
            for off, ref, scale in ((0, q_ref, D_HEAD ** -0.5), (BR_W, k_ref, 1.0)):
                yh = y[:, off + h * D_HEAD:off + (h + 1) * D_HEAD]
                r = lax.rsqrt(jnp.sum(yh * yh, axis=-1, keepdims=True) + EPS)
                ref[:, _hs(h)] = yh * (r * scale)
        v_ref[...] = y[:, 2 * BR_W:]
        abv, _, _, g = _gate_terms(ab_ref, av_ref)
        lane = _iota((tm, LANES), 1)
        gb_ref[...] = jnp.where(lane < N_HEAD, g, jnp.where(lane < 2 * N_HEAD, _sigmoid(abv), 0.0))

    hb = tm // 16
    return pl.pallas_call(
        body,
        out_shape=(jax.ShapeDtypeStruct((T, BR_W), F32),) * 3 + (jax.ShapeDtypeStruct((T, LANES), F32),),
        grid=(T // tm,),
        in_specs=[_rowspec(tm, cw), pl.BlockSpec((16, cw), lambda i: (jnp.maximum(i * hb - 1, 0), 0)),
                  _full((CONV_TAPS, cw)), _rowspec(tm, LANES), _full((2, LANES))],
        out_specs=(_rowspec(tm, BR_W),) * 3 + (_rowspec(tm, LANES),),
        scratch_shapes=[pltpu.VMEM((tm + 8, cw), F32)], name="gdn_pre",
        compiler_params=_cp("parallel"))(proj, proj, conv_w, ab, avec)


def _gdn_pre_bwd(proj, conv_w, ab, avec, dq, dk, dv, dgb):
    T = proj.shape[0]
    tm = min(G1_TM, T)
    cw = 3 * BR_W

    def body(x_ref, halo_ref, w_ref, ab_ref, av_ref, dq_ref, dk_ref, dv_ref, dgb_ref,
             dxc_ref, dab_ref, dcw_ref, dav_ref, xpad):
        i = pl.program_id(0)

        @pl.when(i == 0)
        def _():
            dcw_ref[...] = jnp.zeros_like(dcw_ref)
            dav_ref[...] = jnp.zeros_like(dav_ref)

        xc_all = _conv_tile(x_ref, halo_ref, w_ref, xpad, tm)
        for s in range(cw // D_HEAD):
            cs = slice(s * D_HEAD, (s + 1) * D_HEAD)
            xc = xc_all[:, cs]
            sg = _sigmoid(xc)
            yh = xc * sg
            h = s % N_HEAD
            if s < 2 * N_HEAD:
                dref, scale = (dq_ref, D_HEAD ** -0.5) if s < N_HEAD else (dk_ref, 1.0)
                r = lax.rsqrt(jnp.sum(yh * yh, axis=-1, keepdims=True) + EPS)
                yn = yh * r
                dn = dref[:, _hs(h)]
                dy = (scale * r) * (dn - yn * jnp.sum(yn * dn, axis=-1, keepdims=True))
            else:
                dy = dv_ref[:, _hs(h)]
            dxc = dy * (sg * (1.0 + xc * (1.0 - sg)))
            dxc_ref[:, cs] = dxc.astype(BF16)
            for j in range(CONV_TAPS):
                dcw_ref[j:j + 1, cs] += jnp.sum(dxc * xpad[5 + j:5 + j + tm, cs], axis=0, keepdims=True)

        abv, pre, ea, g = _gate_terms(ab_ref, av_ref)
        dgbv = dgb_ref[...]
        lane = _iota((tm, LANES), 1)
        is_a = lane < N_HEAD
        da = jnp.where(is_a, dgbv * (-ea) * _sigmoid(pre), 0.0)
        bs = _sigmoid(abv)
        db = jnp.where(jnp.logical_and(lane >= N_HEAD, lane < 2 * N_HEAD), dgbv * bs * (1.0 - bs), 0.0)
        dab_ref[...] = (da + db).astype(BF16)
        dav_ref[0:1, :] += jnp.sum(jnp.where(is_a, dgbv * g, 0.0), axis=0, keepdims=True)
        dav_ref[1:2, :] += jnp.sum(da, axis=0, keepdims=True)

    hb = tm // 16
    return pl.pallas_call(
        body,
        out_shape=(jax.ShapeDtypeStruct((T, cw), BF16), jax.ShapeDtypeStruct((T, LANES), BF16),
                   jax.ShapeDtypeStruct((CONV_TAPS, cw), F32), jax.ShapeDtypeStruct((2, LANES), F32)),
        grid=(T // tm,),
        in_specs=[_rowspec(tm, cw), pl.BlockSpec((16, cw), lambda i: (jnp.maximum(i * hb - 1, 0), 0)),
                  _full((CONV_TAPS, cw)), _rowspec(tm, LANES), _full((2, LANES)),
                  _rowspec(tm, BR_W), _rowspec(tm, BR_W), _rowspec(tm, BR_W), _rowspec(tm, LANES)],
        out_specs=(_rowspec(tm, cw), _rowspec(tm, LANES), _full((CONV_TAPS, cw)), _full((2, LANES))),
        scratch_shapes=[pltpu.VMEM((tm + 8, cw), F32)], name="gdn_pre_bwd",
        compiler_params=_cp("arbitrary"))(proj, proj, conv_w, ab, avec, dq, dk, dv, dgb)


def _conv_bwd(dxc, conv_w):
    T, cw = dxc.shape
    tm = min(G1_TM, T)
    nt = T // tm
    hb = tm // 16

    def body(d_ref, halo_ref, w_ref, dx_ref, xpad):
        i = pl.program_id(0)
        xpad[0:tm, :] = d_ref[...].astype(F32)
        xpad[tm:, :] = jnp.where(i < nt - 1, halo_ref[...].astype(F32)[0:8], 0.0)
        w = w_ref[...]
        dx = w[3:4] * xpad[0:tm, :]
        for j in range(CONV_TAPS - 1):
            dx = dx + w[j:j + 1] * xpad[3 - j:3 - j + tm, :]
        dx_ref[...] = dx.astype(BF16)

    return pl.pallas_call(
        body, out_shape=jax.ShapeDtypeStruct((T, cw), BF16), grid=(nt,),
        in_specs=[_rowspec(tm, cw), pl.BlockSpec((16, cw), lambda i: (jnp.minimum((i + 1) * hb, T // 16 - 1), 0)),
                  _full((CONV_TAPS, cw))],
        out_specs=_rowspec(tm, cw), scratch_shapes=[pltpu.VMEM((tm + 8, cw), F32)], name="conv_bwd",
        compiler_params=_cp("parallel"))(dxc, dxc, conv_w)


def _chunk_consts():
    C = GDN_CHUNK
    row, col = _iota((C, C), 0), _iota((C, C), 1)
    return row, col, row >= col, row > col


def _chunk_decay(gbv, incl):
    c_all = _dotf(incl.astype(F32), gbv)
    c_t = jnp.concatenate([c_all, jnp.zeros_like(c_all)], axis=0).T[:, :GDN_CHUNK]
    return c_all, c_t


def _head_decay(c_all, c_t, gbv, incl, h):
    C = GDN_CHUNK
    c_col = c_all[:, h:h + 1]
    c_row = c_t[h:h + 1, :]
    gam = jnp.exp(jnp.where(incl, c_col - c_row, -1e30))
    c_last = c_all[C - 1:C, h:h + 1]
    return gam, jnp.exp(c_col), jnp.exp(c_last - c_col), jnp.exp(c_last), gbv[:, N_HEAD + h:N_HEAD + h + 1]


def _split_bf16(x):
    hi = x.astype(BF16)
    return hi, (x - hi.astype(F32)).astype(BF16)


def _dot3(a, b):
    ah, al = _split_bf16(a)
    bh, bl = _split_bf16(b)
    d = lambda u, v: lax.dot_general(u, v, _NN, preferred_element_type=F32)
    return d(ah, bh) + (d(ah, bl) + d(al, bh))


def _unit_lower_inverses(ms, row, col):
    bi, bj = row // INV_BLOCK, col // INV_BLOCK
    eye = (row == col).astype(F32)
    ns = [jnp.where(bi == bj, -m, 0.0) for m in ms]
    invs = [eye + n for n in ns]
    size = 2
    while size < INV_BLOCK:
        ns = [_dot3(n, n) for n in ns]
        invs = [inv + _dot3(inv, n) for inv, n in zip(invs, ns)]
        size *= 2
    width = 2
    while width * INV_BLOCK <= GDN_CHUNK:
        sel = jnp.logical_and(bi // width == bj // width, bi // (width // 2) > bj // (width // 2))
        ts = [_dot3(inv, jnp.where(sel, m, 0.0)) for inv, m in zip(invs, ms)]
        invs = [inv - _dot3(t, inv) for inv, t in zip(invs, ts)]
        width *= 2
    return invs


def _gdn_inv(k, gb):
    T = k.shape[0]
    C = GDN_CHUNK
    per = min(INV_CHUNKS, T // C)
    rows = per * C

    def body(k_ref, gb_ref, ti_ref):
        row, col, incl, strict = _chunk_consts()
        ms = []
        for ci in range(per):
            rs = slice(ci * C, (ci + 1) * C)
            gbv = gb_ref[rs, :]
            c_all, c_t = _chunk_decay(gbv, incl)
            for h in range(N_HEAD):
                gam, _, _, _, bcol = _head_decay(c_all, c_t, gbv, incl, h)
                K = k_ref[rs, _hs(h)]
                ms.append(jnp.where(strict, _dot_nt(K * bcol, K) * gam, 0.0))
        for i, inv in enumerate(_unit_lower_inverses(ms, row, col)):
            ti_ref[i // N_HEAD, i % N_HEAD] = inv

    return pl.pallas_call(
        body, out_shape=jax.ShapeDtypeStruct((T // C, N_HEAD, C, C), F32), grid=(T // rows,),
        in_specs=[_rowspec(rows, BR_W), _rowspec(rows, LANES)],
        out_specs=pl.BlockSpec((per, N_HEAD, C, C), lambda i: (i, 0, 0, 0)), name="gdn_inv",
        compiler_params=_cp("parallel"))(k, gb)


def _gdn_fwd(q, k, v, gb, proj, gnorm, tinv_all):
    T = q.shape[0]
    C = GDN_CHUNK
    nc = T // C
    per = min(GDN_STEP_CHUNKS, nc)
    zcol = Z_OFF // BR_W
    heads = range(N_HEAD)

    def body(q_ref, k_ref, v_ref, gb_ref, z_ref, gn_ref, ti_ref, og_ref, oraw_ref, sh_ref, vn_ref, s_ref):
        @pl.when(pl.program_id(0) == 0)
        def _():
            s_ref[...] = jnp.zeros_like(s_ref)

        _, _, incl, _ = _chunk_consts()
        S = [s_ref[h] for h in heads]
        for ci in range(per):
            rs = slice(ci * C, (ci + 1) * C)
            gbv = gb_ref[rs, :]
            c_all, c_t = _chunk_decay(gbv, incl)
            dec = [_head_decay(c_all, c_t, gbv, incl, h) for h in heads]
            gam, gcol, dcol, glast, bcol = ([d[i] for d in dec] for i in range(5))
            Q = [q_ref[rs, _hs(h)] for h in heads]
            K = [k_ref[rs, _hs(h)] for h in heads]
            V = [v_ref[rs, _hs(h)] for h in heads]
            Sb = [s.astype(BF16) for s in S]
            KS = [_dot(K[h], Sb[h]) for h in heads]
            QS = [_dot(Q[h], Sb[h]) for h in heads]
            P = [_dot_nt(Q[h], K[h]) * gam[h] for h in heads]
            R = [bcol[h] * (V[h] - gcol[h] * KS[h]) for h in heads]
            vn = [_dot(ti_ref[ci, h], R[h]) for h in heads]
            O = [gcol[h] * QS[h] + _dot(P[h], vn[h]) for h in heads]
            Sn = [glast[h] * S[h] + _dot_tn(K[h] * dcol[h], vn[h]) for h in heads]
            for h in heads:
                sh_ref[ci, h] = S[h]
                vn_ref[rs, _hs(h)] = vn[h]
                oraw_ref[rs, _hs(h)] = O[h]
                rr = lax.rsqrt(jnp.mean(O[h] * O[h], axis=-1, keepdims=True) + EPS)
                zz = z_ref[rs, _hs(h)].astype(F32)
                og_ref[rs, _hs(h)] = (O[h] * rr * gn_ref[...] * (zz * _sigmoid(zz))).astype(BF16)
            S = Sn
        for h in heads:
            s_ref[h] = S[h]

    cspec = lambda w, cb=0: pl.BlockSpec((per * C, w), lambda n: (n, cb))
    hist = lambda a, b: pl.BlockSpec((per, N_HEAD, a, b), lambda n: (n, 0, 0, 0))
    return pl.pallas_call(
        body,
        out_shape=(jax.ShapeDtypeStruct((T, BR_W), BF16), jax.ShapeDtypeStruct((T, BR_W), F32),
                   jax.ShapeDtypeStruct((nc, N_HEAD, D_HEAD, D_HEAD), F32), jax.ShapeDtypeStruct((T, BR_W), F32)),
        grid=(nc // per,),
        in_specs=[cspec(BR_W), cspec(BR_W), cspec(BR_W), cspec(LANES), cspec(BR_W, zcol), _full((1, D_HEAD)),
                  hist(C, C)],
        out_specs=(cspec(BR_W), cspec(BR_W), hist(D_HEAD, D_HEAD), cspec(BR_W)),
        scratch_shapes=[pltpu.VMEM((N_HEAD, D_HEAD, D_HEAD), F32)], name="gdn_chunk_fwd",
        compiler_params=_cp("arbitrary"))(q, k, v, gb, proj, gnorm, tinv_all)


def _gdn_bwd(q, k, v, gb, proj, gnorm, oraw, shist, tinv_all, vn_all, dog):
    T = q.shape[0]
    C = GDN_CHUNK
    nc = T // C
    per = min(GDN_BWD_STEP_CHUNKS, nc)
    zcol = Z_OFF // BR_W

    def body(q_ref, k_ref, v_ref, gb_ref, z_ref, gn_ref, oraw_ref, sh_ref, ti_ref, vn_ref, dog_ref,
             dq_ref, dk_ref, dv_ref, dgb_ref, dz_ref, dgn_ref, ds_ref):
        @pl.when(pl.program_id(0) == 0)
        def _():
            ds_ref[...] = jnp.zeros_like(ds_ref)
            dgn_ref[...] = jnp.zeros_like(dgn_ref)

        row, col, incl, strict = _chunk_consts()
        lane = _iota((C, LANES), 1)
        rowl = _iota((C, LANES), 0)
        ones = jnp.ones((C, LANES), F32)
        upper = (col >= row).astype(F32)
        gn = gn_ref[...]
        heads = range(N_HEAD)
        rsum = lambda a: jnp.sum(a, axis=-1, keepdims=True)
        dgn = jnp.zeros((1, D_HEAD), F32)
        dSn = [ds_ref[h] for h in heads]
        for ci in reversed(range(per)):
            rs = slice(ci * C, (ci + 1) * C)
            gbv = gb_ref[rs, :]
            c_all, c_t = _chunk_decay(gbv, incl)
            dec = [_head_decay(c_all, c_t, gbv, incl, h) for h in heads]
            gam, gcol, dcol, glast, bcol = ([d[i] for d in dec] for i in range(5))
            Q = [q_ref[rs, _hs(h)] for h in heads]
            K = [k_ref[rs, _hs(h)] for h in heads]
            V = [v_ref[rs, _hs(h)] for h in heads]
            dO = []
            for h in heads:
                O = oraw_ref[rs, _hs(h)]
                zz = z_ref[rs, _hs(h)].astype(F32)
                dogv = dog_ref[rs, _hs(h)].astype(F32)
                rr = lax.rsqrt(jnp.mean(O * O, axis=-1, keepdims=True) + EPS)
                on = O * rr
                sg = _sigmoid(zz)
                dz_ref[rs, _hs(h)] = (dogv * on * gn * (sg * (1.0 + zz * (1.0 - sg)))).astype(BF16)
                dyn = dogv * (zz * sg)
                dgn = dgn + jnp.sum(dyn * on, axis=0, keepdims=True)
                dyv = dyn * gn
                dO.append((rr * (dyv - on * jnp.mean(dyv * on, axis=-1, keepdims=True))).astype(BF16))
            S = [sh_ref[ci, h] for h in heads]
            Sb = [s.astype(BF16) for s in S]
            tinv = [ti_ref[ci, h].astype(BF16) for h in heads]
            vn = [vn_ref[rs, _hs(h)] for h in heads]
            vnb = [a.astype(BF16) for a in vn]
            dSb = [a.astype(BF16) for a in dSn]
            Kb = [K[h] * bcol[h] for h in heads]
            M = [jnp.where(strict, _dot_nt(Kb[h], K[h]) * gam[h], 0.0) for h in heads]
            P = [_dot_nt(Q[h], K[h]) * gam[h] for h in heads]
            KS = [_dot(K[h], Sb[h]) for h in heads]
            QS = [_dot(Q[h], Sb[h]) for h in heads]
            dvn = [_dot_tn(P[h], dO[h]) + _dot(K[h] * dcol[h], dSb[h]) for h in heads]
            dR = [_dot_tn(tinv[h], dvn[h]) for h in heads]
            dRb = [a.astype(BF16) for a in dR]
            bg = [bcol[h] * gcol[h] for h in heads]
            dS_new = [glast[h] * dSn[h] + _dot_tn(gcol[h] * Q[h], dO[h]) - _dot_tn(bg[h] * K[h], dRb[h])
                      for h in heads]
            dP = [jnp.where(incl, _dot_nt(dO[h], vnb[h]), 0.0) for h in heads]
            dM = [jnp.where(strict, -_dot_nt(dRb[h], vnb[h]), 0.0) for h in heads]
            dPG = [(dP[h] * gam[h]).astype(BF16) for h in heads]
            dMG = [(dM[h] * gam[h]).astype(BF16) for h in heads]
            E = [_dot_nt(vnb[h], dSb[h]) for h in heads]
            dKb = [_dot(dMG[h], K[h]) for h in heads]
            dc_all = jnp.zeros((C, LANES), F32)
            db_all = jnp.zeros((C, LANES), F32)
            for h in heads:
                dq_ref[rs, _hs(h)] = gcol[h] * _dot_nt(dO[h], Sb[h]) + _dot(dPG[h], K[h])
                dk_ref[rs, _hs(h)] = (_dot_tn(dPG[h], Q[h]) + _dot_tn(dMG[h], Kb[h]) + bcol[h] * dKb[h]
                                      - bg[h] * _dot_nt(dRb[h], Sb[h]) + dcol[h] * E[h])
                dv_ref[rs, _hs(h)] = bcol[h] * dR[h]
                dbeta = rsum(dKb[h] * K[h]) + rsum(dR[h] * (V[h] - gcol[h] * KS[h]))
                X = dP[h] * P[h] + dM[h] * M[h]
                ddel = rsum(K[h] * E[h]) * dcol[h]
                colsum = _dotf(X, ones, _TN)[:, 0:1]
                dc = (rsum(X) - colsum + gcol[h] * rsum(dO[h].astype(F32) * QS[h]) - bg[h] * rsum(dR[h] * KS[h])
                      - ddel)
                last = (jnp.sum(ddel, axis=0, keepdims=True)
                        + glast[h] * jnp.sum(rsum(dSn[h] * S[h]), axis=0, keepdims=True))
                dc_all = dc_all + jnp.where(lane == h, dc + jnp.where(rowl == C - 1, last, 0.0), 0.0)
                db_all = db_all + jnp.where(lane == N_HEAD + h, dbeta, 0.0)
            dgb_ref[rs, :] = _dotf(upper, dc_all) + db_all
            dSn = dS_new
        for h in heads:
            ds_ref[h] = dSn[h]
        dgn_ref[...] += dgn

    nb = nc // per
    cspec = lambda w, cb=0: pl.BlockSpec((per * C, w), lambda n: (nb - 1 - n, cb))
    hist = lambda a, b: pl.BlockSpec((per, N_HEAD, a, b), lambda n: (nb - 1 - n, 0, 0, 0))
    return pl.pallas_call(
        body,
        out_shape=(jax.ShapeDtypeStruct((T, BR_W), F32),) * 3 + (
            jax.ShapeDtypeStruct((T, LANES), F32), jax.ShapeDtypeStruct((T, BR_W), BF16),
            jax.ShapeDtypeStruct((1, D_HEAD), F32)),
        grid=(nb,),
        in_specs=[cspec(BR_W), cspec(BR_W), cspec(BR_W), cspec(LANES), cspec(BR_W, zcol), _full((1, D_HEAD)),
                  cspec(BR_W), hist(D_HEAD, D_HEAD), hist(C, C), cspec(BR_W), cspec(BR_W)],
        out_specs=(cspec(BR_W), cspec(BR_W), cspec(BR_W), cspec(LANES), cspec(BR_W), _full((1, D_HEAD))),
        scratch_shapes=[pltpu.VMEM((N_HEAD, D_HEAD, D_HEAD), F32)], name="gdn_chunk_bwd",
        compiler_params=_cp("arbitrary"))(q, k, v, gb, proj, gnorm, oraw, shist, tinv_all, vn_all, dog)


SB_COL = SB_OFF // BR_W
SB_SCALE = D_HEAD ** -0.5


def _sb_pre(proj, gq, gk):
    T = proj.shape[0]
    tm = min(TM, T)

    def body(xq_ref, xk_ref, xv_ref, gq_ref, gk_ref, q_ref, k_ref, v_ref):
        for h in range(N_HEAD):
            for x_ref, g_ref, ref, scale in ((xq_ref, gq_ref, q_ref, SB_SCALE), (xk_ref, gk_ref, k_ref, 1.0)):
                xh = x_ref[:, _hs(h)].astype(F32)
                r = lax.rsqrt(jnp.mean(xh * xh, axis=-1, keepdims=True) + EPS)
                ref[:, _hs(h)] = (xh * (r * scale) * g_ref[...]).astype(BF16)
        v_ref[...] = xv_ref[...]

    return pl.pallas_call(
        body, out_shape=(jax.ShapeDtypeStruct((T, BR_W), BF16),) * 3, grid=(T // tm,),
        in_specs=[_rowspec(tm, BR_W, SB_COL), _rowspec(tm, BR_W, SB_COL + 1), _rowspec(tm, BR_W, SB_COL + 2),
                  _full((1, D_HEAD)), _full((1, D_HEAD))],
        out_specs=(_rowspec(tm, BR_W),) * 3, name="sb_pre", compiler_params=_cp("parallel"))(proj, proj, proj, gq, gk)


def _sb_pre_bwd(proj, gq, gk, dq, dk, dv):
    T = proj.shape[0]
    tm = min(TM, T)

    def body(xq_ref, xk_ref, gq_ref, gk_ref, dq_ref, dk_ref, dv_ref, dx_ref, dgq_ref, dgk_ref):
        i = pl.program_id(0)

        @pl.when(i == 0)
        def _():
            dgq_ref[...] = jnp.zeros_like(dgq_ref)
            dgk_ref[...] = jnp.zeros_like(dgk_ref)

        for off, x_ref, g_ref, d_ref, dg_ref, scale in ((0, xq_ref, gq_ref, dq_ref, dgq_ref, SB_SCALE),
                                                        (BR_W, xk_ref, gk_ref, dk_ref, dgk_ref, 1.0)):
            dg = jnp.zeros((1, D_HEAD), F32)
            for h in range(N_HEAD):
                xh = x_ref[:, _hs(h)].astype(F32)
                r = lax.rsqrt(jnp.mean(xh * xh, axis=-1, keepdims=True) + EPS)
                y = xh * r
                dn = d_ref[:, _hs(h)] * scale
                dg = dg + jnp.sum(dn * y, axis=0, keepdims=True)
                dy = dn * g_ref[...]
                dx_ref[:, off + h * D_HEAD:off + (h + 1) * D_HEAD] = (
                    r * (dy - y * jnp.mean(dy * y, axis=-1, keepdims=True))).astype(BF16)
            dg_ref[...] += dg
        dx_ref[:, 2 * BR_W:] = dv_ref[...].astype(BF16)

    return pl.pallas_call(
        body,
        out_shape=(jax.ShapeDtypeStruct((T, 3 * BR_W), BF16), jax.ShapeDtypeStruct((1, D_HEAD), F32),
                   jax.ShapeDtypeStruct((1, D_HEAD), F32)),
        grid=(T // tm,),
        in_specs=[_rowspec(tm, BR_W, SB_COL), _rowspec(tm, BR_W, SB_COL + 1), _full((1, D_HEAD)), _full((1, D_HEAD)),
                  _rowspec(tm, BR_W), _rowspec(tm, BR_W), _rowspec(tm, BR_W)],
        out_specs=(_rowspec(tm, 3 * BR_W), _full((1, D_HEAD)), _full((1, D_HEAD))), name="sb_pre_bwd",
        compiler_params=_cp("arbitrary"))(proj, proj, gq, gk, dq, dk, dv)


def _sb_scores(q_ref, k_ref, lead):
    z = _dot_nt(q_ref[...], k_ref[...])
    zc = jnp.minimum(z, 30.0)
    sp = jnp.log(1.0 + jnp.exp(zc)) + (z - zc)
    if lead is None:
        return z, sp, None
    mask = _iota(z.shape, 1) - _iota(z.shape, 0) < -lead
    return z, jnp.where(mask, sp, 0.0), mask


def _sb_blocks(T):
    bq = min(SB_BLK_Q, T)
    bk = min(SB_BLK, bq)
    return bq, bk, min(SB_W, bk), T // bq, bq // bk


def _sb_fwd(sq, sk, sv):
    T = sq.shape[0]
    bq, bk, w, nq, ratio = _sb_blocks(T)
    nsub = bk // w

    def body(qi_ref, kj_ref, q_ref, k_ref, v_ref, o_ref, lt_ref, cut_ref, acc_ref, r_ref, live_ref):
        head, t = pl.program_id(0), pl.program_id(1)
        qi, kj = qi_ref[t], kj_ref[t]

        @pl.when(kj == ratio * qi + ratio - 1)
        def _():
            acc_ref[...] = jnp.zeros_like(acc_ref)
            r_ref[...] = jnp.zeros_like(r_ref)
            live_ref[0] = 1

        def block(masked):
            z, sp, mask = _sb_scores(q_ref, k_ref, kj * bk - qi * bq if masked else None)
            after = (_iota((w, w), 0) > _iota((w, w), 1)).astype(BF16)
            r = r_ref[...]
            acc = acc_ref[...]
            for sb in reversed(range(nsub)):
                cs = slice(sb * w, (sb + 1) * w)
                sps = sp[:, cs]
                a = jnp.exp(z[:, cs] - sps - _dot(sps, after) - r)
                if masked:
                    a = jnp.where(mask[:, cs], a, 0.0)
                acc = acc + _dot(a, v_ref[cs, :])
                r = r + jnp.sum(sps, axis=-1, keepdims=True)
            acc_ref[...] = acc
            r_ref[...] = r
            live_ref[0] = (jnp.min(r) < SB_DEAD).astype(jnp.int32)
            cut_ref[head, qi] = kj.astype(F32)

        live = live_ref[0] == 1
        pl.when(jnp.logical_and(live, kj >= ratio * qi))(functools.partial(block, True))
        pl.when(jnp.logical_and(live, kj < ratio * qi))(functools.partial(block, False))

        @pl.when(kj == 0)
        def _():
            o_ref[...] = acc_ref[...].astype(BF16)
            lt_ref[0] = r_ref[...]

    pairs = [(i, j) for i in range(nq) for j in range(ratio * (i + 1) - 1, -1, -1)]
    qi_tab, kj_tab = (jnp.asarray(np.array(c, np.int32)) for c in zip(*pairs))
    qspec = pl.BlockSpec((bq, D_HEAD), lambda h, t, qi, kj: (qi[t], h))
    kspec = pl.BlockSpec((bk, D_HEAD), lambda h, t, qi, kj: (kj[t], h))
    return pl.pallas_call(
        body,
        out_shape=(jax.ShapeDtypeStruct((T, BR_W), BF16), jax.ShapeDtypeStruct((N_HEAD, T, 1), F32),
                   jax.ShapeDtypeStruct((N_HEAD, nq), F32)),
        grid_spec=pltpu.PrefetchScalarGridSpec(
            num_scalar_prefetch=2, grid=(N_HEAD, len(pairs)), in_specs=[qspec, kspec, kspec],
            out_specs=(qspec, pl.BlockSpec((1, bq, 1), lambda h, t, qi, kj: (h, qi[t], 0)),
                       pl.BlockSpec(memory_space=pltpu.SMEM)),
            scratch_shapes=[pltpu.VMEM((bq, D_HEAD), F32), pltpu.VMEM((bq, 1), F32), pltpu.SMEM((1,), jnp.int32)]),
        name="sb_fwd", compiler_params=_cp("arbitrary", "arbitrary"))(qi_tab, kj_tab, sq, sk, sv)


def _sb_bwd(sq, sk, sv, ltot, do, cut):
    T = sq.shape[0]
    bq, bk, w, nq, ratio = _sb_blocks(T)
    nsub = bk // w

    def body(qi_ref, kj_ref, q_ref, k_ref, v_ref, lt_ref, do_ref, cut_ref, dq_ref, dk_ref, dv_ref, acc_ref, p_ref,
             g_ref):
        t = pl.program_id(1)
        qi, kj = qi_ref[t], kj_ref[t]
        cut = cut_ref[pl.program_id(0), qi].astype(jnp.int32)

        @pl.when(t == 0)
        def _():
            dk_ref[...] = jnp.zeros_like(dk_ref)
            dv_ref[...] = jnp.zeros_like(dv_ref)

        @pl.when(kj == cut)
        def _():
            acc_ref[...] = jnp.zeros_like(acc_ref)
            p_ref[...] = lt_ref[0]
            g_ref[...] = jnp.zeros_like(g_ref)

        def block(masked):
            z, sp, mask = _sb_scores(q_ref, k_ref, kj * bk - qi * bq if masked else None)
            d_a = _dot_nt(do_ref[...], v_ref[...])
            after = (_iota((w, w), 0) > _iota((w, w), 1)).astype(BF16)
            before = (_iota((w, w), 0) < _iota((w, w), 1)).astype(BF16)
            rest = p_ref[...]
            hg = g_ref[...]
            acc = acc_ref[...]
            base = pl.multiple_of(kj * bk, bk)
            for sb in range(nsub):
                cs = slice(sb * w, (sb + 1) * w)
                sps, zs = sp[:, cs], z[:, cs]
                rest = rest - jnp.sum(sps, axis=-1, keepdims=True)
                a = jnp.exp(zs - sps - _dot(sps, after) - rest)
                if masked:
                    a = jnp.where(mask[:, cs], a, 0.0)
                g = a * d_a[:, cs]
                sig = jnp.exp(zs - sps)
                dz = g - sig * (g + (hg + _dot(g, before)))
                if masked:
                    dz = jnp.where(mask[:, cs], dz, 0.0)
                dz = dz.astype(BF16)
                rows = pl.ds(base + sb * w, w)
                dv_ref[rows, :] += _dot_tn(a, do_ref[...])
                dk_ref[rows, :] += _dot_tn(dz, q_ref[...])
                acc = acc + _dot(dz, k_ref[cs, :])
                hg = hg + jnp.sum(g, axis=-1, keepdims=True)
            acc_ref[...] = acc
            p_ref[...] = rest
            g_ref[...] = hg

        pl.when(jnp.logical_and(kj >= cut, kj >= ratio * qi))(functools.partial(block, True))
        pl.when(jnp.logical_and(kj >= cut, kj < ratio * qi))(functools.partial(block, False))

        @pl.when(kj == ratio * qi + ratio - 1)
        def _():
            dq_ref[...] = acc_ref[...]

    pairs = [(i, j) for i in range(nq) for j in range(ratio * (i + 1))]
    qi_tab, kj_tab = (jnp.asarray(np.array(c, np.int32)) for c in zip(*pairs))
    qspec = pl.BlockSpec((bq, D_HEAD), lambda h, t, qi, kj: (qi[t], h))
    kspec = pl.BlockSpec((bk, D_HEAD), lambda h, t, qi, kj: (kj[t], h))
    full = pl.BlockSpec((T, D_HEAD), lambda h, t, qi, kj: (0, h))
    return pl.pallas_call(
        body, out_shape=(jax.ShapeDtypeStruct((T, BR_W), F32),) * 3,
        grid_spec=pltpu.PrefetchScalarGridSpec(
            num_scalar_prefetch=2, grid=(N_HEAD, len(pairs)),
            in_specs=[qspec, kspec, kspec, pl.BlockSpec((1, bq, 1), lambda h, t, qi, kj: (h, qi[t], 0)), qspec,
                      pl.BlockSpec(memory_space=pltpu.SMEM)],
            out_specs=(qspec, full, full),
            scratch_shapes=[pltpu.VMEM((bq, D_HEAD), F32), pltpu.VMEM((bq, 1), F32), pltpu.VMEM((bq, 1), F32)]),
        name="sb_bwd", compiler_params=_cp("arbitrary", "arbitrary"))(qi_tab, kj_tab, sq, sk, sv, ltot, do, cut)


def _mem_kv(mem, gm, w_kv, gk):
    def body(mem_ref, gm_ref, w_ref, gk_ref, mn_ref, kv_ref, kh_ref, vm_ref):
        mv = mem_ref[...]
        r = lax.rsqrt(jnp.mean(mv * mv, axis=-1, keepdims=True) + EPS)
        mn = (mv * r * gm_ref[...]).astype(BF16)
        mn_ref[...] = mn
        kv = lax.dot_general(mn, w_ref[...], _NN, preferred_element_type=F32)
        kv_ref[...] = kv
        for h in range(N_HEAD):
            kh = kv[:, _hs(h)]
            rk = lax.rsqrt(jnp.mean(kh * kh, axis=-1, keepdims=True) + EPS)
            kh_ref[:, _hs(h)] = (kh * rk * gk_ref[...]).astype(BF16)
        vm_ref[...] = kv[:, BR_W:].astype(BF16)

    return pl.pallas_call(
        body,
        out_shape=(jax.ShapeDtypeStruct((N_MEM, D_MODEL), BF16), jax.ShapeDtypeStruct((N_MEM, 2 * BR_W), F32),
                   jax.ShapeDtypeStruct((N_MEM, BR_W), BF16), jax.ShapeDtypeStruct((N_MEM, BR_W), BF16)),
        name="mem_kv", compiler_params=_cp())(mem, gm, w_kv, gk)


def _mem_q(x_ref, gq_ref, h):
    xh = x_ref[:, _hs(h)].astype(F32)
    r = lax.rsqrt(jnp.mean(xh * xh, axis=-1, keepdims=True) + EPS)
    return r, xh * r


def _mem_probs(qn, kh):
    s = _dot_nt(qn, kh) * (D_HEAD ** -0.5)
    e = jnp.exp(s - jnp.max(s, axis=-1, keepdims=True))
    return e / jnp.sum(e, axis=-1, keepdims=True)


def _mem_fwd(proj, kh, vm, gq):
    T = proj.shape[0]
    tm = min(TM, T)

    def body(x_ref, kh_ref, vm_ref, gq_ref, o_ref):
        for h in range(N_HEAD):
            _, y = _mem_q(x_ref, gq_ref, h)
            p = _mem_probs((y * gq_ref[...]).astype(BF16), kh_ref[:, _hs(h)])
            o_ref[:, _hs(h)] = _dot(p, vm_ref[:, _hs(h)]).astype(BF16)

    return pl.pallas_call(
        body, out_shape=jax.ShapeDtypeStruct((T, BR_W), BF16), grid=(T // tm,),
        in_specs=[_rowspec(tm, BR_W, MEMQ_OFF // BR_W), _full((N_MEM, BR_W)), _full((N_MEM, BR_W)),
                  _full((1, D_HEAD))],
        out_specs=_rowspec(tm, BR_W), name="mem_fwd", compiler_params=_cp("parallel"))(proj, kh, vm, gq)


def _mem_bwd(proj, kh, vm, gq, do):
    T = proj.shape[0]
    tm = min(TM, T)

    def body(x_ref, kh_ref, vm_ref, gq_ref, do_ref, dx_ref, dkh_ref, dvm_ref, dgq_ref):
        i = pl.program_id(0)

        @pl.when(i == 0)
        def _():
            dkh_ref[...] = jnp.zeros_like(dkh_ref)
            dvm_ref[...] = jnp.zeros_like(dvm_ref)
            dgq_ref[...] = jnp.zeros_like(dgq_ref)

        dg = jnp.zeros((1, D_HEAD), F32)
        for h in range(N_HEAD):
            r, y = _mem_q(x_ref, gq_ref, h)
            qn = (y * gq_ref[...]).astype(BF16)
            p = _mem_probs(qn, kh_ref[:, _hs(h)])
            dov = do_ref[:, _hs(h)]
            dp = _dot_nt(dov, vm_ref[:, _hs(h)])
            ds = p * (dp - jnp.sum(dp * p, axis=-1, keepdims=True)) * (D_HEAD ** -0.5)
            dqn = _dot(ds, kh_ref[:, _hs(h)])
            dkh_ref[:, _hs(h)] += _dot_tn(ds, qn)
            dvm_ref[:, _hs(h)] += _dot_tn(p, dov)
            dg = dg + jnp.sum(dqn * y, axis=0, keepdims=True)
            dy = dqn * gq_ref[...]
            dx_ref[:, _hs(h)] = (r * (dy - y * jnp.mean(dy * y, axis=-1, keepdims=True))).astype(BF16)
        dgq_ref[...] += dg

    return pl.pallas_call(
        body,
        out_shape=(jax.ShapeDtypeStruct((T, BR_W), BF16), jax.ShapeDtypeStruct((N_MEM, BR_W), F32),
                   jax.ShapeDtypeStruct((N_MEM, BR_W), F32), jax.ShapeDtypeStruct((1, D_HEAD), F32)),
        grid=(T // tm,),
        in_specs=[_rowspec(tm, BR_W, MEMQ_OFF // BR_W), _full((N_MEM, BR_W)), _full((N_MEM, BR_W)),
                  _full((1, D_HEAD)), _rowspec(tm, BR_W)],
        out_specs=(_rowspec(tm, BR_W), _full((N_MEM, BR_W)), _full((N_MEM, BR_W)), _full((1, D_HEAD))),
        name="mem_bwd", compiler_params=_cp("arbitrary"))(proj, kh, vm, gq, do)


def _mem_kv_bwd(mem, gm, w_kv, gk, kv, mn, dkh, dvm):
    def body(mem_ref, gm_ref, w_ref, gk_ref, kv_ref, mn_ref, dkh_ref, dvm_ref, dw_ref, dgm_ref, dgk_ref, dkv_ref):
        dgk = jnp.zeros((1, D_HEAD), F32)
        for h in range(N_HEAD):
            kh = kv_ref[:, _hs(h)]
            r = lax.rsqrt(jnp.mean(kh * kh, axis=-1, keepdims=True) + EPS)
            y = kh * r
            dn = dkh_ref[:, _hs(h)]
            dgk = dgk + jnp.sum(dn * y, axis=0, keepdims=True)
            dy = dn * gk_ref[...]
            dkv_ref[:, _hs(h)] = (r * (dy - y * jnp.mean(dy * y, axis=-1, keepdims=True))).astype(BF16)
        dkv_ref[:, BR_W:] = dvm_ref[...].astype(BF16)
        dgk_ref[...] = dgk
        dkv = dkv_ref[...]
        dw_ref[...] = lax.dot_general(mn_ref[...], dkv, _TN, preferred_element_type=F32)
        dmn = lax.dot_general(dkv, w_ref[...], _NT, preferred_element_type=F32)
        mv = mem_ref[...]
        memn = mv * lax.rsqrt(jnp.mean(mv * mv, axis=-1, keepdims=True) + EPS)
        dgm_ref[...] = jnp.sum(dmn * memn, axis=0, keepdims=True)

    return pl.pallas_call(
        body,
        out_shape=(jax.ShapeDtypeStruct((D_MODEL, 2 * BR_W), F32), jax.ShapeDtypeStruct((1, D_MODEL), F32),
                   jax.ShapeDtypeStruct((1, D_HEAD), F32)),
        scratch_shapes=[pltpu.VMEM((N_MEM, 2 * BR_W), BF16)], name="mem_kv_bwd",
        compiler_params=_cp())(mem, gm, w_kv, gk, kv, mn, dkh, dvm)


def _merge_fwd(og, osb, om, proj, wg, ws, wm):
    T = og.shape[0]
    tm = min(TM, T)

    def body(og_ref, os_ref, om_ref, g0, g1, g2, wg_ref, ws_ref, wm_ref, mix_ref, yg_ref, ys_ref, ym_ref):
        mix = jnp.zeros((tm, D_MODEL), F32)
        for o_ref, gl_ref, w_ref, y_ref in ((og_ref, g0, wg_ref, yg_ref), (os_ref, g1, ws_ref, ys_ref),
                                            (om_ref, g2, wm_ref, ym_ref)):
            y = lax.dot_general(o_ref[...], w_ref[...], _NN, preferred_element_type=F32)
            y_ref[...] = y.astype(BF16)
            mix = mix + _sigmoid(gl_ref[...].astype(F32)) * y
        mix_ref[...] = mix.astype(BF16)

    br = _rowspec(tm, BR_W)
    wspec = _full((BR_W, D_MODEL))
    out = _rowspec(tm, D_MODEL)
    gates = [_rowspec(tm, D_MODEL, GATE_COL + b) for b in range(3)]
    return pl.pallas_call(
        body, out_shape=(jax.ShapeDtypeStruct((T, D_MODEL), BF16),) * 4, grid=(T // tm,),
        in_specs=[br, br, br, *gates, wspec, wspec, wspec],
        out_specs=(out,) * 4, name="merge_fwd",
        compiler_params=_cp("parallel"))(og, osb, om, proj, proj, proj, wg, ws, wm)


def _merge_bwd(dmix, proj, ys, os_, ws):
    T = dmix.shape[0]
    tm = min(TM, T)

    def body(dmix_ref, g0, g1, g2, y0, y1, y2, o0, o1, o2, w0, w1, w2, dgl_ref, do0, do1, do2, dw0, dw1, dw2):
        i = pl.program_id(0)
        dm = dmix_ref[...].astype(F32)
        for b, (gl_ref, y_ref, o_ref, w_ref, do_ref, dw_ref) in enumerate((
                (g0, y0, o0, w0, do0, dw0), (g1, y1, o1, w1, do1, dw1), (g2, y2, o2, w2, do2, dw2))):
            gate = _sigmoid(gl_ref[...].astype(F32))
            dgl_ref[:, b * D_MODEL:(b + 1) * D_MODEL] = (dm * y_ref[...].astype(F32) * gate * (1.0 - gate)).astype(BF16)
            dy = (gate * dm).astype(BF16)
            do_ref[...] = lax.dot_general(dy, w_ref[...], _NT, preferred_element_type=F32).astype(BF16)
            _accum(dw_ref, i == 0, lax.dot_general(dy, o_ref[...], _TN, preferred_element_type=F32))

    br = _rowspec(tm, BR_W)
    wide = _rowspec(tm, D_MODEL)
    wspec = _full((BR_W, D_MODEL))
    wtspec = _full((D_MODEL, BR_W))
    gates = [_rowspec(tm, D_MODEL, GATE_COL + b) for b in range(3)]
    return pl.pallas_call(
        body,
        out_shape=(jax.ShapeDtypeStruct((T, 3 * D_MODEL), BF16),) + (jax.ShapeDtypeStruct((T, BR_W), BF16),) * 3
        + (jax.ShapeDtypeStruct((D_MODEL, BR_W), F32),) * 3,
        grid=(T // tm,),
        in_specs=[wide, *gates, wide, wide, wide, br, br, br, wspec, wspec, wspec],
        out_specs=(_rowspec(tm, 3 * D_MODEL), br, br, br, wtspec, wtspec, wtspec), name="merge_bwd",
        compiler_params=_cp("arbitrary"))(dmix, proj, proj, proj, *ys, *os_, *ws)


def _loss(y, tgt):
    T, dm = y.shape
    tm = min(TM, T)

    def body(y_ref, t_ref, dy_ref, dyb_ref, sq_ref):
        err = y_ref[...] - t_ref[...]
        dy = err * (1.0 / dm)
        dy_ref[...] = dy
        dyb_ref[...] = dy.astype(BF16)
        _accum(sq_ref, pl.program_id(0) == 0, jnp.sum(err * err, axis=0, keepdims=True))

    return pl.pallas_call(
        body,
        out_shape=(jax.ShapeDtypeStruct((T, dm), F32), jax.ShapeDtypeStruct((T, dm), BF16),
                   jax.ShapeDtypeStruct((1, dm), F32)),
        grid=(T // tm,), in_specs=[_rowspec(tm, dm), _rowspec(tm, dm)],
        out_specs=(_rowspec(tm, dm), _rowspec(tm, dm), _full((1, dm))), name="loss",
        compiler_params=_cp("arbitrary"))(y, tgt)


def _local_step(x, mem, tgt, W, P, dist=None):
    w_in = W["w_in"]
    w_main = jnp.concatenate([w_in[:, :SB_OFF], w_in[:, SB_OFF + 8:]], axis=1)
    w_ab = jnp.pad(w_in[:, SB_OFF:SB_OFF + 8], ((0, 0), (0, LANES - 8)))
    avec = jnp.pad(jnp.concatenate([P["a_log"], P["dt_bias"]], axis=0), ((0, 0), (0, LANES - N_HEAD)))

    h = _rms_fwd(x, P["norm1_g"], "rms1")
    if dist is None:
        proj = _mm(h, w_main, "nn", BF16, "in_proj")
    else:
        proj, gathered = _mm(h, w_main, "nn", BF16, "in_proj", comm=dist.gather_rest())
        rest, conv_w = dist.weights_from(gathered)
        W, P = {**W, **rest}, {**P, "conv_w": conv_w}
    wbr = (W["w_br_gdn"], W["w_br_sb"], W["w_br_mem"])
    ab = _mm(h, w_ab, "nn", F32, "in_proj_ab")
    q, k, v, gb = _gdn_pre(proj, P["conv_w"], ab, avec)
    tinv = _gdn_inv(k, gb)
    og, oraw, shist, vn = _gdn_fwd(q, k, v, gb, proj, P["gdn_norm_g"], tinv)
    sq, sk, sv = _sb_pre(proj, P["sb_q_norm_g"], P["sb_k_norm_g"])
    osb, ltot, cut = _sb_fwd(sq, sk, sv)
    mn, kv, kh, vm = _mem_kv(mem, P["mem_norm_g"], W["w_mem_kv"], P["mem_k_norm_g"])
    om = _mem_fwd(proj, kh, vm, P["mem_q_norm_g"])
    mix, yg, ys, ym = _merge_fwd(og, osb, om, proj, *wbr)
    x1 = _mm(mix, W["w_o"], "nn", F32, "out_proj", extra=x, epi=_epi_add)
    h2 = _rms_fwd(x1, P["norm2_g"], "rms2")
    u = _mm(h2, W["w_up"], "nn", BF16, "mlp_up")
    y = _mm(u, W["w_down"], "nn", F32, "mlp_down", a_fn=_relu2, extra=x1, epi=_epi_add)
    dy, dyb, sq_err = _loss(y, tgt)

    G = {}
    du = _mm(dyb, W["w_down"], "nt", BF16, "d_mlp_act", extra=u, epi=_epi_drelu2)
    G["w_down"] = _mm(u, dyb, "tn", F32, "dw_down", a_fn=_relu2)
    G["w_up"] = _mm(du, h2, "tn", F32, "dw_up")
    dh2 = _mm(du, W["w_up"], "nt", F32, "d_h2")
    dx1, dx1b, G["norm2_g"] = _rms_bwd(dh2, x1, P["norm2_g"], dy, "rms2_bwd")
    dmix = _mm(dx1b, W["w_o"], "nt", BF16, "d_mix")
    G["w_o"] = _mm(mix, dx1b, "tn", F32, "dw_o")
    dgates, dog, dosb, dom, G["w_br_gdn"], G["w_br_sb"], G["w_br_mem"] = _merge_bwd(
        dmix, proj, (yg, ys, ym), (og, osb, om), wbr)
    dq, dk, dv, dgb, dz, G["gdn_norm_g"] = _gdn_bwd(q, k, v, gb, proj, P["gdn_norm_g"], oraw, shist, tinv, vn, dog)
    dxc, dab, G["conv_w"], dav = _gdn_pre_bwd(proj, P["conv_w"], ab, avec, dq, dk, dv, dgb)
    dqkv = _conv_bwd(dxc, P["conv_w"])
    G["a_log"], G["dt_bias"] = dav[0:1, :N_HEAD], dav[1:2, :N_HEAD]
    dsq, dsk, dsv = _sb_bwd(sq, sk, sv, ltot, dosb, cut)
    dsb, G["sb_q_norm_g"], G["sb_k_norm_g"] = _sb_pre_bwd(proj, P["sb_q_norm_g"], P["sb_k_norm_g"], dsq, dsk, dsv)
    dmemq, dkh, dvm, G["mem_q_norm_g"] = _mem_bwd(proj, kh, vm, P["mem_q_norm_g"], dom)
    G["w_mem_kv"], G["mem_norm_g"], G["mem_k_norm_g"] = _mem_kv_bwd(
        mem, P["mem_norm_g"], W["w_mem_kv"], P["mem_k_norm_g"], kv, mn, dkh, dvm)
    dproj = jnp.concatenate([dqkv, dz, dsb, dmemq, dgates], axis=1)
    dw_ab = _mm(dab, h, "tn", F32, "dw_in_ab")
    if dist is None:
        dw_main = _mm(dproj, h, "tn", F32, "dw_in")
    else:
        early = [n for n, _, _ in BIG if n != "w_in"]
        dw_main, landed = _mm(dproj, h, "tn", F32, "dw_in", comm=dist.scatter(G, early, "early"))
        dist.collect(early, landed)
    G["w_in"] = jnp.concatenate([dw_main[:SB_OFF], dw_ab[:8], dw_main[SB_OFF:]], axis=0)
    if dist is None:
        dh = _mm(dproj, w_main, "nt", F32, "d_h")
    else:
        dh, landed = _mm(dproj, w_main, "nt", F32, "d_h", comm=dist.scatter(G, ["w_in"], "late"))
        dist.collect(["w_in"], landed)
    dh = _mm(dab, w_ab, "nt", F32, "d_h_ab", extra=dh, epi=_epi_add)
    dx, _, G["norm1_g"] = _rms_bwd(dh, x, P["norm1_g"], dx1, "rms1_bwd")
    return sq_err, dx, G


def _comm(name, ins, out_shapes, plan):
    n_in, n_out = len(ins), len(out_shapes)
    probe = plan([None] * n_in, [None] * n_out, 0, 0, 0, dry=True)
    n_copy = probe

    def body(*refs):
        in_refs, out_refs = refs[:n_in], refs[n_in:n_in + n_out]
        send_sems, recv_sems = refs[n_in + n_out:]
        x, y, c = lax.axis_index("x"), lax.axis_index("y"), lax.axis_index("c")
        copies = []
        for k, (src, dst, dev) in enumerate(plan(in_refs, out_refs, x, y, c, dry=False)):
            if dev is None:
                cp = pltpu.make_async_copy(src, dst, send_sems.at[k])
            else:
                cp = pltpu.make_async_remote_copy(src_ref=src, dst_ref=dst, send_sem=send_sems.at[k],
                                                  recv_sem=recv_sems.at[k], device_id=dev, device_id_type=MESH)
            cp.start()
            copies.append(cp)
        for cp in copies:
            cp.wait()

    return pl.pallas_call(
        body, out_shape=tuple(out_shapes), in_specs=[HBM] * n_in, out_specs=tuple([HBM] * n_out),
        scratch_shapes=[pltpu.SemaphoreType.DMA((n_copy,)), pltpu.SemaphoreType.DMA((n_copy,))], name=name)(*ins)


def _other_chips(x, y):
    return ((1 - x, y), (x, 1 - y), (1 - x, 1 - y))


def _gather_plan(parts, direct=()):
    n, every = len(parts), list(parts) + list(direct)

    def copies(ins, outs, send, recv, scratch):
        x, y, c = lax.axis_index("x"), lax.axis_index("y"), lax.axis_index("c")
        me = 2 * x + y
        chips = _other_chips(x, y)
        local_sems, staged = scratch[0], scratch[1:]

        def remote(src, dst, k, dev):
            return pltpu.make_async_remote_copy(src_ref=src, dst_ref=dst, send_sem=send.at[k], recv_sem=recv.at[k],
                                                device_id=dev, device_id_type=MESH)

        def half(p, ci):
            hr = ins[p].shape[0] // 2
            return pl.ds(pl.multiple_of(ci * hr, 16), hr)

        sent = [remote(ins[p].at[half(p, c)], outs[p].at[me, half(p, c)], 6 * p + f, (px, py, c))
                for p in range(n) for f, (px, py) in enumerate(chips)]
        sent += [remote(ins[p], outs[p].at[me], 6 * n + 3 * (p - n) + f, (px, py, c))
                 for p in range(n, len(every)) for f, (px, py) in enumerate(chips)]
        landed = [outs[p].at[2 * px + py, half(p, c)] for p in range(n) for px, py in chips]
        passed = [remote(landed[3 * p + f], landed[3 * p + f], 6 * p + 3 + f, (x, y, 1 - c))
                  for p in range(n) for f in range(3)]
        loads = [pltpu.make_async_copy(ins[p], staged[p], local_sems.at[2 * p]) for p in range(len(every))]
        stores = [pltpu.make_async_copy(staged[p], outs[p].at[me], local_sems.at[2 * p + 1]) for p in range(len(every))]
        return sent, passed, loads, stores

    def start(*refs):
        sent, _, loads, _ = copies(*refs)
        for cp in loads + sent:
            cp.start()

    def mid(*refs):
        sent, passed, loads, stores = copies(*refs)
        for ld, st in zip(loads, stores):
            ld.wait()
            st.start()
        for p in range(n):
            for f in range(3):
                sent[3 * p + f].wait_recv()
                passed[3 * p + f].start()

    def finish(*refs):
        sent, passed, _, stores = copies(*refs)
        for cp in sent[:3 * n]:
            cp.wait_send()
        for cp in passed + sent[3 * n:] + stores:
            cp.wait()

    return _Hosted(every, [jax.ShapeDtypeStruct((4,) + p.shape, p.dtype) for p in every], 6 * n + 3 * len(direct),
                   start, finish, mid,
                   [pltpu.SemaphoreType.DMA((2 * len(every),))] + [pltpu.VMEM(p.shape, p.dtype) for p in every])


def _scatter_plan(pairs):
    def copies(ins, outs, send, recv, scratch):
        x, y, c = lax.axis_index("x"), lax.axis_index("y"), lax.axis_index("c")
        me = 2 * x + y
        return [pltpu.make_async_remote_copy(src_ref=src.at[2 * px + py], dst_ref=dst.at[me], send_sem=send.at[3 * p + f],
                                             recv_sem=recv.at[3 * p + f], device_id=(px, py, c), device_id_type=MESH)
                for p, (src, dst) in enumerate(zip(ins, outs)) for f, (px, py) in enumerate(_other_chips(x, y))]

    def start(*refs):
        for cp in copies(*refs):
            cp.start()

    def finish(*refs):
        for cp in copies(*refs):
            cp.wait()

    return _Hosted(pairs, [jax.ShapeDtypeStruct(a.shape, a.dtype) for a in pairs], 3 * len(pairs), start, finish)


def _run_hosted(comm, name):
    n_in, n_out = len(comm.ins), len(comm.out_shapes)

    def body(*refs):
        args = (refs[:n_in], refs[n_in:n_in + n_out], refs[n_in + n_out], refs[n_in + n_out + 1], refs[n_in + n_out + 2:])
        comm.start(*args)
        if comm.mid is not None:
            comm.mid(*args)
        comm.finish(*args)

    sems = [pltpu.SemaphoreType.DMA((comm.n_sems,)), pltpu.SemaphoreType.DMA((comm.n_sems,))]
    return list(pl.pallas_call(
        body, out_shape=tuple(comm.out_shapes), in_specs=[HBM] * n_in, out_specs=tuple([HBM] * n_out),
        scratch_shapes=sems + comm.scratch, name=name, compiler_params=_cp())(*comm.ins))


def _swap_halves(slabs, name):
    n = len(slabs)

    def body(*refs):
        ins, outs = refs[:n], refs[n:2 * n]
        send, recv = refs[2 * n:]
        x, y, c = lax.axis_index("x"), lax.axis_index("y"), lax.axis_index("c")
        other = (x, y, 1 - c)
        for p in range(n):
            for j in range(4):
                pltpu.make_async_remote_copy(src_ref=ins[p].at[j, 1 - c], dst_ref=outs[p].at[j], send_sem=send.at[p],
                                             recv_sem=recv.at[p], device_id=other, device_id_type=MESH).start()
        for p in range(n):
            pltpu.make_async_remote_copy(src_ref=outs[p], dst_ref=outs[p], send_sem=send.at[p], recv_sem=recv.at[p],
                                         device_id=other, device_id_type=MESH).wait()

    shapes = [jax.ShapeDtypeStruct((4,) + s.shape[2:], s.dtype) for s in slabs]
    return pl.pallas_call(
        body, out_shape=tuple(shapes), in_specs=[HBM] * n, out_specs=tuple([HBM] * n),
        scratch_shapes=[pltpu.SemaphoreType.DMA((n,)), pltpu.SemaphoreType.DMA((n,))], name=name)(*slabs)


def _join_halves(both):
    n = len(both)

    def body(*refs):
        bufs = refs[n:2 * n]
        send, recv = refs[2 * n:]
        x, y, c = lax.axis_index("x"), lax.axis_index("y"), lax.axis_index("c")
        copies = []
        for p in range(n):
            cp = pltpu.make_async_remote_copy(src_ref=bufs[p].at[c], dst_ref=bufs[p].at[c], send_sem=send.at[p],
                                              recv_sem=recv.at[p], device_id=(x, y, 1 - c), device_id_type=MESH)
            cp.start()
            copies.append(cp)
        for cp in copies:
            cp.wait()

    return pl.pallas_call(
        body, out_shape=tuple(jax.ShapeDtypeStruct(a.shape, a.dtype) for a in both), in_specs=[HBM] * n,
        out_specs=tuple([HBM] * n), input_output_aliases={p: p for p in range(n)},
        scratch_shapes=[pltpu.SemaphoreType.DMA((n,)), pltpu.SemaphoreType.DMA((n,))], name="grad_join_cores")(*both)


def _gather_all(a, name):
    def plan(ins, outs, x, y, c, dry):
        if dry:
            return 8
        me = 4 * x + 2 * y + c
        copies = [(ins[0], outs[0].at[me], None)]
        for f in range(1, 8):
            peer = (1 - x if f & 4 else x, 1 - y if f & 2 else y, 1 - c if f & 1 else c)
            copies.append((ins[0], outs[0].at[me], peer))
        return copies

    return _comm(name, [a], [jax.ShapeDtypeStruct((8,) + a.shape, a.dtype)], plan)[0]


def _sum_slots(a, name, extra=None):
    n, R, _ = a.shape
    rb = min(ROW_BLK, R)

    def body(*refs):
        a_ref, o_ref = refs[0], refs[-1]
        acc = a_ref[0]
        for s in range(1, n):
            acc = acc + a_ref[s]
        if extra is not None:
            acc = acc + refs[1][...]
        o_ref[...] = acc

    ins = [a] + ([extra] if extra is not None else [])
    in_specs = [pl.BlockSpec((n, rb, LANES), lambda i: (0, i, 0))] + ([_rowspec(rb, LANES)] if extra is not None else [])
    return pl.pallas_call(
        body, out_shape=jax.ShapeDtypeStruct((R, LANES), F32), grid=(R // rb,), in_specs=in_specs,
        out_specs=_rowspec(rb, LANES), name=name, compiler_params=_cp("parallel"))(*ins)


def _pair_sum(slab, theirs, core, name):
    _, _, hr, C = slab.shape

    def body(c_ref, a_ref, b_ref, o_ref):
        o_ref[...] = (a_ref[...] + b_ref[...]).astype(BF16)

    return pl.pallas_call(
        body, out_shape=jax.ShapeDtypeStruct((4, hr, C), BF16),
        grid_spec=pltpu.PrefetchScalarGridSpec(
            num_scalar_prefetch=1, grid=(4,),
            in_specs=[pl.BlockSpec((None, None, hr, C), lambda j, c_ref: (j, c_ref[0], 0, 0)),
                      pl.BlockSpec((None, hr, C), lambda j, c_ref: (j, 0, 0))],
            out_specs=pl.BlockSpec((None, hr, C), lambda j, c_ref: (j, 0, 0))),
        name=name, compiler_params=_cp("parallel"))(core, slab, theirs)


def _chip_sum(recv, pairs, where, name):
    _, hr, C = recv.shape

    def body(w_ref, r_ref, p_ref, o_ref):
        me = w_ref[0]
        o_ref[...] = jnp.zeros_like(o_ref)
        for s in range(4):
            @pl.when(me == s)
            def _():
                o_ref[...] += p_ref[...].astype(F32)

            @pl.when(me != s)
            def _():
                o_ref[...] += r_ref[s].astype(F32)

    return pl.pallas_call(
        body, out_shape=jax.ShapeDtypeStruct((2, hr, C), F32),
        grid_spec=pltpu.PrefetchScalarGridSpec(
            num_scalar_prefetch=1, grid=(1,),
            in_specs=[pl.BlockSpec((4, hr, C), lambda i, w_ref: (0, 0, 0)),
                      pl.BlockSpec((None, hr, C), lambda i, w_ref: (w_ref[0], 0, 0))],
            out_specs=pl.BlockSpec((None, hr, C), lambda i, w_ref: (w_ref[1], 0, 0))),
        name=name, compiler_params=_cp("arbitrary"))(where, recv, pairs)


def _adamw(w, g, m, v, name):
    R, C = w.shape
    rb = min(ADAM_ROWS, R)
    c1 = 1.0 - ADAM_B1 ** ADAM_STEP
    c2 = 1.0 - ADAM_B2 ** ADAM_STEP

    def body(w_ref, g_ref, m_ref, v_ref, d_ref, nm_ref, nv_ref):
        gv = g_ref[...]
        nm = ADAM_B1 * m_ref[...] + (1.0 - ADAM_B1) * gv
        nv = ADAM_B2 * v_ref[...] + (1.0 - ADAM_B2) * (gv * gv)
        d_ref[...] = -ADAM_LR * ((nm / c1) / (jnp.sqrt(nv / c2) + ADAM_EPS) + ADAM_WD * w_ref[...])
        nm_ref[...] = nm
        nv_ref[...] = nv

    spec = _rowspec(rb, C)
    return pl.pallas_call(
        body, out_shape=(jax.ShapeDtypeStruct((R, C), F32),) * 3, grid=(R // rb,), in_specs=[spec] * 4,
        out_specs=(spec,) * 3, name=name, compiler_params=_cp("parallel"))(w, g, m, v)


class _Dist:
    def __init__(self, shards):
        self.shards = shards
        self.chip = 2 * lax.axis_index("x") + lax.axis_index("y")
        self.where = jnp.stack([self.chip, lax.axis_index("c")]).astype(jnp.int32)
        self.pairs, self.landed = {}, {}

    @staticmethod
    def _unshard(name, blk):
        _, (r, cc), axis = next(b for b in BIG if b[0] == name)
        return blk.reshape(4 * r, cc) if axis == 0 else blk.transpose(1, 0, 2).reshape(r, 4 * cc)

    def gather_first(self):
        got = _run_hosted(_gather_plan([self.shards["w_in"].astype(BF16)]), "gather_w_in")
        return self._unshard("w_in", got[0])

    def gather_rest(self):
        rest = [self.shards[n].astype(BF16) for n, _, _ in BIG if n != "w_in"]
        return _gather_plan(rest, [self.shards["conv_w"]])

    def weights_from(self, gathered):
        names = [n for n, _, _ in BIG if n != "w_in"]
        conv = gathered[-1]
        taps, width = conv.shape[1:]
        return ({n: self._unshard(n, g) for n, g in zip(names, gathered)},
                conv.transpose(1, 0, 2).reshape(taps, 4 * width))

    def scatter(self, G, names, tag):
        slabs = []
        for name, (r, cc), axis in BIG:
            if name not in names:
                continue
            g = G[name]
            if axis == 0:
                slabs.append(g.reshape(4, 2, r // 2, cc))
            else:
                slabs.append(g.reshape(4, 2, cc // 2, r))
        theirs = _swap_halves(slabs, "grad_swap_cores_" + tag)
        pairs = [_pair_sum(s, t, self.where[1:], "pair_sum_" + n) for s, t, n in zip(slabs, theirs, names)]
        self.pairs.update(zip(names, pairs))
        return _scatter_plan(pairs)

    def collect(self, names, landed):
        self.landed.update(zip(names, landed))

    def finish(self):
        names = [n for n, _, _ in BIG]
        halves = [_chip_sum(self.landed[n], self.pairs[n], self.where, "chip_sum_" + n) for n in names]
        out = {}
        for (name, (r, cc), axis), both in zip(BIG, _join_halves(halves)):
            full = both.reshape(-1, both.shape[-1])
            out[name] = full if axis == 0 else full.T
        return out


def _pack_rows(parts, rows, dtype):
    flat = jnp.concatenate([p.reshape(-1).astype(dtype) for p in parts])
    return jnp.pad(flat, (0, rows * LANES - flat.shape[0])).reshape(rows, LANES)


def _small_rows(n):
    return max(n // LANES, 1)


def _pack_small(vals):
    rows = []
    for name, n in SMALL:
        r = _small_rows(n)
        rows.append(jnp.pad(vals[name].reshape(-1), (0, r * LANES - n)).reshape(r, LANES))
    flat = jnp.concatenate(rows, axis=0)
    return jnp.pad(flat, ((0, SMALL_ROWS - flat.shape[0]), (0, 0)))


def _unpack_small(pack):
    out, r0 = {}, 0
    for name, n in SMALL:
        r = _small_rows(n)
        out[name] = pack[r0:r0 + r].reshape(-1)[:n]
        r0 += r
    return out


def kernel(x, mem, norm1_g, w_in, conv_w, a_log, dt_bias, gdn_norm_g, sb_q_norm_g, sb_k_norm_g, mem_norm_g, w_mem_kv, mem_q_norm_g, mem_k_norm_g, w_br_gdn, w_br_sb, w_br_mem, w_o, norm2_g, w_up, w_down, loss_target, m_norm1_g, m_w_in, m_conv_w, m_a_log, m_dt_bias, m_gdn_norm_g, m_sb_q_norm_g, m_sb_k_norm_g, m_mem_norm_g, m_w_mem_kv, m_mem_q_norm_g, m_mem_k_norm_g, m_w_br_gdn, m_w_br_sb, m_w_br_mem, m_w_o, m_norm2_g, m_w_up, m_w_down, v_norm1_g, v_w_in, v_conv_w, v_a_log, v_dt_bias, v_gdn_norm_g, v_sb_q_norm_g, v_sb_k_norm_g, v_mem_norm_g, v_w_mem_kv, v_mem_q_norm_g, v_mem_k_norm_g, v_w_br_gdn, v_w_br_sb, v_w_br_mem, v_w_o, v_norm2_g, v_w_up, v_w_down):
    wd = dict(norm1_g=norm1_g, w_in=w_in, conv_w=conv_w, a_log=a_log, dt_bias=dt_bias, gdn_norm_g=gdn_norm_g,
              sb_q_norm_g=sb_q_norm_g, sb_k_norm_g=sb_k_norm_g, mem_norm_g=mem_norm_g, w_mem_kv=w_mem_kv,
              mem_q_norm_g=mem_q_norm_g, mem_k_norm_g=mem_k_norm_g, w_br_gdn=w_br_gdn, w_br_sb=w_br_sb,
              w_br_mem=w_br_mem, w_o=w_o, norm2_g=norm2_g, w_up=w_up, w_down=w_down)
    md = dict(norm1_g=m_norm1_g, w_in=m_w_in, conv_w=m_conv_w, a_log=m_a_log, dt_bias=m_dt_bias,
              gdn_norm_g=m_gdn_norm_g, sb_q_norm_g=m_sb_q_norm_g, sb_k_norm_g=m_sb_k_norm_g,
              mem_norm_g=m_mem_norm_g, w_mem_kv=m_w_mem_kv, mem_q_norm_g=m_mem_q_norm_g,
              mem_k_norm_g=m_mem_k_norm_g, w_br_gdn=m_w_br_gdn, w_br_sb=m_w_br_sb, w_br_mem=m_w_br_mem, w_o=m_w_o,
              norm2_g=m_norm2_g, w_up=m_w_up, w_down=m_w_down)
    vd = dict(norm1_g=v_norm1_g, w_in=v_w_in, conv_w=v_conv_w, a_log=v_a_log, dt_bias=v_dt_bias,
              gdn_norm_g=v_gdn_norm_g, sb_q_norm_g=v_sb_q_norm_g, sb_k_norm_g=v_sb_k_norm_g,
              mem_norm_g=v_mem_norm_g, w_mem_kv=v_w_mem_kv, mem_q_norm_g=v_mem_q_norm_g,
              mem_k_norm_g=v_mem_k_norm_g, w_br_gdn=v_w_br_gdn, w_br_sb=v_w_br_sb, w_br_mem=v_w_br_mem, w_o=v_w_o,
              norm2_g=v_norm2_g, w_up=v_w_up, w_down=v_w_down)
    wd, md, vd = ({n: a[0] for n, a in d.items()} for d in (wd, md, vd))
    chip = 2 * lax.axis_index("x") + lax.axis_index("y")
    conv_shard = wd["conv_w"].shape

    dist = _Dist(wd)
    W = {"w_in": dist.gather_first()}
    P = {n: wd[n].reshape(1, -1) for n, _ in SMALL}

    sq_err, grad_x, G = _local_step(x[0], mem[0], loss_target[0], W, P, dist)
    loss = lax.psum(0.5 / D_MODEL * jnp.sum(sq_err), ("x", "y", "c"))

    g_big = dist.finish()

    spack = jnp.concatenate([_pack_small(G), G["conv_w"].reshape(CONV_ROWS, LANES)], axis=0)
    g_small = _sum_slots(_gather_all(spack, "gather_small_grads"), "small_grad_sum")
    g_conv_full = g_small[SMALL_ROWS:].reshape(conv_shard[0], 4 * conv_shard[1])
    g_conv = lax.dynamic_slice_in_dim(g_conv_full, chip * conv_shard[1], conv_shard[1], axis=1)

    grads, deltas, new_m, new_v = dict(g_big), {}, {}, {}
    for name, _, _ in BIG:
        deltas[name], new_m[name], new_v[name] = _adamw(wd[name], g_big[name], md[name], vd[name], "adamw_" + name)
    pack_sm = lambda d: jnp.concatenate([_pack_small(d), _pack_rows([d["conv_w"]], APACK_ROWS - SMALL_ROWS, F32)], axis=0)
    g_sm = jnp.concatenate([g_small[:SMALL_ROWS], _pack_rows([g_conv], APACK_ROWS - SMALL_ROWS, F32)], axis=0)
    small = (g_sm,) + _adamw(pack_sm(wd), g_sm, pack_sm(md), pack_sm(vd), "adamw_small")
    for out, pack in zip((grads, deltas, new_m, new_v), small):
        out.update(_unpack_small(pack[:SMALL_ROWS]))
        out["conv_w"] = pack[SMALL_ROWS:].reshape(-1)[:conv_shard[0] * conv_shard[1]].reshape(conv_shard)

    return (loss, grad_x[None], *[d[n][None] for d in (grads, deltas, new_m, new_v) for n in WEIGHTS])
```

```python
import jax
import jax.numpy as jnp
from jax import lax
from jax.experimental import pallas as pl
from jax.experimental.pallas import tpu as pltpu

F32 = jnp.float32
BF16 = jnp.bfloat16
MESH = pl.DeviceIdType.MESH

D_MODEL = 1024
N_HEAD = 4
D_HEAD = 128
BR_W = N_HEAD * D_HEAD
CONV_TAPS = 4
GDN_CHUNK = 64
INV_BLOCK = 16
INV_CHUNKS = 4
N_MEM = 256
D_FF = 4 * D_MODEL
EPS = 1e-6
LANES = 128
PROJ_W = 7168
GATE_OFF = 4096
SB_OFF = 2048
MEMQ_OFF = 3584
Z_OFF = 1536

ADAM_LR, ADAM_B1, ADAM_B2, ADAM_EPS, ADAM_WD, ADAM_STEP = 0.001, 0.9, 0.999, 1e-08, 0.01, 10

TM = 512
MM_TM = 1024
TK_TOK = 1024
GDN_STEP_CHUNKS = 4
GDN_BWD_STEP_CHUNKS = 1
G1_TM = 256
SB_BLK = 512
SB_W = 256
SB_DEAD = 120.0
VMEM_LIMIT = 48 << 20

BIG = (("w_in", (1024, 1794), 1), ("w_mem_kv", (256, 1024), 0), ("w_br_gdn", (512, 256), 1),
       ("w_br_sb", (512, 256), 1), ("w_br_mem", (512, 256), 1), ("w_o", (256, 1024), 0),
       ("w_up", (1024, 1024), 1), ("w_down", (1024, 1024), 0))
COL_SHARDED = tuple(n for n, _, a in BIG if a == 1)
GATE_COL = GATE_OFF // D_MODEL
ROW_BLK = 1024
ADAM_ROWS = 128
SMALL = (("norm1_g", 1024), ("mem_norm_g", 1024), ("norm2_g", 1024), ("gdn_norm_g", 128), ("sb_q_norm_g", 128),
         ("sb_k_norm_g", 128), ("mem_q_norm_g", 128), ("mem_k_norm_g", 128), ("a_log", 4), ("dt_bias", 4))
SMALL_ROWS = 32
CONV_ROWS = 48
SPACK_ROWS = SMALL_ROWS + CONV_ROWS
APACK_ROWS = SMALL_ROWS + 16

WEIGHTS = ("norm1_g", "w_in", "conv_w", "a_log", "dt_bias", "gdn_norm_g", "sb_q_norm_g", "sb_k_norm_g",
           "mem_norm_g", "w_mem_kv", "mem_q_norm_g", "mem_k_norm_g", "w_br_gdn", "w_br_sb", "w_br_mem", "w_o",
           "norm2_g", "w_up", "w_down")


def _cp(*sem):
    return pltpu.CompilerParams(dimension_semantics=sem if sem else None, vmem_limit_bytes=VMEM_LIMIT)


HBM = pl.BlockSpec(memory_space=pl.ANY)

_NN = (((1,), (0,)), ((), ()))
_NT = (((1,), (1,)), ((), ()))
_TN = (((0,), (0,)), ((), ()))


def _dot(a, b, dims=_NN):
    return lax.dot_general(a.astype(BF16), b.astype(BF16), dims, preferred_element_type=F32)


def _dot_nt(a, b):
    return _dot(a, b, _NT)


def _dot_tn(a, b):
    return _dot(a, b, _TN)


def _dotf(a, b, dims=_NN):
    return lax.dot_general(a, b, dims, precision=lax.Precision.HIGHEST, preferred_element_type=F32)


def _sigmoid(v):
    return 0.5 * jnp.tanh(0.5 * v) + 0.5


def _softplus(v):
    return jnp.maximum(v, 0.0) + jnp.log(1.0 + jnp.exp(-jnp.abs(v)))


def _iota(shape, dim):
    return lax.broadcasted_iota(jnp.int32, shape, dim)


def _hs(h):
    return slice(h * D_HEAD, (h + 1) * D_HEAD)


def _rowspec(tm, w, col=0):
    return pl.BlockSpec((tm, w), lambda i: (i, col))


def _full(shape):
    return pl.BlockSpec(shape, lambda *_: (0,) * len(shape))


def _accum(ref, first, val):
    @pl.when(first)
    def _():
        ref[...] = val

    @pl.when(jnp.logical_not(first))
    def _():
        ref[...] += val


class _Hosted:
    def __init__(self, ins, out_shapes, n_sems, start, finish, mid=None, scratch=()):
        self.ins, self.out_shapes, self.n_sems = list(ins), list(out_shapes), n_sems
        self.start, self.mid, self.finish, self.scratch = start, mid, finish, list(scratch)


def _mm(a, b, mode, out_dtype, name, *, tm=None, tn=None, tk=None, a_fn=None, extra=None, epi=None, comm=None):
    if mode == "tn":
        (K, M), N = a.shape, b.shape[1]
    else:
        (M, K), N = a.shape, (b.shape[0] if mode == "nt" else b.shape[1])
    tm = min(tm or (1024 if mode == "tn" else MM_TM), M)
    tn = min(tn or 1024, N)
    tk = min(tk or (TK_TOK if mode == "tn" else 1024), K)
    nm, nn, nk = M // tm, N // tn, K // tk
    assert nm * tm == M and nn * tn == N and nk * tk == K, (name, a.shape, b.shape)
    if mode == "tn":
        a_spec = pl.BlockSpec((tk, tm), lambda i, j, k: (k, i))
    else:
        a_spec = pl.BlockSpec((tm, tk), lambda i, j, k: (i, k))
    if mode == "nt":
        b_spec = pl.BlockSpec((tn, tk), lambda i, j, k: (j, k))
    else:
        b_spec = pl.BlockSpec((tk, tn), lambda i, j, k: (k, j))
    dims = {"nn": _NN, "nt": _NT, "tn": _TN}[mode]
    o_spec = pl.BlockSpec((tm, tn), lambda i, j, k: (i, j))
    has_extra = extra is not None

    n_ci, n_co = (len(comm.ins), len(comm.out_shapes)) if comm else (0, 0)
    n_in = 2 + has_extra + n_ci
    steps = nm * nn * nk

    def body(*refs):
        a_ref, b_ref = refs[0], refs[1]
        e_ref = refs[2] if has_extra else None
        o_ref = refs[n_in]
        scratch = refs[n_in + 1 + n_co:]
        if comm:
            step = (pl.program_id(0) * nn + pl.program_id(1)) * nk + pl.program_id(2)
            cargs = (refs[2 + has_extra:n_in], refs[n_in + 1:n_in + 1 + n_co], scratch[nk > 1], scratch[(nk > 1) + 1],
                     scratch[(nk > 1) + 2:])
            pl.when(step == 0)(lambda: comm.start(*cargs))
            if comm.mid is not None:
                pl.when(step == (steps * 7) // 8)(lambda: comm.mid(*cargs))
        av = a_ref[...]
        if a_fn is not None:
            av = a_fn(av)
        p = lax.dot_general(av, b_ref[...], dims, preferred_element_type=F32)

        def finish(acc):
            if epi is not None:
                acc = epi(acc, e_ref[...] if has_extra else None)
            o_ref[...] = acc.astype(out_dtype)

        if nk == 1:
            finish(p)
        else:
            acc_ref = scratch[0]
            k = pl.program_id(2)
            _accum(acc_ref, k == 0, p)

            @pl.when(k == nk - 1)
            def _():
                finish(acc_ref[...])

        if comm:
            pl.when(step == steps - 1)(lambda: comm.finish(*cargs))

    ins = [a, b] + ([extra] if has_extra else [])
    in_specs = [a_spec, b_spec] + ([o_spec] if has_extra else [])
    scratch_shapes = [pltpu.VMEM((tm, tn), F32)] if nk > 1 else []
    main = jax.ShapeDtypeStruct((M, N), out_dtype)
    if not comm:
        return pl.pallas_call(
            body, out_shape=main, grid=(nm, nn, nk), in_specs=in_specs, out_specs=o_spec,
            scratch_shapes=scratch_shapes, name=name, compiler_params=_cp("parallel", "parallel", "arbitrary"))(*ins)
    sems = [pltpu.SemaphoreType.DMA((comm.n_sems,)), pltpu.SemaphoreType.DMA((comm.n_sems,))]
    res = pl.pallas_call(
        body, out_shape=(main, *comm.out_shapes), grid=(nm, nn, nk), in_specs=in_specs + [HBM] * n_ci,
        out_specs=(o_spec, *[HBM] * n_co), scratch_shapes=scratch_shapes + sems + comm.scratch, name=name,
        compiler_params=_cp("arbitrary", "arbitrary", "arbitrary"))(*ins, *comm.ins)
    return res[0], list(res[1:])


def _relu2(u):
    r = jnp.maximum(u.astype(F32), 0.0)
    return (r * r).astype(BF16)


def _epi_add(acc, e):
    return acc + e.astype(F32)


def _epi_drelu2(acc, u):
    return acc * (2.0 * jnp.maximum(u.astype(F32), 0.0))


def _rms_fwd(x, g, name):
    T, dm = x.shape
    tm = min(TM, T)

    def body(x_ref, g_ref, h_ref):
        xv = x_ref[...]
        r = lax.rsqrt(jnp.mean(xv * xv, axis=-1, keepdims=True) + EPS)
        h_ref[...] = (xv * r * g_ref[...]).astype(BF16)

    return pl.pallas_call(
        body, out_shape=jax.ShapeDtypeStruct((T, dm), BF16), grid=(T // tm,),
        in_specs=[_rowspec(tm, dm), _full((1, dm))], out_specs=_rowspec(tm, dm), name=name,
        compiler_params=_cp("parallel"))(x, g)


def _rms_bwd(dh, x, g, resid, name):
    T, dm = x.shape
    tm = min(TM, T)

    def body(dh_ref, x_ref, g_ref, res_ref, dx_ref, dxb_ref, dg_ref):
        i = pl.program_id(0)
        xv = x_ref[...]
        r = lax.rsqrt(jnp.mean(xv * xv, axis=-1, keepdims=True) + EPS)
        y = xv * r
        dhv = dh_ref[...].astype(F32)
        dy = dhv * g_ref[...]
        dx = res_ref[...] + r * (dy - y * jnp.mean(dy * y, axis=-1, keepdims=True))
        dx_ref[...] = dx
        dxb_ref[...] = dx.astype(BF16)
        _accum(dg_ref, i == 0, jnp.sum(dhv * y, axis=0, keepdims=True))

    return pl.pallas_call(
        body,
        out_shape=(jax.ShapeDtypeStruct((T, dm), F32), jax.ShapeDtypeStruct((T, dm), BF16),
                   jax.ShapeDtypeStruct((1, dm), F32)),
        grid=(T // tm,),
        in_specs=[_rowspec(tm, dm), _rowspec(tm, dm), _full((1, dm)), _rowspec(tm, dm)],
        out_specs=(_rowspec(tm, dm), _rowspec(tm, dm), _full((1, dm))), name=name,
        compiler_params=_cp("arbitrary"))(dh, x, g, resid)


def _conv_tile(x_ref, halo_ref, w_ref, xpad, tm):
    i = pl.program_id(0)
    halo = halo_ref[...].astype(F32)[8:16]
    xpad[0:8, :] = jnp.where(i > 0, halo, 0.0)
    xpad[8:, :] = x_ref[...].astype(F32)
    w = w_ref[...]
    xc = w[0:1] * xpad[5:5 + tm, :]
    for j in range(1, CONV_TAPS):
        xc = xc + w[j:j + 1] * xpad[5 + j:5 + j + tm, :]
    return xc


def _gate_terms(ab_ref, av_ref):
    abv = ab_ref[...]
    av = av_ref[...]
    pre = abv + av[1:2]
    ea = jnp.exp(av[0:1])
    g = -ea * _softplus(pre)
    return abv, pre, ea, g


def _gdn_pre(proj, conv_w, ab, avec):
    T = proj.shape[0]
    tm = min(G1_TM, T)
    cw = 3 * BR_W

    def body(x_ref, halo_ref, w_ref, ab_ref, av_ref, q_ref, k_ref, v_ref, gb_ref, xpad):
        xc = _conv_tile(x_ref, halo_ref, w_ref, xpad, tm)
        y = xc * _sigmoid(xc)
        for h in range(N_HEAD):
            for off, ref, scale in ((0, q_ref, D_HEAD ** -0.5), (BR_W, k_ref, 1.0)):
                yh = y[:, off + h * D_HEAD:off + (h + 1) * D_HEAD]
                r = lax.rsqrt(jnp.sum(yh * yh, axis=-1, keepdims=True) + EPS)
                ref[:, _hs(h)] = yh * (r * scale)
        v_ref[...] = y[:, 2 * BR_W:]
        abv, _, _, g = _gate_terms(ab_ref, av_ref)
        lane = _iota((tm, LANES), 1)
        gb_ref[...] = jnp.where(lane < N_HEAD, g, jnp.where(lane < 2 * N_HEAD, _sigmoid(abv), 0.0))

    hb = tm // 16
    return pl.pallas_call(
        body,
        out_shape=(jax.ShapeDtypeStruct((T, BR_W), F32),) * 3 + (jax.ShapeDtypeStruct((T, LANES), F32),),
        grid=(T // tm,),
        in_specs=[_rowspec(tm, cw), pl.BlockSpec((16, cw), lambda i: (jnp.maximum(i * hb - 1, 0), 0)),
                  _full((CONV_TAPS, cw)), _rowspec(tm, LANES), _full((2, LANES))],
        out_specs=(_rowspec(tm, BR_W),) * 3 + (_rowspec(tm, LANES),),
        scratch_shapes=[pltpu.VMEM((tm + 8, cw), F32)], name="gdn_pre",
        compiler_params=_cp("parallel"))(proj, proj, conv_w, ab, avec)


def _gdn_pre_bwd(proj, conv_w, ab, avec, dq, dk, dv, dgb):
    T = proj.shape[0]
    tm = min(G1_TM, T)
    cw = 3 * BR_W

    def body(x_ref, halo_ref, w_ref, ab_ref, av_ref, dq_ref, dk_ref, dv_ref, dgb_ref,
             dxc_ref, dab_ref, dcw_ref, dav_ref, xpad):
        i = pl.program_id(0)

        @pl.when(i == 0)
        def _():
            dcw_ref[...] = jnp.zeros_like(dcw_ref)
            dav_ref[...] = jnp.zeros_like(dav_ref)

        xc_all = _conv_tile(x_ref, halo_ref, w_ref, xpad, tm)
        for s in range(cw // D_HEAD):
            cs = slice(s * D_HEAD, (s + 1) * D_HEAD)
            xc = xc_all[:, cs]
            sg = _sigmoid(xc)
            yh = xc * sg
            h = s % N_HEAD
            if s < 2 * N_HEAD:
                dref, scale = (dq_ref, D_HEAD ** -0.5) if s < N_HEAD else (dk_ref, 1.0)
                r = lax.rsqrt(jnp.sum(yh * yh, axis=-1, keepdims=True) + EPS)
                yn = yh * r
                dn = dref[:, _hs(h)]
                dy = (scale * r) * (dn - yn * jnp.sum(yn * dn, axis=-1, keepdims=True))
            else:
                dy = dv_ref[:, _hs(h)]
            dxc = dy * (sg * (1.0 + xc * (1.0 - sg)))
            dxc_ref[:, cs] = dxc.astype(BF16)
            for j in range(CONV_TAPS):
                dcw_ref[j:j + 1, cs] += jnp.sum(dxc * xpad[5 + j:5 + j + tm, cs], axis=0, keepdims=True)

        abv, pre, ea, g = _gate_terms(ab_ref, av_ref)
        dgbv = dgb_ref[...]
        lane = _iota((tm, LANES), 1)
        is_a = lane < N_HEAD
        da = jnp.where(is_a, dgbv * (-ea) * _sigmoid(pre), 0.0)
        bs = _sigmoid(abv)
        db = jnp.where(jnp.logical_and(lane >= N_HEAD, lane < 2 * N_HEAD), dgbv * bs * (1.0 - bs), 0.0)
        dab_ref[...] = (da + db).astype(BF16)
        dav_ref[0:1, :] += jnp.sum(jnp.where(is_a, dgbv * g, 0.0), axis=0, keepdims=True)
        dav_ref[1:2, :] += jnp.sum(da, axis=0, keepdims=True)

    hb = tm // 16
    return pl.pallas_call(
        body,
        out_shape=(jax.ShapeDtypeStruct((T, cw), BF16), jax.ShapeDtypeStruct((T, LANES), BF16),
                   jax.ShapeDtypeStruct((CONV_TAPS, cw), F32), jax.ShapeDtypeStruct((2, LANES), F32)),
        grid=(T // tm,),
        in_specs=[_rowspec(tm, cw), pl.BlockSpec((16, cw), lambda i: (jnp.maximum(i * hb - 1, 0), 0)),
                  _full((CONV_TAPS, cw)), _rowspec(tm, LANES), _full((2, LANES)),
                  _rowspec(tm, BR_W), _rowspec(tm, BR_W), _rowspec(tm, BR_W), _rowspec(tm, LANES)],
        out_specs=(_rowspec(tm, cw), _rowspec(tm, LANES), _full((CONV_TAPS, cw)), _full((2, LANES))),
        scratch_shapes=[pltpu.VMEM((tm + 8, cw), F32)], name="gdn_pre_bwd",
        compiler_params=_cp("arbitrary"))(proj, proj, conv_w, ab, avec, dq, dk, dv, dgb)


def _conv_bwd(dxc, conv_w):
    T, cw = dxc.shape
    tm = min(G1_TM, T)
    nt = T // tm
    hb = tm // 16

    def body(d_ref, halo_ref, w_ref, dx_ref, xpad):
        i = pl.program_id(0)
        xpad[0:tm, :] = d_ref[...].astype(F32)
        xpad[tm:, :] = jnp.where(i < nt - 1, halo_ref[...].astype(F32)[0:8], 0.0)
        w = w_ref[...]
        dx = w[3:4] * xpad[0:tm, :]
        for j in range(CONV_TAPS - 1):
            dx = dx + w[j:j + 1] * xpad[3 - j:3 - j + tm, :]
        dx_ref[...] = dx.astype(BF16)

    return pl.pallas_call(
        body, out_shape=jax.ShapeDtypeStruct((T, cw), BF16), grid=(nt,),
        in_specs=[_rowspec(tm, cw), pl.BlockSpec((16, cw), lambda i: (jnp.minimum((i + 1) * hb, T // 16 - 1), 0)),
                  _full((CONV_TAPS, cw))],
        out_specs=_rowspec(tm, cw), scratch_shapes=[pltpu.VMEM((tm + 8, cw), F32)], name="conv_bwd",
        compiler_params=_cp("parallel"))(dxc, dxc, conv_w)


def _chunk_consts():
    C = GDN_CHUNK
    row, col = _iota((C, C), 0), _iota((C, C), 1)
    return row, col, row >= col, row > col


def _chunk_decay(gbv, incl):
    c_all = _dotf(incl.astype(F32), gbv)
    c_t = jnp.concatenate([c_all, jnp.zeros_like(c_all)], axis=0).T[:, :GDN_CHUNK]
    return c_all, c_t


def _head_decay(c_all, c_t, gbv, incl, h):
    C = GDN_CHUNK
    c_col = c_all[:, h:h + 1]
    c_row = c_t[h:h + 1, :]
    gam = jnp.exp(jnp.where(incl, c_col - c_row, -1e30))
    c_last = c_all[C - 1:C, h:h + 1]
    return gam, jnp.exp(c_col), jnp.exp(c_last - c_col), jnp.exp(c_last), gbv[:, N_HEAD + h:N_HEAD + h + 1]


def _split_bf16(x):
    hi = x.astype(BF16)
    return hi, (x - hi.astype(F32)).astype(BF16)


def _dot3(a, b):
    ah, al = _split_bf16(a)
    bh, bl = _split_bf16(b)
    d = lambda u, v: lax.dot_general(u, v, _NN, preferred_element_type=F32)
    return d(ah, bh) + (d(ah, bl) + d(al, bh))


def _unit_lower_inverses(ms, row, col):
    bi, bj = row // INV_BLOCK, col // INV_BLOCK
    eye = (row == col).astype(F32)
    ns = [jnp.where(bi == bj, -m, 0.0) for m in ms]
    invs = [eye + n for n in ns]
    size = 2
    while size < INV_BLOCK:
        ns = [_dot3(n, n) for n in ns]
        invs = [inv + _dot3(inv, n) for inv, n in zip(invs, ns)]
        size *= 2
    width = 2
    while width * INV_BLOCK <= GDN_CHUNK:
        sel = jnp.logical_and(bi // width == bj // width, bi // (width // 2) > bj // (width // 2))
        ts = [_dot3(inv, jnp.where(sel, m, 0.0)) for inv, m in zip(invs, ms)]
        invs = [inv - _dot3(t, inv) for inv, t in zip(invs, ts)]
        width *= 2
    return invs


def _gdn_inv(k, gb):
    T = k.shape[0]
    C = GDN_CHUNK
    per = min(INV_CHUNKS, T // C)
    rows = per * C

    def body(k_ref, gb_ref, ti_ref):
        row, col, incl, strict = _chunk_consts()
        ms = []
        for ci in range(per):
            rs = slice(ci * C, (ci + 1) * C)
            gbv = gb_ref[rs, :]
            c_all, c_t = _chunk_decay(gbv, incl)
            for h in range(N_HEAD):
                gam, _, _, _, bcol = _head_decay(c_all, c_t, gbv, incl, h)
                K = k_ref[rs, _hs(h)]
                ms.append(jnp.where(strict, _dot_nt(K * bcol, K) * gam, 0.0))
        for i, inv in enumerate(_unit_lower_inverses(ms, row, col)):
            ti_ref[i // N_HEAD, i % N_HEAD] = inv

    return pl.pallas_call(
        body, out_shape=jax.ShapeDtypeStruct((T // C, N_HEAD, C, C), F32), grid=(T // rows,),
        in_specs=[_rowspec(rows, BR_W), _rowspec(rows, LANES)],
        out_specs=pl.BlockSpec((per, N_HEAD, C, C), lambda i: (i, 0, 0, 0)), name="gdn_inv",
        compiler_params=_cp("parallel"))(k, gb)


def _gdn_fwd(q, k, v, gb, proj, gnorm, tinv_all):
    T = q.shape[0]
    C = GDN_CHUNK
    nc = T // C
    per = min(GDN_STEP_CHUNKS, nc)
    zcol = Z_OFF // BR_W
    heads = range(N_HEAD)

    def body(q_ref, k_ref, v_ref, gb_ref, z_ref, gn_ref, ti_ref, og_ref, oraw_ref, sh_ref, vn_ref, s_ref):
        @pl.when(pl.program_id(0) == 0)
        def _():
            s_ref[...] = jnp.zeros_like(s_ref)

        _, _, incl, _ = _chunk_consts()
        S = [s_ref[h] for h in heads]
        for ci in range(per):
            rs = slice(ci * C, (ci + 1) * C)
            gbv = gb_ref[rs, :]
            c_all, c_t = _chunk_decay(gbv, incl)
            dec = [_head_decay(c_all, c_t, gbv, incl, h) for h in heads]
            gam, gcol, dcol, glast, bcol = ([d[i] for d in dec] for i in range(5))
            Q = [q_ref[rs, _hs(h)] for h in heads]
            K = [k_ref[rs, _hs(h)] for h in heads]
            V = [v_ref[rs, _hs(h)] for h in heads]
            Sb = [s.astype(BF16) for s in S]
            KS = [_dot(K[h], Sb[h]) for h in heads]
            QS = [_dot(Q[h], Sb[h]) for h in heads]
            P = [_dot_nt(Q[h], K[h]) * gam[h] for h in heads]
            R = [bcol[h] * (V[h] - gcol[h] * KS[h]) for h in heads]
            vn = [_dot(ti_ref[ci, h], R[h]) for h in heads]
            O = [gcol[h] * QS[h] + _dot(P[h], vn[h]) for h in heads]
            Sn = [glast[h] * S[h] + _dot_tn(K[h] * dcol[h], vn[h]) for h in heads]
            for h in heads:
                sh_ref[ci, h] = S[h]
                vn_ref[rs, _hs(h)] = vn[h]
                oraw_ref[rs, _hs(h)] = O[h]
                rr = lax.rsqrt(jnp.mean(O[h] * O[h], axis=-1, keepdims=True) + EPS)
                zz = z_ref[rs, _hs(h)].astype(F32)
                og_ref[rs, _hs(h)] = (O[h] * rr * gn_ref[...] * (zz * _sigmoid(zz))).astype(BF16)
            S = Sn
        for h in heads:
            s_ref[h] = S[h]

    cspec = lambda w, cb=0: pl.BlockSpec((per * C, w), lambda n: (n, cb))
    hist = lambda a, b: pl.BlockSpec((per, N_HEAD, a, b), lambda n: (n, 0, 0, 0))
    return pl.pallas_call(
        body,
        out_shape=(jax.ShapeDtypeStruct((T, BR_W), BF16), jax.ShapeDtypeStruct((T, BR_W), F32),
                   jax.ShapeDtypeStruct((nc, N_HEAD, D_HEAD, D_HEAD), F32), jax.ShapeDtypeStruct((T, BR_W), F32)),
        grid=(nc // per,),
        in_specs=[cspec(BR_W), cspec(BR_W), cspec(BR_W), cspec(LANES), cspec(BR_W, zcol), _full((1, D_HEAD)),
                  hist(C, C)],
        out_specs=(cspec(BR_W), cspec(BR_W), hist(D_HEAD, D_HEAD), cspec(BR_W)),
        scratch_shapes=[pltpu.VMEM((N_HEAD, D_HEAD, D_HEAD), F32)], name="gdn_chunk_fwd",
        compiler_params=_cp("arbitrary"))(q, k, v, gb, proj, gnorm, tinv_all)


def _gdn_bwd(q, k, v, gb, proj, gnorm, oraw, shist, tinv_all, vn_all, dog):
    T = q.shape[0]
    C = GDN_CHUNK
    nc = T // C
    per = min(GDN_BWD_STEP_CHUNKS, nc)
    zcol = Z_OFF // BR_W

    def body(q_ref, k_ref, v_ref, gb_ref, z_ref, gn_ref, oraw_ref, sh_ref, ti_ref, vn_ref, dog_ref,
             dq_ref, dk_ref, dv_ref, dgb_ref, dz_ref, dgn_ref, ds_ref):
        @pl.when(pl.program_id(0) == 0)
        def _():
            ds_ref[...] = jnp.zeros_like(ds_ref)
            dgn_ref[...] = jnp.zeros_like(dgn_ref)

        row, col, incl, strict = _chunk_consts()
        lane = _iota((C, LANES), 1)
        rowl = _iota((C, LANES), 0)
        ones = jnp.ones((C, LANES), F32)
        upper = (col >= row).astype(F32)
        gn = gn_ref[...]
        heads = range(N_HEAD)
        rsum = lambda a: jnp.sum(a, axis=-1, keepdims=True)
        dgn = jnp.zeros((1, D_HEAD), F32)
        dSn = [ds_ref[h] for h in heads]
        for ci in reversed(range(per)):
            rs = slice(ci * C, (ci + 1) * C)
            gbv = gb_ref[rs, :]
            c_all, c_t = _chunk_decay(gbv, incl)
            dec = [_head_decay(c_all, c_t, gbv, incl, h) for h in heads]
            gam, gcol, dcol, glast, bcol = ([d[i] for d in dec] for i in range(5))
            Q = [q_ref[rs, _hs(h)] for h in heads]
            K = [k_ref[rs, _hs(h)] for h in heads]
            V = [v_ref[rs, _hs(h)] for h in heads]
            dO = []
            for h in heads:
                O = oraw_ref[rs, _hs(h)]
                zz = z_ref[rs, _hs(h)].astype(F32)
                dogv = dog_ref[rs, _hs(h)].astype(F32)
                rr = lax.rsqrt(jnp.mean(O * O, axis=-1, keepdims=True) + EPS)
                on = O * rr
                sg = _sigmoid(zz)
                dz_ref[rs, _hs(h)] = (dogv * on * gn * (sg * (1.0 + zz * (1.0 - sg)))).astype(BF16)
                dyn = dogv * (zz * sg)
                dgn = dgn + jnp.sum(dyn * on, axis=0, keepdims=True)
                dyv = dyn * gn
                dO.append((rr * (dyv - on * jnp.mean(dyv * on, axis=-1, keepdims=True))).astype(BF16))
            S = [sh_ref[ci, h] for h in heads]
            Sb = [s.astype(BF16) for s in S]
            tinv = [ti_ref[ci, h].astype(BF16) for h in heads]
            vn = [vn_ref[rs, _hs(h)] for h in heads]
            vnb = [a.astype(BF16) for a in vn]
            dSb = [a.astype(BF16) for a in dSn]
            Kb = [K[h] * bcol[h] for h in heads]
            M = [jnp.where(strict, _dot_nt(Kb[h], K[h]) * gam[h], 0.0) for h in heads]
            P = [_dot_nt(Q[h], K[h]) * gam[h] for h in heads]
            KS = [_dot(K[h], Sb[h]) for h in heads]
            QS = [_dot(Q[h], Sb[h]) for h in heads]
            dvn = [_dot_tn(P[h], dO[h]) + _dot(K[h] * dcol[h], dSb[h]) for h in heads]
            dR = [_dot_tn(tinv[h], dvn[h]) for h in heads]
            dRb = [a.astype(BF16) for a in dR]
            bg = [bcol[h] * gcol[h] for h in heads]
            dS_new = [glast[h] * dSn[h] + _dot_tn(gcol[h] * Q[h], dO[h]) - _dot_tn(bg[h] * K[h], dRb[h])
                      for h in heads]
            dP = [jnp.where(incl, _dot_nt(dO[h], vnb[h]), 0.0) for h in heads]
            dM = [jnp.where(strict, -_dot_nt(dRb[h], vnb[h]), 0.0) for h in heads]
            dPG = [(dP[h] * gam[h]).astype(BF16) for h in heads]
            dMG = [(dM[h] * gam[h]).astype(BF16) for h in heads]
            E = [_dot_nt(vnb[h], dSb[h]) for h in heads]
            dKb = [_dot(dMG[h], K[h]) for h in heads]
            dc_all = jnp.zeros((C, LANES), F32)
            db_all = jnp.zeros((C, LANES), F32)
            for h in heads:
                dq_ref[rs, _hs(h)] = gcol[h] * _dot_nt(dO[h], Sb[h]) + _dot(dPG[h], K[h])
                dk_ref[rs, _hs(h)] = (_dot_tn(dPG[h], Q[h]) + _dot_tn(dMG[h], Kb[h]) + bcol[h] * dKb[h]
                                      - bg[h] * _dot_nt(dRb[h], Sb[h]) + dcol[h] * E[h])
                dv_ref[rs, _hs(h)] = bcol[h] * dR[h]
                dbeta = rsum(dKb[h] * K[h]) + rsum(dR[h] * (V[h] - gcol[h] * KS[h]))
                X = dP[h] * P[h] + dM[h] * M[h]
                ddel = rsum(K[h] * E[h]) * dcol[h]
                colsum = _dotf(X, ones, _TN)[:, 0:1]
                dc = (rsum(X) - colsum + gcol[h] * rsum(dO[h].astype(F32) * QS[h]) - bg[h] * rsum(dR[h] * KS[h])
                      - ddel)
                last = (jnp.sum(ddel, axis=0, keepdims=True)
                        + glast[h] * jnp.sum(rsum(dSn[h] * S[h]), axis=0, keepdims=True))
                dc_all = dc_all + jnp.where(lane == h, dc + jnp.where(rowl == C - 1, last, 0.0), 0.0)
                db_all = db_all + jnp.where(lane == N_HEAD + h, dbeta, 0.0)
            dgb_ref[rs, :] = _dotf(upper, dc_all) + db_all
            dSn = dS_new
        for h in heads:
            ds_ref[h] = dSn[h]
        dgn_ref[...] += dgn

    nb = nc // per
    cspec = lambda w, cb=0: pl.BlockSpec((per * C, w), lambda n: (nb - 1 - n, cb))
    hist = lambda a, b: pl.BlockSpec((per, N_HEAD, a, b), lambda n: (nb - 1 - n, 0, 0, 0))
    return pl.pallas_call(
        body,
        out_shape=(jax.ShapeDtypeStruct((T, BR_W), F32),) * 3 + (
            jax.ShapeDtypeStruct((T, LANES), F32), jax.ShapeDtypeStruct((T, BR_W), BF16),
            jax.ShapeDtypeStruct((1, D_HEAD), F32)),
        grid=(nb,),
        in_specs=[cspec(BR_W), cspec(BR_W), cspec(BR_W), cspec(LANES), cspec(BR_W, zcol), _full((1, D_HEAD)),
                  cspec(BR_W), hist(D_HEAD, D_HEAD), hist(C, C), cspec(BR_W), cspec(BR_W)],
        out_specs=(cspec(BR_W), cspec(BR_W), cspec(BR_W), cspec(LANES), cspec(BR_W), _full((1, D_HEAD))),
        scratch_shapes=[pltpu.VMEM((N_HEAD, D_HEAD, D_HEAD), F32)], name="gdn_chunk_bwd",
        compiler_params=_cp("arbitrary"))(q, k, v, gb, proj, gnorm, oraw, shist, tinv_all, vn_all, dog)


SB_COL = SB_OFF // BR_W
SB_SCALE = D_HEAD ** -0.5


def _sb_pre(proj, gq, gk):
    T = proj.shape[0]
    tm = min(TM, T)

    def body(xq_ref, xk_ref, xv_ref, gq_ref, gk_ref, q_ref, k_ref, v_ref):
        for h in range(N_HEAD):
            for x_ref, g_ref, ref, scale in ((xq_ref, gq_ref, q_ref, SB_SCALE), (xk_ref, gk_ref, k_ref, 1.0)):
                xh = x_ref[:, _hs(h)].astype(F32)
                r = lax.rsqrt(jnp.mean(xh * xh, axis=-1, keepdims=True) + EPS)
                ref[:, _hs(h)] = (xh * (r * scale) * g_ref[...]).astype(BF16)
        v_ref[...] = xv_ref[...]

    return pl.pallas_call(
        body, out_shape=(jax.ShapeDtypeStruct((T, BR_W), BF16),) * 3, grid=(T // tm,),
        in_specs=[_rowspec(tm, BR_W, SB_COL), _rowspec(tm, BR_W, SB_COL + 1), _rowspec(tm, BR_W, SB_COL + 2),
                  _full((1, D_HEAD)), _full((1, D_HEAD))],
        out_specs=(_rowspec(tm, BR_W),) * 3, name="sb_pre", compiler_params=_cp("parallel"))(proj, proj, proj, gq, gk)


def _sb_pre_bwd(proj, gq, gk, dq, dk, dv):
    T = proj.shape[0]
    tm = min(TM, T)

    def body(xq_ref, xk_ref, gq_ref, gk_ref, dq_ref, dk_ref, dv_ref, dx_ref, dgq_ref, dgk_ref):
        i = pl.program_id(0)

        @pl.when(i == 0)
        def _():
            dgq_ref[...] = jnp.zeros_like(dgq_ref)
            dgk_ref[...] = jnp.zeros_like(dgk_ref)

        for off, x_ref, g_ref, d_ref, dg_ref, scale in ((0, xq_ref, gq_ref, dq_ref, dgq_ref, SB_SCALE),
                                                        (BR_W, xk_ref, gk_ref, dk_ref, dgk_ref, 1.0)):
            dg = jnp.zeros((1, D_HEAD), F32)
            for h in range(N_HEAD):
                xh = x_ref[:, _hs(h)].astype(F32)
                r = lax.rsqrt(jnp.mean(xh * xh, axis=-1, keepdims=True) + EPS)
                y = xh * r
                dn = d_ref[:, _hs(h)] * scale
                dg = dg + jnp.sum(dn * y, axis=0, keepdims=True)
                dy = dn * g_ref[...]
                dx_ref[:, off + h * D_HEAD:off + (h + 1) * D_HEAD] = (
                    r * (dy - y * jnp.mean(dy * y, axis=-1, keepdims=True))).astype(BF16)
            dg_ref[...] += dg
        dx_ref[:, 2 * BR_W:] = dv_ref[...].astype(BF16)

    return pl.pallas_call(
        body,
        out_shape=(jax.ShapeDtypeStruct((T, 3 * BR_W), BF16), jax.ShapeDtypeStruct((1, D_HEAD), F32),
                   jax.ShapeDtypeStruct((1, D_HEAD), F32)),
        grid=(T // tm,),
        in_specs=[_rowspec(tm, BR_W, SB_COL), _rowspec(tm, BR_W, SB_COL + 1), _full((1, D_HEAD)), _full((1, D_HEAD)),
                  _rowspec(tm, BR_W), _rowspec(tm, BR_W), _rowspec(tm, BR_W)],
        out_specs=(_rowspec(tm, 3 * BR_W), _full((1, D_HEAD)), _full((1, D_HEAD))), name="sb_pre_bwd",
        compiler_params=_cp("arbitrary"))(proj, proj, gq, gk, dq, dk, dv)


def _sb_pair(q, k, masked):
    z = _dot_nt(q, k)
    zc = jnp.minimum(z, 30.0)
    sp = jnp.log(1.0 + jnp.exp(zc)) + (z - zc)
    if not masked:
        return z, sp, None
    mask = _iota(z.shape, 1) < _iota(z.shape, 0)
    return z, jnp.where(mask, sp, 0.0), mask


def _sb_fwd(sq, sk, sv):
    T = sq.shape[0]
    blk = min(SB_BLK, T)
    w = min(SB_W, blk)
    nb, nsub = T // blk, blk // w

    def body(q_ref, k_ref, v_ref, o_ref, lt_ref, cut_ref, acc_ref, r_ref):
        head, qi = pl.program_id(0), pl.program_id(1)
        acc_ref[...] = jnp.zeros_like(acc_ref)
        r_ref[...] = jnp.zeros_like(r_ref)

        def block(kj, masked):
            rows = pl.ds(pl.multiple_of(kj * blk, blk), blk)
            z, sp, mask = _sb_pair(q_ref[...], k_ref[rows, :], masked)
            after = (_iota((w, w), 0) > _iota((w, w), 1)).astype(BF16)
            r = r_ref[...]
            acc = acc_ref[...]
            for sb in reversed(range(nsub)):
                cs = slice(sb * w, (sb + 1) * w)
                sps = sp[:, cs]
                a = jnp.exp(z[:, cs] - sps - _dot(sps, after) - r)
                if masked:
                    a = jnp.where(mask[:, cs], a, 0.0)
                acc = acc + _dot(a, v_ref[pl.ds(pl.multiple_of(kj * blk + sb * w, w), w), :])
                r = r + jnp.sum(sps, axis=-1, keepdims=True)
            acc_ref[...] = acc
            r_ref[...] = r
            return jnp.min(r) < SB_DEAD

        def further(state):
            kj, _ = state
            return kj - 1, block(kj, False)

        left, _ = lax.while_loop(lambda s: jnp.logical_and(s[0] >= 0, s[1]), further, (qi - 1, block(qi, True)))
        o_ref[...] = acc_ref[...].astype(BF16)
        lt_ref[0] = r_ref[...]
        cut_ref[head, qi] = (left + 1).astype(F32)

    qspec = pl.BlockSpec((blk, D_HEAD), lambda h, i: (i, h))
    whole = pl.BlockSpec((T, D_HEAD), lambda h, i: (0, h))
    return pl.pallas_call(
        body,
        out_shape=(jax.ShapeDtypeStruct((T, BR_W), BF16), jax.ShapeDtypeStruct((N_HEAD, T, 1), F32),
                   jax.ShapeDtypeStruct((N_HEAD, nb), F32)),
        grid=(N_HEAD, nb), in_specs=[qspec, whole, whole],
        out_specs=(qspec, pl.BlockSpec((1, blk, 1), lambda h, i: (h, i, 0)), pl.BlockSpec(memory_space=pltpu.SMEM)),
        scratch_shapes=[pltpu.VMEM((blk, D_HEAD), F32), pltpu.VMEM((blk, 1), F32)],
        name="sb_fwd", compiler_params=_cp("arbitrary", "arbitrary"))(sq, sk, sv)


def _sb_bwd(sq, sk, sv, ltot, do, cut):
    T = sq.shape[0]
    blk = min(SB_BLK, T)
    w = min(SB_W, blk)
    nb, nsub = T // blk, blk // w

    def body(q_ref, k_ref, v_ref, lt_ref, do_ref, cut_ref, dq_ref, dk_ref, dv_ref, acc_ref, p_ref, g_ref):
        head, qi = pl.program_id(0), pl.program_id(1)
        first = cut_ref[head, qi].astype(jnp.int32)

        @pl.when(qi == 0)
        def _():
            dk_ref[...] = jnp.zeros_like(dk_ref)
            dv_ref[...] = jnp.zeros_like(dv_ref)

        acc_ref[...] = jnp.zeros_like(acc_ref)
        p_ref[...] = lt_ref[0]
        g_ref[...] = jnp.zeros_like(g_ref)

        def block(kj, masked):
            base = pl.multiple_of(kj * blk, blk)
            z, sp, mask = _sb_pair(q_ref[...], k_ref[pl.ds(base, blk), :], masked)
            d_a = _dot_nt(do_ref[...], v_ref[pl.ds(base, blk), :])
            after = (_iota((w, w), 0) > _iota((w, w), 1)).astype(BF16)
            before = (_iota((w, w), 0) < _iota((w, w), 1)).astype(BF16)
            rest = p_ref[...]
            hg = g_ref[...]
            acc = acc_ref[...]
            for sb in range(nsub):
                cs = slice(sb * w, (sb + 1) * w)
                sps, zs = sp[:, cs], z[:, cs]
                rest = rest - jnp.sum(sps, axis=-1, keepdims=True)
                a = jnp.exp(zs - sps - _dot(sps, after) - rest)
                if masked:
                    a = jnp.where(mask[:, cs], a, 0.0)
                g = a * d_a[:, cs]
                sig = jnp.exp(zs - sps)
                dz = g - sig * (g + (hg + _dot(g, before)))
                if masked:
                    dz = jnp.where(mask[:, cs], dz, 0.0)
                dz = dz.astype(BF16)
                rows = pl.ds(pl.multiple_of(base + sb * w, w), w)
                dv_ref[rows, :] += _dot_tn(a, do_ref[...])
                dk_ref[rows, :] += _dot_tn(dz, q_ref[...])
                acc = acc + _dot(dz, k_ref[rows, :])
                hg = hg + jnp.sum(g, axis=-1, keepdims=True)
            acc_ref[...] = acc
            p_ref[...] = rest
            g_ref[...] = hg

        def step(kj, carry):
            block(kj, False)
            return carry

        lax.fori_loop(first, qi, step, 0)
        block(qi, True)
        dq_ref[...] = acc_ref[...]

    qspec = pl.BlockSpec((blk, D_HEAD), lambda h, i: (i, h))
    whole = pl.BlockSpec((T, D_HEAD), lambda h, i: (0, h))
    return pl.pallas_call(
        body, out_shape=(jax.ShapeDtypeStruct((T, BR_W), F32),) * 3, grid=(N_HEAD, nb),
        in_specs=[qspec, whole, whole, pl.BlockSpec((1, blk, 1), lambda h, i: (h, i, 0)), qspec,
                  pl.BlockSpec(memory_space=pltpu.SMEM)],
        out_specs=(qspec, whole, whole),
        scratch_shapes=[pltpu.VMEM((blk, D_HEAD), F32), pltpu.VMEM((blk, 1), F32), pltpu.VMEM((blk, 1), F32)],
        name="sb_bwd", compiler_params=_cp("arbitrary", "arbitrary"))(sq, sk, sv, ltot, do, cut)


def _mem_kv(mem, gm, w_kv, gk):
    def body(mem_ref, gm_ref, w_ref, gk_ref, mn_ref, kv_ref, kh_ref, vm_ref):
        mv = mem_ref[...]
        r = lax.rsqrt(jnp.mean(mv * mv, axis=-1, keepdims=True) + EPS)
        mn = (mv * r * gm_ref[...]).astype(BF16)
        mn_ref[...] = mn
        kv = lax.dot_general(mn, w_ref[...], _NN, preferred_element_type=F32)
        kv_ref[...] = kv
        for h in range(N_HEAD):
            kh = kv[:, _hs(h)]
            rk = lax.rsqrt(jnp.mean(kh * kh, axis=-1, keepdims=True) + EPS)
            kh_ref[:, _hs(h)] = (kh * rk * gk_ref[...]).astype(BF16)
        vm_ref[...] = kv[:, BR_W:].astype(BF16)

    return pl.pallas_call(
        body,
        out_shape=(jax.ShapeDtypeStruct((N_MEM, D_MODEL), BF16), jax.ShapeDtypeStruct((N_MEM, 2 * BR_W), F32),
                   jax.ShapeDtypeStruct((N_MEM, BR_W), BF16), jax.ShapeDtypeStruct((N_MEM, BR_W), BF16)),
        name="mem_kv", compiler_params=_cp())(mem, gm, w_kv, gk)


def _mem_q(x_ref, gq_ref, h):
    xh = x_ref[:, _hs(h)].astype(F32)
    r = lax.rsqrt(jnp.mean(xh * xh, axis=-1, keepdims=True) + EPS)
    return r, xh * r


def _mem_probs(qn, kh):
    s = _dot_nt(qn, kh) * (D_HEAD ** -0.5)
    e = jnp.exp(s - jnp.max(s, axis=-1, keepdims=True))
    return e / jnp.sum(e, axis=-1, keepdims=True)


def _mem_fwd(proj, kh, vm, gq):
    T = proj.shape[0]
    tm = min(TM, T)

    def body(x_ref, kh_ref, vm_ref, gq_ref, o_ref):
        for h in range(N_HEAD):
            _, y = _mem_q(x_ref, gq_ref, h)
            p = _mem_probs((y * gq_ref[...]).astype(BF16), kh_ref[:, _hs(h)])
            o_ref[:, _hs(h)] = _dot(p, vm_ref[:, _hs(h)]).astype(BF16)

    return pl.pallas_call(
        body, out_shape=jax.ShapeDtypeStruct((T, BR_W), BF16), grid=(T // tm,),
        in_specs=[_rowspec(tm, BR_W, MEMQ_OFF // BR_W), _full((N_MEM, BR_W)), _full((N_MEM, BR_W)),
                  _full((1, D_HEAD))],
        out_specs=_rowspec(tm, BR_W), name="mem_fwd", compiler_params=_cp("parallel"))(proj, kh, vm, gq)


def _mem_bwd(proj, kh, vm, gq, do):
    T = proj.shape[0]
    tm = min(TM, T)

    def body(x_ref, kh_ref, vm_ref, gq_ref, do_ref, dx_ref, dkh_ref, dvm_ref, dgq_ref):
        i = pl.program_id(0)

        @pl.when(i == 0)
        def _():
            dkh_ref[...] = jnp.zeros_like(dkh_ref)
            dvm_ref[...] = jnp.zeros_like(dvm_ref)
            dgq_ref[...] = jnp.zeros_like(dgq_ref)

        dg = jnp.zeros((1, D_HEAD), F32)
        for h in range(N_HEAD):
            r, y = _mem_q(x_ref, gq_ref, h)
            qn = (y * gq_ref[...]).astype(BF16)
            p = _mem_probs(qn, kh_ref[:, _hs(h)])
            dov = do_ref[:, _hs(h)]
            dp = _dot_nt(dov, vm_ref[:, _hs(h)])
            ds = p * (dp - jnp.sum(dp * p, axis=-1, keepdims=True)) * (D_HEAD ** -0.5)
            dqn = _dot(ds, kh_ref[:, _hs(h)])
            dkh_ref[:, _hs(h)] += _dot_tn(ds, qn)
            dvm_ref[:, _hs(h)] += _dot_tn(p, dov)
            dg = dg + jnp.sum(dqn * y, axis=0, keepdims=True)
            dy = dqn * gq_ref[...]
            dx_ref[:, _hs(h)] = (r * (dy - y * jnp.mean(dy * y, axis=-1, keepdims=True))).astype(BF16)
        dgq_ref[...] += dg

    return pl.pallas_call(
        body,
        out_shape=(jax.ShapeDtypeStruct((T, BR_W), BF16), jax.ShapeDtypeStruct((N_MEM, BR_W), F32),
                   jax.ShapeDtypeStruct((N_MEM, BR_W), F32), jax.ShapeDtypeStruct((1, D_HEAD), F32)),
        grid=(T // tm,),
        in_specs=[_rowspec(tm, BR_W, MEMQ_OFF // BR_W), _full((N_MEM, BR_W)), _full((N_MEM, BR_W)),
                  _full((1, D_HEAD)), _rowspec(tm, BR_W)],
        out_specs=(_rowspec(tm, BR_W), _full((N_MEM, BR_W)), _full((N_MEM, BR_W)), _full((1, D_HEAD))),
        name="mem_bwd", compiler_params=_cp("arbitrary"))(proj, kh, vm, gq, do)


def _mem_kv_bwd(mem, gm, w_kv, gk, kv, mn, dkh, dvm):
    def body(mem_ref, gm_ref, w_ref, gk_ref, kv_ref, mn_ref, dkh_ref, dvm_ref, dw_ref, dgm_ref, dgk_ref, dkv_ref):
        dgk = jnp.zeros((1, D_HEAD), F32)
        for h in range(N_HEAD):
            kh = kv_ref[:, _hs(h)]
            r = lax.rsqrt(jnp.mean(kh * kh, axis=-1, keepdims=True) + EPS)
            y = kh * r
            dn = dkh_ref[:, _hs(h)]
            dgk = dgk + jnp.sum(dn * y, axis=0, keepdims=True)
            dy = dn * gk_ref[...]
            dkv_ref[:, _hs(h)] = (r * (dy - y * jnp.mean(dy * y, axis=-1, keepdims=True))).astype(BF16)
        dkv_ref[:, BR_W:] = dvm_ref[...].astype(BF16)
        dgk_ref[...] = dgk
        dkv = dkv_ref[...]
        dw_ref[...] = lax.dot_general(mn_ref[...], dkv, _TN, preferred_element_type=F32)
        dmn = lax.dot_general(dkv, w_ref[...], _NT, preferred_element_type=F32)
        mv = mem_ref[...]
        memn = mv * lax.rsqrt(jnp.mean(mv * mv, axis=-1, keepdims=True) + EPS)
        dgm_ref[...] = jnp.sum(dmn * memn, axis=0, keepdims=True)

    return pl.pallas_call(
        body,
        out_shape=(jax.ShapeDtypeStruct((D_MODEL, 2 * BR_W), F32), jax.ShapeDtypeStruct((1, D_MODEL), F32),
                   jax.ShapeDtypeStruct((1, D_HEAD), F32)),
        scratch_shapes=[pltpu.VMEM((N_MEM, 2 * BR_W), BF16)], name="mem_kv_bwd",
        compiler_params=_cp())(mem, gm, w_kv, gk, kv, mn, dkh, dvm)


def _merge_fwd(og, osb, om, proj, wg, ws, wm):
    T = og.shape[0]
    tm = min(TM, T)

    def body(og_ref, os_ref, om_ref, g0, g1, g2, wg_ref, ws_ref, wm_ref, mix_ref, yg_ref, ys_ref, ym_ref):
        mix = jnp.zeros((tm, D_MODEL), F32)
        for o_ref, gl_ref, w_ref, y_ref in ((og_ref, g0, wg_ref, yg_ref), (os_ref, g1, ws_ref, ys_ref),
                                            (om_ref, g2, wm_ref, ym_ref)):
            y = lax.dot_general(o_ref[...], w_ref[...], _NN, preferred_element_type=F32)
            y_ref[...] = y.astype(BF16)
            mix = mix + _sigmoid(gl_ref[...].astype(F32)) * y
        mix_ref[...] = mix.astype(BF16)

    br = _rowspec(tm, BR_W)
    wspec = _full((BR_W, D_MODEL))
    out = _rowspec(tm, D_MODEL)
    gates = [_rowspec(tm, D_MODEL, GATE_COL + b) for b in range(3)]
    return pl.pallas_call(
        body, out_shape=(jax.ShapeDtypeStruct((T, D_MODEL), BF16),) * 4, grid=(T // tm,),
        in_specs=[br, br, br, *gates, wspec, wspec, wspec],
        out_specs=(out,) * 4, name="merge_fwd",
        compiler_params=_cp("parallel"))(og, osb, om, proj, proj, proj, wg, ws, wm)


def _merge_bwd(dmix, proj, ys, os_, ws):
    T = dmix.shape[0]
    tm = min(TM, T)

    def body(dmix_ref, g0, g1, g2, y0, y1, y2, o0, o1, o2, w0, w1, w2, dgl_ref, do0, do1, do2, dw0, dw1, dw2):
        i = pl.program_id(0)
        dm = dmix_ref[...].astype(F32)
        for b, (gl_ref, y_ref, o_ref, w_ref, do_ref, dw_ref) in enumerate((
                (g0, y0, o0, w0, do0, dw0), (g1, y1, o1, w1, do1, dw1), (g2, y2, o2, w2, do2, dw2))):
            gate = _sigmoid(gl_ref[...].astype(F32))
            dgl_ref[:, b * D_MODEL:(b + 1) * D_MODEL] = (dm * y_ref[...].astype(F32) * gate * (1.0 - gate)).astype(BF16)
            dy = (gate * dm).astype(BF16)
            do_ref[...] = lax.dot_general(dy, w_ref[...], _NT, preferred_element_type=F32).astype(BF16)
            _accum(dw_ref, i == 0, lax.dot_general(dy, o_ref[...], _TN, preferred_element_type=F32))

    br = _rowspec(tm, BR_W)
    wide = _rowspec(tm, D_MODEL)
    wspec = _full((BR_W, D_MODEL))
    wtspec = _full((D_MODEL, BR_W))
    gates = [_rowspec(tm, D_MODEL, GATE_COL + b) for b in range(3)]
    return pl.pallas_call(
        body,
        out_shape=(jax.ShapeDtypeStruct((T, 3 * D_MODEL), BF16),) + (jax.ShapeDtypeStruct((T, BR_W), BF16),) * 3
        + (jax.ShapeDtypeStruct((D_MODEL, BR_W), F32),) * 3,
        grid=(T // tm,),
        in_specs=[wide, *gates, wide, wide, wide, br, br, br, wspec, wspec, wspec],
        out_specs=(_rowspec(tm, 3 * D_MODEL), br, br, br, wtspec, wtspec, wtspec), name="merge_bwd",
        compiler_params=_cp("arbitrary"))(dmix, proj, proj, proj, *ys, *os_, *ws)


def _loss(y, tgt):
    T, dm = y.shape
    tm = min(TM, T)

    def body(y_ref, t_ref, dy_ref, dyb_ref, sq_ref):
        err = y_ref[...] - t_ref[...]
        dy = err * (1.0 / dm)
        dy_ref[...] = dy
        dyb_ref[...] = dy.astype(BF16)
        _accum(sq_ref, pl.program_id(0) == 0, jnp.sum(err * err, axis=0, keepdims=True))

    return pl.pallas_call(
        body,
        out_shape=(jax.ShapeDtypeStruct((T, dm), F32), jax.ShapeDtypeStruct((T, dm), BF16),
                   jax.ShapeDtypeStruct((1, dm), F32)),
        grid=(T // tm,), in_specs=[_rowspec(tm, dm), _rowspec(tm, dm)],
        out_specs=(_rowspec(tm, dm), _rowspec(tm, dm), _full((1, dm))), name="loss",
        compiler_params=_cp("arbitrary"))(y, tgt)


def _split_w_in(slabs):
    width = slabs[0].shape[1]
    j, lo = SB_OFF // width, SB_OFF % width
    assert lo + 8 <= width
    parts = list(slabs[:j]) + [slabs[j][:, :lo], slabs[j][:, lo + 8:]] + list(slabs[j + 1:])
    return (jnp.concatenate(parts, axis=1), jnp.pad(slabs[j][:, lo:lo + 8], ((0, 0), (0, LANES - 8))))


def _local_step(x, mem, tgt, W, P, dist=None):
    w_main, w_ab = W["w_main"], W["w_ab"]
    avec = jnp.pad(jnp.concatenate([P["a_log"], P["dt_bias"]], axis=0), ((0, 0), (0, LANES - N_HEAD)))

    h = _rms_fwd(x, P["norm1_g"], "rms1")
    if dist is None:
        proj = _mm(h, w_main, "nn", BF16, "in_proj")
    else:
        proj, gathered = _mm(h, w_main, "nn", BF16, "in_proj", comm=dist.gather_rest())
        rest, conv_w = dist.weights_from(gathered)
        W, P = {**W, **rest}, {**P, "conv_w": conv_w}
    wbr = (W["w_br_gdn"], W["w_br_sb"], W["w_br_mem"])
    ab = _mm(h, w_ab, "nn", F32, "in_proj_ab")
    q, k, v, gb = _gdn_pre(proj, P["conv_w"], ab, avec)
    tinv = _gdn_inv(k, gb)
    og, oraw, shist, vn = _gdn_fwd(q, k, v, gb, proj, P["gdn_norm_g"], tinv)
    sq, sk, sv = _sb_pre(proj, P["sb_q_norm_g"], P["sb_k_norm_g"])
    osb, ltot, cut = _sb_fwd(sq, sk, sv)
    mn, kv, kh, vm = _mem_kv(mem, P["mem_norm_g"], W["w_mem_kv"], P["mem_k_norm_g"])
    om = _mem_fwd(proj, kh, vm, P["mem_q_norm_g"])
    mix, yg, ys, ym = _merge_fwd(og, osb, om, proj, *wbr)
    x1 = _mm(mix, W["w_o"], "nn", F32, "out_proj", extra=x, epi=_epi_add)
    h2 = _rms_fwd(x1, P["norm2_g"], "rms2")
    u = _mm(h2, W["w_up"], "nn", BF16, "mlp_up")
    y = _mm(u, W["w_down"], "nn", F32, "mlp_down", a_fn=_relu2, extra=x1, epi=_epi_add)
    dy, dyb, sq_err = _loss(y, tgt)

    G = {}
    du = _mm(dyb, W["w_down"], "nt", BF16, "d_mlp_act", extra=u, epi=_epi_drelu2)
    G["w_down"] = _mm(u, dyb, "tn", F32, "dw_down", a_fn=_relu2)
    G["w_up"] = _mm(du, h2, "tn", F32, "dw_up")
    dh2 = _mm(du, W["w_up"], "nt", F32, "d_h2")
    dx1, dx1b, G["norm2_g"] = _rms_bwd(dh2, x1, P["norm2_g"], dy, "rms2_bwd")
    dmix = _mm(dx1b, W["w_o"], "nt", BF16, "d_mix")
    G["w_o"] = _mm(mix, dx1b, "tn", F32, "dw_o")
    dgates, dog, dosb, dom, G["w_br_gdn"], G["w_br_sb"], G["w_br_mem"] = _merge_bwd(
        dmix, proj, (yg, ys, ym), (og, osb, om), wbr)
    dq, dk, dv, dgb, dz, G["gdn_norm_g"] = _gdn_bwd(q, k, v, gb, proj, P["gdn_norm_g"], oraw, shist, tinv, vn, dog)
    dxc, dab, G["conv_w"], dav = _gdn_pre_bwd(proj, P["conv_w"], ab, avec, dq, dk, dv, dgb)
    dqkv = _conv_bwd(dxc, P["conv_w"])
    G["a_log"], G["dt_bias"] = dav[0:1, :N_HEAD], dav[1:2, :N_HEAD]
    dsq, dsk, dsv = _sb_bwd(sq, sk, sv, ltot, dosb, cut)
    dsb, G["sb_q_norm_g"], G["sb_k_norm_g"] = _sb_pre_bwd(proj, P["sb_q_norm_g"], P["sb_k_norm_g"], dsq, dsk, dsv)
    dmemq, dkh, dvm, G["mem_q_norm_g"] = _mem_bwd(proj, kh, vm, P["mem_q_norm_g"], dom)
    G["w_mem_kv"], G["mem_norm_g"], G["mem_k_norm_g"] = _mem_kv_bwd(
        mem, P["mem_norm_g"], W["w_mem_kv"], P["mem_k_norm_g"], kv, mn, dkh, dvm)
    dproj = jnp.concatenate([dqkv, dz, dsb, dmemq, dgates], axis=1)
    dw_ab = _mm(dab, h, "tn", F32, "dw_in_ab")
    if dist is None:
        dw_main = _mm(dproj, h, "tn", F32, "dw_in")
    else:
        early = [n for n, _, _ in BIG if n != "w_in"]
        dw_main, landed = _mm(dproj, h, "tn", F32, "dw_in", comm=dist.scatter(G, early, "early"))
        dist.collect(early, landed)
    G["w_in"] = jnp.concatenate([dw_main[:SB_OFF], dw_ab[:8], dw_main[SB_OFF:]], axis=0)
    if dist is None:
        dh = _mm(dproj, w_main, "nt", F32, "d_h")
    else:
        dh, landed = _mm(dproj, w_main, "nt", F32, "d_h", comm=dist.scatter(G, ["w_in"], "late"))
        dist.collect(["w_in"], landed)
    dh = _mm(dab, w_ab, "nt", F32, "d_h_ab", extra=dh, epi=_epi_add)
    dx, _, G["norm1_g"] = _rms_bwd(dh, x, P["norm1_g"], dx1, "rms1_bwd")
    return sq_err, dx, G


def _comm(name, ins, out_shapes, plan):
    n_in, n_out = len(ins), len(out_shapes)
    probe = plan([None] * n_in, [None] * n_out, 0, 0, 0, dry=True)
    n_copy = probe

    def body(*refs):
        in_refs, out_refs = refs[:n_in], refs[n_in:n_in + n_out]
        send_sems, recv_sems = refs[n_in + n_out:]
        x, y, c = lax.axis_index("x"), lax.axis_index("y"), lax.axis_index("c")
        copies = []
        for k, (src, dst, dev) in enumerate(plan(in_refs, out_refs, x, y, c, dry=False)):
            if dev is None:
                cp = pltpu.make_async_copy(src, dst, send_sems.at[k])
            else:
                cp = pltpu.make_async_remote_copy(src_ref=src, dst_ref=dst, send_sem=send_sems.at[k],
                                                  recv_sem=recv_sems.at[k], device_id=dev, device_id_type=MESH)
            cp.start()
            copies.append(cp)
        for cp in copies:
            cp.wait()

    return pl.pallas_call(
        body, out_shape=tuple(out_shapes), in_specs=[HBM] * n_in, out_specs=tuple([HBM] * n_out),
        scratch_shapes=[pltpu.SemaphoreType.DMA((n_copy,)), pltpu.SemaphoreType.DMA((n_copy,))], name=name)(*ins)


def _other_chips(x, y):
    return ((1 - x, y), (x, 1 - y), (1 - x, 1 - y))


def _gather_plan(parts, direct=()):
    n, every = len(parts), list(parts) + list(direct)

    def copies(ins, outs, send, recv, scratch):
        x, y, c = lax.axis_index("x"), lax.axis_index("y"), lax.axis_index("c")
        me = 2 * x + y
        chips = _other_chips(x, y)
        local_sems, staged = scratch[0], scratch[1:]

        def remote(src, dst, k, dev):
            return pltpu.make_async_remote_copy(src_ref=src, dst_ref=dst, send_sem=send.at[k], recv_sem=recv.at[k],
                                                device_id=dev, device_id_type=MESH)

        def half(p, ci):
            hr = ins[p].shape[0] // 2
            return pl.ds(pl.multiple_of(ci * hr, 16), hr)

        sent = [remote(ins[p].at[half(p, c)], outs[p].at[me, half(p, c)], 6 * p + f, (px, py, c))
                for p in range(n) for f, (px, py) in enumerate(chips)]
        sent += [remote(ins[p], outs[p].at[me], 6 * n + 3 * (p - n) + f, (px, py, c))
                 for p in range(n, len(every)) for f, (px, py) in enumerate(chips)]
        landed = [outs[p].at[2 * px + py, half(p, c)] for p in range(n) for px, py in chips]
        passed = [remote(landed[3 * p + f], landed[3 * p + f], 6 * p + 3 + f, (x, y, 1 - c))
                  for p in range(n) for f in range(3)]
        loads = [pltpu.make_async_copy(ins[p], staged[p], local_sems.at[2 * p]) for p in range(len(every))]
        stores = [pltpu.make_async_copy(staged[p], outs[p].at[me], local_sems.at[2 * p + 1]) for p in range(len(every))]
        return sent, passed, loads, stores

    def start(*refs):
        sent, _, loads, _ = copies(*refs)
        for cp in loads + sent:
            cp.start()

    def mid(*refs):
        sent, passed, loads, stores = copies(*refs)
        for ld, st in zip(loads, stores):
            ld.wait()
            st.start()
        for p in range(n):
            for f in range(3):
                sent[3 * p + f].wait_recv()
                passed[3 * p + f].start()

    def finish(*refs):
        sent, passed, _, stores = copies(*refs)
        for cp in sent[:3 * n]:
            cp.wait_send()
        for cp in passed + sent[3 * n:] + stores:
            cp.wait()

    return _Hosted(every, [jax.ShapeDtypeStruct((4,) + p.shape, p.dtype) for p in every], 6 * n + 3 * len(direct),
                   start, finish, mid,
                   [pltpu.SemaphoreType.DMA((2 * len(every),))] + [pltpu.VMEM(p.shape, p.dtype) for p in every])


def _scatter_plan(pairs):
    def copies(ins, outs, send, recv, scratch):
        x, y, c = lax.axis_index("x"), lax.axis_index("y"), lax.axis_index("c")
        me = 2 * x + y
        return [pltpu.make_async_remote_copy(src_ref=src.at[2 * px + py], dst_ref=dst.at[me], send_sem=send.at[3 * p + f],
                                             recv_sem=recv.at[3 * p + f], device_id=(px, py, c), device_id_type=MESH)
                for p, (src, dst) in enumerate(zip(ins, outs)) for f, (px, py) in enumerate(_other_chips(x, y))]

    def start(*refs):
        for cp in copies(*refs):
            cp.start()

    def finish(*refs):
        for cp in copies(*refs):
            cp.wait()

    return _Hosted(pairs, [jax.ShapeDtypeStruct(a.shape, a.dtype) for a in pairs], 3 * len(pairs), start, finish)


def _run_hosted(comm, name):
    n_in, n_out = len(comm.ins), len(comm.out_shapes)

    def body(*refs):
        args = (refs[:n_in], refs[n_in:n_in + n_out], refs[n_in + n_out], refs[n_in + n_out + 1], refs[n_in + n_out + 2:])
        comm.start(*args)
        if comm.mid is not None:
            comm.mid(*args)
        comm.finish(*args)

    sems = [pltpu.SemaphoreType.DMA((comm.n_sems,)), pltpu.SemaphoreType.DMA((comm.n_sems,))]
    return list(pl.pallas_call(
        body, out_shape=tuple(comm.out_shapes), in_specs=[HBM] * n_in, out_specs=tuple([HBM] * n_out),
        scratch_shapes=sems + comm.scratch, name=name, compiler_params=_cp())(*comm.ins))


def _swap_halves(slabs, name):
    n = len(slabs)

    def body(*refs):
        ins, outs = refs[:n], refs[n:2 * n]
        send, recv = refs[2 * n:]
        x, y, c = lax.axis_index("x"), lax.axis_index("y"), lax.axis_index("c")
        other = (x, y, 1 - c)
        for p in range(n):
            for j in range(4):
                pltpu.make_async_remote_copy(src_ref=ins[p].at[j, 1 - c], dst_ref=outs[p].at[j], send_sem=send.at[p],
                                             recv_sem=recv.at[p], device_id=other, device_id_type=MESH).start()
        for p in range(n):
            pltpu.make_async_remote_copy(src_ref=outs[p], dst_ref=outs[p], send_sem=send.at[p], recv_sem=recv.at[p],
                                         device_id=other, device_id_type=MESH).wait()

    shapes = [jax.ShapeDtypeStruct((4,) + s.shape[2:], s.dtype) for s in slabs]
    return pl.pallas_call(
        body, out_shape=tuple(shapes), in_specs=[HBM] * n, out_specs=tuple([HBM] * n),
        scratch_shapes=[pltpu.SemaphoreType.DMA((n,)), pltpu.SemaphoreType.DMA((n,))], name=name)(*slabs)


def _join_halves(both):
    n = len(both)

    def body(*refs):
        bufs = refs[n:2 * n]
        send, recv = refs[2 * n:]
        x, y, c = lax.axis_index("x"), lax.axis_index("y"), lax.axis_index("c")
        copies = []
        for p in range(n):
            cp = pltpu.make_async_remote_copy(src_ref=bufs[p].at[c], dst_ref=bufs[p].at[c], send_sem=send.at[p],
                                              recv_sem=recv.at[p], device_id=(x, y, 1 - c), device_id_type=MESH)
            cp.start()
            copies.append(cp)
        for cp in copies:
            cp.wait()

    return pl.pallas_call(
        body, out_shape=tuple(jax.ShapeDtypeStruct(a.shape, a.dtype) for a in both), in_specs=[HBM] * n,
        out_specs=tuple([HBM] * n), input_output_aliases={p: p for p in range(n)},
        scratch_shapes=[pltpu.SemaphoreType.DMA((n,)), pltpu.SemaphoreType.DMA((n,))], name="grad_join_cores")(*both)


def _gather_all(a, name):
    def plan(ins, outs, x, y, c, dry):
        if dry:
            return 8
        me = 4 * x + 2 * y + c
        copies = [(ins[0], outs[0].at[me], None)]
        for f in range(1, 8):
            peer = (1 - x if f & 4 else x, 1 - y if f & 2 else y, 1 - c if f & 1 else c)
            copies.append((ins[0], outs[0].at[me], peer))
        return copies

    return _comm(name, [a], [jax.ShapeDtypeStruct((8,) + a.shape, a.dtype)], plan)[0]


def _sum_slots(a, name, extra=None):
    n, R, _ = a.shape
    rb = min(ROW_BLK, R)

    def body(*refs):
        a_ref, o_ref = refs[0], refs[-1]
        acc = a_ref[0]
        for s in range(1, n):
            acc = acc + a_ref[s]
        if extra is not None:
            acc = acc + refs[1][...]
        o_ref[...] = acc

    ins = [a] + ([extra] if extra is not None else [])
    in_specs = [pl.BlockSpec((n, rb, LANES), lambda i: (0, i, 0))] + ([_rowspec(rb, LANES)] if extra is not None else [])
    return pl.pallas_call(
        body, out_shape=jax.ShapeDtypeStruct((R, LANES), F32), grid=(R // rb,), in_specs=in_specs,
        out_specs=_rowspec(rb, LANES), name=name, compiler_params=_cp("parallel"))(*ins)


def _pair_sum(slab, theirs, core, name):
    _, _, hr, C = slab.shape

    def body(c_ref, a_ref, b_ref, o_ref):
        o_ref[...] = (a_ref[...] + b_ref[...]).astype(BF16)

    return pl.pallas_call(
        body, out_shape=jax.ShapeDtypeStruct((4, hr, C), BF16),
        grid_spec=pltpu.PrefetchScalarGridSpec(
            num_scalar_prefetch=1, grid=(4,),
            in_specs=[pl.BlockSpec((None, None, hr, C), lambda j, c_ref: (j, c_ref[0], 0, 0)),
                      pl.BlockSpec((None, hr, C), lambda j, c_ref: (j, 0, 0))],
            out_specs=pl.BlockSpec((None, hr, C), lambda j, c_ref: (j, 0, 0))),
        name=name, compiler_params=_cp("parallel"))(core, slab, theirs)


def _chip_sum(recv, pairs, where, name):
    _, hr, C = recv.shape

    def body(w_ref, r_ref, p_ref, o_ref):
        me = w_ref[0]
        o_ref[...] = jnp.zeros_like(o_ref)
        for s in range(4):
            @pl.when(me == s)
            def _():
                o_ref[...] += p_ref[...].astype(F32)

            @pl.when(me != s)
            def _():
                o_ref[...] += r_ref[s].astype(F32)

    return pl.pallas_call(
        body, out_shape=jax.ShapeDtypeStruct((2, hr, C), F32),
        grid_spec=pltpu.PrefetchScalarGridSpec(
            num_scalar_prefetch=1, grid=(1,),
            in_specs=[pl.BlockSpec((4, hr, C), lambda i, w_ref: (0, 0, 0)),
                      pl.BlockSpec((None, hr, C), lambda i, w_ref: (w_ref[0], 0, 0))],
            out_specs=pl.BlockSpec((None, hr, C), lambda i, w_ref: (w_ref[1], 0, 0))),
        name=name, compiler_params=_cp("arbitrary"))(where, recv, pairs)


def _adamw(w, g, m, v, name):
    R, C = w.shape
    rb = min(ADAM_ROWS, R)
    c1 = 1.0 - ADAM_B1 ** ADAM_STEP
    c2 = 1.0 - ADAM_B2 ** ADAM_STEP

    def body(w_ref, g_ref, m_ref, v_ref, d_ref, nm_ref, nv_ref):
        gv = g_ref[...]
        nm = ADAM_B1 * m_ref[...] + (1.0 - ADAM_B1) * gv
        nv = ADAM_B2 * v_ref[...] + (1.0 - ADAM_B2) * (gv * gv)
        d_ref[...] = -ADAM_LR * ((nm / c1) / (jnp.sqrt(nv / c2) + ADAM_EPS) + ADAM_WD * w_ref[...])
        nm_ref[...] = nm
        nv_ref[...] = nv

    spec = _rowspec(rb, C)
    return pl.pallas_call(
        body, out_shape=(jax.ShapeDtypeStruct((R, C), F32),) * 3, grid=(R // rb,), in_specs=[spec] * 4,
        out_specs=(spec,) * 3, name=name, compiler_params=_cp("parallel"))(w, g, m, v)


class _Dist:
    def __init__(self, shards):
        self.shards = shards
        self.chip = 2 * lax.axis_index("x") + lax.axis_index("y")
        self.where = jnp.stack([self.chip, lax.axis_index("c")]).astype(jnp.int32)
        self.pairs, self.landed = {}, {}

    @staticmethod
    def _unshard(name, blk):
        _, (r, cc), axis = next(b for b in BIG if b[0] == name)
        return blk.reshape(4 * r, cc) if axis == 0 else blk.transpose(1, 0, 2).reshape(r, 4 * cc)

    def gather_first(self):
        got = _run_hosted(_gather_plan([self.shards["w_in"].astype(BF16)]), "gather_w_in")[0]
        return _split_w_in([got[j] for j in range(4)])

    def gather_rest(self):
        rest = [self.shards[n].astype(BF16) for n, _, _ in BIG if n != "w_in"]
        return _gather_plan(rest, [self.shards["conv_w"]])

    def weights_from(self, gathered):
        names = [n for n, _, _ in BIG if n != "w_in"]
        conv = gathered[-1]
        taps, width = conv.shape[1:]
        return ({n: self._unshard(n, g) for n, g in zip(names, gathered)},
                conv.transpose(1, 0, 2).reshape(taps, 4 * width))

    def scatter(self, G, names, tag):
        slabs = []
        for name, (r, cc), axis in BIG:
            if name not in names:
                continue
            g = G[name]
            if axis == 0:
                slabs.append(g.reshape(4, 2, r // 2, cc))
            else:
                slabs.append(g.reshape(4, 2, cc // 2, r))
        theirs = _swap_halves(slabs, "grad_swap_cores_" + tag)
        pairs = [_pair_sum(s, t, self.where[1:], "pair_sum_" + n) for s, t, n in zip(slabs, theirs, names)]
        self.pairs.update(zip(names, pairs))
        return _scatter_plan(pairs)

    def collect(self, names, landed):
        self.landed.update(zip(names, landed))

    def finish(self):
        names = [n for n, _, _ in BIG]
        halves = [_chip_sum(self.landed[n], self.pairs[n], self.where, "chip_sum_" + n) for n in names]
        out = {}
        for (name, (r, cc), axis), both in zip(BIG, _join_halves(halves)):
            full = both.reshape(-1, both.shape[-1])
            out[name] = full if axis == 0 else full.T
        return out


def _pack_rows(parts, rows, dtype):
    flat = jnp.concatenate([p.reshape(-1).astype(dtype) for p in parts])
    return jnp.pad(flat, (0, rows * LANES - flat.shape[0])).reshape(rows, LANES)


def _small_rows(n):
    return max(n // LANES, 1)


def _pack_small(vals):
    rows = []
    for name, n in SMALL:
        r = _small_rows(n)
        rows.append(jnp.pad(vals[name].reshape(-1), (0, r * LANES - n)).reshape(r, LANES))
    flat = jnp.concatenate(rows, axis=0)
    return jnp.pad(flat, ((0, SMALL_ROWS - flat.shape[0]), (0, 0)))


def _unpack_small(pack):
    out, r0 = {}, 0
    for name, n in SMALL:
        r = _small_rows(n)
        out[name] = pack[r0:r0 + r].reshape(-1)[:n]
        r0 += r
    return out


def kernel(x, mem, norm1_g, w_in, conv_w, a_log, dt_bias, gdn_norm_g, sb_q_norm_g, sb_k_norm_g, mem_norm_g, w_mem_kv, mem_q_norm_g, mem_k_norm_g, w_br_gdn, w_br_sb, w_br_mem, w_o, norm2_g, w_up, w_down, loss_target, m_norm1_g, m_w_in, m_conv_w, m_a_log, m_dt_bias, m_gdn_norm_g, m_sb_q_norm_g, m_sb_k_norm_g, m_mem_norm_g, m_w_mem_kv, m_mem_q_norm_g, m_mem_k_norm_g, m_w_br_gdn, m_w_br_sb, m_w_br_mem, m_w_o, m_norm2_g, m_w_up, m_w_down, v_norm1_g, v_w_in, v_conv_w, v_a_log, v_dt_bias, v_gdn_norm_g, v_sb_q_norm_g, v_sb_k_norm_g, v_mem_norm_g, v_w_mem_kv, v_mem_q_norm_g, v_mem_k_norm_g, v_w_br_gdn, v_w_br_sb, v_w_br_mem, v_w_o, v_norm2_g, v_w_up, v_w_down):
    wd = dict(norm1_g=norm1_g, w_in=w_in, conv_w=conv_w, a_log=a_log, dt_bias=dt_bias, gdn_norm_g=gdn_norm_g,
              sb_q_norm_g=sb_q_norm_g, sb_k_norm_g=sb_k_norm_g, mem_norm_g=mem_norm_g, w_mem_kv=w_mem_kv,
              mem_q_norm_g=mem_q_norm_g, mem_k_norm_g=mem_k_norm_g, w_br_gdn=w_br_gdn, w_br_sb=w_br_sb,
              w_br_mem=w_br_mem, w_o=w_o, norm2_g=norm2_g, w_up=w_up, w_down=w_down)
    md = dict(norm1_g=m_norm1_g, w_in=m_w_in, conv_w=m_conv_w, a_log=m_a_log, dt_bias=m_dt_bias,
              gdn_norm_g=m_gdn_norm_g, sb_q_norm_g=m_sb_q_norm_g, sb_k_norm_g=m_sb_k_norm_g,
              mem_norm_g=m_mem_norm_g, w_mem_kv=m_w_mem_kv, mem_q_norm_g=m_mem_q_norm_g,
              mem_k_norm_g=m_mem_k_norm_g, w_br_gdn=m_w_br_gdn, w_br_sb=m_w_br_sb, w_br_mem=m_w_br_mem, w_o=m_w_o,
              norm2_g=m_norm2_g, w_up=m_w_up, w_down=m_w_down)
    vd = dict(norm1_g=v_norm1_g, w_in=v_w_in, conv_w=v_conv_w, a_log=v_a_log, dt_bias=v_dt_bias,
              gdn_norm_g=v_gdn_norm_g, sb_q_norm_g=v_sb_q_norm_g, sb_k_norm_g=v_sb_k_norm_g,
              mem_norm_g=v_mem_norm_g, w_mem_kv=v_w_mem_kv, mem_q_norm_g=v_mem_q_norm_g,
              mem_k_norm_g=v_mem_k_norm_g, w_br_gdn=v_w_br_gdn, w_br_sb=v_w_br_sb, w_br_mem=v_w_br_mem, w_o=v_w_o,
              norm2_g=v_norm2_g, w_up=v_w_up, w_down=v_w_down)
    wd, md, vd = ({n: a[0] for n, a in d.items()} for d in (wd, md, vd))
    chip = 2 * lax.axis_index("x") + lax.axis_index("y")
    conv_shard = wd["conv_w"].shape

    dist = _Dist(wd)
    W = dict(zip(("w_main", "w_ab"), dist.gather_first()))
    P = {n: wd[n].reshape(1, -1) for n, _ in SMALL}

    sq_err, grad_x, G = _local_step(x[0], mem[0], loss_target[0], W, P, dist)
    loss = lax.psum(0.5 / D_MODEL * jnp.sum(sq_err), ("x", "y", "c"))

    g_big = dist.finish()

    spack = jnp.concatenate([_pack_small(G), G["conv_w"].reshape(CONV_ROWS, LANES)], axis=0)
    g_small = _sum_slots(_gather_all(spack, "gather_small_grads"), "small_grad_sum")
    g_conv_full = g_small[SMALL_ROWS:].reshape(conv_shard[0], 4 * conv_shard[1])
    g_conv = lax.dynamic_slice_in_dim(g_conv_full, chip * conv_shard[1], conv_shard[1], axis=1)

    grads, deltas, new_m, new_v = dict(g_big), {}, {}, {}
    for name, _, _ in BIG:
        deltas[name], new_m[name], new_v[name] = _adamw(wd[name], g_big[name], md[name], vd[name], "adamw_" + name)
    pack_sm = lambda d: jnp.concatenate([_pack_small(d), _pack_rows([d["conv_w"]], APACK_ROWS - SMALL_ROWS, F32)], axis=0)
    g_sm = jnp.concatenate([g_small[:SMALL_ROWS], _pack_rows([g_conv], APACK_ROWS - SMALL_ROWS, F32)], axis=0)
    small = (g_sm,) + _adamw(pack_sm(wd), g_sm, pack_sm(md), pack_sm(vd), "adamw_small")
    for out, pack in zip((grads, deltas, new_m, new_v), small):
        out.update(_unpack_small(pack[:SMALL_ROWS]))
        out["conv_w"] = pack[SMALL_ROWS:].reshape(-1)[:conv_shard[0] * conv_shard[1]].reshape(conv_shard)

    return (loss, grad_x[None], *[d[n][None] for d in (grads, deltas, new_m, new_v) for n in WEIGHTS])
```

```python
import jax
import jax.numpy as jnp
from jax import lax
from jax.experimental import pallas as pl
from jax.experimental.pallas import tpu as pltpu

F32 = jnp.float32
BF16 = jnp.bfloat16
MESH = pl.DeviceIdType.MESH

D_MODEL = 1024
N_HEAD = 4
D_HEAD = 128
BR_W = N_HEAD * D_HEAD
CONV_TAPS = 4
GDN_CHUNK = 64
INV_BLOCK = 16
INV_CHUNKS = 4
N_MEM = 256
D_FF = 4 * D_MODEL
EPS = 1e-6
LANES = 128
PROJ_W = 7168
GATE_OFF = 0
QKV_OFF = 3072
SB_OFF = 4608
Z_OFF = 6144
MEMQ_OFF = 6656
IN_GATE, IN_QKV, IN_SB, IN_Z, IN_MEMQ, IN_AB = (4104, 7176), (0, 1536), (2056, 3592), (1536, 2048), (3592, 4104), (2048, 2056)

ADAM_LR, ADAM_B1, ADAM_B2, ADAM_EPS, ADAM_WD, ADAM_STEP = 0.001, 0.9, 0.999, 1e-08, 0.01, 10

TM = 512
MM_TM = 1024
TK_TOK = 1024
GDN_STEP_CHUNKS = 4
GDN_BWD_STEP_CHUNKS = 1
G1_TM = 256
SB_BLK = 512
SB_W = 256
SB_DEAD = 120.0
VMEM_LIMIT = 48 << 20

BIG = (("w_in", (1024, 1794), 1), ("w_mem_kv", (256, 1024), 0), ("w_br_gdn", (512, 256), 1),
       ("w_br_sb", (512, 256), 1), ("w_br_mem", (512, 256), 1), ("w_o", (256, 1024), 0),
       ("w_up", (1024, 1024), 1), ("w_down", (1024, 1024), 0))
COL_SHARDED = tuple(n for n, _, a in BIG if a == 1)
GATE_COL = GATE_OFF // D_MODEL
QKV_COL = QKV_OFF // (3 * BR_W)
ROW_BLK = 1024
ADAM_ROWS = 128
SMALL = (("norm1_g", 1024), ("mem_norm_g", 1024), ("norm2_g", 1024), ("gdn_norm_g", 128), ("sb_q_norm_g", 128),
         ("sb_k_norm_g", 128), ("mem_q_norm_g", 128), ("mem_k_norm_g", 128), ("a_log", 4), ("dt_bias", 4))
SMALL_ROWS = 32
CONV_ROWS = 48
SPACK_ROWS = SMALL_ROWS + CONV_ROWS
APACK_ROWS = SMALL_ROWS + 16

WEIGHTS = ("norm1_g", "w_in", "conv_w", "a_log", "dt_bias", "gdn_norm_g", "sb_q_norm_g", "sb_k_norm_g",
           "mem_norm_g", "w_mem_kv", "mem_q_norm_g", "mem_k_norm_g", "w_br_gdn", "w_br_sb", "w_br_mem", "w_o",
           "norm2_g", "w_up", "w_down")


def _cp(*sem):
    return pltpu.CompilerParams(dimension_semantics=sem if sem else None, vmem_limit_bytes=VMEM_LIMIT)


HBM = pl.BlockSpec(memory_space=pl.ANY)

_NN = (((1,), (0,)), ((), ()))
_NT = (((1,), (1,)), ((), ()))
_TN = (((0,), (0,)), ((), ()))


def _dot(a, b, dims=_NN):
    return lax.dot_general(a.astype(BF16), b.astype(BF16), dims, preferred_element_type=F32)


def _dot_nt(a, b):
    return _dot(a, b, _NT)


def _dot_tn(a, b):
    return _dot(a, b, _TN)


def _dotf(a, b, dims=_NN):
    return lax.dot_general(a, b, dims, precision=lax.Precision.HIGHEST, preferred_element_type=F32)


def _sigmoid(v):
    return 0.5 * jnp.tanh(0.5 * v) + 0.5


def _softplus(v):
    return jnp.maximum(v, 0.0) + jnp.log(1.0 + jnp.exp(-jnp.abs(v)))


def _iota(shape, dim):
    return lax.broadcasted_iota(jnp.int32, shape, dim)


def _hs(h):
    return slice(h * D_HEAD, (h + 1) * D_HEAD)


def _rowspec(tm, w, col=0):
    return pl.BlockSpec((tm, w), lambda i: (i, col))


def _full(shape):
    return pl.BlockSpec(shape, lambda *_: (0,) * len(shape))


def _accum(ref, first, val):
    @pl.when(first)
    def _():
        ref[...] = val

    @pl.when(jnp.logical_not(first))
    def _():
        ref[...] += val


class _Hosted:
    def __init__(self, ins, out_shapes, n_sems, start, finish, mid=None, scratch=()):
        self.ins, self.out_shapes, self.n_sems = list(ins), list(out_shapes), n_sems
        self.start, self.mid, self.finish, self.scratch = start, mid, finish, list(scratch)


def _mm(a, b, mode, out_dtype, name, *, tm=None, tn=None, tk=None, a_fn=None, extra=None, epi=None, comm=None):
    if mode == "tn":
        (K, M), N = a.shape, b.shape[1]
    else:
        (M, K), N = a.shape, (b.shape[0] if mode == "nt" else b.shape[1])
    tm = min(tm or (1024 if mode == "tn" else MM_TM), M)
    tn = min(tn or 1024, N)
    tk = min(tk or (TK_TOK if mode == "tn" else 1024), K)
    nm, nn, nk = M // tm, N // tn, K // tk
    assert nm * tm == M and nn * tn == N and nk * tk == K, (name, a.shape, b.shape)
    if mode == "tn":
        a_spec = pl.BlockSpec((tk, tm), lambda i, j, k: (k, i))
    else:
        a_spec = pl.BlockSpec((tm, tk), lambda i, j, k: (i, k))
    if mode == "nt":
        b_spec = pl.BlockSpec((tn, tk), lambda i, j, k: (j, k))
    else:
        b_spec = pl.BlockSpec((tk, tn), lambda i, j, k: (k, j))
    dims = {"nn": _NN, "nt": _NT, "tn": _TN}[mode]
    o_spec = pl.BlockSpec((tm, tn), lambda i, j, k: (i, j))
    has_extra = extra is not None

    n_ci, n_co = (len(comm.ins), len(comm.out_shapes)) if comm else (0, 0)
    n_in = 2 + has_extra + n_ci
    steps = nm * nn * nk

    def body(*refs):
        a_ref, b_ref = refs[0], refs[1]
        e_ref = refs[2] if has_extra else None
        o_ref = refs[n_in]
        scratch = refs[n_in + 1 + n_co:]
        if comm:
            step = (pl.program_id(0) * nn + pl.program_id(1)) * nk + pl.program_id(2)
            cargs = (refs[2 + has_extra:n_in], refs[n_in + 1:n_in + 1 + n_co], scratch[nk > 1], scratch[(nk > 1) + 1],
                     scratch[(nk > 1) + 2:])
            pl.when(step == 0)(lambda: comm.start(*cargs))
            if comm.mid is not None:
                pl.when(step == (steps * 7) // 8)(lambda: comm.mid(*cargs))
        av = a_ref[...]
        if a_fn is not None:
            av = a_fn(av)
        p = lax.dot_general(av, b_ref[...], dims, preferred_element_type=F32)

        def finish(acc):
            if epi is not None:
                acc = epi(acc, e_ref[...] if has_extra else None)
            o_ref[...] = acc.astype(out_dtype)

        if nk == 1:
            finish(p)
        else:
            acc_ref = scratch[0]
            k = pl.program_id(2)
            _accum(acc_ref, k == 0, p)

            @pl.when(k == nk - 1)
            def _():
                finish(acc_ref[...])

        if comm:
            pl.when(step == steps - 1)(lambda: comm.finish(*cargs))

    ins = [a, b] + ([extra] if has_extra else [])
    in_specs = [a_spec, b_spec] + ([o_spec] if has_extra else [])
    scratch_shapes = [pltpu.VMEM((tm, tn), F32)] if nk > 1 else []
    main = jax.ShapeDtypeStruct((M, N), out_dtype)
    if not comm:
        return pl.pallas_call(
            body, out_shape=main, grid=(nm, nn, nk), in_specs=in_specs, out_specs=o_spec,
            scratch_shapes=scratch_shapes, name=name, compiler_params=_cp("parallel", "parallel", "arbitrary"))(*ins)
    sems = [pltpu.SemaphoreType.DMA((comm.n_sems,)), pltpu.SemaphoreType.DMA((comm.n_sems,))]
    res = pl.pallas_call(
        body, out_shape=(main, *comm.out_shapes), grid=(nm, nn, nk), in_specs=in_specs + [HBM] * n_ci,
        out_specs=(o_spec, *[HBM] * n_co), scratch_shapes=scratch_shapes + sems + comm.scratch, name=name,
        compiler_params=_cp("arbitrary", "arbitrary", "arbitrary"))(*ins, *comm.ins)
    return res[0], list(res[1:])


def _relu2(u):
    r = jnp.maximum(u.astype(F32), 0.0)
    return (r * r).astype(BF16)


def _epi_add(acc, e):
    return acc + e.astype(F32)


def _epi_drelu2(acc, u):
    return acc * (2.0 * jnp.maximum(u.astype(F32), 0.0))


def _rms_fwd(x, g, name):
    T, dm = x.shape
    tm = min(TM, T)

    def body(x_ref, g_ref, h_ref):
        xv = x_ref[...]
        r = lax.rsqrt(jnp.mean(xv * xv, axis=-1, keepdims=True) + EPS)
        h_ref[...] = (xv * r * g_ref[...]).astype(BF16)

    return pl.pallas_call(
        body, out_shape=jax.ShapeDtypeStruct((T, dm), BF16), grid=(T // tm,),
        in_specs=[_rowspec(tm, dm), _full((1, dm))], out_specs=_rowspec(tm, dm), name=name,
        compiler_params=_cp("parallel"))(x, g)


def _rms_bwd(dh, x, g, resid, name):
    T, dm = x.shape
    tm = min(TM, T)

    def body(dh_ref, x_ref, g_ref, res_ref, dx_ref, dxb_ref, dg_ref):
        i = pl.program_id(0)
        xv = x_ref[...]
        r = lax.rsqrt(jnp.mean(xv * xv, axis=-1, keepdims=True) + EPS)
        y = xv * r
        dhv = dh_ref[...].astype(F32)
        dy = dhv * g_ref[...]
        dx = res_ref[...] + r * (dy - y * jnp.mean(dy * y, axis=-1, keepdims=True))
        dx_ref[...] = dx
        dxb_ref[...] = dx.astype(BF16)
        _accum(dg_ref, i == 0, jnp.sum(dhv * y, axis=0, keepdims=True))

    return pl.pallas_call(
        body,
        out_shape=(jax.ShapeDtypeStruct((T, dm), F32), jax.ShapeDtypeStruct((T, dm), BF16),
                   jax.ShapeDtypeStruct((1, dm), F32)),
        grid=(T // tm,),
        in_specs=[_rowspec(tm, dm), _rowspec(tm, dm), _full((1, dm)), _rowspec(tm, dm)],
        out_specs=(_rowspec(tm, dm), _rowspec(tm, dm), _full((1, dm))), name=name,
        compiler_params=_cp("arbitrary"))(dh, x, g, resid)


def _conv_tile(x_ref, halo_ref, w_ref, xpad, tm):
    i = pl.program_id(0)
    halo = halo_ref[...].astype(F32)[8:16]
    xpad[0:8, :] = jnp.where(i > 0, halo, 0.0)
    xpad[8:, :] = x_ref[...].astype(F32)
    w = w_ref[...]
    xc = w[0:1] * xpad[5:5 + tm, :]
    for j in range(1, CONV_TAPS):
        xc = xc + w[j:j + 1] * xpad[5 + j:5 + j + tm, :]
    return xc


def _gate_terms(ab_ref, av_ref):
    abv = ab_ref[...]
    av = av_ref[...]
    pre = abv + av[1:2]
    ea = jnp.exp(av[0:1])
    g = -ea * _softplus(pre)
    return abv, pre, ea, g


def _gdn_pre(proj, conv_w, ab, avec):
    T = proj.shape[0]
    tm = min(G1_TM, T)
    cw = 3 * BR_W

    def body(x_ref, halo_ref, w_ref, ab_ref, av_ref, q_ref, k_ref, v_ref, gb_ref, xpad):
        xc = _conv_tile(x_ref, halo_ref, w_ref, xpad, tm)
        y = xc * _sigmoid(xc)
        for h in range(N_HEAD):
            for off, ref, scale in ((0, q_ref, D_HEAD ** -0.5), (BR_W, k_ref, 1.0)):
                yh = y[:, off + h * D_HEAD:off + (h + 1) * D_HEAD]
                r = lax.rsqrt(jnp.sum(yh * yh, axis=-1, keepdims=True) + EPS)
                ref[:, _hs(h)] = yh * (r * scale)
        v_ref[...] = y[:, 2 * BR_W:]
        abv, _, _, g = _gate_terms(ab_ref, av_ref)
        lane = _iota((tm, LANES), 1)
        gb_ref[...] = jnp.where(lane < N_HEAD, g, jnp.where(lane < 2 * N_HEAD, _sigmoid(abv), 0.0))

    hb = tm // 16
    return pl.pallas_call(
        body,
        out_shape=(jax.ShapeDtypeStruct((T, BR_W), F32),) * 3 + (jax.ShapeDtypeStruct((T, LANES), F32),),
        grid=(T // tm,),
        in_specs=[_rowspec(tm, cw, QKV_COL), pl.BlockSpec((16, cw), lambda i: (jnp.maximum(i * hb - 1, 0), QKV_COL)),
                  _full((CONV_TAPS, cw)), _rowspec(tm, LANES), _full((2, LANES))],
        out_specs=(_rowspec(tm, BR_W),) * 3 + (_rowspec(tm, LANES),),
        scratch_shapes=[pltpu.VMEM((tm + 8, cw), F32)], name="gdn_pre",
        compiler_params=_cp("parallel"))(proj, proj, conv_w, ab, avec)


def _gdn_pre_bwd(proj, conv_w, ab, avec, dq, dk, dv, dgb):
    T = proj.shape[0]
    tm = min(G1_TM, T)
    cw = 3 * BR_W

    def body(x_ref, halo_ref, w_ref, ab_ref, av_ref, dq_ref, dk_ref, dv_ref, dgb_ref,
             dxc_ref, dab_ref, dcw_ref, dav_ref, xpad):
        i = pl.program_id(0)

        @pl.when(i == 0)
        def _():
            dcw_ref[...] = jnp.zeros_like(dcw_ref)
            dav_ref[...] = jnp.zeros_like(dav_ref)

        xc_all = _conv_tile(x_ref, halo_ref, w_ref, xpad, tm)
        for s in range(cw // D_HEAD):
            cs = slice(s * D_HEAD, (s + 1) * D_HEAD)
            xc = xc_all[:, cs]
            sg = _sigmoid(xc)
            yh = xc * sg
            h = s % N_HEAD
            if s < 2 * N_HEAD:
                dref, scale = (dq_ref, D_HEAD ** -0.5) if s < N_HEAD else (dk_ref, 1.0)
                r = lax.rsqrt(jnp.sum(yh * yh, axis=-1, keepdims=True) + EPS)
                yn = yh * r
                dn = dref[:, _hs(h)]
                dy = (scale * r) * (dn - yn * jnp.sum(yn * dn, axis=-1, keepdims=True))
            else:
                dy = dv_ref[:, _hs(h)]
            dxc = dy * (sg * (1.0 + xc * (1.0 - sg)))
            dxc_ref[:, cs] = dxc.astype(BF16)
            for j in range(CONV_TAPS):
                dcw_ref[j:j + 1, cs] += jnp.sum(dxc * xpad[5 + j:5 + j + tm, cs], axis=0, keepdims=True)

        abv, pre, ea, g = _gate_terms(ab_ref, av_ref)
        dgbv = dgb_ref[...]
        lane = _iota((tm, LANES), 1)
        is_a = lane < N_HEAD
        da = jnp.where(is_a, dgbv * (-ea) * _sigmoid(pre), 0.0)
        bs = _sigmoid(abv)
        db = jnp.where(jnp.logical_and(lane >= N_HEAD, lane < 2 * N_HEAD), dgbv * bs * (1.0 - bs), 0.0)
        dab_ref[...] = (da + db).astype(BF16)
        dav_ref[0:1, :] += jnp.sum(jnp.where(is_a, dgbv * g, 0.0), axis=0, keepdims=True)
        dav_ref[1:2, :] += jnp.sum(da, axis=0, keepdims=True)

    hb = tm // 16
    return pl.pallas_call(
        body,
        out_shape=(jax.ShapeDtypeStruct((T, cw), BF16), jax.ShapeDtypeStruct((T, LANES), BF16),
                   jax.ShapeDtypeStruct((CONV_TAPS, cw), F32), jax.ShapeDtypeStruct((2, LANES), F32)),
        grid=(T // tm,),
        in_specs=[_rowspec(tm, cw, QKV_COL), pl.BlockSpec((16, cw), lambda i: (jnp.maximum(i * hb - 1, 0), QKV_COL)),
                  _full((CONV_TAPS, cw)), _rowspec(tm, LANES), _full((2, LANES)),
                  _rowspec(tm, BR_W), _rowspec(tm, BR_W), _rowspec(tm, BR_W), _rowspec(tm, LANES)],
        out_specs=(_rowspec(tm, cw), _rowspec(tm, LANES), _full((CONV_TAPS, cw)), _full((2, LANES))),
        scratch_shapes=[pltpu.VMEM((tm + 8, cw), F32)], name="gdn_pre_bwd",
        compiler_params=_cp("arbitrary"))(proj, proj, conv_w, ab, avec, dq, dk, dv, dgb)


def _conv_bwd(dxc, conv_w, into):
    T, cw = dxc.shape
    tm = min(G1_TM, T)
    nt = T // tm
    hb = tm // 16

    def body(d_ref, halo_ref, w_ref, into_ref, dx_ref, xpad):
        i = pl.program_id(0)
        xpad[0:tm, :] = d_ref[...].astype(F32)
        xpad[tm:, :] = jnp.where(i < nt - 1, halo_ref[...].astype(F32)[0:8], 0.0)
        w = w_ref[...]
        dx = w[3:4] * xpad[0:tm, :]
        for j in range(CONV_TAPS - 1):
            dx = dx + w[j:j + 1] * xpad[3 - j:3 - j + tm, :]
        dx_ref[...] = dx.astype(BF16)

    return pl.pallas_call(
        body, out_shape=jax.ShapeDtypeStruct(into.shape, BF16), grid=(nt,),
        in_specs=[_rowspec(tm, cw), pl.BlockSpec((16, cw), lambda i: (jnp.minimum((i + 1) * hb, T // 16 - 1), 0)),
                  _full((CONV_TAPS, cw)), HBM],
        out_specs=_rowspec(tm, cw, QKV_COL), scratch_shapes=[pltpu.VMEM((tm + 8, cw), F32)],
        input_output_aliases={3: 0}, name="conv_bwd", compiler_params=_cp("parallel"))(dxc, dxc, conv_w, into)


def _chunk_consts():
    C = GDN_CHUNK
    row, col = _iota((C, C), 0), _iota((C, C), 1)
    return row, col, row >= col, row > col


def _chunk_decay(gbv, incl):
    c_all = _dotf(incl.astype(F32), gbv)
    c_t = jnp.concatenate([c_all, jnp.zeros_like(c_all)], axis=0).T[:, :GDN_CHUNK]
    return c_all, c_t


def _head_decay(c_all, c_t, gbv, incl, h):
    C = GDN_CHUNK
    c_col = c_all[:, h:h + 1]
    c_row = c_t[h:h + 1, :]
    gam = jnp.exp(jnp.where(incl, c_col - c_row, -1e30))
    c_last = c_all[C - 1:C, h:h + 1]
    return gam, jnp.exp(c_col), jnp.exp(c_last - c_col), jnp.exp(c_last), gbv[:, N_HEAD + h:N_HEAD + h + 1]


def _split_bf16(x):
    hi = x.astype(BF16)
    return hi, (x - hi.astype(F32)).astype(BF16)


def _dot3(a, b):
    ah, al = _split_bf16(a)
    bh, bl = _split_bf16(b)
    d = lambda u, v: lax.dot_general(u, v, _NN, preferred_element_type=F32)
    return d(ah, bh) + (d(ah, bl) + d(al, bh))


def _unit_lower_inverses(ms, row, col):
    bi, bj = row // INV_BLOCK, col // INV_BLOCK
    eye = (row == col).astype(F32)
    ns = [jnp.where(bi == bj, -m, 0.0) for m in ms]
    invs = [eye + n for n in ns]
    size = 2
    while size < INV_BLOCK:
        ns = [_dot3(n, n) for n in ns]
        invs = [inv + _dot3(inv, n) for inv, n in zip(invs, ns)]
        size *= 2
    width = 2
    while width * INV_BLOCK <= GDN_CHUNK:
        sel = jnp.logical_and(bi // width == bj // width, bi // (width // 2) > bj // (width // 2))
        ts = [_dot3(inv, jnp.where(sel, m, 0.0)) for inv, m in zip(invs, ms)]
        invs = [inv - _dot3(t, inv) for inv, t in zip(invs, ts)]
        width *= 2
    return invs


def _gdn_inv(k, gb):
    T = k.shape[0]
    C = GDN_CHUNK
    per = min(INV_CHUNKS, T // C)
    rows = per * C

    def body(k_ref, gb_ref, ti_ref, tt_ref):
        row, col, incl, strict = _chunk_consts()
        ms = []
        for ci in range(per):
            rs = slice(ci * C, (ci + 1) * C)
            gbv = gb_ref[rs, :]
            c_all, c_t = _chunk_decay(gbv, incl)
            for h in range(N_HEAD):
                gam, _, _, _, bcol = _head_decay(c_all, c_t, gbv, incl, h)
                K = k_ref[rs, _hs(h)]
                ms.append(jnp.where(strict, _dot_nt(K * bcol, K) * gam, 0.0))
        eye = (row == col).astype(BF16)
        for i, inv in enumerate(_unit_lower_inverses(ms, row, col)):
            ti_ref[i // N_HEAD, i % N_HEAD] = inv
            tt_ref[i // N_HEAD, i % N_HEAD] = _dot_tn(inv, eye).astype(BF16)

    spec = pl.BlockSpec((per, N_HEAD, C, C), lambda i: (i, 0, 0, 0))
    return pl.pallas_call(
        body, out_shape=(jax.ShapeDtypeStruct((T // C, N_HEAD, C, C), F32),
                         jax.ShapeDtypeStruct((T // C, N_HEAD, C, C), BF16)),
        grid=(T // rows,), in_specs=[_rowspec(rows, BR_W), _rowspec(rows, LANES)], out_specs=(spec, spec),
        name="gdn_inv", compiler_params=_cp("parallel"))(k, gb)


def _gdn_fwd(q, k, v, gb, proj, gnorm, tinv_all):
    T = q.shape[0]
    C = GDN_CHUNK
    nc = T // C
    per = min(GDN_STEP_CHUNKS, nc)
    zcol = Z_OFF // BR_W
    heads = range(N_HEAD)

    def body(q_ref, k_ref, v_ref, gb_ref, z_ref, gn_ref, ti_ref, og_ref, oraw_ref, sh_ref, vn_ref, s_ref):
        @pl.when(pl.program_id(0) == 0)
        def _():
            s_ref[...] = jnp.zeros_like(s_ref)

        _, _, incl, _ = _chunk_consts()
        S = [s_ref[h] for h in heads]
        for ci in range(per):
            rs = slice(ci * C, (ci + 1) * C)
            gbv = gb_ref[rs, :]
            c_all, c_t = _chunk_decay(gbv, incl)
            dec = [_head_decay(c_all, c_t, gbv, incl, h) for h in heads]
            gam, gcol, dcol, glast, bcol = ([d[i] for d in dec] for i in range(5))
            Q = [q_ref[rs, _hs(h)] for h in heads]
            K = [k_ref[rs, _hs(h)] for h in heads]
            V = [v_ref[rs, _hs(h)] for h in heads]
            Sb = [s.astype(BF16) for s in S]
            KS = [_dot(K[h], Sb[h]) for h in heads]
            QS = [_dot(Q[h], Sb[h]) for h in heads]
            P = [_dot_nt(Q[h], K[h]) * gam[h] for h in heads]
            R = [bcol[h] * (V[h] - gcol[h] * KS[h]) for h in heads]
            vn = [_dot(ti_ref[ci, h], R[h]) for h in heads]
            O = [gcol[h] * QS[h] + _dot(P[h], vn[h]) for h in heads]
            Sn = [glast[h] * S[h] + _dot_tn(K[h] * dcol[h], vn[h]) for h in heads]
            for h in heads:
                sh_ref[ci, h] = S[h]
                vn_ref[rs, _hs(h)] = vn[h]
                oraw_ref[rs, _hs(h)] = O[h]
                rr = lax.rsqrt(jnp.mean(O[h] * O[h], axis=-1, keepdims=True) + EPS)
                zz = z_ref[rs, _hs(h)].astype(F32)
                og_ref[rs, _hs(h)] = (O[h] * rr * gn_ref[...] * (zz * _sigmoid(zz))).astype(BF16)
            S = Sn
        for h in heads:
            s_ref[h] = S[h]

    cspec = lambda w, cb=0: pl.BlockSpec((per * C, w), lambda n: (n, cb))
    hist = lambda a, b: pl.BlockSpec((per, N_HEAD, a, b), lambda n: (n, 0, 0, 0))
    return pl.pallas_call(
        body,
        out_shape=(jax.ShapeDtypeStruct((T, BR_W), BF16), jax.ShapeDtypeStruct((T, BR_W), F32),
                   jax.ShapeDtypeStruct((nc, N_HEAD, D_HEAD, D_HEAD), F32), jax.ShapeDtypeStruct((T, BR_W), F32)),
        grid=(nc // per,),
        in_specs=[cspec(BR_W), cspec(BR_W), cspec(BR_W), cspec(LANES), cspec(BR_W, zcol), _full((1, D_HEAD)),
                  hist(C, C)],
        out_specs=(cspec(BR_W), cspec(BR_W), hist(D_HEAD, D_HEAD), cspec(BR_W)),
        scratch_shapes=[pltpu.VMEM((N_HEAD, D_HEAD, D_HEAD), F32)], name="gdn_chunk_fwd",
        compiler_params=_cp("arbitrary"))(q, k, v, gb, proj, gnorm, tinv_all)


def _gdn_bwd(q, k, v, gb, proj, gnorm, oraw, shist, tinv_all, vn_all, dog, into):
    T = q.shape[0]
    C = GDN_CHUNK
    nc = T // C
    per = min(GDN_BWD_STEP_CHUNKS, nc)
    zcol = Z_OFF // BR_W

    def body(q_ref, k_ref, v_ref, gb_ref, z_ref, gn_ref, oraw_ref, sh_ref, tt_ref, vn_ref, dog_ref, into_ref,
             dq_ref, dk_ref, dv_ref, dgb_ref, dz_ref, dgn_ref, ds_ref):
        @pl.when(pl.program_id(0) == 0)
        def _():
            ds_ref[...] = jnp.zeros_like(ds_ref)
            dgn_ref[...] = jnp.zeros_like(dgn_ref)

        row, col, incl, strict = _chunk_consts()
        lane = _iota((C, LANES), 1)
        rowl = _iota((C, LANES), 0)
        eye = (row == col).astype(F32)
        upper = (col >= row).astype(F32)
        gn = gn_ref[...]
        heads = range(N_HEAD)
        rsum = lambda a: jnp.sum(a, axis=-1, keepdims=True)
        dgn = jnp.zeros((1, D_HEAD), F32)
        dSn = [ds_ref[h] for h in heads]
        for ci in reversed(range(per)):
            rs = slice(ci * C, (ci + 1) * C)
            gbv = gb_ref[rs, :]
            c_all, c_t = _chunk_decay(gbv, incl)
            dec = [_head_decay(c_all, c_t, gbv, incl, h) for h in heads]
            gam, gcol, dcol, glast, bcol = ([d[i] for d in dec] for i in range(5))
            Q = [q_ref[rs, _hs(h)] for h in heads]
            K = [k_ref[rs, _hs(h)] for h in heads]
            V = [v_ref[rs, _hs(h)] for h in heads]
            dO = []
            for h in heads:
                O = oraw_ref[rs, _hs(h)]
                zz = z_ref[rs, _hs(h)].astype(F32)
                dogv = dog_ref[rs, _hs(h)].astype(F32)
                rr = lax.rsqrt(jnp.mean(O * O, axis=-1, keepdims=True) + EPS)
                on = O * rr
                sg = _sigmoid(zz)
                dz_ref[rs, _hs(h)] = (dogv * on * gn * (sg * (1.0 + zz * (1.0 - sg)))).astype(BF16)
                dyn = dogv * (zz * sg)
                dgn = dgn + jnp.sum(dyn * on, axis=0, keepdims=True)
                dyv = dyn * gn
                dO.append((rr * (dyv - on * jnp.mean(dyv * on, axis=-1, keepdims=True))).astype(BF16))
            S = [sh_ref[ci, h] for h in heads]
            Sb = [s.astype(BF16) for s in S]
            vn = [vn_ref[rs, _hs(h)] for h in heads]
            vnb = [a.astype(BF16) for a in vn]
            dSb = [a.astype(BF16) for a in dSn]
            Kb = [K[h] * bcol[h] for h in heads]
            gam_t = [jnp.exp(jnp.where(col >= row, c_t[h:h + 1, :] - c_all[:, h:h + 1], -1e30)) for h in heads]
            M = [jnp.where(strict, _dot_nt(Kb[h], K[h]) * gam[h], 0.0) for h in heads]
            P = [_dot_nt(Q[h], K[h]) * gam[h] for h in heads]
            P_t = [_dot_nt(K[h], Q[h]) * gam_t[h] for h in heads]
            KS = [_dot(K[h], Sb[h]) for h in heads]
            QS = [_dot(Q[h], Sb[h]) for h in heads]
            dvn = [_dot(P_t[h], dO[h]) + _dot(K[h] * dcol[h], dSb[h]) for h in heads]
            dR = [_dot(tt_ref[ci, h], dvn[h]) for h in heads]
            dRb = [a.astype(BF16) for a in dR]
            bg = [bcol[h] * gcol[h] for h in heads]
            dS_new = [glast[h] * dSn[h] + _dot_tn(gcol[h] * Q[h], dO[h]) - _dot_tn(bg[h] * K[h], dRb[h])
                      for h in heads]
            dP = [jnp.where(incl, _dot_nt(dO[h], vnb[h]), 0.0) for h in heads]
            dM = [jnp.where(strict, -_dot_nt(dRb[h], vnb[h]), 0.0) for h in heads]
            dPG = [(dP[h] * gam[h]).astype(BF16) for h in heads]
            dMG = [(dM[h] * gam[h]).astype(BF16) for h in heads]
            dPG_t = [(jnp.where(col >= row, _dot_nt(vnb[h], dO[h]), 0.0) * gam_t[h]).astype(BF16) for h in heads]
            dMG_t = [(jnp.where(col > row, -_dot_nt(vnb[h], dRb[h]), 0.0) * gam_t[h]).astype(BF16) for h in heads]
            E = [_dot_nt(vnb[h], dSb[h]) for h in heads]
            dKb = [_dot(dMG[h], K[h]) for h in heads]
            dc_all = jnp.zeros((C, LANES), F32)
            db_all = jnp.zeros((C, LANES), F32)
            for h in heads:
                dq_ref[rs, _hs(h)] = gcol[h] * _dot_nt(dO[h], Sb[h]) + _dot(dPG[h], K[h])
                dk_ref[rs, _hs(h)] = (_dot(dPG_t[h], Q[h]) + _dot(dMG_t[h], Kb[h]) + bcol[h] * dKb[h]
                                      - bg[h] * _dot_nt(dRb[h], Sb[h]) + dcol[h] * E[h])
                dv_ref[rs, _hs(h)] = bcol[h] * dR[h]
                dbeta = rsum(dKb[h] * K[h]) + rsum(dR[h] * (V[h] - gcol[h] * KS[h]))
                X = dP[h] * P[h] + dM[h] * M[h]
                ddel = rsum(K[h] * E[h]) * dcol[h]
                colsum = rsum(eye * jnp.sum(X, axis=0, keepdims=True))
                dc = (rsum(X) - colsum + gcol[h] * rsum(dO[h].astype(F32) * QS[h]) - bg[h] * rsum(dR[h] * KS[h])
                      - ddel)
                last = (jnp.sum(ddel, axis=0, keepdims=True)
                        + glast[h] * jnp.sum(rsum(dSn[h] * S[h]), axis=0, keepdims=True))
                dc_all = dc_all + jnp.where(lane == h, dc + jnp.where(rowl == C - 1, last, 0.0), 0.0)
                db_all = db_all + jnp.where(lane == N_HEAD + h, dbeta, 0.0)
            dgb_ref[rs, :] = _dotf(upper, dc_all) + db_all
            dSn = dS_new
        for h in heads:
            ds_ref[h] = dSn[h]
        dgn_ref[...] += dgn

    nb = nc // per
    cspec = lambda w, cb=0: pl.BlockSpec((per * C, w), lambda n: (nb - 1 - n, cb))
    hist = lambda a, b: pl.BlockSpec((per, N_HEAD, a, b), lambda n: (nb - 1 - n, 0, 0, 0))
    return pl.pallas_call(
        body,
        out_shape=(jax.ShapeDtypeStruct((T, BR_W), F32),) * 3 + (
            jax.ShapeDtypeStruct((T, LANES), F32), jax.ShapeDtypeStruct(into.shape, BF16),
            jax.ShapeDtypeStruct((1, D_HEAD), F32)),
        grid=(nb,),
        in_specs=[cspec(BR_W), cspec(BR_W), cspec(BR_W), cspec(LANES), cspec(BR_W, zcol), _full((1, D_HEAD)),
                  cspec(BR_W), hist(D_HEAD, D_HEAD), hist(C, C), cspec(BR_W), cspec(BR_W), HBM],
        out_specs=(cspec(BR_W), cspec(BR_W), cspec(BR_W), cspec(LANES), cspec(BR_W, zcol), _full((1, D_HEAD))),
        scratch_shapes=[pltpu.VMEM((N_HEAD, D_HEAD, D_HEAD), F32)], input_output_aliases={11: 4},
        name="gdn_chunk_bwd",
        compiler_params=_cp("arbitrary"))(q, k, v, gb, proj, gnorm, oraw, shist, tinv_all, vn_all, dog, into)


SB_COL = SB_OFF // BR_W
SB_SCALE = D_HEAD ** -0.5


def _sb_pre(proj, gq, gk):
    T = proj.shape[0]
    tm = min(TM, T)

    def body(xq_ref, xk_ref, xv_ref, gq_ref, gk_ref, q_ref, k_ref, v_ref):
        for h in range(N_HEAD):
            for x_ref, g_ref, ref, scale in ((xq_ref, gq_ref, q_ref, SB_SCALE), (xk_ref, gk_ref, k_ref, 1.0)):
                xh = x_ref[:, _hs(h)].astype(F32)
                r = lax.rsqrt(jnp.mean(xh * xh, axis=-1, keepdims=True) + EPS)
                ref[:, _hs(h)] = (xh * (r * scale) * g_ref[...]).astype(BF16)
        v_ref[...] = xv_ref[...]

    return pl.pallas_call(
        body, out_shape=(jax.ShapeDtypeStruct((T, BR_W), BF16),) * 3, grid=(T // tm,),
        in_specs=[_rowspec(tm, BR_W, SB_COL), _rowspec(tm, BR_W, SB_COL + 1), _rowspec(tm, BR_W, SB_COL + 2),
                  _full((1, D_HEAD)), _full((1, D_HEAD))],
        out_specs=(_rowspec(tm, BR_W),) * 3, name="sb_pre", compiler_params=_cp("parallel"))(proj, proj, proj, gq, gk)


def _sb_pre_bwd(proj, gq, gk, dq, dk, dv, into):
    T = proj.shape[0]
    tm = min(TM, T)

    def body(xq_ref, xk_ref, gq_ref, gk_ref, dq_ref, dk_ref, dv_ref, into_ref, dx_ref, dgq_ref, dgk_ref):
        i = pl.program_id(0)

        @pl.when(i == 0)
        def _():
            dgq_ref[...] = jnp.zeros_like(dgq_ref)
            dgk_ref[...] = jnp.zeros_like(dgk_ref)

        for off, x_ref, g_ref, d_ref, dg_ref, scale in ((0, xq_ref, gq_ref, dq_ref, dgq_ref, SB_SCALE),
                                                        (BR_W, xk_ref, gk_ref, dk_ref, dgk_ref, 1.0)):
            dg = jnp.zeros((1, D_HEAD), F32)
            for h in range(N_HEAD):
                xh = x_ref[:, _hs(h)].astype(F32)
                r = lax.rsqrt(jnp.mean(xh * xh, axis=-1, keepdims=True) + EPS)
                y = xh * r
                dn = d_ref[:, _hs(h)] * scale
                dg = dg + jnp.sum(dn * y, axis=0, keepdims=True)
                dy = dn * g_ref[...]
                dx_ref[:, off + h * D_HEAD:off + (h + 1) * D_HEAD] = (
                    r * (dy - y * jnp.mean(dy * y, axis=-1, keepdims=True))).astype(BF16)
            dg_ref[...] += dg
        dx_ref[:, 2 * BR_W:] = dv_ref[...].astype(BF16)

    return pl.pallas_call(
        body,
        out_shape=(jax.ShapeDtypeStruct(into.shape, BF16), jax.ShapeDtypeStruct((1, D_HEAD), F32),
                   jax.ShapeDtypeStruct((1, D_HEAD), F32)),
        grid=(T // tm,),
        in_specs=[_rowspec(tm, BR_W, SB_COL), _rowspec(tm, BR_W, SB_COL + 1), _full((1, D_HEAD)), _full((1, D_HEAD)),
                  _rowspec(tm, BR_W), _rowspec(tm, BR_W), _rowspec(tm, BR_W), HBM],
        out_specs=(_rowspec(tm, 3 * BR_W, SB_OFF // (3 * BR_W)), _full((1, D_HEAD)), _full((1, D_HEAD))),
        input_output_aliases={7: 0}, name="sb_pre_bwd",
        compiler_params=_cp("arbitrary"))(proj, proj, gq, gk, dq, dk, dv, into)


def _sb_pair(q, k, masked):
    z = _dot_nt(q, k)
    zc = jnp.minimum(z, 30.0)
    sp = jnp.log(1.0 + jnp.exp(zc)) + (z - zc)
    if not masked:
        return z, sp, None
    mask = _iota(z.shape, 1) < _iota(z.shape, 0)
    return z, jnp.where(mask, sp, 0.0), mask


def _sb_fwd(sq, sk, sv):
    T = sq.shape[0]
    blk = min(SB_BLK, T)
    w = min(SB_W, blk)
    nb, nsub = T // blk, blk // w

    def body(q_ref, k_ref, v_ref, o_ref, lt_ref, cut_ref, acc_ref, r_ref):
        head, qi = pl.program_id(0), pl.program_id(1)
        acc_ref[...] = jnp.zeros_like(acc_ref)
        r_ref[...] = jnp.zeros_like(r_ref)

        def block(kj, masked):
            rows = pl.ds(pl.multiple_of(kj * blk, blk), blk)
            z, sp, mask = _sb_pair(q_ref[...], k_ref[rows, :], masked)
            after = (_iota((w, w), 0) > _iota((w, w), 1)).astype(BF16)
            r = r_ref[...]
            acc = acc_ref[...]
            for sb in reversed(range(nsub)):
                cs = slice(sb * w, (sb + 1) * w)
                sps = sp[:, cs]
                a = jnp.exp(z[:, cs] - sps - _dot(sps, after) - r)
                if masked:
                    a = jnp.where(mask[:, cs], a, 0.0)
                acc = acc + _dot(a, v_ref[pl.ds(pl.multiple_of(kj * blk + sb * w, w), w), :])
                r = r + jnp.sum(sps, axis=-1, keepdims=True)
            acc_ref[...] = acc
            r_ref[...] = r
            return jnp.min(r) < SB_DEAD

        def further(state):
            kj, _ = state
            return kj - 1, block(kj, False)

        left, _ = lax.while_loop(lambda s: jnp.logical_and(s[0] >= 0, s[1]), further, (qi - 1, block(qi, True)))
        o_ref[...] = acc_ref[...].astype(BF16)
        lt_ref[0] = r_ref[...]
        cut_ref[head, qi] = (left + 1).astype(F32)

    qspec = pl.BlockSpec((blk, D_HEAD), lambda h, i: (i, h))
    whole = pl.BlockSpec((T, D_HEAD), lambda h, i: (0, h))
    return pl.pallas_call(
        body,
        out_shape=(jax.ShapeDtypeStruct((T, BR_W), BF16), jax.ShapeDtypeStruct((N_HEAD, T, 1), F32),
                   jax.ShapeDtypeStruct((N_HEAD, nb), F32)),
        grid=(N_HEAD, nb), in_specs=[qspec, whole, whole],
        out_specs=(qspec, pl.BlockSpec((1, blk, 1), lambda h, i: (h, i, 0)), pl.BlockSpec(memory_space=pltpu.SMEM)),
        scratch_shapes=[pltpu.VMEM((blk, D_HEAD), F32), pltpu.VMEM((blk, 1), F32)],
        name="sb_fwd", compiler_params=_cp("arbitrary", "arbitrary"))(sq, sk, sv)


def _sb_bwd(sq, sk, sv, ltot, do, cut):
    T = sq.shape[0]
    blk = min(SB_BLK, T)
    w = min(SB_W, blk)
    nb, nsub = T // blk, blk // w

    def body(q_ref, k_ref, v_ref, lt_ref, do_ref, cut_ref, dq_ref, dk_ref, dv_ref, acc_ref, p_ref, g_ref):
        head, qi = pl.program_id(0), pl.program_id(1)
        first = cut_ref[head, qi].astype(jnp.int32)

        @pl.when(qi == 0)
        def _():
            dk_ref[...] = jnp.zeros_like(dk_ref)
            dv_ref[...] = jnp.zeros_like(dv_ref)

        acc_ref[...] = jnp.zeros_like(acc_ref)
        p_ref[...] = lt_ref[0]
        g_ref[...] = jnp.zeros_like(g_ref)

        def block(kj, masked):
            base = pl.multiple_of(kj * blk, blk)
            z, sp, mask = _sb_pair(q_ref[...], k_ref[pl.ds(base, blk), :], masked)
            d_a = _dot_nt(do_ref[...], v_ref[pl.ds(base, blk), :])
            after = (_iota((w, w), 0) > _iota((w, w), 1)).astype(BF16)
            before = (_iota((w, w), 0) < _iota((w, w), 1)).astype(BF16)
            rest = p_ref[...]
            hg = g_ref[...]
            acc = acc_ref[...]
            for sb in range(nsub):
                cs = slice(sb * w, (sb + 1) * w)
                sps, zs = sp[:, cs], z[:, cs]
                rest = rest - jnp.sum(sps, axis=-1, keepdims=True)
                a = jnp.exp(zs - sps - _dot(sps, after) - rest)
                if masked:
                    a = jnp.where(mask[:, cs], a, 0.0)
                g = a * d_a[:, cs]
                sig = jnp.exp(zs - sps)
                dz = g - sig * (g + (hg + _dot(g, before)))
                if masked:
                    dz = jnp.where(mask[:, cs], dz, 0.0)
                dz = dz.astype(BF16)
                rows = pl.ds(pl.multiple_of(base + sb * w, w), w)
                dv_ref[rows, :] += _dot_tn(a, do_ref[...])
                dk_ref[rows, :] += _dot_tn(dz, q_ref[...])
                acc = acc + _dot(dz, k_ref[rows, :])
                hg = hg + jnp.sum(g, axis=-1, keepdims=True)
            acc_ref[...] = acc
            p_ref[...] = rest
            g_ref[...] = hg

        def step(kj, carry):
            block(kj, False)
            return carry

        lax.fori_loop(first, qi, step, 0)
        block(qi, True)
        dq_ref[...] = acc_ref[...]

    qspec = pl.BlockSpec((blk, D_HEAD), lambda h, i: (i, h))
    whole = pl.BlockSpec((T, D_HEAD), lambda h, i: (0, h))
    return pl.pallas_call(
        body, out_shape=(jax.ShapeDtypeStruct((T, BR_W), F32),) * 3, grid=(N_HEAD, nb),
        in_specs=[qspec, whole, whole, pl.BlockSpec((1, blk, 1), lambda h, i: (h, i, 0)), qspec,
                  pl.BlockSpec(memory_space=pltpu.SMEM)],
        out_specs=(qspec, whole, whole),
        scratch_shapes=[pltpu.VMEM((blk, D_HEAD), F32), pltpu.VMEM((blk, 1), F32), pltpu.VMEM((blk, 1), F32)],
        name="sb_bwd", compiler_params=_cp("arbitrary", "arbitrary"))(sq, sk, sv, ltot, do, cut)


def _mem_kv(mem, gm, w_kv, gk):
    def body(mem_ref, gm_ref, w_ref, gk_ref, mn_ref, kv_ref, kh_ref, vm_ref):
        mv = mem_ref[...]
        r = lax.rsqrt(jnp.mean(mv * mv, axis=-1, keepdims=True) + EPS)
        mn = (mv * r * gm_ref[...]).astype(BF16)
        mn_ref[...] = mn
        kv = lax.dot_general(mn, w_ref[...], _NN, preferred_element_type=F32)
        kv_ref[...] = kv
        for h in range(N_HEAD):
            kh = kv[:, _hs(h)]
            rk = lax.rsqrt(jnp.mean(kh * kh, axis=-1, keepdims=True) + EPS)
            kh_ref[:, _hs(h)] = (kh * rk * gk_ref[...]).astype(BF16)
        vm_ref[...] = kv[:, BR_W:].astype(BF16)

    return pl.pallas_call(
        body,
        out_shape=(jax.ShapeDtypeStruct((N_MEM, D_MODEL), BF16), jax.ShapeDtypeStruct((N_MEM, 2 * BR_W), F32),
                   jax.ShapeDtypeStruct((N_MEM, BR_W), BF16), jax.ShapeDtypeStruct((N_MEM, BR_W), BF16)),
        name="mem_kv", compiler_params=_cp())(mem, gm, w_kv, gk)


def _mem_q(x_ref, gq_ref, h):
    xh = x_ref[:, _hs(h)].astype(F32)
    r = lax.rsqrt(jnp.mean(xh * xh, axis=-1, keepdims=True) + EPS)
    return r, xh * r


def _mem_probs(qn, kh):
    s = _dot_nt(qn, kh) * (D_HEAD ** -0.5)
    e = jnp.exp(s - jnp.max(s, axis=-1, keepdims=True))
    return e / jnp.sum(e, axis=-1, keepdims=True)


def _mem_fwd(proj, kh, vm, gq):
    T = proj.shape[0]
    tm = min(TM, T)

    def body(x_ref, kh_ref, vm_ref, gq_ref, o_ref):
        for h in range(N_HEAD):
            _, y = _mem_q(x_ref, gq_ref, h)
            p = _mem_probs((y * gq_ref[...]).astype(BF16), kh_ref[:, _hs(h)])
            o_ref[:, _hs(h)] = _dot(p, vm_ref[:, _hs(h)]).astype(BF16)

    return pl.pallas_call(
        body, out_shape=jax.ShapeDtypeStruct((T, BR_W), BF16), grid=(T // tm,),
        in_specs=[_rowspec(tm, BR_W, MEMQ_OFF // BR_W), _full((N_MEM, BR_W)), _full((N_MEM, BR_W)),
                  _full((1, D_HEAD))],
        out_specs=_rowspec(tm, BR_W), name="mem_fwd", compiler_params=_cp("parallel"))(proj, kh, vm, gq)


def _mem_bwd(proj, kh, vm, gq, do, into):
    T = proj.shape[0]
    tm = min(TM, T)

    def body(x_ref, kh_ref, vm_ref, gq_ref, do_ref, into_ref, dx_ref, dkh_ref, dvm_ref, dgq_ref):
        i = pl.program_id(0)

        @pl.when(i == 0)
        def _():
            dkh_ref[...] = jnp.zeros_like(dkh_ref)
            dvm_ref[...] = jnp.zeros_like(dvm_ref)
            dgq_ref[...] = jnp.zeros_like(dgq_ref)

        dg = jnp.zeros((1, D_HEAD), F32)
        for h in range(N_HEAD):
            r, y = _mem_q(x_ref, gq_ref, h)
            qn = (y * gq_ref[...]).astype(BF16)
            p = _mem_probs(qn, kh_ref[:, _hs(h)])
            dov = do_ref[:, _hs(h)]
            dp = _dot_nt(dov, vm_ref[:, _hs(h)])
            ds = p * (dp - jnp.sum(dp * p, axis=-1, keepdims=True)) * (D_HEAD ** -0.5)
            dqn = _dot(ds, kh_ref[:, _hs(h)])
            dkh_ref[:, _hs(h)] += _dot_tn(ds, qn)
            dvm_ref[:, _hs(h)] += _dot_tn(p, dov)
            dg = dg + jnp.sum(dqn * y, axis=0, keepdims=True)
            dy = dqn * gq_ref[...]
            dx_ref[:, _hs(h)] = (r * (dy - y * jnp.mean(dy * y, axis=-1, keepdims=True))).astype(BF16)
        dgq_ref[...] += dg

    return pl.pallas_call(
        body,
        out_shape=(jax.ShapeDtypeStruct(into.shape, BF16), jax.ShapeDtypeStruct((N_MEM, BR_W), F32),
                   jax.ShapeDtypeStruct((N_MEM, BR_W), F32), jax.ShapeDtypeStruct((1, D_HEAD), F32)),
        grid=(T // tm,),
        in_specs=[_rowspec(tm, BR_W, MEMQ_OFF // BR_W), _full((N_MEM, BR_W)), _full((N_MEM, BR_W)),
                  _full((1, D_HEAD)), _rowspec(tm, BR_W), HBM],
        out_specs=(_rowspec(tm, BR_W, MEMQ_OFF // BR_W), _full((N_MEM, BR_W)), _full((N_MEM, BR_W)),
                   _full((1, D_HEAD))),
        input_output_aliases={5: 0}, name="mem_bwd", compiler_params=_cp("arbitrary"))(proj, kh, vm, gq, do, into)


def _mem_kv_bwd(mem, gm, w_kv, gk, kv, mn, dkh, dvm):
    def body(mem_ref, gm_ref, w_ref, gk_ref, kv_ref, mn_ref, dkh_ref, dvm_ref, dw_ref, dgm_ref, dgk_ref, dkv_ref):
        dgk = jnp.zeros((1, D_HEAD), F32)
        for h in range(N_HEAD):
            kh = kv_ref[:, _hs(h)]
            r = lax.rsqrt(jnp.mean(kh * kh, axis=-1, keepdims=True) + EPS)
            y = kh * r
            dn = dkh_ref[:, _hs(h)]
            dgk = dgk + jnp.sum(dn * y, axis=0, keepdims=True)
            dy = dn * gk_ref[...]
            dkv_ref[:, _hs(h)] = (r * (dy - y * jnp.mean(dy * y, axis=-1, keepdims=True))).astype(BF16)
        dkv_ref[:, BR_W:] = dvm_ref[...].astype(BF16)
        dgk_ref[...] = dgk
        dkv = dkv_ref[...]
        dw_ref[...] = lax.dot_general(mn_ref[...], dkv, _TN, preferred_element_type=F32)
        dmn = lax.dot_general(dkv, w_ref[...], _NT, preferred_element_type=F32)
        mv = mem_ref[...]
        memn = mv * lax.rsqrt(jnp.mean(mv * mv, axis=-1, keepdims=True) + EPS)
        dgm_ref[...] = jnp.sum(dmn * memn, axis=0, keepdims=True)

    return pl.pallas_call(
        body,
        out_shape=(jax.ShapeDtypeStruct((D_MODEL, 2 * BR_W), F32), jax.ShapeDtypeStruct((1, D_MODEL), F32),
                   jax.ShapeDtypeStruct((1, D_HEAD), F32)),
        scratch_shapes=[pltpu.VMEM((N_MEM, 2 * BR_W), BF16)], name="mem_kv_bwd",
        compiler_params=_cp())(mem, gm, w_kv, gk, kv, mn, dkh, dvm)


def _merge_fwd(og, osb, om, proj, wg, ws, wm):
    T = og.shape[0]
    tm = min(TM, T)

    def body(og_ref, os_ref, om_ref, g0, g1, g2, wg_ref, ws_ref, wm_ref, mix_ref, yg_ref, ys_ref, ym_ref):
        mix = jnp.zeros((tm, D_MODEL), F32)
        for o_ref, gl_ref, w_ref, y_ref in ((og_ref, g0, wg_ref, yg_ref), (os_ref, g1, ws_ref, ys_ref),
                                            (om_ref, g2, wm_ref, ym_ref)):
            y = lax.dot_general(o_ref[...], w_ref[...], _NN, preferred_element_type=F32)
            y_ref[...] = y.astype(BF16)
            mix = mix + _sigmoid(gl_ref[...].astype(F32)) * y
        mix_ref[...] = mix.astype(BF16)

    br = _rowspec(tm, BR_W)
    wspec = _full((BR_W, D_MODEL))
    out = _rowspec(tm, D_MODEL)
    gates = [_rowspec(tm, D_MODEL, GATE_COL + b) for b in range(3)]
    return pl.pallas_call(
        body, out_shape=(jax.ShapeDtypeStruct((T, D_MODEL), BF16),) * 4, grid=(T // tm,),
        in_specs=[br, br, br, *gates, wspec, wspec, wspec],
        out_specs=(out,) * 4, name="merge_fwd",
        compiler_params=_cp("parallel"))(og, osb, om, proj, proj, proj, wg, ws, wm)


def _merge_bwd(dmix, proj, ys, os_, ws):
    T = dmix.shape[0]
    tm = min(TM, T)

    def body(dmix_ref, g0, g1, g2, y0, y1, y2, o0, o1, o2, w0, w1, w2, dgl_ref, do0, do1, do2, dw0, dw1, dw2):
        i = pl.program_id(0)
        dm = dmix_ref[...].astype(F32)
        for b, (gl_ref, y_ref, o_ref, w_ref, do_ref, dw_ref) in enumerate((
                (g0, y0, o0, w0, do0, dw0), (g1, y1, o1, w1, do1, dw1), (g2, y2, o2, w2, do2, dw2))):
            gate = _sigmoid(gl_ref[...].astype(F32))
            dgl_ref[:, b * D_MODEL:(b + 1) * D_MODEL] = (dm * y_ref[...].astype(F32) * gate * (1.0 - gate)).astype(BF16)
            dy = (gate * dm).astype(BF16)
            do_ref[...] = lax.dot_general(dy, w_ref[...], _NT, preferred_element_type=F32).astype(BF16)
            _accum(dw_ref, i == 0, lax.dot_general(dy, o_ref[...], _TN, preferred_element_type=F32))

    br = _rowspec(tm, BR_W)
    wide = _rowspec(tm, D_MODEL)
    wspec = _full((BR_W, D_MODEL))
    wtspec = _full((D_MODEL, BR_W))
    gates = [_rowspec(tm, D_MODEL, GATE_COL + b) for b in range(3)]
    return pl.pallas_call(
        body,
        out_shape=(jax.ShapeDtypeStruct((T, PROJ_W), BF16),) + (jax.ShapeDtypeStruct((T, BR_W), BF16),) * 3
        + (jax.ShapeDtypeStruct((D_MODEL, BR_W), F32),) * 3,
        grid=(T // tm,),
        in_specs=[wide, *gates, wide, wide, wide, br, br, br, wspec, wspec, wspec],
        out_specs=(_rowspec(tm, 3 * D_MODEL, GATE_OFF // (3 * D_MODEL)), br, br, br, wtspec, wtspec, wtspec),
        name="merge_bwd",
        compiler_params=_cp("arbitrary"))(dmix, proj, proj, proj, *ys, *os_, *ws)


def _loss(y, tgt):
    T, dm = y.shape
    tm = min(TM, T)

    def body(y_ref, t_ref, dy_ref, dyb_ref, sq_ref):
        err = y_ref[...] - t_ref[...]
        dy = err * (1.0 / dm)
        dy_ref[...] = dy
        dyb_ref[...] = dy.astype(BF16)
        _accum(sq_ref, pl.program_id(0) == 0, jnp.sum(err * err, axis=0, keepdims=True))

    return pl.pallas_call(
        body,
        out_shape=(jax.ShapeDtypeStruct((T, dm), F32), jax.ShapeDtypeStruct((T, dm), BF16),
                   jax.ShapeDtypeStruct((1, dm), F32)),
        grid=(T // tm,), in_specs=[_rowspec(tm, dm), _rowspec(tm, dm)],
        out_specs=(_rowspec(tm, dm), _rowspec(tm, dm), _full((1, dm))), name="loss",
        compiler_params=_cp("arbitrary"))(y, tgt)


def _split_w_in(slabs):
    width = slabs[0].shape[1]

    def cols(lo, hi):
        return [s[:, max(lo - j * width, 0):min(hi - j * width, width)] for j, s in enumerate(slabs)
                if lo < (j + 1) * width and hi > j * width]

    main = [c for piece in (IN_GATE, IN_QKV, IN_SB, IN_Z, IN_MEMQ) for c in cols(*piece)]
    ab = jnp.concatenate(cols(*IN_AB), axis=1)
    return jnp.concatenate(main, axis=1), jnp.pad(ab, ((0, 0), (0, LANES - ab.shape[1])))


def _local_step(x, mem, tgt, W, P, dist=None):
    w_main, w_ab = W["w_main"], W["w_ab"]
    avec = jnp.pad(jnp.concatenate([P["a_log"], P["dt_bias"]], axis=0), ((0, 0), (0, LANES - N_HEAD)))

    h = _rms_fwd(x, P["norm1_g"], "rms1")
    if dist is None:
        proj = _mm(h, w_main, "nn", BF16, "in_proj")
    else:
        proj, gathered = _mm(h, w_main, "nn", BF16, "in_proj", comm=dist.gather_rest())
        rest, conv_w = dist.weights_from(gathered)
        W, P = {**W, **rest}, {**P, "conv_w": conv_w}
    wbr = (W["w_br_gdn"], W["w_br_sb"], W["w_br_mem"])
    ab = _mm(h, w_ab, "nn", F32, "in_proj_ab")
    q, k, v, gb = _gdn_pre(proj, P["conv_w"], ab, avec)
    tinv, tinv_t = _gdn_inv(k, gb)
    og, oraw, shist, vn = _gdn_fwd(q, k, v, gb, proj, P["gdn_norm_g"], tinv)
    sq, sk, sv = _sb_pre(proj, P["sb_q_norm_g"], P["sb_k_norm_g"])
    osb, ltot, cut = _sb_fwd(sq, sk, sv)
    mn, kv, kh, vm = _mem_kv(mem, P["mem_norm_g"], W["w_mem_kv"], P["mem_k_norm_g"])
    om = _mem_fwd(proj, kh, vm, P["mem_q_norm_g"])
    mix, yg, ys, ym = _merge_fwd(og, osb, om, proj, *wbr)
    x1 = _mm(mix, W["w_o"], "nn", F32, "out_proj", extra=x, epi=_epi_add)
    h2 = _rms_fwd(x1, P["norm2_g"], "rms2")
    u = _mm(h2, W["w_up"], "nn", BF16, "mlp_up")
    y = _mm(u, W["w_down"], "nn", F32, "mlp_down", a_fn=_relu2, extra=x1, epi=_epi_add)
    dy, dyb, sq_err = _loss(y, tgt)

    G = {}
    du = _mm(dyb, W["w_down"], "nt", BF16, "d_mlp_act", extra=u, epi=_epi_drelu2)
    G["w_down"] = _mm(u, dyb, "tn", F32, "dw_down", a_fn=_relu2)
    G["w_up"] = _mm(du, h2, "tn", F32, "dw_up")
    dh2 = _mm(du, W["w_up"], "nt", F32, "d_h2")
    dx1, dx1b, G["norm2_g"] = _rms_bwd(dh2, x1, P["norm2_g"], dy, "rms2_bwd")
    dmix = _mm(dx1b, W["w_o"], "nt", BF16, "d_mix")
    G["w_o"] = _mm(mix, dx1b, "tn", F32, "dw_o")
    dproj, dog, dosb, dom, G["w_br_gdn"], G["w_br_sb"], G["w_br_mem"] = _merge_bwd(
        dmix, proj, (yg, ys, ym), (og, osb, om), wbr)
    dq, dk, dv, dgb, dproj, G["gdn_norm_g"] = _gdn_bwd(q, k, v, gb, proj, P["gdn_norm_g"], oraw, shist, tinv_t, vn, dog,
                                                      dproj)
    dxc, dab, G["conv_w"], dav = _gdn_pre_bwd(proj, P["conv_w"], ab, avec, dq, dk, dv, dgb)
    dproj = _conv_bwd(dxc, P["conv_w"], dproj)
    G["a_log"], G["dt_bias"] = dav[0:1, :N_HEAD], dav[1:2, :N_HEAD]
    dsq, dsk, dsv = _sb_bwd(sq, sk, sv, ltot, dosb, cut)
    dproj, G["sb_q_norm_g"], G["sb_k_norm_g"] = _sb_pre_bwd(proj, P["sb_q_norm_g"], P["sb_k_norm_g"], dsq, dsk, dsv,
                                                            dproj)
    dproj, dkh, dvm, G["mem_q_norm_g"] = _mem_bwd(proj, kh, vm, P["mem_q_norm_g"], dom, dproj)
    G["w_mem_kv"], G["mem_norm_g"], G["mem_k_norm_g"] = _mem_kv_bwd(
        mem, P["mem_norm_g"], W["w_mem_kv"], P["mem_k_norm_g"], kv, mn, dkh, dvm)
    dw_ab = _mm(dab, h, "tn", F32, "dw_in_ab")
    if dist is None:
        dw_main = _mm(dproj, h, "tn", F32, "dw_in")
    else:
        early = [n for n, _, _ in BIG if n != "w_in"]
        dw_main, landed = _mm(dproj, h, "tn", F32, "dw_in", comm=dist.scatter(G, early, "early"))
        dist.collect(early, landed)
    G["w_in"] = jnp.concatenate([dw_main[QKV_OFF:SB_OFF], dw_main[Z_OFF:MEMQ_OFF], dw_ab[:8], dw_main[SB_OFF:Z_OFF],
                                 dw_main[MEMQ_OFF:], dw_main[:QKV_OFF]], axis=0)
    if dist is None:
        dh = _mm(dproj, w_main, "nt", F32, "d_h")
    else:
        dh, landed = _mm(dproj, w_main, "nt", F32, "d_h", comm=dist.scatter(G, ["w_in"], "late"))
        dist.collect(["w_in"], landed)
    dh = _mm(dab, w_ab, "nt", F32, "d_h_ab", extra=dh, epi=_epi_add)
    dx, _, G["norm1_g"] = _rms_bwd(dh, x, P["norm1_g"], dx1, "rms1_bwd")
    return sq_err, dx, G


def _comm(name, ins, out_shapes, plan):
    n_in, n_out = len(ins), len(out_shapes)
    probe = plan([None] * n_in, [None] * n_out, 0, 0, 0, dry=True)
    n_copy = probe

    def body(*refs):
        in_refs, out_refs = refs[:n_in], refs[n_in:n_in + n_out]
        send_sems, recv_sems = refs[n_in + n_out:]
        x, y, c = lax.axis_index("x"), lax.axis_index("y"), lax.axis_index("c")
        copies = []
        for k, (src, dst, dev) in enumerate(plan(in_refs, out_refs, x, y, c, dry=False)):
            if dev is None:
                cp = pltpu.make_async_copy(src, dst, send_sems.at[k])
            else:
                cp = pltpu.make_async_remote_copy(src_ref=src, dst_ref=dst, send_sem=send_sems.at[k],
                                                  recv_sem=recv_sems.at[k], device_id=dev, device_id_type=MESH)
            cp.start()
            copies.append(cp)
        for cp in copies:
            cp.wait()

    return pl.pallas_call(
        body, out_shape=tuple(out_shapes), in_specs=[HBM] * n_in, out_specs=tuple([HBM] * n_out),
        scratch_shapes=[pltpu.SemaphoreType.DMA((n_copy,)), pltpu.SemaphoreType.DMA((n_copy,))], name=name)(*ins)


def _other_chips(x, y):
    return ((1 - x, y), (x, 1 - y), (1 - x, 1 - y))


def _gather_plan(parts, direct=()):
    n, every = len(parts), list(parts) + list(direct)

    def copies(ins, outs, send, recv, scratch):
        x, y, c = lax.axis_index("x"), lax.axis_index("y"), lax.axis_index("c")
        me = 2 * x + y
        chips = _other_chips(x, y)
        local_sems, staged = scratch[0], scratch[1:]

        def remote(src, dst, k, dev):
            return pltpu.make_async_remote_copy(src_ref=src, dst_ref=dst, send_sem=send.at[k], recv_sem=recv.at[k],
                                                device_id=dev, device_id_type=MESH)

        def half(p, ci):
            hr = ins[p].shape[0] // 2
            return pl.ds(pl.multiple_of(ci * hr, 16), hr)

        sent = [remote(ins[p].at[half(p, c)], outs[p].at[me, half(p, c)], 6 * p + f, (px, py, c))
                for p in range(n) for f, (px, py) in enumerate(chips)]
        sent += [remote(ins[p], outs[p].at[me], 6 * n + 3 * (p - n) + f, (px, py, c))
                 for p in range(n, len(every)) for f, (px, py) in enumerate(chips)]
        landed = [outs[p].at[2 * px + py, half(p, c)] for p in range(n) for px, py in chips]
        passed = [remote(landed[3 * p + f], landed[3 * p + f], 6 * p + 3 + f, (x, y, 1 - c))
                  for p in range(n) for f in range(3)]
        loads = [pltpu.make_async_copy(ins[p], staged[p], local_sems.at[2 * p]) for p in range(len(every))]
        stores = [pltpu.make_async_copy(staged[p], outs[p].at[me], local_sems.at[2 * p + 1]) for p in range(len(every))]
        return sent, passed, loads, stores

    def start(*refs):
        sent, _, loads, _ = copies(*refs)
        for cp in loads + sent:
            cp.start()

    def mid(*refs):
        sent, passed, loads, stores = copies(*refs)
        for ld, st in zip(loads, stores):
            ld.wait()
            st.start()
        for p in range(n):
            for f in range(3):
                sent[3 * p + f].wait_recv()
                passed[3 * p + f].start()

    def finish(*refs):
        sent, passed, _, stores = copies(*refs)
        for cp in sent[:3 * n]:
            cp.wait_send()
        for cp in passed + sent[3 * n:] + stores:
            cp.wait()

    return _Hosted(every, [jax.ShapeDtypeStruct((4,) + p.shape, p.dtype) for p in every], 6 * n + 3 * len(direct),
                   start, finish, mid,
                   [pltpu.SemaphoreType.DMA((2 * len(every),))] + [pltpu.VMEM(p.shape, p.dtype) for p in every])


def _scatter_plan(pairs):
    def copies(ins, outs, send, recv, scratch):
        x, y, c = lax.axis_index("x"), lax.axis_index("y"), lax.axis_index("c")
        me = 2 * x + y
        return [pltpu.make_async_remote_copy(src_ref=src.at[2 * px + py], dst_ref=dst.at[me], send_sem=send.at[3 * p + f],
                                             recv_sem=recv.at[3 * p + f], device_id=(px, py, c), device_id_type=MESH)
                for p, (src, dst) in enumerate(zip(ins, outs)) for f, (px, py) in enumerate(_other_chips(x, y))]

    def start(*refs):
        for cp in copies(*refs):
            cp.start()

    def finish(*refs):
        for cp in copies(*refs):
            cp.wait()

    return _Hosted(pairs, [jax.ShapeDtypeStruct(a.shape, a.dtype) for a in pairs], 3 * len(pairs), start, finish)


def _run_hosted(comm, name):
    n_in, n_out = len(comm.ins), len(comm.out_shapes)

    def body(*refs):
        args = (refs[:n_in], refs[n_in:n_in + n_out], refs[n_in + n_out], refs[n_in + n_out + 1], refs[n_in + n_out + 2:])
        comm.start(*args)
        if comm.mid is not None:
            comm.mid(*args)
        comm.finish(*args)

    sems = [pltpu.SemaphoreType.DMA((comm.n_sems,)), pltpu.SemaphoreType.DMA((comm.n_sems,))]
    return list(pl.pallas_call(
        body, out_shape=tuple(comm.out_shapes), in_specs=[HBM] * n_in, out_specs=tuple([HBM] * n_out),
        scratch_shapes=sems + comm.scratch, name=name, compiler_params=_cp())(*comm.ins))


def _swap_halves(slabs, name):
    n = len(slabs)

    def body(*refs):
        ins, outs = refs[:n], refs[n:2 * n]
        send, recv = refs[2 * n:]
        x, y, c = lax.axis_index("x"), lax.axis_index("y"), lax.axis_index("c")
        other = (x, y, 1 - c)
        for p in range(n):
            for j in range(4):
                pltpu.make_async_remote_copy(src_ref=ins[p].at[j, 1 - c], dst_ref=outs[p].at[j], send_sem=send.at[p],
                                             recv_sem=recv.at[p], device_id=other, device_id_type=MESH).start()
        for p in range(n):
            pltpu.make_async_remote_copy(src_ref=outs[p], dst_ref=outs[p], send_sem=send.at[p], recv_sem=recv.at[p],
                                         device_id=other, device_id_type=MESH).wait()

    shapes = [jax.ShapeDtypeStruct((4,) + s.shape[2:], s.dtype) for s in slabs]
    return pl.pallas_call(
        body, out_shape=tuple(shapes), in_specs=[HBM] * n, out_specs=tuple([HBM] * n),
        scratch_shapes=[pltpu.SemaphoreType.DMA((n,)), pltpu.SemaphoreType.DMA((n,))], name=name)(*slabs)


def _join_halves(both):
    n = len(both)

    def body(*refs):
        bufs = refs[n:2 * n]
        send, recv = refs[2 * n:]
        x, y, c = lax.axis_index("x"), lax.axis_index("y"), lax.axis_index("c")
        copies = []
        for p in range(n):
            cp = pltpu.make_async_remote_copy(src_ref=bufs[p].at[c], dst_ref=bufs[p].at[c], send_sem=send.at[p],
                                              recv_sem=recv.at[p], device_id=(x, y, 1 - c), device_id_type=MESH)
            cp.start()
            copies.append(cp)
        for cp in copies:
            cp.wait()

    return pl.pallas_call(
        body, out_shape=tuple(jax.ShapeDtypeStruct(a.shape, a.dtype) for a in both), in_specs=[HBM] * n,
        out_specs=tuple([HBM] * n), input_output_aliases={p: p for p in range(n)},
        scratch_shapes=[pltpu.SemaphoreType.DMA((n,)), pltpu.SemaphoreType.DMA((n,))], name="grad_join_cores")(*both)


def _gather_all(a, name):
    def plan(ins, outs, x, y, c, dry):
        if dry:
            return 8
        me = 4 * x + 2 * y + c
        copies = [(ins[0], outs[0].at[me], None)]
        for f in range(1, 8):
            peer = (1 - x if f & 4 else x, 1 - y if f & 2 else y, 1 - c if f & 1 else c)
            copies.append((ins[0], outs[0].at[me], peer))
        return copies

    return _comm(name, [a], [jax.ShapeDtypeStruct((8,) + a.shape, a.dtype)], plan)[0]


def _sum_slots(a, name, extra=None):
    n, R, _ = a.shape
    rb = min(ROW_BLK, R)

    def body(*refs):
        a_ref, o_ref = refs[0], refs[-1]
        acc = a_ref[0]
        for s in range(1, n):
            acc = acc + a_ref[s]
        if extra is not None:
            acc = acc + refs[1][...]
        o_ref[...] = acc

    ins = [a] + ([extra] if extra is not None else [])
    in_specs = [pl.BlockSpec((n, rb, LANES), lambda i: (0, i, 0))] + ([_rowspec(rb, LANES)] if extra is not None else [])
    return pl.pallas_call(
        body, out_shape=jax.ShapeDtypeStruct((R, LANES), F32), grid=(R // rb,), in_specs=in_specs,
        out_specs=_rowspec(rb, LANES), name=name, compiler_params=_cp("parallel"))(*ins)


def _pair_sum(slab, theirs, core, name):
    _, _, hr, C = slab.shape

    def body(c_ref, a_ref, b_ref, o_ref):
        o_ref[...] = (a_ref[...] + b_ref[...]).astype(BF16)

    return pl.pallas_call(
        body, out_shape=jax.ShapeDtypeStruct((4, hr, C), BF16),
        grid_spec=pltpu.PrefetchScalarGridSpec(
            num_scalar_prefetch=1, grid=(4,),
            in_specs=[pl.BlockSpec((None, None, hr, C), lambda j, c_ref: (j, c_ref[0], 0, 0)),
                      pl.BlockSpec((None, hr, C), lambda j, c_ref: (j, 0, 0))],
            out_specs=pl.BlockSpec((None, hr, C), lambda j, c_ref: (j, 0, 0))),
        name=name, compiler_params=_cp("parallel"))(core, slab, theirs)


def _chip_sum(recv, pairs, where, name):
    _, hr, C = recv.shape

    def body(w_ref, r_ref, p_ref, o_ref):
        me = w_ref[0]
        o_ref[...] = jnp.zeros_like(o_ref)
        for s in range(4):
            @pl.when(me == s)
            def _():
                o_ref[...] += p_ref[...].astype(F32)

            @pl.when(me != s)
            def _():
                o_ref[...] += r_ref[s].astype(F32)

    return pl.pallas_call(
        body, out_shape=jax.ShapeDtypeStruct((2, hr, C), F32),
        grid_spec=pltpu.PrefetchScalarGridSpec(
            num_scalar_prefetch=1, grid=(1,),
            in_specs=[pl.BlockSpec((4, hr, C), lambda i, w_ref: (0, 0, 0)),
                      pl.BlockSpec((None, hr, C), lambda i, w_ref: (w_ref[0], 0, 0))],
            out_specs=pl.BlockSpec((None, hr, C), lambda i, w_ref: (w_ref[1], 0, 0))),
        name=name, compiler_params=_cp("arbitrary"))(where, recv, pairs)


def _adamw(w, g, m, v, name):
    R, C = w.shape
    rb = min(ADAM_ROWS, R)
    c1 = 1.0 - ADAM_B1 ** ADAM_STEP
    c2 = 1.0 - ADAM_B2 ** ADAM_STEP

    def body(w_ref, g_ref, m_ref, v_ref, d_ref, nm_ref, nv_ref):
        gv = g_ref[...]
        nm = ADAM_B1 * m_ref[...] + (1.0 - ADAM_B1) * gv
        nv = ADAM_B2 * v_ref[...] + (1.0 - ADAM_B2) * (gv * gv)
        d_ref[...] = -ADAM_LR * ((nm / c1) / (jnp.sqrt(nv / c2) + ADAM_EPS) + ADAM_WD * w_ref[...])
        nm_ref[...] = nm
        nv_ref[...] = nv

    spec = _rowspec(rb, C)
    return pl.pallas_call(
        body, out_shape=(jax.ShapeDtypeStruct((R, C), F32),) * 3, grid=(R // rb,), in_specs=[spec] * 4,
        out_specs=(spec,) * 3, name=name, compiler_params=_cp("parallel"))(w, g, m, v)


class _Dist:
    def __init__(self, shards):
        self.shards = shards
        self.chip = 2 * lax.axis_index("x") + lax.axis_index("y")
        self.where = jnp.stack([self.chip, lax.axis_index("c")]).astype(jnp.int32)
        self.pairs, self.landed = {}, {}

    @staticmethod
    def _unshard(name, blk):
        _, (r, cc), axis = next(b for b in BIG if b[0] == name)
        return blk.reshape(4 * r, cc) if axis == 0 else blk.transpose(1, 0, 2).reshape(r, 4 * cc)

    def gather_first(self):
        got = _run_hosted(_gather_plan([self.shards["w_in"].astype(BF16)]), "gather_w_in")[0]
        return _split_w_in([got[j] for j in range(4)])

    def gather_rest(self):
        rest = [self.shards[n].astype(BF16) for n, _, _ in BIG if n != "w_in"]
        return _gather_plan(rest, [self.shards["conv_w"]])

    def weights_from(self, gathered):
        names = [n for n, _, _ in BIG if n != "w_in"]
        conv = gathered[-1]
        taps, width = conv.shape[1:]
        return ({n: self._unshard(n, g) for n, g in zip(names, gathered)},
                conv.transpose(1, 0, 2).reshape(taps, 4 * width))

    def scatter(self, G, names, tag):
        slabs = []
        for name, (r, cc), axis in BIG:
            if name not in names:
                continue
            g = G[name]
            if axis == 0:
                slabs.append(g.reshape(4, 2, r // 2, cc))
            else:
                slabs.append(g.reshape(4, 2, cc // 2, r))
        theirs = _swap_halves(slabs, "grad_swap_cores_" + tag)
        pairs = [_pair_sum(s, t, self.where[1:], "pair_sum_" + n) for s, t, n in zip(slabs, theirs, names)]
        self.pairs.update(zip(names, pairs))
        return _scatter_plan(pairs)

    def collect(self, names, landed):
        self.landed.update(zip(names, landed))

    def finish(self):
        names = [n for n, _, _ in BIG]
        halves = [_chip_sum(self.landed[n], self.pairs[n], self.where, "chip_sum_" + n) for n in names]
        out = {}
        for (name, (r, cc), axis), both in zip(BIG, _join_halves(halves)):
            full = both.reshape(-1, both.shape[-1])
            out[name] = full if axis == 0 else full.T
        return out


def _pack_rows(parts, rows, dtype):
    flat = jnp.concatenate([p.reshape(-1).astype(dtype) for p in parts])
    return jnp.pad(flat, (0, rows * LANES - flat.shape[0])).reshape(rows, LANES)


def _small_rows(n):
    return max(n // LANES, 1)


def _pack_small(vals):
    rows = []
    for name, n in SMALL:
        r = _small_rows(n)
        rows.append(jnp.pad(vals[name].reshape(-1), (0, r * LANES - n)).reshape(r, LANES))
    flat = jnp.concatenate(rows, axis=0)
    return jnp.pad(flat, ((0, SMALL_ROWS - flat.shape[0]), (0, 0)))


def _unpack_small(pack):
    out, r0 = {}, 0
    for name, n in SMALL:
        r = _small_rows(n)
        out[name] = pack[r0:r0 + r].reshape(-1)[:n]
        r0 += r
    return out


def kernel(x, mem, norm1_g, w_in, conv_w, a_log, dt_bias, gdn_norm_g, sb_q_norm_g, sb_k_norm_g, mem_norm_g, w_mem_kv, mem_q_norm_g, mem_k_norm_g, w_br_gdn, w_br_sb, w_br_mem, w_o, norm2_g, w_up, w_down, loss_target, m_norm1_g, m_w_in, m_conv_w, m_a_log, m_dt_bias, m_gdn_norm_g, m_sb_q_norm_g, m_sb_k_norm_g, m_mem_norm_g, m_w_mem_kv, m_mem_q_norm_g, m_mem_k_norm_g, m_w_br_gdn, m_w_br_sb, m_w_br_mem, m_w_o, m_norm2_g, m_w_up, m_w_down, v_norm1_g, v_w_in, v_conv_w, v_a_log, v_dt_bias, v_gdn_norm_g, v_sb_q_norm_g, v_sb_k_norm_g, v_mem_norm_g, v_w_mem_kv, v_mem_q_norm_g, v_mem_k_norm_g, v_w_br_gdn, v_w_br_sb, v_w_br_mem, v_w_o, v_norm2_g, v_w_up, v_w_down):
    wd = dict(norm1_g=norm1_g, w_in=w_in, conv_w=conv_w, a_log=a_log, dt_bias=dt_bias, gdn_norm_g=gdn_norm_g,
              sb_q_norm_g=sb_q_norm_g, sb_k_norm_g=sb_k_norm_g, mem_norm_g=mem_norm_g, w_mem_kv=w_mem_kv,
              mem_q_norm_g=mem_q_norm_g, mem_k_norm_g=mem_k_norm_g, w_br_gdn=w_br_gdn, w_br_sb=w_br_sb,
              w_br_mem=w_br_mem, w_o=w_o, norm2_g=norm2_g, w_up=w_up, w_down=w_down)
    md = dict(norm1_g=m_norm1_g, w_in=m_w_in, conv_w=m_conv_w, a_log=m_a_log, dt_bias=m_dt_bias,
              gdn_norm_g=m_gdn_norm_g, sb_q_norm_g=m_sb_q_norm_g, sb_k_norm_g=m_sb_k_norm_g,
              mem_norm_g=m_mem_norm_g, w_mem_kv=m_w_mem_kv, mem_q_norm_g=m_mem_q_norm_g,
              mem_k_norm_g=m_mem_k_norm_g, w_br_gdn=m_w_br_gdn, w_br_sb=m_w_br_sb, w_br_mem=m_w_br_mem, w_o=m_w_o,
              norm2_g=m_norm2_g, w_up=m_w_up, w_down=m_w_down)
    vd = dict(norm1_g=v_norm1_g, w_in=v_w_in, conv_w=v_conv_w, a_log=v_a_log, dt_bias=v_dt_bias,
              gdn_norm_g=v_gdn_norm_g, sb_q_norm_g=v_sb_q_norm_g, sb_k_norm_g=v_sb_k_norm_g,
              mem_norm_g=v_mem_norm_g, w_mem_kv=v_w_mem_kv, mem_q_norm_g=v_mem_q_norm_g,
              mem_k_norm_g=v_mem_k_norm_g, w_br_gdn=v_w_br_gdn, w_br_sb=v_w_br_sb, w_br_mem=v_w_br_mem, w_o=v_w_o,
              norm2_g=v_norm2_g, w_up=v_w_up, w_down=v_w_down)
    wd, md, vd = ({n: a[0] for n, a in d.items()} for d in (wd, md, vd))
    chip = 2 * lax.axis_index("x") + lax.axis_index("y")
    conv_shard = wd["conv_w"].shape

    dist = _Dist(wd)
    W = dict(zip(("w_main", "w_ab"), dist.gather_first()))
    P = {n: wd[n].reshape(1, -1) for n, _ in SMALL}

    sq_err, grad_x, G = _local_step(x[0], mem[0], loss_target[0], W, P, dist)
    loss = lax.psum(0.5 / D_MODEL * jnp.sum(sq_err), ("x", "y", "c"))

    g_big = dist.finish()

    spack = jnp.concatenate([_pack_small(G), G["conv_w"].reshape(CONV_ROWS, LANES)], axis=0)
    g_small = _sum_slots(_gather_all(spack, "gather_small_grads"), "small_grad_sum")
    g_conv_full = g_small[SMALL_ROWS:].reshape(conv_shard[0], 4 * conv_shard[1])
    g_conv = lax.dynamic_slice_in_dim(g_conv_full, chip * conv_shard[1], conv_shard[1], axis=1)

    grads, deltas, new_m, new_v = dict(g_big), {}, {}, {}
    for name, _, _ in BIG:
        deltas[name], new_m[name], new_v[name] = _adamw(wd[name], g_big[name], md[name], vd[name], "adamw_" + name)
    pack_sm = lambda d: jnp.concatenate([_pack_small(d), _pack_rows([d["conv_w"]], APACK_ROWS - SMALL_ROWS, F32)], axis=0)
    g_sm = jnp.concatenate([g_small[:SMALL_ROWS], _pack_rows([g_conv], APACK_ROWS - SMALL_ROWS, F32)], axis=0)
    small = (g_sm,) + _adamw(pack_sm(wd), g_sm, pack_sm(md), pack_sm(vd), "adamw_small")
    for out, pack in zip((grads, deltas, new_m, new_v), small):
        out.update(_unpack_small(pack[:SMALL_ROWS]))
        out["conv_w"] = pack[SMALL_ROWS:].reshape(-1)[:conv_shard[0] * conv_shard[1]].reshape(conv_shard)

    return (loss, grad_x[None], *[d[n][None] for d in (grads, deltas, new_m, new_v) for n in WEIGHTS])
```

```python
import jax
import jax.numpy as jnp
from jax import lax
from jax.experimental import pallas as pl
from jax.experimental.pallas import tpu as pltpu

F32 = jnp.float32
BF16 = jnp.bfloat16
MESH = pl.DeviceIdType.MESH

D_MODEL = 1024
N_HEAD = 4
D_HEAD = 128
BR_W = N_HEAD * D_HEAD
CONV_TAPS = 4
GDN_CHUNK = 64
INV_BLOCK = 16
INV_CHUNKS = 4
N_MEM = 256
D_FF = 4 * D_MODEL
EPS = 1e-6
LANES = 128
PROJ_W = 7168
GATE_OFF = 0
QKV_OFF = 3072
SB_OFF = 4608
Z_OFF = 6144
MEMQ_OFF = 6656
IN_GATE, IN_QKV, IN_SB, IN_Z, IN_MEMQ, IN_AB = (4104, 7176), (0, 1536), (2056, 3592), (1536, 2048), (3592, 4104), (2048, 2056)

ADAM_LR, ADAM_B1, ADAM_B2, ADAM_EPS, ADAM_WD, ADAM_STEP = 0.001, 0.9, 0.999, 1e-08, 0.01, 10

TM = 512
MM_TM = 1024
MM_TK = (2048, 1792, 1024, 128)
TK_TOK = 2048
GDN_STEP_CHUNKS = 4
GDN_BWD_STEP_CHUNKS = 1
G1_TM = 256
SB_BLK = 512
SB_W = 256
SB_DEAD = 120.0
VMEM_LIMIT = 48 << 20

BIG = (("w_in", (1024, 1794), 1), ("w_mem_kv", (256, 1024), 0), ("w_br_gdn", (512, 256), 1),
       ("w_br_sb", (512, 256), 1), ("w_br_mem", (512, 256), 1), ("w_o", (256, 1024), 0),
       ("w_up", (1024, 1024), 1), ("w_down", (1024, 1024), 0))
COL_SHARDED = tuple(n for n, _, a in BIG if a == 1)
GATE_COL = GATE_OFF // D_MODEL
QKV_COL = QKV_OFF // (3 * BR_W)
ROW_BLK = 1024
ADAM_ROWS = 128
SMALL = (("norm1_g", 1024), ("mem_norm_g", 1024), ("norm2_g", 1024), ("gdn_norm_g", 128), ("sb_q_norm_g", 128),
         ("sb_k_norm_g", 128), ("mem_q_norm_g", 128), ("mem_k_norm_g", 128), ("a_log", 4), ("dt_bias", 4))
SMALL_ROWS = 32
CONV_ROWS = 48
SPACK_ROWS = SMALL_ROWS + CONV_ROWS
APACK_ROWS = SMALL_ROWS + 16

WEIGHTS = ("norm1_g", "w_in", "conv_w", "a_log", "dt_bias", "gdn_norm_g", "sb_q_norm_g", "sb_k_norm_g",
           "mem_norm_g", "w_mem_kv", "mem_q_norm_g", "mem_k_norm_g", "w_br_gdn", "w_br_sb", "w_br_mem", "w_o",
           "norm2_g", "w_up", "w_down")


def _cp(*sem):
    return pltpu.CompilerParams(dimension_semantics=sem if sem else None, vmem_limit_bytes=VMEM_LIMIT)


HBM = pl.BlockSpec(memory_space=pl.ANY)

_NN = (((1,), (0,)), ((), ()))
_NT = (((1,), (1,)), ((), ()))
_TN = (((0,), (0,)), ((), ()))


def _dot(a, b, dims=_NN):
    return lax.dot_general(a.astype(BF16), b.astype(BF16), dims, preferred_element_type=F32)


def _dot_nt(a, b):
    return _dot(a, b, _NT)


def _dot_tn(a, b):
    return _dot(a, b, _TN)


def _dotf(a, b, dims=_NN):
    return lax.dot_general(a, b, dims, precision=lax.Precision.HIGHEST, preferred_element_type=F32)


def _sigmoid(v):
    return 0.5 * jnp.tanh(0.5 * v) + 0.5


def _softplus(v):
    return jnp.maximum(v, 0.0) + jnp.log(1.0 + jnp.exp(-jnp.abs(v)))


def _iota(shape, dim):
    return lax.broadcasted_iota(jnp.int32, shape, dim)


def _hs(h):
    return slice(h * D_HEAD, (h + 1) * D_HEAD)


def _rowspec(tm, w, col=0):
    return pl.BlockSpec((tm, w), lambda i: (i, col))


def _full(shape):
    return pl.BlockSpec(shape, lambda *_: (0,) * len(shape))


def _accum(ref, first, val):
    @pl.when(first)
    def _():
        ref[...] = val

    @pl.when(jnp.logical_not(first))
    def _():
        ref[...] += val


class _Hosted:
    def __init__(self, ins, out_shapes, n_sems, start, finish, mid=None, scratch=()):
        self.ins, self.out_shapes, self.n_sems = list(ins), list(out_shapes), n_sems
        self.start, self.mid, self.finish, self.scratch = start, mid, finish, list(scratch)


def _mm(a, b, mode, out_dtype, name, *, tm=None, tn=None, tk=None, a_fn=None, extra=None, epi=None, comm=None):
    if mode == "tn":
        (K, M), N = a.shape, b.shape[1]
    else:
        (M, K), N = a.shape, (b.shape[0] if mode == "nt" else b.shape[1])
    tm = min(tm or (1024 if mode == "tn" else MM_TM), M)
    tn = min(tn or 1024, N)
    tk = min(tk or (TK_TOK if mode == "tn" else next(t for t in MM_TK if K % t == 0)), K)
    nm, nn, nk = M // tm, N // tn, K // tk
    assert nm * tm == M and nn * tn == N and nk * tk == K, (name, a.shape, b.shape)
    if mode == "tn":
        a_spec = pl.BlockSpec((tk, tm), lambda i, j, k: (k, i))
    else:
        a_spec = pl.BlockSpec((tm, tk), lambda i, j, k: (i, k))
    if mode == "nt":
        b_spec = pl.BlockSpec((tn, tk), lambda i, j, k: (j, k))
    else:
        b_spec = pl.BlockSpec((tk, tn), lambda i, j, k: (k, j))
    dims = {"nn": _NN, "nt": _NT, "tn": _TN}[mode]
    o_spec = pl.BlockSpec((tm, tn), lambda i, j, k: (i, j))
    has_extra = extra is not None

    n_ci, n_co = (len(comm.ins), len(comm.out_shapes)) if comm else (0, 0)
    n_in = 2 + has_extra + n_ci
    steps = nm * nn * nk

    def body(*refs):
        a_ref, b_ref = refs[0], refs[1]
        e_ref = refs[2] if has_extra else None
        o_ref = refs[n_in]
        scratch = refs[n_in + 1 + n_co:]
        if comm:
            step = (pl.program_id(0) * nn + pl.program_id(1)) * nk + pl.program_id(2)
            cargs = (refs[2 + has_extra:n_in], refs[n_in + 1:n_in + 1 + n_co], scratch[nk > 1], scratch[(nk > 1) + 1],
                     scratch[(nk > 1) + 2:])
            pl.when(step == 0)(lambda: comm.start(*cargs))
            if comm.mid is not None:
                pl.when(step == (steps * 7) // 8)(lambda: comm.mid(*cargs))
        av = a_ref[...]
        if a_fn is not None:
            av = a_fn(av)
        p = lax.dot_general(av, b_ref[...], dims, preferred_element_type=F32)

        def finish(acc):
            if epi is not None:
                acc = epi(acc, e_ref[...] if has_extra else None)
            o_ref[...] = acc.astype(out_dtype)

        if nk == 1:
            finish(p)
        else:
            acc_ref = scratch[0]
            k = pl.program_id(2)
            _accum(acc_ref, k == 0, p)

            @pl.when(k == nk - 1)
            def _():
                finish(acc_ref[...])

        if comm:
            pl.when(step == steps - 1)(lambda: comm.finish(*cargs))

    ins = [a, b] + ([extra] if has_extra else [])
    in_specs = [a_spec, b_spec] + ([o_spec] if has_extra else [])
    scratch_shapes = [pltpu.VMEM((tm, tn), F32)] if nk > 1 else []
    main = jax.ShapeDtypeStruct((M, N), out_dtype)
    if not comm:
        return pl.pallas_call(
            body, out_shape=main, grid=(nm, nn, nk), in_specs=in_specs, out_specs=o_spec,
            scratch_shapes=scratch_shapes, name=name, compiler_params=_cp("parallel", "parallel", "arbitrary"))(*ins)
    sems = [pltpu.SemaphoreType.DMA((comm.n_sems,)), pltpu.SemaphoreType.DMA((comm.n_sems,))]
    res = pl.pallas_call(
        body, out_shape=(main, *comm.out_shapes), grid=(nm, nn, nk), in_specs=in_specs + [HBM] * n_ci,
        out_specs=(o_spec, *[HBM] * n_co), scratch_shapes=scratch_shapes + sems + comm.scratch, name=name,
        compiler_params=_cp("arbitrary", "arbitrary", "arbitrary"))(*ins, *comm.ins)
    return res[0], list(res[1:])


def _relu2(u):
    r = jnp.maximum(u.astype(F32), 0.0)
    return (r * r).astype(BF16)


def _epi_add(acc, e):
    return acc + e.astype(F32)


def _epi_drelu2(acc, u):
    return acc * (2.0 * jnp.maximum(u.astype(F32), 0.0))


def _rms_fwd(x, g, name):
    T, dm = x.shape
    tm = min(TM, T)

    def body(x_ref, g_ref, h_ref):
        xv = x_ref[...]
        r = lax.rsqrt(jnp.mean(xv * xv, axis=-1, keepdims=True) + EPS)
        h_ref[...] = (xv * r * g_ref[...]).astype(BF16)

    return pl.pallas_call(
        body, out_shape=jax.ShapeDtypeStruct((T, dm), BF16), grid=(T // tm,),
        in_specs=[_rowspec(tm, dm), _full((1, dm))], out_specs=_rowspec(tm, dm), name=name,
        compiler_params=_cp("parallel"))(x, g)


def _rms_bwd(dh, x, g, resid, name):
    T, dm = x.shape
    tm = min(TM, T)

    def body(dh_ref, x_ref, g_ref, res_ref, dx_ref, dxb_ref, dg_ref):
        i = pl.program_id(0)
        xv = x_ref[...]
        r = lax.rsqrt(jnp.mean(xv * xv, axis=-1, keepdims=True) + EPS)
        y = xv * r
        dhv = dh_ref[...].astype(F32)
        dy = dhv * g_ref[...]
        dx = res_ref[...] + r * (dy - y * jnp.mean(dy * y, axis=-1, keepdims=True))
        dx_ref[...] = dx
        dxb_ref[...] = dx.astype(BF16)
        _accum(dg_ref, i == 0, jnp.sum(dhv * y, axis=0, keepdims=True))

    return pl.pallas_call(
        body,
        out_shape=(jax.ShapeDtypeStruct((T, dm), F32), jax.ShapeDtypeStruct((T, dm), BF16),
                   jax.ShapeDtypeStruct((1, dm), F32)),
        grid=(T // tm,),
        in_specs=[_rowspec(tm, dm), _rowspec(tm, dm), _full((1, dm)), _rowspec(tm, dm)],
        out_specs=(_rowspec(tm, dm), _rowspec(tm, dm), _full((1, dm))), name=name,
        compiler_params=_cp("arbitrary"))(dh, x, g, resid)


def _conv_tile(x_ref, halo_ref, w_ref, xpad, tm):
    i = pl.program_id(0)
    halo = halo_ref[...].astype(F32)[8:16]
    xpad[0:8, :] = jnp.where(i > 0, halo, 0.0)
    xpad[8:, :] = x_ref[...].astype(F32)
    w = w_ref[...]
    xc = w[0:1] * xpad[5:5 + tm, :]
    for j in range(1, CONV_TAPS):
        xc = xc + w[j:j + 1] * xpad[5 + j:5 + j + tm, :]
    return xc


def _gate_terms(ab_ref, av_ref):
    abv = ab_ref[...]
    av = av_ref[...]
    pre = abv + av[1:2]
    ea = jnp.exp(av[0:1])
    g = -ea * _softplus(pre)
    return abv, pre, ea, g


def _gdn_pre(proj, conv_w, ab, avec):
    T = proj.shape[0]
    tm = min(G1_TM, T)
    cw = 3 * BR_W

    def body(x_ref, halo_ref, w_ref, ab_ref, av_ref, q_ref, k_ref, v_ref, gb_ref, xpad):
        xc = _conv_tile(x_ref, halo_ref, w_ref, xpad, tm)
        y = xc * _sigmoid(xc)
        for h in range(N_HEAD):
            for off, ref, scale in ((0, q_ref, D_HEAD ** -0.5), (BR_W, k_ref, 1.0)):
                yh = y[:, off + h * D_HEAD:off + (h + 1) * D_HEAD]
                r = lax.rsqrt(jnp.sum(yh * yh, axis=-1, keepdims=True) + EPS)
                ref[:, _hs(h)] = yh * (r * scale)
        v_ref[...] = y[:, 2 * BR_W:]
        abv, _, _, g = _gate_terms(ab_ref, av_ref)
        lane = _iota((tm, LANES), 1)
        gb_ref[...] = jnp.where(lane < N_HEAD, g, jnp.where(lane < 2 * N_HEAD, _sigmoid(abv), 0.0))

    hb = tm // 16
    return pl.pallas_call(
        body,
        out_shape=(jax.ShapeDtypeStruct((T, BR_W), F32),) * 3 + (jax.ShapeDtypeStruct((T, LANES), F32),),
        grid=(T // tm,),
        in_specs=[_rowspec(tm, cw, QKV_COL), pl.BlockSpec((16, cw), lambda i: (jnp.maximum(i * hb - 1, 0), QKV_COL)),
                  _full((CONV_TAPS, cw)), _rowspec(tm, LANES), _full((2, LANES))],
        out_specs=(_rowspec(tm, BR_W),) * 3 + (_rowspec(tm, LANES),),
        scratch_shapes=[pltpu.VMEM((tm + 8, cw), F32)], name="gdn_pre",
        compiler_params=_cp("parallel"))(proj, proj, conv_w, ab, avec)


def _gdn_pre_bwd(proj, conv_w, ab, avec, dq, dk, dv, dgb):
    T = proj.shape[0]
    tm = min(G1_TM, T)
    cw = 3 * BR_W

    def body(x_ref, halo_ref, w_ref, ab_ref, av_ref, dq_ref, dk_ref, dv_ref, dgb_ref,
             dxc_ref, dab_ref, dcw_ref, dav_ref, xpad):
        i = pl.program_id(0)

        @pl.when(i == 0)
        def _():
            dcw_ref[...] = jnp.zeros_like(dcw_ref)
            dav_ref[...] = jnp.zeros_like(dav_ref)

        xc_all = _conv_tile(x_ref, halo_ref, w_ref, xpad, tm)
        for s in range(cw // D_HEAD):
            cs = slice(s * D_HEAD, (s + 1) * D_HEAD)
            xc = xc_all[:, cs]
            sg = _sigmoid(xc)
            yh = xc * sg
            h = s % N_HEAD
            if s < 2 * N_HEAD:
                dref, scale = (dq_ref, D_HEAD ** -0.5) if s < N_HEAD else (dk_ref, 1.0)
                r = lax.rsqrt(jnp.sum(yh * yh, axis=-1, keepdims=True) + EPS)
                yn = yh * r
                dn = dref[:, _hs(h)]
                dy = (scale * r) * (dn - yn * jnp.sum(yn * dn, axis=-1, keepdims=True))
            else:
                dy = dv_ref[:, _hs(h)]
            dxc = dy * (sg * (1.0 + xc * (1.0 - sg)))
            dxc_ref[:, cs] = dxc.astype(BF16)
            for j in range(CONV_TAPS):
                dcw_ref[j:j + 1, cs] += jnp.sum(dxc * xpad[5 + j:5 + j + tm, cs], axis=0, keepdims=True)

        abv, pre, ea, g = _gate_terms(ab_ref, av_ref)
        dgbv = dgb_ref[...]
        lane = _iota((tm, LANES), 1)
        is_a = lane < N_HEAD
        da = jnp.where(is_a, dgbv * (-ea) * _sigmoid(pre), 0.0)
        bs = _sigmoid(abv)
        db = jnp.where(jnp.logical_and(lane >= N_HEAD, lane < 2 * N_HEAD), dgbv * bs * (1.0 - bs), 0.0)
        dab_ref[...] = (da + db).astype(BF16)
        dav_ref[0:1, :] += jnp.sum(jnp.where(is_a, dgbv * g, 0.0), axis=0, keepdims=True)
        dav_ref[1:2, :] += jnp.sum(da, axis=0, keepdims=True)

    hb = tm // 16
    return pl.pallas_call(
        body,
        out_shape=(jax.ShapeDtypeStruct((T, cw), BF16), jax.ShapeDtypeStruct((T, LANES), BF16),
                   jax.ShapeDtypeStruct((CONV_TAPS, cw), F32), jax.ShapeDtypeStruct((2, LANES), F32)),
        grid=(T // tm,),
        in_specs=[_rowspec(tm, cw, QKV_COL), pl.BlockSpec((16, cw), lambda i: (jnp.maximum(i * hb - 1, 0), QKV_COL)),
                  _full((CONV_TAPS, cw)), _rowspec(tm, LANES), _full((2, LANES)),
                  _rowspec(tm, BR_W), _rowspec(tm, BR_W), _rowspec(tm, BR_W), _rowspec(tm, LANES)],
        out_specs=(_rowspec(tm, cw), _rowspec(tm, LANES), _full((CONV_TAPS, cw)), _full((2, LANES))),
        scratch_shapes=[pltpu.VMEM((tm + 8, cw), F32)], name="gdn_pre_bwd",
        compiler_params=_cp("arbitrary"))(proj, proj, conv_w, ab, avec, dq, dk, dv, dgb)


def _conv_bwd(dxc, conv_w, into):
    T, cw = dxc.shape
    tm = min(G1_TM, T)
    nt = T // tm
    hb = tm // 16

    def body(d_ref, halo_ref, w_ref, into_ref, dx_ref, xpad):
        i = pl.program_id(0)
        xpad[0:tm, :] = d_ref[...].astype(F32)
        xpad[tm:, :] = jnp.where(i < nt - 1, halo_ref[...].astype(F32)[0:8], 0.0)
        w = w_ref[...]
        dx = w[3:4] * xpad[0:tm, :]
        for j in range(CONV_TAPS - 1):
            dx = dx + w[j:j + 1] * xpad[3 - j:3 - j + tm, :]
        dx_ref[...] = dx.astype(BF16)

    return pl.pallas_call(
        body, out_shape=jax.ShapeDtypeStruct(into.shape, BF16), grid=(nt,),
        in_specs=[_rowspec(tm, cw), pl.BlockSpec((16, cw), lambda i: (jnp.minimum((i + 1) * hb, T // 16 - 1), 0)),
                  _full((CONV_TAPS, cw)), HBM],
        out_specs=_rowspec(tm, cw, QKV_COL), scratch_shapes=[pltpu.VMEM((tm + 8, cw), F32)],
        input_output_aliases={3: 0}, name="conv_bwd", compiler_params=_cp("parallel"))(dxc, dxc, conv_w, into)


def _chunk_consts():
    C = GDN_CHUNK
    row, col = _iota((C, C), 0), _iota((C, C), 1)
    return row, col, row >= col, row > col


def _chunk_decay(gbv, incl):
    c_all = _dotf(incl.astype(F32), gbv)
    c_t = jnp.concatenate([c_all, jnp.zeros_like(c_all)], axis=0).T[:, :GDN_CHUNK]
    return c_all, c_t


def _head_decay(c_all, c_t, gbv, incl, h):
    C = GDN_CHUNK
    c_col = c_all[:, h:h + 1]
    c_row = c_t[h:h + 1, :]
    gam = jnp.exp(jnp.where(incl, c_col - c_row, -1e30))
    c_last = c_all[C - 1:C, h:h + 1]
    return gam, jnp.exp(c_col), jnp.exp(c_last - c_col), jnp.exp(c_last), gbv[:, N_HEAD + h:N_HEAD + h + 1]


def _split_bf16(x):
    hi = x.astype(BF16)
    return hi, (x - hi.astype(F32)).astype(BF16)


def _dot3(a, b):
    ah, al = _split_bf16(a)
    bh, bl = _split_bf16(b)
    d = lambda u, v: lax.dot_general(u, v, _NN, preferred_element_type=F32)
    return d(ah, bh) + (d(ah, bl) + d(al, bh))


def _unit_lower_inverses(ms, row, col):
    bi, bj = row // INV_BLOCK, col // INV_BLOCK
    eye = (row == col).astype(F32)
    ns = [jnp.where(bi == bj, -m, 0.0) for m in ms]
    invs = [eye + n for n in ns]
    size = 2
    while size < INV_BLOCK:
        ns = [_dot3(n, n) for n in ns]
        invs = [inv + _dot3(inv, n) for inv, n in zip(invs, ns)]
        size *= 2
    width = 2
    while width * INV_BLOCK <= GDN_CHUNK:
        sel = jnp.logical_and(bi // width == bj // width, bi // (width // 2) > bj // (width // 2))
        ts = [_dot3(inv, jnp.where(sel, m, 0.0)) for inv, m in zip(invs, ms)]
        invs = [inv - _dot3(t, inv) for inv, t in zip(invs, ts)]
        width *= 2
    return invs


def _gdn_inv(k, gb):
    T = k.shape[0]
    C = GDN_CHUNK
    per = min(INV_CHUNKS, T // C)
    rows = per * C

    def body(k_ref, gb_ref, ti_ref, tt_ref):
        row, col, incl, strict = _chunk_consts()
        ms = []
        for ci in range(per):
            rs = slice(ci * C, (ci + 1) * C)
            gbv = gb_ref[rs, :]
            c_all, c_t = _chunk_decay(gbv, incl)
            for h in range(N_HEAD):
                gam, _, _, _, bcol = _head_decay(c_all, c_t, gbv, incl, h)
                K = k_ref[rs, _hs(h)]
                ms.append(jnp.where(strict, _dot_nt(K * bcol, K) * gam, 0.0))
        eye = (row == col).astype(BF16)
        for i, inv in enumerate(_unit_lower_inverses(ms, row, col)):
            ti_ref[i // N_HEAD, i % N_HEAD] = inv
            tt_ref[i // N_HEAD, i % N_HEAD] = _dot_tn(inv, eye).astype(BF16)

    spec = pl.BlockSpec((per, N_HEAD, C, C), lambda i: (i, 0, 0, 0))
    return pl.pallas_call(
        body, out_shape=(jax.ShapeDtypeStruct((T // C, N_HEAD, C, C), F32),
                         jax.ShapeDtypeStruct((T // C, N_HEAD, C, C), BF16)),
        grid=(T // rows,), in_specs=[_rowspec(rows, BR_W), _rowspec(rows, LANES)], out_specs=(spec, spec),
        name="gdn_inv", compiler_params=_cp("parallel"))(k, gb)


def _gdn_fwd(q, k, v, gb, proj, gnorm, tinv_all):
    T = q.shape[0]
    C = GDN_CHUNK
    nc = T // C
    per = min(GDN_STEP_CHUNKS, nc)
    zcol = Z_OFF // BR_W
    heads = range(N_HEAD)

    def body(q_ref, k_ref, v_ref, gb_ref, z_ref, gn_ref, ti_ref, og_ref, oraw_ref, sh_ref, vn_ref, s_ref):
        @pl.when(pl.program_id(0) == 0)
        def _():
            s_ref[...] = jnp.zeros_like(s_ref)

        _, _, incl, _ = _chunk_consts()
        S = [s_ref[h] for h in heads]
        for ci in range(per):
            rs = slice(ci * C, (ci + 1) * C)
            gbv = gb_ref[rs, :]
            c_all, c_t = _chunk_decay(gbv, incl)
            dec = [_head_decay(c_all, c_t, gbv, incl, h) for h in heads]
            gam, gcol, dcol, glast, bcol = ([d[i] for d in dec] for i in range(5))
            Q = [q_ref[rs, _hs(h)] for h in heads]
            K = [k_ref[rs, _hs(h)] for h in heads]
            V = [v_ref[rs, _hs(h)] for h in heads]
            Sb = [s.astype(BF16) for s in S]
            KS = [_dot(K[h], Sb[h]) for h in heads]
            QS = [_dot(Q[h], Sb[h]) for h in heads]
            P = [_dot_nt(Q[h], K[h]) * gam[h] for h in heads]
            R = [bcol[h] * (V[h] - gcol[h] * KS[h]) for h in heads]
            vn = [_dot(ti_ref[ci, h], R[h]) for h in heads]
            O = [gcol[h] * QS[h] + _dot(P[h], vn[h]) for h in heads]
            Sn = [glast[h] * S[h] + _dot_tn(K[h] * dcol[h], vn[h]) for h in heads]
            for h in heads:
                sh_ref[ci, h] = S[h]
                vn_ref[rs, _hs(h)] = vn[h]
                oraw_ref[rs, _hs(h)] = O[h]
                rr = lax.rsqrt(jnp.mean(O[h] * O[h], axis=-1, keepdims=True) + EPS)
                zz = z_ref[rs, _hs(h)].astype(F32)
                og_ref[rs, _hs(h)] = (O[h] * rr * gn_ref[...] * (zz * _sigmoid(zz))).astype(BF16)
            S = Sn
        for h in heads:
            s_ref[h] = S[h]

    cspec = lambda w, cb=0: pl.BlockSpec((per * C, w), lambda n: (n, cb))
    hist = lambda a, b: pl.BlockSpec((per, N_HEAD, a, b), lambda n: (n, 0, 0, 0))
    return pl.pallas_call(
        body,
        out_shape=(jax.ShapeDtypeStruct((T, BR_W), BF16), jax.ShapeDtypeStruct((T, BR_W), F32),
                   jax.ShapeDtypeStruct((nc, N_HEAD, D_HEAD, D_HEAD), F32), jax.ShapeDtypeStruct((T, BR_W), F32)),
        grid=(nc // per,),
        in_specs=[cspec(BR_W), cspec(BR_W), cspec(BR_W), cspec(LANES), cspec(BR_W, zcol), _full((1, D_HEAD)),
                  hist(C, C)],
        out_specs=(cspec(BR_W), cspec(BR_W), hist(D_HEAD, D_HEAD), cspec(BR_W)),
        scratch_shapes=[pltpu.VMEM((N_HEAD, D_HEAD, D_HEAD), F32)], name="gdn_chunk_fwd",
        compiler_params=_cp("arbitrary"))(q, k, v, gb, proj, gnorm, tinv_all)


def _gdn_bwd(q, k, v, gb, proj, gnorm, oraw, shist, tinv_all, vn_all, dog, into):
    T = q.shape[0]
    C = GDN_CHUNK
    nc = T // C
    per = min(GDN_BWD_STEP_CHUNKS, nc)
    zcol = Z_OFF // BR_W

    def body(q_ref, k_ref, v_ref, gb_ref, z_ref, gn_ref, oraw_ref, sh_ref, tt_ref, vn_ref, dog_ref, into_ref,
             dq_ref, dk_ref, dv_ref, dgb_ref, dz_ref, dgn_ref, ds_ref):
        @pl.when(pl.program_id(0) == 0)
        def _():
            ds_ref[...] = jnp.zeros_like(ds_ref)
            dgn_ref[...] = jnp.zeros_like(dgn_ref)

        row, col, incl, strict = _chunk_consts()
        lane = _iota((C, LANES), 1)
        rowl = _iota((C, LANES), 0)
        eye = (row == col).astype(F32)
        upper = (col >= row).astype(F32)
        gn = gn_ref[...]
        heads = range(N_HEAD)
        rsum = lambda a: jnp.sum(a, axis=-1, keepdims=True)
        dgn = jnp.zeros((1, D_HEAD), F32)
        dSn = [ds_ref[h] for h in heads]
        for ci in reversed(range(per)):
            rs = slice(ci * C, (ci + 1) * C)
            gbv = gb_ref[rs, :]
            c_all, c_t = _chunk_decay(gbv, incl)
            dec = [_head_decay(c_all, c_t, gbv, incl, h) for h in heads]
            gam, gcol, dcol, glast, bcol = ([d[i] for d in dec] for i in range(5))
            Q = [q_ref[rs, _hs(h)] for h in heads]
            K = [k_ref[rs, _hs(h)] for h in heads]
            V = [v_ref[rs, _hs(h)] for h in heads]
            dO = []
            for h in heads:
                O = oraw_ref[rs, _hs(h)]
                zz = z_ref[rs, _hs(h)].astype(F32)
                dogv = dog_ref[rs, _hs(h)].astype(F32)
                rr = lax.rsqrt(jnp.mean(O * O, axis=-1, keepdims=True) + EPS)
                on = O * rr
                sg = _sigmoid(zz)
                dz_ref[rs, _hs(h)] = (dogv * on * gn * (sg * (1.0 + zz * (1.0 - sg)))).astype(BF16)
                dyn = dogv * (zz * sg)
                dgn = dgn + jnp.sum(dyn * on, axis=0, keepdims=True)
                dyv = dyn * gn
                dO.append((rr * (dyv - on * jnp.mean(dyv * on, axis=-1, keepdims=True))).astype(BF16))
            S = [sh_ref[ci, h] for h in heads]
            Sb = [s.astype(BF16) for s in S]
            vn = [vn_ref[rs, _hs(h)] for h in heads]
            vnb = [a.astype(BF16) for a in vn]
            dSb = [a.astype(BF16) for a in dSn]
            Kb = [K[h] * bcol[h] for h in heads]
            gam_t = [jnp.exp(jnp.where(col >= row, c_t[h:h + 1, :] - c_all[:, h:h + 1], -1e30)) for h in heads]
            M = [jnp.where(strict, _dot_nt(Kb[h], K[h]) * gam[h], 0.0) for h in heads]
            P = [_dot_nt(Q[h], K[h]) * gam[h] for h in heads]
            P_t = [_dot_nt(K[h], Q[h]) * gam_t[h] for h in heads]
            KS = [_dot(K[h], Sb[h]) for h in heads]
            QS = [_dot(Q[h], Sb[h]) for h in heads]
            dvn = [_dot(P_t[h], dO[h]) + _dot(K[h] * dcol[h], dSb[h]) for h in heads]
            dR = [_dot(tt_ref[ci, h], dvn[h]) for h in heads]
            dRb = [a.astype(BF16) for a in dR]
            bg = [bcol[h] * gcol[h] for h in heads]
            dS_new = [glast[h] * dSn[h] + _dot_tn(gcol[h] * Q[h], dO[h]) - _dot_tn(bg[h] * K[h], dRb[h])
                      for h in heads]
            dP = [jnp.where(incl, _dot_nt(dO[h], vnb[h]), 0.0) for h in heads]
            dM = [jnp.where(strict, -_dot_nt(dRb[h], vnb[h]), 0.0) for h in heads]
            dPG = [(dP[h] * gam[h]).astype(BF16) for h in heads]
            dMG = [(dM[h] * gam[h]).astype(BF16) for h in heads]
            dPG_t = [(jnp.where(col >= row, _dot_nt(vnb[h], dO[h]), 0.0) * gam_t[h]).astype(BF16) for h in heads]
            dMG_t = [(jnp.where(col > row, -_dot_nt(vnb[h], dRb[h]), 0.0) * gam_t[h]).astype(BF16) for h in heads]
            E = [_dot_nt(vnb[h], dSb[h]) for h in heads]
            dKb = [_dot(dMG[h], K[h]) for h in heads]
            dc_all = jnp.zeros((C, LANES), F32)
            db_all = jnp.zeros((C, LANES), F32)
            for h in heads:
                dq_ref[rs, _hs(h)] = gcol[h] * _dot_nt(dO[h], Sb[h]) + _dot(dPG[h], K[h])
                dk_ref[rs, _hs(h)] = (_dot(dPG_t[h], Q[h]) + _dot(dMG_t[h], Kb[h]) + bcol[h] * dKb[h]
                                      - bg[h] * _dot_nt(dRb[h], Sb[h]) + dcol[h] * E[h])
                dv_ref[rs, _hs(h)] = bcol[h] * dR[h]
                dbeta = rsum(dKb[h] * K[h]) + rsum(dR[h] * (V[h] - gcol[h] * KS[h]))
                X = dP[h] * P[h] + dM[h] * M[h]
                ddel = rsum(K[h] * E[h]) * dcol[h]
                colsum = rsum(eye * jnp.sum(X, axis=0, keepdims=True))
                dc = (rsum(X) - colsum + gcol[h] * rsum(dO[h].astype(F32) * QS[h]) - bg[h] * rsum(dR[h] * KS[h])
                      - ddel)
                last = (jnp.sum(ddel, axis=0, keepdims=True)
                        + glast[h] * jnp.sum(rsum(dSn[h] * S[h]), axis=0, keepdims=True))
                dc_all = dc_all + jnp.where(lane == h, dc + jnp.where(rowl == C - 1, last, 0.0), 0.0)
                db_all = db_all + jnp.where(lane == N_HEAD + h, dbeta, 0.0)
            dgb_ref[rs, :] = _dotf(upper, dc_all) + db_all
            dSn = dS_new
        for h in heads:
            ds_ref[h] = dSn[h]
        dgn_ref[...] += dgn

    nb = nc // per
    cspec = lambda w, cb=0: pl.BlockSpec((per * C, w), lambda n: (nb - 1 - n, cb))
    hist = lambda a, b: pl.BlockSpec((per, N_HEAD, a, b), lambda n: (nb - 1 - n, 0, 0, 0))
    return pl.pallas_call(
        body,
        out_shape=(jax.ShapeDtypeStruct((T, BR_W), F32),) * 3 + (
            jax.ShapeDtypeStruct((T, LANES), F32), jax.ShapeDtypeStruct(into.shape, BF16),
            jax.ShapeDtypeStruct((1, D_HEAD), F32)),
        grid=(nb,),
        in_specs=[cspec(BR_W), cspec(BR_W), cspec(BR_W), cspec(LANES), cspec(BR_W, zcol), _full((1, D_HEAD)),
                  cspec(BR_W), hist(D_HEAD, D_HEAD), hist(C, C), cspec(BR_W), cspec(BR_W), HBM],
        out_specs=(cspec(BR_W), cspec(BR_W), cspec(BR_W), cspec(LANES), cspec(BR_W, zcol), _full((1, D_HEAD))),
        scratch_shapes=[pltpu.VMEM((N_HEAD, D_HEAD, D_HEAD), F32)], input_output_aliases={11: 4},
        name="gdn_chunk_bwd",
        compiler_params=_cp("arbitrary"))(q, k, v, gb, proj, gnorm, oraw, shist, tinv_all, vn_all, dog, into)


SB_COL = SB_OFF // BR_W
SB_SCALE = D_HEAD ** -0.5


def _sb_pre(proj, gq, gk):
    T = proj.shape[0]
    tm = min(TM, T)

    def body(xq_ref, xk_ref, xv_ref, gq_ref, gk_ref, q_ref, k_ref, v_ref):
        for h in range(N_HEAD):
            for x_ref, g_ref, ref, scale in ((xq_ref, gq_ref, q_ref, SB_SCALE), (xk_ref, gk_ref, k_ref, 1.0)):
                xh = x_ref[:, _hs(h)].astype(F32)
                r = lax.rsqrt(jnp.mean(xh * xh, axis=-1, keepdims=True) + EPS)
                ref[:, _hs(h)] = (xh * (r * scale) * g_ref[...]).astype(BF16)
        v_ref[...] = xv_ref[...]

    return pl.pallas_call(
        body, out_shape=(jax.ShapeDtypeStruct((T, BR_W), BF16),) * 3, grid=(T // tm,),
        in_specs=[_rowspec(tm, BR_W, SB_COL), _rowspec(tm, BR_W, SB_COL + 1), _rowspec(tm, BR_W, SB_COL + 2),
                  _full((1, D_HEAD)), _full((1, D_HEAD))],
        out_specs=(_rowspec(tm, BR_W),) * 3, name="sb_pre", compiler_params=_cp("parallel"))(proj, proj, proj, gq, gk)


def _sb_pre_bwd(proj, gq, gk, dq, dk, dv, into):
    T = proj.shape[0]
    tm = min(TM, T)

    def body(xq_ref, xk_ref, gq_ref, gk_ref, dq_ref, dk_ref, dv_ref, into_ref, dx_ref, dgq_ref, dgk_ref):
        i = pl.program_id(0)

        @pl.when(i == 0)
        def _():
            dgq_ref[...] = jnp.zeros_like(dgq_ref)
            dgk_ref[...] = jnp.zeros_like(dgk_ref)

        for off, x_ref, g_ref, d_ref, dg_ref, scale in ((0, xq_ref, gq_ref, dq_ref, dgq_ref, SB_SCALE),
                                                        (BR_W, xk_ref, gk_ref, dk_ref, dgk_ref, 1.0)):
            dg = jnp.zeros((1, D_HEAD), F32)
            for h in range(N_HEAD):
                xh = x_ref[:, _hs(h)].astype(F32)
                r = lax.rsqrt(jnp.mean(xh * xh, axis=-1, keepdims=True) + EPS)
                y = xh * r
                dn = d_ref[:, _hs(h)] * scale
                dg = dg + jnp.sum(dn * y, axis=0, keepdims=True)
                dy = dn * g_ref[...]
                dx_ref[:, off + h * D_HEAD:off + (h + 1) * D_HEAD] = (
                    r * (dy - y * jnp.mean(dy * y, axis=-1, keepdims=True))).astype(BF16)
            dg_ref[...] += dg
        dx_ref[:, 2 * BR_W:] = dv_ref[...].astype(BF16)

    return pl.pallas_call(
        body,
        out_shape=(jax.ShapeDtypeStruct(into.shape, BF16), jax.ShapeDtypeStruct((1, D_HEAD), F32),
                   jax.ShapeDtypeStruct((1, D_HEAD), F32)),
        grid=(T // tm,),
        in_specs=[_rowspec(tm, BR_W, SB_COL), _rowspec(tm, BR_W, SB_COL + 1), _full((1, D_HEAD)), _full((1, D_HEAD)),
                  _rowspec(tm, BR_W), _rowspec(tm, BR_W), _rowspec(tm, BR_W), HBM],
        out_specs=(_rowspec(tm, 3 * BR_W, SB_OFF // (3 * BR_W)), _full((1, D_HEAD)), _full((1, D_HEAD))),
        input_output_aliases={7: 0}, name="sb_pre_bwd",
        compiler_params=_cp("arbitrary"))(proj, proj, gq, gk, dq, dk, dv, into)


def _sb_pair(q, k, masked):
    z = _dot_nt(q, k)
    zc = jnp.minimum(z, 30.0)
    sp = jnp.log(1.0 + jnp.exp(zc)) + (z - zc)
    if not masked:
        return z, sp, None
    mask = _iota(z.shape, 1) < _iota(z.shape, 0)
    return z, jnp.where(mask, sp, 0.0), mask


def _sb_fwd(sq, sk, sv):
    T = sq.shape[0]
    blk = min(SB_BLK, T)
    w = min(SB_W, blk)
    nb, nsub = T // blk, blk // w

    def body(q_ref, k_ref, v_ref, o_ref, lt_ref, cut_ref, acc_ref, r_ref):
        head, qi = pl.program_id(0), pl.program_id(1)
        acc_ref[...] = jnp.zeros_like(acc_ref)
        r_ref[...] = jnp.zeros_like(r_ref)

        def block(kj, masked):
            rows = pl.ds(pl.multiple_of(kj * blk, blk), blk)
            z, sp, mask = _sb_pair(q_ref[...], k_ref[rows, :], masked)
            after = (_iota((w, w), 0) > _iota((w, w), 1)).astype(BF16)
            r = r_ref[...]
            acc = acc_ref[...]
            for sb in reversed(range(nsub)):
                cs = slice(sb * w, (sb + 1) * w)
                sps = sp[:, cs]
                a = jnp.exp(z[:, cs] - sps - _dot(sps, after) - r)
                if masked:
                    a = jnp.where(mask[:, cs], a, 0.0)
                acc = acc + _dot(a, v_ref[pl.ds(pl.multiple_of(kj * blk + sb * w, w), w), :])
                r = r + jnp.sum(sps, axis=-1, keepdims=True)
            acc_ref[...] = acc
            r_ref[...] = r
            return jnp.min(r) < SB_DEAD

        def further(state):
            kj, _ = state
            return kj - 1, block(kj, False)

        left, _ = lax.while_loop(lambda s: jnp.logical_and(s[0] >= 0, s[1]), further, (qi - 1, block(qi, True)))
        o_ref[...] = acc_ref[...].astype(BF16)
        lt_ref[0] = r_ref[...]
        cut_ref[head, qi] = (left + 1).astype(F32)

    qspec = pl.BlockSpec((blk, D_HEAD), lambda h, i: (i, h))
    whole = pl.BlockSpec((T, D_HEAD), lambda h, i: (0, h))
    return pl.pallas_call(
        body,
        out_shape=(jax.ShapeDtypeStruct((T, BR_W), BF16), jax.ShapeDtypeStruct((N_HEAD, T, 1), F32),
                   jax.ShapeDtypeStruct((N_HEAD, nb), F32)),
        grid=(N_HEAD, nb), in_specs=[qspec, whole, whole],
        out_specs=(qspec, pl.BlockSpec((1, blk, 1), lambda h, i: (h, i, 0)), pl.BlockSpec(memory_space=pltpu.SMEM)),
        scratch_shapes=[pltpu.VMEM((blk, D_HEAD), F32), pltpu.VMEM((blk, 1), F32)],
        name="sb_fwd", compiler_params=_cp("arbitrary", "arbitrary"))(sq, sk, sv)


def _sb_bwd(sq, sk, sv, ltot, do, cut):
    T = sq.shape[0]
    blk = min(SB_BLK, T)
    w = min(SB_W, blk)
    nb, nsub = T // blk, blk // w

    def body(q_ref, k_ref, v_ref, lt_ref, do_ref, cut_ref, dq_ref, dk_ref, dv_ref, acc_ref, p_ref, g_ref):
        head, qi = pl.program_id(0), pl.program_id(1)
        first = cut_ref[head, qi].astype(jnp.int32)

        @pl.when(qi == 0)
        def _():
            dk_ref[...] = jnp.zeros_like(dk_ref)
            dv_ref[...] = jnp.zeros_like(dv_ref)

        acc_ref[...] = jnp.zeros_like(acc_ref)
        p_ref[...] = lt_ref[0]
        g_ref[...] = jnp.zeros_like(g_ref)

        def block(kj, masked):
            base = pl.multiple_of(kj * blk, blk)
            z, sp, mask = _sb_pair(q_ref[...], k_ref[pl.ds(base, blk), :], masked)
            d_a = _dot_nt(do_ref[...], v_ref[pl.ds(base, blk), :])
            after = (_iota((w, w), 0) > _iota((w, w), 1)).astype(BF16)
            before = (_iota((w, w), 0) < _iota((w, w), 1)).astype(BF16)
            rest = p_ref[...]
            hg = g_ref[...]
            acc = acc_ref[...]
            for sb in range(nsub):
                cs = slice(sb * w, (sb + 1) * w)
                sps, zs = sp[:, cs], z[:, cs]
                rest = rest - jnp.sum(sps, axis=-1, keepdims=True)
                a = jnp.exp(zs - sps - _dot(sps, after) - rest)
                if masked:
                    a = jnp.where(mask[:, cs], a, 0.0)
                g = a * d_a[:, cs]
                sig = jnp.exp(zs - sps)
                dz = g - sig * (g + (hg + _dot(g, before)))
                if masked:
                    dz = jnp.where(mask[:, cs], dz, 0.0)
                dz = dz.astype(BF16)
                rows = pl.ds(pl.multiple_of(base + sb * w, w), w)
                dv_ref[rows, :] += _dot_tn(a, do_ref[...])
                dk_ref[rows, :] += _dot_tn(dz, q_ref[...])
                acc = acc + _dot(dz, k_ref[rows, :])
                hg = hg + jnp.sum(g, axis=-1, keepdims=True)
            acc_ref[...] = acc
            p_ref[...] = rest
            g_ref[...] = hg

        def step(kj, carry):
            block(kj, False)
            return carry

        lax.fori_loop(first, qi, step, 0)
        block(qi, True)
        dq_ref[...] = acc_ref[...]

    qspec = pl.BlockSpec((blk, D_HEAD), lambda h, i: (i, h))
    whole = pl.BlockSpec((T, D_HEAD), lambda h, i: (0, h))
    return pl.pallas_call(
        body, out_shape=(jax.ShapeDtypeStruct((T, BR_W), F32),) * 3, grid=(N_HEAD, nb),
        in_specs=[qspec, whole, whole, pl.BlockSpec((1, blk, 1), lambda h, i: (h, i, 0)), qspec,
                  pl.BlockSpec(memory_space=pltpu.SMEM)],
        out_specs=(qspec, whole, whole),
        scratch_shapes=[pltpu.VMEM((blk, D_HEAD), F32), pltpu.VMEM((blk, 1), F32), pltpu.VMEM((blk, 1), F32)],
        name="sb_bwd", compiler_params=_cp("arbitrary", "arbitrary"))(sq, sk, sv, ltot, do, cut)


def _mem_kv(mem, gm, w_kv, gk):
    def body(mem_ref, gm_ref, w_ref, gk_ref, mn_ref, kv_ref, kh_ref, vm_ref):
        mv = mem_ref[...]
        r = lax.rsqrt(jnp.mean(mv * mv, axis=-1, keepdims=True) + EPS)
        mn = (mv * r * gm_ref[...]).astype(BF16)
        mn_ref[...] = mn
        kv = lax.dot_general(mn, w_ref[...], _NN, preferred_element_type=F32)
        kv_ref[...] = kv
        for h in range(N_HEAD):
            kh = kv[:, _hs(h)]
            rk = lax.rsqrt(jnp.mean(kh * kh, axis=-1, keepdims=True) + EPS)
            kh_ref[:, _hs(h)] = (kh * rk * gk_ref[...]).astype(BF16)
        vm_ref[...] = kv[:, BR_W:].astype(BF16)

    return pl.pallas_call(
        body,
        out_shape=(jax.ShapeDtypeStruct((N_MEM, D_MODEL), BF16), jax.ShapeDtypeStruct((N_MEM, 2 * BR_W), F32),
                   jax.ShapeDtypeStruct((N_MEM, BR_W), BF16), jax.ShapeDtypeStruct((N_MEM, BR_W), BF16)),
        name="mem_kv", compiler_params=_cp())(mem, gm, w_kv, gk)


def _mem_q(x_ref, gq_ref, h):
    xh = x_ref[:, _hs(h)].astype(F32)
    r = lax.rsqrt(jnp.mean(xh * xh, axis=-1, keepdims=True) + EPS)
    return r, xh * r


def _mem_probs(qn, kh):
    s = _dot_nt(qn, kh) * (D_HEAD ** -0.5)
    e = jnp.exp(s - jnp.max(s, axis=-1, keepdims=True))
    return e / jnp.sum(e, axis=-1, keepdims=True)


def _mem_fwd(proj, kh, vm, gq):
    T = proj.shape[0]
    tm = min(TM, T)

    def body(x_ref, kh_ref, vm_ref, gq_ref, o_ref):
        for h in range(N_HEAD):
            _, y = _mem_q(x_ref, gq_ref, h)
            p = _mem_probs((y * gq_ref[...]).astype(BF16), kh_ref[:, _hs(h)])
            o_ref[:, _hs(h)] = _dot(p, vm_ref[:, _hs(h)]).astype(BF16)

    return pl.pallas_call(
        body, out_shape=jax.ShapeDtypeStruct((T, BR_W), BF16), grid=(T // tm,),
        in_specs=[_rowspec(tm, BR_W, MEMQ_OFF // BR_W), _full((N_MEM, BR_W)), _full((N_MEM, BR_W)),
                  _full((1, D_HEAD))],
        out_specs=_rowspec(tm, BR_W), name="mem_fwd", compiler_params=_cp("parallel"))(proj, kh, vm, gq)


def _mem_bwd(proj, kh, vm, gq, do, into):
    T = proj.shape[0]
    tm = min(TM, T)

    def body(x_ref, kh_ref, vm_ref, gq_ref, do_ref, into_ref, dx_ref, dkh_ref, dvm_ref, dgq_ref):
        i = pl.program_id(0)

        @pl.when(i == 0)
        def _():
            dkh_ref[...] = jnp.zeros_like(dkh_ref)
            dvm_ref[...] = jnp.zeros_like(dvm_ref)
            dgq_ref[...] = jnp.zeros_like(dgq_ref)

        dg = jnp.zeros((1, D_HEAD), F32)
        for h in range(N_HEAD):
            r, y = _mem_q(x_ref, gq_ref, h)
            qn = (y * gq_ref[...]).astype(BF16)
            p = _mem_probs(qn, kh_ref[:, _hs(h)])
            dov = do_ref[:, _hs(h)]
            dp = _dot_nt(dov, vm_ref[:, _hs(h)])
            ds = p * (dp - jnp.sum(dp * p, axis=-1, keepdims=True)) * (D_HEAD ** -0.5)
            dqn = _dot(ds, kh_ref[:, _hs(h)])
            dkh_ref[:, _hs(h)] += _dot_tn(ds, qn)
            dvm_ref[:, _hs(h)] += _dot_tn(p, dov)
            dg = dg + jnp.sum(dqn * y, axis=0, keepdims=True)
            dy = dqn * gq_ref[...]
            dx_ref[:, _hs(h)] = (r * (dy - y * jnp.mean(dy * y, axis=-1, keepdims=True))).astype(BF16)
        dgq_ref[...] += dg

    return pl.pallas_call(
        body,
        out_shape=(jax.ShapeDtypeStruct(into.shape, BF16), jax.ShapeDtypeStruct((N_MEM, BR_W), F32),
                   jax.ShapeDtypeStruct((N_MEM, BR_W), F32), jax.ShapeDtypeStruct((1, D_HEAD), F32)),
        grid=(T // tm,),
        in_specs=[_rowspec(tm, BR_W, MEMQ_OFF // BR_W), _full((N_MEM, BR_W)), _full((N_MEM, BR_W)),
                  _full((1, D_HEAD)), _rowspec(tm, BR_W), HBM],
        out_specs=(_rowspec(tm, BR_W, MEMQ_OFF // BR_W), _full((N_MEM, BR_W)), _full((N_MEM, BR_W)),
                   _full((1, D_HEAD))),
        input_output_aliases={5: 0}, name="mem_bwd", compiler_params=_cp("arbitrary"))(proj, kh, vm, gq, do, into)


def _mem_kv_bwd(mem, gm, w_kv, gk, kv, mn, dkh, dvm):
    def body(mem_ref, gm_ref, w_ref, gk_ref, kv_ref, mn_ref, dkh_ref, dvm_ref, dw_ref, dgm_ref, dgk_ref, dkv_ref):
        dgk = jnp.zeros((1, D_HEAD), F32)
        for h in range(N_HEAD):
            kh = kv_ref[:, _hs(h)]
            r = lax.rsqrt(jnp.mean(kh * kh, axis=-1, keepdims=True) + EPS)
            y = kh * r
            dn = dkh_ref[:, _hs(h)]
            dgk = dgk + jnp.sum(dn * y, axis=0, keepdims=True)
            dy = dn * gk_ref[...]
            dkv_ref[:, _hs(h)] = (r * (dy - y * jnp.mean(dy * y, axis=-1, keepdims=True))).astype(BF16)
        dkv_ref[:, BR_W:] = dvm_ref[...].astype(BF16)
        dgk_ref[...] = dgk
        dkv = dkv_ref[...]
        dw_ref[...] = lax.dot_general(mn_ref[...], dkv, _TN, preferred_element_type=F32)
        dmn = lax.dot_general(dkv, w_ref[...], _NT, preferred_element_type=F32)
        mv = mem_ref[...]
        memn = mv * lax.rsqrt(jnp.mean(mv * mv, axis=-1, keepdims=True) + EPS)
        dgm_ref[...] = jnp.sum(dmn * memn, axis=0, keepdims=True)

    return pl.pallas_call(
        body,
        out_shape=(jax.ShapeDtypeStruct((D_MODEL, 2 * BR_W), F32), jax.ShapeDtypeStruct((1, D_MODEL), F32),
                   jax.ShapeDtypeStruct((1, D_HEAD), F32)),
        scratch_shapes=[pltpu.VMEM((N_MEM, 2 * BR_W), BF16)], name="mem_kv_bwd",
        compiler_params=_cp())(mem, gm, w_kv, gk, kv, mn, dkh, dvm)


def _merge_fwd(og, osb, om, proj, wg, ws, wm):
    T = og.shape[0]
    tm = min(TM, T)

    def body(og_ref, os_ref, om_ref, g0, g1, g2, wg_ref, ws_ref, wm_ref, mix_ref, yg_ref, ys_ref, ym_ref):
        mix = jnp.zeros((tm, D_MODEL), F32)
        for o_ref, gl_ref, w_ref, y_ref in ((og_ref, g0, wg_ref, yg_ref), (os_ref, g1, ws_ref, ys_ref),
                                            (om_ref, g2, wm_ref, ym_ref)):
            y = lax.dot_general(o_ref[...], w_ref[...], _NN, preferred_element_type=F32)
            y_ref[...] = y.astype(BF16)
            mix = mix + _sigmoid(gl_ref[...].astype(F32)) * y
        mix_ref[...] = mix.astype(BF16)

    br = _rowspec(tm, BR_W)
    wspec = _full((BR_W, D_MODEL))
    out = _rowspec(tm, D_MODEL)
    gates = [_rowspec(tm, D_MODEL, GATE_COL + b) for b in range(3)]
    return pl.pallas_call(
        body, out_shape=(jax.ShapeDtypeStruct((T, D_MODEL), BF16),) * 4, grid=(T // tm,),
        in_specs=[br, br, br, *gates, wspec, wspec, wspec],
        out_specs=(out,) * 4, name="merge_fwd",
        compiler_params=_cp("parallel"))(og, osb, om, proj, proj, proj, wg, ws, wm)


def _merge_bwd(dmix, proj, ys, os_, ws):
    T = dmix.shape[0]
    tm = min(TM, T)

    def body(dmix_ref, g0, g1, g2, y0, y1, y2, o0, o1, o2, w0, w1, w2, dgl_ref, do0, do1, do2, dw0, dw1, dw2):
        i = pl.program_id(0)
        dm = dmix_ref[...].astype(F32)
        for b, (gl_ref, y_ref, o_ref, w_ref, do_ref, dw_ref) in enumerate((
                (g0, y0, o0, w0, do0, dw0), (g1, y1, o1, w1, do1, dw1), (g2, y2, o2, w2, do2, dw2))):
            gate = _sigmoid(gl_ref[...].astype(F32))
            dgl_ref[:, b * D_MODEL:(b + 1) * D_MODEL] = (dm * y_ref[...].astype(F32) * gate * (1.0 - gate)).astype(BF16)
            dy = (gate * dm).astype(BF16)
            do_ref[...] = lax.dot_general(dy, w_ref[...], _NT, preferred_element_type=F32).astype(BF16)
            _accum(dw_ref, i == 0, lax.dot_general(dy, o_ref[...], _TN, preferred_element_type=F32))

    br = _rowspec(tm, BR_W)
    wide = _rowspec(tm, D_MODEL)
    wspec = _full((BR_W, D_MODEL))
    wtspec = _full((D_MODEL, BR_W))
    gates = [_rowspec(tm, D_MODEL, GATE_COL + b) for b in range(3)]
    return pl.pallas_call(
        body,
        out_shape=(jax.ShapeDtypeStruct((T, PROJ_W), BF16),) + (jax.ShapeDtypeStruct((T, BR_W), BF16),) * 3
        + (jax.ShapeDtypeStruct((D_MODEL, BR_W), F32),) * 3,
        grid=(T // tm,),
        in_specs=[wide, *gates, wide, wide, wide, br, br, br, wspec, wspec, wspec],
        out_specs=(_rowspec(tm, 3 * D_MODEL, GATE_OFF // (3 * D_MODEL)), br, br, br, wtspec, wtspec, wtspec),
        name="merge_bwd",
        compiler_params=_cp("arbitrary"))(dmix, proj, proj, proj, *ys, *os_, *ws)


def _loss(y, tgt):
    T, dm = y.shape
    tm = min(TM, T)

    def body(y_ref, t_ref, dy_ref, dyb_ref, sq_ref):
        err = y_ref[...] - t_ref[...]
        dy = err * (1.0 / dm)
        dy_ref[...] = dy
        dyb_ref[...] = dy.astype(BF16)
        _accum(sq_ref, pl.program_id(0) == 0, jnp.sum(err * err, axis=0, keepdims=True))

    return pl.pallas_call(
        body,
        out_shape=(jax.ShapeDtypeStruct((T, dm), F32), jax.ShapeDtypeStruct((T, dm), BF16),
                   jax.ShapeDtypeStruct((1, dm), F32)),
        grid=(T // tm,), in_specs=[_rowspec(tm, dm), _rowspec(tm, dm)],
        out_specs=(_rowspec(tm, dm), _rowspec(tm, dm), _full((1, dm))), name="loss",
        compiler_params=_cp("arbitrary"))(y, tgt)


def _split_w_in(slabs):
    width = slabs[0].shape[1]

    def cols(lo, hi):
        return [s[:, max(lo - j * width, 0):min(hi - j * width, width)] for j, s in enumerate(slabs)
                if lo < (j + 1) * width and hi > j * width]

    main = [c for piece in (IN_GATE, IN_QKV, IN_SB, IN_Z, IN_MEMQ) for c in cols(*piece)]
    ab = jnp.concatenate(cols(*IN_AB), axis=1)
    return jnp.concatenate(main, axis=1), jnp.pad(ab, ((0, 0), (0, LANES - ab.shape[1])))


def _local_step(x, mem, tgt, W, P, dist=None):
    w_main, w_ab = W["w_main"], W["w_ab"]
    avec = jnp.pad(jnp.concatenate([P["a_log"], P["dt_bias"]], axis=0), ((0, 0), (0, LANES - N_HEAD)))

    h = _rms_fwd(x, P["norm1_g"], "rms1")
    if dist is None:
        proj = _mm(h, w_main, "nn", BF16, "in_proj")
    else:
        proj, gathered = _mm(h, w_main, "nn", BF16, "in_proj", comm=dist.gather_rest())
        rest, conv_w = dist.weights_from(gathered)
        W, P = {**W, **rest}, {**P, "conv_w": conv_w}
    wbr = (W["w_br_gdn"], W["w_br_sb"], W["w_br_mem"])
    ab = _mm(h, w_ab, "nn", F32, "in_proj_ab")
    q, k, v, gb = _gdn_pre(proj, P["conv_w"], ab, avec)
    tinv, tinv_t = _gdn_inv(k, gb)
    og, oraw, shist, vn = _gdn_fwd(q, k, v, gb, proj, P["gdn_norm_g"], tinv)
    sq, sk, sv = _sb_pre(proj, P["sb_q_norm_g"], P["sb_k_norm_g"])
    osb, ltot, cut = _sb_fwd(sq, sk, sv)
    mn, kv, kh, vm = _mem_kv(mem, P["mem_norm_g"], W["w_mem_kv"], P["mem_k_norm_g"])
    om = _mem_fwd(proj, kh, vm, P["mem_q_norm_g"])
    mix, yg, ys, ym = _merge_fwd(og, osb, om, proj, *wbr)
    x1 = _mm(mix, W["w_o"], "nn", F32, "out_proj", extra=x, epi=_epi_add)
    h2 = _rms_fwd(x1, P["norm2_g"], "rms2")
    u = _mm(h2, W["w_up"], "nn", BF16, "mlp_up")
    y = _mm(u, W["w_down"], "nn", F32, "mlp_down", a_fn=_relu2, extra=x1, epi=_epi_add)
    dy, dyb, sq_err = _loss(y, tgt)

    G = {}
    du = _mm(dyb, W["w_down"], "nt", BF16, "d_mlp_act", extra=u, epi=_epi_drelu2)
    G["w_down"] = _mm(u, dyb, "tn", F32, "dw_down", a_fn=_relu2)
    G["w_up"] = _mm(du, h2, "tn", F32, "dw_up")
    dh2 = _mm(du, W["w_up"], "nt", F32, "d_h2")
    dx1, dx1b, G["norm2_g"] = _rms_bwd(dh2, x1, P["norm2_g"], dy, "rms2_bwd")
    dmix = _mm(dx1b, W["w_o"], "nt", BF16, "d_mix")
    G["w_o"] = _mm(mix, dx1b, "tn", F32, "dw_o")
    dproj, dog, dosb, dom, G["w_br_gdn"], G["w_br_sb"], G["w_br_mem"] = _merge_bwd(
        dmix, proj, (yg, ys, ym), (og, osb, om), wbr)
    dq, dk, dv, dgb, dproj, G["gdn_norm_g"] = _gdn_bwd(q, k, v, gb, proj, P["gdn_norm_g"], oraw, shist, tinv_t, vn, dog,
                                                      dproj)
    dxc, dab, G["conv_w"], dav = _gdn_pre_bwd(proj, P["conv_w"], ab, avec, dq, dk, dv, dgb)
    dproj = _conv_bwd(dxc, P["conv_w"], dproj)
    G["a_log"], G["dt_bias"] = dav[0:1, :N_HEAD], dav[1:2, :N_HEAD]
    dsq, dsk, dsv = _sb_bwd(sq, sk, sv, ltot, dosb, cut)
    dproj, G["sb_q_norm_g"], G["sb_k_norm_g"] = _sb_pre_bwd(proj, P["sb_q_norm_g"], P["sb_k_norm_g"], dsq, dsk, dsv,
                                                            dproj)
    dproj, dkh, dvm, G["mem_q_norm_g"] = _mem_bwd(proj, kh, vm, P["mem_q_norm_g"], dom, dproj)
    G["w_mem_kv"], G["mem_norm_g"], G["mem_k_norm_g"] = _mem_kv_bwd(
        mem, P["mem_norm_g"], W["w_mem_kv"], P["mem_k_norm_g"], kv, mn, dkh, dvm)
    dw_ab = _mm(dab, h, "tn", F32, "dw_in_ab")
    if dist is None:
        dw_main = _mm(dproj, h, "tn", F32, "dw_in")
    else:
        early = [n for n, _, _ in BIG if n != "w_in"]
        dw_main, landed = _mm(dproj, h, "tn", F32, "dw_in", comm=dist.scatter(G, early, "early"))
        dist.collect(early, landed)
    G["w_in"] = jnp.concatenate([dw_main[QKV_OFF:SB_OFF], dw_main[Z_OFF:MEMQ_OFF], dw_ab[:8], dw_main[SB_OFF:Z_OFF],
                                 dw_main[MEMQ_OFF:], dw_main[:QKV_OFF]], axis=0)
    if dist is None:
        dh = _mm(dproj, w_main, "nt", F32, "d_h")
    else:
        dh, landed = _mm(dproj, w_main, "nt", F32, "d_h", comm=dist.scatter(G, ["w_in"], "late"))
        dist.collect(["w_in"], landed)
    dh = _mm(dab, w_ab, "nt", F32, "d_h_ab", extra=dh, epi=_epi_add)
    dx, _, G["norm1_g"] = _rms_bwd(dh, x, P["norm1_g"], dx1, "rms1_bwd")
    return sq_err, dx, G


def _comm(name, ins, out_shapes, plan):
    n_in, n_out = len(ins), len(out_shapes)
    probe = plan([None] * n_in, [None] * n_out, 0, 0, 0, dry=True)
    n_copy = probe

    def body(*refs):
        in_refs, out_refs = refs[:n_in], refs[n_in:n_in + n_out]
        send_sems, recv_sems = refs[n_in + n_out:]
        x, y, c = lax.axis_index("x"), lax.axis_index("y"), lax.axis_index("c")
        copies = []
        for k, (src, dst, dev) in enumerate(plan(in_refs, out_refs, x, y, c, dry=False)):
            if dev is None:
                cp = pltpu.make_async_copy(src, dst, send_sems.at[k])
            else:
                cp = pltpu.make_async_remote_copy(src_ref=src, dst_ref=dst, send_sem=send_sems.at[k],
                                                  recv_sem=recv_sems.at[k], device_id=dev, device_id_type=MESH)
            cp.start()
            copies.append(cp)
        for cp in copies:
            cp.wait()

    return pl.pallas_call(
        body, out_shape=tuple(out_shapes), in_specs=[HBM] * n_in, out_specs=tuple([HBM] * n_out),
        scratch_shapes=[pltpu.SemaphoreType.DMA((n_copy,)), pltpu.SemaphoreType.DMA((n_copy,))], name=name)(*ins)


def _other_chips(x, y):
    return ((1 - x, y), (x, 1 - y), (1 - x, 1 - y))


def _gather_plan(parts, direct=()):
    n, every = len(parts), list(parts) + list(direct)

    def copies(ins, outs, send, recv, scratch):
        x, y, c = lax.axis_index("x"), lax.axis_index("y"), lax.axis_index("c")
        me = 2 * x + y
        chips = _other_chips(x, y)
        local_sems, staged = scratch[0], scratch[1:]

        def remote(src, dst, k, dev):
            return pltpu.make_async_remote_copy(src_ref=src, dst_ref=dst, send_sem=send.at[k], recv_sem=recv.at[k],
                                                device_id=dev, device_id_type=MESH)

        def half(p, ci):
            hr = ins[p].shape[0] // 2
            return pl.ds(pl.multiple_of(ci * hr, 16), hr)

        sent = [remote(ins[p].at[half(p, c)], outs[p].at[me, half(p, c)], 6 * p + f, (px, py, c))
                for p in range(n) for f, (px, py) in enumerate(chips)]
        sent += [remote(ins[p], outs[p].at[me], 6 * n + 3 * (p - n) + f, (px, py, c))
                 for p in range(n, len(every)) for f, (px, py) in enumerate(chips)]
        landed = [outs[p].at[2 * px + py, half(p, c)] for p in range(n) for px, py in chips]
        passed = [remote(landed[3 * p + f], landed[3 * p + f], 6 * p + 3 + f, (x, y, 1 - c))
                  for p in range(n) for f in range(3)]
        loads = [pltpu.make_async_copy(ins[p], staged[p], local_sems.at[2 * p]) for p in range(len(every))]
        stores = [pltpu.make_async_copy(staged[p], outs[p].at[me], local_sems.at[2 * p + 1]) for p in range(len(every))]
        return sent, passed, loads, stores

    def start(*refs):
        sent, _, loads, _ = copies(*refs)
        for cp in loads + sent:
            cp.start()

    def mid(*refs):
        sent, passed, loads, stores = copies(*refs)
        for ld, st in zip(loads, stores):
            ld.wait()
            st.start()
        for p in range(n):
            for f in range(3):
                sent[3 * p + f].wait_recv()
                passed[3 * p + f].start()

    def finish(*refs):
        sent, passed, _, stores = copies(*refs)
        for cp in sent[:3 * n]:
            cp.wait_send()
        for cp in passed + sent[3 * n:] + stores:
            cp.wait()

    return _Hosted(every, [jax.ShapeDtypeStruct((4,) + p.shape, p.dtype) for p in every], 6 * n + 3 * len(direct),
                   start, finish, mid,
                   [pltpu.SemaphoreType.DMA((2 * len(every),))] + [pltpu.VMEM(p.shape, p.dtype) for p in every])


def _scatter_plan(pairs):
    def copies(ins, outs, send, recv, scratch):
        x, y, c = lax.axis_index("x"), lax.axis_index("y"), lax.axis_index("c")
        me = 2 * x + y
        return [pltpu.make_async_remote_copy(src_ref=src.at[2 * px + py], dst_ref=dst.at[me], send_sem=send.at[3 * p + f],
                                             recv_sem=recv.at[3 * p + f], device_id=(px, py, c), device_id_type=MESH)
                for p, (src, dst) in enumerate(zip(ins, outs)) for f, (px, py) in enumerate(_other_chips(x, y))]

    def start(*refs):
        for cp in copies(*refs):
            cp.start()

    def finish(*refs):
        for cp in copies(*refs):
            cp.wait()

    return _Hosted(pairs, [jax.ShapeDtypeStruct(a.shape, a.dtype) for a in pairs], 3 * len(pairs), start, finish)


def _run_hosted(comm, name):
    n_in, n_out = len(comm.ins), len(comm.out_shapes)

    def body(*refs):
        args = (refs[:n_in], refs[n_in:n_in + n_out], refs[n_in + n_out], refs[n_in + n_out + 1], refs[n_in + n_out + 2:])
        comm.start(*args)
        if comm.mid is not None:
            comm.mid(*args)
        comm.finish(*args)

    sems = [pltpu.SemaphoreType.DMA((comm.n_sems,)), pltpu.SemaphoreType.DMA((comm.n_sems,))]
    return list(pl.pallas_call(
        body, out_shape=tuple(comm.out_shapes), in_specs=[HBM] * n_in, out_specs=tuple([HBM] * n_out),
        scratch_shapes=sems + comm.scratch, name=name, compiler_params=_cp())(*comm.ins))


def _swap_halves(slabs, name):
    n = len(slabs)

    def body(*refs):
        ins, outs = refs[:n], refs[n:2 * n]
        send, recv = refs[2 * n:]
        x, y, c = lax.axis_index("x"), lax.axis_index("y"), lax.axis_index("c")
        other = (x, y, 1 - c)
        for p in range(n):
            for j in range(4):
                pltpu.make_async_remote_copy(src_ref=ins[p].at[j, 1 - c], dst_ref=outs[p].at[j], send_sem=send.at[p],
                                             recv_sem=recv.at[p], device_id=other, device_id_type=MESH).start()
        for p in range(n):
            pltpu.make_async_remote_copy(src_ref=outs[p], dst_ref=outs[p], send_sem=send.at[p], recv_sem=recv.at[p],
                                         device_id=other, device_id_type=MESH).wait()

    shapes = [jax.ShapeDtypeStruct((4,) + s.shape[2:], s.dtype) for s in slabs]
    return pl.pallas_call(
        body, out_shape=tuple(shapes), in_specs=[HBM] * n, out_specs=tuple([HBM] * n),
        scratch_shapes=[pltpu.SemaphoreType.DMA((n,)), pltpu.SemaphoreType.DMA((n,))], name=name)(*slabs)


def _join_halves(both):
    n = len(both)

    def body(*refs):
        bufs = refs[n:2 * n]
        send, recv = refs[2 * n:]
        x, y, c = lax.axis_index("x"), lax.axis_index("y"), lax.axis_index("c")
        copies = []
        for p in range(n):
            cp = pltpu.make_async_remote_copy(src_ref=bufs[p].at[c], dst_ref=bufs[p].at[c], send_sem=send.at[p],
                                              recv_sem=recv.at[p], device_id=(x, y, 1 - c), device_id_type=MESH)
            cp.start()
            copies.append(cp)
        for cp in copies:
            cp.wait()

    return pl.pallas_call(
        body, out_shape=tuple(jax.ShapeDtypeStruct(a.shape, a.dtype) for a in both), in_specs=[HBM] * n,
        out_specs=tuple([HBM] * n), input_output_aliases={p: p for p in range(n)},
        scratch_shapes=[pltpu.SemaphoreType.DMA((n,)), pltpu.SemaphoreType.DMA((n,))], name="grad_join_cores")(*both)


def _gather_all(a, name):
    def plan(ins, outs, x, y, c, dry):
        if dry:
            return 8
        me = 4 * x + 2 * y + c
        copies = [(ins[0], outs[0].at[me], None)]
        for f in range(1, 8):
            peer = (1 - x if f & 4 else x, 1 - y if f & 2 else y, 1 - c if f & 1 else c)
            copies.append((ins[0], outs[0].at[me], peer))
        return copies

    return _comm(name, [a], [jax.ShapeDtypeStruct((8,) + a.shape, a.dtype)], plan)[0]


def _sum_slots(a, name, extra=None):
    n, R, _ = a.shape
    rb = min(ROW_BLK, R)

    def body(*refs):
        a_ref, o_ref = refs[0], refs[-1]
        acc = a_ref[0]
        for s in range(1, n):
            acc = acc + a_ref[s]
        if extra is not None:
            acc = acc + refs[1][...]
        o_ref[...] = acc

    ins = [a] + ([extra] if extra is not None else [])
    in_specs = [pl.BlockSpec((n, rb, LANES), lambda i: (0, i, 0))] + ([_rowspec(rb, LANES)] if extra is not None else [])
    return pl.pallas_call(
        body, out_shape=jax.ShapeDtypeStruct((R, LANES), F32), grid=(R // rb,), in_specs=in_specs,
        out_specs=_rowspec(rb, LANES), name=name, compiler_params=_cp("parallel"))(*ins)


def _pair_sum(slab, theirs, core, name):
    _, _, hr, C = slab.shape

    def body(c_ref, a_ref, b_ref, o_ref):
        o_ref[...] = (a_ref[...] + b_ref[...]).astype(BF16)

    return pl.pallas_call(
        body, out_shape=jax.ShapeDtypeStruct((4, hr, C), BF16),
        grid_spec=pltpu.PrefetchScalarGridSpec(
            num_scalar_prefetch=1, grid=(4,),
            in_specs=[pl.BlockSpec((None, None, hr, C), lambda j, c_ref: (j, c_ref[0], 0, 0)),
                      pl.BlockSpec((None, hr, C), lambda j, c_ref: (j, 0, 0))],
            out_specs=pl.BlockSpec((None, hr, C), lambda j, c_ref: (j, 0, 0))),
        name=name, compiler_params=_cp("parallel"))(core, slab, theirs)


def _chip_sum(recv, pairs, where, name):
    _, hr, C = recv.shape

    def body(w_ref, r_ref, p_ref, o_ref):
        me = w_ref[0]
        o_ref[...] = jnp.zeros_like(o_ref)
        for s in range(4):
            @pl.when(me == s)
            def _():
                o_ref[...] += p_ref[...].astype(F32)

            @pl.when(me != s)
            def _():
                o_ref[...] += r_ref[s].astype(F32)

    return pl.pallas_call(
        body, out_shape=jax.ShapeDtypeStruct((2, hr, C), F32),
        grid_spec=pltpu.PrefetchScalarGridSpec(
            num_scalar_prefetch=1, grid=(1,),
            in_specs=[pl.BlockSpec((4, hr, C), lambda i, w_ref: (0, 0, 0)),
                      pl.BlockSpec((None, hr, C), lambda i, w_ref: (w_ref[0], 0, 0))],
            out_specs=pl.BlockSpec((None, hr, C), lambda i, w_ref: (w_ref[1], 0, 0))),
        name=name, compiler_params=_cp("arbitrary"))(where, recv, pairs)


def _adamw(w, g, m, v, name):
    R, C = w.shape
    rb = min(ADAM_ROWS, R)
    c1 = 1.0 - ADAM_B1 ** ADAM_STEP
    c2 = 1.0 - ADAM_B2 ** ADAM_STEP

    def body(w_ref, g_ref, m_ref, v_ref, d_ref, nm_ref, nv_ref):
        gv = g_ref[...]
        nm = ADAM_B1 * m_ref[...] + (1.0 - ADAM_B1) * gv
        nv = ADAM_B2 * v_ref[...] + (1.0 - ADAM_B2) * (gv * gv)
        d_ref[...] = -ADAM_LR * ((nm / c1) / (jnp.sqrt(nv / c2) + ADAM_EPS) + ADAM_WD * w_ref[...])
        nm_ref[...] = nm
        nv_ref[...] = nv

    spec = _rowspec(rb, C)
    return pl.pallas_call(
        body, out_shape=(jax.ShapeDtypeStruct((R, C), F32),) * 3, grid=(R // rb,), in_specs=[spec] * 4,
        out_specs=(spec,) * 3, name=name, compiler_params=_cp("parallel"))(w, g, m, v)


class _Dist:
    def __init__(self, shards):
        self.shards = shards
        self.chip = 2 * lax.axis_index("x") + lax.axis_index("y")
        self.where = jnp.stack([self.chip, lax.axis_index("c")]).astype(jnp.int32)
        self.pairs, self.landed = {}, {}

    @staticmethod
    def _unshard(name, blk):
        _, (r, cc), axis = next(b for b in BIG if b[0] == name)
        return blk.reshape(4 * r, cc) if axis == 0 else blk.transpose(1, 0, 2).reshape(r, 4 * cc)

    def gather_first(self):
        got = _run_hosted(_gather_plan([self.shards["w_in"].astype(BF16)]), "gather_w_in")[0]
        return _split_w_in([got[j] for j in range(4)])

    def gather_rest(self):
        rest = [self.shards[n].astype(BF16) for n, _, _ in BIG if n != "w_in"]
        return _gather_plan(rest, [self.shards["conv_w"]])

    def weights_from(self, gathered):
        names = [n for n, _, _ in BIG if n != "w_in"]
        conv = gathered[-1]
        taps, width = conv.shape[1:]
        return ({n: self._unshard(n, g) for n, g in zip(names, gathered)},
                conv.transpose(1, 0, 2).reshape(taps, 4 * width))

    def scatter(self, G, names, tag):
        slabs = []
        for name, (r, cc), axis in BIG:
            if name not in names:
                continue
            g = G[name]
            if axis == 0:
                slabs.append(g.reshape(4, 2, r // 2, cc))
            else:
                slabs.append(g.reshape(4, 2, cc // 2, r))
        theirs = _swap_halves(slabs, "grad_swap_cores_" + tag)
        pairs = [_pair_sum(s, t, self.where[1:], "pair_sum_" + n) for s, t, n in zip(slabs, theirs, names)]
        self.pairs.update(zip(names, pairs))
        return _scatter_plan(pairs)

    def collect(self, names, landed):
        self.landed.update(zip(names, landed))

    def finish(self):
        names = [n for n, _, _ in BIG]
        halves = [_chip_sum(self.landed[n], self.pairs[n], self.where, "chip_sum_" + n) for n in names]
        out = {}
        for (name, (r, cc), axis), both in zip(BIG, _join_halves(halves)):
            full = both.reshape(-1, both.shape[-1])
            out[name] = full if axis == 0 else full.T
        return out


def _pack_rows(parts, rows, dtype):
    flat = jnp.concatenate([p.reshape(-1).astype(dtype) for p in parts])
    return jnp.pad(flat, (0, rows * LANES - flat.shape[0])).reshape(rows, LANES)


def _small_rows(n):
    return max(n // LANES, 1)


def _pack_small(vals):
    rows = []
    for name, n in SMALL:
        r = _small_rows(n)
        rows.append(jnp.pad(vals[name].reshape(-1), (0, r * LANES - n)).reshape(r, LANES))
    flat = jnp.concatenate(rows, axis=0)
    return jnp.pad(flat, ((0, SMALL_ROWS - flat.shape[0]), (0, 0)))


def _unpack_small(pack):
    out, r0 = {}, 0
    for name, n in SMALL:
        r = _small_rows(n)
        out[name] = pack[r0:r0 + r].reshape(-1)[:n]
        r0 += r
    return out


def kernel(x, mem, norm1_g, w_in, conv_w, a_log, dt_bias, gdn_norm_g, sb_q_norm_g, sb_k_norm_g, mem_norm_g, w_mem_kv, mem_q_norm_g, mem_k_norm_g, w_br_gdn, w_br_sb, w_br_mem, w_o, norm2_g, w_up, w_down, loss_target, m_norm1_g, m_w_in, m_conv_w, m_a_log, m_dt_bias, m_gdn_norm_g, m_sb_q_norm_g, m_sb_k_norm_g, m_mem_norm_g, m_w_mem_kv, m_mem_q_norm_g, m_mem_k_norm_g, m_w_br_gdn, m_w_br_sb, m_w_br_mem, m_w_o, m_norm2_g, m_w_up, m_w_down, v_norm1_g, v_w_in, v_conv_w, v_a_log, v_dt_bias, v_gdn_norm_g, v_sb_q_norm_g, v_sb_k_norm_g, v_mem_norm_g, v_w_mem_kv, v_mem_q_norm_g, v_mem_k_norm_g, v_w_br_gdn, v_w_br_sb, v_w_br_mem, v_w_o, v_norm2_g, v_w_up, v_w_down):
    wd = dict(norm1_g=norm1_g, w_in=w_in, conv_w=conv_w, a_log=a_log, dt_bias=dt_bias, gdn_norm_g=gdn_norm_g,
              sb_q_norm_g=sb_q_norm_g, sb_k_norm_g=sb_k_norm_g, mem_norm_g=mem_norm_g, w_mem_kv=w_mem_kv,
              mem_q_norm_g=mem_q_norm_g, mem_k_norm_g=mem_k_norm_g, w_br_gdn=w_br_gdn, w_br_sb=w_br_sb,
              w_br_mem=w_br_mem, w_o=w_o, norm2_g=norm2_g, w_up=w_up, w_down=w_down)
    md = dict(norm1_g=m_norm1_g, w_in=m_w_in, conv_w=m_conv_w, a_log=m_a_log, dt_bias=m_dt_bias,
              gdn_norm_g=m_gdn_norm_g, sb_q_norm_g=m_sb_q_norm_g, sb_k_norm_g=m_sb_k_norm_g,
              mem_norm_g=m_mem_norm_g, w_mem_kv=m_w_mem_kv, mem_q_norm_g=m_mem_q_norm_g,
              mem_k_norm_g=m_mem_k_norm_g, w_br_gdn=m_w_br_gdn, w_br_sb=m_w_br_sb, w_br_mem=m_w_br_mem, w_o=m_w_o,
              norm2_g=m_norm2_g, w_up=m_w_up, w_down=m_w_down)
    vd = dict(norm1_g=v_norm1_g, w_in=v_w_in, conv_w=v_conv_w, a_log=v_a_log, dt_bias=v_dt_bias,
              gdn_norm_g=v_gdn_norm_g, sb_q_norm_g=v_sb_q_norm_g, sb_k_norm_g=v_sb_k_norm_g,
              mem_norm_g=v_mem_norm_g, w_mem_kv=v_w_mem_kv, mem_q_norm_g=v_mem_q_norm_g,
              mem_k_norm_g=v_mem_k_norm_g, w_br_gdn=v_w_br_gdn, w_br_sb=v_w_br_sb, w_br_mem=v_w_br_mem, w_o=v_w_o,
              norm2_g=v_norm2_g, w_up=v_w_up, w_down=v_w_down)
    wd, md, vd = ({n: a[0] for n, a in d.items()} for d in (wd, md, vd))
    chip = 2 * lax.axis_index("x") + lax.axis_index("y")
    conv_shard = wd["conv_w"].shape

    dist = _Dist(wd)
    W = dict(zip(("w_main", "w_ab"), dist.gather_first()))
    P = {n: wd[n].reshape(1, -1) for n, _ in SMALL}

    sq_err, grad_x, G = _local_step(x[0], mem[0], loss_target[0], W, P, dist)
    loss = lax.psum(0.5 / D_MODEL * jnp.sum(sq_err), ("x", "y", "c"))

    g_big = dist.finish()

    spack = jnp.concatenate([_pack_small(G), G["conv_w"].reshape(CONV_ROWS, LANES)], axis=0)
    g_small = _sum_slots(_gather_all(spack, "gather_small_grads"), "small_grad_sum")
    g_conv_full = g_small[SMALL_ROWS:].reshape(conv_shard[0], 4 * conv_shard[1])
    g_conv = lax.dynamic_slice_in_dim(g_conv_full, chip * conv_shard[1], conv_shard[1], axis=1)

    grads, deltas, new_m, new_v = dict(g_big), {}, {}, {}
    for name, _, _ in BIG:
        deltas[name], new_m[name], new_v[name] = _adamw(wd[name], g_big[name], md[name], vd[name], "adamw_" + name)
    pack_sm = lambda d: jnp.concatenate([_pack_small(d), _pack_rows([d["conv_w"]], APACK_ROWS - SMALL_ROWS, F32)], axis=0)
    g_sm = jnp.concatenate([g_small[:SMALL_ROWS], _pack_rows([g_conv], APACK_ROWS - SMALL_ROWS, F32)], axis=0)
    small = (g_sm,) + _adamw(pack_sm(wd), g_sm, pack_sm(md), pack_sm(vd), "adamw_small")
    for out, pack in zip((grads, deltas, new_m, new_v), small):
        out.update(_unpack_small(pack[:SMALL_ROWS]))
        out["conv_w"] = pack[SMALL_ROWS:].reshape(-1)[:conv_shard[0] * conv_shard[1]].reshape(conv_shard)

    return (loss, grad_x[None], *[d[n][None] for d in (grads, deltas, new_m, new_v) for n in WEIGHTS])
```

```python
import jax
import jax.numpy as jnp
from jax import lax
from jax.experimental import pallas as pl
from jax.experimental.pallas import tpu as pltpu

F32 = jnp.float32
BF16 = jnp.bfloat16
MESH = pl.DeviceIdType.MESH

D_MODEL = 1024
N_HEAD = 4
D_HEAD = 128
BR_W = N_HEAD * D_HEAD
CONV_TAPS = 4
GDN_CHUNK = 64
INV_BLOCK = 16
INV_CHUNKS = 4
N_MEM = 256
D_FF = 4 * D_MODEL
EPS = 1e-6
LANES = 128
PROJ_W = 7168
GATE_OFF = 0
QKV_OFF = 3072
SB_OFF = 4608
Z_OFF = 6144
MEMQ_OFF = 6656
IN_GATE, IN_QKV, IN_SB, IN_Z, IN_MEMQ, IN_AB = (4104, 7176), (0, 1536), (2056, 3592), (1536, 2048), (3592, 4104), (2048, 2056)

ADAM_LR, ADAM_B1, ADAM_B2, ADAM_EPS, ADAM_WD, ADAM_STEP = 0.001, 0.9, 0.999, 1e-08, 0.01, 10

TM = 512
MM_TM = 1024
MM_TK = (2048, 1792, 1024, 128)
TK_TOK = 2048
GDN_STEP_CHUNKS = 4
GDN_BWD_STEP_CHUNKS = 1
G1_TM = 256
SB_BLK = 512
SB_DEAD = 120.0
VMEM_LIMIT = 48 << 20

BIG = (("w_in", (1024, 1794), 1), ("w_mem_kv", (256, 1024), 0), ("w_br_gdn", (512, 256), 1),
       ("w_br_sb", (512, 256), 1), ("w_br_mem", (512, 256), 1), ("w_o", (256, 1024), 0),
       ("w_up", (1024, 1024), 1), ("w_down", (1024, 1024), 0))
COL_SHARDED = tuple(n for n, _, a in BIG if a == 1)
GATE_COL = GATE_OFF // D_MODEL
QKV_COL = QKV_OFF // (3 * BR_W)
ROW_BLK = 1024
ADAM_ROWS = 128
SMALL = (("norm1_g", 1024), ("mem_norm_g", 1024), ("norm2_g", 1024), ("gdn_norm_g", 128), ("sb_q_norm_g", 128),
         ("sb_k_norm_g", 128), ("mem_q_norm_g", 128), ("mem_k_norm_g", 128), ("a_log", 4), ("dt_bias", 4))
SMALL_ROWS = 32
CONV_ROWS = 48
SPACK_ROWS = SMALL_ROWS + CONV_ROWS
APACK_ROWS = SMALL_ROWS + 16

WEIGHTS = ("norm1_g", "w_in", "conv_w", "a_log", "dt_bias", "gdn_norm_g", "sb_q_norm_g", "sb_k_norm_g",
           "mem_norm_g", "w_mem_kv", "mem_q_norm_g", "mem_k_norm_g", "w_br_gdn", "w_br_sb", "w_br_mem", "w_o",
           "norm2_g", "w_up", "w_down")


def _cp(*sem):
    return pltpu.CompilerParams(dimension_semantics=sem if sem else None, vmem_limit_bytes=VMEM_LIMIT)


HBM = pl.BlockSpec(memory_space=pl.ANY)

_NN = (((1,), (0,)), ((), ()))
_NT = (((1,), (1,)), ((), ()))
_TN = (((0,), (0,)), ((), ()))


def _dot(a, b, dims=_NN):
    return lax.dot_general(a.astype(BF16), b.astype(BF16), dims, preferred_element_type=F32)


def _dot_nt(a, b):
    return _dot(a, b, _NT)


def _dot_tn(a, b):
    return _dot(a, b, _TN)


def _dotf(a, b, dims=_NN):
    return lax.dot_general(a, b, dims, precision=lax.Precision.HIGHEST, preferred_element_type=F32)


def _sigmoid(v):
    return 0.5 * jnp.tanh(0.5 * v) + 0.5


def _softplus(v):
    return jnp.maximum(v, 0.0) + jnp.log(1.0 + jnp.exp(-jnp.abs(v)))


def _iota(shape, dim):
    return lax.broadcasted_iota(jnp.int32, shape, dim)


def _hs(h):
    return slice(h * D_HEAD, (h + 1) * D_HEAD)


def _rowspec(tm, w, col=0):
    return pl.BlockSpec((tm, w), lambda i: (i, col))


def _full(shape):
    return pl.BlockSpec(shape, lambda *_: (0,) * len(shape))


def _accum(ref, first, val):
    @pl.when(first)
    def _():
        ref[...] = val

    @pl.when(jnp.logical_not(first))
    def _():
        ref[...] += val


class _Hosted:
    def __init__(self, ins, out_shapes, n_sems, start, finish, mid=None, scratch=()):
        self.ins, self.out_shapes, self.n_sems = list(ins), list(out_shapes), n_sems
        self.start, self.mid, self.finish, self.scratch = start, mid, finish, list(scratch)


def _mm(a, b, mode, out_dtype, name, *, tm=None, tn=None, tk=None, a_fn=None, extra=None, epi=None, comm=None):
    if mode == "tn":
        (K, M), N = a.shape, b.shape[1]
    else:
        (M, K), N = a.shape, (b.shape[0] if mode == "nt" else b.shape[1])
    tm = min(tm or (1024 if mode == "tn" else MM_TM), M)
    tn = min(tn or 1024, N)
    tk = min(tk or (TK_TOK if mode == "tn" else next(t for t in MM_TK if K % t == 0)), K)
    nm, nn, nk = M // tm, N // tn, K // tk
    assert nm * tm == M and nn * tn == N and nk * tk == K, (name, a.shape, b.shape)
    if mode == "tn":
        a_spec = pl.BlockSpec((tk, tm), lambda i, j, k: (k, i))
    else:
        a_spec = pl.BlockSpec((tm, tk), lambda i, j, k: (i, k))
    if mode == "nt":
        b_spec = pl.BlockSpec((tn, tk), lambda i, j, k: (j, k))
    else:
        b_spec = pl.BlockSpec((tk, tn), lambda i, j, k: (k, j))
    dims = {"nn": _NN, "nt": _NT, "tn": _TN}[mode]
    o_spec = pl.BlockSpec((tm, tn), lambda i, j, k: (i, j))
    has_extra = extra is not None

    n_ci, n_co = (len(comm.ins), len(comm.out_shapes)) if comm else (0, 0)
    n_in = 2 + has_extra + n_ci
    steps = nm * nn * nk

    def body(*refs):
        a_ref, b_ref = refs[0], refs[1]
        e_ref = refs[2] if has_extra else None
        o_ref = refs[n_in]
        scratch = refs[n_in + 1 + n_co:]
        if comm:
            step = (pl.program_id(0) * nn + pl.program_id(1)) * nk + pl.program_id(2)
            cargs = (refs[2 + has_extra:n_in], refs[n_in + 1:n_in + 1 + n_co], scratch[nk > 1], scratch[(nk > 1) + 1],
                     scratch[(nk > 1) + 2:])
            pl.when(step == 0)(lambda: comm.start(*cargs))
            if comm.mid is not None:
                pl.when(step == (steps * 7) // 8)(lambda: comm.mid(*cargs))
        av = a_ref[...]
        if a_fn is not None:
            av = a_fn(av)
        p = lax.dot_general(av, b_ref[...], dims, preferred_element_type=F32)

        def finish(acc):
            if epi is not None:
                acc = epi(acc, e_ref[...] if has_extra else None)
            o_ref[...] = acc.astype(out_dtype)

        if nk == 1:
            finish(p)
        else:
            acc_ref = scratch[0]
            k = pl.program_id(2)
            _accum(acc_ref, k == 0, p)

            @pl.when(k == nk - 1)
            def _():
                finish(acc_ref[...])

        if comm:
            pl.when(step == steps - 1)(lambda: comm.finish(*cargs))

    ins = [a, b] + ([extra] if has_extra else [])
    in_specs = [a_spec, b_spec] + ([o_spec] if has_extra else [])
    scratch_shapes = [pltpu.VMEM((tm, tn), F32)] if nk > 1 else []
    main = jax.ShapeDtypeStruct((M, N), out_dtype)
    if not comm:
        return pl.pallas_call(
            body, out_shape=main, grid=(nm, nn, nk), in_specs=in_specs, out_specs=o_spec,
            scratch_shapes=scratch_shapes, name=name, compiler_params=_cp("parallel", "parallel", "arbitrary"))(*ins)
    sems = [pltpu.SemaphoreType.DMA((comm.n_sems,)), pltpu.SemaphoreType.DMA((comm.n_sems,))]
    res = pl.pallas_call(
        body, out_shape=(main, *comm.out_shapes), grid=(nm, nn, nk), in_specs=in_specs + [HBM] * n_ci,
        out_specs=(o_spec, *[HBM] * n_co), scratch_shapes=scratch_shapes + sems + comm.scratch, name=name,
        compiler_params=_cp("arbitrary", "arbitrary", "arbitrary"))(*ins, *comm.ins)
    return res[0], list(res[1:])


def _relu2(u):
    r = jnp.maximum(u.astype(F32), 0.0)
    return (r * r).astype(BF16)


def _epi_add(acc, e):
    return acc + e.astype(F32)


def _epi_drelu2(acc, u):
    return acc * (2.0 * jnp.maximum(u.astype(F32), 0.0))


def _rms_fwd(x, g, name):
    T, dm = x.shape
    tm = min(TM, T)

    def body(x_ref, g_ref, h_ref):
        xv = x_ref[...]
        r = lax.rsqrt(jnp.mean(xv * xv, axis=-1, keepdims=True) + EPS)
        h_ref[...] = (xv * r * g_ref[...]).astype(BF16)

    return pl.pallas_call(
        body, out_shape=jax.ShapeDtypeStruct((T, dm), BF16), grid=(T // tm,),
        in_specs=[_rowspec(tm, dm), _full((1, dm))], out_specs=_rowspec(tm, dm), name=name,
        compiler_params=_cp("parallel"))(x, g)


def _rms_bwd(dh, x, g, resid, name):
    T, dm = x.shape
    tm = min(TM, T)

    def body(dh_ref, x_ref, g_ref, res_ref, dx_ref, dxb_ref, dg_ref):
        i = pl.program_id(0)
        xv = x_ref[...]
        r = lax.rsqrt(jnp.mean(xv * xv, axis=-1, keepdims=True) + EPS)
        y = xv * r
        dhv = dh_ref[...].astype(F32)
        dy = dhv * g_ref[...]
        dx = res_ref[...] + r * (dy - y * jnp.mean(dy * y, axis=-1, keepdims=True))
        dx_ref[...] = dx
        dxb_ref[...] = dx.astype(BF16)
        _accum(dg_ref, i == 0, jnp.sum(dhv * y, axis=0, keepdims=True))

    return pl.pallas_call(
        body,
        out_shape=(jax.ShapeDtypeStruct((T, dm), F32), jax.ShapeDtypeStruct((T, dm), BF16),
                   jax.ShapeDtypeStruct((1, dm), F32)),
        grid=(T // tm,),
        in_specs=[_rowspec(tm, dm), _rowspec(tm, dm), _full((1, dm)), _rowspec(tm, dm)],
        out_specs=(_rowspec(tm, dm), _rowspec(tm, dm), _full((1, dm))), name=name,
        compiler_params=_cp("arbitrary"))(dh, x, g, resid)


def _conv_tile(x_ref, halo_ref, w_ref, xpad, tm):
    i = pl.program_id(0)
    halo = halo_ref[...].astype(F32)[8:16]
    xpad[0:8, :] = jnp.where(i > 0, halo, 0.0)
    xpad[8:, :] = x_ref[...].astype(F32)
    w = w_ref[...]
    xc = w[0:1] * xpad[5:5 + tm, :]
    for j in range(1, CONV_TAPS):
        xc = xc + w[j:j + 1] * xpad[5 + j:5 + j + tm, :]
    return xc


def _gate_terms(ab_ref, av_ref):
    abv = ab_ref[...]
    av = av_ref[...]
    pre = abv + av[1:2]
    ea = jnp.exp(av[0:1])
    g = -ea * _softplus(pre)
    return abv, pre, ea, g


def _gdn_pre(proj, conv_w, ab, avec):
    T = proj.shape[0]
    tm = min(G1_TM, T)
    cw = 3 * BR_W

    def body(x_ref, halo_ref, w_ref, ab_ref, av_ref, q_ref, k_ref, v_ref, gb_ref, xpad):
        xc = _conv_tile(x_ref, halo_ref, w_ref, xpad, tm)
        y = xc * _sigmoid(xc)
        for h in range(N_HEAD):
            for off, ref, scale in ((0, q_ref, D_HEAD ** -0.5), (BR_W, k_ref, 1.0)):
                yh = y[:, off + h * D_HEAD:off + (h + 1) * D_HEAD]
                r = lax.rsqrt(jnp.sum(yh * yh, axis=-1, keepdims=True) + EPS)
                ref[:, _hs(h)] = yh * (r * scale)
        v_ref[...] = y[:, 2 * BR_W:]
        abv, _, _, g = _gate_terms(ab_ref, av_ref)
        lane = _iota((tm, LANES), 1)
        gb_ref[...] = jnp.where(lane < N_HEAD, g, jnp.where(lane < 2 * N_HEAD, _sigmoid(abv), 0.0))

    hb = tm // 16
    return pl.pallas_call(
        body,
        out_shape=(jax.ShapeDtypeStruct((T, BR_W), F32),) * 3 + (jax.ShapeDtypeStruct((T, LANES), F32),),
        grid=(T // tm,),
        in_specs=[_rowspec(tm, cw, QKV_COL), pl.BlockSpec((16, cw), lambda i: (jnp.maximum(i * hb - 1, 0), QKV_COL)),
                  _full((CONV_TAPS, cw)), _rowspec(tm, LANES), _full((2, LANES))],
        out_specs=(_rowspec(tm, BR_W),) * 3 + (_rowspec(tm, LANES),),
        scratch_shapes=[pltpu.VMEM((tm + 8, cw), F32)], name="gdn_pre",
        compiler_params=_cp("parallel"))(proj, proj, conv_w, ab, avec)


def _gdn_pre_bwd(proj, conv_w, ab, avec, dq, dk, dv, dgb):
    T = proj.shape[0]
    tm = min(G1_TM, T)
    cw = 3 * BR_W

    def body(x_ref, halo_ref, w_ref, ab_ref, av_ref, dq_ref, dk_ref, dv_ref, dgb_ref,
             dxc_ref, dab_ref, dcw_ref, dav_ref, xpad):
        i = pl.program_id(0)

        @pl.when(i == 0)
        def _():
            dcw_ref[...] = jnp.zeros_like(dcw_ref)
            dav_ref[...] = jnp.zeros_like(dav_ref)

        xc_all = _conv_tile(x_ref, halo_ref, w_ref, xpad, tm)
        for s in range(cw // D_HEAD):
            cs = slice(s * D_HEAD, (s + 1) * D_HEAD)
            xc = xc_all[:, cs]
            sg = _sigmoid(xc)
            yh = xc * sg
            h = s % N_HEAD
            if s < 2 * N_HEAD:
                dref, scale = (dq_ref, D_HEAD ** -0.5) if s < N_HEAD else (dk_ref, 1.0)
                r = lax.rsqrt(jnp.sum(yh * yh, axis=-1, keepdims=True) + EPS)
                yn = yh * r
                dn = dref[:, _hs(h)]
                dy = (scale * r) * (dn - yn * jnp.sum(yn * dn, axis=-1, keepdims=True))
            else:
                dy = dv_ref[:, _hs(h)]
            dxc = dy * (sg * (1.0 + xc * (1.0 - sg)))
            dxc_ref[:, cs] = dxc.astype(BF16)
            for j in range(CONV_TAPS):
                dcw_ref[j:j + 1, cs] += jnp.sum(dxc * xpad[5 + j:5 + j + tm, cs], axis=0, keepdims=True)

        abv, pre, ea, g = _gate_terms(ab_ref, av_ref)
        dgbv = dgb_ref[...]
        lane = _iota((tm, LANES), 1)
        is_a = lane < N_HEAD
        da = jnp.where(is_a, dgbv * (-ea) * _sigmoid(pre), 0.0)
        bs = _sigmoid(abv)
        db = jnp.where(jnp.logical_and(lane >= N_HEAD, lane < 2 * N_HEAD), dgbv * bs * (1.0 - bs), 0.0)
        dab_ref[...] = (da + db).astype(BF16)
        dav_ref[0:1, :] += jnp.sum(jnp.where(is_a, dgbv * g, 0.0), axis=0, keepdims=True)
        dav_ref[1:2, :] += jnp.sum(da, axis=0, keepdims=True)

    hb = tm // 16
    return pl.pallas_call(
        body,
        out_shape=(jax.ShapeDtypeStruct((T, cw), BF16), jax.ShapeDtypeStruct((T, LANES), BF16),
                   jax.ShapeDtypeStruct((CONV_TAPS, cw), F32), jax.ShapeDtypeStruct((2, LANES), F32)),
        grid=(T // tm,),
        in_specs=[_rowspec(tm, cw, QKV_COL), pl.BlockSpec((16, cw), lambda i: (jnp.maximum(i * hb - 1, 0), QKV_COL)),
                  _full((CONV_TAPS, cw)), _rowspec(tm, LANES), _full((2, LANES)),
                  _rowspec(tm, BR_W), _rowspec(tm, BR_W), _rowspec(tm, BR_W), _rowspec(tm, LANES)],
        out_specs=(_rowspec(tm, cw), _rowspec(tm, LANES), _full((CONV_TAPS, cw)), _full((2, LANES))),
        scratch_shapes=[pltpu.VMEM((tm + 8, cw), F32)], name="gdn_pre_bwd",
        compiler_params=_cp("arbitrary"))(proj, proj, conv_w, ab, avec, dq, dk, dv, dgb)


def _conv_bwd(dxc, conv_w, into):
    T, cw = dxc.shape
    tm = min(G1_TM, T)
    nt = T // tm
    hb = tm // 16

    def body(d_ref, halo_ref, w_ref, into_ref, dx_ref, xpad):
        i = pl.program_id(0)
        xpad[0:tm, :] = d_ref[...].astype(F32)
        xpad[tm:, :] = jnp.where(i < nt - 1, halo_ref[...].astype(F32)[0:8], 0.0)
        w = w_ref[...]
        dx = w[3:4] * xpad[0:tm, :]
        for j in range(CONV_TAPS - 1):
            dx = dx + w[j:j + 1] * xpad[3 - j:3 - j + tm, :]
        dx_ref[...] = dx.astype(BF16)

    return pl.pallas_call(
        body, out_shape=jax.ShapeDtypeStruct(into.shape, BF16), grid=(nt,),
        in_specs=[_rowspec(tm, cw), pl.BlockSpec((16, cw), lambda i: (jnp.minimum((i + 1) * hb, T // 16 - 1), 0)),
                  _full((CONV_TAPS, cw)), HBM],
        out_specs=_rowspec(tm, cw, QKV_COL), scratch_shapes=[pltpu.VMEM((tm + 8, cw), F32)],
        input_output_aliases={3: 0}, name="conv_bwd", compiler_params=_cp("parallel"))(dxc, dxc, conv_w, into)


def _chunk_consts():
    C = GDN_CHUNK
    row, col = _iota((C, C), 0), _iota((C, C), 1)
    return row, col, row >= col, row > col


def _chunk_decay(gbv, incl):
    c_all = _dotf(incl.astype(F32), gbv)
    c_t = jnp.concatenate([c_all, jnp.zeros_like(c_all)], axis=0).T[:, :GDN_CHUNK]
    return c_all, c_t


def _head_decay(c_all, c_t, gbv, incl, h):
    C = GDN_CHUNK
    c_col = c_all[:, h:h + 1]
    c_row = c_t[h:h + 1, :]
    gam = jnp.exp(jnp.where(incl, c_col - c_row, -1e30))
    c_last = c_all[C - 1:C, h:h + 1]
    return gam, jnp.exp(c_col), jnp.exp(c_last - c_col), jnp.exp(c_last), gbv[:, N_HEAD + h:N_HEAD + h + 1]


def _split_bf16(x):
    hi = x.astype(BF16)
    return hi, (x - hi.astype(F32)).astype(BF16)


def _dot3(a, b):
    ah, al = _split_bf16(a)
    bh, bl = _split_bf16(b)
    d = lambda u, v: lax.dot_general(u, v, _NN, preferred_element_type=F32)
    return d(ah, bh) + (d(ah, bl) + d(al, bh))


def _unit_lower_inverses(ms, row, col):
    bi, bj = row // INV_BLOCK, col // INV_BLOCK
    eye = (row == col).astype(F32)
    ns = [jnp.where(bi == bj, -m, 0.0) for m in ms]
    invs = [eye + n for n in ns]
    size = 2
    while size < INV_BLOCK:
        ns = [_dot3(n, n) for n in ns]
        invs = [inv + _dot3(inv, n) for inv, n in zip(invs, ns)]
        size *= 2
    width = 2
    while width * INV_BLOCK <= GDN_CHUNK:
        sel = jnp.logical_and(bi // width == bj // width, bi // (width // 2) > bj // (width // 2))
        ts = [_dot3(inv, jnp.where(sel, m, 0.0)) for inv, m in zip(invs, ms)]
        invs = [inv - _dot3(t, inv) for inv, t in zip(invs, ts)]
        width *= 2
    return invs


def _gdn_inv(k, gb):
    T = k.shape[0]
    C = GDN_CHUNK
    per = min(INV_CHUNKS, T // C)
    rows = per * C

    def body(k_ref, gb_ref, ti_ref, tt_ref):
        row, col, incl, strict = _chunk_consts()
        ms = []
        for ci in range(per):
            rs = slice(ci * C, (ci + 1) * C)
            gbv = gb_ref[rs, :]
            c_all, c_t = _chunk_decay(gbv, incl)
            for h in range(N_HEAD):
                gam, _, _, _, bcol = _head_decay(c_all, c_t, gbv, incl, h)
                K = k_ref[rs, _hs(h)]
                ms.append(jnp.where(strict, _dot_nt(K * bcol, K) * gam, 0.0))
        eye = (row == col).astype(BF16)
        for i, inv in enumerate(_unit_lower_inverses(ms, row, col)):
            ti_ref[i // N_HEAD, i % N_HEAD] = inv
            tt_ref[i // N_HEAD, i % N_HEAD] = _dot_tn(inv, eye).astype(BF16)

    spec = pl.BlockSpec((per, N_HEAD, C, C), lambda i: (i, 0, 0, 0))
    return pl.pallas_call(
        body, out_shape=(jax.ShapeDtypeStruct((T // C, N_HEAD, C, C), F32),
                         jax.ShapeDtypeStruct((T // C, N_HEAD, C, C), BF16)),
        grid=(T // rows,), in_specs=[_rowspec(rows, BR_W), _rowspec(rows, LANES)], out_specs=(spec, spec),
        name="gdn_inv", compiler_params=_cp("parallel"))(k, gb)


def _gdn_fwd(q, k, v, gb, proj, gnorm, tinv_all):
    T = q.shape[0]
    C = GDN_CHUNK
    nc = T // C
    per = min(GDN_STEP_CHUNKS, nc)
    zcol = Z_OFF // BR_W
    heads = range(N_HEAD)

    def body(q_ref, k_ref, v_ref, gb_ref, z_ref, gn_ref, ti_ref, og_ref, oraw_ref, sh_ref, vn_ref, s_ref):
        @pl.when(pl.program_id(0) == 0)
        def _():
            s_ref[...] = jnp.zeros_like(s_ref)

        _, _, incl, _ = _chunk_consts()
        S = [s_ref[h] for h in heads]
        for ci in range(per):
            rs = slice(ci * C, (ci + 1) * C)
            gbv = gb_ref[rs, :]
            c_all, c_t = _chunk_decay(gbv, incl)
            dec = [_head_decay(c_all, c_t, gbv, incl, h) for h in heads]
            gam, gcol, dcol, glast, bcol = ([d[i] for d in dec] for i in range(5))
            Q = [q_ref[rs, _hs(h)] for h in heads]
            K = [k_ref[rs, _hs(h)] for h in heads]
            V = [v_ref[rs, _hs(h)] for h in heads]
            Sb = [s.astype(BF16) for s in S]
            KS = [_dot(K[h], Sb[h]) for h in heads]
            QS = [_dot(Q[h], Sb[h]) for h in heads]
            P = [_dot_nt(Q[h], K[h]) * gam[h] for h in heads]
            R = [bcol[h] * (V[h] - gcol[h] * KS[h]) for h in heads]
            vn = [_dot(ti_ref[ci, h], R[h]) for h in heads]
            O = [gcol[h] * QS[h] + _dot(P[h], vn[h]) for h in heads]
            Sn = [glast[h] * S[h] + _dot_tn(K[h] * dcol[h], vn[h]) for h in heads]
            for h in heads:
                sh_ref[ci, h] = S[h]
                vn_ref[rs, _hs(h)] = vn[h]
                oraw_ref[rs, _hs(h)] = O[h]
                rr = lax.rsqrt(jnp.mean(O[h] * O[h], axis=-1, keepdims=True) + EPS)
                zz = z_ref[rs, _hs(h)].astype(F32)
                og_ref[rs, _hs(h)] = (O[h] * rr * gn_ref[...] * (zz * _sigmoid(zz))).astype(BF16)
            S = Sn
        for h in heads:
            s_ref[h] = S[h]

    cspec = lambda w, cb=0: pl.BlockSpec((per * C, w), lambda n: (n, cb))
    hist = lambda a, b: pl.BlockSpec((per, N_HEAD, a, b), lambda n: (n, 0, 0, 0))
    return pl.pallas_call(
        body,
        out_shape=(jax.ShapeDtypeStruct((T, BR_W), BF16), jax.ShapeDtypeStruct((T, BR_W), F32),
                   jax.ShapeDtypeStruct((nc, N_HEAD, D_HEAD, D_HEAD), F32), jax.ShapeDtypeStruct((T, BR_W), F32)),
        grid=(nc // per,),
        in_specs=[cspec(BR_W), cspec(BR_W), cspec(BR_W), cspec(LANES), cspec(BR_W, zcol), _full((1, D_HEAD)),
                  hist(C, C)],
        out_specs=(cspec(BR_W), cspec(BR_W), hist(D_HEAD, D_HEAD), cspec(BR_W)),
        scratch_shapes=[pltpu.VMEM((N_HEAD, D_HEAD, D_HEAD), F32)], name="gdn_chunk_fwd",
        compiler_params=_cp("arbitrary"))(q, k, v, gb, proj, gnorm, tinv_all)


def _gdn_bwd(q, k, v, gb, proj, gnorm, oraw, shist, tinv_all, vn_all, dog, into):
    T = q.shape[0]
    C = GDN_CHUNK
    nc = T // C
    per = min(GDN_BWD_STEP_CHUNKS, nc)
    zcol = Z_OFF // BR_W

    def body(q_ref, k_ref, v_ref, gb_ref, z_ref, gn_ref, oraw_ref, sh_ref, tt_ref, vn_ref, dog_ref, into_ref,
             dq_ref, dk_ref, dv_ref, dgb_ref, dz_ref, dgn_ref, ds_ref):
        @pl.when(pl.program_id(0) == 0)
        def _():
            ds_ref[...] = jnp.zeros_like(ds_ref)
            dgn_ref[...] = jnp.zeros_like(dgn_ref)

        row, col, incl, strict = _chunk_consts()
        lane = _iota((C, LANES), 1)
        rowl = _iota((C, LANES), 0)
        eye = (row == col).astype(F32)
        upper = (col >= row).astype(F32)
        gn = gn_ref[...]
        heads = range(N_HEAD)
        rsum = lambda a: jnp.sum(a, axis=-1, keepdims=True)
        dgn = jnp.zeros((1, D_HEAD), F32)
        dSn = [ds_ref[h] for h in heads]
        for ci in reversed(range(per)):
            rs = slice(ci * C, (ci + 1) * C)
            gbv = gb_ref[rs, :]
            c_all, c_t = _chunk_decay(gbv, incl)
            dec = [_head_decay(c_all, c_t, gbv, incl, h) for h in heads]
            gam, gcol, dcol, glast, bcol = ([d[i] for d in dec] for i in range(5))
            Q = [q_ref[rs, _hs(h)] for h in heads]
            K = [k_ref[rs, _hs(h)] for h in heads]
            V = [v_ref[rs, _hs(h)] for h in heads]
            dO = []
            for h in heads:
                O = oraw_ref[rs, _hs(h)]
                zz = z_ref[rs, _hs(h)].astype(F32)
                dogv = dog_ref[rs, _hs(h)].astype(F32)
                rr = lax.rsqrt(jnp.mean(O * O, axis=-1, keepdims=True) + EPS)
                on = O * rr
                sg = _sigmoid(zz)
                dz_ref[rs, _hs(h)] = (dogv * on * gn * (sg * (1.0 + zz * (1.0 - sg)))).astype(BF16)
                dyn = dogv * (zz * sg)
                dgn = dgn + jnp.sum(dyn * on, axis=0, keepdims=True)
                dyv = dyn * gn
                dO.append((rr * (dyv - on * jnp.mean(dyv * on, axis=-1, keepdims=True))).astype(BF16))
            S = [sh_ref[ci, h] for h in heads]
            Sb = [s.astype(BF16) for s in S]
            vn = [vn_ref[rs, _hs(h)] for h in heads]
            vnb = [a.astype(BF16) for a in vn]
            dSb = [a.astype(BF16) for a in dSn]
            Kb = [K[h] * bcol[h] for h in heads]
            gam_t = [jnp.exp(jnp.where(col >= row, c_t[h:h + 1, :] - c_all[:, h:h + 1], -1e30)) for h in heads]
            M = [jnp.where(strict, _dot_nt(Kb[h], K[h]) * gam[h], 0.0) for h in heads]
            P = [_dot_nt(Q[h], K[h]) * gam[h] for h in heads]
            P_t = [_dot_nt(K[h], Q[h]) * gam_t[h] for h in heads]
            KS = [_dot(K[h], Sb[h]) for h in heads]
            QS = [_dot(Q[h], Sb[h]) for h in heads]
            dvn = [_dot(P_t[h], dO[h]) + _dot(K[h] * dcol[h], dSb[h]) for h in heads]
            dR = [_dot(tt_ref[ci, h], dvn[h]) for h in heads]
            dRb = [a.astype(BF16) for a in dR]
            bg = [bcol[h] * gcol[h] for h in heads]
            dS_new = [glast[h] * dSn[h] + _dot_tn(gcol[h] * Q[h], dO[h]) - _dot_tn(bg[h] * K[h], dRb[h])
                      for h in heads]
            dP = [jnp.where(incl, _dot_nt(dO[h], vnb[h]), 0.0) for h in heads]
            dM = [jnp.where(strict, -_dot_nt(dRb[h], vnb[h]), 0.0) for h in heads]
            dPG = [(dP[h] * gam[h]).astype(BF16) for h in heads]
            dMG = [(dM[h] * gam[h]).astype(BF16) for h in heads]
            dPG_t = [(jnp.where(col >= row, _dot_nt(vnb[h], dO[h]), 0.0) * gam_t[h]).astype(BF16) for h in heads]
            dMG_t = [(jnp.where(col > row, -_dot_nt(vnb[h], dRb[h]), 0.0) * gam_t[h]).astype(BF16) for h in heads]
            E = [_dot_nt(vnb[h], dSb[h]) for h in heads]
            dKb = [_dot(dMG[h], K[h]) for h in heads]
            dc_all = jnp.zeros((C, LANES), F32)
            db_all = jnp.zeros((C, LANES), F32)
            for h in heads:
                dq_ref[rs, _hs(h)] = gcol[h] * _dot_nt(dO[h], Sb[h]) + _dot(dPG[h], K[h])
                dk_ref[rs, _hs(h)] = (_dot(dPG_t[h], Q[h]) + _dot(dMG_t[h], Kb[h]) + bcol[h] * dKb[h]
                                      - bg[h] * _dot_nt(dRb[h], Sb[h]) + dcol[h] * E[h])
                dv_ref[rs, _hs(h)] = bcol[h] * dR[h]
                dbeta = rsum(dKb[h] * K[h]) + rsum(dR[h] * (V[h] - gcol[h] * KS[h]))
                X = dP[h] * P[h] + dM[h] * M[h]
                ddel = rsum(K[h] * E[h]) * dcol[h]
                colsum = rsum(eye * jnp.sum(X, axis=0, keepdims=True))
                dc = (rsum(X) - colsum + gcol[h] * rsum(dO[h].astype(F32) * QS[h]) - bg[h] * rsum(dR[h] * KS[h])
                      - ddel)
                last = (jnp.sum(ddel, axis=0, keepdims=True)
                        + glast[h] * jnp.sum(rsum(dSn[h] * S[h]), axis=0, keepdims=True))
                dc_all = dc_all + jnp.where(lane == h, dc + jnp.where(rowl == C - 1, last, 0.0), 0.0)
                db_all = db_all + jnp.where(lane == N_HEAD + h, dbeta, 0.0)
            dgb_ref[rs, :] = _dotf(upper, dc_all) + db_all
            dSn = dS_new
        for h in heads:
            ds_ref[h] = dSn[h]
        dgn_ref[...] += dgn

    nb = nc // per
    cspec = lambda w, cb=0: pl.BlockSpec((per * C, w), lambda n: (nb - 1 - n, cb))
    hist = lambda a, b: pl.BlockSpec((per, N_HEAD, a, b), lambda n: (nb - 1 - n, 0, 0, 0))
    return pl.pallas_call(
        body,
        out_shape=(jax.ShapeDtypeStruct((T, BR_W), F32),) * 3 + (
            jax.ShapeDtypeStruct((T, LANES), F32), jax.ShapeDtypeStruct(into.shape, BF16),
            jax.ShapeDtypeStruct((1, D_HEAD), F32)),
        grid=(nb,),
        in_specs=[cspec(BR_W), cspec(BR_W), cspec(BR_W), cspec(LANES), cspec(BR_W, zcol), _full((1, D_HEAD)),
                  cspec(BR_W), hist(D_HEAD, D_HEAD), hist(C, C), cspec(BR_W), cspec(BR_W), HBM],
        out_specs=(cspec(BR_W), cspec(BR_W), cspec(BR_W), cspec(LANES), cspec(BR_W, zcol), _full((1, D_HEAD))),
        scratch_shapes=[pltpu.VMEM((N_HEAD, D_HEAD, D_HEAD), F32)], input_output_aliases={11: 4},
        name="gdn_chunk_bwd",
        compiler_params=_cp("arbitrary"))(q, k, v, gb, proj, gnorm, oraw, shist, tinv_all, vn_all, dog, into)


SB_COL = SB_OFF // BR_W
SB_SCALE = D_HEAD ** -0.5


def _sb_pre(proj, gq, gk):
    T = proj.shape[0]
    tm = min(TM, T)

    def body(xq_ref, xk_ref, xv_ref, gq_ref, gk_ref, q_ref, k_ref, v_ref):
        for h in range(N_HEAD):
            for x_ref, g_ref, ref, scale in ((xq_ref, gq_ref, q_ref, SB_SCALE), (xk_ref, gk_ref, k_ref, 1.0)):
                xh = x_ref[:, _hs(h)].astype(F32)
                r = lax.rsqrt(jnp.mean(xh * xh, axis=-1, keepdims=True) + EPS)
                ref[:, _hs(h)] = (xh * (r * scale) * g_ref[...]).astype(BF16)
        v_ref[...] = xv_ref[...]

    return pl.pallas_call(
        body, out_shape=(jax.ShapeDtypeStruct((T, BR_W), BF16),) * 3, grid=(T // tm,),
        in_specs=[_rowspec(tm, BR_W, SB_COL), _rowspec(tm, BR_W, SB_COL + 1), _rowspec(tm, BR_W, SB_COL + 2),
                  _full((1, D_HEAD)), _full((1, D_HEAD))],
        out_specs=(_rowspec(tm, BR_W),) * 3, name="sb_pre", compiler_params=_cp("parallel"))(proj, proj, proj, gq, gk)


def _sb_pre_bwd(proj, gq, gk, dq, dk, dv, into):
    T = proj.shape[0]
    tm = min(TM, T)

    def body(xq_ref, xk_ref, gq_ref, gk_ref, dq_ref, dk_ref, dv_ref, into_ref, dx_ref, dgq_ref, dgk_ref):
        i = pl.program_id(0)

        @pl.when(i == 0)
        def _():
            dgq_ref[...] = jnp.zeros_like(dgq_ref)
            dgk_ref[...] = jnp.zeros_like(dgk_ref)

        for off, x_ref, g_ref, d_ref, dg_ref, scale in ((0, xq_ref, gq_ref, dq_ref, dgq_ref, SB_SCALE),
                                                        (BR_W, xk_ref, gk_ref, dk_ref, dgk_ref, 1.0)):
            dg = jnp.zeros((1, D_HEAD), F32)
            for h in range(N_HEAD):
                xh = x_ref[:, _hs(h)].astype(F32)
                r = lax.rsqrt(jnp.mean(xh * xh, axis=-1, keepdims=True) + EPS)
                y = xh * r
                dn = d_ref[:, _hs(h)] * scale
                dg = dg + jnp.sum(dn * y, axis=0, keepdims=True)
                dy = dn * g_ref[...]
                dx_ref[:, off + h * D_HEAD:off + (h + 1) * D_HEAD] = (
                    r * (dy - y * jnp.mean(dy * y, axis=-1, keepdims=True))).astype(BF16)
            dg_ref[...] += dg
        dx_ref[:, 2 * BR_W:] = dv_ref[...].astype(BF16)

    return pl.pallas_call(
        body,
        out_shape=(jax.ShapeDtypeStruct(into.shape, BF16), jax.ShapeDtypeStruct((1, D_HEAD), F32),
                   jax.ShapeDtypeStruct((1, D_HEAD), F32)),
        grid=(T // tm,),
        in_specs=[_rowspec(tm, BR_W, SB_COL), _rowspec(tm, BR_W, SB_COL + 1), _full((1, D_HEAD)), _full((1, D_HEAD)),
                  _rowspec(tm, BR_W), _rowspec(tm, BR_W), _rowspec(tm, BR_W), HBM],
        out_specs=(_rowspec(tm, 3 * BR_W, SB_OFF // (3 * BR_W)), _full((1, D_HEAD)), _full((1, D_HEAD))),
        input_output_aliases={7: 0}, name="sb_pre_bwd",
        compiler_params=_cp("arbitrary"))(proj, proj, gq, gk, dq, dk, dv, into)


def _sb_pair(q, k, masked):
    z = _dot_nt(q, k)
    zc = jnp.minimum(z, 30.0)
    sp = jnp.log(1.0 + jnp.exp(zc)) + (z - zc)
    if not masked:
        return z, sp, None
    mask = _iota(z.shape, 1) < _iota(z.shape, 0)
    return z, jnp.where(mask, sp, 0.0), mask


def _sb_fwd(sq, sk, sv):
    T = sq.shape[0]
    blk = min(SB_BLK, T)
    w = blk // 2
    nb = T // blk

    def body(q_ref, k_ref, v_ref, o_ref, lt_ref, cut_ref, acc_ref, r_ref):
        head, qi = pl.program_id(0), pl.program_id(1)
        acc_ref[...] = jnp.zeros_like(acc_ref)
        r_ref[...] = jnp.zeros_like(r_ref)
        after = (_iota((w, w), 0) > _iota((w, w), 1)).astype(BF16)

        def block(rows, kb, masked):
            keys = pl.ds(pl.multiple_of(kb * w, w), w)
            z, sp, mask = _sb_pair(q_ref[rows, :], k_ref[keys, :], masked)
            r = r_ref[rows, :]
            a = jnp.exp(z - sp - _dot(sp, after) - r)
            if masked:
                a = jnp.where(mask, a, 0.0)
            acc_ref[rows, :] += _dot(a, v_ref[keys, :])
            r_ref[rows, :] = r + jnp.sum(sp, axis=-1, keepdims=True)

        def alive():
            return jnp.min(r_ref[...]) < SB_DEAD

        def further(state):
            kb, _ = state
            block(slice(0, blk), kb, False)
            return kb - 1, alive()

        block(slice(w, blk), 2 * qi + 1, True)
        block(slice(0, blk), 2 * qi, True)
        left, _ = lax.while_loop(lambda s: jnp.logical_and(s[0] >= 0, s[1]), further, (2 * qi - 1, alive()))
        o_ref[...] = acc_ref[...].astype(BF16)
        lt_ref[0] = r_ref[...]
        cut_ref[head, qi] = (left + 1).astype(F32)

    qspec = pl.BlockSpec((blk, D_HEAD), lambda h, i: (i, h))
    whole = pl.BlockSpec((T, D_HEAD), lambda h, i: (0, h))
    return pl.pallas_call(
        body,
        out_shape=(jax.ShapeDtypeStruct((T, BR_W), BF16), jax.ShapeDtypeStruct((N_HEAD, T, 1), F32),
                   jax.ShapeDtypeStruct((N_HEAD, nb), F32)),
        grid=(N_HEAD, nb), in_specs=[qspec, whole, whole],
        out_specs=(qspec, pl.BlockSpec((1, blk, 1), lambda h, i: (h, i, 0)), pl.BlockSpec(memory_space=pltpu.SMEM)),
        scratch_shapes=[pltpu.VMEM((blk, D_HEAD), F32), pltpu.VMEM((blk, 1), F32)],
        name="sb_fwd", compiler_params=_cp("arbitrary", "arbitrary"))(sq, sk, sv)


def _sb_bwd(sq, sk, sv, ltot, do, cut):
    T = sq.shape[0]
    blk = min(SB_BLK, T)
    w = blk // 2
    nb = T // blk

    def body(q_ref, k_ref, v_ref, lt_ref, do_ref, cut_ref, dq_ref, dk_ref, dv_ref, acc_ref, p_ref, g_ref):
        head, qi = pl.program_id(0), pl.program_id(1)
        first = cut_ref[head, qi].astype(jnp.int32)

        @pl.when(qi == 0)
        def _():
            dk_ref[...] = jnp.zeros_like(dk_ref)
            dv_ref[...] = jnp.zeros_like(dv_ref)

        acc_ref[...] = jnp.zeros_like(acc_ref)
        p_ref[...] = lt_ref[0]
        g_ref[...] = jnp.zeros_like(g_ref)
        after = (_iota((w, w), 0) > _iota((w, w), 1)).astype(BF16)
        before = (_iota((w, w), 0) < _iota((w, w), 1)).astype(BF16)

        def block(rows, kb, masked):
            keys = pl.ds(pl.multiple_of(kb * w, w), w)
            q, do = q_ref[rows, :], do_ref[rows, :]
            z, sp, mask = _sb_pair(q, k_ref[keys, :], masked)
            d_a = _dot_nt(do, v_ref[keys, :])
            rest = p_ref[rows, :] - jnp.sum(sp, axis=-1, keepdims=True)
            a = jnp.exp(z - sp - _dot(sp, after) - rest)
            if masked:
                a = jnp.where(mask, a, 0.0)
            g = a * d_a
            sig = jnp.exp(z - sp)
            dz = g - sig * (g + (g_ref[rows, :] + _dot(g, before)))
            if masked:
                dz = jnp.where(mask, dz, 0.0)
            dz = dz.astype(BF16)
            dv_ref[keys, :] += _dot_tn(a, do)
            dk_ref[keys, :] += _dot_tn(dz, q)
            acc_ref[rows, :] += _dot(dz, k_ref[keys, :])
            p_ref[rows, :] = rest
            g_ref[rows, :] += jnp.sum(g, axis=-1, keepdims=True)

        def step(kb, carry):
            block(slice(0, blk), kb, False)
            return carry

        lax.fori_loop(first, 2 * qi, step, 0)
        block(slice(0, blk), 2 * qi, True)
        block(slice(w, blk), 2 * qi + 1, True)
        dq_ref[...] = acc_ref[...]

    qspec = pl.BlockSpec((blk, D_HEAD), lambda h, i: (i, h))
    whole = pl.BlockSpec((T, D_HEAD), lambda h, i: (0, h))
    return pl.pallas_call(
        body, out_shape=(jax.ShapeDtypeStruct((T, BR_W), F32),) * 3, grid=(N_HEAD, nb),
        in_specs=[qspec, whole, whole, pl.BlockSpec((1, blk, 1), lambda h, i: (h, i, 0)), qspec,
                  pl.BlockSpec(memory_space=pltpu.SMEM)],
        out_specs=(qspec, whole, whole),
        scratch_shapes=[pltpu.VMEM((blk, D_HEAD), F32), pltpu.VMEM((blk, 1), F32), pltpu.VMEM((blk, 1), F32)],
        name="sb_bwd", compiler_params=_cp("arbitrary", "arbitrary"))(sq, sk, sv, ltot, do, cut)


def _mem_kv(mem, gm, w_kv, gk):
    def body(mem_ref, gm_ref, w_ref, gk_ref, mn_ref, kv_ref, kh_ref, vm_ref):
        mv = mem_ref[...]
        r = lax.rsqrt(jnp.mean(mv * mv, axis=-1, keepdims=True) + EPS)
        mn = (mv * r * gm_ref[...]).astype(BF16)
        mn_ref[...] = mn
        kv = lax.dot_general(mn, w_ref[...], _NN, preferred_element_type=F32)
        kv_ref[...] = kv
        for h in range(N_HEAD):
            kh = kv[:, _hs(h)]
            rk = lax.rsqrt(jnp.mean(kh * kh, axis=-1, keepdims=True) + EPS)
            kh_ref[:, _hs(h)] = (kh * rk * gk_ref[...]).astype(BF16)
        vm_ref[...] = kv[:, BR_W:].astype(BF16)

    return pl.pallas_call(
        body,
        out_shape=(jax.ShapeDtypeStruct((N_MEM, D_MODEL), BF16), jax.ShapeDtypeStruct((N_MEM, 2 * BR_W), F32),
                   jax.ShapeDtypeStruct((N_MEM, BR_W), BF16), jax.ShapeDtypeStruct((N_MEM, BR_W), BF16)),
        name="mem_kv", compiler_params=_cp())(mem, gm, w_kv, gk)


def _mem_q(x_ref, gq_ref, h):
    xh = x_ref[:, _hs(h)].astype(F32)
    r = lax.rsqrt(jnp.mean(xh * xh, axis=-1, keepdims=True) + EPS)
    return r, xh * r


def _mem_probs(qn, kh):
    s = _dot_nt(qn, kh) * (D_HEAD ** -0.5)
    e = jnp.exp(s - jnp.max(s, axis=-1, keepdims=True))
    return e / jnp.sum(e, axis=-1, keepdims=True)


def _mem_fwd(proj, kh, vm, gq):
    T = proj.shape[0]
    tm = min(TM, T)

    def body(x_ref, kh_ref, vm_ref, gq_ref, o_ref):
        for h in range(N_HEAD):
            _, y = _mem_q(x_ref, gq_ref, h)
            p = _mem_probs((y * gq_ref[...]).astype(BF16), kh_ref[:, _hs(h)])
            o_ref[:, _hs(h)] = _dot(p, vm_ref[:, _hs(h)]).astype(BF16)

    return pl.pallas_call(
        body, out_shape=jax.ShapeDtypeStruct((T, BR_W), BF16), grid=(T // tm,),
        in_specs=[_rowspec(tm, BR_W, MEMQ_OFF // BR_W), _full((N_MEM, BR_W)), _full((N_MEM, BR_W)),
                  _full((1, D_HEAD))],
        out_specs=_rowspec(tm, BR_W), name="mem_fwd", compiler_params=_cp("parallel"))(proj, kh, vm, gq)


def _mem_bwd(proj, kh, vm, gq, do, into):
    T = proj.shape[0]
    tm = min(TM, T)

    def body(x_ref, kh_ref, vm_ref, gq_ref, do_ref, into_ref, dx_ref, dkh_ref, dvm_ref, dgq_ref):
        i = pl.program_id(0)

        @pl.when(i == 0)
        def _():
            dkh_ref[...] = jnp.zeros_like(dkh_ref)
            dvm_ref[...] = jnp.zeros_like(dvm_ref)
            dgq_ref[...] = jnp.zeros_like(dgq_ref)

        dg = jnp.zeros((1, D_HEAD), F32)
        for h in range(N_HEAD):
            r, y = _mem_q(x_ref, gq_ref, h)
            qn = (y * gq_ref[...]).astype(BF16)
            p = _mem_probs(qn, kh_ref[:, _hs(h)])
            dov = do_ref[:, _hs(h)]
            dp = _dot_nt(dov, vm_ref[:, _hs(h)])
            ds = p * (dp - jnp.sum(dp * p, axis=-1, keepdims=True)) * (D_HEAD ** -0.5)
            dqn = _dot(ds, kh_ref[:, _hs(h)])
            dkh_ref[:, _hs(h)] += _dot_tn(ds, qn)
            dvm_ref[:, _hs(h)] += _dot_tn(p, dov)
            dg = dg + jnp.sum(dqn * y, axis=0, keepdims=True)
            dy = dqn * gq_ref[...]
            dx_ref[:, _hs(h)] = (r * (dy - y * jnp.mean(dy * y, axis=-1, keepdims=True))).astype(BF16)
        dgq_ref[...] += dg

    return pl.pallas_call(
        body,
        out_shape=(jax.ShapeDtypeStruct(into.shape, BF16), jax.ShapeDtypeStruct((N_MEM, BR_W), F32),
                   jax.ShapeDtypeStruct((N_MEM, BR_W), F32), jax.ShapeDtypeStruct((1, D_HEAD), F32)),
        grid=(T // tm,),
        in_specs=[_rowspec(tm, BR_W, MEMQ_OFF // BR_W), _full((N_MEM, BR_W)), _full((N_MEM, BR_W)),
                  _full((1, D_HEAD)), _rowspec(tm, BR_W), HBM],
        out_specs=(_rowspec(tm, BR_W, MEMQ_OFF // BR_W), _full((N_MEM, BR_W)), _full((N_MEM, BR_W)),
                   _full((1, D_HEAD))),
        input_output_aliases={5: 0}, name="mem_bwd", compiler_params=_cp("arbitrary"))(proj, kh, vm, gq, do, into)


def _mem_kv_bwd(mem, gm, w_kv, gk, kv, mn, dkh, dvm):
    def body(mem_ref, gm_ref, w_ref, gk_ref, kv_ref, mn_ref, dkh_ref, dvm_ref, dw_ref, dgm_ref, dgk_ref, dkv_ref):
        dgk = jnp.zeros((1, D_HEAD), F32)
        for h in range(N_HEAD):
            kh = kv_ref[:, _hs(h)]
            r = lax.rsqrt(jnp.mean(kh * kh, axis=-1, keepdims=True) + EPS)
            y = kh * r
            dn = dkh_ref[:, _hs(h)]
            dgk = dgk + jnp.sum(dn * y, axis=0, keepdims=True)
            dy = dn * gk_ref[...]
            dkv_ref[:, _hs(h)] = (r * (dy - y * jnp.mean(dy * y, axis=-1, keepdims=True))).astype(BF16)
        dkv_ref[:, BR_W:] = dvm_ref[...].astype(BF16)
        dgk_ref[...] = dgk
        dkv = dkv_ref[...]
        dw_ref[...] = lax.dot_general(mn_ref[...], dkv, _TN, preferred_element_type=F32)
        dmn = lax.dot_general(dkv, w_ref[...], _NT, preferred_element_type=F32)
        mv = mem_ref[...]
        memn = mv * lax.rsqrt(jnp.mean(mv * mv, axis=-1, keepdims=True) + EPS)
        dgm_ref[...] = jnp.sum(dmn * memn, axis=0, keepdims=True)

    return pl.pallas_call(
        body,
        out_shape=(jax.ShapeDtypeStruct((D_MODEL, 2 * BR_W), F32), jax.ShapeDtypeStruct((1, D_MODEL), F32),
                   jax.ShapeDtypeStruct((1, D_HEAD), F32)),
        scratch_shapes=[pltpu.VMEM((N_MEM, 2 * BR_W), BF16)], name="mem_kv_bwd",
        compiler_params=_cp())(mem, gm, w_kv, gk, kv, mn, dkh, dvm)


def _merge_fwd(og, osb, om, proj, wg, ws, wm):
    T = og.shape[0]
    tm = min(TM, T)

    def body(og_ref, os_ref, om_ref, g0, g1, g2, wg_ref, ws_ref, wm_ref, mix_ref, yg_ref, ys_ref, ym_ref):
        mix = jnp.zeros((tm, D_MODEL), F32)
        for o_ref, gl_ref, w_ref, y_ref in ((og_ref, g0, wg_ref, yg_ref), (os_ref, g1, ws_ref, ys_ref),
                                            (om_ref, g2, wm_ref, ym_ref)):
            y = lax.dot_general(o_ref[...], w_ref[...], _NN, preferred_element_type=F32)
            y_ref[...] = y.astype(BF16)
            mix = mix + _sigmoid(gl_ref[...].astype(F32)) * y
        mix_ref[...] = mix.astype(BF16)

    br = _rowspec(tm, BR_W)
    wspec = _full((BR_W, D_MODEL))
    out = _rowspec(tm, D_MODEL)
    gates = [_rowspec(tm, D_MODEL, GATE_COL + b) for b in range(3)]
    return pl.pallas_call(
        body, out_shape=(jax.ShapeDtypeStruct((T, D_MODEL), BF16),) * 4, grid=(T // tm,),
        in_specs=[br, br, br, *gates, wspec, wspec, wspec],
        out_specs=(out,) * 4, name="merge_fwd",
        compiler_params=_cp("parallel"))(og, osb, om, proj, proj, proj, wg, ws, wm)


def _merge_bwd(dmix, proj, ys, os_, ws):
    T = dmix.shape[0]
    tm = min(TM, T)

    def body(dmix_ref, g0, g1, g2, y0, y1, y2, o0, o1, o2, w0, w1, w2, dgl_ref, do0, do1, do2, dw0, dw1, dw2):
        i = pl.program_id(0)
        dm = dmix_ref[...].astype(F32)
        for b, (gl_ref, y_ref, o_ref, w_ref, do_ref, dw_ref) in enumerate((
                (g0, y0, o0, w0, do0, dw0), (g1, y1, o1, w1, do1, dw1), (g2, y2, o2, w2, do2, dw2))):
            gate = _sigmoid(gl_ref[...].astype(F32))
            dgl_ref[:, b * D_MODEL:(b + 1) * D_MODEL] = (dm * y_ref[...].astype(F32) * gate * (1.0 - gate)).astype(BF16)
            dy = (gate * dm).astype(BF16)
            do_ref[...] = lax.dot_general(dy, w_ref[...], _NT, preferred_element_type=F32).astype(BF16)
            _accum(dw_ref, i == 0, lax.dot_general(dy, o_ref[...], _TN, preferred_element_type=F32))

    br = _rowspec(tm, BR_W)
    wide = _rowspec(tm, D_MODEL)
    wspec = _full((BR_W, D_MODEL))
    wtspec = _full((D_MODEL, BR_W))
    gates = [_rowspec(tm, D_MODEL, GATE_COL + b) for b in range(3)]
    return pl.pallas_call(
        body,
        out_shape=(jax.ShapeDtypeStruct((T, PROJ_W), BF16),) + (jax.ShapeDtypeStruct((T, BR_W), BF16),) * 3
        + (jax.ShapeDtypeStruct((D_MODEL, BR_W), F32),) * 3,
        grid=(T // tm,),
        in_specs=[wide, *gates, wide, wide, wide, br, br, br, wspec, wspec, wspec],
        out_specs=(_rowspec(tm, 3 * D_MODEL, GATE_OFF // (3 * D_MODEL)), br, br, br, wtspec, wtspec, wtspec),
        name="merge_bwd",
        compiler_params=_cp("arbitrary"))(dmix, proj, proj, proj, *ys, *os_, *ws)


def _loss(y, tgt):
    T, dm = y.shape
    tm = min(TM, T)

    def body(y_ref, t_ref, dy_ref, dyb_ref, sq_ref):
        err = y_ref[...] - t_ref[...]
        dy = err * (1.0 / dm)
        dy_ref[...] = dy
        dyb_ref[...] = dy.astype(BF16)
        _accum(sq_ref, pl.program_id(0) == 0, jnp.sum(err * err, axis=0, keepdims=True))

    return pl.pallas_call(
        body,
        out_shape=(jax.ShapeDtypeStruct((T, dm), F32), jax.ShapeDtypeStruct((T, dm), BF16),
                   jax.ShapeDtypeStruct((1, dm), F32)),
        grid=(T // tm,), in_specs=[_rowspec(tm, dm), _rowspec(tm, dm)],
        out_specs=(_rowspec(tm, dm), _rowspec(tm, dm), _full((1, dm))), name="loss",
        compiler_params=_cp("arbitrary"))(y, tgt)


def _split_w_in(slabs):
    width = slabs[0].shape[1]

    def cols(lo, hi):
        return [s[:, max(lo - j * width, 0):min(hi - j * width, width)] for j, s in enumerate(slabs)
                if lo < (j + 1) * width and hi > j * width]

    main = [c for piece in (IN_GATE, IN_QKV, IN_SB, IN_Z, IN_MEMQ) for c in cols(*piece)]
    ab = jnp.concatenate(cols(*IN_AB), axis=1)
    return jnp.concatenate(main, axis=1), jnp.pad(ab, ((0, 0), (0, LANES - ab.shape[1])))


def _local_step(x, mem, tgt, W, P, dist=None):
    w_main, w_ab = W["w_main"], W["w_ab"]
    avec = jnp.pad(jnp.concatenate([P["a_log"], P["dt_bias"]], axis=0), ((0, 0), (0, LANES - N_HEAD)))

    h = _rms_fwd(x, P["norm1_g"], "rms1")
    if dist is None:
        proj = _mm(h, w_main, "nn", BF16, "in_proj", tn=1792)
    else:
        proj, gathered = _mm(h, w_main, "nn", BF16, "in_proj", tn=1792, comm=dist.gather_rest())
        rest, conv_w = dist.weights_from(gathered)
        W, P = {**W, **rest}, {**P, "conv_w": conv_w}
    wbr = (W["w_br_gdn"], W["w_br_sb"], W["w_br_mem"])
    ab = _mm(h, w_ab, "nn", F32, "in_proj_ab")
    q, k, v, gb = _gdn_pre(proj, P["conv_w"], ab, avec)
    tinv, tinv_t = _gdn_inv(k, gb)
    og, oraw, shist, vn = _gdn_fwd(q, k, v, gb, proj, P["gdn_norm_g"], tinv)
    sq, sk, sv = _sb_pre(proj, P["sb_q_norm_g"], P["sb_k_norm_g"])
    osb, ltot, cut = _sb_fwd(sq, sk, sv)
    mn, kv, kh, vm = _mem_kv(mem, P["mem_norm_g"], W["w_mem_kv"], P["mem_k_norm_g"])
    om = _mem_fwd(proj, kh, vm, P["mem_q_norm_g"])
    mix, yg, ys, ym = _merge_fwd(og, osb, om, proj, *wbr)
    x1 = _mm(mix, W["w_o"], "nn", F32, "out_proj", extra=x, epi=_epi_add)
    h2 = _rms_fwd(x1, P["norm2_g"], "rms2")
    u = _mm(h2, W["w_up"], "nn", BF16, "mlp_up", tn=2048)
    y = _mm(u, W["w_down"], "nn", F32, "mlp_down", a_fn=_relu2, extra=x1, epi=_epi_add)
    dy, dyb, sq_err = _loss(y, tgt)

    G = {}
    du = _mm(dyb, W["w_down"], "nt", BF16, "d_mlp_act", tn=2048, extra=u, epi=_epi_drelu2)
    G["w_down"] = _mm(u, dyb, "tn", F32, "dw_down", a_fn=_relu2)
    G["w_up"] = _mm(du, h2, "tn", F32, "dw_up")
    dh2 = _mm(du, W["w_up"], "nt", F32, "d_h2")
    dx1, dx1b, G["norm2_g"] = _rms_bwd(dh2, x1, P["norm2_g"], dy, "rms2_bwd")
    dmix = _mm(dx1b, W["w_o"], "nt", BF16, "d_mix")
    G["w_o"] = _mm(mix, dx1b, "tn", F32, "dw_o")
    dproj, dog, dosb, dom, G["w_br_gdn"], G["w_br_sb"], G["w_br_mem"] = _merge_bwd(
        dmix, proj, (yg, ys, ym), (og, osb, om), wbr)
    dq, dk, dv, dgb, dproj, G["gdn_norm_g"] = _gdn_bwd(q, k, v, gb, proj, P["gdn_norm_g"], oraw, shist, tinv_t, vn, dog,
                                                      dproj)
    dxc, dab, G["conv_w"], dav = _gdn_pre_bwd(proj, P["conv_w"], ab, avec, dq, dk, dv, dgb)
    dproj = _conv_bwd(dxc, P["conv_w"], dproj)
    G["a_log"], G["dt_bias"] = dav[0:1, :N_HEAD], dav[1:2, :N_HEAD]
    dsq, dsk, dsv = _sb_bwd(sq, sk, sv, ltot, dosb, cut)
    dproj, G["sb_q_norm_g"], G["sb_k_norm_g"] = _sb_pre_bwd(proj, P["sb_q_norm_g"], P["sb_k_norm_g"], dsq, dsk, dsv,
                                                            dproj)
    dproj, dkh, dvm, G["mem_q_norm_g"] = _mem_bwd(proj, kh, vm, P["mem_q_norm_g"], dom, dproj)
    G["w_mem_kv"], G["mem_norm_g"], G["mem_k_norm_g"] = _mem_kv_bwd(
        mem, P["mem_norm_g"], W["w_mem_kv"], P["mem_k_norm_g"], kv, mn, dkh, dvm)
    dw_ab = _mm(dab, h, "tn", F32, "dw_in_ab")
    if dist is None:
        dw_main = _mm(dproj, h, "tn", F32, "dw_in")
    else:
        early = [n for n, _, _ in BIG if n != "w_in"]
        dw_main, landed = _mm(dproj, h, "tn", F32, "dw_in", comm=dist.scatter(G, early, "early"))
        dist.collect(early, landed)
    G["w_in"] = jnp.concatenate([dw_main[QKV_OFF:SB_OFF], dw_main[Z_OFF:MEMQ_OFF], dw_ab[:8], dw_main[SB_OFF:Z_OFF],
                                 dw_main[MEMQ_OFF:], dw_main[:QKV_OFF]], axis=0)
    if dist is None:
        dh = _mm(dproj, w_main, "nt", F32, "d_h")
    else:
        dh, landed = _mm(dproj, w_main, "nt", F32, "d_h", comm=dist.scatter(G, ["w_in"], "late"))
        dist.collect(["w_in"], landed)
    dh = _mm(dab, w_ab, "nt", F32, "d_h_ab", extra=dh, epi=_epi_add)
    dx, _, G["norm1_g"] = _rms_bwd(dh, x, P["norm1_g"], dx1, "rms1_bwd")
    return sq_err, dx, G


def _comm(name, ins, out_shapes, plan):
    n_in, n_out = len(ins), len(out_shapes)
    probe = plan([None] * n_in, [None] * n_out, 0, 0, 0, dry=True)
    n_copy = probe

    def body(*refs):
        in_refs, out_refs = refs[:n_in], refs[n_in:n_in + n_out]
        send_sems, recv_sems = refs[n_in + n_out:]
        x, y, c = lax.axis_index("x"), lax.axis_index("y"), lax.axis_index("c")
        copies = []
        for k, (src, dst, dev) in enumerate(plan(in_refs, out_refs, x, y, c, dry=False)):
            if dev is None:
                cp = pltpu.make_async_copy(src, dst, send_sems.at[k])
            else:
                cp = pltpu.make_async_remote_copy(src_ref=src, dst_ref=dst, send_sem=send_sems.at[k],
                                                  recv_sem=recv_sems.at[k], device_id=dev, device_id_type=MESH)
            cp.start()
            copies.append(cp)
        for cp in copies:
            cp.wait()

    return pl.pallas_call(
        body, out_shape=tuple(out_shapes), in_specs=[HBM] * n_in, out_specs=tuple([HBM] * n_out),
        scratch_shapes=[pltpu.SemaphoreType.DMA((n_copy,)), pltpu.SemaphoreType.DMA((n_copy,))], name=name)(*ins)


def _other_chips(x, y):
    return ((1 - x, y), (x, 1 - y), (1 - x, 1 - y))


def _gather_plan(parts, direct=()):
    n, every = len(parts), list(parts) + list(direct)

    def copies(ins, outs, send, recv, scratch):
        x, y, c = lax.axis_index("x"), lax.axis_index("y"), lax.axis_index("c")
        me = 2 * x + y
        chips = _other_chips(x, y)
        local_sems, staged = scratch[0], scratch[1:]

        def remote(src, dst, k, dev):
            return pltpu.make_async_remote_copy(src_ref=src, dst_ref=dst, send_sem=send.at[k], recv_sem=recv.at[k],
                                                device_id=dev, device_id_type=MESH)

        def half(p, ci):
            hr = ins[p].shape[0] // 2
            return pl.ds(pl.multiple_of(ci * hr, 16), hr)

        sent = [remote(ins[p].at[half(p, c)], outs[p].at[me, half(p, c)], 6 * p + f, (px, py, c))
                for p in range(n) for f, (px, py) in enumerate(chips)]
        sent += [remote(ins[p], outs[p].at[me], 6 * n + 3 * (p - n) + f, (px, py, c))
                 for p in range(n, len(every)) for f, (px, py) in enumerate(chips)]
        landed = [outs[p].at[2 * px + py, half(p, c)] for p in range(n) for px, py in chips]
        passed = [remote(landed[3 * p + f], landed[3 * p + f], 6 * p + 3 + f, (x, y, 1 - c))
                  for p in range(n) for f in range(3)]
        loads = [pltpu.make_async_copy(ins[p], staged[p], local_sems.at[2 * p]) for p in range(len(every))]
        stores = [pltpu.make_async_copy(staged[p], outs[p].at[me], local_sems.at[2 * p + 1]) for p in range(len(every))]
        return sent, passed, loads, stores

    def start(*refs):
        sent, _, loads, _ = copies(*refs)
        for cp in loads + sent:
            cp.start()

    def mid(*refs):
        sent, passed, loads, stores = copies(*refs)
        for ld, st in zip(loads, stores):
            ld.wait()
            st.start()
        for p in range(n):
            for f in range(3):
                sent[3 * p + f].wait_recv()
                passed[3 * p + f].start()

    def finish(*refs):
        sent, passed, _, stores = copies(*refs)
        for cp in sent[:3 * n]:
            cp.wait_send()
        for cp in passed + sent[3 * n:] + stores:
            cp.wait()

    return _Hosted(every, [jax.ShapeDtypeStruct((4,) + p.shape, p.dtype) for p in every], 6 * n + 3 * len(direct),
                   start, finish, mid,
                   [pltpu.SemaphoreType.DMA((2 * len(every),))] + [pltpu.VMEM(p.shape, p.dtype) for p in every])


def _scatter_plan(pairs):
    def copies(ins, outs, send, recv, scratch):
        x, y, c = lax.axis_index("x"), lax.axis_index("y"), lax.axis_index("c")
        me = 2 * x + y
        return [pltpu.make_async_remote_copy(src_ref=src.at[2 * px + py], dst_ref=dst.at[me], send_sem=send.at[3 * p + f],
                                             recv_sem=recv.at[3 * p + f], device_id=(px, py, c), device_id_type=MESH)
                for p, (src, dst) in enumerate(zip(ins, outs)) for f, (px, py) in enumerate(_other_chips(x, y))]

    def start(*refs):
        for cp in copies(*refs):
            cp.start()

    def finish(*refs):
        for cp in copies(*refs):
            cp.wait()

    return _Hosted(pairs, [jax.ShapeDtypeStruct(a.shape, a.dtype) for a in pairs], 3 * len(pairs), start, finish)


def _run_hosted(comm, name):
    n_in, n_out = len(comm.ins), len(comm.out_shapes)

    def body(*refs):
        args = (refs[:n_in], refs[n_in:n_in + n_out], refs[n_in + n_out], refs[n_in + n_out + 1], refs[n_in + n_out + 2:])
        comm.start(*args)
        if comm.mid is not None:
            comm.mid(*args)
        comm.finish(*args)

    sems = [pltpu.SemaphoreType.DMA((comm.n_sems,)), pltpu.SemaphoreType.DMA((comm.n_sems,))]
    return list(pl.pallas_call(
        body, out_shape=tuple(comm.out_shapes), in_specs=[HBM] * n_in, out_specs=tuple([HBM] * n_out),
        scratch_shapes=sems + comm.scratch, name=name, compiler_params=_cp())(*comm.ins))


def _swap_halves(slabs, name):
    n = len(slabs)

    def body(*refs):
        ins, outs = refs[:n], refs[n:2 * n]
        send, recv = refs[2 * n:]
        x, y, c = lax.axis_index("x"), lax.axis_index("y"), lax.axis_index("c")
        other = (x, y, 1 - c)
        for p in range(n):
            for j in range(4):
                pltpu.make_async_remote_copy(src_ref=ins[p].at[j, 1 - c], dst_ref=outs[p].at[j], send_sem=send.at[p],
                                             recv_sem=recv.at[p], device_id=other, device_id_type=MESH).start()
        for p in range(n):
            pltpu.make_async_remote_copy(src_ref=outs[p], dst_ref=outs[p], send_sem=send.at[p], recv_sem=recv.at[p],
                                         device_id=other, device_id_type=MESH).wait()

    shapes = [jax.ShapeDtypeStruct((4,) + s.shape[2:], s.dtype) for s in slabs]
    return pl.pallas_call(
        body, out_shape=tuple(shapes), in_specs=[HBM] * n, out_specs=tuple([HBM] * n),
        scratch_shapes=[pltpu.SemaphoreType.DMA((n,)), pltpu.SemaphoreType.DMA((n,))], name=name)(*slabs)


def _join_halves(both):
    n = len(both)

    def body(*refs):
        bufs = refs[n:2 * n]
        send, recv = refs[2 * n:]
        x, y, c = lax.axis_index("x"), lax.axis_index("y"), lax.axis_index("c")
        copies = []
        for p in range(n):
            cp = pltpu.make_async_remote_copy(src_ref=bufs[p].at[c], dst_ref=bufs[p].at[c], send_sem=send.at[p],
                                              recv_sem=recv.at[p], device_id=(x, y, 1 - c), device_id_type=MESH)
            cp.start()
            copies.append(cp)
        for cp in copies:
            cp.wait()

    return pl.pallas_call(
        body, out_shape=tuple(jax.ShapeDtypeStruct(a.shape, a.dtype) for a in both), in_specs=[HBM] * n,
        out_specs=tuple([HBM] * n), input_output_aliases={p: p for p in range(n)},
        scratch_shapes=[pltpu.SemaphoreType.DMA((n,)), pltpu.SemaphoreType.DMA((n,))], name="grad_join_cores")(*both)


def _gather_all(a, name):
    def plan(ins, outs, x, y, c, dry):
        if dry:
            return 8
        me = 4 * x + 2 * y + c
        copies = [(ins[0], outs[0].at[me], None)]
        for f in range(1, 8):
            peer = (1 - x if f & 4 else x, 1 - y if f & 2 else y, 1 - c if f & 1 else c)
            copies.append((ins[0], outs[0].at[me], peer))
        return copies

    return _comm(name, [a], [jax.ShapeDtypeStruct((8,) + a.shape, a.dtype)], plan)[0]


def _sum_slots(a, name, extra=None):
    n, R, _ = a.shape
    rb = min(ROW_BLK, R)

    def body(*refs):
        a_ref, o_ref = refs[0], refs[-1]
        acc = a_ref[0]
        for s in range(1, n):
            acc = acc + a_ref[s]
        if extra is not None:
            acc = acc + refs[1][...]
        o_ref[...] = acc

    ins = [a] + ([extra] if extra is not None else [])
    in_specs = [pl.BlockSpec((n, rb, LANES), lambda i: (0, i, 0))] + ([_rowspec(rb, LANES)] if extra is not None else [])
    return pl.pallas_call(
        body, out_shape=jax.ShapeDtypeStruct((R, LANES), F32), grid=(R // rb,), in_specs=in_specs,
        out_specs=_rowspec(rb, LANES), name=name, compiler_params=_cp("parallel"))(*ins)


def _pair_sum(slab, theirs, core, name):
    _, _, hr, C = slab.shape

    def body(c_ref, a_ref, b_ref, o_ref):
        o_ref[...] = (a_ref[...] + b_ref[...]).astype(BF16)

    return pl.pallas_call(
        body, out_shape=jax.ShapeDtypeStruct((4, hr, C), BF16),
        grid_spec=pltpu.PrefetchScalarGridSpec(
            num_scalar_prefetch=1, grid=(4,),
            in_specs=[pl.BlockSpec((None, None, hr, C), lambda j, c_ref: (j, c_ref[0], 0, 0)),
                      pl.BlockSpec((None, hr, C), lambda j, c_ref: (j, 0, 0))],
            out_specs=pl.BlockSpec((None, hr, C), lambda j, c_ref: (j, 0, 0))),
        name=name, compiler_params=_cp("parallel"))(core, slab, theirs)


def _chip_sum(recv, pairs, where, name):
    _, hr, C = recv.shape

    def body(w_ref, r_ref, p_ref, o_ref):
        me = w_ref[0]
        o_ref[...] = jnp.zeros_like(o_ref)
        for s in range(4):
            @pl.when(me == s)
            def _():
                o_ref[...] += p_ref[...].astype(F32)

            @pl.when(me != s)
            def _():
                o_ref[...] += r_ref[s].astype(F32)

    return pl.pallas_call(
        body, out_shape=jax.ShapeDtypeStruct((2, hr, C), F32),
        grid_spec=pltpu.PrefetchScalarGridSpec(
            num_scalar_prefetch=1, grid=(1,),
            in_specs=[pl.BlockSpec((4, hr, C), lambda i, w_ref: (0, 0, 0)),
                      pl.BlockSpec((None, hr, C), lambda i, w_ref: (w_ref[0], 0, 0))],
            out_specs=pl.BlockSpec((None, hr, C), lambda i, w_ref: (w_ref[1], 0, 0))),
        name=name, compiler_params=_cp("arbitrary"))(where, recv, pairs)


def _adamw(w, g, m, v, name):
    R, C = w.shape
    rb = min(ADAM_ROWS, R)
    c1 = 1.0 - ADAM_B1 ** ADAM_STEP
    c2 = 1.0 - ADAM_B2 ** ADAM_STEP

    def body(w_ref, g_ref, m_ref, v_ref, d_ref, nm_ref, nv_ref):
        gv = g_ref[...]
        nm = ADAM_B1 * m_ref[...] + (1.0 - ADAM_B1) * gv
        nv = ADAM_B2 * v_ref[...] + (1.0 - ADAM_B2) * (gv * gv)
        d_ref[...] = -ADAM_LR * ((nm / c1) / (jnp.sqrt(nv / c2) + ADAM_EPS) + ADAM_WD * w_ref[...])
        nm_ref[...] = nm
        nv_ref[...] = nv

    spec = _rowspec(rb, C)
    return pl.pallas_call(
        body, out_shape=(jax.ShapeDtypeStruct((R, C), F32),) * 3, grid=(R // rb,), in_specs=[spec] * 4,
        out_specs=(spec,) * 3, name=name, compiler_params=_cp("parallel"))(w, g, m, v)


class _Dist:
    def __init__(self, shards):
        self.shards = shards
        self.chip = 2 * lax.axis_index("x") + lax.axis_index("y")
        self.where = jnp.stack([self.chip, lax.axis_index("c")]).astype(jnp.int32)
        self.pairs, self.landed = {}, {}

    @staticmethod
    def _unshard(name, blk):
        _, (r, cc), axis = next(b for b in BIG if b[0] == name)
        return blk.reshape(4 * r, cc) if axis == 0 else blk.transpose(1, 0, 2).reshape(r, 4 * cc)

    def gather_first(self):
        got = _run_hosted(_gather_plan([self.shards["w_in"].astype(BF16)]), "gather_w_in")[0]
        return _split_w_in([got[j] for j in range(4)])

    def gather_rest(self):
        rest = [self.shards[n].astype(BF16) for n, _, _ in BIG if n != "w_in"]
        return _gather_plan(rest, [self.shards["conv_w"]])

    def weights_from(self, gathered):
        names = [n for n, _, _ in BIG if n != "w_in"]
        conv = gathered[-1]
        taps, width = conv.shape[1:]
        return ({n: self._unshard(n, g) for n, g in zip(names, gathered)},
                conv.transpose(1, 0, 2).reshape(taps, 4 * width))

    def scatter(self, G, names, tag):
        slabs = []
        for name, (r, cc), axis in BIG:
            if name not in names:
                continue
            g = G[name]
            if axis == 0:
                slabs.append(g.reshape(4, 2, r // 2, cc))
            else:
                slabs.append(g.reshape(4, 2, cc // 2, r))
        theirs = _swap_halves(slabs, "grad_swap_cores_" + tag)
        pairs = [_pair_sum(s, t, self.where[1:], "pair_sum_" + n) for s, t, n in zip(slabs, theirs, names)]
        self.pairs.update(zip(names, pairs))
        return _scatter_plan(pairs)

    def collect(self, names, landed):
        self.landed.update(zip(names, landed))

    def finish(self):
        names = [n for n, _, _ in BIG]
        halves = [_chip_sum(self.landed[n], self.pairs[n], self.where, "chip_sum_" + n) for n in names]
        out = {}
        for (name, (r, cc), axis), both in zip(BIG, _join_halves(halves)):
            full = both.reshape(-1, both.shape[-1])
            out[name] = full if axis == 0 else full.T
        return out


def _pack_rows(parts, rows, dtype):
    flat = jnp.concatenate([p.reshape(-1).astype(dtype) for p in parts])
    return jnp.pad(flat, (0, rows * LANES - flat.shape[0])).reshape(rows, LANES)


def _small_rows(n):
    return max(n // LANES, 1)


def _pack_small(vals):
    rows = []
    for name, n in SMALL:
        r = _small_rows(n)
        rows.append(jnp.pad(vals[name].reshape(-1), (0, r * LANES - n)).reshape(r, LANES))
    flat = jnp.concatenate(rows, axis=0)
    return jnp.pad(flat, ((0, SMALL_ROWS - flat.shape[0]), (0, 0)))


def _unpack_small(pack):
    out, r0 = {}, 0
    for name, n in SMALL:
        r = _small_rows(n)
        out[name] = pack[r0:r0 + r].reshape(-1)[:n]
        r0 += r
    return out


def kernel(x, mem, norm1_g, w_in, conv_w, a_log, dt_bias, gdn_norm_g, sb_q_norm_g, sb_k_norm_g, mem_norm_g, w_mem_kv, mem_q_norm_g, mem_k_norm_g, w_br_gdn, w_br_sb, w_br_mem, w_o, norm2_g, w_up, w_down, loss_target, m_norm1_g, m_w_in, m_conv_w, m_a_log, m_dt_bias, m_gdn_norm_g, m_sb_q_norm_g, m_sb_k_norm_g, m_mem_norm_g, m_w_mem_kv, m_mem_q_norm_g, m_mem_k_norm_g, m_w_br_gdn, m_w_br_sb, m_w_br_mem, m_w_o, m_norm2_g, m_w_up, m_w_down, v_norm1_g, v_w_in, v_conv_w, v_a_log, v_dt_bias, v_gdn_norm_g, v_sb_q_norm_g, v_sb_k_norm_g, v_mem_norm_g, v_w_mem_kv, v_mem_q_norm_g, v_mem_k_norm_g, v_w_br_gdn, v_w_br_sb, v_w_br_mem, v_w_o, v_norm2_g, v_w_up, v_w_down):
    wd = dict(norm1_g=norm1_g, w_in=w_in, conv_w=conv_w, a_log=a_log, dt_bias=dt_bias, gdn_norm_g=gdn_norm_g,
              sb_q_norm_g=sb_q_norm_g, sb_k_norm_g=sb_k_norm_g, mem_norm_g=mem_norm_g, w_mem_kv=w_mem_kv,
              mem_q_norm_g=mem_q_norm_g, mem_k_norm_g=mem_k_norm_g, w_br_gdn=w_br_gdn, w_br_sb=w_br_sb,
              w_br_mem=w_br_mem, w_o=w_o, norm2_g=norm2_g, w_up=w_up, w_down=w_down)
    md = dict(norm1_g=m_norm1_g, w_in=m_w_in, conv_w=m_conv_w, a_log=m_a_log, dt_bias=m_dt_bias,
              gdn_norm_g=m_gdn_norm_g, sb_q_norm_g=m_sb_q_norm_g, sb_k_norm_g=m_sb_k_norm_g,
              mem_norm_g=m_mem_norm_g, w_mem_kv=m_w_mem_kv, mem_q_norm_g=m_mem_q_norm_g,
              mem_k_norm_g=m_mem_k_norm_g, w_br_gdn=m_w_br_gdn, w_br_sb=m_w_br_sb, w_br_mem=m_w_br_mem, w_o=m_w_o,
              norm2_g=m_norm2_g, w_up=m_w_up, w_down=m_w_down)
    vd = dict(norm1_g=v_norm1_g, w_in=v_w_in, conv_w=v_conv_w, a_log=v_a_log, dt_bias=v_dt_bias,
              gdn_norm_g=v_gdn_norm_g, sb_q_norm_g=v_sb_q_norm_g, sb_k_norm_g=v_sb_k_norm_g,
              mem_norm_g=v_mem_norm_g, w_mem_kv=v_w_mem_kv, mem_q_norm_g=v_mem_q_norm_g,
              mem_k_norm_g=v_mem_k_norm_g, w_br_gdn=v_w_br_gdn, w_br_sb=v_w_br_sb, w_br_mem=v_w_br_mem, w_o=v_w_o,
              norm2_g=v_norm2_g, w_up=v_w_up, w_down=v_w_down)
    wd, md, vd = ({n: a[0] for n, a in d.items()} for d in (wd, md, vd))
    chip = 2 * lax.axis_index("x") + lax.axis_index("y")
    conv_shard = wd["conv_w"].shape

    dist = _Dist(wd)
    W = dict(zip(("w_main", "w_ab"), dist.gather_first()))
    P = {n: wd[n].reshape(1, -1) for n, _ in SMALL}

    sq_err, grad_x, G = _local_step(x[0], mem[0], loss_target[0], W, P, dist)
    loss = lax.psum(0.5 / D_MODEL * jnp.sum(sq_err), ("x", "y", "c"))

    g_big = dist.finish()

    spack = jnp.concatenate([_pack_small(G), G["conv_w"].reshape(CONV_ROWS, LANES)], axis=0)
    g_small = _sum_slots(_gather_all(spack, "gather_small_grads"), "small_grad_sum")
    g_conv_full = g_small[SMALL_ROWS:].reshape(conv_shard[0], 4 * conv_shard[1])
    g_conv = lax.dynamic_slice_in_dim(g_conv_full, chip * conv_shard[1], conv_shard[1], axis=1)

    grads, deltas, new_m, new_v = dict(g_big), {}, {}, {}
    for name, _, _ in BIG:
        deltas[name], new_m[name], new_v[name] = _adamw(wd[name], g_big[name], md[name], vd[name], "adamw_" + name)
    pack_sm = lambda d: jnp.concatenate([_pack_small(d), _pack_rows([d["conv_w"]], APACK_ROWS - SMALL_ROWS, F32)], axis=0)
    g_sm = jnp.concatenate([g_small[:SMALL_ROWS], _pack_rows([g_conv], APACK_ROWS - SMALL_ROWS, F32)], axis=0)
    small = (g_sm,) + _adamw(pack_sm(wd), g_sm, pack_sm(md), pack_sm(vd), "adamw_small")
    for out, pack in zip((grads, deltas, new_m, new_v), small):
        out.update(_unpack_small(pack[:SMALL_ROWS]))
        out["conv_w"] = pack[SMALL_ROWS:].reshape(-1)[:conv_shard[0] * conv_shard[1]].reshape(conv_shard)

    return (loss, grad_x[None], *[d[n][None] for d in (grads, deltas, new_m, new_v) for n in WEIGHTS])
```

```python
import jax
import jax.numpy as jnp
from jax import lax
from jax.experimental import pallas as pl
from jax.experimental.pallas import tpu as pltpu

F32 = jnp.float32
BF16 = jnp.bfloat16
MESH = pl.DeviceIdType.MESH

D_MODEL = 1024
N_HEAD = 4
D_HEAD = 128
BR_W = N_HEAD * D_HEAD
CONV_TAPS = 4
GDN_CHUNK = 64
INV_BLOCK = 16
INV_CHUNKS = 4
N_MEM = 256
D_FF = 4 * D_MODEL
EPS = 1e-6
LANES = 128
PROJ_W = 7168
GATE_OFF = 0
QKV_OFF = 3072
SB_OFF = 4608
Z_OFF = 6144
MEMQ_OFF = 6656
IN_GATE, IN_QKV, IN_SB, IN_Z, IN_MEMQ, IN_AB = (4104, 7176), (0, 1536), (2056, 3592), (1536, 2048), (3592, 4104), (2048, 2056)

ADAM_LR, ADAM_B1, ADAM_B2, ADAM_EPS, ADAM_WD, ADAM_STEP = 0.001, 0.9, 0.999, 1e-08, 0.01, 10

TM = 512
MM_TM = 1024
MM_TK = (2048, 1792, 1024, 128)
TK_TOK = 2048
GDN_STEP_CHUNKS = 4
GDN_BWD_STEP_CHUNKS = 1
G1_TM = 256
SB_BLK = 512
SB_DEAD = 120.0
VMEM_LIMIT = 48 << 20

BIG = (("w_in", (1024, 1794), 1), ("w_mem_kv", (256, 1024), 0), ("w_br_gdn", (512, 256), 1),
       ("w_br_sb", (512, 256), 1), ("w_br_mem", (512, 256), 1), ("w_o", (256, 1024), 0),
       ("w_up", (1024, 1024), 1), ("w_down", (1024, 1024), 0))
COL_SHARDED = tuple(n for n, _, a in BIG if a == 1)
GATE_COL = GATE_OFF // D_MODEL
QKV_COL = QKV_OFF // (3 * BR_W)
ROW_BLK = 1024
ADAM_ROWS = 128
SMALL = (("norm1_g", 1024), ("mem_norm_g", 1024), ("norm2_g", 1024), ("gdn_norm_g", 128), ("sb_q_norm_g", 128),
         ("sb_k_norm_g", 128), ("mem_q_norm_g", 128), ("mem_k_norm_g", 128), ("a_log", 4), ("dt_bias", 4))
SMALL_ROWS = 32
CONV_ROWS = 48
SPACK_ROWS = SMALL_ROWS + CONV_ROWS
APACK_ROWS = SMALL_ROWS + 16

WEIGHTS = ("norm1_g", "w_in", "conv_w", "a_log", "dt_bias", "gdn_norm_g", "sb_q_norm_g", "sb_k_norm_g",
           "mem_norm_g", "w_mem_kv", "mem_q_norm_g", "mem_k_norm_g", "w_br_gdn", "w_br_sb", "w_br_mem", "w_o",
           "norm2_g", "w_up", "w_down")


def _cp(*sem):
    return pltpu.CompilerParams(dimension_semantics=sem if sem else None, vmem_limit_bytes=VMEM_LIMIT)


HBM = pl.BlockSpec(memory_space=pl.ANY)

_NN = (((1,), (0,)), ((), ()))
_NT = (((1,), (1,)), ((), ()))
_TN = (((0,), (0,)), ((), ()))


def _dot(a, b, dims=_NN):
    return lax.dot_general(a.astype(BF16), b.astype(BF16), dims, preferred_element_type=F32)


def _dot_nt(a, b):
    return _dot(a, b, _NT)


def _dot_tn(a, b):
    return _dot(a, b, _TN)


def _dotf(a, b, dims=_NN):
    return lax.dot_general(a, b, dims, precision=lax.Precision.HIGHEST, preferred_element_type=F32)


def _sigmoid(v):
    return 0.5 * jnp.tanh(0.5 * v) + 0.5


def _softplus(v):
    return jnp.maximum(v, 0.0) + jnp.log(1.0 + jnp.exp(-jnp.abs(v)))


def _iota(shape, dim):
    return lax.broadcasted_iota(jnp.int32, shape, dim)


def _hs(h):
    return slice(h * D_HEAD, (h + 1) * D_HEAD)


def _rowspec(tm, w, col=0):
    return pl.BlockSpec((tm, w), lambda i: (i, col))


def _full(shape):
    return pl.BlockSpec(shape, lambda *_: (0,) * len(shape))


def _accum(ref, first, val):
    @pl.when(first)
    def _():
        ref[...] = val

    @pl.when(jnp.logical_not(first))
    def _():
        ref[...] += val


class _Hosted:
    def __init__(self, ins, out_shapes, n_sems, start, finish, mid=None, scratch=()):
        self.ins, self.out_shapes, self.n_sems = list(ins), list(out_shapes), n_sems
        self.start, self.mid, self.finish, self.scratch = start, mid, finish, list(scratch)


def _mm(a, b, mode, out_dtype, name, *, tm=None, tn=None, tk=None, a_fn=None, extra=None, epi=None, comm=None):
    if mode == "tn":
        (K, M), N = a.shape, b.shape[1]
    else:
        (M, K), N = a.shape, (b.shape[0] if mode == "nt" else b.shape[1])
    tm = min(tm or (1024 if mode == "tn" else MM_TM), M)
    tn = min(tn or 1024, N)
    tk = min(tk or (TK_TOK if mode == "tn" else next(t for t in MM_TK if K % t == 0)), K)
    nm, nn, nk = M // tm, N // tn, K // tk
    assert nm * tm == M and nn * tn == N and nk * tk == K, (name, a.shape, b.shape)
    if mode == "tn":
        a_spec = pl.BlockSpec((tk, tm), lambda i, j, k: (k, i))
    else:
        a_spec = pl.BlockSpec((tm, tk), lambda i, j, k: (i, k))
    if mode == "nt":
        b_spec = pl.BlockSpec((tn, tk), lambda i, j, k: (j, k))
    else:
        b_spec = pl.BlockSpec((tk, tn), lambda i, j, k: (k, j))
    dims = {"nn": _NN, "nt": _NT, "tn": _TN}[mode]
    o_spec = pl.BlockSpec((tm, tn), lambda i, j, k: (i, j))
    has_extra = extra is not None

    n_ci, n_co = (len(comm.ins), len(comm.out_shapes)) if comm else (0, 0)
    n_in = 2 + has_extra + n_ci
    steps = nm * nn * nk

    def body(*refs):
        a_ref, b_ref = refs[0], refs[1]
        e_ref = refs[2] if has_extra else None
        o_ref = refs[n_in]
        scratch = refs[n_in + 1 + n_co:]
        if comm:
            step = (pl.program_id(0) * nn + pl.program_id(1)) * nk + pl.program_id(2)
            cargs = (refs[2 + has_extra:n_in], refs[n_in + 1:n_in + 1 + n_co], scratch[nk > 1], scratch[(nk > 1) + 1],
                     scratch[(nk > 1) + 2:])
            pl.when(step == 0)(lambda: comm.start(*cargs))
            if comm.mid is not None:
                pl.when(step == (steps * 7) // 8)(lambda: comm.mid(*cargs))
        av = a_ref[...]
        if a_fn is not None:
            av = a_fn(av)
        p = lax.dot_general(av, b_ref[...], dims, preferred_element_type=F32)

        def finish(acc):
            if epi is not None:
                acc = epi(acc, e_ref[...] if has_extra else None)
            o_ref[...] = acc.astype(out_dtype)

        if nk == 1:
            finish(p)
        else:
            acc_ref = scratch[0]
            k = pl.program_id(2)
            _accum(acc_ref, k == 0, p)

            @pl.when(k == nk - 1)
            def _():
                finish(acc_ref[...])

        if comm:
            pl.when(step == steps - 1)(lambda: comm.finish(*cargs))

    ins = [a, b] + ([extra] if has_extra else [])
    in_specs = [a_spec, b_spec] + ([o_spec] if has_extra else [])
    scratch_shapes = [pltpu.VMEM((tm, tn), F32)] if nk > 1 else []
    main = jax.ShapeDtypeStruct((M, N), out_dtype)
    if not comm:
        return pl.pallas_call(
            body, out_shape=main, grid=(nm, nn, nk), in_specs=in_specs, out_specs=o_spec,
            scratch_shapes=scratch_shapes, name=name, compiler_params=_cp("parallel", "parallel", "arbitrary"))(*ins)
    sems = [pltpu.SemaphoreType.DMA((comm.n_sems,)), pltpu.SemaphoreType.DMA((comm.n_sems,))]
    res = pl.pallas_call(
        body, out_shape=(main, *comm.out_shapes), grid=(nm, nn, nk), in_specs=in_specs + [HBM] * n_ci,
        out_specs=(o_spec, *[HBM] * n_co), scratch_shapes=scratch_shapes + sems + comm.scratch, name=name,
        compiler_params=_cp("arbitrary", "arbitrary", "arbitrary"))(*ins, *comm.ins)
    return res[0], list(res[1:])


def _relu2(u):
    r = jnp.maximum(u.astype(F32), 0.0)
    return (r * r).astype(BF16)


def _epi_add(acc, e):
    return acc + e.astype(F32)


def _epi_drelu2(acc, u):
    return acc * (2.0 * jnp.maximum(u.astype(F32), 0.0))


def _rms_fwd(x, g, name):
    T, dm = x.shape
    tm = min(TM, T)

    def body(x_ref, g_ref, h_ref):
        xv = x_ref[...]
        r = lax.rsqrt(jnp.mean(xv * xv, axis=-1, keepdims=True) + EPS)
        h_ref[...] = (xv * r * g_ref[...]).astype(BF16)

    return pl.pallas_call(
        body, out_shape=jax.ShapeDtypeStruct((T, dm), BF16), grid=(T // tm,),
        in_specs=[_rowspec(tm, dm), _full((1, dm))], out_specs=_rowspec(tm, dm), name=name,
        compiler_params=_cp("parallel"))(x, g)


def _rms_bwd(dh, x, g, resid, name):
    T, dm = x.shape
    tm = min(TM, T)

    def body(dh_ref, x_ref, g_ref, res_ref, dx_ref, dxb_ref, dg_ref):
        i = pl.program_id(0)
        xv = x_ref[...]
        r = lax.rsqrt(jnp.mean(xv * xv, axis=-1, keepdims=True) + EPS)
        y = xv * r
        dhv = dh_ref[...].astype(F32)
        dy = dhv * g_ref[...]
        dx = res_ref[...] + r * (dy - y * jnp.mean(dy * y, axis=-1, keepdims=True))
        dx_ref[...] = dx
        dxb_ref[...] = dx.astype(BF16)
        _accum(dg_ref, i == 0, jnp.sum(dhv * y, axis=0, keepdims=True))

    return pl.pallas_call(
        body,
        out_shape=(jax.ShapeDtypeStruct((T, dm), F32), jax.ShapeDtypeStruct((T, dm), BF16),
                   jax.ShapeDtypeStruct((1, dm), F32)),
        grid=(T // tm,),
        in_specs=[_rowspec(tm, dm), _rowspec(tm, dm), _full((1, dm)), _rowspec(tm, dm)],
        out_specs=(_rowspec(tm, dm), _rowspec(tm, dm), _full((1, dm))), name=name,
        compiler_params=_cp("arbitrary"))(dh, x, g, resid)


def _conv_tile(x_ref, halo_ref, w_ref, xpad, tm):
    i = pl.program_id(0)
    halo = halo_ref[...].astype(F32)[8:16]
    xpad[0:8, :] = jnp.where(i > 0, halo, 0.0)
    xpad[8:, :] = x_ref[...].astype(F32)
    w = w_ref[...]
    xc = w[0:1] * xpad[5:5 + tm, :]
    for j in range(1, CONV_TAPS):
        xc = xc + w[j:j + 1] * xpad[5 + j:5 + j + tm, :]
    return xc


def _gate_terms(ab_ref, av_ref):
    abv = ab_ref[...]
    av = av_ref[...]
    pre = abv + av[1:2]
    ea = jnp.exp(av[0:1])
    g = -ea * _softplus(pre)
    return abv, pre, ea, g


def _gdn_pre(proj, conv_w, ab, avec):
    T = proj.shape[0]
    tm = min(G1_TM, T)
    cw = 3 * BR_W

    def body(x_ref, halo_ref, w_ref, ab_ref, av_ref, q_ref, k_ref, v_ref, gb_ref, xpad):
        xc = _conv_tile(x_ref, halo_ref, w_ref, xpad, tm)
        y = xc * _sigmoid(xc)
        for h in range(N_HEAD):
            for off, ref, scale in ((0, q_ref, D_HEAD ** -0.5), (BR_W, k_ref, 1.0)):
                yh = y[:, off + h * D_HEAD:off + (h + 1) * D_HEAD]
                r = lax.rsqrt(jnp.sum(yh * yh, axis=-1, keepdims=True) + EPS)
                ref[:, _hs(h)] = yh * (r * scale)
        v_ref[...] = y[:, 2 * BR_W:]
        abv, _, _, g = _gate_terms(ab_ref, av_ref)
        lane = _iota((tm, LANES), 1)
        gb_ref[...] = jnp.where(lane < N_HEAD, g, jnp.where(lane < 2 * N_HEAD, _sigmoid(abv), 0.0))

    hb = tm // 16
    return pl.pallas_call(
        body,
        out_shape=(jax.ShapeDtypeStruct((T, BR_W), F32),) * 3 + (jax.ShapeDtypeStruct((T, LANES), F32),),
        grid=(T // tm,),
        in_specs=[_rowspec(tm, cw, QKV_COL), pl.BlockSpec((16, cw), lambda i: (jnp.maximum(i * hb - 1, 0), QKV_COL)),
                  _full((CONV_TAPS, cw)), _rowspec(tm, LANES), _full((2, LANES))],
        out_specs=(_rowspec(tm, BR_W),) * 3 + (_rowspec(tm, LANES),),
        scratch_shapes=[pltpu.VMEM((tm + 8, cw), F32)], name="gdn_pre",
        compiler_params=_cp("parallel"))(proj, proj, conv_w, ab, avec)


def _gdn_pre_bwd(proj, conv_w, ab, avec, dq, dk, dv, dgb):
    T = proj.shape[0]
    tm = min(G1_TM, T)
    cw = 3 * BR_W

    def body(x_ref, halo_ref, w_ref, ab_ref, av_ref, dq_ref, dk_ref, dv_ref, dgb_ref,
             dxc_ref, dab_ref, dcw_ref, dav_ref, xpad):
        i = pl.program_id(0)

        @pl.when(i == 0)
        def _():
            dcw_ref[...] = jnp.zeros_like(dcw_ref)
            dav_ref[...] = jnp.zeros_like(dav_ref)

        xc_all = _conv_tile(x_ref, halo_ref, w_ref, xpad, tm)
        for s in range(cw // D_HEAD):
            cs = slice(s * D_HEAD, (s + 1) * D_HEAD)
            xc = xc_all[:, cs]
            sg = _sigmoid(xc)
            yh = xc * sg
            h = s % N_HEAD
            if s < 2 * N_HEAD:
                dref, scale = (dq_ref, D_HEAD ** -0.5) if s < N_HEAD else (dk_ref, 1.0)
                r = lax.rsqrt(jnp.sum(yh * yh, axis=-1, keepdims=True) + EPS)
                yn = yh * r
                dn = dref[:, _hs(h)]
                dy = (scale * r) * (dn - yn * jnp.sum(yn * dn, axis=-1, keepdims=True))
            else:
                dy = dv_ref[:, _hs(h)]
            dxc = dy * (sg * (1.0 + xc * (1.0 - sg)))
            dxc_ref[:, cs] = dxc.astype(BF16)
            for j in range(CONV_TAPS):
                dcw_ref[j:j + 1, cs] += jnp.sum(dxc * xpad[5 + j:5 + j + tm, cs], axis=0, keepdims=True)

        abv, pre, ea, g = _gate_terms(ab_ref, av_ref)
        dgbv = dgb_ref[...]
        lane = _iota((tm, LANES), 1)
        is_a = lane < N_HEAD
        da = jnp.where(is_a, dgbv * (-ea) * _sigmoid(pre), 0.0)
        bs = _sigmoid(abv)
        db = jnp.where(jnp.logical_and(lane >= N_HEAD, lane < 2 * N_HEAD), dgbv * bs * (1.0 - bs), 0.0)
        dab_ref[...] = (da + db).astype(BF16)
        dav_ref[0:1, :] += jnp.sum(jnp.where(is_a, dgbv * g, 0.0), axis=0, keepdims=True)
        dav_ref[1:2, :] += jnp.sum(da, axis=0, keepdims=True)

    hb = tm // 16
    return pl.pallas_call(
        body,
        out_shape=(jax.ShapeDtypeStruct((T, cw), BF16), jax.ShapeDtypeStruct((T, LANES), BF16),
                   jax.ShapeDtypeStruct((CONV_TAPS, cw), F32), jax.ShapeDtypeStruct((2, LANES), F32)),
        grid=(T // tm,),
        in_specs=[_rowspec(tm, cw, QKV_COL), pl.BlockSpec((16, cw), lambda i: (jnp.maximum(i * hb - 1, 0), QKV_COL)),
                  _full((CONV_TAPS, cw)), _rowspec(tm, LANES), _full((2, LANES)),
                  _rowspec(tm, BR_W), _rowspec(tm, BR_W), _rowspec(tm, BR_W), _rowspec(tm, LANES)],
        out_specs=(_rowspec(tm, cw), _rowspec(tm, LANES), _full((CONV_TAPS, cw)), _full((2, LANES))),
        scratch_shapes=[pltpu.VMEM((tm + 8, cw), F32)], name="gdn_pre_bwd",
        compiler_params=_cp("arbitrary"))(proj, proj, conv_w, ab, avec, dq, dk, dv, dgb)


def _conv_bwd(dxc, conv_w, into):
    T, cw = dxc.shape
    tm = min(G1_TM, T)
    nt = T // tm
    hb = tm // 16

    def body(d_ref, halo_ref, w_ref, into_ref, dx_ref, xpad):
        i = pl.program_id(0)
        xpad[0:tm, :] = d_ref[...].astype(F32)
        xpad[tm:, :] = jnp.where(i < nt - 1, halo_ref[...].astype(F32)[0:8], 0.0)
        w = w_ref[...]
        dx = w[3:4] * xpad[0:tm, :]
        for j in range(CONV_TAPS - 1):
            dx = dx + w[j:j + 1] * xpad[3 - j:3 - j + tm, :]
        dx_ref[...] = dx.astype(BF16)

    return pl.pallas_call(
        body, out_shape=jax.ShapeDtypeStruct(into.shape, BF16), grid=(nt,),
        in_specs=[_rowspec(tm, cw), pl.BlockSpec((16, cw), lambda i: (jnp.minimum((i + 1) * hb, T // 16 - 1), 0)),
                  _full((CONV_TAPS, cw)), HBM],
        out_specs=_rowspec(tm, cw, QKV_COL), scratch_shapes=[pltpu.VMEM((tm + 8, cw), F32)],
        input_output_aliases={3: 0}, name="conv_bwd", compiler_params=_cp("parallel"))(dxc, dxc, conv_w, into)


def _chunk_consts():
    C = GDN_CHUNK
    row, col = _iota((C, C), 0), _iota((C, C), 1)
    return row, col, row >= col, row > col


def _chunk_decay(gbv, incl):
    c_all = _dotf(incl.astype(F32), gbv)
    c_t = jnp.concatenate([c_all, jnp.zeros_like(c_all)], axis=0).T[:, :GDN_CHUNK]
    return c_all, c_t


def _head_decay(c_all, c_t, gbv, incl, h):
    C = GDN_CHUNK
    c_col = c_all[:, h:h + 1]
    c_row = c_t[h:h + 1, :]
    gam = jnp.exp(jnp.where(incl, c_col - c_row, -1e30))
    c_last = c_all[C - 1:C, h:h + 1]
    return gam, jnp.exp(c_col), jnp.exp(c_last - c_col), jnp.exp(c_last), gbv[:, N_HEAD + h:N_HEAD + h + 1]


def _split_bf16(x):
    hi = x.astype(BF16)
    return hi, (x - hi.astype(F32)).astype(BF16)


def _dot3(a, b):
    ah, al = _split_bf16(a)
    bh, bl = _split_bf16(b)
    d = lambda u, v: lax.dot_general(u, v, _NN, preferred_element_type=F32)
    return d(ah, bh) + (d(ah, bl) + d(al, bh))


def _unit_lower_inverses(ms, row, col):
    bi, bj = row // INV_BLOCK, col // INV_BLOCK
    eye = (row == col).astype(F32)
    ns = [jnp.where(bi == bj, -m, 0.0) for m in ms]
    invs = [eye + n for n in ns]
    size = 2
    while size < INV_BLOCK:
        ns = [_dot3(n, n) for n in ns]
        invs = [inv + _dot3(inv, n) for inv, n in zip(invs, ns)]
        size *= 2
    width = 2
    while width * INV_BLOCK <= GDN_CHUNK:
        sel = jnp.logical_and(bi // width == bj // width, bi // (width // 2) > bj // (width // 2))
        ts = [_dot3(inv, jnp.where(sel, m, 0.0)) for inv, m in zip(invs, ms)]
        invs = [inv - _dot3(t, inv) for inv, t in zip(invs, ts)]
        width *= 2
    return invs


def _gdn_inv(k, gb):
    T = k.shape[0]
    C = GDN_CHUNK
    per = min(INV_CHUNKS, T // C)
    rows = per * C

    def body(k_ref, gb_ref, ti_ref, tt_ref):
        row, col, incl, strict = _chunk_consts()
        ms = []
        for ci in range(per):
            rs = slice(ci * C, (ci + 1) * C)
            gbv = gb_ref[rs, :]
            c_all, c_t = _chunk_decay(gbv, incl)
            for h in range(N_HEAD):
                gam, _, _, _, bcol = _head_decay(c_all, c_t, gbv, incl, h)
                K = k_ref[rs, _hs(h)]
                ms.append(jnp.where(strict, _dot_nt(K * bcol, K) * gam, 0.0))
        eye = (row == col).astype(BF16)
        for i, inv in enumerate(_unit_lower_inverses(ms, row, col)):
            ti_ref[i // N_HEAD, i % N_HEAD] = inv
            tt_ref[i // N_HEAD, i % N_HEAD] = _dot_tn(inv, eye).astype(BF16)

    spec = pl.BlockSpec((per, N_HEAD, C, C), lambda i: (i, 0, 0, 0))
    return pl.pallas_call(
        body, out_shape=(jax.ShapeDtypeStruct((T // C, N_HEAD, C, C), F32),
                         jax.ShapeDtypeStruct((T // C, N_HEAD, C, C), BF16)),
        grid=(T // rows,), in_specs=[_rowspec(rows, BR_W), _rowspec(rows, LANES)], out_specs=(spec, spec),
        name="gdn_inv", compiler_params=_cp("parallel"))(k, gb)


def _gdn_fwd(q, k, v, gb, proj, gnorm, tinv_all):
    T = q.shape[0]
    C = GDN_CHUNK
    nc = T // C
    per = min(GDN_STEP_CHUNKS, nc)
    zcol = Z_OFF // BR_W
    heads = range(N_HEAD)

    def body(q_ref, k_ref, v_ref, gb_ref, z_ref, gn_ref, ti_ref, og_ref, oraw_ref, sh_ref, vn_ref, s_ref):
        @pl.when(pl.program_id(0) == 0)
        def _():
            s_ref[...] = jnp.zeros_like(s_ref)

        _, _, incl, _ = _chunk_consts()
        S = [s_ref[h] for h in heads]
        for ci in range(per):
            rs = slice(ci * C, (ci + 1) * C)
            gbv = gb_ref[rs, :]
            c_all, c_t = _chunk_decay(gbv, incl)
            dec = [_head_decay(c_all, c_t, gbv, incl, h) for h in heads]
            gam, gcol, dcol, glast, bcol = ([d[i] for d in dec] for i in range(5))
            Q = [q_ref[rs, _hs(h)] for h in heads]
            K = [k_ref[rs, _hs(h)] for h in heads]
            V = [v_ref[rs, _hs(h)] for h in heads]
            Sb = [s.astype(BF16) for s in S]
            KS = [_dot(K[h], Sb[h]) for h in heads]
            QS = [_dot(Q[h], Sb[h]) for h in heads]
            P = [_dot_nt(Q[h], K[h]) * gam[h] for h in heads]
            R = [bcol[h] * (V[h] - gcol[h] * KS[h]) for h in heads]
            vn = [_dot(ti_ref[ci, h], R[h]) for h in heads]
            O = [gcol[h] * QS[h] + _dot(P[h], vn[h]) for h in heads]
            Sn = [glast[h] * S[h] + _dot_tn(K[h] * dcol[h], vn[h]) for h in heads]
            for h in heads:
                sh_ref[ci, h] = S[h]
                vn_ref[rs, _hs(h)] = vn[h]
                oraw_ref[rs, _hs(h)] = O[h]
                rr = lax.rsqrt(jnp.mean(O[h] * O[h], axis=-1, keepdims=True) + EPS)
                zz = z_ref[rs, _hs(h)].astype(F32)
                og_ref[rs, _hs(h)] = (O[h] * rr * gn_ref[...] * (zz * _sigmoid(zz))).astype(BF16)
            S = Sn
        for h in heads:
            s_ref[h] = S[h]

    cspec = lambda w, cb=0: pl.BlockSpec((per * C, w), lambda n: (n, cb))
    hist = lambda a, b: pl.BlockSpec((per, N_HEAD, a, b), lambda n: (n, 0, 0, 0))
    return pl.pallas_call(
        body,
        out_shape=(jax.ShapeDtypeStruct((T, BR_W), BF16), jax.ShapeDtypeStruct((T, BR_W), F32),
                   jax.ShapeDtypeStruct((nc, N_HEAD, D_HEAD, D_HEAD), F32), jax.ShapeDtypeStruct((T, BR_W), F32)),
        grid=(nc // per,),
        in_specs=[cspec(BR_W), cspec(BR_W), cspec(BR_W), cspec(LANES), cspec(BR_W, zcol), _full((1, D_HEAD)),
                  hist(C, C)],
        out_specs=(cspec(BR_W), cspec(BR_W), hist(D_HEAD, D_HEAD), cspec(BR_W)),
        scratch_shapes=[pltpu.VMEM((N_HEAD, D_HEAD, D_HEAD), F32)], name="gdn_chunk_fwd",
        compiler_params=_cp("arbitrary"))(q, k, v, gb, proj, gnorm, tinv_all)


def _gdn_bwd(q, k, v, gb, proj, gnorm, oraw, shist, tinv_all, vn_all, dog, into):
    T = q.shape[0]
    C = GDN_CHUNK
    nc = T // C
    per = min(GDN_BWD_STEP_CHUNKS, nc)
    zcol = Z_OFF // BR_W

    def body(q_ref, k_ref, v_ref, gb_ref, z_ref, gn_ref, oraw_ref, sh_ref, tt_ref, vn_ref, dog_ref, into_ref,
             dq_ref, dk_ref, dv_ref, dgb_ref, dz_ref, dgn_ref, ds_ref):
        @pl.when(pl.program_id(0) == 0)
        def _():
            ds_ref[...] = jnp.zeros_like(ds_ref)
            dgn_ref[...] = jnp.zeros_like(dgn_ref)

        row, col, incl, strict = _chunk_consts()
        lane = _iota((C, LANES), 1)
        rowl = _iota((C, LANES), 0)
        eye = (row == col).astype(F32)
        upper = (col >= row).astype(F32)
        gn = gn_ref[...]
        heads = range(N_HEAD)
        rsum = lambda a: jnp.sum(a, axis=-1, keepdims=True)
        dgn = jnp.zeros((1, D_HEAD), F32)
        dSn = [ds_ref[h] for h in heads]
        for ci in reversed(range(per)):
            rs = slice(ci * C, (ci + 1) * C)
            gbv = gb_ref[rs, :]
            c_all, c_t = _chunk_decay(gbv, incl)
            dec = [_head_decay(c_all, c_t, gbv, incl, h) for h in heads]
            gam, gcol, dcol, glast, bcol = ([d[i] for d in dec] for i in range(5))
            Q = [q_ref[rs, _hs(h)] for h in heads]
            K = [k_ref[rs, _hs(h)] for h in heads]
            V = [v_ref[rs, _hs(h)] for h in heads]
            dO = []
            for h in heads:
                O = oraw_ref[rs, _hs(h)]
                zz = z_ref[rs, _hs(h)].astype(F32)
                dogv = dog_ref[rs, _hs(h)].astype(F32)
                rr = lax.rsqrt(jnp.mean(O * O, axis=-1, keepdims=True) + EPS)
                on = O * rr
                sg = _sigmoid(zz)
                dz_ref[rs, _hs(h)] = (dogv * on * gn * (sg * (1.0 + zz * (1.0 - sg)))).astype(BF16)
                dyn = dogv * (zz * sg)
                dgn = dgn + jnp.sum(dyn * on, axis=0, keepdims=True)
                dyv = dyn * gn
                dO.append((rr * (dyv - on * jnp.mean(dyv * on, axis=-1, keepdims=True))).astype(BF16))
            S = [sh_ref[ci, h] for h in heads]
            Sb = [s.astype(BF16) for s in S]
            vn = [vn_ref[rs, _hs(h)] for h in heads]
            vnb = [a.astype(BF16) for a in vn]
            dSb = [a.astype(BF16) for a in dSn]
            Kb = [K[h] * bcol[h] for h in heads]
            gam_t = [jnp.exp(jnp.where(col >= row, c_t[h:h + 1, :] - c_all[:, h:h + 1], -1e30)) for h in heads]
            M = [jnp.where(strict, _dot_nt(Kb[h], K[h]) * gam[h], 0.0) for h in heads]
            P = [_dot_nt(Q[h], K[h]) * gam[h] for h in heads]
            P_t = [_dot_nt(K[h], Q[h]) * gam_t[h] for h in heads]
            KS = [_dot(K[h], Sb[h]) for h in heads]
            QS = [_dot(Q[h], Sb[h]) for h in heads]
            dvn = [_dot(P_t[h], dO[h]) + _dot(K[h] * dcol[h], dSb[h]) for h in heads]
            dR = [_dot(tt_ref[ci, h], dvn[h]) for h in heads]
            dRb = [a.astype(BF16) for a in dR]
            bg = [bcol[h] * gcol[h] for h in heads]
            dS_new = [glast[h] * dSn[h] + _dot_tn(gcol[h] * Q[h], dO[h]) - _dot_tn(bg[h] * K[h], dRb[h])
                      for h in heads]
            dP = [jnp.where(incl, _dot_nt(dO[h], vnb[h]), 0.0) for h in heads]
            dM = [jnp.where(strict, -_dot_nt(dRb[h], vnb[h]), 0.0) for h in heads]
            dPG = [(dP[h] * gam[h]).astype(BF16) for h in heads]
            dMG = [(dM[h] * gam[h]).astype(BF16) for h in heads]
            dPG_t = [(jnp.where(col >= row, _dot_nt(vnb[h], dO[h]), 0.0) * gam_t[h]).astype(BF16) for h in heads]
            dMG_t = [(jnp.where(col > row, -_dot_nt(vnb[h], dRb[h]), 0.0) * gam_t[h]).astype(BF16) for h in heads]
            E = [_dot_nt(vnb[h], dSb[h]) for h in heads]
            dKb = [_dot(dMG[h], K[h]) for h in heads]
            dc_all = jnp.zeros((C, LANES), F32)
            db_all = jnp.zeros((C, LANES), F32)
            for h in heads:
                dq_ref[rs, _hs(h)] = gcol[h] * _dot_nt(dO[h], Sb[h]) + _dot(dPG[h], K[h])
                dk_ref[rs, _hs(h)] = (_dot(dPG_t[h], Q[h]) + _dot(dMG_t[h], Kb[h]) + bcol[h] * dKb[h]
                                      - bg[h] * _dot_nt(dRb[h], Sb[h]) + dcol[h] * E[h])
                dv_ref[rs, _hs(h)] = bcol[h] * dR[h]
                dbeta = rsum(dKb[h] * K[h]) + rsum(dR[h] * (V[h] - gcol[h] * KS[h]))
                X = dP[h] * P[h] + dM[h] * M[h]
                ddel = rsum(K[h] * E[h]) * dcol[h]
                colsum = rsum(eye * jnp.sum(X, axis=0, keepdims=True))
                dc = (rsum(X) - colsum + gcol[h] * rsum(dO[h].astype(F32) * QS[h]) - bg[h] * rsum(dR[h] * KS[h])
                      - ddel)
                last = (jnp.sum(ddel, axis=0, keepdims=True)
                        + glast[h] * jnp.sum(rsum(dSn[h] * S[h]), axis=0, keepdims=True))
                dc_all = dc_all + jnp.where(lane == h, dc + jnp.where(rowl == C - 1, last, 0.0), 0.0)
                db_all = db_all + jnp.where(lane == N_HEAD + h, dbeta, 0.0)
            dgb_ref[rs, :] = _dotf(upper, dc_all) + db_all
            dSn = dS_new
        for h in heads:
            ds_ref[h] = dSn[h]
        dgn_ref[...] += dgn

    nb = nc // per
    cspec = lambda w, cb=0: pl.BlockSpec((per * C, w), lambda n: (nb - 1 - n, cb))
    hist = lambda a, b: pl.BlockSpec((per, N_HEAD, a, b), lambda n: (nb - 1 - n, 0, 0, 0))
    return pl.pallas_call(
        body,
        out_shape=(jax.ShapeDtypeStruct((T, BR_W), F32),) * 3 + (
            jax.ShapeDtypeStruct((T, LANES), F32), jax.ShapeDtypeStruct(into.shape, BF16),
            jax.ShapeDtypeStruct((1, D_HEAD), F32)),
        grid=(nb,),
        in_specs=[cspec(BR_W), cspec(BR_W), cspec(BR_W), cspec(LANES), cspec(BR_W, zcol), _full((1, D_HEAD)),
                  cspec(BR_W), hist(D_HEAD, D_HEAD), hist(C, C), cspec(BR_W), cspec(BR_W), HBM],
        out_specs=(cspec(BR_W), cspec(BR_W), cspec(BR_W), cspec(LANES), cspec(BR_W, zcol), _full((1, D_HEAD))),
        scratch_shapes=[pltpu.VMEM((N_HEAD, D_HEAD, D_HEAD), F32)], input_output_aliases={11: 4},
        name="gdn_chunk_bwd",
        compiler_params=_cp("arbitrary"))(q, k, v, gb, proj, gnorm, oraw, shist, tinv_all, vn_all, dog, into)


SB_COL = SB_OFF // BR_W
SB_SCALE = D_HEAD ** -0.5


def _sb_pre(proj, gq, gk):
    T = proj.shape[0]
    tm = min(TM, T)

    def body(xq_ref, xk_ref, xv_ref, gq_ref, gk_ref, q_ref, k_ref, v_ref):
        for h in range(N_HEAD):
            for x_ref, g_ref, ref, scale in ((xq_ref, gq_ref, q_ref, SB_SCALE), (xk_ref, gk_ref, k_ref, 1.0)):
                xh = x_ref[:, _hs(h)].astype(F32)
                r = lax.rsqrt(jnp.mean(xh * xh, axis=-1, keepdims=True) + EPS)
                ref[:, _hs(h)] = (xh * (r * scale) * g_ref[...]).astype(BF16)
        v_ref[...] = xv_ref[...]

    return pl.pallas_call(
        body, out_shape=(jax.ShapeDtypeStruct((T, BR_W), BF16),) * 3, grid=(T // tm,),
        in_specs=[_rowspec(tm, BR_W, SB_COL), _rowspec(tm, BR_W, SB_COL + 1), _rowspec(tm, BR_W, SB_COL + 2),
                  _full((1, D_HEAD)), _full((1, D_HEAD))],
        out_specs=(_rowspec(tm, BR_W),) * 3, name="sb_pre", compiler_params=_cp("parallel"))(proj, proj, proj, gq, gk)


def _sb_pre_bwd(proj, gq, gk, dq, dk, dv, into):
    T = proj.shape[0]
    tm = min(TM, T)

    def body(xq_ref, xk_ref, gq_ref, gk_ref, dq_ref, dk_ref, dv_ref, into_ref, dx_ref, dgq_ref, dgk_ref):
        i = pl.program_id(0)

        @pl.when(i == 0)
        def _():
            dgq_ref[...] = jnp.zeros_like(dgq_ref)
            dgk_ref[...] = jnp.zeros_like(dgk_ref)

        for off, x_ref, g_ref, d_ref, dg_ref, scale in ((0, xq_ref, gq_ref, dq_ref, dgq_ref, SB_SCALE),
                                                        (BR_W, xk_ref, gk_ref, dk_ref, dgk_ref, 1.0)):
            dg = jnp.zeros((1, D_HEAD), F32)
            for h in range(N_HEAD):
                xh = x_ref[:, _hs(h)].astype(F32)
                r = lax.rsqrt(jnp.mean(xh * xh, axis=-1, keepdims=True) + EPS)
                y = xh * r
                dn = d_ref[:, _hs(h)] * scale
                dg = dg + jnp.sum(dn * y, axis=0, keepdims=True)
                dy = dn * g_ref[...]
                dx_ref[:, off + h * D_HEAD:off + (h + 1) * D_HEAD] = (
                    r * (dy - y * jnp.mean(dy * y, axis=-1, keepdims=True))).astype(BF16)
            dg_ref[...] += dg
        dx_ref[:, 2 * BR_W:] = dv_ref[...].astype(BF16)

    return pl.pallas_call(
        body,
        out_shape=(jax.ShapeDtypeStruct(into.shape, BF16), jax.ShapeDtypeStruct((1, D_HEAD), F32),
                   jax.ShapeDtypeStruct((1, D_HEAD), F32)),
        grid=(T // tm,),
        in_specs=[_rowspec(tm, BR_W, SB_COL), _rowspec(tm, BR_W, SB_COL + 1), _full((1, D_HEAD)), _full((1, D_HEAD)),
                  _rowspec(tm, BR_W), _rowspec(tm, BR_W), _rowspec(tm, BR_W), HBM],
        out_specs=(_rowspec(tm, 3 * BR_W, SB_OFF // (3 * BR_W)), _full((1, D_HEAD)), _full((1, D_HEAD))),
        input_output_aliases={7: 0}, name="sb_pre_bwd",
        compiler_params=_cp("arbitrary"))(proj, proj, gq, gk, dq, dk, dv, into)


def _sb_pair(q, k, masked):
    z = _dot_nt(q, k)
    zc = jnp.minimum(z, 30.0)
    sp = jnp.log(1.0 + jnp.exp(zc)) + (z - zc)
    if not masked:
        return z, sp, None
    mask = _iota(z.shape, 1) < _iota(z.shape, 0)
    return z, jnp.where(mask, sp, 0.0), mask


def _sb_fwd(sq, sk, sv):
    T = sq.shape[0]
    blk = min(SB_BLK, T)
    w = blk // 2
    nb = T // blk

    def body(q_ref, k_ref, v_ref, o_ref, lt_ref, cut_ref, acc_ref, r_ref):
        head, qi = pl.program_id(0), pl.program_id(1)
        acc_ref[...] = jnp.zeros_like(acc_ref)
        r_ref[...] = jnp.zeros_like(r_ref)
        after = (_iota((w, w), 0) > _iota((w, w), 1)).astype(BF16)

        def block(rows, kb, masked):
            keys = pl.ds(pl.multiple_of(kb * w, w), w)
            z, sp, mask = _sb_pair(q_ref[rows, :], k_ref[keys, :], masked)
            r = r_ref[rows, :]
            a = jnp.exp(z - sp - _dot(sp, after) - r)
            if masked:
                a = jnp.where(mask, a, 0.0)
            acc_ref[rows, :] += _dot(a, v_ref[keys, :])
            r_ref[rows, :] = r + jnp.sum(sp, axis=-1, keepdims=True)

        def alive():
            return jnp.min(r_ref[...]) < SB_DEAD

        def further(state):
            kb, _ = state
            block(slice(0, blk), kb, False)
            return kb - 1, alive()

        block(slice(w, blk), 2 * qi + 1, True)
        block(slice(0, blk), 2 * qi, True)
        left, _ = lax.while_loop(lambda s: jnp.logical_and(s[0] >= 0, s[1]), further, (2 * qi - 1, alive()))
        o_ref[...] = acc_ref[...].astype(BF16)
        lt_ref[0] = r_ref[...]
        cut_ref[head, qi] = (left + 1).astype(F32)

    qspec = pl.BlockSpec((blk, D_HEAD), lambda h, i: (i, h))
    whole = pl.BlockSpec((T, D_HEAD), lambda h, i: (0, h))
    return pl.pallas_call(
        body,
        out_shape=(jax.ShapeDtypeStruct((T, BR_W), BF16), jax.ShapeDtypeStruct((N_HEAD, T, 1), F32),
                   jax.ShapeDtypeStruct((N_HEAD, nb), F32)),
        grid=(N_HEAD, nb), in_specs=[qspec, whole, whole],
        out_specs=(qspec, pl.BlockSpec((1, blk, 1), lambda h, i: (h, i, 0)), pl.BlockSpec(memory_space=pltpu.SMEM)),
        scratch_shapes=[pltpu.VMEM((blk, D_HEAD), F32), pltpu.VMEM((blk, 1), F32)],
        name="sb_fwd", compiler_params=_cp("arbitrary", "arbitrary"))(sq, sk, sv)


def _sb_bwd(sq, sk, sv, ltot, do, cut):
    T = sq.shape[0]
    blk = min(SB_BLK, T)
    w = blk // 2
    nb = T // blk

    def body(q_ref, k_ref, v_ref, lt_ref, do_ref, cut_ref, dq_ref, dk_ref, dv_ref, acc_ref, p_ref, g_ref):
        head, qi = pl.program_id(0), pl.program_id(1)
        first = cut_ref[head, qi].astype(jnp.int32)

        @pl.when(qi == 0)
        def _():
            dk_ref[...] = jnp.zeros_like(dk_ref)
            dv_ref[...] = jnp.zeros_like(dv_ref)

        acc_ref[...] = jnp.zeros_like(acc_ref)
        p_ref[...] = lt_ref[0]
        g_ref[...] = jnp.zeros_like(g_ref)
        after = (_iota((w, w), 0) > _iota((w, w), 1)).astype(BF16)
        before = (_iota((w, w), 0) < _iota((w, w), 1)).astype(BF16)

        def block(rows, kb, masked):
            keys = pl.ds(pl.multiple_of(kb * w, w), w)
            q, do = q_ref[rows, :], do_ref[rows, :]
            z, sp, mask = _sb_pair(q, k_ref[keys, :], masked)
            d_a = _dot_nt(do, v_ref[keys, :])
            rest = p_ref[rows, :] - jnp.sum(sp, axis=-1, keepdims=True)
            a = jnp.exp(z - sp - _dot(sp, after) - rest)
            if masked:
                a = jnp.where(mask, a, 0.0)
            g = a * d_a
            sig = jnp.exp(z - sp)
            dz = g - sig * (g + (g_ref[rows, :] + _dot(g, before)))
            if masked:
                dz = jnp.where(mask, dz, 0.0)
            dz = dz.astype(BF16)
            dv_ref[keys, :] += _dot_tn(a, do)
            dk_ref[keys, :] += _dot_tn(dz, q)
            acc_ref[rows, :] += _dot(dz, k_ref[keys, :])
            p_ref[rows, :] = rest
            g_ref[rows, :] += jnp.sum(g, axis=-1, keepdims=True)

        def step(kb, carry):
            block(slice(0, blk), kb, False)
            return carry

        lax.fori_loop(first, 2 * qi, step, 0)
        block(slice(0, blk), 2 * qi, True)
        block(slice(w, blk), 2 * qi + 1, True)
        dq_ref[...] = acc_ref[...]

    qspec = pl.BlockSpec((blk, D_HEAD), lambda h, i: (i, h))
    whole = pl.BlockSpec((T, D_HEAD), lambda h, i: (0, h))
    return pl.pallas_call(
        body, out_shape=(jax.ShapeDtypeStruct((T, BR_W), F32),) * 3, grid=(N_HEAD, nb),
        in_specs=[qspec, whole, whole, pl.BlockSpec((1, blk, 1), lambda h, i: (h, i, 0)), qspec,
                  pl.BlockSpec(memory_space=pltpu.SMEM)],
        out_specs=(qspec, whole, whole),
        scratch_shapes=[pltpu.VMEM((blk, D_HEAD), F32), pltpu.VMEM((blk, 1), F32), pltpu.VMEM((blk, 1), F32)],
        name="sb_bwd", compiler_params=_cp("arbitrary", "arbitrary"))(sq, sk, sv, ltot, do, cut)


def _mem_kv(mem, gm, w_kv, gk):
    def body(mem_ref, gm_ref, w_ref, gk_ref, mn_ref, kv_ref, kh_ref, vm_ref):
        mv = mem_ref[...]
        r = lax.rsqrt(jnp.mean(mv * mv, axis=-1, keepdims=True) + EPS)
        mn = (mv * r * gm_ref[...]).astype(BF16)
        mn_ref[...] = mn
        kv = lax.dot_general(mn, w_ref[...], _NN, preferred_element_type=F32)
        kv_ref[...] = kv
        for h in range(N_HEAD):
            kh = kv[:, _hs(h)]
            rk = lax.rsqrt(jnp.mean(kh * kh, axis=-1, keepdims=True) + EPS)
            kh_ref[:, _hs(h)] = (kh * rk * gk_ref[...]).astype(BF16)
        vm_ref[...] = kv[:, BR_W:].astype(BF16)

    return pl.pallas_call(
        body,
        out_shape=(jax.ShapeDtypeStruct((N_MEM, D_MODEL), BF16), jax.ShapeDtypeStruct((N_MEM, 2 * BR_W), F32),
                   jax.ShapeDtypeStruct((N_MEM, BR_W), BF16), jax.ShapeDtypeStruct((N_MEM, BR_W), BF16)),
        name="mem_kv", compiler_params=_cp())(mem, gm, w_kv, gk)


def _mem_q(x_ref, gq_ref, h):
    xh = x_ref[:, _hs(h)].astype(F32)
    r = lax.rsqrt(jnp.mean(xh * xh, axis=-1, keepdims=True) + EPS)
    return r, xh * r


def _mem_probs(qn, kh):
    s = _dot_nt(qn, kh) * (D_HEAD ** -0.5)
    e = jnp.exp(s - jnp.max(s, axis=-1, keepdims=True))
    return e / jnp.sum(e, axis=-1, keepdims=True)


def _mem_fwd(proj, kh, vm, gq):
    T = proj.shape[0]
    tm = min(TM, T)

    def body(x_ref, kh_ref, vm_ref, gq_ref, o_ref):
        for h in range(N_HEAD):
            _, y = _mem_q(x_ref, gq_ref, h)
            p = _mem_probs((y * gq_ref[...]).astype(BF16), kh_ref[:, _hs(h)])
            o_ref[:, _hs(h)] = _dot(p, vm_ref[:, _hs(h)]).astype(BF16)

    return pl.pallas_call(
        body, out_shape=jax.ShapeDtypeStruct((T, BR_W), BF16), grid=(T // tm,),
        in_specs=[_rowspec(tm, BR_W, MEMQ_OFF // BR_W), _full((N_MEM, BR_W)), _full((N_MEM, BR_W)),
                  _full((1, D_HEAD))],
        out_specs=_rowspec(tm, BR_W), name="mem_fwd", compiler_params=_cp("parallel"))(proj, kh, vm, gq)


def _mem_bwd(proj, kh, vm, gq, do, into):
    T = proj.shape[0]
    tm = min(TM, T)

    def body(x_ref, kh_ref, vm_ref, gq_ref, do_ref, into_ref, dx_ref, dkh_ref, dvm_ref, dgq_ref):
        i = pl.program_id(0)

        @pl.when(i == 0)
        def _():
            dkh_ref[...] = jnp.zeros_like(dkh_ref)
            dvm_ref[...] = jnp.zeros_like(dvm_ref)
            dgq_ref[...] = jnp.zeros_like(dgq_ref)

        dg = jnp.zeros((1, D_HEAD), F32)
        for h in range(N_HEAD):
            r, y = _mem_q(x_ref, gq_ref, h)
            qn = (y * gq_ref[...]).astype(BF16)
            p = _mem_probs(qn, kh_ref[:, _hs(h)])
            dov = do_ref[:, _hs(h)]
            dp = _dot_nt(dov, vm_ref[:, _hs(h)])
            ds = p * (dp - jnp.sum(dp * p, axis=-1, keepdims=True)) * (D_HEAD ** -0.5)
            dqn = _dot(ds, kh_ref[:, _hs(h)])
            dkh_ref[:, _hs(h)] += _dot_tn(ds, qn)
            dvm_ref[:, _hs(h)] += _dot_tn(p, dov)
            dg = dg + jnp.sum(dqn * y, axis=0, keepdims=True)
            dy = dqn * gq_ref[...]
            dx_ref[:, _hs(h)] = (r * (dy - y * jnp.mean(dy * y, axis=-1, keepdims=True))).astype(BF16)
        dgq_ref[...] += dg

    return pl.pallas_call(
        body,
        out_shape=(jax.ShapeDtypeStruct(into.shape, BF16), jax.ShapeDtypeStruct((N_MEM, BR_W), F32),
                   jax.ShapeDtypeStruct((N_MEM, BR_W), F32), jax.ShapeDtypeStruct((1, D_HEAD), F32)),
        grid=(T // tm,),
        in_specs=[_rowspec(tm, BR_W, MEMQ_OFF // BR_W), _full((N_MEM, BR_W)), _full((N_MEM, BR_W)),
                  _full((1, D_HEAD)), _rowspec(tm, BR_W), HBM],
        out_specs=(_rowspec(tm, BR_W, MEMQ_OFF // BR_W), _full((N_MEM, BR_W)), _full((N_MEM, BR_W)),
                   _full((1, D_HEAD))),
        input_output_aliases={5: 0}, name="mem_bwd", compiler_params=_cp("arbitrary"))(proj, kh, vm, gq, do, into)


def _mem_kv_bwd(mem, gm, w_kv, gk, kv, mn, dkh, dvm):
    def body(mem_ref, gm_ref, w_ref, gk_ref, kv_ref, mn_ref, dkh_ref, dvm_ref, dw_ref, dgm_ref, dgk_ref, dkv_ref):
        dgk = jnp.zeros((1, D_HEAD), F32)
        for h in range(N_HEAD):
            kh = kv_ref[:, _hs(h)]
            r = lax.rsqrt(jnp.mean(kh * kh, axis=-1, keepdims=True) + EPS)
            y = kh * r
            dn = dkh_ref[:, _hs(h)]
            dgk = dgk + jnp.sum(dn * y, axis=0, keepdims=True)
            dy = dn * gk_ref[...]
            dkv_ref[:, _hs(h)] = (r * (dy - y * jnp.mean(dy * y, axis=-1, keepdims=True))).astype(BF16)
        dkv_ref[:, BR_W:] = dvm_ref[...].astype(BF16)
        dgk_ref[...] = dgk
        dkv = dkv_ref[...]
        dw_ref[...] = lax.dot_general(mn_ref[...], dkv, _TN, preferred_element_type=F32)
        dmn = lax.dot_general(dkv, w_ref[...], _NT, preferred_element_type=F32)
        mv = mem_ref[...]
        memn = mv * lax.rsqrt(jnp.mean(mv * mv, axis=-1, keepdims=True) + EPS)
        dgm_ref[...] = jnp.sum(dmn * memn, axis=0, keepdims=True)

    return pl.pallas_call(
        body,
        out_shape=(jax.ShapeDtypeStruct((D_MODEL, 2 * BR_W), F32), jax.ShapeDtypeStruct((1, D_MODEL), F32),
                   jax.ShapeDtypeStruct((1, D_HEAD), F32)),
        scratch_shapes=[pltpu.VMEM((N_MEM, 2 * BR_W), BF16)], name="mem_kv_bwd",
        compiler_params=_cp())(mem, gm, w_kv, gk, kv, mn, dkh, dvm)


def _merge_fwd(og, osb, om, proj, wg, ws, wm):
    T = og.shape[0]
    tm = min(TM, T)

    def body(og_ref, os_ref, om_ref, g0, g1, g2, wg_ref, ws_ref, wm_ref, mix_ref, yg_ref, ys_ref, ym_ref):
        mix = jnp.zeros((tm, D_MODEL), F32)
        for o_ref, gl_ref, w_ref, y_ref in ((og_ref, g0, wg_ref, yg_ref), (os_ref, g1, ws_ref, ys_ref),
                                            (om_ref, g2, wm_ref, ym_ref)):
            y = lax.dot_general(o_ref[...], w_ref[...], _NN, preferred_element_type=F32)
            y_ref[...] = y.astype(BF16)
            mix = mix + _sigmoid(gl_ref[...].astype(F32)) * y
        mix_ref[...] = mix.astype(BF16)

    br = _rowspec(tm, BR_W)
    wspec = _full((BR_W, D_MODEL))
    out = _rowspec(tm, D_MODEL)
    gates = [_rowspec(tm, D_MODEL, GATE_COL + b) for b in range(3)]
    return pl.pallas_call(
        body, out_shape=(jax.ShapeDtypeStruct((T, D_MODEL), BF16),) * 4, grid=(T // tm,),
        in_specs=[br, br, br, *gates, wspec, wspec, wspec],
        out_specs=(out,) * 4, name="merge_fwd",
        compiler_params=_cp("parallel"))(og, osb, om, proj, proj, proj, wg, ws, wm)


def _merge_bwd(dmix, proj, ys, os_, ws):
    T = dmix.shape[0]
    tm = min(TM, T)

    def body(dmix_ref, g0, g1, g2, y0, y1, y2, o0, o1, o2, w0, w1, w2, dgl_ref, do0, do1, do2, dw0, dw1, dw2):
        i = pl.program_id(0)
        dm = dmix_ref[...].astype(F32)
        for b, (gl_ref, y_ref, o_ref, w_ref, do_ref, dw_ref) in enumerate((
                (g0, y0, o0, w0, do0, dw0), (g1, y1, o1, w1, do1, dw1), (g2, y2, o2, w2, do2, dw2))):
            gate = _sigmoid(gl_ref[...].astype(F32))
            dgl_ref[:, b * D_MODEL:(b + 1) * D_MODEL] = (dm * y_ref[...].astype(F32) * gate * (1.0 - gate)).astype(BF16)
            dy = (gate * dm).astype(BF16)
            do_ref[...] = lax.dot_general(dy, w_ref[...], _NT, preferred_element_type=F32).astype(BF16)
            _accum(dw_ref, i == 0, lax.dot_general(dy, o_ref[...], _TN, preferred_element_type=F32))

    br = _rowspec(tm, BR_W)
    wide = _rowspec(tm, D_MODEL)
    wspec = _full((BR_W, D_MODEL))
    wtspec = _full((D_MODEL, BR_W))
    gates = [_rowspec(tm, D_MODEL, GATE_COL + b) for b in range(3)]
    return pl.pallas_call(
        body,
        out_shape=(jax.ShapeDtypeStruct((T, PROJ_W), BF16),) + (jax.ShapeDtypeStruct((T, BR_W), BF16),) * 3
        + (jax.ShapeDtypeStruct((D_MODEL, BR_W), F32),) * 3,
        grid=(T // tm,),
        in_specs=[wide, *gates, wide, wide, wide, br, br, br, wspec, wspec, wspec],
        out_specs=(_rowspec(tm, 3 * D_MODEL, GATE_OFF // (3 * D_MODEL)), br, br, br, wtspec, wtspec, wtspec),
        name="merge_bwd",
        compiler_params=_cp("arbitrary"))(dmix, proj, proj, proj, *ys, *os_, *ws)


def _loss(y, tgt):
    T, dm = y.shape
    tm = min(TM, T)

    def body(y_ref, t_ref, dy_ref, dyb_ref, sq_ref):
        err = y_ref[...] - t_ref[...]
        dy = err * (1.0 / dm)
        dy_ref[...] = dy
        dyb_ref[...] = dy.astype(BF16)
        _accum(sq_ref, pl.program_id(0) == 0, jnp.sum(err * err, axis=0, keepdims=True))

    return pl.pallas_call(
        body,
        out_shape=(jax.ShapeDtypeStruct((T, dm), F32), jax.ShapeDtypeStruct((T, dm), BF16),
                   jax.ShapeDtypeStruct((1, dm), F32)),
        grid=(T // tm,), in_specs=[_rowspec(tm, dm), _rowspec(tm, dm)],
        out_specs=(_rowspec(tm, dm), _rowspec(tm, dm), _full((1, dm))), name="loss",
        compiler_params=_cp("arbitrary"))(y, tgt)


def _split_w_in(slabs):
    width = slabs[0].shape[1]

    def cols(lo, hi):
        return [s[:, max(lo - j * width, 0):min(hi - j * width, width)] for j, s in enumerate(slabs)
                if lo < (j + 1) * width and hi > j * width]

    main = [c for piece in (IN_GATE, IN_QKV, IN_SB, IN_Z, IN_MEMQ) for c in cols(*piece)]
    ab = jnp.concatenate(cols(*IN_AB), axis=1)
    return jnp.concatenate(main, axis=1), jnp.pad(ab, ((0, 0), (0, LANES - ab.shape[1])))


def _local_step(x, mem, tgt, W, P, dist=None):
    w_main, w_ab = W["w_main"], W["w_ab"]
    avec = jnp.pad(jnp.concatenate([P["a_log"], P["dt_bias"]], axis=0), ((0, 0), (0, LANES - N_HEAD)))

    h = _rms_fwd(x, P["norm1_g"], "rms1")
    if dist is None:
        proj = _mm(h, w_main, "nn", BF16, "in_proj", tn=1792)
    else:
        proj, gathered = _mm(h, w_main, "nn", BF16, "in_proj", tn=1792, comm=dist.gather_rest())
        rest, conv_w = dist.weights_from(gathered)
        W, P = {**W, **rest}, {**P, "conv_w": conv_w}
    wbr = (W["w_br_gdn"], W["w_br_sb"], W["w_br_mem"])
    ab = _mm(h, w_ab, "nn", F32, "in_proj_ab")
    q, k, v, gb = _gdn_pre(proj, P["conv_w"], ab, avec)
    tinv, tinv_t = _gdn_inv(k, gb)
    og, oraw, shist, vn = _gdn_fwd(q, k, v, gb, proj, P["gdn_norm_g"], tinv)
    sq, sk, sv = _sb_pre(proj, P["sb_q_norm_g"], P["sb_k_norm_g"])
    osb, ltot, cut = _sb_fwd(sq, sk, sv)
    mn, kv, kh, vm = _mem_kv(mem, P["mem_norm_g"], W["w_mem_kv"], P["mem_k_norm_g"])
    om = _mem_fwd(proj, kh, vm, P["mem_q_norm_g"])
    mix, yg, ys, ym = _merge_fwd(og, osb, om, proj, *wbr)
    x1 = _mm(mix, W["w_o"], "nn", F32, "out_proj", extra=x, epi=_epi_add)
    h2 = _rms_fwd(x1, P["norm2_g"], "rms2")
    u = _mm(h2, W["w_up"], "nn", BF16, "mlp_up", tn=2048)
    y = _mm(u, W["w_down"], "nn", F32, "mlp_down", a_fn=_relu2, extra=x1, epi=_epi_add)
    dy, dyb, sq_err = _loss(y, tgt)

    G = {}
    du = _mm(dyb, W["w_down"], "nt", BF16, "d_mlp_act", tn=2048, extra=u, epi=_epi_drelu2)
    G["w_down"] = _mm(u, dyb, "tn", F32, "dw_down", a_fn=_relu2)
    G["w_up"] = _mm(du, h2, "tn", F32, "dw_up")
    dh2 = _mm(du, W["w_up"], "nt", F32, "d_h2")
    dx1, dx1b, G["norm2_g"] = _rms_bwd(dh2, x1, P["norm2_g"], dy, "rms2_bwd")
    dmix = _mm(dx1b, W["w_o"], "nt", BF16, "d_mix")
    G["w_o"] = _mm(mix, dx1b, "tn", F32, "dw_o")
    dproj, dog, dosb, dom, G["w_br_gdn"], G["w_br_sb"], G["w_br_mem"] = _merge_bwd(
        dmix, proj, (yg, ys, ym), (og, osb, om), wbr)
    dq, dk, dv, dgb, dproj, G["gdn_norm_g"] = _gdn_bwd(q, k, v, gb, proj, P["gdn_norm_g"], oraw, shist, tinv_t, vn, dog,
                                                      dproj)
    dxc, dab, G["conv_w"], dav = _gdn_pre_bwd(proj, P["conv_w"], ab, avec, dq, dk, dv, dgb)
    dproj = _conv_bwd(dxc, P["conv_w"], dproj)
    G["a_log"], G["dt_bias"] = dav[0:1, :N_HEAD], dav[1:2, :N_HEAD]
    dsq, dsk, dsv = _sb_bwd(sq, sk, sv, ltot, dosb, cut)
    dproj, G["sb_q_norm_g"], G["sb_k_norm_g"] = _sb_pre_bwd(proj, P["sb_q_norm_g"], P["sb_k_norm_g"], dsq, dsk, dsv,
                                                            dproj)
    dproj, dkh, dvm, G["mem_q_norm_g"] = _mem_bwd(proj, kh, vm, P["mem_q_norm_g"], dom, dproj)
    G["w_mem_kv"], G["mem_norm_g"], G["mem_k_norm_g"] = _mem_kv_bwd(
        mem, P["mem_norm_g"], W["w_mem_kv"], P["mem_k_norm_g"], kv, mn, dkh, dvm)
    dw_ab = _mm(dab, h, "tn", F32, "dw_in_ab")
    if dist is None:
        dw_main = _mm(dproj, h, "tn", F32, "dw_in")
    else:
        early = [n for n, _, _ in BIG if n != "w_in"]
        dw_main, landed = _mm(dproj, h, "tn", F32, "dw_in", comm=dist.scatter(G, early, "early"))
        dist.collect(early, landed)
    G["w_in"] = jnp.concatenate([dw_main[QKV_OFF:SB_OFF], dw_main[Z_OFF:MEMQ_OFF], dw_ab[:8], dw_main[SB_OFF:Z_OFF],
                                 dw_main[MEMQ_OFF:], dw_main[:QKV_OFF]], axis=0)
    if dist is None:
        dh = _mm(dproj, w_main, "nt", F32, "d_h")
    else:
        dh, landed = _mm(dproj, w_main, "nt", F32, "d_h", comm=dist.scatter(G, ["w_in"], "late"))
        dist.collect(["w_in"], landed)
    dh = _mm(dab, w_ab, "nt", F32, "d_h_ab", extra=dh, epi=_epi_add)
    dx, _, G["norm1_g"] = _rms_bwd(dh, x, P["norm1_g"], dx1, "rms1_bwd")
    return sq_err, dx, G


def _comm(name, ins, out_shapes, plan):
    n_in, n_out = len(ins), len(out_shapes)
    probe = plan([None] * n_in, [None] * n_out, 0, 0, 0, dry=True)
    n_copy = probe

    def body(*refs):
        in_refs, out_refs = refs[:n_in], refs[n_in:n_in + n_out]
        send_sems, recv_sems = refs[n_in + n_out:]
        x, y, c = lax.axis_index("x"), lax.axis_index("y"), lax.axis_index("c")
        copies = []
        for k, (src, dst, dev) in enumerate(plan(in_refs, out_refs, x, y, c, dry=False)):
            if dev is None:
                cp = pltpu.make_async_copy(src, dst, send_sems.at[k])
            else:
                cp = pltpu.make_async_remote_copy(src_ref=src, dst_ref=dst, send_sem=send_sems.at[k],
                                                  recv_sem=recv_sems.at[k], device_id=dev, device_id_type=MESH)
            cp.start()
            copies.append(cp)
        for cp in copies:
            cp.wait()

    return pl.pallas_call(
        body, out_shape=tuple(out_shapes), in_specs=[HBM] * n_in, out_specs=tuple([HBM] * n_out),
        scratch_shapes=[pltpu.SemaphoreType.DMA((n_copy,)), pltpu.SemaphoreType.DMA((n_copy,))], name=name)(*ins)


def _other_chips(x, y):
    return ((1 - x, y), (x, 1 - y), (1 - x, 1 - y))


def _gather_plan(parts, direct=()):
    n, every = len(parts), list(parts) + list(direct)

    def copies(ins, outs, send, recv, scratch):
        x, y, c = lax.axis_index("x"), lax.axis_index("y"), lax.axis_index("c")
        me = 2 * x + y
        chips = _other_chips(x, y)
        local_sems, staged = scratch[0], scratch[1:]

        def remote(src, dst, k, dev):
            return pltpu.make_async_remote_copy(src_ref=src, dst_ref=dst, send_sem=send.at[k], recv_sem=recv.at[k],
                                                device_id=dev, device_id_type=MESH)

        def half(p, ci):
            hr = ins[p].shape[0] // 2
            return pl.ds(pl.multiple_of(ci * hr, 16), hr)

        sent = [remote(ins[p].at[half(p, c)], outs[p].at[me, half(p, c)], 6 * p + f, (px, py, c))
                for p in range(n) for f, (px, py) in enumerate(chips)]
        sent += [remote(ins[p], outs[p].at[me], 6 * n + 3 * (p - n) + f, (px, py, c))
                 for p in range(n, len(every)) for f, (px, py) in enumerate(chips)]
        landed = [outs[p].at[2 * px + py, half(p, c)] for p in range(n) for px, py in chips]
        passed = [remote(landed[3 * p + f], landed[3 * p + f], 6 * p + 3 + f, (x, y, 1 - c))
                  for p in range(n) for f in range(3)]
        loads = [pltpu.make_async_copy(ins[p], staged[p], local_sems.at[2 * p]) for p in range(len(every))]
        stores = [pltpu.make_async_copy(staged[p], outs[p].at[me], local_sems.at[2 * p + 1]) for p in range(len(every))]
        return sent, passed, loads, stores

    def start(*refs):
        sent, _, loads, _ = copies(*refs)
        for cp in loads + sent:
            cp.start()

    def mid(*refs):
        sent, passed, loads, stores = copies(*refs)
        for ld, st in zip(loads, stores):
            ld.wait()
            st.start()
        for p in range(n):
            for f in range(3):
                sent[3 * p + f].wait_recv()
                passed[3 * p + f].start()

    def finish(*refs):
        sent, passed, _, stores = copies(*refs)
        for cp in sent[:3 * n]:
            cp.wait_send()
        for cp in passed + sent[3 * n:] + stores:
            cp.wait()

    return _Hosted(every, [jax.ShapeDtypeStruct((4,) + p.shape, p.dtype) for p in every], 6 * n + 3 * len(direct),
                   start, finish, mid,
                   [pltpu.SemaphoreType.DMA((2 * len(every),))] + [pltpu.VMEM(p.shape, p.dtype) for p in every])


def _scatter_plan(pairs):
    def copies(ins, outs, send, recv, scratch):
        x, y, c = lax.axis_index("x"), lax.axis_index("y"), lax.axis_index("c")
        me = 2 * x + y
        return [pltpu.make_async_remote_copy(src_ref=src.at[2 * px + py], dst_ref=dst.at[me], send_sem=send.at[3 * p + f],
                                             recv_sem=recv.at[3 * p + f], device_id=(px, py, c), device_id_type=MESH)
                for p, (src, dst) in enumerate(zip(ins, outs)) for f, (px, py) in enumerate(_other_chips(x, y))]

    def start(*refs):
        for cp in copies(*refs):
            cp.start()

    def finish(*refs):
        for cp in copies(*refs):
            cp.wait()

    return _Hosted(pairs, [jax.ShapeDtypeStruct(a.shape, a.dtype) for a in pairs], 3 * len(pairs), start, finish)


def _run_hosted(comm, name):
    n_in, n_out = len(comm.ins), len(comm.out_shapes)

    def body(*refs):
        args = (refs[:n_in], refs[n_in:n_in + n_out], refs[n_in + n_out], refs[n_in + n_out + 1], refs[n_in + n_out + 2:])
        comm.start(*args)
        if comm.mid is not None:
            comm.mid(*args)
        comm.finish(*args)

    sems = [pltpu.SemaphoreType.DMA((comm.n_sems,)), pltpu.SemaphoreType.DMA((comm.n_sems,))]
    return list(pl.pallas_call(
        body, out_shape=tuple(comm.out_shapes), in_specs=[HBM] * n_in, out_specs=tuple([HBM] * n_out),
        scratch_shapes=sems + comm.scratch, name=name, compiler_params=_cp())(*comm.ins))


def _swap_halves(slabs, name):
    n = len(slabs)

    def body(*refs):
        ins, outs = refs[:n], refs[n:2 * n]
        send, recv = refs[2 * n:]
        x, y, c = lax.axis_index("x"), lax.axis_index("y"), lax.axis_index("c")
        other = (x, y, 1 - c)
        for p in range(n):
            for j in range(4):
                pltpu.make_async_remote_copy(src_ref=ins[p].at[j, 1 - c], dst_ref=outs[p].at[j], send_sem=send.at[p],
                                             recv_sem=recv.at[p], device_id=other, device_id_type=MESH).start()
        for p in range(n):
            pltpu.make_async_remote_copy(src_ref=outs[p], dst_ref=outs[p], send_sem=send.at[p], recv_sem=recv.at[p],
                                         device_id=other, device_id_type=MESH).wait()

    shapes = [jax.ShapeDtypeStruct((4,) + s.shape[2:], s.dtype) for s in slabs]
    return pl.pallas_call(
        body, out_shape=tuple(shapes), in_specs=[HBM] * n, out_specs=tuple([HBM] * n),
        scratch_shapes=[pltpu.SemaphoreType.DMA((n,)), pltpu.SemaphoreType.DMA((n,))], name=name)(*slabs)


def _join_halves(both):
    n = len(both)

    def body(*refs):
        bufs = refs[n:2 * n]
        send, recv = refs[2 * n:]
        x, y, c = lax.axis_index("x"), lax.axis_index("y"), lax.axis_index("c")
        copies = []
        for p in range(n):
            cp = pltpu.make_async_remote_copy(src_ref=bufs[p].at[c], dst_ref=bufs[p].at[c], send_sem=send.at[p],
                                              recv_sem=recv.at[p], device_id=(x, y, 1 - c), device_id_type=MESH)
            cp.start()
            copies.append(cp)
        for cp in copies:
            cp.wait()

    return pl.pallas_call(
        body, out_shape=tuple(jax.ShapeDtypeStruct(a.shape, a.dtype) for a in both), in_specs=[HBM] * n,
        out_specs=tuple([HBM] * n), input_output_aliases={p: p for p in range(n)},
        scratch_shapes=[pltpu.SemaphoreType.DMA((n,)), pltpu.SemaphoreType.DMA((n,))], name="grad_join_cores")(*both)


def _gather_all(a, name):
    def plan(ins, outs, x, y, c, dry):
        if dry:
            return 8
        me = 4 * x + 2 * y + c
        copies = [(ins[0], outs[0].at[me], None)]
        for f in range(1, 8):
            peer = (1 - x if f & 4 else x, 1 - y if f & 2 else y, 1 - c if f & 1 else c)
            copies.append((ins[0], outs[0].at[me], peer))
        return copies

    return _comm(name, [a], [jax.ShapeDtypeStruct((8,) + a.shape, a.dtype)], plan)[0]


def _sum_slots(a, name, extra=None):
    n, R, _ = a.shape
    rb = min(ROW_BLK, R)

    def body(*refs):
        a_ref, o_ref = refs[0], refs[-1]
        acc = a_ref[0]
        for s in range(1, n):
            acc = acc + a_ref[s]
        if extra is not None:
            acc = acc + refs[1][...]
        o_ref[...] = acc

    ins = [a] + ([extra] if extra is not None else [])
    in_specs = [pl.BlockSpec((n, rb, LANES), lambda i: (0, i, 0))] + ([_rowspec(rb, LANES)] if extra is not None else [])
    return pl.pallas_call(
        body, out_shape=jax.ShapeDtypeStruct((R, LANES), F32), grid=(R // rb,), in_specs=in_specs,
        out_specs=_rowspec(rb, LANES), name=name, compiler_params=_cp("parallel"))(*ins)


def _pair_sum(slab, theirs, core, name):
    _, _, hr, C = slab.shape

    def body(c_ref, a_ref, b_ref, o_ref):
        o_ref[...] = (a_ref[...] + b_ref[...]).astype(BF16)

    return pl.pallas_call(
        body, out_shape=jax.ShapeDtypeStruct((4, hr, C), BF16),
        grid_spec=pltpu.PrefetchScalarGridSpec(
            num_scalar_prefetch=1, grid=(4,),
            in_specs=[pl.BlockSpec((None, None, hr, C), lambda j, c_ref: (j, c_ref[0], 0, 0)),
                      pl.BlockSpec((None, hr, C), lambda j, c_ref: (j, 0, 0))],
            out_specs=pl.BlockSpec((None, hr, C), lambda j, c_ref: (j, 0, 0))),
        name=name, compiler_params=_cp("parallel"))(core, slab, theirs)


def _chip_sum(recv, pairs, where, name):
    _, hr, C = recv.shape

    def body(w_ref, r_ref, p_ref, o_ref):
        me = w_ref[0]
        o_ref[...] = jnp.zeros_like(o_ref)
        for s in range(4):
            @pl.when(me == s)
            def _():
                o_ref[...] += p_ref[...].astype(F32)

            @pl.when(me != s)
            def _():
                o_ref[...] += r_ref[s].astype(F32)

    return pl.pallas_call(
        body, out_shape=jax.ShapeDtypeStruct((2, hr, C), F32),
        grid_spec=pltpu.PrefetchScalarGridSpec(
            num_scalar_prefetch=1, grid=(1,),
            in_specs=[pl.BlockSpec((4, hr, C), lambda i, w_ref: (0, 0, 0)),
                      pl.BlockSpec((None, hr, C), lambda i, w_ref: (w_ref[0], 0, 0))],
            out_specs=pl.BlockSpec((None, hr, C), lambda i, w_ref: (w_ref[1], 0, 0))),
        name=name, compiler_params=_cp("arbitrary"))(where, recv, pairs)


def _adamw(w, g, m, v, name):
    R, C = w.shape[-2:]
    lead = w.ndim - 2
    rb = min(ADAM_ROWS, R)
    c1 = 1.0 - ADAM_B1 ** ADAM_STEP
    c2 = 1.0 - ADAM_B2 ** ADAM_STEP

    def body(w_ref, g_ref, m_ref, v_ref, go_ref, d_ref, nm_ref, nv_ref):
        gv = g_ref[...]
        nm = ADAM_B1 * m_ref[...] + (1.0 - ADAM_B1) * gv
        nv = ADAM_B2 * v_ref[...] + (1.0 - ADAM_B2) * (gv * gv)
        go_ref[...] = gv
        d_ref[...] = -ADAM_LR * ((nm / c1) / (jnp.sqrt(nv / c2) + ADAM_EPS) + ADAM_WD * w_ref[...])
        nm_ref[...] = nm
        nv_ref[...] = nv

    spec = pl.BlockSpec((None,) * lead + (rb, C), lambda i: (0,) * lead + (i, 0))
    return pl.pallas_call(
        body, out_shape=(jax.ShapeDtypeStruct(w.shape, F32),) * 4, grid=(R // rb,),
        in_specs=[spec, _rowspec(rb, C), spec, spec], out_specs=(spec,) * 4, name=name,
        compiler_params=_cp("parallel"))(w, g, m, v)


class _Dist:
    def __init__(self, shards):
        self.shards = shards
        self.chip = 2 * lax.axis_index("x") + lax.axis_index("y")
        self.where = jnp.stack([self.chip, lax.axis_index("c")]).astype(jnp.int32)
        self.pairs, self.landed = {}, {}

    @staticmethod
    def _unshard(name, blk):
        _, (r, cc), axis = next(b for b in BIG if b[0] == name)
        return blk.reshape(4 * r, cc) if axis == 0 else blk.transpose(1, 0, 2).reshape(r, 4 * cc)

    def gather_first(self):
        got = _run_hosted(_gather_plan([self.shards["w_in"].astype(BF16)]), "gather_w_in")[0]
        return _split_w_in([got[j] for j in range(4)])

    def gather_rest(self):
        rest = [self.shards[n].astype(BF16) for n, _, _ in BIG if n != "w_in"]
        return _gather_plan(rest, [self.shards["conv_w"]])

    def weights_from(self, gathered):
        names = [n for n, _, _ in BIG if n != "w_in"]
        conv = gathered[-1]
        taps, width = conv.shape[1:]
        return ({n: self._unshard(n, g) for n, g in zip(names, gathered)},
                conv.transpose(1, 0, 2).reshape(taps, 4 * width))

    def scatter(self, G, names, tag):
        slabs = []
        for name, (r, cc), axis in BIG:
            if name not in names:
                continue
            g = G[name]
            if axis == 0:
                slabs.append(g.reshape(4, 2, r // 2, cc))
            else:
                slabs.append(g.reshape(4, 2, cc // 2, r))
        theirs = _swap_halves(slabs, "grad_swap_cores_" + tag)
        pairs = [_pair_sum(s, t, self.where[1:], "pair_sum_" + n) for s, t, n in zip(slabs, theirs, names)]
        self.pairs.update(zip(names, pairs))
        return _scatter_plan(pairs)

    def collect(self, names, landed):
        self.landed.update(zip(names, landed))

    def finish(self):
        names = [n for n, _, _ in BIG]
        halves = [_chip_sum(self.landed[n], self.pairs[n], self.where, "chip_sum_" + n) for n in names]
        out = {}
        for (name, (r, cc), axis), both in zip(BIG, _join_halves(halves)):
            full = both.reshape(-1, both.shape[-1])
            out[name] = full if axis == 0 else full.T
        return out


def _pack_rows(parts, rows, dtype):
    flat = jnp.concatenate([p.reshape(-1).astype(dtype) for p in parts])
    return jnp.pad(flat, (0, rows * LANES - flat.shape[0])).reshape(rows, LANES)


def _small_rows(n):
    return max(n // LANES, 1)


def _pack_small(vals):
    rows = []
    for name, n in SMALL:
        r = _small_rows(n)
        rows.append(jnp.pad(vals[name].reshape(-1), (0, r * LANES - n)).reshape(r, LANES))
    flat = jnp.concatenate(rows, axis=0)
    return jnp.pad(flat, ((0, SMALL_ROWS - flat.shape[0]), (0, 0)))


def _unpack_small(pack):
    out, r0 = {}, 0
    for name, n in SMALL:
        r = _small_rows(n)
        out[name] = pack[r0:r0 + r].reshape(-1)[:n]
        r0 += r
    return out


def kernel(x, mem, norm1_g, w_in, conv_w, a_log, dt_bias, gdn_norm_g, sb_q_norm_g, sb_k_norm_g, mem_norm_g, w_mem_kv, mem_q_norm_g, mem_k_norm_g, w_br_gdn, w_br_sb, w_br_mem, w_o, norm2_g, w_up, w_down, loss_target, m_norm1_g, m_w_in, m_conv_w, m_a_log, m_dt_bias, m_gdn_norm_g, m_sb_q_norm_g, m_sb_k_norm_g, m_mem_norm_g, m_w_mem_kv, m_mem_q_norm_g, m_mem_k_norm_g, m_w_br_gdn, m_w_br_sb, m_w_br_mem, m_w_o, m_norm2_g, m_w_up, m_w_down, v_norm1_g, v_w_in, v_conv_w, v_a_log, v_dt_bias, v_gdn_norm_g, v_sb_q_norm_g, v_sb_k_norm_g, v_mem_norm_g, v_w_mem_kv, v_mem_q_norm_g, v_mem_k_norm_g, v_w_br_gdn, v_w_br_sb, v_w_br_mem, v_w_o, v_norm2_g, v_w_up, v_w_down):
    wd = dict(norm1_g=norm1_g, w_in=w_in, conv_w=conv_w, a_log=a_log, dt_bias=dt_bias, gdn_norm_g=gdn_norm_g,
              sb_q_norm_g=sb_q_norm_g, sb_k_norm_g=sb_k_norm_g, mem_norm_g=mem_norm_g, w_mem_kv=w_mem_kv,
              mem_q_norm_g=mem_q_norm_g, mem_k_norm_g=mem_k_norm_g, w_br_gdn=w_br_gdn, w_br_sb=w_br_sb,
              w_br_mem=w_br_mem, w_o=w_o, norm2_g=norm2_g, w_up=w_up, w_down=w_down)
    md = dict(norm1_g=m_norm1_g, w_in=m_w_in, conv_w=m_conv_w, a_log=m_a_log, dt_bias=m_dt_bias,
              gdn_norm_g=m_gdn_norm_g, sb_q_norm_g=m_sb_q_norm_g, sb_k_norm_g=m_sb_k_norm_g,
              mem_norm_g=m_mem_norm_g, w_mem_kv=m_w_mem_kv, mem_q_norm_g=m_mem_q_norm_g,
              mem_k_norm_g=m_mem_k_norm_g, w_br_gdn=m_w_br_gdn, w_br_sb=m_w_br_sb, w_br_mem=m_w_br_mem, w_o=m_w_o,
              norm2_g=m_norm2_g, w_up=m_w_up, w_down=m_w_down)
    vd = dict(norm1_g=v_norm1_g, w_in=v_w_in, conv_w=v_conv_w, a_log=v_a_log, dt_bias=v_dt_bias,
              gdn_norm_g=v_gdn_norm_g, sb_q_norm_g=v_sb_q_norm_g, sb_k_norm_g=v_sb_k_norm_g,
              mem_norm_g=v_mem_norm_g, w_mem_kv=v_w_mem_kv, mem_q_norm_g=v_mem_q_norm_g,
              mem_k_norm_g=v_mem_k_norm_g, w_br_gdn=v_w_br_gdn, w_br_sb=v_w_br_sb, w_br_mem=v_w_br_mem, w_o=v_w_o,
              norm2_g=v_norm2_g, w_up=v_w_up, w_down=v_w_down)
    stacked = (wd, md, vd)
    wd, md, vd = ({n: a[0] for n, a in d.items()} for d in (wd, md, vd))
    chip = 2 * lax.axis_index("x") + lax.axis_index("y")
    conv_shard = wd["conv_w"].shape

    dist = _Dist(wd)
    W = dict(zip(("w_main", "w_ab"), dist.gather_first()))
    P = {n: wd[n].reshape(1, -1) for n, _ in SMALL}

    sq_err, grad_x, G = _local_step(x[0], mem[0], loss_target[0], W, P, dist)
    loss = lax.psum(0.5 / D_MODEL * jnp.sum(sq_err), ("x", "y", "c"))

    g_big = dist.finish()

    spack = jnp.concatenate([_pack_small(G), G["conv_w"].reshape(CONV_ROWS, LANES)], axis=0)
    g_small = _sum_slots(_gather_all(spack, "gather_small_grads"), "small_grad_sum")
    g_conv_full = g_small[SMALL_ROWS:].reshape(conv_shard[0], 4 * conv_shard[1])
    g_conv = lax.dynamic_slice_in_dim(g_conv_full, chip * conv_shard[1], conv_shard[1], axis=1)

    grads, deltas, new_m, new_v = {}, {}, {}, {}
    for name, _, _ in BIG:
        grads[name], deltas[name], new_m[name], new_v[name] = _adamw(
            stacked[0][name], g_big[name], stacked[1][name], stacked[2][name], "adamw_" + name)
    pack_sm = lambda d: jnp.concatenate([_pack_small(d), _pack_rows([d["conv_w"]], APACK_ROWS - SMALL_ROWS, F32)], axis=0)
    g_sm = jnp.concatenate([g_small[:SMALL_ROWS], _pack_rows([g_conv], APACK_ROWS - SMALL_ROWS, F32)], axis=0)
    small = _adamw(pack_sm(wd), g_sm, pack_sm(md), pack_sm(vd), "adamw_small")
    for out, pack in zip((grads, deltas, new_m, new_v), small):
        for name, vec in _unpack_small(pack[:SMALL_ROWS]).items():
            out[name] = vec[None]
        out["conv_w"] = pack[SMALL_ROWS:].reshape(-1)[:conv_shard[0] * conv_shard[1]].reshape((1,) + conv_shard)

    return (loss, grad_x[None], *[d[n] for d in (grads, deltas, new_m, new_v) for n in WEIGHTS])
```

```python
import jax
import jax.numpy as jnp
from jax import lax
from jax.experimental import pallas as pl
from jax.experimental.pallas import tpu as pltpu

F32 = jnp.float32
BF16 = jnp.bfloat16
MESH = pl.DeviceIdType.MESH

D_MODEL = 1024
N_HEAD = 4
D_HEAD = 128
BR_W = N_HEAD * D_HEAD
CONV_TAPS = 4
GDN_CHUNK = 64
INV_BLOCK = 16
INV_CHUNKS = 4
N_MEM = 256
D_FF = 4 * D_MODEL
EPS = 1e-6
LANES = 128
PROJ_W = 7168
GATE_OFF = 0
QKV_OFF = 3072
SB_OFF = 4608
Z_OFF = 6144
MEMQ_OFF = 6656
IN_GATE, IN_QKV, IN_SB, IN_Z, IN_MEMQ, IN_AB = (4104, 7176), (0, 1536), (2056, 3592), (1536, 2048), (3592, 4104), (2048, 2056)

ADAM_LR, ADAM_B1, ADAM_B2, ADAM_EPS, ADAM_WD, ADAM_STEP = 0.001, 0.9, 0.999, 1e-08, 0.01, 10

TM = 512
MM_TM = 1024
MM_TK = (2048, 1792, 1024, 128)
TK_TOK = 2048
GDN_STEP_CHUNKS = 8
GDN_BWD_STEP_CHUNKS = 1
G1_TM = 256
SB_BLK = 512
SB_DEAD = 120.0
VMEM_LIMIT = 48 << 20

BIG = (("w_in", (1024, 1794), 1), ("w_mem_kv", (256, 1024), 0), ("w_br_gdn", (512, 256), 1),
       ("w_br_sb", (512, 256), 1), ("w_br_mem", (512, 256), 1), ("w_o", (256, 1024), 0),
       ("w_up", (1024, 1024), 1), ("w_down", (1024, 1024), 0))
COL_SHARDED = tuple(n for n, _, a in BIG if a == 1)
GATE_COL = GATE_OFF // D_MODEL
QKV_COL = QKV_OFF // (3 * BR_W)
ROW_BLK = 1024
ADAM_ROWS = 128
SMALL = (("norm1_g", 1024), ("mem_norm_g", 1024), ("norm2_g", 1024), ("gdn_norm_g", 128), ("sb_q_norm_g", 128),
         ("sb_k_norm_g", 128), ("mem_q_norm_g", 128), ("mem_k_norm_g", 128), ("a_log", 4), ("dt_bias", 4))
SMALL_ROWS = 32
CONV_ROWS = 48
SPACK_ROWS = SMALL_ROWS + CONV_ROWS
APACK_ROWS = SMALL_ROWS + 16

WEIGHTS = ("norm1_g", "w_in", "conv_w", "a_log", "dt_bias", "gdn_norm_g", "sb_q_norm_g", "sb_k_norm_g",
           "mem_norm_g", "w_mem_kv", "mem_q_norm_g", "mem_k_norm_g", "w_br_gdn", "w_br_sb", "w_br_mem", "w_o",
           "norm2_g", "w_up", "w_down")


def _cp(*sem):
    return pltpu.CompilerParams(dimension_semantics=sem if sem else None, vmem_limit_bytes=VMEM_LIMIT)


HBM = pl.BlockSpec(memory_space=pl.ANY)

_NN = (((1,), (0,)), ((), ()))
_NT = (((1,), (1,)), ((), ()))
_TN = (((0,), (0,)), ((), ()))


def _dot(a, b, dims=_NN):
    return lax.dot_general(a.astype(BF16), b.astype(BF16), dims, preferred_element_type=F32)


def _dot_nt(a, b):
    return _dot(a, b, _NT)


def _dot_tn(a, b):
    return _dot(a, b, _TN)


def _dotf(a, b, dims=_NN):
    return lax.dot_general(a, b, dims, precision=lax.Precision.HIGHEST, preferred_element_type=F32)


def _sigmoid(v):
    return 0.5 * jnp.tanh(0.5 * v) + 0.5


def _softplus(v):
    return jnp.maximum(v, 0.0) + jnp.log(1.0 + jnp.exp(-jnp.abs(v)))


def _iota(shape, dim):
    return lax.broadcasted_iota(jnp.int32, shape, dim)


def _hs(h):
    return slice(h * D_HEAD, (h + 1) * D_HEAD)


def _rowspec(tm, w, col=0):
    return pl.BlockSpec((tm, w), lambda i: (i, col))


def _full(shape):
    return pl.BlockSpec(shape, lambda *_: (0,) * len(shape))


def _accum(ref, first, val):
    @pl.when(first)
    def _():
        ref[...] = val

    @pl.when(jnp.logical_not(first))
    def _():
        ref[...] += val


class _Hosted:
    def __init__(self, ins, out_shapes, n_sems, start, finish, mid=None, scratch=()):
        self.ins, self.out_shapes, self.n_sems = list(ins), list(out_shapes), n_sems
        self.start, self.mid, self.finish, self.scratch = start, mid, finish, list(scratch)


def _mm(a, b, mode, out_dtype, name, *, tm=None, tn=None, tk=None, a_fn=None, extra=None, epi=None, comm=None):
    if mode == "tn":
        (K, M), N = a.shape, b.shape[1]
    else:
        (M, K), N = a.shape, (b.shape[0] if mode == "nt" else b.shape[1])
    tm = min(tm or (1024 if mode == "tn" else MM_TM), M)
    tn = min(tn or 1024, N)
    tk = min(tk or (TK_TOK if mode == "tn" else next(t for t in MM_TK if K % t == 0)), K)
    nm, nn, nk = M // tm, N // tn, K // tk
    assert nm * tm == M and nn * tn == N and nk * tk == K, (name, a.shape, b.shape)
    if mode == "tn":
        a_spec = pl.BlockSpec((tk, tm), lambda i, j, k: (k, i))
    else:
        a_spec = pl.BlockSpec((tm, tk), lambda i, j, k: (i, k))
    if mode == "nt":
        b_spec = pl.BlockSpec((tn, tk), lambda i, j, k: (j, k))
    else:
        b_spec = pl.BlockSpec((tk, tn), lambda i, j, k: (k, j))
    dims = {"nn": _NN, "nt": _NT, "tn": _TN}[mode]
    o_spec = pl.BlockSpec((tm, tn), lambda i, j, k: (i, j))
    has_extra = extra is not None

    n_ci, n_co = (len(comm.ins), len(comm.out_shapes)) if comm else (0, 0)
    n_in = 2 + has_extra + n_ci
    steps = nm * nn * nk

    def body(*refs):
        a_ref, b_ref = refs[0], refs[1]
        e_ref = refs[2] if has_extra else None
        o_ref = refs[n_in]
        scratch = refs[n_in + 1 + n_co:]
        if comm:
            step = (pl.program_id(0) * nn + pl.program_id(1)) * nk + pl.program_id(2)
            cargs = (refs[2 + has_extra:n_in], refs[n_in + 1:n_in + 1 + n_co], scratch[nk > 1], scratch[(nk > 1) + 1],
                     scratch[(nk > 1) + 2:])
            pl.when(step == 0)(lambda: comm.start(*cargs))
            if comm.mid is not None:
                pl.when(step == (steps * 7) // 8)(lambda: comm.mid(*cargs))
        av = a_ref[...]
        if a_fn is not None:
            av = a_fn(av)
        p = lax.dot_general(av, b_ref[...], dims, preferred_element_type=F32)

        def finish(acc):
            if epi is not None:
                acc = epi(acc, e_ref[...] if has_extra else None)
            o_ref[...] = acc.astype(out_dtype)

        if nk == 1:
            finish(p)
        else:
            acc_ref = scratch[0]
            k = pl.program_id(2)
            _accum(acc_ref, k == 0, p)

            @pl.when(k == nk - 1)
            def _():
                finish(acc_ref[...])

        if comm:
            pl.when(step == steps - 1)(lambda: comm.finish(*cargs))

    ins = [a, b] + ([extra] if has_extra else [])
    in_specs = [a_spec, b_spec] + ([o_spec] if has_extra else [])
    scratch_shapes = [pltpu.VMEM((tm, tn), F32)] if nk > 1 else []
    main = jax.ShapeDtypeStruct((M, N), out_dtype)
    if not comm:
        return pl.pallas_call(
            body, out_shape=main, grid=(nm, nn, nk), in_specs=in_specs, out_specs=o_spec,
            scratch_shapes=scratch_shapes, name=name, compiler_params=_cp("parallel", "parallel", "arbitrary"))(*ins)
    sems = [pltpu.SemaphoreType.DMA((comm.n_sems,)), pltpu.SemaphoreType.DMA((comm.n_sems,))]
    res = pl.pallas_call(
        body, out_shape=(main, *comm.out_shapes), grid=(nm, nn, nk), in_specs=in_specs + [HBM] * n_ci,
        out_specs=(o_spec, *[HBM] * n_co), scratch_shapes=scratch_shapes + sems + comm.scratch, name=name,
        compiler_params=_cp("arbitrary", "arbitrary", "arbitrary"))(*ins, *comm.ins)
    return res[0], list(res[1:])


def _relu2(u):
    r = jnp.maximum(u.astype(F32), 0.0)
    return (r * r).astype(BF16)


def _epi_add(acc, e):
    return acc + e.astype(F32)


def _epi_drelu2(acc, u):
    return acc * (2.0 * jnp.maximum(u.astype(F32), 0.0))


def _rms_fwd(x, g, name):
    T, dm = x.shape
    tm = min(TM, T)

    def body(x_ref, g_ref, h_ref):
        xv = x_ref[...]
        r = lax.rsqrt(jnp.mean(xv * xv, axis=-1, keepdims=True) + EPS)
        h_ref[...] = (xv * r * g_ref[...]).astype(BF16)

    return pl.pallas_call(
        body, out_shape=jax.ShapeDtypeStruct((T, dm), BF16), grid=(T // tm,),
        in_specs=[_rowspec(tm, dm), _full((1, dm))], out_specs=_rowspec(tm, dm), name=name,
        compiler_params=_cp("parallel"))(x, g)


def _rms_bwd(dh, x, g, resid, name):
    T, dm = x.shape
    tm = min(TM, T)

    def body(dh_ref, x_ref, g_ref, res_ref, dx_ref, dxb_ref, dg_ref):
        i = pl.program_id(0)
        xv = x_ref[...]
        r = lax.rsqrt(jnp.mean(xv * xv, axis=-1, keepdims=True) + EPS)
        y = xv * r
        dhv = dh_ref[...].astype(F32)
        dy = dhv * g_ref[...]
        dx = res_ref[...] + r * (dy - y * jnp.mean(dy * y, axis=-1, keepdims=True))
        dx_ref[...] = dx
        dxb_ref[...] = dx.astype(BF16)
        _accum(dg_ref, i == 0, jnp.sum(dhv * y, axis=0, keepdims=True))

    return pl.pallas_call(
        body,
        out_shape=(jax.ShapeDtypeStruct((T, dm), F32), jax.ShapeDtypeStruct((T, dm), BF16),
                   jax.ShapeDtypeStruct((1, dm), F32)),
        grid=(T // tm,),
        in_specs=[_rowspec(tm, dm), _rowspec(tm, dm), _full((1, dm)), _rowspec(tm, dm)],
        out_specs=(_rowspec(tm, dm), _rowspec(tm, dm), _full((1, dm))), name=name,
        compiler_params=_cp("arbitrary"))(dh, x, g, resid)


def _conv_tile(x_ref, halo_ref, w_ref, xpad, tm):
    i = pl.program_id(0)
    halo = halo_ref[...].astype(F32)[8:16]
    xpad[0:8, :] = jnp.where(i > 0, halo, 0.0)
    xpad[8:, :] = x_ref[...].astype(F32)
    w = w_ref[...]
    xc = w[0:1] * xpad[5:5 + tm, :]
    for j in range(1, CONV_TAPS):
        xc = xc + w[j:j + 1] * xpad[5 + j:5 + j + tm, :]
    return xc


def _gate_terms(ab_ref, av_ref):
    abv = ab_ref[...]
    av = av_ref[...]
    pre = abv + av[1:2]
    ea = jnp.exp(av[0:1])
    g = -ea * _softplus(pre)
    return abv, pre, ea, g


def _gdn_pre(proj, conv_w, ab, avec):
    T = proj.shape[0]
    tm = min(G1_TM, T)
    cw = 3 * BR_W

    def body(x_ref, halo_ref, w_ref, ab_ref, av_ref, q_ref, k_ref, v_ref, gb_ref, xpad):
        xc = _conv_tile(x_ref, halo_ref, w_ref, xpad, tm)
        y = xc * _sigmoid(xc)
        for h in range(N_HEAD):
            for off, ref, scale in ((0, q_ref, D_HEAD ** -0.5), (BR_W, k_ref, 1.0)):
                yh = y[:, off + h * D_HEAD:off + (h + 1) * D_HEAD]
                r = lax.rsqrt(jnp.sum(yh * yh, axis=-1, keepdims=True) + EPS)
                ref[:, _hs(h)] = yh * (r * scale)
        v_ref[...] = y[:, 2 * BR_W:]
        abv, _, _, g = _gate_terms(ab_ref, av_ref)
        lane = _iota((tm, LANES), 1)
        gb_ref[...] = jnp.where(lane < N_HEAD, g, jnp.where(lane < 2 * N_HEAD, _sigmoid(abv), 0.0))

    hb = tm // 16
    return pl.pallas_call(
        body,
        out_shape=(jax.ShapeDtypeStruct((T, BR_W), F32),) * 3 + (jax.ShapeDtypeStruct((T, LANES), F32),),
        grid=(T // tm,),
        in_specs=[_rowspec(tm, cw, QKV_COL), pl.BlockSpec((16, cw), lambda i: (jnp.maximum(i * hb - 1, 0), QKV_COL)),
                  _full((CONV_TAPS, cw)), _rowspec(tm, LANES), _full((2, LANES))],
        out_specs=(_rowspec(tm, BR_W),) * 3 + (_rowspec(tm, LANES),),
        scratch_shapes=[pltpu.VMEM((tm + 8, cw), F32)], name="gdn_pre",
        compiler_params=_cp("parallel"))(proj, proj, conv_w, ab, avec)


def _gdn_pre_bwd(proj, conv_w, ab, avec, dq, dk, dv, dgb):
    T = proj.shape[0]
    tm = min(G1_TM, T)
    cw = 3 * BR_W

    def body(x_ref, halo_ref, w_ref, ab_ref, av_ref, dq_ref, dk_ref, dv_ref, dgb_ref,
             dxc_ref, dab_ref, dcw_ref, dav_ref, xpad):
        i = pl.program_id(0)

        @pl.when(i == 0)
        def _():
            dcw_ref[...] = jnp.zeros_like(dcw_ref)
            dav_ref[...] = jnp.zeros_like(dav_ref)

        xc_all = _conv_tile(x_ref, halo_ref, w_ref, xpad, tm)
        for s in range(cw // D_HEAD):
            cs = slice(s * D_HEAD, (s + 1) * D_HEAD)
            xc = xc_all[:, cs]
            sg = _sigmoid(xc)
            yh = xc * sg
            h = s % N_HEAD
            if s < 2 * N_HEAD:
                dref, scale = (dq_ref, D_HEAD ** -0.5) if s < N_HEAD else (dk_ref, 1.0)
                r = lax.rsqrt(jnp.sum(yh * yh, axis=-1, keepdims=True) + EPS)
                yn = yh * r
                dn = dref[:, _hs(h)]
                dy = (scale * r) * (dn - yn * jnp.sum(yn * dn, axis=-1, keepdims=True))
            else:
                dy = dv_ref[:, _hs(h)]
            dxc = dy * (sg * (1.0 + xc * (1.0 - sg)))
            dxc_ref[:, cs] = dxc.astype(BF16)
            for j in range(CONV_TAPS):
                dcw_ref[j:j + 1, cs] += jnp.sum(dxc * xpad[5 + j:5 + j + tm, cs], axis=0, keepdims=True)

        abv, pre, ea, g = _gate_terms(ab_ref, av_ref)
        dgbv = dgb_ref[...]
        lane = _iota((tm, LANES), 1)
        is_a = lane < N_HEAD
        da = jnp.where(is_a, dgbv * (-ea) * _sigmoid(pre), 0.0)
        bs = _sigmoid(abv)
        db = jnp.where(jnp.logical_and(lane >= N_HEAD, lane < 2 * N_HEAD), dgbv * bs * (1.0 - bs), 0.0)
        dab_ref[...] = (da + db).astype(BF16)
        dav_ref[0:1, :] += jnp.sum(jnp.where(is_a, dgbv * g, 0.0), axis=0, keepdims=True)
        dav_ref[1:2, :] += jnp.sum(da, axis=0, keepdims=True)

    hb = tm // 16
    return pl.pallas_call(
        body,
        out_shape=(jax.ShapeDtypeStruct((T, cw), BF16), jax.ShapeDtypeStruct((T, LANES), BF16),
                   jax.ShapeDtypeStruct((CONV_TAPS, cw), F32), jax.ShapeDtypeStruct((2, LANES), F32)),
        grid=(T // tm,),
        in_specs=[_rowspec(tm, cw, QKV_COL), pl.BlockSpec((16, cw), lambda i: (jnp.maximum(i * hb - 1, 0), QKV_COL)),
                  _full((CONV_TAPS, cw)), _rowspec(tm, LANES), _full((2, LANES)),
                  _rowspec(tm, BR_W), _rowspec(tm, BR_W), _rowspec(tm, BR_W), _rowspec(tm, LANES)],
        out_specs=(_rowspec(tm, cw), _rowspec(tm, LANES), _full((CONV_TAPS, cw)), _full((2, LANES))),
        scratch_shapes=[pltpu.VMEM((tm + 8, cw), F32)], name="gdn_pre_bwd",
        compiler_params=_cp("arbitrary"))(proj, proj, conv_w, ab, avec, dq, dk, dv, dgb)


def _conv_bwd(dxc, conv_w, into):
    T, cw = dxc.shape
    tm = min(G1_TM, T)
    nt = T // tm
    hb = tm // 16

    def body(d_ref, halo_ref, w_ref, into_ref, dx_ref, xpad):
        i = pl.program_id(0)
        xpad[0:tm, :] = d_ref[...].astype(F32)
        xpad[tm:, :] = jnp.where(i < nt - 1, halo_ref[...].astype(F32)[0:8], 0.0)
        w = w_ref[...]
        dx = w[3:4] * xpad[0:tm, :]
        for j in range(CONV_TAPS - 1):
            dx = dx + w[j:j + 1] * xpad[3 - j:3 - j + tm, :]
        dx_ref[...] = dx.astype(BF16)

    return pl.pallas_call(
        body, out_shape=jax.ShapeDtypeStruct(into.shape, BF16), grid=(nt,),
        in_specs=[_rowspec(tm, cw), pl.BlockSpec((16, cw), lambda i: (jnp.minimum((i + 1) * hb, T // 16 - 1), 0)),
                  _full((CONV_TAPS, cw)), HBM],
        out_specs=_rowspec(tm, cw, QKV_COL), scratch_shapes=[pltpu.VMEM((tm + 8, cw), F32)],
        input_output_aliases={3: 0}, name="conv_bwd", compiler_params=_cp("parallel"))(dxc, dxc, conv_w, into)


def _chunk_consts():
    C = GDN_CHUNK
    row, col = _iota((C, C), 0), _iota((C, C), 1)
    return row, col, row >= col, row > col


def _chunk_decay(gbv, incl):
    c_all = _dotf(incl.astype(F32), gbv)
    c_t = jnp.concatenate([c_all, jnp.zeros_like(c_all)], axis=0).T[:, :GDN_CHUNK]
    return c_all, c_t


def _head_decay(c_all, c_t, gbv, incl, h):
    C = GDN_CHUNK
    c_col = c_all[:, h:h + 1]
    c_row = c_t[h:h + 1, :]
    gam = jnp.exp(jnp.where(incl, c_col - c_row, -1e30))
    c_last = c_all[C - 1:C, h:h + 1]
    return gam, jnp.exp(c_col), jnp.exp(c_last - c_col), jnp.exp(c_last), gbv[:, N_HEAD + h:N_HEAD + h + 1]


def _split_bf16(x):
    hi = x.astype(BF16)
    return hi, (x - hi.astype(F32)).astype(BF16)


def _dot3(a, b):
    ah, al = _split_bf16(a)
    bh, bl = _split_bf16(b)
    d = lambda u, v: lax.dot_general(u, v, _NN, preferred_element_type=F32)
    return d(ah, bh) + (d(ah, bl) + d(al, bh))


def _unit_lower_inverses(ms, row, col):
    bi, bj = row // INV_BLOCK, col // INV_BLOCK
    eye = (row == col).astype(F32)
    ns = [jnp.where(bi == bj, -m, 0.0) for m in ms]
    invs = [eye + n for n in ns]
    size = 2
    while size < INV_BLOCK:
        ns = [_dot3(n, n) for n in ns]
        invs = [inv + _dot3(inv, n) for inv, n in zip(invs, ns)]
        size *= 2
    width = 2
    while width * INV_BLOCK <= GDN_CHUNK:
        sel = jnp.logical_and(bi // width == bj // width, bi // (width // 2) > bj // (width // 2))
        ts = [_dot3(inv, jnp.where(sel, m, 0.0)) for inv, m in zip(invs, ms)]
        invs = [inv - _dot3(t, inv) for inv, t in zip(invs, ts)]
        width *= 2
    return invs


def _gdn_inv(k, gb):
    T = k.shape[0]
    C = GDN_CHUNK
    per = min(INV_CHUNKS, T // C)
    rows = per * C

    def body(k_ref, gb_ref, ti_ref, tt_ref):
        row, col, incl, strict = _chunk_consts()
        ms = []
        for ci in range(per):
            rs = slice(ci * C, (ci + 1) * C)
            gbv = gb_ref[rs, :]
            c_all, c_t = _chunk_decay(gbv, incl)
            for h in range(N_HEAD):
                gam, _, _, _, bcol = _head_decay(c_all, c_t, gbv, incl, h)
                K = k_ref[rs, _hs(h)]
                ms.append(jnp.where(strict, _dot_nt(K * bcol, K) * gam, 0.0))
        eye = (row == col).astype(BF16)
        for i, inv in enumerate(_unit_lower_inverses(ms, row, col)):
            ti_ref[i // N_HEAD, i % N_HEAD] = inv
            tt_ref[i // N_HEAD, i % N_HEAD] = _dot_tn(inv, eye).astype(BF16)

    spec = pl.BlockSpec((per, N_HEAD, C, C), lambda i: (i, 0, 0, 0))
    return pl.pallas_call(
        body, out_shape=(jax.ShapeDtypeStruct((T // C, N_HEAD, C, C), F32),
                         jax.ShapeDtypeStruct((T // C, N_HEAD, C, C), BF16)),
        grid=(T // rows,), in_specs=[_rowspec(rows, BR_W), _rowspec(rows, LANES)], out_specs=(spec, spec),
        name="gdn_inv", compiler_params=_cp("parallel"))(k, gb)


def _gdn_fwd(q, k, v, gb, proj, gnorm, tinv_all):
    T = q.shape[0]
    C = GDN_CHUNK
    nc = T // C
    per = min(GDN_STEP_CHUNKS, nc)
    zcol = Z_OFF // BR_W
    heads = range(N_HEAD)

    def body(q_ref, k_ref, v_ref, gb_ref, z_ref, gn_ref, ti_ref, og_ref, oraw_ref, sh_ref, vn_ref, s_ref):
        @pl.when(pl.program_id(0) == 0)
        def _():
            s_ref[...] = jnp.zeros_like(s_ref)

        _, _, incl, _ = _chunk_consts()
        S = [s_ref[h] for h in heads]
        for ci in range(per):
            rs = slice(ci * C, (ci + 1) * C)
            gbv = gb_ref[rs, :]
            c_all, c_t = _chunk_decay(gbv, incl)
            dec = [_head_decay(c_all, c_t, gbv, incl, h) for h in heads]
            gam, gcol, dcol, glast, bcol = ([d[i] for d in dec] for i in range(5))
            Q = [q_ref[rs, _hs(h)] for h in heads]
            K = [k_ref[rs, _hs(h)] for h in heads]
            V = [v_ref[rs, _hs(h)] for h in heads]
            Sb = [s.astype(BF16) for s in S]
            KS = [_dot(K[h], Sb[h]) for h in heads]
            QS = [_dot(Q[h], Sb[h]) for h in heads]
            P = [_dot_nt(Q[h], K[h]) * gam[h] for h in heads]
            R = [bcol[h] * (V[h] - gcol[h] * KS[h]) for h in heads]
            vn = [_dot(ti_ref[ci, h], R[h]) for h in heads]
            O = [gcol[h] * QS[h] + _dot(P[h], vn[h]) for h in heads]
            Sn = [glast[h] * S[h] + _dot_tn(K[h] * dcol[h], vn[h]) for h in heads]
            for h in heads:
                sh_ref[ci, h] = S[h]
                vn_ref[rs, _hs(h)] = vn[h]
                oraw_ref[rs, _hs(h)] = O[h]
                rr = lax.rsqrt(jnp.mean(O[h] * O[h], axis=-1, keepdims=True) + EPS)
                zz = z_ref[rs, _hs(h)].astype(F32)
                og_ref[rs, _hs(h)] = (O[h] * rr * gn_ref[...] * (zz * _sigmoid(zz))).astype(BF16)
            S = Sn
        for h in heads:
            s_ref[h] = S[h]

    cspec = lambda w, cb=0: pl.BlockSpec((per * C, w), lambda n: (n, cb))
    hist = lambda a, b: pl.BlockSpec((per, N_HEAD, a, b), lambda n: (n, 0, 0, 0))
    return pl.pallas_call(
        body,
        out_shape=(jax.ShapeDtypeStruct((T, BR_W), BF16), jax.ShapeDtypeStruct((T, BR_W), F32),
                   jax.ShapeDtypeStruct((nc, N_HEAD, D_HEAD, D_HEAD), F32), jax.ShapeDtypeStruct((T, BR_W), F32)),
        grid=(nc // per,),
        in_specs=[cspec(BR_W), cspec(BR_W), cspec(BR_W), cspec(LANES), cspec(BR_W, zcol), _full((1, D_HEAD)),
                  hist(C, C)],
        out_specs=(cspec(BR_W), cspec(BR_W), hist(D_HEAD, D_HEAD), cspec(BR_W)),
        scratch_shapes=[pltpu.VMEM((N_HEAD, D_HEAD, D_HEAD), F32)], name="gdn_chunk_fwd",
        compiler_params=_cp("arbitrary"))(q, k, v, gb, proj, gnorm, tinv_all)


def _gdn_bwd(q, k, v, gb, proj, gnorm, oraw, shist, tinv_all, vn_all, dog, into):
    T = q.shape[0]
    C = GDN_CHUNK
    nc = T // C
    per = min(GDN_BWD_STEP_CHUNKS, nc)
    zcol = Z_OFF // BR_W

    def body(q_ref, k_ref, v_ref, gb_ref, z_ref, gn_ref, oraw_ref, sh_ref, tt_ref, vn_ref, dog_ref, into_ref,
             dq_ref, dk_ref, dv_ref, dgb_ref, dz_ref, dgn_ref, ds_ref):
        @pl.when(pl.program_id(0) == 0)
        def _():
            ds_ref[...] = jnp.zeros_like(ds_ref)
            dgn_ref[...] = jnp.zeros_like(dgn_ref)

        row, col, incl, strict = _chunk_consts()
        lane = _iota((C, LANES), 1)
        rowl = _iota((C, LANES), 0)
        eye = (row == col).astype(F32)
        upper = (col >= row).astype(F32)
        gn = gn_ref[...]
        heads = range(N_HEAD)
        rsum = lambda a: jnp.sum(a, axis=-1, keepdims=True)
        dgn = jnp.zeros((1, D_HEAD), F32)
        dSn = [ds_ref[h] for h in heads]
        for ci in reversed(range(per)):
            rs = slice(ci * C, (ci + 1) * C)
            gbv = gb_ref[rs, :]
            c_all, c_t = _chunk_decay(gbv, incl)
            dec = [_head_decay(c_all, c_t, gbv, incl, h) for h in heads]
            gam, gcol, dcol, glast, bcol = ([d[i] for d in dec] for i in range(5))
            Q = [q_ref[rs, _hs(h)] for h in heads]
            K = [k_ref[rs, _hs(h)] for h in heads]
            V = [v_ref[rs, _hs(h)] for h in heads]
            dO = []
            for h in heads:
                O = oraw_ref[rs, _hs(h)]
                zz = z_ref[rs, _hs(h)].astype(F32)
                dogv = dog_ref[rs, _hs(h)].astype(F32)
                rr = lax.rsqrt(jnp.mean(O * O, axis=-1, keepdims=True) + EPS)
                on = O * rr
                sg = _sigmoid(zz)
                dz_ref[rs, _hs(h)] = (dogv * on * gn * (sg * (1.0 + zz * (1.0 - sg)))).astype(BF16)
                dyn = dogv * (zz * sg)
                dgn = dgn + jnp.sum(dyn * on, axis=0, keepdims=True)
                dyv = dyn * gn
                dO.append((rr * (dyv - on * jnp.mean(dyv * on, axis=-1, keepdims=True))).astype(BF16))
            S = [sh_ref[ci, h] for h in heads]
            Sb = [s.astype(BF16) for s in S]
            vn = [vn_ref[rs, _hs(h)] for h in heads]
            vnb = [a.astype(BF16) for a in vn]
            dSb = [a.astype(BF16) for a in dSn]
            Kb = [K[h] * bcol[h] for h in heads]
            gam_t = [jnp.exp(jnp.where(col >= row, c_t[h:h + 1, :] - c_all[:, h:h + 1], -1e30)) for h in heads]
            M = [jnp.where(strict, _dot_nt(Kb[h], K[h]) * gam[h], 0.0) for h in heads]
            P = [_dot_nt(Q[h], K[h]) * gam[h] for h in heads]
            P_t = [_dot_nt(K[h], Q[h]) * gam_t[h] for h in heads]
            KS = [_dot(K[h], Sb[h]) for h in heads]
            QS = [_dot(Q[h], Sb[h]) for h in heads]
            dvn = [_dot(P_t[h], dO[h]) + _dot(K[h] * dcol[h], dSb[h]) for h in heads]
            dR = [_dot(tt_ref[ci, h], dvn[h]) for h in heads]
            dRb = [a.astype(BF16) for a in dR]
            bg = [bcol[h] * gcol[h] for h in heads]
            dS_new = [glast[h] * dSn[h] + _dot_tn(gcol[h] * Q[h], dO[h]) - _dot_tn(bg[h] * K[h], dRb[h])
                      for h in heads]
            dP = [jnp.where(incl, _dot_nt(dO[h], vnb[h]), 0.0) for h in heads]
            dM = [jnp.where(strict, -_dot_nt(dRb[h], vnb[h]), 0.0) for h in heads]
            dPG = [(dP[h] * gam[h]).astype(BF16) for h in heads]
            dMG = [(dM[h] * gam[h]).astype(BF16) for h in heads]
            dPG_t = [(jnp.where(col >= row, _dot_nt(vnb[h], dO[h]), 0.0) * gam_t[h]).astype(BF16) for h in heads]
            dMG_t = [(jnp.where(col > row, -_dot_nt(vnb[h], dRb[h]), 0.0) * gam_t[h]).astype(BF16) for h in heads]
            E = [_dot_nt(vnb[h], dSb[h]) for h in heads]
            dKb = [_dot(dMG[h], K[h]) for h in heads]
            dc_all = jnp.zeros((C, LANES), F32)
            db_all = jnp.zeros((C, LANES), F32)
            for h in heads:
                dq_ref[rs, _hs(h)] = gcol[h] * _dot_nt(dO[h], Sb[h]) + _dot(dPG[h], K[h])
                dk_ref[rs, _hs(h)] = (_dot(dPG_t[h], Q[h]) + _dot(dMG_t[h], Kb[h]) + bcol[h] * dKb[h]
                                      - bg[h] * _dot_nt(dRb[h], Sb[h]) + dcol[h] * E[h])
                dv_ref[rs, _hs(h)] = bcol[h] * dR[h]
                dbeta = rsum(dKb[h] * K[h]) + rsum(dR[h] * (V[h] - gcol[h] * KS[h]))
                X = dP[h] * P[h] + dM[h] * M[h]
                ddel = rsum(K[h] * E[h]) * dcol[h]
                colsum = rsum(eye * jnp.sum(X, axis=0, keepdims=True))
                dc = (rsum(X) - colsum + gcol[h] * rsum(dO[h].astype(F32) * QS[h]) - bg[h] * rsum(dR[h] * KS[h])
                      - ddel)
                last = (jnp.sum(ddel, axis=0, keepdims=True)
                        + glast[h] * jnp.sum(rsum(dSn[h] * S[h]), axis=0, keepdims=True))
                dc_all = dc_all + jnp.where(lane == h, dc + jnp.where(rowl == C - 1, last, 0.0), 0.0)
                db_all = db_all + jnp.where(lane == N_HEAD + h, dbeta, 0.0)
            dgb_ref[rs, :] = _dotf(upper, dc_all) + db_all
            dSn = dS_new
        for h in heads:
            ds_ref[h] = dSn[h]
        dgn_ref[...] += dgn

    nb = nc // per
    cspec = lambda w, cb=0: pl.BlockSpec((per * C, w), lambda n: (nb - 1 - n, cb))
    hist = lambda a, b: pl.BlockSpec((per, N_HEAD, a, b), lambda n: (nb - 1 - n, 0, 0, 0))
    return pl.pallas_call(
        body,
        out_shape=(jax.ShapeDtypeStruct((T, BR_W), F32),) * 3 + (
            jax.ShapeDtypeStruct((T, LANES), F32), jax.ShapeDtypeStruct(into.shape, BF16),
            jax.ShapeDtypeStruct((1, D_HEAD), F32)),
        grid=(nb,),
        in_specs=[cspec(BR_W), cspec(BR_W), cspec(BR_W), cspec(LANES), cspec(BR_W, zcol), _full((1, D_HEAD)),
                  cspec(BR_W), hist(D_HEAD, D_HEAD), hist(C, C), cspec(BR_W), cspec(BR_W), HBM],
        out_specs=(cspec(BR_W), cspec(BR_W), cspec(BR_W), cspec(LANES), cspec(BR_W, zcol), _full((1, D_HEAD))),
        scratch_shapes=[pltpu.VMEM((N_HEAD, D_HEAD, D_HEAD), F32)], input_output_aliases={11: 4},
        name="gdn_chunk_bwd",
        compiler_params=_cp("arbitrary"))(q, k, v, gb, proj, gnorm, oraw, shist, tinv_all, vn_all, dog, into)


SB_COL = SB_OFF // BR_W
SB_SCALE = D_HEAD ** -0.5


def _sb_pre(proj, gq, gk):
    T = proj.shape[0]
    tm = min(TM, T)

    def body(xq_ref, xk_ref, xv_ref, gq_ref, gk_ref, q_ref, k_ref, v_ref):
        for h in range(N_HEAD):
            for x_ref, g_ref, ref, scale in ((xq_ref, gq_ref, q_ref, SB_SCALE), (xk_ref, gk_ref, k_ref, 1.0)):
                xh = x_ref[:, _hs(h)].astype(F32)
                r = lax.rsqrt(jnp.mean(xh * xh, axis=-1, keepdims=True) + EPS)
                ref[:, _hs(h)] = (xh * (r * scale) * g_ref[...]).astype(BF16)
        v_ref[...] = xv_ref[...]

    return pl.pallas_call(
        body, out_shape=(jax.ShapeDtypeStruct((T, BR_W), BF16),) * 3, grid=(T // tm,),
        in_specs=[_rowspec(tm, BR_W, SB_COL), _rowspec(tm, BR_W, SB_COL + 1), _rowspec(tm, BR_W, SB_COL + 2),
                  _full((1, D_HEAD)), _full((1, D_HEAD))],
        out_specs=(_rowspec(tm, BR_W),) * 3, name="sb_pre", compiler_params=_cp("parallel"))(proj, proj, proj, gq, gk)


def _sb_pre_bwd(proj, gq, gk, dq, dk, dv, into):
    T = proj.shape[0]
    tm = min(TM, T)

    def body(xq_ref, xk_ref, gq_ref, gk_ref, dq_ref, dk_ref, dv_ref, into_ref, dx_ref, dgq_ref, dgk_ref):
        i = pl.program_id(0)

        @pl.when(i == 0)
        def _():
            dgq_ref[...] = jnp.zeros_like(dgq_ref)
            dgk_ref[...] = jnp.zeros_like(dgk_ref)

        for off, x_ref, g_ref, d_ref, dg_ref, scale in ((0, xq_ref, gq_ref, dq_ref, dgq_ref, SB_SCALE),
                                                        (BR_W, xk_ref, gk_ref, dk_ref, dgk_ref, 1.0)):
            dg = jnp.zeros((1, D_HEAD), F32)
            for h in range(N_HEAD):
                xh = x_ref[:, _hs(h)].astype(F32)
                r = lax.rsqrt(jnp.mean(xh * xh, axis=-1, keepdims=True) + EPS)
                y = xh * r
                dn = d_ref[:, _hs(h)] * scale
                dg = dg + jnp.sum(dn * y, axis=0, keepdims=True)
                dy = dn * g_ref[...]
                dx_ref[:, off + h * D_HEAD:off + (h + 1) * D_HEAD] = (
                    r * (dy - y * jnp.mean(dy * y, axis=-1, keepdims=True))).astype(BF16)
            dg_ref[...] += dg
        dx_ref[:, 2 * BR_W:] = dv_ref[...].astype(BF16)

    return pl.pallas_call(
        body,
        out_shape=(jax.ShapeDtypeStruct(into.shape, BF16), jax.ShapeDtypeStruct((1, D_HEAD), F32),
                   jax.ShapeDtypeStruct((1, D_HEAD), F32)),
        grid=(T // tm,),
        in_specs=[_rowspec(tm, BR_W, SB_COL), _rowspec(tm, BR_W, SB_COL + 1), _full((1, D_HEAD)), _full((1, D_HEAD)),
                  _rowspec(tm, BR_W), _rowspec(tm, BR_W), _rowspec(tm, BR_W), HBM],
        out_specs=(_rowspec(tm, 3 * BR_W, SB_OFF // (3 * BR_W)), _full((1, D_HEAD)), _full((1, D_HEAD))),
        input_output_aliases={7: 0}, name="sb_pre_bwd",
        compiler_params=_cp("arbitrary"))(proj, proj, gq, gk, dq, dk, dv, into)


def _sb_pair(q, k, masked):
    z = _dot_nt(q, k)
    zc = jnp.minimum(z, 30.0)
    sp = jnp.log(1.0 + jnp.exp(zc)) + (z - zc)
    if not masked:
        return z, sp, None
    mask = _iota(z.shape, 1) < _iota(z.shape, 0)
    return z, jnp.where(mask, sp, 0.0), mask


def _sb_fwd(sq, sk, sv):
    T = sq.shape[0]
    blk = min(SB_BLK, T)
    w = blk // 2
    nb = T // blk

    def body(q_ref, k_ref, v_ref, o_ref, lt_ref, cut_ref, acc_ref, r_ref):
        head, qi = pl.program_id(0), pl.program_id(1)
        acc_ref[...] = jnp.zeros_like(acc_ref)
        r_ref[...] = jnp.zeros_like(r_ref)
        after = (_iota((w, w), 0) > _iota((w, w), 1)).astype(BF16)

        def block(rows, kb, masked):
            keys = pl.ds(pl.multiple_of(kb * w, w), w)
            z, sp, mask = _sb_pair(q_ref[rows, :], k_ref[keys, :], masked)
            r = r_ref[rows, :]
            a = jnp.exp(z - sp - _dot(sp, after) - r)
            if masked:
                a = jnp.where(mask, a, 0.0)
            acc_ref[rows, :] += _dot(a, v_ref[keys, :])
            r_ref[rows, :] = r + jnp.sum(sp, axis=-1, keepdims=True)

        def alive():
            return jnp.min(r_ref[...]) < SB_DEAD

        def further(state):
            kb, _ = state
            block(slice(0, blk), kb, False)
            return kb - 1, alive()

        block(slice(w, blk), 2 * qi + 1, True)
        block(slice(0, blk), 2 * qi, True)
        left, _ = lax.while_loop(lambda s: jnp.logical_and(s[0] >= 0, s[1]), further, (2 * qi - 1, alive()))
        o_ref[...] = acc_ref[...].astype(BF16)
        lt_ref[0] = r_ref[...]
        cut_ref[head, qi] = (left + 1).astype(F32)

    qspec = pl.BlockSpec((blk, D_HEAD), lambda h, i: (i, h))
    whole = pl.BlockSpec((T, D_HEAD), lambda h, i: (0, h))
    return pl.pallas_call(
        body,
        out_shape=(jax.ShapeDtypeStruct((T, BR_W), BF16), jax.ShapeDtypeStruct((N_HEAD, T, 1), F32),
                   jax.ShapeDtypeStruct((N_HEAD, nb), F32)),
        grid=(N_HEAD, nb), in_specs=[qspec, whole, whole],
        out_specs=(qspec, pl.BlockSpec((1, blk, 1), lambda h, i: (h, i, 0)), pl.BlockSpec(memory_space=pltpu.SMEM)),
        scratch_shapes=[pltpu.VMEM((blk, D_HEAD), F32), pltpu.VMEM((blk, 1), F32)],
        name="sb_fwd", compiler_params=_cp("arbitrary", "arbitrary"))(sq, sk, sv)


def _sb_bwd(sq, sk, sv, ltot, do, cut):
    T = sq.shape[0]
    blk = min(SB_BLK, T)
    w = blk // 2
    nb = T // blk

    def body(q_ref, k_ref, v_ref, lt_ref, do_ref, cut_ref, dq_ref, dk_ref, dv_ref, acc_ref, p_ref, g_ref):
        head, qi = pl.program_id(0), pl.program_id(1)
        first = cut_ref[head, qi].astype(jnp.int32)

        @pl.when(qi == 0)
        def _():
            dk_ref[...] = jnp.zeros_like(dk_ref)
            dv_ref[...] = jnp.zeros_like(dv_ref)

        acc_ref[...] = jnp.zeros_like(acc_ref)
        p_ref[...] = lt_ref[0]
        g_ref[...] = jnp.zeros_like(g_ref)
        after = (_iota((w, w), 0) > _iota((w, w), 1)).astype(BF16)
        before = (_iota((w, w), 0) < _iota((w, w), 1)).astype(BF16)

        def block(rows, kb, masked):
            keys = pl.ds(pl.multiple_of(kb * w, w), w)
            q, do = q_ref[rows, :], do_ref[rows, :]
            z, sp, mask = _sb_pair(q, k_ref[keys, :], masked)
            d_a = _dot_nt(do, v_ref[keys, :])
            rest = p_ref[rows, :] - jnp.sum(sp, axis=-1, keepdims=True)
            a = jnp.exp(z - sp - _dot(sp, after) - rest)
            if masked:
                a = jnp.where(mask, a, 0.0)
            g = a * d_a
            sig = jnp.exp(z - sp)
            dz = g - sig * (g + (g_ref[rows, :] + _dot(g, before)))
            if masked:
                dz = jnp.where(mask, dz, 0.0)
            dz = dz.astype(BF16)
            dv_ref[keys, :] += _dot_tn(a, do)
            dk_ref[keys, :] += _dot_tn(dz, q)
            acc_ref[rows, :] += _dot(dz, k_ref[keys, :])
            p_ref[rows, :] = rest
            g_ref[rows, :] += jnp.sum(g, axis=-1, keepdims=True)

        def step(kb, carry):
            block(slice(0, blk), kb, False)
            return carry

        lax.fori_loop(first, 2 * qi, step, 0)
        block(slice(0, blk), 2 * qi, True)
        block(slice(w, blk), 2 * qi + 1, True)
        dq_ref[...] = acc_ref[...]

    qspec = pl.BlockSpec((blk, D_HEAD), lambda h, i: (i, h))
    whole = pl.BlockSpec((T, D_HEAD), lambda h, i: (0, h))
    return pl.pallas_call(
        body, out_shape=(jax.ShapeDtypeStruct((T, BR_W), F32),) * 3, grid=(N_HEAD, nb),
        in_specs=[qspec, whole, whole, pl.BlockSpec((1, blk, 1), lambda h, i: (h, i, 0)), qspec,
                  pl.BlockSpec(memory_space=pltpu.SMEM)],
        out_specs=(qspec, whole, whole),
        scratch_shapes=[pltpu.VMEM((blk, D_HEAD), F32), pltpu.VMEM((blk, 1), F32), pltpu.VMEM((blk, 1), F32)],
        name="sb_bwd", compiler_params=_cp("arbitrary", "arbitrary"))(sq, sk, sv, ltot, do, cut)


def _mem_kv(mem, gm, w_kv, gk):
    def body(mem_ref, gm_ref, w_ref, gk_ref, mn_ref, kv_ref, kh_ref, vm_ref):
        mv = mem_ref[...]
        r = lax.rsqrt(jnp.mean(mv * mv, axis=-1, keepdims=True) + EPS)
        mn = (mv * r * gm_ref[...]).astype(BF16)
        mn_ref[...] = mn
        kv = lax.dot_general(mn, w_ref[...], _NN, preferred_element_type=F32)
        kv_ref[...] = kv
        for h in range(N_HEAD):
            kh = kv[:, _hs(h)]
            rk = lax.rsqrt(jnp.mean(kh * kh, axis=-1, keepdims=True) + EPS)
            kh_ref[:, _hs(h)] = (kh * rk * gk_ref[...]).astype(BF16)
        vm_ref[...] = kv[:, BR_W:].astype(BF16)

    return pl.pallas_call(
        body,
        out_shape=(jax.ShapeDtypeStruct((N_MEM, D_MODEL), BF16), jax.ShapeDtypeStruct((N_MEM, 2 * BR_W), F32),
                   jax.ShapeDtypeStruct((N_MEM, BR_W), BF16), jax.ShapeDtypeStruct((N_MEM, BR_W), BF16)),
        name="mem_kv", compiler_params=_cp())(mem, gm, w_kv, gk)


def _mem_q(x_ref, gq_ref, h):
    xh = x_ref[:, _hs(h)].astype(F32)
    r = lax.rsqrt(jnp.mean(xh * xh, axis=-1, keepdims=True) + EPS)
    return r, xh * r


def _mem_probs(qn, kh):
    s = _dot_nt(qn, kh) * (D_HEAD ** -0.5)
    e = jnp.exp(s - jnp.max(s, axis=-1, keepdims=True))
    return e / jnp.sum(e, axis=-1, keepdims=True)


def _mem_fwd(proj, kh, vm, gq):
    T = proj.shape[0]
    tm = min(TM, T)

    def body(x_ref, kh_ref, vm_ref, gq_ref, o_ref):
        for h in range(N_HEAD):
            _, y = _mem_q(x_ref, gq_ref, h)
            p = _mem_probs((y * gq_ref[...]).astype(BF16), kh_ref[:, _hs(h)])
            o_ref[:, _hs(h)] = _dot(p, vm_ref[:, _hs(h)]).astype(BF16)

    return pl.pallas_call(
        body, out_shape=jax.ShapeDtypeStruct((T, BR_W), BF16), grid=(T // tm,),
        in_specs=[_rowspec(tm, BR_W, MEMQ_OFF // BR_W), _full((N_MEM, BR_W)), _full((N_MEM, BR_W)),
                  _full((1, D_HEAD))],
        out_specs=_rowspec(tm, BR_W), name="mem_fwd", compiler_params=_cp("parallel"))(proj, kh, vm, gq)


def _mem_bwd(proj, kh, vm, gq, do, into):
    T = proj.shape[0]
    tm = min(TM, T)

    def body(x_ref, kh_ref, vm_ref, gq_ref, do_ref, into_ref, dx_ref, dkh_ref, dvm_ref, dgq_ref):
        i = pl.program_id(0)

        @pl.when(i == 0)
        def _():
            dkh_ref[...] = jnp.zeros_like(dkh_ref)
            dvm_ref[...] = jnp.zeros_like(dvm_ref)
            dgq_ref[...] = jnp.zeros_like(dgq_ref)

        dg = jnp.zeros((1, D_HEAD), F32)
        for h in range(N_HEAD):
            r, y = _mem_q(x_ref, gq_ref, h)
            qn = (y * gq_ref[...]).astype(BF16)
            p = _mem_probs(qn, kh_ref[:, _hs(h)])
            dov = do_ref[:, _hs(h)]
            dp = _dot_nt(dov, vm_ref[:, _hs(h)])
            ds = p * (dp - jnp.sum(dp * p, axis=-1, keepdims=True)) * (D_HEAD ** -0.5)
            dqn = _dot(ds, kh_ref[:, _hs(h)])
            dkh_ref[:, _hs(h)] += _dot_tn(ds, qn)
            dvm_ref[:, _hs(h)] += _dot_tn(p, dov)
            dg = dg + jnp.sum(dqn * y, axis=0, keepdims=True)
            dy = dqn * gq_ref[...]
            dx_ref[:, _hs(h)] = (r * (dy - y * jnp.mean(dy * y, axis=-1, keepdims=True))).astype(BF16)
        dgq_ref[...] += dg

    return pl.pallas_call(
        body,
        out_shape=(jax.ShapeDtypeStruct(into.shape, BF16), jax.ShapeDtypeStruct((N_MEM, BR_W), F32),
                   jax.ShapeDtypeStruct((N_MEM, BR_W), F32), jax.ShapeDtypeStruct((1, D_HEAD), F32)),
        grid=(T // tm,),
        in_specs=[_rowspec(tm, BR_W, MEMQ_OFF // BR_W), _full((N_MEM, BR_W)), _full((N_MEM, BR_W)),
                  _full((1, D_HEAD)), _rowspec(tm, BR_W), HBM],
        out_specs=(_rowspec(tm, BR_W, MEMQ_OFF // BR_W), _full((N_MEM, BR_W)), _full((N_MEM, BR_W)),
                   _full((1, D_HEAD))),
        input_output_aliases={5: 0}, name="mem_bwd", compiler_params=_cp("arbitrary"))(proj, kh, vm, gq, do, into)


def _mem_kv_bwd(mem, gm, w_kv, gk, kv, mn, dkh, dvm):
    def body(mem_ref, gm_ref, w_ref, gk_ref, kv_ref, mn_ref, dkh_ref, dvm_ref, dw_ref, dgm_ref, dgk_ref, dkv_ref):
        dgk = jnp.zeros((1, D_HEAD), F32)
        for h in range(N_HEAD):
            kh = kv_ref[:, _hs(h)]
            r = lax.rsqrt(jnp.mean(kh * kh, axis=-1, keepdims=True) + EPS)
            y = kh * r
            dn = dkh_ref[:, _hs(h)]
            dgk = dgk + jnp.sum(dn * y, axis=0, keepdims=True)
            dy = dn * gk_ref[...]
            dkv_ref[:, _hs(h)] = (r * (dy - y * jnp.mean(dy * y, axis=-1, keepdims=True))).astype(BF16)
        dkv_ref[:, BR_W:] = dvm_ref[...].astype(BF16)
        dgk_ref[...] = dgk
        dkv = dkv_ref[...]
        dw_ref[...] = lax.dot_general(mn_ref[...], dkv, _TN, preferred_element_type=F32)
        dmn = lax.dot_general(dkv, w_ref[...], _NT, preferred_element_type=F32)
        mv = mem_ref[...]
        memn = mv * lax.rsqrt(jnp.mean(mv * mv, axis=-1, keepdims=True) + EPS)
        dgm_ref[...] = jnp.sum(dmn * memn, axis=0, keepdims=True)

    return pl.pallas_call(
        body,
        out_shape=(jax.ShapeDtypeStruct((D_MODEL, 2 * BR_W), F32), jax.ShapeDtypeStruct((1, D_MODEL), F32),
                   jax.ShapeDtypeStruct((1, D_HEAD), F32)),
        scratch_shapes=[pltpu.VMEM((N_MEM, 2 * BR_W), BF16)], name="mem_kv_bwd",
        compiler_params=_cp())(mem, gm, w_kv, gk, kv, mn, dkh, dvm)


def _merge_fwd(og, osb, om, proj, wg, ws, wm):
    T = og.shape[0]
    tm = min(TM, T)

    def body(og_ref, os_ref, om_ref, g0, g1, g2, wg_ref, ws_ref, wm_ref, mix_ref, yg_ref, ys_ref, ym_ref):
        mix = jnp.zeros((tm, D_MODEL), F32)
        for o_ref, gl_ref, w_ref, y_ref in ((og_ref, g0, wg_ref, yg_ref), (os_ref, g1, ws_ref, ys_ref),
                                            (om_ref, g2, wm_ref, ym_ref)):
            y = lax.dot_general(o_ref[...], w_ref[...], _NN, preferred_element_type=F32)
            y_ref[...] = y.astype(BF16)
            mix = mix + _sigmoid(gl_ref[...].astype(F32)) * y
        mix_ref[...] = mix.astype(BF16)

    br = _rowspec(tm, BR_W)
    wspec = _full((BR_W, D_MODEL))
    out = _rowspec(tm, D_MODEL)
    gates = [_rowspec(tm, D_MODEL, GATE_COL + b) for b in range(3)]
    return pl.pallas_call(
        body, out_shape=(jax.ShapeDtypeStruct((T, D_MODEL), BF16),) * 4, grid=(T // tm,),
        in_specs=[br, br, br, *gates, wspec, wspec, wspec],
        out_specs=(out,) * 4, name="merge_fwd",
        compiler_params=_cp("parallel"))(og, osb, om, proj, proj, proj, wg, ws, wm)


def _merge_bwd(dmix, proj, ys, os_, ws):
    T = dmix.shape[0]
    tm = min(TM, T)

    def body(dmix_ref, g0, g1, g2, y0, y1, y2, o0, o1, o2, w0, w1, w2, dgl_ref, do0, do1, do2, dw0, dw1, dw2):
        i = pl.program_id(0)
        dm = dmix_ref[...].astype(F32)
        for b, (gl_ref, y_ref, o_ref, w_ref, do_ref, dw_ref) in enumerate((
                (g0, y0, o0, w0, do0, dw0), (g1, y1, o1, w1, do1, dw1), (g2, y2, o2, w2, do2, dw2))):
            gate = _sigmoid(gl_ref[...].astype(F32))
            dgl_ref[:, b * D_MODEL:(b + 1) * D_MODEL] = (dm * y_ref[...].astype(F32) * gate * (1.0 - gate)).astype(BF16)
            dy = (gate * dm).astype(BF16)
            do_ref[...] = lax.dot_general(dy, w_ref[...], _NT, preferred_element_type=F32).astype(BF16)
            _accum(dw_ref, i == 0, lax.dot_general(dy, o_ref[...], _TN, preferred_element_type=F32))

    br = _rowspec(tm, BR_W)
    wide = _rowspec(tm, D_MODEL)
    wspec = _full((BR_W, D_MODEL))
    wtspec = _full((D_MODEL, BR_W))
    gates = [_rowspec(tm, D_MODEL, GATE_COL + b) for b in range(3)]
    return pl.pallas_call(
        body,
        out_shape=(jax.ShapeDtypeStruct((T, PROJ_W), BF16),) + (jax.ShapeDtypeStruct((T, BR_W), BF16),) * 3
        + (jax.ShapeDtypeStruct((D_MODEL, BR_W), F32),) * 3,
        grid=(T // tm,),
        in_specs=[wide, *gates, wide, wide, wide, br, br, br, wspec, wspec, wspec],
        out_specs=(_rowspec(tm, 3 * D_MODEL, GATE_OFF // (3 * D_MODEL)), br, br, br, wtspec, wtspec, wtspec),
        name="merge_bwd",
        compiler_params=_cp("arbitrary"))(dmix, proj, proj, proj, *ys, *os_, *ws)


def _loss(y, tgt):
    T, dm = y.shape
    tm = min(TM, T)

    def body(y_ref, t_ref, dy_ref, dyb_ref, sq_ref):
        err = y_ref[...] - t_ref[...]
        dy = err * (1.0 / dm)
        dy_ref[...] = dy
        dyb_ref[...] = dy.astype(BF16)
        _accum(sq_ref, pl.program_id(0) == 0, jnp.sum(err * err, axis=0, keepdims=True))

    return pl.pallas_call(
        body,
        out_shape=(jax.ShapeDtypeStruct((T, dm), F32), jax.ShapeDtypeStruct((T, dm), BF16),
                   jax.ShapeDtypeStruct((1, dm), F32)),
        grid=(T // tm,), in_specs=[_rowspec(tm, dm), _rowspec(tm, dm)],
        out_specs=(_rowspec(tm, dm), _rowspec(tm, dm), _full((1, dm))), name="loss",
        compiler_params=_cp("arbitrary"))(y, tgt)


def _split_w_in(slabs):
    width = slabs[0].shape[1]

    def cols(lo, hi):
        return [s[:, max(lo - j * width, 0):min(hi - j * width, width)] for j, s in enumerate(slabs)
                if lo < (j + 1) * width and hi > j * width]

    main = [c for piece in (IN_GATE, IN_QKV, IN_SB, IN_Z, IN_MEMQ) for c in cols(*piece)]
    ab = jnp.concatenate(cols(*IN_AB), axis=1)
    return jnp.concatenate(main, axis=1), jnp.pad(ab, ((0, 0), (0, LANES - ab.shape[1])))


def _local_step(x, mem, tgt, W, P, dist=None):
    w_main, w_ab = W["w_main"], W["w_ab"]
    avec = jnp.pad(jnp.concatenate([P["a_log"], P["dt_bias"]], axis=0), ((0, 0), (0, LANES - N_HEAD)))

    h = _rms_fwd(x, P["norm1_g"], "rms1")
    if dist is None:
        proj = _mm(h, w_main, "nn", BF16, "in_proj", tn=1792)
    else:
        proj, gathered = _mm(h, w_main, "nn", BF16, "in_proj", tn=1792, comm=dist.gather_rest())
        rest, conv_w = dist.weights_from(gathered)
        W, P = {**W, **rest}, {**P, "conv_w": conv_w}
    wbr = (W["w_br_gdn"], W["w_br_sb"], W["w_br_mem"])
    ab = _mm(h, w_ab, "nn", F32, "in_proj_ab")
    q, k, v, gb = _gdn_pre(proj, P["conv_w"], ab, avec)
    tinv, tinv_t = _gdn_inv(k, gb)
    og, oraw, shist, vn = _gdn_fwd(q, k, v, gb, proj, P["gdn_norm_g"], tinv)
    sq, sk, sv = _sb_pre(proj, P["sb_q_norm_g"], P["sb_k_norm_g"])
    osb, ltot, cut = _sb_fwd(sq, sk, sv)
    mn, kv, kh, vm = _mem_kv(mem, P["mem_norm_g"], W["w_mem_kv"], P["mem_k_norm_g"])
    om = _mem_fwd(proj, kh, vm, P["mem_q_norm_g"])
    mix, yg, ys, ym = _merge_fwd(og, osb, om, proj, *wbr)
    x1 = _mm(mix, W["w_o"], "nn", F32, "out_proj", extra=x, epi=_epi_add)
    h2 = _rms_fwd(x1, P["norm2_g"], "rms2")
    if dist is None:
        u = _mm(h2, W["w_up"], "nn", BF16, "mlp_up", tn=2048)
    else:
        u, gathered = _mm(h2, W["w_up"], "nn", BF16, "mlp_up", tn=2048, comm=dist.gather_late())
        W = {**W, **dist.late_weights_from(gathered)}
    y = _mm(u, W["w_down"], "nn", F32, "mlp_down", a_fn=_relu2, extra=x1, epi=_epi_add)
    dy, dyb, sq_err = _loss(y, tgt)

    G = {}
    du = _mm(dyb, W["w_down"], "nt", BF16, "d_mlp_act", tn=2048, extra=u, epi=_epi_drelu2)
    G["w_down"] = _mm(u, dyb, "tn", F32, "dw_down", a_fn=_relu2)
    G["w_up"] = _mm(du, h2, "tn", F32, "dw_up")
    dh2 = _mm(du, W["w_up"], "nt", F32, "d_h2")
    dx1, dx1b, G["norm2_g"] = _rms_bwd(dh2, x1, P["norm2_g"], dy, "rms2_bwd")
    dmix = _mm(dx1b, W["w_o"], "nt", BF16, "d_mix")
    G["w_o"] = _mm(mix, dx1b, "tn", F32, "dw_o")
    dproj, dog, dosb, dom, G["w_br_gdn"], G["w_br_sb"], G["w_br_mem"] = _merge_bwd(
        dmix, proj, (yg, ys, ym), (og, osb, om), wbr)
    dq, dk, dv, dgb, dproj, G["gdn_norm_g"] = _gdn_bwd(q, k, v, gb, proj, P["gdn_norm_g"], oraw, shist, tinv_t, vn, dog,
                                                      dproj)
    dxc, dab, G["conv_w"], dav = _gdn_pre_bwd(proj, P["conv_w"], ab, avec, dq, dk, dv, dgb)
    dproj = _conv_bwd(dxc, P["conv_w"], dproj)
    G["a_log"], G["dt_bias"] = dav[0:1, :N_HEAD], dav[1:2, :N_HEAD]
    dsq, dsk, dsv = _sb_bwd(sq, sk, sv, ltot, dosb, cut)
    dproj, G["sb_q_norm_g"], G["sb_k_norm_g"] = _sb_pre_bwd(proj, P["sb_q_norm_g"], P["sb_k_norm_g"], dsq, dsk, dsv,
                                                            dproj)
    dproj, dkh, dvm, G["mem_q_norm_g"] = _mem_bwd(proj, kh, vm, P["mem_q_norm_g"], dom, dproj)
    G["w_mem_kv"], G["mem_norm_g"], G["mem_k_norm_g"] = _mem_kv_bwd(
        mem, P["mem_norm_g"], W["w_mem_kv"], P["mem_k_norm_g"], kv, mn, dkh, dvm)
    dw_ab = _mm(dab, h, "tn", F32, "dw_in_ab")
    if dist is None:
        dw_main = _mm(dproj, h, "tn", F32, "dw_in")
    else:
        early = [n for n, _, _ in BIG if n != "w_in"]
        dw_main, landed = _mm(dproj, h, "tn", F32, "dw_in", comm=dist.scatter(G, early, "early"))
        dist.collect(early, landed)
    G["w_in"] = jnp.concatenate([dw_main[QKV_OFF:SB_OFF], dw_main[Z_OFF:MEMQ_OFF], dw_ab[:8], dw_main[SB_OFF:Z_OFF],
                                 dw_main[MEMQ_OFF:], dw_main[:QKV_OFF]], axis=0)
    if dist is None:
        dh = _mm(dproj, w_main, "nt", F32, "d_h")
    else:
        dh, landed = _mm(dproj, w_main, "nt", F32, "d_h", comm=dist.scatter(G, ["w_in"], "late"))
        dist.collect(["w_in"], landed)
    dh = _mm(dab, w_ab, "nt", F32, "d_h_ab", extra=dh, epi=_epi_add)
    dx, _, G["norm1_g"] = _rms_bwd(dh, x, P["norm1_g"], dx1, "rms1_bwd")
    return sq_err, dx, G


def _comm(name, ins, out_shapes, plan):
    n_in, n_out = len(ins), len(out_shapes)
    probe = plan([None] * n_in, [None] * n_out, 0, 0, 0, dry=True)
    n_copy = probe

    def body(*refs):
        in_refs, out_refs = refs[:n_in], refs[n_in:n_in + n_out]
        send_sems, recv_sems = refs[n_in + n_out:]
        x, y, c = lax.axis_index("x"), lax.axis_index("y"), lax.axis_index("c")
        copies = []
        for k, (src, dst, dev) in enumerate(plan(in_refs, out_refs, x, y, c, dry=False)):
            if dev is None:
                cp = pltpu.make_async_copy(src, dst, send_sems.at[k])
            else:
                cp = pltpu.make_async_remote_copy(src_ref=src, dst_ref=dst, send_sem=send_sems.at[k],
                                                  recv_sem=recv_sems.at[k], device_id=dev, device_id_type=MESH)
            cp.start()
            copies.append(cp)
        for cp in copies:
            cp.wait()

    return pl.pallas_call(
        body, out_shape=tuple(out_shapes), in_specs=[HBM] * n_in, out_specs=tuple([HBM] * n_out),
        scratch_shapes=[pltpu.SemaphoreType.DMA((n_copy,)), pltpu.SemaphoreType.DMA((n_copy,))], name=name)(*ins)


def _other_chips(x, y):
    return ((1 - x, y), (x, 1 - y), (1 - x, 1 - y))


def _gather_plan(parts, direct=()):
    n, every = len(parts), list(parts) + list(direct)

    def copies(ins, outs, send, recv, scratch):
        x, y, c = lax.axis_index("x"), lax.axis_index("y"), lax.axis_index("c")
        me = 2 * x + y
        chips = _other_chips(x, y)
        local_sems, staged = scratch[0], scratch[1:]

        def remote(src, dst, k, dev):
            return pltpu.make_async_remote_copy(src_ref=src, dst_ref=dst, send_sem=send.at[k], recv_sem=recv.at[k],
                                                device_id=dev, device_id_type=MESH)

        def half(p, ci):
            hr = ins[p].shape[0] // 2
            return pl.ds(pl.multiple_of(ci * hr, 16), hr)

        sent = [remote(ins[p].at[half(p, c)], outs[p].at[me, half(p, c)], 6 * p + f, (px, py, c))
                for p in range(n) for f, (px, py) in enumerate(chips)]
        sent += [remote(ins[p], outs[p].at[me], 6 * n + 3 * (p - n) + f, (px, py, c))
                 for p in range(n, len(every)) for f, (px, py) in enumerate(chips)]
        landed = [outs[p].at[2 * px + py, half(p, c)] for p in range(n) for px, py in chips]
        passed = [remote(landed[3 * p + f], landed[3 * p + f], 6 * p + 3 + f, (x, y, 1 - c))
                  for p in range(n) for f in range(3)]
        loads = [pltpu.make_async_copy(ins[p], staged[p], local_sems.at[2 * p]) for p in range(len(every))]
        stores = [pltpu.make_async_copy(staged[p], outs[p].at[me], local_sems.at[2 * p + 1]) for p in range(len(every))]
        return sent, passed, loads, stores

    def start(*refs):
        sent, _, loads, _ = copies(*refs)
        for cp in loads + sent:
            cp.start()

    def mid(*refs):
        sent, passed, loads, stores = copies(*refs)
        for ld, st in zip(loads, stores):
            ld.wait()
            st.start()
        for p in range(n):
            for f in range(3):
                sent[3 * p + f].wait_recv()
                passed[3 * p + f].start()

    def finish(*refs):
        sent, passed, _, stores = copies(*refs)
        for cp in sent[:3 * n]:
            cp.wait_send()
        for cp in passed + sent[3 * n:] + stores:
            cp.wait()

    return _Hosted(every, [jax.ShapeDtypeStruct((4,) + p.shape, p.dtype) for p in every], 6 * n + 3 * len(direct),
                   start, finish, mid,
                   [pltpu.SemaphoreType.DMA((2 * len(every),))] + [pltpu.VMEM(p.shape, p.dtype) for p in every])


def _scatter_plan(pairs):
    def copies(ins, outs, send, recv, scratch):
        x, y, c = lax.axis_index("x"), lax.axis_index("y"), lax.axis_index("c")
        me = 2 * x + y
        return [pltpu.make_async_remote_copy(src_ref=src.at[2 * px + py], dst_ref=dst.at[me], send_sem=send.at[3 * p + f],
                                             recv_sem=recv.at[3 * p + f], device_id=(px, py, c), device_id_type=MESH)
                for p, (src, dst) in enumerate(zip(ins, outs)) for f, (px, py) in enumerate(_other_chips(x, y))]

    def start(*refs):
        for cp in copies(*refs):
            cp.start()

    def finish(*refs):
        for cp in copies(*refs):
            cp.wait()

    return _Hosted(pairs, [jax.ShapeDtypeStruct(a.shape, a.dtype) for a in pairs], 3 * len(pairs), start, finish)


def _run_hosted(comm, name):
    n_in, n_out = len(comm.ins), len(comm.out_shapes)

    def body(*refs):
        args = (refs[:n_in], refs[n_in:n_in + n_out], refs[n_in + n_out], refs[n_in + n_out + 1], refs[n_in + n_out + 2:])
        comm.start(*args)
        if comm.mid is not None:
            comm.mid(*args)
        comm.finish(*args)

    sems = [pltpu.SemaphoreType.DMA((comm.n_sems,)), pltpu.SemaphoreType.DMA((comm.n_sems,))]
    return list(pl.pallas_call(
        body, out_shape=tuple(comm.out_shapes), in_specs=[HBM] * n_in, out_specs=tuple([HBM] * n_out),
        scratch_shapes=sems + comm.scratch, name=name, compiler_params=_cp())(*comm.ins))


def _swap_halves(slabs, name):
    n = len(slabs)

    def body(*refs):
        ins, outs = refs[:n], refs[n:2 * n]
        send, recv = refs[2 * n:]
        x, y, c = lax.axis_index("x"), lax.axis_index("y"), lax.axis_index("c")
        other = (x, y, 1 - c)
        for p in range(n):
            for j in range(4):
                pltpu.make_async_remote_copy(src_ref=ins[p].at[j, 1 - c], dst_ref=outs[p].at[j], send_sem=send.at[p],
                                             recv_sem=recv.at[p], device_id=other, device_id_type=MESH).start()
        for p in range(n):
            pltpu.make_async_remote_copy(src_ref=outs[p], dst_ref=outs[p], send_sem=send.at[p], recv_sem=recv.at[p],
                                         device_id=other, device_id_type=MESH).wait()

    shapes = [jax.ShapeDtypeStruct((4,) + s.shape[2:], s.dtype) for s in slabs]
    return pl.pallas_call(
        body, out_shape=tuple(shapes), in_specs=[HBM] * n, out_specs=tuple([HBM] * n),
        scratch_shapes=[pltpu.SemaphoreType.DMA((n,)), pltpu.SemaphoreType.DMA((n,))], name=name)(*slabs)


def _join_halves(both):
    n = len(both)

    def body(*refs):
        bufs = refs[n:2 * n]
        send, recv = refs[2 * n:]
        x, y, c = lax.axis_index("x"), lax.axis_index("y"), lax.axis_index("c")
        copies = []
        for p in range(n):
            cp = pltpu.make_async_remote_copy(src_ref=bufs[p].at[c], dst_ref=bufs[p].at[c], send_sem=send.at[p],
                                              recv_sem=recv.at[p], device_id=(x, y, 1 - c), device_id_type=MESH)
            cp.start()
            copies.append(cp)
        for cp in copies:
            cp.wait()

    return pl.pallas_call(
        body, out_shape=tuple(jax.ShapeDtypeStruct(a.shape, a.dtype) for a in both), in_specs=[HBM] * n,
        out_specs=tuple([HBM] * n), input_output_aliases={p: p for p in range(n)},
        scratch_shapes=[pltpu.SemaphoreType.DMA((n,)), pltpu.SemaphoreType.DMA((n,))], name="grad_join_cores")(*both)


def _gather_all(a, name):
    def plan(ins, outs, x, y, c, dry):
        if dry:
            return 8
        me = 4 * x + 2 * y + c
        copies = [(ins[0], outs[0].at[me], None)]
        for f in range(1, 8):
            peer = (1 - x if f & 4 else x, 1 - y if f & 2 else y, 1 - c if f & 1 else c)
            copies.append((ins[0], outs[0].at[me], peer))
        return copies

    return _comm(name, [a], [jax.ShapeDtypeStruct((8,) + a.shape, a.dtype)], plan)[0]


def _sum_slots(a, name, extra=None):
    n, R, _ = a.shape
    rb = min(ROW_BLK, R)

    def body(*refs):
        a_ref, o_ref = refs[0], refs[-1]
        acc = a_ref[0]
        for s in range(1, n):
            acc = acc + a_ref[s]
        if extra is not None:
            acc = acc + refs[1][...]
        o_ref[...] = acc

    ins = [a] + ([extra] if extra is not None else [])
    in_specs = [pl.BlockSpec((n, rb, LANES), lambda i: (0, i, 0))] + ([_rowspec(rb, LANES)] if extra is not None else [])
    return pl.pallas_call(
        body, out_shape=jax.ShapeDtypeStruct((R, LANES), F32), grid=(R // rb,), in_specs=in_specs,
        out_specs=_rowspec(rb, LANES), name=name, compiler_params=_cp("parallel"))(*ins)


def _pair_sum(slab, theirs, core, name):
    _, _, hr, C = slab.shape

    def body(c_ref, a_ref, b_ref, o_ref):
        o_ref[...] = (a_ref[...] + b_ref[...]).astype(BF16)

    return pl.pallas_call(
        body, out_shape=jax.ShapeDtypeStruct((4, hr, C), BF16),
        grid_spec=pltpu.PrefetchScalarGridSpec(
            num_scalar_prefetch=1, grid=(4,),
            in_specs=[pl.BlockSpec((None, None, hr, C), lambda j, c_ref: (j, c_ref[0], 0, 0)),
                      pl.BlockSpec((None, hr, C), lambda j, c_ref: (j, 0, 0))],
            out_specs=pl.BlockSpec((None, hr, C), lambda j, c_ref: (j, 0, 0))),
        name=name, compiler_params=_cp("parallel"))(core, slab, theirs)


def _chip_sum(recv, pairs, where, name):
    _, hr, C = recv.shape

    def body(w_ref, r_ref, p_ref, o_ref):
        me = w_ref[0]
        o_ref[...] = jnp.zeros_like(o_ref)
        for s in range(4):
            @pl.when(me == s)
            def _():
                o_ref[...] += p_ref[...].astype(F32)

            @pl.when(me != s)
            def _():
                o_ref[...] += r_ref[s].astype(F32)

    return pl.pallas_call(
        body, out_shape=jax.ShapeDtypeStruct((2, hr, C), F32),
        grid_spec=pltpu.PrefetchScalarGridSpec(
            num_scalar_prefetch=1, grid=(1,),
            in_specs=[pl.BlockSpec((4, hr, C), lambda i, w_ref: (0, 0, 0)),
                      pl.BlockSpec((None, hr, C), lambda i, w_ref: (w_ref[0], 0, 0))],
            out_specs=pl.BlockSpec((None, hr, C), lambda i, w_ref: (w_ref[1], 0, 0))),
        name=name, compiler_params=_cp("arbitrary"))(where, recv, pairs)


def _adamw(w, g, m, v, name):
    R, C = w.shape
    rb = min(ADAM_ROWS, R)
    c1 = 1.0 - ADAM_B1 ** ADAM_STEP
    c2 = 1.0 - ADAM_B2 ** ADAM_STEP

    def body(w_ref, g_ref, m_ref, v_ref, d_ref, nm_ref, nv_ref):
        gv = g_ref[...]
        nm = ADAM_B1 * m_ref[...] + (1.0 - ADAM_B1) * gv
        nv = ADAM_B2 * v_ref[...] + (1.0 - ADAM_B2) * (gv * gv)
        d_ref[...] = -ADAM_LR * ((nm / c1) / (jnp.sqrt(nv / c2) + ADAM_EPS) + ADAM_WD * w_ref[...])
        nm_ref[...] = nm
        nv_ref[...] = nv

    spec = _rowspec(rb, C)
    return pl.pallas_call(
        body, out_shape=(jax.ShapeDtypeStruct((R, C), F32),) * 3, grid=(R // rb,), in_specs=[spec] * 4,
        out_specs=(spec,) * 3, name=name, compiler_params=_cp("parallel"))(w, g, m, v)


class _Dist:
    def __init__(self, shards):
        self.shards = shards
        self.chip = 2 * lax.axis_index("x") + lax.axis_index("y")
        self.where = jnp.stack([self.chip, lax.axis_index("c")]).astype(jnp.int32)
        self.pairs, self.landed = {}, {}

    @staticmethod
    def _unshard(name, blk):
        _, (r, cc), axis = next(b for b in BIG if b[0] == name)
        return blk.reshape(4 * r, cc) if axis == 0 else blk.transpose(1, 0, 2).reshape(r, 4 * cc)

    def gather_first(self):
        got = _run_hosted(_gather_plan([self.shards["w_in"].astype(BF16)]), "gather_w_in")[0]
        return _split_w_in([got[j] for j in range(4)])

    LATE = ("w_down",)

    def gather_rest(self):
        rest = [self.shards[n].astype(BF16) for n, _, _ in BIG if n != "w_in" and n not in self.LATE]
        return _gather_plan(rest, [self.shards["conv_w"]])

    def weights_from(self, gathered):
        names = [n for n, _, _ in BIG if n != "w_in" and n not in self.LATE]
        conv = gathered[-1]
        taps, width = conv.shape[1:]
        return ({n: self._unshard(n, g) for n, g in zip(names, gathered)},
                conv.transpose(1, 0, 2).reshape(taps, 4 * width))

    def gather_late(self):
        return _gather_plan([self.shards[n].astype(BF16) for n in self.LATE])

    def late_weights_from(self, gathered):
        return {n: self._unshard(n, g) for n, g in zip(self.LATE, gathered)}

    def scatter(self, G, names, tag):
        slabs = []
        for name, (r, cc), axis in BIG:
            if name not in names:
                continue
            g = G[name]
            if axis == 0:
                slabs.append(g.reshape(4, 2, r // 2, cc))
            else:
                slabs.append(g.reshape(4, 2, cc // 2, r))
        theirs = _swap_halves(slabs, "grad_swap_cores_" + tag)
        pairs = [_pair_sum(s, t, self.where[1:], "pair_sum_" + n) for s, t, n in zip(slabs, theirs, names)]
        self.pairs.update(zip(names, pairs))
        return _scatter_plan(pairs)

    def collect(self, names, landed):
        self.landed.update(zip(names, landed))

    def finish(self):
        names = [n for n, _, _ in BIG]
        halves = [_chip_sum(self.landed[n], self.pairs[n], self.where, "chip_sum_" + n) for n in names]
        out = {}
        for (name, (r, cc), axis), both in zip(BIG, _join_halves(halves)):
            full = both.reshape(-1, both.shape[-1])
            out[name] = full if axis == 0 else full.T
        return out


def _pack_rows(parts, rows, dtype):
    flat = jnp.concatenate([p.reshape(-1).astype(dtype) for p in parts])
    return jnp.pad(flat, (0, rows * LANES - flat.shape[0])).reshape(rows, LANES)


def _small_rows(n):
    return max(n // LANES, 1)


def _pack_small(vals):
    rows = []
    for name, n in SMALL:
        r = _small_rows(n)
        rows.append(jnp.pad(vals[name].reshape(-1), (0, r * LANES - n)).reshape(r, LANES))
    flat = jnp.concatenate(rows, axis=0)
    return jnp.pad(flat, ((0, SMALL_ROWS - flat.shape[0]), (0, 0)))


def _unpack_small(pack):
    out, r0 = {}, 0
    for name, n in SMALL:
        r = _small_rows(n)
        out[name] = pack[r0:r0 + r].reshape(-1)[:n]
        r0 += r
    return out


def kernel(x, mem, norm1_g, w_in, conv_w, a_log, dt_bias, gdn_norm_g, sb_q_norm_g, sb_k_norm_g, mem_norm_g, w_mem_kv, mem_q_norm_g, mem_k_norm_g, w_br_gdn, w_br_sb, w_br_mem, w_o, norm2_g, w_up, w_down, loss_target, m_norm1_g, m_w_in, m_conv_w, m_a_log, m_dt_bias, m_gdn_norm_g, m_sb_q_norm_g, m_sb_k_norm_g, m_mem_norm_g, m_w_mem_kv, m_mem_q_norm_g, m_mem_k_norm_g, m_w_br_gdn, m_w_br_sb, m_w_br_mem, m_w_o, m_norm2_g, m_w_up, m_w_down, v_norm1_g, v_w_in, v_conv_w, v_a_log, v_dt_bias, v_gdn_norm_g, v_sb_q_norm_g, v_sb_k_norm_g, v_mem_norm_g, v_w_mem_kv, v_mem_q_norm_g, v_mem_k_norm_g, v_w_br_gdn, v_w_br_sb, v_w_br_mem, v_w_o, v_norm2_g, v_w_up, v_w_down):
    wd = dict(norm1_g=norm1_g, w_in=w_in, conv_w=conv_w, a_log=a_log, dt_bias=dt_bias, gdn_norm_g=gdn_norm_g,
              sb_q_norm_g=sb_q_norm_g, sb_k_norm_g=sb_k_norm_g, mem_norm_g=mem_norm_g, w_mem_kv=w_mem_kv,
              mem_q_norm_g=mem_q_norm_g, mem_k_norm_g=mem_k_norm_g, w_br_gdn=w_br_gdn, w_br_sb=w_br_sb,
              w_br_mem=w_br_mem, w_o=w_o, norm2_g=norm2_g, w_up=w_up, w_down=w_down)
    md = dict(norm1_g=m_norm1_g, w_in=m_w_in, conv_w=m_conv_w, a_log=m_a_log, dt_bias=m_dt_bias,
              gdn_norm_g=m_gdn_norm_g, sb_q_norm_g=m_sb_q_norm_g, sb_k_norm_g=m_sb_k_norm_g,
              mem_norm_g=m_mem_norm_g, w_mem_kv=m_w_mem_kv, mem_q_norm_g=m_mem_q_norm_g,
              mem_k_norm_g=m_mem_k_norm_g, w_br_gdn=m_w_br_gdn, w_br_sb=m_w_br_sb, w_br_mem=m_w_br_mem, w_o=m_w_o,
              norm2_g=m_norm2_g, w_up=m_w_up, w_down=m_w_down)
    vd = dict(norm1_g=v_norm1_g, w_in=v_w_in, conv_w=v_conv_w, a_log=v_a_log, dt_bias=v_dt_bias,
              gdn_norm_g=v_gdn_norm_g, sb_q_norm_g=v_sb_q_norm_g, sb_k_norm_g=v_sb_k_norm_g,
              mem_norm_g=v_mem_norm_g, w_mem_kv=v_w_mem_kv, mem_q_norm_g=v_mem_q_norm_g,
              mem_k_norm_g=v_mem_k_norm_g, w_br_gdn=v_w_br_gdn, w_br_sb=v_w_br_sb, w_br_mem=v_w_br_mem, w_o=v_w_o,
              norm2_g=v_norm2_g, w_up=v_w_up, w_down=v_w_down)
    wd, md, vd = ({n: a[0] for n, a in d.items()} for d in (wd, md, vd))
    chip = 2 * lax.axis_index("x") + lax.axis_index("y")
    conv_shard = wd["conv_w"].shape

    dist = _Dist(wd)
    W = dict(zip(("w_main", "w_ab"), dist.gather_first()))
    P = {n: wd[n].reshape(1, -1) for n, _ in SMALL}

    sq_err, grad_x, G = _local_step(x[0], mem[0], loss_target[0], W, P, dist)
    loss = lax.psum(0.5 / D_MODEL * jnp.sum(sq_err), ("x", "y", "c"))

    g_big = dist.finish()

    spack = jnp.concatenate([_pack_small(G), G["conv_w"].reshape(CONV_ROWS, LANES)], axis=0)
    g_small = _sum_slots(_gather_all(spack, "gather_small_grads"), "small_grad_sum")
    g_conv_full = g_small[SMALL_ROWS:].reshape(conv_shard[0], 4 * conv_shard[1])
    g_conv = lax.dynamic_slice_in_dim(g_conv_full, chip * conv_shard[1], conv_shard[1], axis=1)

    grads, deltas, new_m, new_v = dict(g_big), {}, {}, {}
    for name, _, _ in BIG:
        deltas[name], new_m[name], new_v[name] = _adamw(wd[name], g_big[name], md[name], vd[name], "adamw_" + name)
    pack_sm = lambda d: jnp.concatenate([_pack_small(d), _pack_rows([d["conv_w"]], APACK_ROWS - SMALL_ROWS, F32)], axis=0)
    g_sm = jnp.concatenate([g_small[:SMALL_ROWS], _pack_rows([g_conv], APACK_ROWS - SMALL_ROWS, F32)], axis=0)
    small = (g_sm,) + _adamw(pack_sm(wd), g_sm, pack_sm(md), pack_sm(vd), "adamw_small")
    for out, pack in zip((grads, deltas, new_m, new_v), small):
        out.update(_unpack_small(pack[:SMALL_ROWS]))
        out["conv_w"] = pack[SMALL_ROWS:].reshape(-1)[:conv_shard[0] * conv_shard[1]].reshape(conv_shard)

    return (loss, grad_x[None], *[d[n][None] for d in (grads, deltas, new_m, new_v) for n in WEIGHTS])
```

```python
import jax
import jax.numpy as jnp
from jax import lax
from jax.experimental import pallas as pl
from jax.experimental.pallas import tpu as pltpu

F32 = jnp.float32
BF16 = jnp.bfloat16
MESH = pl.DeviceIdType.MESH

D_MODEL = 1024
N_HEAD = 4
D_HEAD = 128
BR_W = N_HEAD * D_HEAD
CONV_TAPS = 4
GDN_CHUNK = 64
INV_BLOCK = 16
INV_CHUNKS = 4
N_MEM = 256
D_FF = 4 * D_MODEL
EPS = 1e-6
LANES = 128
PROJ_W = 7168
GATE_OFF = 0
QKV_OFF = 3072
SB_OFF = 4608
Z_OFF = 6144
MEMQ_OFF = 6656
IN_GATE, IN_QKV, IN_SB, IN_Z, IN_MEMQ, IN_AB = (4104, 7176), (0, 1536), (2056, 3592), (1536, 2048), (3592, 4104), (2048, 2056)

ADAM_LR, ADAM_B1, ADAM_B2, ADAM_EPS, ADAM_WD, ADAM_STEP = 0.001, 0.9, 0.999, 1e-08, 0.01, 10

TM = 512
MM_TM = 1024
MM_TK = (2048, 1792, 1024, 128)
TK_TOK = 2048
GDN_STEP_CHUNKS = 8
GDN_BWD_STEP_CHUNKS = 2
G1_TM = 256
SB_BLK = 512
SB_DEAD = 120.0
VMEM_LIMIT = 48 << 20

BIG = (("w_in", (1024, 1794), 1), ("w_mem_kv", (256, 1024), 0), ("w_br_gdn", (512, 256), 1),
       ("w_br_sb", (512, 256), 1), ("w_br_mem", (512, 256), 1), ("w_o", (256, 1024), 0),
       ("w_up", (1024, 1024), 1), ("w_down", (1024, 1024), 0))
COL_SHARDED = tuple(n for n, _, a in BIG if a == 1)
GATE_COL = GATE_OFF // D_MODEL
QKV_COL = QKV_OFF // (3 * BR_W)
ROW_BLK = 1024
ADAM_ROWS = 128
SMALL = (("norm1_g", 1024), ("mem_norm_g", 1024), ("norm2_g", 1024), ("gdn_norm_g", 128), ("sb_q_norm_g", 128),
         ("sb_k_norm_g", 128), ("mem_q_norm_g", 128), ("mem_k_norm_g", 128), ("a_log", 4), ("dt_bias", 4))
SMALL_ROWS = 32
CONV_ROWS = 48
SPACK_ROWS = SMALL_ROWS + CONV_ROWS
APACK_ROWS = SMALL_ROWS + 16

WEIGHTS = ("norm1_g", "w_in", "conv_w", "a_log", "dt_bias", "gdn_norm_g", "sb_q_norm_g", "sb_k_norm_g",
           "mem_norm_g", "w_mem_kv", "mem_q_norm_g", "mem_k_norm_g", "w_br_gdn", "w_br_sb", "w_br_mem", "w_o",
           "norm2_g", "w_up", "w_down")


def _cp(*sem):
    return pltpu.CompilerParams(dimension_semantics=sem if sem else None, vmem_limit_bytes=VMEM_LIMIT)


HBM = pl.BlockSpec(memory_space=pl.ANY)

_NN = (((1,), (0,)), ((), ()))
_NT = (((1,), (1,)), ((), ()))
_TN = (((0,), (0,)), ((), ()))


def _dot(a, b, dims=_NN):
    return lax.dot_general(a.astype(BF16), b.astype(BF16), dims, preferred_element_type=F32)


def _dot_nt(a, b):
    return _dot(a, b, _NT)


def _dot_tn(a, b):
    return _dot(a, b, _TN)


def _dotf(a, b, dims=_NN):
    return lax.dot_general(a, b, dims, precision=lax.Precision.HIGHEST, preferred_element_type=F32)


def _sigmoid(v):
    return 0.5 * jnp.tanh(0.5 * v) + 0.5


def _softplus(v):
    return jnp.maximum(v, 0.0) + jnp.log(1.0 + jnp.exp(-jnp.abs(v)))


def _iota(shape, dim):
    return lax.broadcasted_iota(jnp.int32, shape, dim)


def _hs(h):
    return slice(h * D_HEAD, (h + 1) * D_HEAD)


def _rowspec(tm, w, col=0):
    return pl.BlockSpec((tm, w), lambda i: (i, col))


def _full(shape):
    return pl.BlockSpec(shape, lambda *_: (0,) * len(shape))


def _accum(ref, first, val):
    @pl.when(first)
    def _():
        ref[...] = val

    @pl.when(jnp.logical_not(first))
    def _():
        ref[...] += val


class _Hosted:
    def __init__(self, ins, out_shapes, n_sems, start, finish, mid=None, scratch=()):
        self.ins, self.out_shapes, self.n_sems = list(ins), list(out_shapes), n_sems
        self.start, self.mid, self.finish, self.scratch = start, mid, finish, list(scratch)


def _mm(a, b, mode, out_dtype, name, *, tm=None, tn=None, tk=None, a_fn=None, extra=None, epi=None, comm=None):
    if mode == "tn":
        (K, M), N = a.shape, b.shape[1]
    else:
        (M, K), N = a.shape, (b.shape[0] if mode == "nt" else b.shape[1])
    tm = min(tm or (1024 if mode == "tn" else MM_TM), M)
    tn = min(tn or 1024, N)
    tk = min(tk or (TK_TOK if mode == "tn" else next(t for t in MM_TK if K % t == 0)), K)
    nm, nn, nk = M // tm, N // tn, K // tk
    assert nm * tm == M and nn * tn == N and nk * tk == K, (name, a.shape, b.shape)
    if mode == "tn":
        a_spec = pl.BlockSpec((tk, tm), lambda i, j, k: (k, i))
    else:
        a_spec = pl.BlockSpec((tm, tk), lambda i, j, k: (i, k))
    if mode == "nt":
        b_spec = pl.BlockSpec((tn, tk), lambda i, j, k: (j, k))
    else:
        b_spec = pl.BlockSpec((tk, tn), lambda i, j, k: (k, j))
    dims = {"nn": _NN, "nt": _NT, "tn": _TN}[mode]
    o_spec = pl.BlockSpec((tm, tn), lambda i, j, k: (i, j))
    has_extra = extra is not None

    n_ci, n_co = (len(comm.ins), len(comm.out_shapes)) if comm else (0, 0)
    n_in = 2 + has_extra + n_ci
    steps = nm * nn * nk

    def body(*refs):
        a_ref, b_ref = refs[0], refs[1]
        e_ref = refs[2] if has_extra else None
        o_ref = refs[n_in]
        scratch = refs[n_in + 1 + n_co:]
        if comm:
            step = (pl.program_id(0) * nn + pl.program_id(1)) * nk + pl.program_id(2)
            cargs = (refs[2 + has_extra:n_in], refs[n_in + 1:n_in + 1 + n_co], scratch[nk > 1], scratch[(nk > 1) + 1],
                     scratch[(nk > 1) + 2:])
            pl.when(step == 0)(lambda: comm.start(*cargs))
            if comm.mid is not None:
                pl.when(step == (steps * 7) // 8)(lambda: comm.mid(*cargs))
        av = a_ref[...]
        if a_fn is not None:
            av = a_fn(av)
        p = lax.dot_general(av, b_ref[...], dims, preferred_element_type=F32)

        def finish(acc):
            if epi is not None:
                acc = epi(acc, e_ref[...] if has_extra else None)
            o_ref[...] = acc.astype(out_dtype)

        if nk == 1:
            finish(p)
        else:
            acc_ref = scratch[0]
            k = pl.program_id(2)
            _accum(acc_ref, k == 0, p)

            @pl.when(k == nk - 1)
            def _():
                finish(acc_ref[...])

        if comm:
            pl.when(step == steps - 1)(lambda: comm.finish(*cargs))

    ins = [a, b] + ([extra] if has_extra else [])
    in_specs = [a_spec, b_spec] + ([o_spec] if has_extra else [])
    scratch_shapes = [pltpu.VMEM((tm, tn), F32)] if nk > 1 else []
    main = jax.ShapeDtypeStruct((M, N), out_dtype)
    if not comm:
        return pl.pallas_call(
            body, out_shape=main, grid=(nm, nn, nk), in_specs=in_specs, out_specs=o_spec,
            scratch_shapes=scratch_shapes, name=name, compiler_params=_cp("parallel", "parallel", "arbitrary"))(*ins)
    sems = [pltpu.SemaphoreType.DMA((comm.n_sems,)), pltpu.SemaphoreType.DMA((comm.n_sems,))]
    res = pl.pallas_call(
        body, out_shape=(main, *comm.out_shapes), grid=(nm, nn, nk), in_specs=in_specs + [HBM] * n_ci,
        out_specs=(o_spec, *[HBM] * n_co), scratch_shapes=scratch_shapes + sems + comm.scratch, name=name,
        compiler_params=_cp("arbitrary", "arbitrary", "arbitrary"))(*ins, *comm.ins)
    return res[0], list(res[1:])


def _relu2(u):
    r = jnp.maximum(u.astype(F32), 0.0)
    return (r * r).astype(BF16)


def _epi_add(acc, e):
    return acc + e.astype(F32)


def _epi_drelu2(acc, u):
    return acc * (2.0 * jnp.maximum(u.astype(F32), 0.0))


def _rms_fwd(x, g, name):
    T, dm = x.shape
    tm = min(TM, T)

    def body(x_ref, g_ref, h_ref):
        xv = x_ref[...]
        r = lax.rsqrt(jnp.mean(xv * xv, axis=-1, keepdims=True) + EPS)
        h_ref[...] = (xv * r * g_ref[...]).astype(BF16)

    return pl.pallas_call(
        body, out_shape=jax.ShapeDtypeStruct((T, dm), BF16), grid=(T // tm,),
        in_specs=[_rowspec(tm, dm), _full((1, dm))], out_specs=_rowspec(tm, dm), name=name,
        compiler_params=_cp("parallel"))(x, g)


def _rms_bwd(dh, x, g, resid, name):
    T, dm = x.shape
    tm = min(TM, T)

    def body(dh_ref, x_ref, g_ref, res_ref, dx_ref, dxb_ref, dg_ref):
        i = pl.program_id(0)
        xv = x_ref[...]
        r = lax.rsqrt(jnp.mean(xv * xv, axis=-1, keepdims=True) + EPS)
        y = xv * r
        dhv = dh_ref[...].astype(F32)
        dy = dhv * g_ref[...]
        dx = res_ref[...] + r * (dy - y * jnp.mean(dy * y, axis=-1, keepdims=True))
        dx_ref[...] = dx
        dxb_ref[...] = dx.astype(BF16)
        _accum(dg_ref, i == 0, jnp.sum(dhv * y, axis=0, keepdims=True))

    return pl.pallas_call(
        body,
        out_shape=(jax.ShapeDtypeStruct((T, dm), F32), jax.ShapeDtypeStruct((T, dm), BF16),
                   jax.ShapeDtypeStruct((1, dm), F32)),
        grid=(T // tm,),
        in_specs=[_rowspec(tm, dm), _rowspec(tm, dm), _full((1, dm)), _rowspec(tm, dm)],
        out_specs=(_rowspec(tm, dm), _rowspec(tm, dm), _full((1, dm))), name=name,
        compiler_params=_cp("arbitrary"))(dh, x, g, resid)


def _conv_tile(x_ref, halo_ref, w_ref, xpad, tm):
    i = pl.program_id(0)
    halo = halo_ref[...].astype(F32)[8:16]
    xpad[0:8, :] = jnp.where(i > 0, halo, 0.0)
    xpad[8:, :] = x_ref[...].astype(F32)
    w = w_ref[...]
    xc = w[0:1] * xpad[5:5 + tm, :]
    for j in range(1, CONV_TAPS):
        xc = xc + w[j:j + 1] * xpad[5 + j:5 + j + tm, :]
    return xc


def _gate_terms(ab_ref, av_ref):
    abv = ab_ref[...]
    av = av_ref[...]
    pre = abv + av[1:2]
    ea = jnp.exp(av[0:1])
    g = -ea * _softplus(pre)
    return abv, pre, ea, g


def _gdn_pre(proj, conv_w, ab, avec):
    T = proj.shape[0]
    tm = min(G1_TM, T)
    cw = 3 * BR_W

    def body(x_ref, halo_ref, w_ref, ab_ref, av_ref, q_ref, k_ref, v_ref, gb_ref, xpad):
        xc = _conv_tile(x_ref, halo_ref, w_ref, xpad, tm)
        y = xc * _sigmoid(xc)
        for h in range(N_HEAD):
            for off, ref, scale in ((0, q_ref, D_HEAD ** -0.5), (BR_W, k_ref, 1.0)):
                yh = y[:, off + h * D_HEAD:off + (h + 1) * D_HEAD]
                r = lax.rsqrt(jnp.sum(yh * yh, axis=-1, keepdims=True) + EPS)
                ref[:, _hs(h)] = yh * (r * scale)
        v_ref[...] = y[:, 2 * BR_W:]
        abv, _, _, g = _gate_terms(ab_ref, av_ref)
        lane = _iota((tm, LANES), 1)
        gb_ref[...] = jnp.where(lane < N_HEAD, g, jnp.where(lane < 2 * N_HEAD, _sigmoid(abv), 0.0))

    hb = tm // 16
    return pl.pallas_call(
        body,
        out_shape=(jax.ShapeDtypeStruct((T, BR_W), F32),) * 3 + (jax.ShapeDtypeStruct((T, LANES), F32),),
        grid=(T // tm,),
        in_specs=[_rowspec(tm, cw, QKV_COL), pl.BlockSpec((16, cw), lambda i: (jnp.maximum(i * hb - 1, 0), QKV_COL)),
                  _full((CONV_TAPS, cw)), _rowspec(tm, LANES), _full((2, LANES))],
        out_specs=(_rowspec(tm, BR_W),) * 3 + (_rowspec(tm, LANES),),
        scratch_shapes=[pltpu.VMEM((tm + 8, cw), F32)], name="gdn_pre",
        compiler_params=_cp("parallel"))(proj, proj, conv_w, ab, avec)


def _gdn_pre_bwd(proj, conv_w, ab, avec, dq, dk, dv, dgb):
    T = proj.shape[0]
    tm = min(G1_TM, T)
    cw = 3 * BR_W

    def body(x_ref, halo_ref, w_ref, ab_ref, av_ref, dq_ref, dk_ref, dv_ref, dgb_ref,
             dxc_ref, dab_ref, dcw_ref, dav_ref, xpad):
        i = pl.program_id(0)

        @pl.when(i == 0)
        def _():
            dcw_ref[...] = jnp.zeros_like(dcw_ref)
            dav_ref[...] = jnp.zeros_like(dav_ref)

        xc_all = _conv_tile(x_ref, halo_ref, w_ref, xpad, tm)
        for s in range(cw // D_HEAD):
            cs = slice(s * D_HEAD, (s + 1) * D_HEAD)
            xc = xc_all[:, cs]
            sg = _sigmoid(xc)
            yh = xc * sg
            h = s % N_HEAD
            if s < 2 * N_HEAD:
                dref, scale = (dq_ref, D_HEAD ** -0.5) if s < N_HEAD else (dk_ref, 1.0)
                r = lax.rsqrt(jnp.sum(yh * yh, axis=-1, keepdims=True) + EPS)
                yn = yh * r
                dn = dref[:, _hs(h)]
                dy = (scale * r) * (dn - yn * jnp.sum(yn * dn, axis=-1, keepdims=True))
            else:
                dy = dv_ref[:, _hs(h)]
            dxc = dy * (sg * (1.0 + xc * (1.0 - sg)))
            dxc_ref[:, cs] = dxc.astype(BF16)
            for j in range(CONV_TAPS):
                dcw_ref[j:j + 1, cs] += jnp.sum(dxc * xpad[5 + j:5 + j + tm, cs], axis=0, keepdims=True)

        abv, pre, ea, g = _gate_terms(ab_ref, av_ref)
        dgbv = dgb_ref[...]
        lane = _iota((tm, LANES), 1)
        is_a = lane < N_HEAD
        da = jnp.where(is_a, dgbv * (-ea) * _sigmoid(pre), 0.0)
        bs = _sigmoid(abv)
        db = jnp.where(jnp.logical_and(lane >= N_HEAD, lane < 2 * N_HEAD), dgbv * bs * (1.0 - bs), 0.0)
        dab_ref[...] = (da + db).astype(BF16)
        dav_ref[0:1, :] += jnp.sum(jnp.where(is_a, dgbv * g, 0.0), axis=0, keepdims=True)
        dav_ref[1:2, :] += jnp.sum(da, axis=0, keepdims=True)

    hb = tm // 16
    return pl.pallas_call(
        body,
        out_shape=(jax.ShapeDtypeStruct((T, cw), BF16), jax.ShapeDtypeStruct((T, LANES), BF16),
                   jax.ShapeDtypeStruct((CONV_TAPS, cw), F32), jax.ShapeDtypeStruct((2, LANES), F32)),
        grid=(T // tm,),
        in_specs=[_rowspec(tm, cw, QKV_COL), pl.BlockSpec((16, cw), lambda i: (jnp.maximum(i * hb - 1, 0), QKV_COL)),
                  _full((CONV_TAPS, cw)), _rowspec(tm, LANES), _full((2, LANES)),
                  _rowspec(tm, BR_W), _rowspec(tm, BR_W), _rowspec(tm, BR_W), _rowspec(tm, LANES)],
        out_specs=(_rowspec(tm, cw), _rowspec(tm, LANES), _full((CONV_TAPS, cw)), _full((2, LANES))),
        scratch_shapes=[pltpu.VMEM((tm + 8, cw), F32)], name="gdn_pre_bwd",
        compiler_params=_cp("arbitrary"))(proj, proj, conv_w, ab, avec, dq, dk, dv, dgb)


def _conv_bwd(dxc, conv_w, into):
    T, cw = dxc.shape
    tm = min(G1_TM, T)
    nt = T // tm
    hb = tm // 16

    def body(d_ref, halo_ref, w_ref, into_ref, dx_ref, xpad):
        i = pl.program_id(0)
        xpad[0:tm, :] = d_ref[...].astype(F32)
        xpad[tm:, :] = jnp.where(i < nt - 1, halo_ref[...].astype(F32)[0:8], 0.0)
        w = w_ref[...]
        dx = w[3:4] * xpad[0:tm, :]
        for j in range(CONV_TAPS - 1):
            dx = dx + w[j:j + 1] * xpad[3 - j:3 - j + tm, :]
        dx_ref[...] = dx.astype(BF16)

    return pl.pallas_call(
        body, out_shape=jax.ShapeDtypeStruct(into.shape, BF16), grid=(nt,),
        in_specs=[_rowspec(tm, cw), pl.BlockSpec((16, cw), lambda i: (jnp.minimum((i + 1) * hb, T // 16 - 1), 0)),
                  _full((CONV_TAPS, cw)), HBM],
        out_specs=_rowspec(tm, cw, QKV_COL), scratch_shapes=[pltpu.VMEM((tm + 8, cw), F32)],
        input_output_aliases={3: 0}, name="conv_bwd", compiler_params=_cp("parallel"))(dxc, dxc, conv_w, into)


def _chunk_consts():
    C = GDN_CHUNK
    row, col = _iota((C, C), 0), _iota((C, C), 1)
    return row, col, row >= col, row > col


def _chunk_decay(gbv, incl):
    c_all = _dotf(incl.astype(F32), gbv)
    c_t = jnp.concatenate([c_all, jnp.zeros_like(c_all)], axis=0).T[:, :GDN_CHUNK]
    return c_all, c_t


def _head_decay(c_all, c_t, gbv, incl, h):
    C = GDN_CHUNK
    c_col = c_all[:, h:h + 1]
    c_row = c_t[h:h + 1, :]
    gam = jnp.exp(jnp.where(incl, c_col - c_row, -1e30))
    c_last = c_all[C - 1:C, h:h + 1]
    return gam, jnp.exp(c_col), jnp.exp(c_last - c_col), jnp.exp(c_last), gbv[:, N_HEAD + h:N_HEAD + h + 1]


def _split_bf16(x):
    hi = x.astype(BF16)
    return hi, (x - hi.astype(F32)).astype(BF16)


def _dot3(a, b):
    ah, al = _split_bf16(a)
    bh, bl = _split_bf16(b)
    d = lambda u, v: lax.dot_general(u, v, _NN, preferred_element_type=F32)
    return d(ah, bh) + (d(ah, bl) + d(al, bh))


def _unit_lower_inverses(ms, row, col):
    bi, bj = row // INV_BLOCK, col // INV_BLOCK
    eye = (row == col).astype(F32)
    ns = [jnp.where(bi == bj, -m, 0.0) for m in ms]
    invs = [eye + n for n in ns]
    size = 2
    while size < INV_BLOCK:
        ns = [_dot3(n, n) for n in ns]
        invs = [inv + _dot3(inv, n) for inv, n in zip(invs, ns)]
        size *= 2
    width = 2
    while width * INV_BLOCK <= GDN_CHUNK:
        sel = jnp.logical_and(bi // width == bj // width, bi // (width // 2) > bj // (width // 2))
        ts = [_dot3(inv, jnp.where(sel, m, 0.0)) for inv, m in zip(invs, ms)]
        invs = [inv - _dot3(t, inv) for inv, t in zip(invs, ts)]
        width *= 2
    return invs


def _gdn_inv(k, gb):
    T = k.shape[0]
    C = GDN_CHUNK
    per = min(INV_CHUNKS, T // C)
    rows = per * C

    def body(k_ref, gb_ref, ti_ref, tt_ref):
        row, col, incl, strict = _chunk_consts()
        ms = []
        for ci in range(per):
            rs = slice(ci * C, (ci + 1) * C)
            gbv = gb_ref[rs, :]
            c_all, c_t = _chunk_decay(gbv, incl)
            for h in range(N_HEAD):
                gam, _, _, _, bcol = _head_decay(c_all, c_t, gbv, incl, h)
                K = k_ref[rs, _hs(h)]
                ms.append(jnp.where(strict, _dot_nt(K * bcol, K) * gam, 0.0))
        eye = (row == col).astype(BF16)
        for i, inv in enumerate(_unit_lower_inverses(ms, row, col)):
            ti_ref[i // N_HEAD, i % N_HEAD] = inv
            tt_ref[i // N_HEAD, i % N_HEAD] = _dot_tn(inv, eye).astype(BF16)

    spec = pl.BlockSpec((per, N_HEAD, C, C), lambda i: (i, 0, 0, 0))
    return pl.pallas_call(
        body, out_shape=(jax.ShapeDtypeStruct((T // C, N_HEAD, C, C), F32),
                         jax.ShapeDtypeStruct((T // C, N_HEAD, C, C), BF16)),
        grid=(T // rows,), in_specs=[_rowspec(rows, BR_W), _rowspec(rows, LANES)], out_specs=(spec, spec),
        name="gdn_inv", compiler_params=_cp("parallel"))(k, gb)


def _gdn_fwd(q, k, v, gb, proj, gnorm, tinv_all):
    T = q.shape[0]
    C = GDN_CHUNK
    nc = T // C
    per = min(GDN_STEP_CHUNKS, nc)
    zcol = Z_OFF // BR_W
    heads = range(N_HEAD)

    def body(q_ref, k_ref, v_ref, gb_ref, z_ref, gn_ref, ti_ref, og_ref, oraw_ref, sh_ref, vn_ref, s_ref):
        @pl.when(pl.program_id(0) == 0)
        def _():
            s_ref[...] = jnp.zeros_like(s_ref)

        _, _, incl, _ = _chunk_consts()
        S = [s_ref[h] for h in heads]
        for ci in range(per):
            rs = slice(ci * C, (ci + 1) * C)
            gbv = gb_ref[rs, :]
            c_all, c_t = _chunk_decay(gbv, incl)
            dec = [_head_decay(c_all, c_t, gbv, incl, h) for h in heads]
            gam, gcol, dcol, glast, bcol = ([d[i] for d in dec] for i in range(5))
            Q = [q_ref[rs, _hs(h)] for h in heads]
            K = [k_ref[rs, _hs(h)] for h in heads]
            V = [v_ref[rs, _hs(h)] for h in heads]
            Sb = [s.astype(BF16) for s in S]
            KS = [_dot(K[h], Sb[h]) for h in heads]
            QS = [_dot(Q[h], Sb[h]) for h in heads]
            P = [_dot_nt(Q[h], K[h]) * gam[h] for h in heads]
            R = [bcol[h] * (V[h] - gcol[h] * KS[h]) for h in heads]
            vn = [_dot(ti_ref[ci, h], R[h]) for h in heads]
            O = [gcol[h] * QS[h] + _dot(P[h], vn[h]) for h in heads]
            Sn = [glast[h] * S[h] + _dot_tn(K[h] * dcol[h], vn[h]) for h in heads]
            for h in heads:
                sh_ref[ci, h] = S[h]
                vn_ref[rs, _hs(h)] = vn[h]
                oraw_ref[rs, _hs(h)] = O[h]
                rr = lax.rsqrt(jnp.mean(O[h] * O[h], axis=-1, keepdims=True) + EPS)
                zz = z_ref[rs, _hs(h)].astype(F32)
                og_ref[rs, _hs(h)] = (O[h] * rr * gn_ref[...] * (zz * _sigmoid(zz))).astype(BF16)
            S = Sn
        for h in heads:
            s_ref[h] = S[h]

    cspec = lambda w, cb=0: pl.BlockSpec((per * C, w), lambda n: (n, cb))
    hist = lambda a, b: pl.BlockSpec((per, N_HEAD, a, b), lambda n: (n, 0, 0, 0))
    return pl.pallas_call(
        body,
        out_shape=(jax.ShapeDtypeStruct((T, BR_W), BF16), jax.ShapeDtypeStruct((T, BR_W), F32),
                   jax.ShapeDtypeStruct((nc, N_HEAD, D_HEAD, D_HEAD), F32), jax.ShapeDtypeStruct((T, BR_W), F32)),
        grid=(nc // per,),
        in_specs=[cspec(BR_W), cspec(BR_W), cspec(BR_W), cspec(LANES), cspec(BR_W, zcol), _full((1, D_HEAD)),
                  hist(C, C)],
        out_specs=(cspec(BR_W), cspec(BR_W), hist(D_HEAD, D_HEAD), cspec(BR_W)),
        scratch_shapes=[pltpu.VMEM((N_HEAD, D_HEAD, D_HEAD), F32)], name="gdn_chunk_fwd",
        compiler_params=_cp("arbitrary"))(q, k, v, gb, proj, gnorm, tinv_all)


def _gdn_bwd(q, k, v, gb, proj, gnorm, oraw, shist, tinv_all, vn_all, dog, into):
    T = q.shape[0]
    C = GDN_CHUNK
    nc = T // C
    per = min(GDN_BWD_STEP_CHUNKS, nc)
    zcol = Z_OFF // BR_W

    def body(q_ref, k_ref, v_ref, gb_ref, z_ref, gn_ref, oraw_ref, sh_ref, tt_ref, vn_ref, dog_ref, into_ref,
             dq_ref, dk_ref, dv_ref, dgb_ref, dz_ref, dgn_ref, ds_ref):
        @pl.when(pl.program_id(0) == 0)
        def _():
            ds_ref[...] = jnp.zeros_like(ds_ref)
            dgn_ref[...] = jnp.zeros_like(dgn_ref)

        row, col, incl, strict = _chunk_consts()
        lane = _iota((C, LANES), 1)
        rowl = _iota((C, LANES), 0)
        eye = (row == col).astype(F32)
        upper = (col >= row).astype(F32)
        gn = gn_ref[...]
        heads = range(N_HEAD)
        rsum = lambda a: jnp.sum(a, axis=-1, keepdims=True)
        dgn = jnp.zeros((1, D_HEAD), F32)
        dSn = [ds_ref[h] for h in heads]
        for ci in reversed(range(per)):
            rs = slice(ci * C, (ci + 1) * C)
            gbv = gb_ref[rs, :]
            c_all, c_t = _chunk_decay(gbv, incl)
            dec = [_head_decay(c_all, c_t, gbv, incl, h) for h in heads]
            gam, gcol, dcol, glast, bcol = ([d[i] for d in dec] for i in range(5))
            Q = [q_ref[rs, _hs(h)] for h in heads]
            K = [k_ref[rs, _hs(h)] for h in heads]
            V = [v_ref[rs, _hs(h)] for h in heads]
            dO = []
            for h in heads:
                O = oraw_ref[rs, _hs(h)]
                zz = z_ref[rs, _hs(h)].astype(F32)
                dogv = dog_ref[rs, _hs(h)].astype(F32)
                rr = lax.rsqrt(jnp.mean(O * O, axis=-1, keepdims=True) + EPS)
                on = O * rr
                sg = _sigmoid(zz)
                dz_ref[rs, _hs(h)] = (dogv * on * gn * (sg * (1.0 + zz * (1.0 - sg)))).astype(BF16)
                dyn = dogv * (zz * sg)
                dgn = dgn + jnp.sum(dyn * on, axis=0, keepdims=True)
                dyv = dyn * gn
                dO.append((rr * (dyv - on * jnp.mean(dyv * on, axis=-1, keepdims=True))).astype(BF16))
            S = [sh_ref[ci, h] for h in heads]
            Sb = [s.astype(BF16) for s in S]
            vn = [vn_ref[rs, _hs(h)] for h in heads]
            vnb = [a.astype(BF16) for a in vn]
            dSb = [a.astype(BF16) for a in dSn]
            Kb = [K[h] * bcol[h] for h in heads]
            gam_t = [jnp.exp(jnp.where(col >= row, c_t[h:h + 1, :] - c_all[:, h:h + 1], -1e30)) for h in heads]
            M = [jnp.where(strict, _dot_nt(Kb[h], K[h]) * gam[h], 0.0) for h in heads]
            P = [_dot_nt(Q[h], K[h]) * gam[h] for h in heads]
            P_t = [_dot_nt(K[h], Q[h]) * gam_t[h] for h in heads]
            KS = [_dot(K[h], Sb[h]) for h in heads]
            QS = [_dot(Q[h], Sb[h]) for h in heads]
            dvn = [_dot(P_t[h], dO[h]) + _dot(K[h] * dcol[h], dSb[h]) for h in heads]
            dR = [_dot(tt_ref[ci, h], dvn[h]) for h in heads]
            dRb = [a.astype(BF16) for a in dR]
            bg = [bcol[h] * gcol[h] for h in heads]
            dS_new = [glast[h] * dSn[h] + _dot_tn(gcol[h] * Q[h], dO[h]) - _dot_tn(bg[h] * K[h], dRb[h])
                      for h in heads]
            dP = [jnp.where(incl, _dot_nt(dO[h], vnb[h]), 0.0) for h in heads]
            dM = [jnp.where(strict, -_dot_nt(dRb[h], vnb[h]), 0.0) for h in heads]
            dPG = [(dP[h] * gam[h]).astype(BF16) for h in heads]
            dMG = [(dM[h] * gam[h]).astype(BF16) for h in heads]
            dPG_t = [(jnp.where(col >= row, _dot_nt(vnb[h], dO[h]), 0.0) * gam_t[h]).astype(BF16) for h in heads]
            dMG_t = [(jnp.where(col > row, -_dot_nt(vnb[h], dRb[h]), 0.0) * gam_t[h]).astype(BF16) for h in heads]
            E = [_dot_nt(vnb[h], dSb[h]) for h in heads]
            dKb = [_dot(dMG[h], K[h]) for h in heads]
            dc_all = jnp.zeros((C, LANES), F32)
            db_all = jnp.zeros((C, LANES), F32)
            for h in heads:
                dq_ref[rs, _hs(h)] = gcol[h] * _dot_nt(dO[h], Sb[h]) + _dot(dPG[h], K[h])
                dk_ref[rs, _hs(h)] = (_dot(dPG_t[h], Q[h]) + _dot(dMG_t[h], Kb[h]) + bcol[h] * dKb[h]
                                      - bg[h] * _dot_nt(dRb[h], Sb[h]) + dcol[h] * E[h])
                dv_ref[rs, _hs(h)] = bcol[h] * dR[h]
                dbeta = rsum(dKb[h] * K[h]) + rsum(dR[h] * (V[h] - gcol[h] * KS[h]))
                X = dP[h] * P[h] + dM[h] * M[h]
                ddel = rsum(K[h] * E[h]) * dcol[h]
                colsum = rsum(eye * jnp.sum(X, axis=0, keepdims=True))
                dc = (rsum(X) - colsum + gcol[h] * rsum(dO[h].astype(F32) * QS[h]) - bg[h] * rsum(dR[h] * KS[h])
                      - ddel)
                last = (jnp.sum(ddel, axis=0, keepdims=True)
                        + glast[h] * jnp.sum(rsum(dSn[h] * S[h]), axis=0, keepdims=True))
                dc_all = dc_all + jnp.where(lane == h, dc + jnp.where(rowl == C - 1, last, 0.0), 0.0)
                db_all = db_all + jnp.where(lane == N_HEAD + h, dbeta, 0.0)
            dgb_ref[rs, :] = _dotf(upper, dc_all) + db_all
            dSn = dS_new
        for h in heads:
            ds_ref[h] = dSn[h]
        dgn_ref[...] += dgn

    nb = nc // per
    cspec = lambda w, cb=0: pl.BlockSpec((per * C, w), lambda n: (nb - 1 - n, cb))
    hist = lambda a, b: pl.BlockSpec((per, N_HEAD, a, b), lambda n: (nb - 1 - n, 0, 0, 0))
    return pl.pallas_call(
        body,
        out_shape=(jax.ShapeDtypeStruct((T, BR_W), F32),) * 3 + (
            jax.ShapeDtypeStruct((T, LANES), F32), jax.ShapeDtypeStruct(into.shape, BF16),
            jax.ShapeDtypeStruct((1, D_HEAD), F32)),
        grid=(nb,),
        in_specs=[cspec(BR_W), cspec(BR_W), cspec(BR_W), cspec(LANES), cspec(BR_W, zcol), _full((1, D_HEAD)),
                  cspec(BR_W), hist(D_HEAD, D_HEAD), hist(C, C), cspec(BR_W), cspec(BR_W), HBM],
        out_specs=(cspec(BR_W), cspec(BR_W), cspec(BR_W), cspec(LANES), cspec(BR_W, zcol), _full((1, D_HEAD))),
        scratch_shapes=[pltpu.VMEM((N_HEAD, D_HEAD, D_HEAD), F32)], input_output_aliases={11: 4},
        name="gdn_chunk_bwd",
        compiler_params=_cp("arbitrary"))(q, k, v, gb, proj, gnorm, oraw, shist, tinv_all, vn_all, dog, into)


SB_COL = SB_OFF // BR_W
SB_SCALE = D_HEAD ** -0.5


def _sb_pre(proj, gq, gk):
    T = proj.shape[0]
    tm = min(TM, T)

    def body(xq_ref, xk_ref, xv_ref, gq_ref, gk_ref, q_ref, k_ref, v_ref):
        for h in range(N_HEAD):
            for x_ref, g_ref, ref, scale in ((xq_ref, gq_ref, q_ref, SB_SCALE), (xk_ref, gk_ref, k_ref, 1.0)):
                xh = x_ref[:, _hs(h)].astype(F32)
                r = lax.rsqrt(jnp.mean(xh * xh, axis=-1, keepdims=True) + EPS)
                ref[:, _hs(h)] = (xh * (r * scale) * g_ref[...]).astype(BF16)
        v_ref[...] = xv_ref[...]

    return pl.pallas_call(
        body, out_shape=(jax.ShapeDtypeStruct((T, BR_W), BF16),) * 3, grid=(T // tm,),
        in_specs=[_rowspec(tm, BR_W, SB_COL), _rowspec(tm, BR_W, SB_COL + 1), _rowspec(tm, BR_W, SB_COL + 2),
                  _full((1, D_HEAD)), _full((1, D_HEAD))],
        out_specs=(_rowspec(tm, BR_W),) * 3, name="sb_pre", compiler_params=_cp("parallel"))(proj, proj, proj, gq, gk)


def _sb_pre_bwd(proj, gq, gk, dq, dk, dv, into):
    T = proj.shape[0]
    tm = min(TM, T)

    def body(xq_ref, xk_ref, gq_ref, gk_ref, dq_ref, dk_ref, dv_ref, into_ref, dx_ref, dgq_ref, dgk_ref):
        i = pl.program_id(0)

        @pl.when(i == 0)
        def _():
            dgq_ref[...] = jnp.zeros_like(dgq_ref)
            dgk_ref[...] = jnp.zeros_like(dgk_ref)

        for off, x_ref, g_ref, d_ref, dg_ref, scale in ((0, xq_ref, gq_ref, dq_ref, dgq_ref, SB_SCALE),
                                                        (BR_W, xk_ref, gk_ref, dk_ref, dgk_ref, 1.0)):
            dg = jnp.zeros((1, D_HEAD), F32)
            for h in range(N_HEAD):
                xh = x_ref[:, _hs(h)].astype(F32)
                r = lax.rsqrt(jnp.mean(xh * xh, axis=-1, keepdims=True) + EPS)
                y = xh * r
                dn = d_ref[:, _hs(h)] * scale
                dg = dg + jnp.sum(dn * y, axis=0, keepdims=True)
                dy = dn * g_ref[...]
                dx_ref[:, off + h * D_HEAD:off + (h + 1) * D_HEAD] = (
                    r * (dy - y * jnp.mean(dy * y, axis=-1, keepdims=True))).astype(BF16)
            dg_ref[...] += dg
        dx_ref[:, 2 * BR_W:] = dv_ref[...].astype(BF16)

    return pl.pallas_call(
        body,
        out_shape=(jax.ShapeDtypeStruct(into.shape, BF16), jax.ShapeDtypeStruct((1, D_HEAD), F32),
                   jax.ShapeDtypeStruct((1, D_HEAD), F32)),
        grid=(T // tm,),
        in_specs=[_rowspec(tm, BR_W, SB_COL), _rowspec(tm, BR_W, SB_COL + 1), _full((1, D_HEAD)), _full((1, D_HEAD)),
                  _rowspec(tm, BR_W), _rowspec(tm, BR_W), _rowspec(tm, BR_W), HBM],
        out_specs=(_rowspec(tm, 3 * BR_W, SB_OFF // (3 * BR_W)), _full((1, D_HEAD)), _full((1, D_HEAD))),
        input_output_aliases={7: 0}, name="sb_pre_bwd",
        compiler_params=_cp("arbitrary"))(proj, proj, gq, gk, dq, dk, dv, into)


def _sb_pair(q, k, masked):
    z = _dot_nt(q, k)
    zc = jnp.minimum(z, 30.0)
    sp = jnp.log(1.0 + jnp.exp(zc)) + (z - zc)
    if not masked:
        return z, sp, None
    mask = _iota(z.shape, 1) < _iota(z.shape, 0)
    return z, jnp.where(mask, sp, 0.0), mask


def _sb_fwd(sq, sk, sv):
    T = sq.shape[0]
    blk = min(SB_BLK, T)
    w = blk // 2
    nb = T // blk

    def body(q_ref, k_ref, v_ref, o_ref, lt_ref, cut_ref, acc_ref, r_ref):
        head, qi = pl.program_id(0), pl.program_id(1)
        acc_ref[...] = jnp.zeros_like(acc_ref)
        r_ref[...] = jnp.zeros_like(r_ref)
        after = (_iota((w, w), 0) > _iota((w, w), 1)).astype(BF16)

        def block(rows, kb, masked):
            keys = pl.ds(pl.multiple_of(kb * w, w), w)
            z, sp, mask = _sb_pair(q_ref[rows, :], k_ref[keys, :], masked)
            r = r_ref[rows, :]
            a = jnp.exp(z - sp - _dot(sp, after) - r)
            if masked:
                a = jnp.where(mask, a, 0.0)
            acc_ref[rows, :] += _dot(a, v_ref[keys, :])
            r_ref[rows, :] = r + jnp.sum(sp, axis=-1, keepdims=True)

        def alive():
            return jnp.min(r_ref[...]) < SB_DEAD

        def further(state):
            kb, _ = state
            block(slice(0, blk), kb, False)
            return kb - 1, alive()

        block(slice(w, blk), 2 * qi + 1, True)
        block(slice(0, blk), 2 * qi, True)
        left, _ = lax.while_loop(lambda s: jnp.logical_and(s[0] >= 0, s[1]), further, (2 * qi - 1, alive()))
        o_ref[...] = acc_ref[...].astype(BF16)
        lt_ref[0] = r_ref[...]
        cut_ref[head, qi] = (left + 1).astype(F32)

    qspec = pl.BlockSpec((blk, D_HEAD), lambda h, i: (i, h))
    whole = pl.BlockSpec((T, D_HEAD), lambda h, i: (0, h))
    return pl.pallas_call(
        body,
        out_shape=(jax.ShapeDtypeStruct((T, BR_W), BF16), jax.ShapeDtypeStruct((N_HEAD, T, 1), F32),
                   jax.ShapeDtypeStruct((N_HEAD, nb), F32)),
        grid=(N_HEAD, nb), in_specs=[qspec, whole, whole],
        out_specs=(qspec, pl.BlockSpec((1, blk, 1), lambda h, i: (h, i, 0)), pl.BlockSpec(memory_space=pltpu.SMEM)),
        scratch_shapes=[pltpu.VMEM((blk, D_HEAD), F32), pltpu.VMEM((blk, 1), F32)],
        name="sb_fwd", compiler_params=_cp("arbitrary", "arbitrary"))(sq, sk, sv)


def _sb_bwd(sq, sk, sv, ltot, do, cut):
    T = sq.shape[0]
    blk = min(SB_BLK, T)
    w = blk // 2
    nb = T // blk

    def body(q_ref, k_ref, v_ref, lt_ref, do_ref, cut_ref, dq_ref, dk_ref, dv_ref, acc_ref, p_ref, g_ref):
        head, qi = pl.program_id(0), pl.program_id(1)
        first = cut_ref[head, qi].astype(jnp.int32)

        @pl.when(qi == 0)
        def _():
            dk_ref[...] = jnp.zeros_like(dk_ref)
            dv_ref[...] = jnp.zeros_like(dv_ref)

        acc_ref[...] = jnp.zeros_like(acc_ref)
        p_ref[...] = lt_ref[0]
        g_ref[...] = jnp.zeros_like(g_ref)
        after = (_iota((w, w), 0) > _iota((w, w), 1)).astype(BF16)
        before = (_iota((w, w), 0) < _iota((w, w), 1)).astype(BF16)

        def block(rows, kb, masked):
            keys = pl.ds(pl.multiple_of(kb * w, w), w)
            q, do = q_ref[rows, :], do_ref[rows, :]
            z, sp, mask = _sb_pair(q, k_ref[keys, :], masked)
            d_a = _dot_nt(do, v_ref[keys, :])
            rest = p_ref[rows, :] - jnp.sum(sp, axis=-1, keepdims=True)
            a = jnp.exp(z - sp - _dot(sp, after) - rest)
            if masked:
                a = jnp.where(mask, a, 0.0)
            g = a * d_a
            sig = jnp.exp(z - sp)
            dz = g - sig * (g + (g_ref[rows, :] + _dot(g, before)))
            if masked:
                dz = jnp.where(mask, dz, 0.0)
            dz = dz.astype(BF16)
            dv_ref[keys, :] += _dot_tn(a, do)
            dk_ref[keys, :] += _dot_tn(dz, q)
            acc_ref[rows, :] += _dot(dz, k_ref[keys, :])
            p_ref[rows, :] = rest
            g_ref[rows, :] += jnp.sum(g, axis=-1, keepdims=True)

        def step(kb, carry):
            block(slice(0, blk), kb, False)
            return carry

        lax.fori_loop(first, 2 * qi, step, 0)
        block(slice(0, blk), 2 * qi, True)
        block(slice(w, blk), 2 * qi + 1, True)
        dq_ref[...] = acc_ref[...]

    qspec = pl.BlockSpec((blk, D_HEAD), lambda h, i: (i, h))
    whole = pl.BlockSpec((T, D_HEAD), lambda h, i: (0, h))
    return pl.pallas_call(
        body, out_shape=(jax.ShapeDtypeStruct((T, BR_W), F32),) * 3, grid=(N_HEAD, nb),
        in_specs=[qspec, whole, whole, pl.BlockSpec((1, blk, 1), lambda h, i: (h, i, 0)), qspec,
                  pl.BlockSpec(memory_space=pltpu.SMEM)],
        out_specs=(qspec, whole, whole),
        scratch_shapes=[pltpu.VMEM((blk, D_HEAD), F32), pltpu.VMEM((blk, 1), F32), pltpu.VMEM((blk, 1), F32)],
        name="sb_bwd", compiler_params=_cp("arbitrary", "arbitrary"))(sq, sk, sv, ltot, do, cut)


def _mem_kv(mem, gm, w_kv, gk):
    def body(mem_ref, gm_ref, w_ref, gk_ref, mn_ref, kv_ref, kh_ref, vm_ref):
        mv = mem_ref[...]
        r = lax.rsqrt(jnp.mean(mv * mv, axis=-1, keepdims=True) + EPS)
        mn = (mv * r * gm_ref[...]).astype(BF16)
        mn_ref[...] = mn
        kv = lax.dot_general(mn, w_ref[...], _NN, preferred_element_type=F32)
        kv_ref[...] = kv
        for h in range(N_HEAD):
            kh = kv[:, _hs(h)]
            rk = lax.rsqrt(jnp.mean(kh * kh, axis=-1, keepdims=True) + EPS)
            kh_ref[:, _hs(h)] = (kh * rk * gk_ref[...]).astype(BF16)
        vm_ref[...] = kv[:, BR_W:].astype(BF16)

    return pl.pallas_call(
        body,
        out_shape=(jax.ShapeDtypeStruct((N_MEM, D_MODEL), BF16), jax.ShapeDtypeStruct((N_MEM, 2 * BR_W), F32),
                   jax.ShapeDtypeStruct((N_MEM, BR_W), BF16), jax.ShapeDtypeStruct((N_MEM, BR_W), BF16)),
        name="mem_kv", compiler_params=_cp())(mem, gm, w_kv, gk)


def _mem_q(x_ref, gq_ref, h):
    xh = x_ref[:, _hs(h)].astype(F32)
    r = lax.rsqrt(jnp.mean(xh * xh, axis=-1, keepdims=True) + EPS)
    return r, xh * r


def _mem_probs(qn, kh):
    s = _dot_nt(qn, kh) * (D_HEAD ** -0.5)
    e = jnp.exp(s - jnp.max(s, axis=-1, keepdims=True))
    return e / jnp.sum(e, axis=-1, keepdims=True)


def _mem_fwd(proj, kh, vm, gq):
    T = proj.shape[0]
    tm = min(TM, T)

    def body(x_ref, kh_ref, vm_ref, gq_ref, o_ref):
        for h in range(N_HEAD):
            _, y = _mem_q(x_ref, gq_ref, h)
            p = _mem_probs((y * gq_ref[...]).astype(BF16), kh_ref[:, _hs(h)])
            o_ref[:, _hs(h)] = _dot(p, vm_ref[:, _hs(h)]).astype(BF16)

    return pl.pallas_call(
        body, out_shape=jax.ShapeDtypeStruct((T, BR_W), BF16), grid=(T // tm,),
        in_specs=[_rowspec(tm, BR_W, MEMQ_OFF // BR_W), _full((N_MEM, BR_W)), _full((N_MEM, BR_W)),
                  _full((1, D_HEAD))],
        out_specs=_rowspec(tm, BR_W), name="mem_fwd", compiler_params=_cp("parallel"))(proj, kh, vm, gq)


def _mem_bwd(proj, kh, vm, gq, do, into):
    T = proj.shape[0]
    tm = min(TM, T)

    def body(x_ref, kh_ref, vm_ref, gq_ref, do_ref, into_ref, dx_ref, dkh_ref, dvm_ref, dgq_ref):
        i = pl.program_id(0)

        @pl.when(i == 0)
        def _():
            dkh_ref[...] = jnp.zeros_like(dkh_ref)
            dvm_ref[...] = jnp.zeros_like(dvm_ref)
            dgq_ref[...] = jnp.zeros_like(dgq_ref)

        dg = jnp.zeros((1, D_HEAD), F32)
        for h in range(N_HEAD):
            r, y = _mem_q(x_ref, gq_ref, h)
            qn = (y * gq_ref[...]).astype(BF16)
            p = _mem_probs(qn, kh_ref[:, _hs(h)])
            dov = do_ref[:, _hs(h)]
            dp = _dot_nt(dov, vm_ref[:, _hs(h)])
            ds = p * (dp - jnp.sum(dp * p, axis=-1, keepdims=True)) * (D_HEAD ** -0.5)
            dqn = _dot(ds, kh_ref[:, _hs(h)])
            dkh_ref[:, _hs(h)] += _dot_tn(ds, qn)
            dvm_ref[:, _hs(h)] += _dot_tn(p, dov)
            dg = dg + jnp.sum(dqn * y, axis=0, keepdims=True)
            dy = dqn * gq_ref[...]
            dx_ref[:, _hs(h)] = (r * (dy - y * jnp.mean(dy * y, axis=-1, keepdims=True))).astype(BF16)
        dgq_ref[...] += dg

    return pl.pallas_call(
        body,
        out_shape=(jax.ShapeDtypeStruct(into.shape, BF16), jax.ShapeDtypeStruct((N_MEM, BR_W), F32),
                   jax.ShapeDtypeStruct((N_MEM, BR_W), F32), jax.ShapeDtypeStruct((1, D_HEAD), F32)),
        grid=(T // tm,),
        in_specs=[_rowspec(tm, BR_W, MEMQ_OFF // BR_W), _full((N_MEM, BR_W)), _full((N_MEM, BR_W)),
                  _full((1, D_HEAD)), _rowspec(tm, BR_W), HBM],
        out_specs=(_rowspec(tm, BR_W, MEMQ_OFF // BR_W), _full((N_MEM, BR_W)), _full((N_MEM, BR_W)),
                   _full((1, D_HEAD))),
        input_output_aliases={5: 0}, name="mem_bwd", compiler_params=_cp("arbitrary"))(proj, kh, vm, gq, do, into)


def _mem_kv_bwd(mem, gm, w_kv, gk, kv, mn, dkh, dvm):
    def body(mem_ref, gm_ref, w_ref, gk_ref, kv_ref, mn_ref, dkh_ref, dvm_ref, dw_ref, dgm_ref, dgk_ref, dkv_ref):
        dgk = jnp.zeros((1, D_HEAD), F32)
        for h in range(N_HEAD):
            kh = kv_ref[:, _hs(h)]
            r = lax.rsqrt(jnp.mean(kh * kh, axis=-1, keepdims=True) + EPS)
            y = kh * r
            dn = dkh_ref[:, _hs(h)]
            dgk = dgk + jnp.sum(dn * y, axis=0, keepdims=True)
            dy = dn * gk_ref[...]
            dkv_ref[:, _hs(h)] = (r * (dy - y * jnp.mean(dy * y, axis=-1, keepdims=True))).astype(BF16)
        dkv_ref[:, BR_W:] = dvm_ref[...].astype(BF16)
        dgk_ref[...] = dgk
        dkv = dkv_ref[...]
        dw_ref[...] = lax.dot_general(mn_ref[...], dkv, _TN, preferred_element_type=F32)
        dmn = lax.dot_general(dkv, w_ref[...], _NT, preferred_element_type=F32)
        mv = mem_ref[...]
        memn = mv * lax.rsqrt(jnp.mean(mv * mv, axis=-1, keepdims=True) + EPS)
        dgm_ref[...] = jnp.sum(dmn * memn, axis=0, keepdims=True)

    return pl.pallas_call(
        body,
        out_shape=(jax.ShapeDtypeStruct((D_MODEL, 2 * BR_W), F32), jax.ShapeDtypeStruct((1, D_MODEL), F32),
                   jax.ShapeDtypeStruct((1, D_HEAD), F32)),
        scratch_shapes=[pltpu.VMEM((N_MEM, 2 * BR_W), BF16)], name="mem_kv_bwd",
        compiler_params=_cp())(mem, gm, w_kv, gk, kv, mn, dkh, dvm)


def _merge_fwd(og, osb, om, proj, wg, ws, wm):
    T = og.shape[0]
    tm = min(TM, T)

    def body(og_ref, os_ref, om_ref, g0, g1, g2, wg_ref, ws_ref, wm_ref, mix_ref, yg_ref, ys_ref, ym_ref):
        mix = jnp.zeros((tm, D_MODEL), F32)
        for o_ref, gl_ref, w_ref, y_ref in ((og_ref, g0, wg_ref, yg_ref), (os_ref, g1, ws_ref, ys_ref),
                                            (om_ref, g2, wm_ref, ym_ref)):
            y = lax.dot_general(o_ref[...], w_ref[...], _NN, preferred_element_type=F32)
            y_ref[...] = y.astype(BF16)
            mix = mix + _sigmoid(gl_ref[...].astype(F32)) * y
        mix_ref[...] = mix.astype(BF16)

    br = _rowspec(tm, BR_W)
    wspec = _full((BR_W, D_MODEL))
    out = _rowspec(tm, D_MODEL)
    gates = [_rowspec(tm, D_MODEL, GATE_COL + b) for b in range(3)]
    return pl.pallas_call(
        body, out_shape=(jax.ShapeDtypeStruct((T, D_MODEL), BF16),) * 4, grid=(T // tm,),
        in_specs=[br, br, br, *gates, wspec, wspec, wspec],
        out_specs=(out,) * 4, name="merge_fwd",
        compiler_params=_cp("parallel"))(og, osb, om, proj, proj, proj, wg, ws, wm)


def _merge_bwd(dmix, proj, ys, os_, ws):
    T = dmix.shape[0]
    tm = min(TM, T)

    def body(dmix_ref, g0, g1, g2, y0, y1, y2, o0, o1, o2, w0, w1, w2, dgl_ref, do0, do1, do2, dw0, dw1, dw2):
        i = pl.program_id(0)
        dm = dmix_ref[...].astype(F32)
        for b, (gl_ref, y_ref, o_ref, w_ref, do_ref, dw_ref) in enumerate((
                (g0, y0, o0, w0, do0, dw0), (g1, y1, o1, w1, do1, dw1), (g2, y2, o2, w2, do2, dw2))):
            gate = _sigmoid(gl_ref[...].astype(F32))
            dgl_ref[:, b * D_MODEL:(b + 1) * D_MODEL] = (dm * y_ref[...].astype(F32) * gate * (1.0 - gate)).astype(BF16)
            dy = (gate * dm).astype(BF16)
            do_ref[...] = lax.dot_general(dy, w_ref[...], _NT, preferred_element_type=F32).astype(BF16)
            _accum(dw_ref, i == 0, lax.dot_general(dy, o_ref[...], _TN, preferred_element_type=F32))

    br = _rowspec(tm, BR_W)
    wide = _rowspec(tm, D_MODEL)
    wspec = _full((BR_W, D_MODEL))
    wtspec = _full((D_MODEL, BR_W))
    gates = [_rowspec(tm, D_MODEL, GATE_COL + b) for b in range(3)]
    return pl.pallas_call(
        body,
        out_shape=(jax.ShapeDtypeStruct((T, PROJ_W), BF16),) + (jax.ShapeDtypeStruct((T, BR_W), BF16),) * 3
        + (jax.ShapeDtypeStruct((D_MODEL, BR_W), F32),) * 3,
        grid=(T // tm,),
        in_specs=[wide, *gates, wide, wide, wide, br, br, br, wspec, wspec, wspec],
        out_specs=(_rowspec(tm, 3 * D_MODEL, GATE_OFF // (3 * D_MODEL)), br, br, br, wtspec, wtspec, wtspec),
        name="merge_bwd",
        compiler_params=_cp("arbitrary"))(dmix, proj, proj, proj, *ys, *os_, *ws)


def _loss(y, tgt):
    T, dm = y.shape
    tm = min(TM, T)

    def body(y_ref, t_ref, dy_ref, dyb_ref, sq_ref):
        err = y_ref[...] - t_ref[...]
        dy = err * (1.0 / dm)
        dy_ref[...] = dy
        dyb_ref[...] = dy.astype(BF16)
        _accum(sq_ref, pl.program_id(0) == 0, jnp.sum(err * err, axis=0, keepdims=True))

    return pl.pallas_call(
        body,
        out_shape=(jax.ShapeDtypeStruct((T, dm), F32), jax.ShapeDtypeStruct((T, dm), BF16),
                   jax.ShapeDtypeStruct((1, dm), F32)),
        grid=(T // tm,), in_specs=[_rowspec(tm, dm), _rowspec(tm, dm)],
        out_specs=(_rowspec(tm, dm), _rowspec(tm, dm), _full((1, dm))), name="loss",
        compiler_params=_cp("arbitrary"))(y, tgt)


def _split_w_in(slabs):
    width = slabs[0].shape[1]

    def cols(lo, hi):
        return [s[:, max(lo - j * width, 0):min(hi - j * width, width)] for j, s in enumerate(slabs)
                if lo < (j + 1) * width and hi > j * width]

    main = [c for piece in (IN_GATE, IN_QKV, IN_SB, IN_Z, IN_MEMQ) for c in cols(*piece)]
    ab = jnp.concatenate(cols(*IN_AB), axis=1)
    return jnp.concatenate(main, axis=1), jnp.pad(ab, ((0, 0), (0, LANES - ab.shape[1])))


def _local_step(x, mem, tgt, W, P, dist=None):
    w_main, w_ab = W["w_main"], W["w_ab"]
    avec = jnp.pad(jnp.concatenate([P["a_log"], P["dt_bias"]], axis=0), ((0, 0), (0, LANES - N_HEAD)))

    h = _rms_fwd(x, P["norm1_g"], "rms1")
    if dist is None:
        proj = _mm(h, w_main, "nn", BF16, "in_proj", tn=1792)
    else:
        proj, gathered = _mm(h, w_main, "nn", BF16, "in_proj", tn=1792, comm=dist.gather_rest())
        rest, conv_w = dist.weights_from(gathered)
        W, P = {**W, **rest}, {**P, "conv_w": conv_w}
    wbr = (W["w_br_gdn"], W["w_br_sb"], W["w_br_mem"])
    ab = _mm(h, w_ab, "nn", F32, "in_proj_ab")
    q, k, v, gb = _gdn_pre(proj, P["conv_w"], ab, avec)
    tinv, tinv_t = _gdn_inv(k, gb)
    og, oraw, shist, vn = _gdn_fwd(q, k, v, gb, proj, P["gdn_norm_g"], tinv)
    sq, sk, sv = _sb_pre(proj, P["sb_q_norm_g"], P["sb_k_norm_g"])
    osb, ltot, cut = _sb_fwd(sq, sk, sv)
    mn, kv, kh, vm = _mem_kv(mem, P["mem_norm_g"], W["w_mem_kv"], P["mem_k_norm_g"])
    om = _mem_fwd(proj, kh, vm, P["mem_q_norm_g"])
    mix, yg, ys, ym = _merge_fwd(og, osb, om, proj, *wbr)
    x1 = _mm(mix, W["w_o"], "nn", F32, "out_proj", extra=x, epi=_epi_add)
    h2 = _rms_fwd(x1, P["norm2_g"], "rms2")
    if dist is None:
        u = _mm(h2, W["w_up"], "nn", BF16, "mlp_up", tn=2048)
    else:
        u, gathered = _mm(h2, W["w_up"], "nn", BF16, "mlp_up", tn=2048, comm=dist.gather_late())
        W = {**W, **dist.late_weights_from(gathered)}
    y = _mm(u, W["w_down"], "nn", F32, "mlp_down", a_fn=_relu2, extra=x1, epi=_epi_add)
    dy, dyb, sq_err = _loss(y, tgt)

    G = {}
    du = _mm(dyb, W["w_down"], "nt", BF16, "d_mlp_act", tn=2048, extra=u, epi=_epi_drelu2)
    G["w_down"] = _mm(u, dyb, "tn", F32, "dw_down", a_fn=_relu2)
    G["w_up"] = _mm(du, h2, "tn", F32, "dw_up")
    dh2 = _mm(du, W["w_up"], "nt", F32, "d_h2")
    dx1, dx1b, G["norm2_g"] = _rms_bwd(dh2, x1, P["norm2_g"], dy, "rms2_bwd")
    dmix = _mm(dx1b, W["w_o"], "nt", BF16, "d_mix")
    G["w_o"] = _mm(mix, dx1b, "tn", F32, "dw_o")
    dproj, dog, dosb, dom, G["w_br_gdn"], G["w_br_sb"], G["w_br_mem"] = _merge_bwd(
        dmix, proj, (yg, ys, ym), (og, osb, om), wbr)
    dq, dk, dv, dgb, dproj, G["gdn_norm_g"] = _gdn_bwd(q, k, v, gb, proj, P["gdn_norm_g"], oraw, shist, tinv_t, vn, dog,
                                                      dproj)
    dxc, dab, G["conv_w"], dav = _gdn_pre_bwd(proj, P["conv_w"], ab, avec, dq, dk, dv, dgb)
    dproj = _conv_bwd(dxc, P["conv_w"], dproj)
    G["a_log"], G["dt_bias"] = dav[0:1, :N_HEAD], dav[1:2, :N_HEAD]
    dsq, dsk, dsv = _sb_bwd(sq, sk, sv, ltot, dosb, cut)
    dproj, G["sb_q_norm_g"], G["sb_k_norm_g"] = _sb_pre_bwd(proj, P["sb_q_norm_g"], P["sb_k_norm_g"], dsq, dsk, dsv,
                                                            dproj)
    dproj, dkh, dvm, G["mem_q_norm_g"] = _mem_bwd(proj, kh, vm, P["mem_q_norm_g"], dom, dproj)
    G["w_mem_kv"], G["mem_norm_g"], G["mem_k_norm_g"] = _mem_kv_bwd(
        mem, P["mem_norm_g"], W["w_mem_kv"], P["mem_k_norm_g"], kv, mn, dkh, dvm)
    dw_ab = _mm(dab, h, "tn", F32, "dw_in_ab")
    if dist is None:
        dw_main = _mm(dproj, h, "tn", F32, "dw_in")
    else:
        early = [n for n, _, _ in BIG if n != "w_in"]
        dw_main, landed = _mm(dproj, h, "tn", F32, "dw_in", comm=dist.scatter(G, early, "early"))
        dist.collect(early, landed)
    G["w_in"] = jnp.concatenate([dw_main[QKV_OFF:SB_OFF], dw_main[Z_OFF:MEMQ_OFF], dw_ab[:8], dw_main[SB_OFF:Z_OFF],
                                 dw_main[MEMQ_OFF:], dw_main[:QKV_OFF]], axis=0)
    if dist is None:
        dh = _mm(dproj, w_main, "nt", F32, "d_h")
    else:
        dh, landed = _mm(dproj, w_main, "nt", F32, "d_h", comm=dist.scatter(G, ["w_in"], "late"))
        dist.collect(["w_in"], landed)
    dh = _mm(dab, w_ab, "nt", F32, "d_h_ab", extra=dh, epi=_epi_add)
    dx, _, G["norm1_g"] = _rms_bwd(dh, x, P["norm1_g"], dx1, "rms1_bwd")
    return sq_err, dx, G


def _comm(name, ins, out_shapes, plan):
    n_in, n_out = len(ins), len(out_shapes)
    probe = plan([None] * n_in, [None] * n_out, 0, 0, 0, dry=True)
    n_copy = probe

    def body(*refs):
        in_refs, out_refs = refs[:n_in], refs[n_in:n_in + n_out]
        send_sems, recv_sems = refs[n_in + n_out:]
        x, y, c = lax.axis_index("x"), lax.axis_index("y"), lax.axis_index("c")
        copies = []
        for k, (src, dst, dev) in enumerate(plan(in_refs, out_refs, x, y, c, dry=False)):
            if dev is None:
                cp = pltpu.make_async_copy(src, dst, send_sems.at[k])
            else:
                cp = pltpu.make_async_remote_copy(src_ref=src, dst_ref=dst, send_sem=send_sems.at[k],
                                                  recv_sem=recv_sems.at[k], device_id=dev, device_id_type=MESH)
            cp.start()
            copies.append(cp)
        for cp in copies:
            cp.wait()

    return pl.pallas_call(
        body, out_shape=tuple(out_shapes), in_specs=[HBM] * n_in, out_specs=tuple([HBM] * n_out),
        scratch_shapes=[pltpu.SemaphoreType.DMA((n_copy,)), pltpu.SemaphoreType.DMA((n_copy,))], name=name)(*ins)


def _other_chips(x, y):
    return ((1 - x, y), (x, 1 - y), (1 - x, 1 - y))


def _gather_plan(parts, direct=()):
    n, every = len(parts), list(parts) + list(direct)

    def copies(ins, outs, send, recv, scratch):
        x, y, c = lax.axis_index("x"), lax.axis_index("y"), lax.axis_index("c")
        me = 2 * x + y
        chips = _other_chips(x, y)
        local_sems, staged = scratch[0], scratch[1:]

        def remote(src, dst, k, dev):
            return pltpu.make_async_remote_copy(src_ref=src, dst_ref=dst, send_sem=send.at[k], recv_sem=recv.at[k],
                                                device_id=dev, device_id_type=MESH)

        def half(p, ci):
            hr = ins[p].shape[0] // 2
            return pl.ds(pl.multiple_of(ci * hr, 16), hr)

        sent = [remote(ins[p].at[half(p, c)], outs[p].at[me, half(p, c)], 6 * p + f, (px, py, c))
                for p in range(n) for f, (px, py) in enumerate(chips)]
        sent += [remote(ins[p], outs[p].at[me], 6 * n + 3 * (p - n) + f, (px, py, c))
                 for p in range(n, len(every)) for f, (px, py) in enumerate(chips)]
        landed = [outs[p].at[2 * px + py, half(p, c)] for p in range(n) for px, py in chips]
        passed = [remote(landed[3 * p + f], landed[3 * p + f], 6 * p + 3 + f, (x, y, 1 - c))
                  for p in range(n) for f in range(3)]
        loads = [pltpu.make_async_copy(ins[p], staged[p], local_sems.at[2 * p]) for p in range(len(every))]
        stores = [pltpu.make_async_copy(staged[p], outs[p].at[me], local_sems.at[2 * p + 1]) for p in range(len(every))]
        return sent, passed, loads, stores

    def start(*refs):
        sent, _, loads, _ = copies(*refs)
        for cp in loads + sent:
            cp.start()

    def mid(*refs):
        sent, passed, loads, stores = copies(*refs)
        for ld, st in zip(loads, stores):
            ld.wait()
            st.start()
        for p in range(n):
            for f in range(3):
                sent[3 * p + f].wait_recv()
                passed[3 * p + f].start()

    def finish(*refs):
        sent, passed, _, stores = copies(*refs)
        for cp in sent[:3 * n]:
            cp.wait_send()
        for cp in passed + sent[3 * n:] + stores:
            cp.wait()

    return _Hosted(every, [jax.ShapeDtypeStruct((4,) + p.shape, p.dtype) for p in every], 6 * n + 3 * len(direct),
                   start, finish, mid,
                   [pltpu.SemaphoreType.DMA((2 * len(every),))] + [pltpu.VMEM(p.shape, p.dtype) for p in every])


def _scatter_plan(pairs):
    def copies(ins, outs, send, recv, scratch):
        x, y, c = lax.axis_index("x"), lax.axis_index("y"), lax.axis_index("c")
        me = 2 * x + y
        return [pltpu.make_async_remote_copy(src_ref=src.at[2 * px + py], dst_ref=dst.at[me], send_sem=send.at[3 * p + f],
                                             recv_sem=recv.at[3 * p + f], device_id=(px, py, c), device_id_type=MESH)
                for p, (src, dst) in enumerate(zip(ins, outs)) for f, (px, py) in enumerate(_other_chips(x, y))]

    def start(*refs):
        for cp in copies(*refs):
            cp.start()

    def finish(*refs):
        for cp in copies(*refs):
            cp.wait()

    return _Hosted(pairs, [jax.ShapeDtypeStruct(a.shape, a.dtype) for a in pairs], 3 * len(pairs), start, finish)


def _run_hosted(comm, name):
    n_in, n_out = len(comm.ins), len(comm.out_shapes)

    def body(*refs):
        args = (refs[:n_in], refs[n_in:n_in + n_out], refs[n_in + n_out], refs[n_in + n_out + 1], refs[n_in + n_out + 2:])
        comm.start(*args)
        if comm.mid is not None:
            comm.mid(*args)
        comm.finish(*args)

    sems = [pltpu.SemaphoreType.DMA((comm.n_sems,)), pltpu.SemaphoreType.DMA((comm.n_sems,))]
    return list(pl.pallas_call(
        body, out_shape=tuple(comm.out_shapes), in_specs=[HBM] * n_in, out_specs=tuple([HBM] * n_out),
        scratch_shapes=sems + comm.scratch, name=name, compiler_params=_cp())(*comm.ins))


def _swap_halves(slabs, name):
    n = len(slabs)

    def body(*refs):
        ins, outs = refs[:n], refs[n:2 * n]
        send, recv = refs[2 * n:]
        x, y, c = lax.axis_index("x"), lax.axis_index("y"), lax.axis_index("c")
        other = (x, y, 1 - c)
        for p in range(n):
            for j in range(4):
                pltpu.make_async_remote_copy(src_ref=ins[p].at[j, 1 - c], dst_ref=outs[p].at[j], send_sem=send.at[p],
                                             recv_sem=recv.at[p], device_id=other, device_id_type=MESH).start()
        for p in range(n):
            pltpu.make_async_remote_copy(src_ref=outs[p], dst_ref=outs[p], send_sem=send.at[p], recv_sem=recv.at[p],
                                         device_id=other, device_id_type=MESH).wait()

    shapes = [jax.ShapeDtypeStruct((4,) + s.shape[2:], s.dtype) for s in slabs]
    return pl.pallas_call(
        body, out_shape=tuple(shapes), in_specs=[HBM] * n, out_specs=tuple([HBM] * n),
        scratch_shapes=[pltpu.SemaphoreType.DMA((n,)), pltpu.SemaphoreType.DMA((n,))], name=name)(*slabs)


def _join_halves(both):
    n = len(both)

    def body(*refs):
        bufs = refs[n:2 * n]
        send, recv = refs[2 * n:]
        x, y, c = lax.axis_index("x"), lax.axis_index("y"), lax.axis_index("c")
        copies = []
        for p in range(n):
            cp = pltpu.make_async_remote_copy(src_ref=bufs[p].at[c], dst_ref=bufs[p].at[c], send_sem=send.at[p],
                                              recv_sem=recv.at[p], device_id=(x, y, 1 - c), device_id_type=MESH)
            cp.start()
            copies.append(cp)
        for cp in copies:
            cp.wait()

    return pl.pallas_call(
        body, out_shape=tuple(jax.ShapeDtypeStruct(a.shape, a.dtype) for a in both), in_specs=[HBM] * n,
        out_specs=tuple([HBM] * n), input_output_aliases={p: p for p in range(n)},
        scratch_shapes=[pltpu.SemaphoreType.DMA((n,)), pltpu.SemaphoreType.DMA((n,))], name="grad_join_cores")(*both)


def _gather_all(a, name):
    def plan(ins, outs, x, y, c, dry):
        if dry:
            return 8
        me = 4 * x + 2 * y + c
        copies = [(ins[0], outs[0].at[me], None)]
        for f in range(1, 8):
            peer = (1 - x if f & 4 else x, 1 - y if f & 2 else y, 1 - c if f & 1 else c)
            copies.append((ins[0], outs[0].at[me], peer))
        return copies

    return _comm(name, [a], [jax.ShapeDtypeStruct((8,) + a.shape, a.dtype)], plan)[0]


def _sum_slots(a, name, extra=None):
    n, R, _ = a.shape
    rb = min(ROW_BLK, R)

    def body(*refs):
        a_ref, o_ref = refs[0], refs[-1]
        acc = a_ref[0]
        for s in range(1, n):
            acc = acc + a_ref[s]
        if extra is not None:
            acc = acc + refs[1][...]
        o_ref[...] = acc

    ins = [a] + ([extra] if extra is not None else [])
    in_specs = [pl.BlockSpec((n, rb, LANES), lambda i: (0, i, 0))] + ([_rowspec(rb, LANES)] if extra is not None else [])
    return pl.pallas_call(
        body, out_shape=jax.ShapeDtypeStruct((R, LANES), F32), grid=(R // rb,), in_specs=in_specs,
        out_specs=_rowspec(rb, LANES), name=name, compiler_params=_cp("parallel"))(*ins)


def _pair_sum(slab, theirs, core, name):
    _, _, hr, C = slab.shape

    def body(c_ref, a_ref, b_ref, o_ref):
        o_ref[...] = (a_ref[...] + b_ref[...]).astype(BF16)

    return pl.pallas_call(
        body, out_shape=jax.ShapeDtypeStruct((4, hr, C), BF16),
        grid_spec=pltpu.PrefetchScalarGridSpec(
            num_scalar_prefetch=1, grid=(4,),
            in_specs=[pl.BlockSpec((None, None, hr, C), lambda j, c_ref: (j, c_ref[0], 0, 0)),
                      pl.BlockSpec((None, hr, C), lambda j, c_ref: (j, 0, 0))],
            out_specs=pl.BlockSpec((None, hr, C), lambda j, c_ref: (j, 0, 0))),
        name=name, compiler_params=_cp("parallel"))(core, slab, theirs)


def _chip_sum(recv, pairs, where, name):
    _, hr, C = recv.shape

    def body(w_ref, r_ref, p_ref, o_ref):
        me = w_ref[0]
        o_ref[...] = jnp.zeros_like(o_ref)
        for s in range(4):
            @pl.when(me == s)
            def _():
                o_ref[...] += p_ref[...].astype(F32)

            @pl.when(me != s)
            def _():
                o_ref[...] += r_ref[s].astype(F32)

    return pl.pallas_call(
        body, out_shape=jax.ShapeDtypeStruct((2, hr, C), F32),
        grid_spec=pltpu.PrefetchScalarGridSpec(
            num_scalar_prefetch=1, grid=(1,),
            in_specs=[pl.BlockSpec((4, hr, C), lambda i, w_ref: (0, 0, 0)),
                      pl.BlockSpec((None, hr, C), lambda i, w_ref: (w_ref[0], 0, 0))],
            out_specs=pl.BlockSpec((None, hr, C), lambda i, w_ref: (w_ref[1], 0, 0))),
        name=name, compiler_params=_cp("arbitrary"))(where, recv, pairs)


def _adamw(w, g, m, v, name):
    R, C = w.shape
    rb = min(ADAM_ROWS, R)
    c1 = 1.0 - ADAM_B1 ** ADAM_STEP
    c2 = 1.0 - ADAM_B2 ** ADAM_STEP

    def body(w_ref, g_ref, m_ref, v_ref, d_ref, nm_ref, nv_ref):
        gv = g_ref[...]
        nm = ADAM_B1 * m_ref[...] + (1.0 - ADAM_B1) * gv
        nv = ADAM_B2 * v_ref[...] + (1.0 - ADAM_B2) * (gv * gv)
        d_ref[...] = -ADAM_LR * ((nm / c1) / (jnp.sqrt(nv / c2) + ADAM_EPS) + ADAM_WD * w_ref[...])
        nm_ref[...] = nm
        nv_ref[...] = nv

    spec = _rowspec(rb, C)
    return pl.pallas_call(
        body, out_shape=(jax.ShapeDtypeStruct((R, C), F32),) * 3, grid=(R // rb,), in_specs=[spec] * 4,
        out_specs=(spec,) * 3, name=name, compiler_params=_cp("parallel"))(w, g, m, v)


class _Dist:
    def __init__(self, shards):
        self.shards = shards
        self.chip = 2 * lax.axis_index("x") + lax.axis_index("y")
        self.where = jnp.stack([self.chip, lax.axis_index("c")]).astype(jnp.int32)
        self.pairs, self.landed = {}, {}

    @staticmethod
    def _unshard(name, blk):
        _, (r, cc), axis = next(b for b in BIG if b[0] == name)
        return blk.reshape(4 * r, cc) if axis == 0 else blk.transpose(1, 0, 2).reshape(r, 4 * cc)

    def gather_first(self):
        got = _run_hosted(_gather_plan([self.shards["w_in"].astype(BF16)]), "gather_w_in")[0]
        return _split_w_in([got[j] for j in range(4)])

    LATE = ("w_down",)

    def gather_rest(self):
        rest = [self.shards[n].astype(BF16) for n, _, _ in BIG if n != "w_in" and n not in self.LATE]
        return _gather_plan(rest, [self.shards["conv_w"]])

    def weights_from(self, gathered):
        names = [n for n, _, _ in BIG if n != "w_in" and n not in self.LATE]
        conv = gathered[-1]
        taps, width = conv.shape[1:]
        return ({n: self._unshard(n, g) for n, g in zip(names, gathered)},
                conv.transpose(1, 0, 2).reshape(taps, 4 * width))

    def gather_late(self):
        return _gather_plan([self.shards[n].astype(BF16) for n in self.LATE])

    def late_weights_from(self, gathered):
        return {n: self._unshard(n, g) for n, g in zip(self.LATE, gathered)}

    def scatter(self, G, names, tag):
        slabs = []
        for name, (r, cc), axis in BIG:
            if name not in names:
                continue
            g = G[name]
            if axis == 0:
                slabs.append(g.reshape(4, 2, r // 2, cc))
            else:
                slabs.append(g.reshape(4, 2, cc // 2, r))
        theirs = _swap_halves(slabs, "grad_swap_cores_" + tag)
        pairs = [_pair_sum(s, t, self.where[1:], "pair_sum_" + n) for s, t, n in zip(slabs, theirs, names)]
        self.pairs.update(zip(names, pairs))
        return _scatter_plan(pairs)

    def collect(self, names, landed):
        self.landed.update(zip(names, landed))

    def finish(self):
        names = [n for n, _, _ in BIG]
        halves = [_chip_sum(self.landed[n], self.pairs[n], self.where, "chip_sum_" + n) for n in names]
        out = {}
        for (name, (r, cc), axis), both in zip(BIG, _join_halves(halves)):
            full = both.reshape(-1, both.shape[-1])
            out[name] = full if axis == 0 else full.T
        return out


def _pack_rows(parts, rows, dtype):
    flat = jnp.concatenate([p.reshape(-1).astype(dtype) for p in parts])
    return jnp.pad(flat, (0, rows * LANES - flat.shape[0])).reshape(rows, LANES)


def _small_rows(n):
    return max(n // LANES, 1)


def _pack_small(vals):
    rows = []
    for name, n in SMALL:
        r = _small_rows(n)
        rows.append(jnp.pad(vals[name].reshape(-1), (0, r * LANES - n)).reshape(r, LANES))
    flat = jnp.concatenate(rows, axis=0)
    return jnp.pad(flat, ((0, SMALL_ROWS - flat.shape[0]), (0, 0)))


def _unpack_small(pack):
    out, r0 = {}, 0
    for name, n in SMALL:
        r = _small_rows(n)
        out[name] = pack[r0:r0 + r].reshape(-1)[:n]
        r0 += r
    return out


def kernel(x, mem, norm1_g, w_in, conv_w, a_log, dt_bias, gdn_norm_g, sb_q_norm_g, sb_k_norm_g, mem_norm_g, w_mem_kv, mem_q_norm_g, mem_k_norm_g, w_br_gdn, w_br_sb, w_br_mem, w_o, norm2_g, w_up, w_down, loss_target, m_norm1_g, m_w_in, m_conv_w, m_a_log, m_dt_bias, m_gdn_norm_g, m_sb_q_norm_g, m_sb_k_norm_g, m_mem_norm_g, m_w_mem_kv, m_mem_q_norm_g, m_mem_k_norm_g, m_w_br_gdn, m_w_br_sb, m_w_br_mem, m_w_o, m_norm2_g, m_w_up, m_w_down, v_norm1_g, v_w_in, v_conv_w, v_a_log, v_dt_bias, v_gdn_norm_g, v_sb_q_norm_g, v_sb_k_norm_g, v_mem_norm_g, v_w_mem_kv, v_mem_q_norm_g, v_mem_k_norm_g, v_w_br_gdn, v_w_br_sb, v_w_br_mem, v_w_o, v_norm2_g, v_w_up, v_w_down):
    wd = dict(norm1_g=norm1_g, w_in=w_in, conv_w=conv_w, a_log=a_log, dt_bias=dt_bias, gdn_norm_g=gdn_norm_g,
              sb_q_norm_g=sb_q_norm_g, sb_k_norm_g=sb_k_norm_g, mem_norm_g=mem_norm_g, w_mem_kv=w_mem_kv,
              mem_q_norm_g=mem_q_norm_g, mem_k_norm_g=mem_k_norm_g, w_br_gdn=w_br_gdn, w_br_sb=w_br_sb,
              w_br_mem=w_br_mem, w_o=w_o, norm2_g=norm2_g, w_up=w_up, w_down=w_down)
    md = dict(norm1_g=m_norm1_g, w_in=m_w_in, conv_w=m_conv_w, a_log=m_a_log, dt_bias=m_dt_bias,
              gdn_norm_g=m_gdn_norm_g, sb_q_norm_g=m_sb_q_norm_g, sb_k_norm_g=m_sb_k_norm_g,
              mem_norm_g=m_mem_norm_g, w_mem_kv=m_w_mem_kv, mem_q_norm_g=m_mem_q_norm_g,
              mem_k_norm_g=m_mem_k_norm_g, w_br_gdn=m_w_br_gdn, w_br_sb=m_w_br_sb, w_br_mem=m_w_br_mem, w_o=m_w_o,
              norm2_g=m_norm2_g, w_up=m_w_up, w_down=m_w_down)
    vd = dict(norm1_g=v_norm1_g, w_in=v_w_in, conv_w=v_conv_w, a_log=v_a_log, dt_bias=v_dt_bias,
              gdn_norm_g=v_gdn_norm_g, sb_q_norm_g=v_sb_q_norm_g, sb_k_norm_g=v_sb_k_norm_g,
              mem_norm_g=v_mem_norm_g, w_mem_kv=v_w_mem_kv, mem_q_norm_g=v_mem_q_norm_g,
              mem_k_norm_g=v_mem_k_norm_g, w_br_gdn=v_w_br_gdn, w_br_sb=v_w_br_sb, w_br_mem=v_w_br_mem, w_o=v_w_o,
              norm2_g=v_norm2_g, w_up=v_w_up, w_down=v_w_down)
    wd, md, vd = ({n: a[0] for n, a in d.items()} for d in (wd, md, vd))
    chip = 2 * lax.axis_index("x") + lax.axis_index("y")
    conv_shard = wd["conv_w"].shape

    dist = _Dist(wd)
    W = dict(zip(("w_main", "w_ab"), dist.gather_first()))
    P = {n: wd[n].reshape(1, -1) for n, _ in SMALL}

    sq_err, grad_x, G = _local_step(x[0], mem[0], loss_target[0], W, P, dist)
    loss = lax.psum(0.5 / D_MODEL * jnp.sum(sq_err), ("x", "y", "c"))

    g_big = dist.finish()

    spack = jnp.concatenate([_pack_small(G), G["conv_w"].reshape(CONV_ROWS, LANES)], axis=0)
    g_small = _sum_slots(_gather_all(spack, "gather_small_grads"), "small_grad_sum")
    g_conv_full = g_small[SMALL_ROWS:].reshape(conv_shard[0], 4 * conv_shard[1])
    g_conv = lax.dynamic_slice_in_dim(g_conv_full, chip * conv_shard[1], conv_shard[1], axis=1)

    grads, deltas, new_m, new_v = dict(g_big), {}, {}, {}
    for name, _, _ in BIG:
        deltas[name], new_m[name], new_v[name] = _adamw(wd[name], g_big[name], md[name], vd[name], "adamw_" + name)
    pack_sm = lambda d: jnp.concatenate([_pack_small(d), _pack_rows([d["conv_w"]], APACK_ROWS - SMALL_ROWS, F32)], axis=0)
    g_sm = jnp.concatenate([g_small[:SMALL_ROWS], _pack_rows([g_conv], APACK_ROWS - SMALL_ROWS, F32)], axis=0)
    small = (g_sm,) + _adamw(pack_sm(wd), g_sm, pack_sm(md), pack_sm(vd), "adamw_small")
    for out, pack in zip((grads, deltas, new_m, new_v), small):
        out.update(_unpack_small(pack[:SMALL_ROWS]))
        out["conv_w"] = pack[SMALL_ROWS:].reshape(-1)[:conv_shard[0] * conv_shard[1]].reshape(conv_shard)

    return (loss, grad_x[None], *[d[n][None] for d in (grads, deltas, new_m, new_v) for n in WEIGHTS])
```

```python
import jax
import jax.numpy as jnp
from jax import lax
from jax.experimental import pallas as pl
from jax.experimental.pallas import tpu as pltpu

F32 = jnp.float32
BF16 = jnp.bfloat16
MESH = pl.DeviceIdType.MESH

D_MODEL = 1024
N_HEAD = 4
D_HEAD = 128
BR_W = N_HEAD * D_HEAD
CONV_TAPS = 4
GDN_CHUNK = 64
INV_BLOCK = 16
INV_CHUNKS = 4
N_MEM = 256
D_FF = 4 * D_MODEL
EPS = 1e-6
LANES = 128
PROJ_W = 7168
GATE_OFF = 0
QKV_OFF = 3072
SB_OFF = 4608
Z_OFF = 6144
MEMQ_OFF = 6656
IN_GATE, IN_QKV, IN_SB, IN_Z, IN_MEMQ, IN_AB = (4104, 7176), (0, 1536), (2056, 3592), (1536, 2048), (3592, 4104), (2048, 2056)

ADAM_LR, ADAM_B1, ADAM_B2, ADAM_EPS, ADAM_WD, ADAM_STEP = 0.001, 0.9, 0.999, 1e-08, 0.01, 10

TM = 512
MM_TM = 1024
MM_TK = (2048, 1792, 1024, 128)
TK_TOK = 2048
GDN_STEP_CHUNKS = 8
GDN_BWD_STEP_CHUNKS = 2
G1_TM = 256
SB_BLK = 512
SB_DEAD = 120.0
VMEM_LIMIT = 48 << 20

BIG = (("w_in", (1024, 1794), 1), ("w_mem_kv", (256, 1024), 0), ("w_br_gdn", (512, 256), 1),
       ("w_br_sb", (512, 256), 1), ("w_br_mem", (512, 256), 1), ("w_o", (256, 1024), 0),
       ("w_up", (1024, 1024), 1), ("w_down", (1024, 1024), 0))
COL_SHARDED = tuple(n for n, _, a in BIG if a == 1)
GATE_COL = GATE_OFF // D_MODEL
QKV_COL = QKV_OFF // (3 * BR_W)
ROW_BLK = 1024
ADAM_ROWS = 128
SMALL = (("norm1_g", 1024), ("mem_norm_g", 1024), ("norm2_g", 1024), ("gdn_norm_g", 128), ("sb_q_norm_g", 128),
         ("sb_k_norm_g", 128), ("mem_q_norm_g", 128), ("mem_k_norm_g", 128), ("a_log", 4), ("dt_bias", 4))
SMALL_ROWS = 32
CONV_ROWS = 48
SPACK_ROWS = SMALL_ROWS + CONV_ROWS
APACK_ROWS = SMALL_ROWS + 16

WEIGHTS = ("norm1_g", "w_in", "conv_w", "a_log", "dt_bias", "gdn_norm_g", "sb_q_norm_g", "sb_k_norm_g",
           "mem_norm_g", "w_mem_kv", "mem_q_norm_g", "mem_k_norm_g", "w_br_gdn", "w_br_sb", "w_br_mem", "w_o",
           "norm2_g", "w_up", "w_down")


def _cp(*sem):
    return pltpu.CompilerParams(dimension_semantics=sem if sem else None, vmem_limit_bytes=VMEM_LIMIT)


HBM = pl.BlockSpec(memory_space=pl.ANY)

_NN = (((1,), (0,)), ((), ()))
_NT = (((1,), (1,)), ((), ()))
_TN = (((0,), (0,)), ((), ()))


def _dot(a, b, dims=_NN):
    return lax.dot_general(a.astype(BF16), b.astype(BF16), dims, preferred_element_type=F32)


def _dot_nt(a, b):
    return _dot(a, b, _NT)


def _dot_tn(a, b):
    return _dot(a, b, _TN)


def _dotf(a, b, dims=_NN):
    return lax.dot_general(a, b, dims, precision=lax.Precision.HIGHEST, preferred_element_type=F32)


def _sigmoid(v):
    return 0.5 * jnp.tanh(0.5 * v) + 0.5


def _softplus(v):
    return jnp.maximum(v, 0.0) + jnp.log(1.0 + jnp.exp(-jnp.abs(v)))


def _iota(shape, dim):
    return lax.broadcasted_iota(jnp.int32, shape, dim)


def _hs(h):
    return slice(h * D_HEAD, (h + 1) * D_HEAD)


def _rowspec(tm, w, col=0):
    return pl.BlockSpec((tm, w), lambda i: (i, col))


def _full(shape):
    return pl.BlockSpec(shape, lambda *_: (0,) * len(shape))


def _accum(ref, first, val):
    @pl.when(first)
    def _():
        ref[...] = val

    @pl.when(jnp.logical_not(first))
    def _():
        ref[...] += val


class _Hosted:
    def __init__(self, ins, out_shapes, n_sems, start, finish, mid=None, scratch=()):
        self.ins, self.out_shapes, self.n_sems = list(ins), list(out_shapes), n_sems
        self.start, self.mid, self.finish, self.scratch = start, mid, finish, list(scratch)


def _mm(a, b, mode, out_dtype, name, *, tm=None, tn=None, tk=None, a_fn=None, extra=None, epi=None, comm=None):
    if mode == "tn":
        (K, M), N = a.shape, b.shape[1]
    else:
        (M, K), N = a.shape, (b.shape[0] if mode == "nt" else b.shape[1])
    tm = min(tm or (1024 if mode == "tn" else MM_TM), M)
    tn = min(tn or 1024, N)
    tk = min(tk or (TK_TOK if mode == "tn" else next(t for t in MM_TK if K % t == 0)), K)
    nm, nn, nk = M // tm, N // tn, K // tk
    assert nm * tm == M and nn * tn == N and nk * tk == K, (name, a.shape, b.shape)
    if mode == "tn":
        a_spec = pl.BlockSpec((tk, tm), lambda i, j, k: (k, i))
    else:
        a_spec = pl.BlockSpec((tm, tk), lambda i, j, k: (i, k))
    if mode == "nt":
        b_spec = pl.BlockSpec((tn, tk), lambda i, j, k: (j, k))
    else:
        b_spec = pl.BlockSpec((tk, tn), lambda i, j, k: (k, j))
    dims = {"nn": _NN, "nt": _NT, "tn": _TN}[mode]
    o_spec = pl.BlockSpec((tm, tn), lambda i, j, k: (i, j))
    has_extra = extra is not None

    n_ci, n_co = (len(comm.ins), len(comm.out_shapes)) if comm else (0, 0)
    n_in = 2 + has_extra + n_ci
    steps = nm * nn * nk

    def body(*refs):
        a_ref, b_ref = refs[0], refs[1]
        e_ref = refs[2] if has_extra else None
        o_ref = refs[n_in]
        scratch = refs[n_in + 1 + n_co:]
        if comm:
            step = (pl.program_id(0) * nn + pl.program_id(1)) * nk + pl.program_id(2)
            cargs = (refs[2 + has_extra:n_in], refs[n_in + 1:n_in + 1 + n_co], scratch[nk > 1], scratch[(nk > 1) + 1],
                     scratch[(nk > 1) + 2:])
            pl.when(step == 0)(lambda: comm.start(*cargs))
            if comm.mid is not None:
                pl.when(step == (steps * 7) // 8)(lambda: comm.mid(*cargs))
        av = a_ref[...]
        if a_fn is not None:
            av = a_fn(av)
        p = lax.dot_general(av, b_ref[...], dims, preferred_element_type=F32)

        def finish(acc):
            if epi is not None:
                acc = epi(acc, e_ref[...] if has_extra else None)
            o_ref[...] = acc.astype(out_dtype)

        if nk == 1:
            finish(p)
        else:
            acc_ref = scratch[0]
            k = pl.program_id(2)
            _accum(acc_ref, k == 0, p)

            @pl.when(k == nk - 1)
            def _():
                finish(acc_ref[...])

        if comm:
            pl.when(step == steps - 1)(lambda: comm.finish(*cargs))

    ins = [a, b] + ([extra] if has_extra else [])
    in_specs = [a_spec, b_spec] + ([o_spec] if has_extra else [])
    scratch_shapes = [pltpu.VMEM((tm, tn), F32)] if nk > 1 else []
    main = jax.ShapeDtypeStruct((M, N), out_dtype)
    if not comm:
        return pl.pallas_call(
            body, out_shape=main, grid=(nm, nn, nk), in_specs=in_specs, out_specs=o_spec,
            scratch_shapes=scratch_shapes, name=name, compiler_params=_cp("parallel", "parallel", "arbitrary"))(*ins)
    sems = [pltpu.SemaphoreType.DMA((comm.n_sems,)), pltpu.SemaphoreType.DMA((comm.n_sems,))]
    res = pl.pallas_call(
        body, out_shape=(main, *comm.out_shapes), grid=(nm, nn, nk), in_specs=in_specs + [HBM] * n_ci,
        out_specs=(o_spec, *[HBM] * n_co), scratch_shapes=scratch_shapes + sems + comm.scratch, name=name,
        compiler_params=_cp("arbitrary", "arbitrary", "arbitrary"))(*ins, *comm.ins)
    return res[0], list(res[1:])


def _relu2(u):
    r = jnp.maximum(u.astype(F32), 0.0)
    return (r * r).astype(BF16)


def _epi_add(acc, e):
    return acc + e.astype(F32)


def _epi_drelu2(acc, u):
    return acc * (2.0 * jnp.maximum(u.astype(F32), 0.0))


def _rms_fwd(x, g, name):
    T, dm = x.shape
    tm = min(TM, T)

    def body(x_ref, g_ref, h_ref):
        xv = x_ref[...]
        r = lax.rsqrt(jnp.mean(xv * xv, axis=-1, keepdims=True) + EPS)
        h_ref[...] = (xv * r * g_ref[...]).astype(BF16)

    return pl.pallas_call(
        body, out_shape=jax.ShapeDtypeStruct((T, dm), BF16), grid=(T // tm,),
        in_specs=[_rowspec(tm, dm), _full((1, dm))], out_specs=_rowspec(tm, dm), name=name,
        compiler_params=_cp("parallel"))(x, g)


def _rms_bwd(dh, x, g, resid, name):
    T, dm = x.shape
    tm = min(TM, T)

    def body(dh_ref, x_ref, g_ref, res_ref, dx_ref, dxb_ref, dg_ref):
        i = pl.program_id(0)
        xv = x_ref[...]
        r = lax.rsqrt(jnp.mean(xv * xv, axis=-1, keepdims=True) + EPS)
        y = xv * r
        dhv = dh_ref[...].astype(F32)
        dy = dhv * g_ref[...]
        dx = res_ref[...] + r * (dy - y * jnp.mean(dy * y, axis=-1, keepdims=True))
        dx_ref[...] = dx
        dxb_ref[...] = dx.astype(BF16)
        _accum(dg_ref, i == 0, jnp.sum(dhv * y, axis=0, keepdims=True))

    return pl.pallas_call(
        body,
        out_shape=(jax.ShapeDtypeStruct((T, dm), F32), jax.ShapeDtypeStruct((T, dm), BF16),
                   jax.ShapeDtypeStruct((1, dm), F32)),
        grid=(T // tm,),
        in_specs=[_rowspec(tm, dm), _rowspec(tm, dm), _full((1, dm)), _rowspec(tm, dm)],
        out_specs=(_rowspec(tm, dm), _rowspec(tm, dm), _full((1, dm))), name=name,
        compiler_params=_cp("arbitrary"))(dh, x, g, resid)


def _conv_tile(x_ref, halo_ref, w_ref, xpad, tm):
    i = pl.program_id(0)
    halo = halo_ref[...].astype(F32)[8:16]
    xpad[0:8, :] = jnp.where(i > 0, halo, 0.0)
    xpad[8:, :] = x_ref[...].astype(F32)
    w = w_ref[...]
    xc = w[0:1] * xpad[5:5 + tm, :]
    for j in range(1, CONV_TAPS):
        xc = xc + w[j:j + 1] * xpad[5 + j:5 + j + tm, :]
    return xc


def _gate_terms(ab_ref, av_ref):
    abv = ab_ref[...]
    av = av_ref[...]
    pre = abv + av[1:2]
    ea = jnp.exp(av[0:1])
    g = -ea * _softplus(pre)
    return abv, pre, ea, g


def _gdn_pre(proj, conv_w, ab, avec):
    T = proj.shape[0]
    tm = min(G1_TM, T)
    cw = 3 * BR_W

    def body(x_ref, halo_ref, w_ref, ab_ref, av_ref, q_ref, k_ref, v_ref, gb_ref, xpad):
        xc = _conv_tile(x_ref, halo_ref, w_ref, xpad, tm)
        y = xc * _sigmoid(xc)
        for h in range(N_HEAD):
            for off, ref, scale in ((0, q_ref, D_HEAD ** -0.5), (BR_W, k_ref, 1.0)):
                yh = y[:, off + h * D_HEAD:off + (h + 1) * D_HEAD]
                r = lax.rsqrt(jnp.sum(yh * yh, axis=-1, keepdims=True) + EPS)
                ref[:, _hs(h)] = yh * (r * scale)
        v_ref[...] = y[:, 2 * BR_W:]
        abv, _, _, g = _gate_terms(ab_ref, av_ref)
        lane = _iota((tm, LANES), 1)
        gb_ref[...] = jnp.where(lane < N_HEAD, g, jnp.where(lane < 2 * N_HEAD, _sigmoid(abv), 0.0))

    hb = tm // 16
    return pl.pallas_call(
        body,
        out_shape=(jax.ShapeDtypeStruct((T, BR_W), F32),) * 3 + (jax.ShapeDtypeStruct((T, LANES), F32),),
        grid=(T // tm,),
        in_specs=[_rowspec(tm, cw, QKV_COL), pl.BlockSpec((16, cw), lambda i: (jnp.maximum(i * hb - 1, 0), QKV_COL)),
                  _full((CONV_TAPS, cw)), _rowspec(tm, LANES), _full((2, LANES))],
        out_specs=(_rowspec(tm, BR_W),) * 3 + (_rowspec(tm, LANES),),
        scratch_shapes=[pltpu.VMEM((tm + 8, cw), F32)], name="gdn_pre",
        compiler_params=_cp("parallel"))(proj, proj, conv_w, ab, avec)


def _gdn_pre_bwd(proj, conv_w, ab, avec, dq, dk, dv, dgb):
    T = proj.shape[0]
    tm = min(G1_TM, T)
    cw = 3 * BR_W

    def body(x_ref, halo_ref, w_ref, ab_ref, av_ref, dq_ref, dk_ref, dv_ref, dgb_ref,
             dxc_ref, dab_ref, dcw_ref, dav_ref, xpad):
        i = pl.program_id(0)

        @pl.when(i == 0)
        def _():
            dcw_ref[...] = jnp.zeros_like(dcw_ref)
            dav_ref[...] = jnp.zeros_like(dav_ref)

        xc_all = _conv_tile(x_ref, halo_ref, w_ref, xpad, tm)
        for s in range(cw // D_HEAD):
            cs = slice(s * D_HEAD, (s + 1) * D_HEAD)
            xc = xc_all[:, cs]
            sg = _sigmoid(xc)
            yh = xc * sg
            h = s % N_HEAD
            if s < 2 * N_HEAD:
                dref, scale = (dq_ref, D_HEAD ** -0.5) if s < N_HEAD else (dk_ref, 1.0)
                r = lax.rsqrt(jnp.sum(yh * yh, axis=-1, keepdims=True) + EPS)
                yn = yh * r
                dn = dref[:, _hs(h)]
                dy = (scale * r) * (dn - yn * jnp.sum(yn * dn, axis=-1, keepdims=True))
            else:
                dy = dv_ref[:, _hs(h)]
            dxc = dy * (sg * (1.0 + xc * (1.0 - sg)))
            dxc_ref[:, cs] = dxc.astype(BF16)
            for j in range(CONV_TAPS):
                dcw_ref[j:j + 1, cs] += jnp.sum(dxc * xpad[5 + j:5 + j + tm, cs], axis=0, keepdims=True)

        abv, pre, ea, g = _gate_terms(ab_ref, av_ref)
        dgbv = dgb_ref[...]
        lane = _iota((tm, LANES), 1)
        is_a = lane < N_HEAD
        da = jnp.where(is_a, dgbv * (-ea) * _sigmoid(pre), 0.0)
        bs = _sigmoid(abv)
        db = jnp.where(jnp.logical_and(lane >= N_HEAD, lane < 2 * N_HEAD), dgbv * bs * (1.0 - bs), 0.0)
        dab_ref[...] = (da + db).astype(BF16)
        dav_ref[0:1, :] += jnp.sum(jnp.where(is_a, dgbv * g, 0.0), axis=0, keepdims=True)
        dav_ref[1:2, :] += jnp.sum(da, axis=0, keepdims=True)

    hb = tm // 16
    return pl.pallas_call(
        body,
        out_shape=(jax.ShapeDtypeStruct((T, cw), BF16), jax.ShapeDtypeStruct((T, LANES), BF16),
                   jax.ShapeDtypeStruct((CONV_TAPS, cw), F32), jax.ShapeDtypeStruct((2, LANES), F32)),
        grid=(T // tm,),
        in_specs=[_rowspec(tm, cw, QKV_COL), pl.BlockSpec((16, cw), lambda i: (jnp.maximum(i * hb - 1, 0), QKV_COL)),
                  _full((CONV_TAPS, cw)), _rowspec(tm, LANES), _full((2, LANES)),
                  _rowspec(tm, BR_W), _rowspec(tm, BR_W), _rowspec(tm, BR_W), _rowspec(tm, LANES)],
        out_specs=(_rowspec(tm, cw), _rowspec(tm, LANES), _full((CONV_TAPS, cw)), _full((2, LANES))),
        scratch_shapes=[pltpu.VMEM((tm + 8, cw), F32)], name="gdn_pre_bwd",
        compiler_params=_cp("arbitrary"))(proj, proj, conv_w, ab, avec, dq, dk, dv, dgb)


def _conv_bwd(dxc, conv_w, into):
    T, cw = dxc.shape
    tm = min(G1_TM, T)
    nt = T // tm
    hb = tm // 16

    def body(d_ref, halo_ref, w_ref, into_ref, dx_ref, xpad):
        i = pl.program_id(0)
        xpad[0:tm, :] = d_ref[...].astype(F32)
        xpad[tm:, :] = jnp.where(i < nt - 1, halo_ref[...].astype(F32)[0:8], 0.0)
        w = w_ref[...]
        dx = w[3:4] * xpad[0:tm, :]
        for j in range(CONV_TAPS - 1):
            dx = dx + w[j:j + 1] * xpad[3 - j:3 - j + tm, :]
        dx_ref[...] = dx.astype(BF16)

    return pl.pallas_call(
        body, out_shape=jax.ShapeDtypeStruct(into.shape, BF16), grid=(nt,),
        in_specs=[_rowspec(tm, cw), pl.BlockSpec((16, cw), lambda i: (jnp.minimum((i + 1) * hb, T // 16 - 1), 0)),
                  _full((CONV_TAPS, cw)), HBM],
        out_specs=_rowspec(tm, cw, QKV_COL), scratch_shapes=[pltpu.VMEM((tm + 8, cw), F32)],
        input_output_aliases={3: 0}, name="conv_bwd", compiler_params=_cp("parallel"))(dxc, dxc, conv_w, into)


def _chunk_consts():
    C = GDN_CHUNK
    row, col = _iota((C, C), 0), _iota((C, C), 1)
    return row, col, row >= col, row > col


def _chunk_decay(gbv, incl):
    c_all = _dotf(incl.astype(F32), gbv)
    c_t = jnp.concatenate([c_all, jnp.zeros_like(c_all)], axis=0).T[:, :GDN_CHUNK]
    return c_all, c_t


def _head_decay(c_all, c_t, gbv, incl, h):
    C = GDN_CHUNK
    c_col = c_all[:, h:h + 1]
    c_row = c_t[h:h + 1, :]
    gam = jnp.exp(jnp.where(incl, c_col - c_row, -1e30))
    c_last = c_all[C - 1:C, h:h + 1]
    return gam, jnp.exp(c_col), jnp.exp(c_last - c_col), jnp.exp(c_last), gbv[:, N_HEAD + h:N_HEAD + h + 1]


def _split_bf16(x):
    hi = x.astype(BF16)
    return hi, (x - hi.astype(F32)).astype(BF16)


def _dot3(a, b):
    ah, al = _split_bf16(a)
    bh, bl = _split_bf16(b)
    d = lambda u, v: lax.dot_general(u, v, _NN, preferred_element_type=F32)
    return d(ah, bh) + (d(ah, bl) + d(al, bh))


def _unit_lower_inverses(ms, row, col):
    bi, bj = row // INV_BLOCK, col // INV_BLOCK
    eye = (row == col).astype(F32)
    ns = [jnp.where(bi == bj, -m, 0.0) for m in ms]
    invs = [eye + n for n in ns]
    size = 2
    while size < INV_BLOCK:
        ns = [_dot3(n, n) for n in ns]
        invs = [inv + _dot3(inv, n) for inv, n in zip(invs, ns)]
        size *= 2
    width = 2
    while width * INV_BLOCK <= GDN_CHUNK:
        sel = jnp.logical_and(bi // width == bj // width, bi // (width // 2) > bj // (width // 2))
        ts = [_dot3(inv, jnp.where(sel, m, 0.0)) for inv, m in zip(invs, ms)]
        invs = [inv - _dot3(t, inv) for inv, t in zip(invs, ts)]
        width *= 2
    return invs


def _gdn_inv(k, gb):
    T = k.shape[0]
    C = GDN_CHUNK
    per = min(INV_CHUNKS, T // C)
    rows = per * C

    def body(k_ref, gb_ref, ti_ref, tt_ref):
        row, col, incl, strict = _chunk_consts()
        ms = []
        for ci in range(per):
            rs = slice(ci * C, (ci + 1) * C)
            gbv = gb_ref[rs, :]
            c_all, c_t = _chunk_decay(gbv, incl)
            for h in range(N_HEAD):
                gam, _, _, _, bcol = _head_decay(c_all, c_t, gbv, incl, h)
                K = k_ref[rs, _hs(h)]
                ms.append(jnp.where(strict, _dot_nt(K * bcol, K) * gam, 0.0))
        eye = (row == col).astype(BF16)
        for i, inv in enumerate(_unit_lower_inverses(ms, row, col)):
            ti_ref[i // N_HEAD, i % N_HEAD] = inv
            tt_ref[i // N_HEAD, i % N_HEAD] = _dot_tn(inv, eye).astype(BF16)

    spec = pl.BlockSpec((per, N_HEAD, C, C), lambda i: (i, 0, 0, 0))
    return pl.pallas_call(
        body, out_shape=(jax.ShapeDtypeStruct((T // C, N_HEAD, C, C), F32),
                         jax.ShapeDtypeStruct((T // C, N_HEAD, C, C), BF16)),
        grid=(T // rows,), in_specs=[_rowspec(rows, BR_W), _rowspec(rows, LANES)], out_specs=(spec, spec),
        name="gdn_inv", compiler_params=_cp("parallel"))(k, gb)


def _gdn_fwd(q, k, v, gb, proj, gnorm, tinv_all):
    T = q.shape[0]
    C = GDN_CHUNK
    nc = T // C
    per = min(GDN_STEP_CHUNKS, nc)
    zcol = Z_OFF // BR_W
    heads = range(N_HEAD)

    def body(q_ref, k_ref, v_ref, gb_ref, z_ref, gn_ref, ti_ref, og_ref, oraw_ref, sh_ref, vn_ref, s_ref):
        @pl.when(pl.program_id(0) == 0)
        def _():
            s_ref[...] = jnp.zeros_like(s_ref)

        _, _, incl, _ = _chunk_consts()
        S = [s_ref[h] for h in heads]
        for ci in range(per):
            rs = slice(ci * C, (ci + 1) * C)
            gbv = gb_ref[rs, :]
            c_all, c_t = _chunk_decay(gbv, incl)
            dec = [_head_decay(c_all, c_t, gbv, incl, h) for h in heads]
            gam, gcol, dcol, glast, bcol = ([d[i] for d in dec] for i in range(5))
            Q = [q_ref[rs, _hs(h)] for h in heads]
            K = [k_ref[rs, _hs(h)] for h in heads]
            V = [v_ref[rs, _hs(h)] for h in heads]
            Sb = [s.astype(BF16) for s in S]
            KS = [_dot(K[h], Sb[h]) for h in heads]
            QS = [_dot(Q[h], Sb[h]) for h in heads]
            P = [_dot_nt(Q[h], K[h]) * gam[h] for h in heads]
            R = [bcol[h] * (V[h] - gcol[h] * KS[h]) for h in heads]
            vn = [_dot(ti_ref[ci, h], R[h]) for h in heads]
            O = [gcol[h] * QS[h] + _dot(P[h], vn[h]) for h in heads]
            Sn = [glast[h] * S[h] + _dot_tn(K[h] * dcol[h], vn[h]) for h in heads]
            for h in heads:
                sh_ref[ci, h] = S[h]
                vn_ref[rs, _hs(h)] = vn[h]
                oraw_ref[rs, _hs(h)] = O[h]
                rr = lax.rsqrt(jnp.mean(O[h] * O[h], axis=-1, keepdims=True) + EPS)
                zz = z_ref[rs, _hs(h)].astype(F32)
                og_ref[rs, _hs(h)] = (O[h] * rr * gn_ref[...] * (zz * _sigmoid(zz))).astype(BF16)
            S = Sn
        for h in heads:
            s_ref[h] = S[h]

    cspec = lambda w, cb=0: pl.BlockSpec((per * C, w), lambda n: (n, cb))
    hist = lambda a, b: pl.BlockSpec((per, N_HEAD, a, b), lambda n: (n, 0, 0, 0))
    return pl.pallas_call(
        body,
        out_shape=(jax.ShapeDtypeStruct((T, BR_W), BF16), jax.ShapeDtypeStruct((T, BR_W), F32),
                   jax.ShapeDtypeStruct((nc, N_HEAD, D_HEAD, D_HEAD), F32), jax.ShapeDtypeStruct((T, BR_W), F32)),
        grid=(nc // per,),
        in_specs=[cspec(BR_W), cspec(BR_W), cspec(BR_W), cspec(LANES), cspec(BR_W, zcol), _full((1, D_HEAD)),
                  hist(C, C)],
        out_specs=(cspec(BR_W), cspec(BR_W), hist(D_HEAD, D_HEAD), cspec(BR_W)),
        scratch_shapes=[pltpu.VMEM((N_HEAD, D_HEAD, D_HEAD), F32)], name="gdn_chunk_fwd",
        compiler_params=_cp("arbitrary"))(q, k, v, gb, proj, gnorm, tinv_all)


def _gdn_bwd(q, k, v, gb, proj, gnorm, oraw, shist, tinv_all, vn_all, dog, into):
    T = q.shape[0]
    C = GDN_CHUNK
    nc = T // C
    per = min(GDN_BWD_STEP_CHUNKS, nc)
    zcol = Z_OFF // BR_W

    def body(q_ref, k_ref, v_ref, gb_ref, z_ref, gn_ref, oraw_ref, sh_ref, tt_ref, vn_ref, dog_ref, into_ref,
             dq_ref, dk_ref, dv_ref, dgb_ref, dz_ref, dgn_ref, ds_ref):
        @pl.when(pl.program_id(0) == 0)
        def _():
            ds_ref[...] = jnp.zeros_like(ds_ref)
            dgn_ref[...] = jnp.zeros_like(dgn_ref)

        row, col, incl, strict = _chunk_consts()
        lane = _iota((C, LANES), 1)
        rowl = _iota((C, LANES), 0)
        eye = (row == col).astype(F32)
        upper = (col >= row).astype(F32)
        gn = gn_ref[...]
        heads = range(N_HEAD)
        rsum = lambda a: jnp.sum(a, axis=-1, keepdims=True)
        dgn = jnp.zeros((1, D_HEAD), F32)
        dSn = [ds_ref[h] for h in heads]
        for ci in reversed(range(per)):
            rs = slice(ci * C, (ci + 1) * C)
            gbv = gb_ref[rs, :]
            c_all, c_t = _chunk_decay(gbv, incl)
            dec = [_head_decay(c_all, c_t, gbv, incl, h) for h in heads]
            gam, gcol, dcol, glast, bcol = ([d[i] for d in dec] for i in range(5))
            Q = [q_ref[rs, _hs(h)] for h in heads]
            K = [k_ref[rs, _hs(h)] for h in heads]
            V = [v_ref[rs, _hs(h)] for h in heads]
            dO = []
            for h in heads:
                O = oraw_ref[rs, _hs(h)]
                zz = z_ref[rs, _hs(h)].astype(F32)
                dogv = dog_ref[rs, _hs(h)].astype(F32)
                rr = lax.rsqrt(jnp.mean(O * O, axis=-1, keepdims=True) + EPS)
                on = O * rr
                sg = _sigmoid(zz)
                dz_ref[rs, _hs(h)] = (dogv * on * gn * (sg * (1.0 + zz * (1.0 - sg)))).astype(BF16)
                dyn = dogv * (zz * sg)
                dgn = dgn + jnp.sum(dyn * on, axis=0, keepdims=True)
                dyv = dyn * gn
                dO.append((rr * (dyv - on * jnp.mean(dyv * on, axis=-1, keepdims=True))).astype(BF16))
            S = [sh_ref[ci, h] for h in heads]
            Sb = [s.astype(BF16) for s in S]
            vn = [vn_ref[rs, _hs(h)] for h in heads]
            vnb = [a.astype(BF16) for a in vn]
            dSb = [a.astype(BF16) for a in dSn]
            Kb = [K[h] * bcol[h] for h in heads]
            gam_t = [jnp.exp(jnp.where(col >= row, c_t[h:h + 1, :] - c_all[:, h:h + 1], -1e30)) for h in heads]
            M = [jnp.where(strict, _dot_nt(Kb[h], K[h]) * gam[h], 0.0) for h in heads]
            P = [_dot_nt(Q[h], K[h]) * gam[h] for h in heads]
            P_t = [_dot_nt(K[h], Q[h]) * gam_t[h] for h in heads]
            KS = [_dot(K[h], Sb[h]) for h in heads]
            QS = [_dot(Q[h], Sb[h]) for h in heads]
            dvn = [_dot(P_t[h], dO[h]) + _dot(K[h] * dcol[h], dSb[h]) for h in heads]
            dR = [_dot(tt_ref[ci, h], dvn[h]) for h in heads]
            dRb = [a.astype(BF16) for a in dR]
            bg = [bcol[h] * gcol[h] for h in heads]
            dS_new = [glast[h] * dSn[h] + _dot_tn(gcol[h] * Q[h], dO[h]) - _dot_tn(bg[h] * K[h], dRb[h])
                      for h in heads]
            dP = [jnp.where(incl, _dot_nt(dO[h], vnb[h]), 0.0) for h in heads]
            dM = [jnp.where(strict, -_dot_nt(dRb[h], vnb[h]), 0.0) for h in heads]
            dPG = [(dP[h] * gam[h]).astype(BF16) for h in heads]
            dMG = [(dM[h] * gam[h]).astype(BF16) for h in heads]
            dPG_t = [(jnp.where(col >= row, _dot_nt(vnb[h], dO[h]), 0.0) * gam_t[h]).astype(BF16) for h in heads]
            dMG_t = [(jnp.where(col > row, -_dot_nt(vnb[h], dRb[h]), 0.0) * gam_t[h]).astype(BF16) for h in heads]
            E = [_dot_nt(vnb[h], dSb[h]) for h in heads]
            dKb = [_dot(dMG[h], K[h]) for h in heads]
            dc_all = jnp.zeros((C, LANES), F32)
            db_all = jnp.zeros((C, LANES), F32)
            for h in heads:
                dq_ref[rs, _hs(h)] = gcol[h] * _dot_nt(dO[h], Sb[h]) + _dot(dPG[h], K[h])
                dk_ref[rs, _hs(h)] = (_dot(dPG_t[h], Q[h]) + _dot(dMG_t[h], Kb[h]) + bcol[h] * dKb[h]
                                      - bg[h] * _dot_nt(dRb[h], Sb[h]) + dcol[h] * E[h])
                dv_ref[rs, _hs(h)] = bcol[h] * dR[h]
                dbeta = rsum(dKb[h] * K[h]) + rsum(dR[h] * (V[h] - gcol[h] * KS[h]))
                X = dP[h] * P[h] + dM[h] * M[h]
                ddel = rsum(K[h] * E[h]) * dcol[h]
                colsum = rsum(eye * jnp.sum(X, axis=0, keepdims=True))
                dc = (rsum(X) - colsum + gcol[h] * rsum(dO[h].astype(F32) * QS[h]) - bg[h] * rsum(dR[h] * KS[h])
                      - ddel)
                last = (jnp.sum(ddel, axis=0, keepdims=True)
                        + glast[h] * jnp.sum(rsum(dSn[h] * S[h]), axis=0, keepdims=True))
                dc_all = dc_all + jnp.where(lane == h, dc + jnp.where(rowl == C - 1, last, 0.0), 0.0)
                db_all = db_all + jnp.where(lane == N_HEAD + h, dbeta, 0.0)
            dgb_ref[rs, :] = _dotf(upper, dc_all) + db_all
            dSn = dS_new
        for h in heads:
            ds_ref[h] = dSn[h]
        dgn_ref[...] += dgn

    nb = nc // per
    cspec = lambda w, cb=0: pl.BlockSpec((per * C, w), lambda n: (nb - 1 - n, cb))
    hist = lambda a, b: pl.BlockSpec((per, N_HEAD, a, b), lambda n: (nb - 1 - n, 0, 0, 0))
    return pl.pallas_call(
        body,
        out_shape=(jax.ShapeDtypeStruct((T, BR_W), F32),) * 3 + (
            jax.ShapeDtypeStruct((T, LANES), F32), jax.ShapeDtypeStruct(into.shape, BF16),
            jax.ShapeDtypeStruct((1, D_HEAD), F32)),
        grid=(nb,),
        in_specs=[cspec(BR_W), cspec(BR_W), cspec(BR_W), cspec(LANES), cspec(BR_W, zcol), _full((1, D_HEAD)),
                  cspec(BR_W), hist(D_HEAD, D_HEAD), hist(C, C), cspec(BR_W), cspec(BR_W), HBM],
        out_specs=(cspec(BR_W), cspec(BR_W), cspec(BR_W), cspec(LANES), cspec(BR_W, zcol), _full((1, D_HEAD))),
        scratch_shapes=[pltpu.VMEM((N_HEAD, D_HEAD, D_HEAD), F32)], input_output_aliases={11: 4},
        name="gdn_chunk_bwd",
        compiler_params=_cp("arbitrary"))(q, k, v, gb, proj, gnorm, oraw, shist, tinv_all, vn_all, dog, into)


SB_COL = SB_OFF // BR_W
SB_SCALE = D_HEAD ** -0.5


def _sb_pre(proj, gq, gk):
    T = proj.shape[0]
    tm = min(TM, T)

    def body(xq_ref, xk_ref, xv_ref, gq_ref, gk_ref, q_ref, k_ref, v_ref):
        for h in range(N_HEAD):
            for x_ref, g_ref, ref, scale in ((xq_ref, gq_ref, q_ref, SB_SCALE), (xk_ref, gk_ref, k_ref, 1.0)):
                xh = x_ref[:, _hs(h)].astype(F32)
                r = lax.rsqrt(jnp.mean(xh * xh, axis=-1, keepdims=True) + EPS)
                ref[:, _hs(h)] = (xh * (r * scale) * g_ref[...]).astype(BF16)
        v_ref[...] = xv_ref[...]

    return pl.pallas_call(
        body, out_shape=(jax.ShapeDtypeStruct((T, BR_W), BF16),) * 3, grid=(T // tm,),
        in_specs=[_rowspec(tm, BR_W, SB_COL), _rowspec(tm, BR_W, SB_COL + 1), _rowspec(tm, BR_W, SB_COL + 2),
                  _full((1, D_HEAD)), _full((1, D_HEAD))],
        out_specs=(_rowspec(tm, BR_W),) * 3, name="sb_pre", compiler_params=_cp("parallel"))(proj, proj, proj, gq, gk)


def _sb_pre_bwd(proj, gq, gk, dq, dk, dv, into):
    T = proj.shape[0]
    tm = min(TM, T)

    def body(xq_ref, xk_ref, gq_ref, gk_ref, dq_ref, dk_ref, dv_ref, into_ref, dx_ref, dgq_ref, dgk_ref):
        i = pl.program_id(0)

        @pl.when(i == 0)
        def _():
            dgq_ref[...] = jnp.zeros_like(dgq_ref)
            dgk_ref[...] = jnp.zeros_like(dgk_ref)

        for off, x_ref, g_ref, d_ref, dg_ref, scale in ((0, xq_ref, gq_ref, dq_ref, dgq_ref, SB_SCALE),
                                                        (BR_W, xk_ref, gk_ref, dk_ref, dgk_ref, 1.0)):
            dg = jnp.zeros((1, D_HEAD), F32)
            for h in range(N_HEAD):
                xh = x_ref[:, _hs(h)].astype(F32)
                r = lax.rsqrt(jnp.mean(xh * xh, axis=-1, keepdims=True) + EPS)
                y = xh * r
                dn = d_ref[:, _hs(h)] * scale
                dg = dg + jnp.sum(dn * y, axis=0, keepdims=True)
                dy = dn * g_ref[...]
                dx_ref[:, off + h * D_HEAD:off + (h + 1) * D_HEAD] = (
                    r * (dy - y * jnp.mean(dy * y, axis=-1, keepdims=True))).astype(BF16)
            dg_ref[...] += dg
        dx_ref[:, 2 * BR_W:] = dv_ref[...].astype(BF16)

    return pl.pallas_call(
        body,
        out_shape=(jax.ShapeDtypeStruct(into.shape, BF16), jax.ShapeDtypeStruct((1, D_HEAD), F32),
                   jax.ShapeDtypeStruct((1, D_HEAD), F32)),
        grid=(T // tm,),
        in_specs=[_rowspec(tm, BR_W, SB_COL), _rowspec(tm, BR_W, SB_COL + 1), _full((1, D_HEAD)), _full((1, D_HEAD)),
                  _rowspec(tm, BR_W), _rowspec(tm, BR_W), _rowspec(tm, BR_W), HBM],
        out_specs=(_rowspec(tm, 3 * BR_W, SB_OFF // (3 * BR_W)), _full((1, D_HEAD)), _full((1, D_HEAD))),
        input_output_aliases={7: 0}, name="sb_pre_bwd",
        compiler_params=_cp("arbitrary"))(proj, proj, gq, gk, dq, dk, dv, into)


def _sb_pair(q, k, masked):
    z = _dot_nt(q, k)
    zc = jnp.minimum(z, 30.0)
    sp = jnp.log(1.0 + jnp.exp(zc)) + (z - zc)
    if not masked:
        return z, sp, None
    mask = _iota(z.shape, 1) < _iota(z.shape, 0)
    return z, jnp.where(mask, sp, 0.0), mask


def _sb_fwd(sq, sk, sv):
    T = sq.shape[0]
    blk = min(SB_BLK, T)
    w = blk // 2
    nb = T // blk

    def body(q_ref, k_ref, v_ref, o_ref, lt_ref, cut_ref, acc_ref, r_ref):
        head, qi = pl.program_id(0), pl.program_id(1)
        acc_ref[...] = jnp.zeros_like(acc_ref)
        r_ref[...] = jnp.zeros_like(r_ref)
        after = (_iota((w, w), 0) > _iota((w, w), 1)).astype(BF16)

        def block(rows, kb, masked):
            keys = pl.ds(pl.multiple_of(kb * w, w), w)
            z, sp, mask = _sb_pair(q_ref[rows, :], k_ref[keys, :], masked)
            r = r_ref[rows, :]
            a = jnp.exp(z - sp - _dot(sp, after) - r)
            if masked:
                a = jnp.where(mask, a, 0.0)
            acc_ref[rows, :] += _dot(a, v_ref[keys, :])
            r_ref[rows, :] = r + jnp.sum(sp, axis=-1, keepdims=True)

        def alive():
            return jnp.min(r_ref[...]) < SB_DEAD

        def further(state):
            kb, _ = state
            block(slice(0, blk), kb, False)
            return kb - 1, alive()

        block(slice(w, blk), 2 * qi + 1, True)
        block(slice(0, blk), 2 * qi, True)
        left, _ = lax.while_loop(lambda s: jnp.logical_and(s[0] >= 0, s[1]), further, (2 * qi - 1, alive()))
        o_ref[...] = acc_ref[...].astype(BF16)
        lt_ref[0] = r_ref[...]
        cut_ref[head, qi] = (left + 1).astype(F32)

    qspec = pl.BlockSpec((blk, D_HEAD), lambda h, i: (i, h))
    whole = pl.BlockSpec((T, D_HEAD), lambda h, i: (0, h))
    return pl.pallas_call(
        body,
        out_shape=(jax.ShapeDtypeStruct((T, BR_W), BF16), jax.ShapeDtypeStruct((N_HEAD, T, 1), F32),
                   jax.ShapeDtypeStruct((N_HEAD, nb), F32)),
        grid=(N_HEAD, nb), in_specs=[qspec, whole, whole],
        out_specs=(qspec, pl.BlockSpec((1, blk, 1), lambda h, i: (h, i, 0)), pl.BlockSpec(memory_space=pltpu.SMEM)),
        scratch_shapes=[pltpu.VMEM((blk, D_HEAD), F32), pltpu.VMEM((blk, 1), F32)],
        name="sb_fwd", compiler_params=_cp("arbitrary", "arbitrary"))(sq, sk, sv)


def _sb_bwd(sq, sk, sv, ltot, do, cut):
    T = sq.shape[0]
    blk = min(SB_BLK, T)
    w = blk // 2
    nb = T // blk

    def body(q_ref, k_ref, v_ref, lt_ref, do_ref, cut_ref, dq_ref, dk_ref, dv_ref, acc_ref, p_ref, g_ref):
        head, qi = pl.program_id(0), pl.program_id(1)
        first = cut_ref[head, qi].astype(jnp.int32)

        @pl.when(qi == 0)
        def _():
            dk_ref[...] = jnp.zeros_like(dk_ref)
            dv_ref[...] = jnp.zeros_like(dv_ref)

        acc_ref[...] = jnp.zeros_like(acc_ref)
        p_ref[...] = lt_ref[0]
        g_ref[...] = jnp.zeros_like(g_ref)
        after = (_iota((w, w), 0) > _iota((w, w), 1)).astype(BF16)
        before = (_iota((w, w), 0) < _iota((w, w), 1)).astype(BF16)

        def block(rows, kb, masked):
            keys = pl.ds(pl.multiple_of(kb * w, w), w)
            q, do = q_ref[rows, :], do_ref[rows, :]
            z, sp, mask = _sb_pair(q, k_ref[keys, :], masked)
            d_a = _dot_nt(do, v_ref[keys, :])
            rest = p_ref[rows, :] - jnp.sum(sp, axis=-1, keepdims=True)
            a = jnp.exp(z - sp - _dot(sp, after) - rest)
            if masked:
                a = jnp.where(mask, a, 0.0)
            g = a * d_a
            sig = jnp.exp(z - sp)
            dz = g - sig * (g + (g_ref[rows, :] + _dot(g, before)))
            if masked:
                dz = jnp.where(mask, dz, 0.0)
            dz = dz.astype(BF16)
            dv_ref[keys, :] += _dot_tn(a, do)
            dk_ref[keys, :] += _dot_tn(dz, q)
            acc_ref[rows, :] += _dot(dz, k_ref[keys, :])
            p_ref[rows, :] = rest
            g_ref[rows, :] += jnp.sum(g, axis=-1, keepdims=True)

        def step(kb, carry):
            block(slice(0, blk), kb, False)
            return carry

        lax.fori_loop(first, 2 * qi, step, 0)
        block(slice(0, blk), 2 * qi, True)
        block(slice(w, blk), 2 * qi + 1, True)
        dq_ref[...] = acc_ref[...]

    qspec = pl.BlockSpec((blk, D_HEAD), lambda h, i: (i, h))
    whole = pl.BlockSpec((T, D_HEAD), lambda h, i: (0, h))
    return pl.pallas_call(
        body, out_shape=(jax.ShapeDtypeStruct((T, BR_W), F32),) * 3, grid=(N_HEAD, nb),
        in_specs=[qspec, whole, whole, pl.BlockSpec((1, blk, 1), lambda h, i: (h, i, 0)), qspec,
                  pl.BlockSpec(memory_space=pltpu.SMEM)],
        out_specs=(qspec, whole, whole),
        scratch_shapes=[pltpu.VMEM((blk, D_HEAD), F32), pltpu.VMEM((blk, 1), F32), pltpu.VMEM((blk, 1), F32)],
        name="sb_bwd", compiler_params=_cp("arbitrary", "arbitrary"))(sq, sk, sv, ltot, do, cut)


def _mem_kv(mem, gm, w_kv, gk):
    def body(mem_ref, gm_ref, w_ref, gk_ref, mn_ref, kv_ref, kh_ref, vm_ref):
        mv = mem_ref[...]
        r = lax.rsqrt(jnp.mean(mv * mv, axis=-1, keepdims=True) + EPS)
        mn = (mv * r * gm_ref[...]).astype(BF16)
        mn_ref[...] = mn
        kv = lax.dot_general(mn, w_ref[...], _NN, preferred_element_type=F32)
        kv_ref[...] = kv
        for h in range(N_HEAD):
            kh = kv[:, _hs(h)]
            rk = lax.rsqrt(jnp.mean(kh * kh, axis=-1, keepdims=True) + EPS)
            kh_ref[:, _hs(h)] = (kh * rk * gk_ref[...]).astype(BF16)
        vm_ref[...] = kv[:, BR_W:].astype(BF16)

    return pl.pallas_call(
        body,
        out_shape=(jax.ShapeDtypeStruct((N_MEM, D_MODEL), BF16), jax.ShapeDtypeStruct((N_MEM, 2 * BR_W), F32),
                   jax.ShapeDtypeStruct((N_MEM, BR_W), BF16), jax.ShapeDtypeStruct((N_MEM, BR_W), BF16)),
        name="mem_kv", compiler_params=_cp())(mem, gm, w_kv, gk)


def _mem_q(x_ref, gq_ref, h):
    xh = x_ref[:, _hs(h)].astype(F32)
    r = lax.rsqrt(jnp.mean(xh * xh, axis=-1, keepdims=True) + EPS)
    return r, xh * r


def _mem_probs(qn, kh):
    s = _dot_nt(qn, kh) * (D_HEAD ** -0.5)
    e = jnp.exp(s - jnp.max(s, axis=-1, keepdims=True))
    return e / jnp.sum(e, axis=-1, keepdims=True)


def _mem_fwd(proj, kh, vm, gq):
    T = proj.shape[0]
    tm = min(TM, T)

    def body(x_ref, kh_ref, vm_ref, gq_ref, o_ref):
        for h in range(N_HEAD):
            _, y = _mem_q(x_ref, gq_ref, h)
            p = _mem_probs((y * gq_ref[...]).astype(BF16), kh_ref[:, _hs(h)])
            o_ref[:, _hs(h)] = _dot(p, vm_ref[:, _hs(h)]).astype(BF16)

    return pl.pallas_call(
        body, out_shape=jax.ShapeDtypeStruct((T, BR_W), BF16), grid=(T // tm,),
        in_specs=[_rowspec(tm, BR_W, MEMQ_OFF // BR_W), _full((N_MEM, BR_W)), _full((N_MEM, BR_W)),
                  _full((1, D_HEAD))],
        out_specs=_rowspec(tm, BR_W), name="mem_fwd", compiler_params=_cp("parallel"))(proj, kh, vm, gq)


def _mem_bwd(proj, kh, vm, gq, do, into):
    T = proj.shape[0]
    tm = min(TM, T)

    def body(x_ref, kh_ref, vm_ref, gq_ref, do_ref, into_ref, dx_ref, dkh_ref, dvm_ref, dgq_ref):
        i = pl.program_id(0)

        @pl.when(i == 0)
        def _():
            dkh_ref[...] = jnp.zeros_like(dkh_ref)
            dvm_ref[...] = jnp.zeros_like(dvm_ref)
            dgq_ref[...] = jnp.zeros_like(dgq_ref)

        dg = jnp.zeros((1, D_HEAD), F32)
        for h in range(N_HEAD):
            r, y = _mem_q(x_ref, gq_ref, h)
            qn = (y * gq_ref[...]).astype(BF16)
            p = _mem_probs(qn, kh_ref[:, _hs(h)])
            dov = do_ref[:, _hs(h)]
            dp = _dot_nt(dov, vm_ref[:, _hs(h)])
            ds = p * (dp - jnp.sum(dp * p, axis=-1, keepdims=True)) * (D_HEAD ** -0.5)
            dqn = _dot(ds, kh_ref[:, _hs(h)])
            dkh_ref[:, _hs(h)] += _dot_tn(ds, qn)
            dvm_ref[:, _hs(h)] += _dot_tn(p, dov)
            dg = dg + jnp.sum(dqn * y, axis=0, keepdims=True)
            dy = dqn * gq_ref[...]
            dx_ref[:, _hs(h)] = (r * (dy - y * jnp.mean(dy * y, axis=-1, keepdims=True))).astype(BF16)
        dgq_ref[...] += dg

    return pl.pallas_call(
        body,
        out_shape=(jax.ShapeDtypeStruct(into.shape, BF16), jax.ShapeDtypeStruct((N_MEM, BR_W), F32),
                   jax.ShapeDtypeStruct((N_MEM, BR_W), F32), jax.ShapeDtypeStruct((1, D_HEAD), F32)),
        grid=(T // tm,),
        in_specs=[_rowspec(tm, BR_W, MEMQ_OFF // BR_W), _full((N_MEM, BR_W)), _full((N_MEM, BR_W)),
                  _full((1, D_HEAD)), _rowspec(tm, BR_W), HBM],
        out_specs=(_rowspec(tm, BR_W, MEMQ_OFF // BR_W), _full((N_MEM, BR_W)), _full((N_MEM, BR_W)),
                   _full((1, D_HEAD))),
        input_output_aliases={5: 0}, name="mem_bwd", compiler_params=_cp("arbitrary"))(proj, kh, vm, gq, do, into)


def _mem_kv_bwd(mem, gm, w_kv, gk, kv, mn, dkh, dvm):
    def body(mem_ref, gm_ref, w_ref, gk_ref, kv_ref, mn_ref, dkh_ref, dvm_ref, dw_ref, dgm_ref, dgk_ref, dkv_ref):
        dgk = jnp.zeros((1, D_HEAD), F32)
        for h in range(N_HEAD):
            kh = kv_ref[:, _hs(h)]
            r = lax.rsqrt(jnp.mean(kh * kh, axis=-1, keepdims=True) + EPS)
            y = kh * r
            dn = dkh_ref[:, _hs(h)]
            dgk = dgk + jnp.sum(dn * y, axis=0, keepdims=True)
            dy = dn * gk_ref[...]
            dkv_ref[:, _hs(h)] = (r * (dy - y * jnp.mean(dy * y, axis=-1, keepdims=True))).astype(BF16)
        dkv_ref[:, BR_W:] = dvm_ref[...].astype(BF16)
        dgk_ref[...] = dgk
        dkv = dkv_ref[...]
        dw_ref[...] = lax.dot_general(mn_ref[...], dkv, _TN, preferred_element_type=F32)
        dmn = lax.dot_general(dkv, w_ref[...], _NT, preferred_element_type=F32)
        mv = mem_ref[...]
        memn = mv * lax.rsqrt(jnp.mean(mv * mv, axis=-1, keepdims=True) + EPS)
        dgm_ref[...] = jnp.sum(dmn * memn, axis=0, keepdims=True)

    return pl.pallas_call(
        body,
        out_shape=(jax.ShapeDtypeStruct((D_MODEL, 2 * BR_W), F32), jax.ShapeDtypeStruct((1, D_MODEL), F32),
                   jax.ShapeDtypeStruct((1, D_HEAD), F32)),
        scratch_shapes=[pltpu.VMEM((N_MEM, 2 * BR_W), BF16)], name="mem_kv_bwd",
        compiler_params=_cp())(mem, gm, w_kv, gk, kv, mn, dkh, dvm)


def _merge_fwd(og, osb, om, proj, wg, ws, wm):
    T = og.shape[0]
    tm = min(TM, T)

    def body(og_ref, os_ref, om_ref, g0, g1, g2, wg_ref, ws_ref, wm_ref, mix_ref, yg_ref, ys_ref, ym_ref):
        mix = jnp.zeros((tm, D_MODEL), F32)
        for o_ref, gl_ref, w_ref, y_ref in ((og_ref, g0, wg_ref, yg_ref), (os_ref, g1, ws_ref, ys_ref),
                                            (om_ref, g2, wm_ref, ym_ref)):
            y = lax.dot_general(o_ref[...], w_ref[...], _NN, preferred_element_type=F32)
            y_ref[...] = y.astype(BF16)
            mix = mix + _sigmoid(gl_ref[...].astype(F32)) * y
        mix_ref[...] = mix.astype(BF16)

    br = _rowspec(tm, BR_W)
    wspec = _full((BR_W, D_MODEL))
    out = _rowspec(tm, D_MODEL)
    gates = [_rowspec(tm, D_MODEL, GATE_COL + b) for b in range(3)]
    return pl.pallas_call(
        body, out_shape=(jax.ShapeDtypeStruct((T, D_MODEL), BF16),) * 4, grid=(T // tm,),
        in_specs=[br, br, br, *gates, wspec, wspec, wspec],
        out_specs=(out,) * 4, name="merge_fwd",
        compiler_params=_cp("parallel"))(og, osb, om, proj, proj, proj, wg, ws, wm)


def _merge_bwd(dmix, proj, ys, os_, ws):
    T = dmix.shape[0]
    tm = min(TM, T)

    def body(dmix_ref, g0, g1, g2, y0, y1, y2, o0, o1, o2, w0, w1, w2, dgl_ref, do0, do1, do2, dw0, dw1, dw2):
        i = pl.program_id(0)
        dm = dmix_ref[...].astype(F32)
        for b, (gl_ref, y_ref, o_ref, w_ref, do_ref, dw_ref) in enumerate((
                (g0, y0, o0, w0, do0, dw0), (g1, y1, o1, w1, do1, dw1), (g2, y2, o2, w2, do2, dw2))):
            gate = _sigmoid(gl_ref[...].astype(F32))
            dgl_ref[:, b * D_MODEL:(b + 1) * D_MODEL] = (dm * y_ref[...].astype(F32) * gate * (1.0 - gate)).astype(BF16)
            dy = (gate * dm).astype(BF16)
            do_ref[...] = lax.dot_general(dy, w_ref[...], _NT, preferred_element_type=F32).astype(BF16)
            _accum(dw_ref, i == 0, lax.dot_general(dy, o_ref[...], _TN, preferred_element_type=F32))

    br = _rowspec(tm, BR_W)
    wide = _rowspec(tm, D_MODEL)
    wspec = _full((BR_W, D_MODEL))
    wtspec = _full((D_MODEL, BR_W))
    gates = [_rowspec(tm, D_MODEL, GATE_COL + b) for b in range(3)]
    return pl.pallas_call(
        body,
        out_shape=(jax.ShapeDtypeStruct((T, PROJ_W), BF16),) + (jax.ShapeDtypeStruct((T, BR_W), BF16),) * 3
        + (jax.ShapeDtypeStruct((D_MODEL, BR_W), F32),) * 3,
        grid=(T // tm,),
        in_specs=[wide, *gates, wide, wide, wide, br, br, br, wspec, wspec, wspec],
        out_specs=(_rowspec(tm, 3 * D_MODEL, GATE_OFF // (3 * D_MODEL)), br, br, br, wtspec, wtspec, wtspec),
        name="merge_bwd",
        compiler_params=_cp("arbitrary"))(dmix, proj, proj, proj, *ys, *os_, *ws)


def _loss(y, tgt):
    T, dm = y.shape
    tm = min(TM, T)

    def body(y_ref, t_ref, dy_ref, dyb_ref, sq_ref):
        err = y_ref[...] - t_ref[...]
        dy = err * (1.0 / dm)
        dy_ref[...] = dy
        dyb_ref[...] = dy.astype(BF16)
        _accum(sq_ref, pl.program_id(0) == 0, jnp.sum(err * err, axis=0, keepdims=True))

    return pl.pallas_call(
        body,
        out_shape=(jax.ShapeDtypeStruct((T, dm), F32), jax.ShapeDtypeStruct((T, dm), BF16),
                   jax.ShapeDtypeStruct((1, dm), F32)),
        grid=(T // tm,), in_specs=[_rowspec(tm, dm), _rowspec(tm, dm)],
        out_specs=(_rowspec(tm, dm), _rowspec(tm, dm), _full((1, dm))), name="loss",
        compiler_params=_cp("arbitrary"))(y, tgt)


def _split_w_in(slabs):
    width = slabs[0].shape[1]

    def cols(lo, hi):
        return [s[:, max(lo - j * width, 0):min(hi - j * width, width)] for j, s in enumerate(slabs)
                if lo < (j + 1) * width and hi > j * width]

    main = [c for piece in (IN_GATE, IN_QKV, IN_SB, IN_Z, IN_MEMQ) for c in cols(*piece)]
    ab = jnp.concatenate(cols(*IN_AB), axis=1)
    return jnp.concatenate(main, axis=1), jnp.pad(ab, ((0, 0), (0, LANES - ab.shape[1])))


def _local_step(x, mem, tgt, W, P, dist=None):
    w_main, w_ab = W["w_main"], W["w_ab"]
    avec = jnp.pad(jnp.concatenate([P["a_log"], P["dt_bias"]], axis=0), ((0, 0), (0, LANES - N_HEAD)))

    h = _rms_fwd(x, P["norm1_g"], "rms1")
    if dist is None:
        proj = _mm(h, w_main, "nn", BF16, "in_proj", tn=1792)
    else:
        proj, gathered = _mm(h, w_main, "nn", BF16, "in_proj", tn=1792, comm=dist.gather_rest())
        rest, conv_w = dist.weights_from(gathered)
        W, P = {**W, **rest}, {**P, "conv_w": conv_w}
    wbr = (W["w_br_gdn"], W["w_br_sb"], W["w_br_mem"])
    ab = _mm(h, w_ab, "nn", F32, "in_proj_ab")
    q, k, v, gb = _gdn_pre(proj, P["conv_w"], ab, avec)
    tinv, tinv_t = _gdn_inv(k, gb)
    og, oraw, shist, vn = _gdn_fwd(q, k, v, gb, proj, P["gdn_norm_g"], tinv)
    sq, sk, sv = _sb_pre(proj, P["sb_q_norm_g"], P["sb_k_norm_g"])
    osb, ltot, cut = _sb_fwd(sq, sk, sv)
    mn, kv, kh, vm = _mem_kv(mem, P["mem_norm_g"], W["w_mem_kv"], P["mem_k_norm_g"])
    om = _mem_fwd(proj, kh, vm, P["mem_q_norm_g"])
    mix, yg, ys, ym = _merge_fwd(og, osb, om, proj, *wbr)
    x1 = _mm(mix, W["w_o"], "nn", F32, "out_proj", extra=x, epi=_epi_add)
    h2 = _rms_fwd(x1, P["norm2_g"], "rms2")
    if dist is None:
        u = _mm(h2, W["w_up"], "nn", BF16, "mlp_up", tn=2048)
    else:
        u, gathered = _mm(h2, W["w_up"], "nn", BF16, "mlp_up", tn=2048, comm=dist.gather_late())
        W = {**W, **dist.late_weights_from(gathered)}
    y = _mm(u, W["w_down"], "nn", F32, "mlp_down", a_fn=_relu2, extra=x1, epi=_epi_add)
    dy, dyb, sq_err = _loss(y, tgt)

    G = {}
    du = _mm(dyb, W["w_down"], "nt", BF16, "d_mlp_act", tn=2048, extra=u, epi=_epi_drelu2)
    G["w_down"] = _mm(u, dyb, "tn", F32, "dw_down", a_fn=_relu2)
    G["w_up"] = _mm(du, h2, "tn", F32, "dw_up")
    dh2 = _mm(du, W["w_up"], "nt", F32, "d_h2")
    dx1, dx1b, G["norm2_g"] = _rms_bwd(dh2, x1, P["norm2_g"], dy, "rms2_bwd")
    if dist is None:
        dmix = _mm(dx1b, W["w_o"], "nt", BF16, "d_mix")
    else:
        mlp = ["w_up", "w_down"]
        dmix, got = _mm(dx1b, W["w_o"], "nt", BF16, "d_mix", comm=dist.swap_behind(G, mlp))
        dist.swapped(mlp, got)
    G["w_o"] = _mm(mix, dx1b, "tn", F32, "dw_o")
    dproj, dog, dosb, dom, G["w_br_gdn"], G["w_br_sb"], G["w_br_mem"] = _merge_bwd(
        dmix, proj, (yg, ys, ym), (og, osb, om), wbr)
    dq, dk, dv, dgb, dproj, G["gdn_norm_g"] = _gdn_bwd(q, k, v, gb, proj, P["gdn_norm_g"], oraw, shist, tinv_t, vn, dog,
                                                      dproj)
    dxc, dab, G["conv_w"], dav = _gdn_pre_bwd(proj, P["conv_w"], ab, avec, dq, dk, dv, dgb)
    dproj = _conv_bwd(dxc, P["conv_w"], dproj)
    G["a_log"], G["dt_bias"] = dav[0:1, :N_HEAD], dav[1:2, :N_HEAD]
    dsq, dsk, dsv = _sb_bwd(sq, sk, sv, ltot, dosb, cut)
    dproj, G["sb_q_norm_g"], G["sb_k_norm_g"] = _sb_pre_bwd(proj, P["sb_q_norm_g"], P["sb_k_norm_g"], dsq, dsk, dsv,
                                                            dproj)
    dproj, dkh, dvm, G["mem_q_norm_g"] = _mem_bwd(proj, kh, vm, P["mem_q_norm_g"], dom, dproj)
    G["w_mem_kv"], G["mem_norm_g"], G["mem_k_norm_g"] = _mem_kv_bwd(
        mem, P["mem_norm_g"], W["w_mem_kv"], P["mem_k_norm_g"], kv, mn, dkh, dvm)
    dw_ab = _mm(dab, h, "tn", F32, "dw_in_ab")
    if dist is None:
        dw_main = _mm(dproj, h, "tn", F32, "dw_in")
    else:
        early = [n for n, _, _ in BIG if n != "w_in"]
        dw_main, landed = _mm(dproj, h, "tn", F32, "dw_in", comm=dist.scatter(G, early, "early"))
        dist.collect(early, landed)
    G["w_in"] = jnp.concatenate([dw_main[QKV_OFF:SB_OFF], dw_main[Z_OFF:MEMQ_OFF], dw_ab[:8], dw_main[SB_OFF:Z_OFF],
                                 dw_main[MEMQ_OFF:], dw_main[:QKV_OFF]], axis=0)
    if dist is None:
        dh = _mm(dproj, w_main, "nt", F32, "d_h")
    else:
        dh, landed = _mm(dproj, w_main, "nt", F32, "d_h", comm=dist.scatter(G, ["w_in"], "late"))
        dist.collect(["w_in"], landed)
    dh = _mm(dab, w_ab, "nt", F32, "d_h_ab", extra=dh, epi=_epi_add)
    dx, _, G["norm1_g"] = _rms_bwd(dh, x, P["norm1_g"], dx1, "rms1_bwd")
    return sq_err, dx, G


def _comm(name, ins, out_shapes, plan):
    n_in, n_out = len(ins), len(out_shapes)
    probe = plan([None] * n_in, [None] * n_out, 0, 0, 0, dry=True)
    n_copy = probe

    def body(*refs):
        in_refs, out_refs = refs[:n_in], refs[n_in:n_in + n_out]
        send_sems, recv_sems = refs[n_in + n_out:]
        x, y, c = lax.axis_index("x"), lax.axis_index("y"), lax.axis_index("c")
        copies = []
        for k, (src, dst, dev) in enumerate(plan(in_refs, out_refs, x, y, c, dry=False)):
            if dev is None:
                cp = pltpu.make_async_copy(src, dst, send_sems.at[k])
            else:
                cp = pltpu.make_async_remote_copy(src_ref=src, dst_ref=dst, send_sem=send_sems.at[k],
                                                  recv_sem=recv_sems.at[k], device_id=dev, device_id_type=MESH)
            cp.start()
            copies.append(cp)
        for cp in copies:
            cp.wait()

    return pl.pallas_call(
        body, out_shape=tuple(out_shapes), in_specs=[HBM] * n_in, out_specs=tuple([HBM] * n_out),
        scratch_shapes=[pltpu.SemaphoreType.DMA((n_copy,)), pltpu.SemaphoreType.DMA((n_copy,))], name=name)(*ins)


def _other_chips(x, y):
    return ((1 - x, y), (x, 1 - y), (1 - x, 1 - y))


def _gather_plan(parts, direct=()):
    n, every = len(parts), list(parts) + list(direct)

    def copies(ins, outs, send, recv, scratch):
        x, y, c = lax.axis_index("x"), lax.axis_index("y"), lax.axis_index("c")
        me = 2 * x + y
        chips = _other_chips(x, y)
        local_sems, staged = scratch[0], scratch[1:]

        def remote(src, dst, k, dev):
            return pltpu.make_async_remote_copy(src_ref=src, dst_ref=dst, send_sem=send.at[k], recv_sem=recv.at[k],
                                                device_id=dev, device_id_type=MESH)

        def half(p, ci):
            hr = ins[p].shape[0] // 2
            return pl.ds(pl.multiple_of(ci * hr, 16), hr)

        sent = [remote(ins[p].at[half(p, c)], outs[p].at[me, half(p, c)], 6 * p + f, (px, py, c))
                for p in range(n) for f, (px, py) in enumerate(chips)]
        sent += [remote(ins[p], outs[p].at[me], 6 * n + 3 * (p - n) + f, (px, py, c))
                 for p in range(n, len(every)) for f, (px, py) in enumerate(chips)]
        landed = [outs[p].at[2 * px + py, half(p, c)] for p in range(n) for px, py in chips]
        passed = [remote(landed[3 * p + f], landed[3 * p + f], 6 * p + 3 + f, (x, y, 1 - c))
                  for p in range(n) for f in range(3)]
        loads = [pltpu.make_async_copy(ins[p], staged[p], local_sems.at[2 * p]) for p in range(len(every))]
        stores = [pltpu.make_async_copy(staged[p], outs[p].at[me], local_sems.at[2 * p + 1]) for p in range(len(every))]
        return sent, passed, loads, stores

    def start(*refs):
        sent, _, loads, _ = copies(*refs)
        for cp in loads + sent:
            cp.start()

    def mid(*refs):
        sent, passed, loads, stores = copies(*refs)
        for ld, st in zip(loads, stores):
            ld.wait()
            st.start()
        for p in range(n):
            for f in range(3):
                sent[3 * p + f].wait_recv()
                passed[3 * p + f].start()

    def finish(*refs):
        sent, passed, _, stores = copies(*refs)
        for cp in sent[:3 * n]:
            cp.wait_send()
        for cp in passed + sent[3 * n:] + stores:
            cp.wait()

    return _Hosted(every, [jax.ShapeDtypeStruct((4,) + p.shape, p.dtype) for p in every], 6 * n + 3 * len(direct),
                   start, finish, mid,
                   [pltpu.SemaphoreType.DMA((2 * len(every),))] + [pltpu.VMEM(p.shape, p.dtype) for p in every])


def _scatter_plan(pairs):
    def copies(ins, outs, send, recv, scratch):
        x, y, c = lax.axis_index("x"), lax.axis_index("y"), lax.axis_index("c")
        me = 2 * x + y
        return [pltpu.make_async_remote_copy(src_ref=src.at[2 * px + py], dst_ref=dst.at[me], send_sem=send.at[3 * p + f],
                                             recv_sem=recv.at[3 * p + f], device_id=(px, py, c), device_id_type=MESH)
                for p, (src, dst) in enumerate(zip(ins, outs)) for f, (px, py) in enumerate(_other_chips(x, y))]

    def start(*refs):
        for cp in copies(*refs):
            cp.start()

    def finish(*refs):
        for cp in copies(*refs):
            cp.wait()

    return _Hosted(pairs, [jax.ShapeDtypeStruct(a.shape, a.dtype) for a in pairs], 3 * len(pairs), start, finish)


def _run_hosted(comm, name):
    n_in, n_out = len(comm.ins), len(comm.out_shapes)

    def body(*refs):
        args = (refs[:n_in], refs[n_in:n_in + n_out], refs[n_in + n_out], refs[n_in + n_out + 1], refs[n_in + n_out + 2:])
        comm.start(*args)
        if comm.mid is not None:
            comm.mid(*args)
        comm.finish(*args)

    sems = [pltpu.SemaphoreType.DMA((comm.n_sems,)), pltpu.SemaphoreType.DMA((comm.n_sems,))]
    return list(pl.pallas_call(
        body, out_shape=tuple(comm.out_shapes), in_specs=[HBM] * n_in, out_specs=tuple([HBM] * n_out),
        scratch_shapes=sems + comm.scratch, name=name, compiler_params=_cp())(*comm.ins))


def _swap_halves(slabs, name):
    n = len(slabs)

    def body(*refs):
        ins, outs = refs[:n], refs[n:2 * n]
        send, recv = refs[2 * n:]
        x, y, c = lax.axis_index("x"), lax.axis_index("y"), lax.axis_index("c")
        other = (x, y, 1 - c)
        for p in range(n):
            for j in range(4):
                pltpu.make_async_remote_copy(src_ref=ins[p].at[j, 1 - c], dst_ref=outs[p].at[j], send_sem=send.at[p],
                                             recv_sem=recv.at[p], device_id=other, device_id_type=MESH).start()
        for p in range(n):
            pltpu.make_async_remote_copy(src_ref=outs[p], dst_ref=outs[p], send_sem=send.at[p], recv_sem=recv.at[p],
                                         device_id=other, device_id_type=MESH).wait()

    shapes = [jax.ShapeDtypeStruct((4,) + s.shape[2:], s.dtype) for s in slabs]
    return pl.pallas_call(
        body, out_shape=tuple(shapes), in_specs=[HBM] * n, out_specs=tuple([HBM] * n),
        scratch_shapes=[pltpu.SemaphoreType.DMA((n,)), pltpu.SemaphoreType.DMA((n,))], name=name)(*slabs)


def _join_halves(both):
    n = len(both)

    def body(*refs):
        bufs = refs[n:2 * n]
        send, recv = refs[2 * n:]
        x, y, c = lax.axis_index("x"), lax.axis_index("y"), lax.axis_index("c")
        copies = []
        for p in range(n):
            cp = pltpu.make_async_remote_copy(src_ref=bufs[p].at[c], dst_ref=bufs[p].at[c], send_sem=send.at[p],
                                              recv_sem=recv.at[p], device_id=(x, y, 1 - c), device_id_type=MESH)
            cp.start()
            copies.append(cp)
        for cp in copies:
            cp.wait()

    return pl.pallas_call(
        body, out_shape=tuple(jax.ShapeDtypeStruct(a.shape, a.dtype) for a in both), in_specs=[HBM] * n,
        out_specs=tuple([HBM] * n), input_output_aliases={p: p for p in range(n)},
        scratch_shapes=[pltpu.SemaphoreType.DMA((n,)), pltpu.SemaphoreType.DMA((n,))], name="grad_join_cores")(*both)


def _gather_all(a, name):
    def plan(ins, outs, x, y, c, dry):
        if dry:
            return 8
        me = 4 * x + 2 * y + c
        copies = [(ins[0], outs[0].at[me], None)]
        for f in range(1, 8):
            peer = (1 - x if f & 4 else x, 1 - y if f & 2 else y, 1 - c if f & 1 else c)
            copies.append((ins[0], outs[0].at[me], peer))
        return copies

    return _comm(name, [a], [jax.ShapeDtypeStruct((8,) + a.shape, a.dtype)], plan)[0]


def _sum_slots(a, name, extra=None):
    n, R, _ = a.shape
    rb = min(ROW_BLK, R)

    def body(*refs):
        a_ref, o_ref = refs[0], refs[-1]
        acc = a_ref[0]
        for s in range(1, n):
            acc = acc + a_ref[s]
        if extra is not None:
            acc = acc + refs[1][...]
        o_ref[...] = acc

    ins = [a] + ([extra] if extra is not None else [])
    in_specs = [pl.BlockSpec((n, rb, LANES), lambda i: (0, i, 0))] + ([_rowspec(rb, LANES)] if extra is not None else [])
    return pl.pallas_call(
        body, out_shape=jax.ShapeDtypeStruct((R, LANES), F32), grid=(R // rb,), in_specs=in_specs,
        out_specs=_rowspec(rb, LANES), name=name, compiler_params=_cp("parallel"))(*ins)


def _pair_sum(slab, theirs, core, name):
    _, _, hr, C = slab.shape

    def body(c_ref, a_ref, b_ref, o_ref):
        o_ref[...] = (a_ref[...] + b_ref[...]).astype(BF16)

    return pl.pallas_call(
        body, out_shape=jax.ShapeDtypeStruct((4, hr, C), BF16),
        grid_spec=pltpu.PrefetchScalarGridSpec(
            num_scalar_prefetch=1, grid=(4,),
            in_specs=[pl.BlockSpec((None, None, hr, C), lambda j, c_ref: (j, c_ref[0], 0, 0)),
                      pl.BlockSpec((None, hr, C), lambda j, c_ref: (j, 0, 0))],
            out_specs=pl.BlockSpec((None, hr, C), lambda j, c_ref: (j, 0, 0))),
        name=name, compiler_params=_cp("parallel"))(core, slab, theirs)


def _chip_sum(recv, pairs, where, name):
    _, hr, C = recv.shape

    def body(w_ref, r_ref, p_ref, o_ref):
        me = w_ref[0]
        o_ref[...] = jnp.zeros_like(o_ref)
        for s in range(4):
            @pl.when(me == s)
            def _():
                o_ref[...] += p_ref[...].astype(F32)

            @pl.when(me != s)
            def _():
                o_ref[...] += r_ref[s].astype(F32)

    return pl.pallas_call(
        body, out_shape=jax.ShapeDtypeStruct((2, hr, C), F32),
        grid_spec=pltpu.PrefetchScalarGridSpec(
            num_scalar_prefetch=1, grid=(1,),
            in_specs=[pl.BlockSpec((4, hr, C), lambda i, w_ref: (0, 0, 0)),
                      pl.BlockSpec((None, hr, C), lambda i, w_ref: (w_ref[0], 0, 0))],
            out_specs=pl.BlockSpec((None, hr, C), lambda i, w_ref: (w_ref[1], 0, 0))),
        name=name, compiler_params=_cp("arbitrary"))(where, recv, pairs)


def _adamw(w, g, m, v, name):
    R, C = w.shape
    rb = min(ADAM_ROWS, R)
    c1 = 1.0 - ADAM_B1 ** ADAM_STEP
    c2 = 1.0 - ADAM_B2 ** ADAM_STEP

    def body(w_ref, g_ref, m_ref, v_ref, d_ref, nm_ref, nv_ref):
        gv = g_ref[...]
        nm = ADAM_B1 * m_ref[...] + (1.0 - ADAM_B1) * gv
        nv = ADAM_B2 * v_ref[...] + (1.0 - ADAM_B2) * (gv * gv)
        d_ref[...] = -ADAM_LR * ((nm / c1) / (jnp.sqrt(nv / c2) + ADAM_EPS) + ADAM_WD * w_ref[...])
        nm_ref[...] = nm
        nv_ref[...] = nv

    spec = _rowspec(rb, C)
    return pl.pallas_call(
        body, out_shape=(jax.ShapeDtypeStruct((R, C), F32),) * 3, grid=(R // rb,), in_specs=[spec] * 4,
        out_specs=(spec,) * 3, name=name, compiler_params=_cp("parallel"))(w, g, m, v)


class _Dist:
    def __init__(self, shards):
        self.shards = shards
        self.chip = 2 * lax.axis_index("x") + lax.axis_index("y")
        self.where = jnp.stack([self.chip, lax.axis_index("c")]).astype(jnp.int32)
        self.pairs, self.landed, self.theirs = {}, {}, {}

    @staticmethod
    def _unshard(name, blk):
        _, (r, cc), axis = next(b for b in BIG if b[0] == name)
        return blk.reshape(4 * r, cc) if axis == 0 else blk.transpose(1, 0, 2).reshape(r, 4 * cc)

    def gather_first(self):
        got = _run_hosted(_gather_plan([self.shards["w_in"].astype(BF16)]), "gather_w_in")[0]
        return _split_w_in([got[j] for j in range(4)])

    LATE = ("w_down",)

    def gather_rest(self):
        rest = [self.shards[n].astype(BF16) for n, _, _ in BIG if n != "w_in" and n not in self.LATE]
        return _gather_plan(rest, [self.shards["conv_w"]])

    def weights_from(self, gathered):
        names = [n for n, _, _ in BIG if n != "w_in" and n not in self.LATE]
        conv = gathered[-1]
        taps, width = conv.shape[1:]
        return ({n: self._unshard(n, g) for n, g in zip(names, gathered)},
                conv.transpose(1, 0, 2).reshape(taps, 4 * width))

    def gather_late(self):
        return _gather_plan([self.shards[n].astype(BF16) for n in self.LATE])

    def late_weights_from(self, gathered):
        return {n: self._unshard(n, g) for n, g in zip(self.LATE, gathered)}

    def scatter(self, G, names, tag):
        slabs = self._slabs(G, names)
        todo = [n for n in names if n not in self.theirs]
        if todo:
            got = _swap_halves([s for s, n in zip(slabs, names) if n in todo], "grad_swap_cores_" + tag)
            self.theirs.update(zip(todo, got))
        pairs = [_pair_sum(s, self.theirs[n], self.where[1:], "pair_sum_" + n) for s, n in zip(slabs, names)]
        self.pairs.update(zip(names, pairs))
        return _scatter_plan(pairs)

    @staticmethod
    def _slabs(G, names):
        return [G[n].reshape(4, 2, (r if axis == 0 else cc) // 2, cc if axis == 0 else r)
                for n, (r, cc), axis in BIG if n in names]

    def swap_behind(self, G, names):
        slabs = self._slabs(G, names)
        n = len(slabs)

        def copies(ins, outs, send, recv, scratch):
            x, y, c = lax.axis_index("x"), lax.axis_index("y"), lax.axis_index("c")
            mk = lambda src, dst, p: pltpu.make_async_remote_copy(
                src_ref=src, dst_ref=dst, send_sem=send.at[p], recv_sem=recv.at[p], device_id=(x, y, 1 - c),
                device_id_type=MESH)
            return ([mk(ins[p].at[j, 1 - c], outs[p].at[j], p) for p in range(n) for j in range(4)],
                    [mk(outs[p], outs[p], p) for p in range(n)])

        def start(*refs):
            for cp in copies(*refs)[0]:
                cp.start()

        def finish(*refs):
            for cp in copies(*refs)[1]:
                cp.wait()

        return _Hosted(slabs, [jax.ShapeDtypeStruct((4,) + s.shape[2:], s.dtype) for s in slabs], n, start, finish)

    def swapped(self, names, got):
        self.theirs.update(zip(names, got))

    def collect(self, names, landed):
        self.landed.update(zip(names, landed))

    def finish(self):
        names = [n for n, _, _ in BIG]
        halves = [_chip_sum(self.landed[n], self.pairs[n], self.where, "chip_sum_" + n) for n in names]
        out = {}
        for (name, (r, cc), axis), both in zip(BIG, _join_halves(halves)):
            full = both.reshape(-1, both.shape[-1])
            out[name] = full if axis == 0 else full.T
        return out


def _pack_rows(parts, rows, dtype):
    flat = jnp.concatenate([p.reshape(-1).astype(dtype) for p in parts])
    return jnp.pad(flat, (0, rows * LANES - flat.shape[0])).reshape(rows, LANES)


def _small_rows(n):
    return max(n // LANES, 1)


def _pack_small(vals):
    rows = []
    for name, n in SMALL:
        r = _small_rows(n)
        rows.append(jnp.pad(vals[name].reshape(-1), (0, r * LANES - n)).reshape(r, LANES))
    flat = jnp.concatenate(rows, axis=0)
    return jnp.pad(flat, ((0, SMALL_ROWS - flat.shape[0]), (0, 0)))


def _unpack_small(pack):
    out, r0 = {}, 0
    for name, n in SMALL:
        r = _small_rows(n)
        out[name] = pack[r0:r0 + r].reshape(-1)[:n]
        r0 += r
    return out


def kernel(x, mem, norm1_g, w_in, conv_w, a_log, dt_bias, gdn_norm_g, sb_q_norm_g, sb_k_norm_g, mem_norm_g, w_mem_kv, mem_q_norm_g, mem_k_norm_g, w_br_gdn, w_br_sb, w_br_mem, w_o, norm2_g, w_up, w_down, loss_target, m_norm1_g, m_w_in, m_conv_w, m_a_log, m_dt_bias, m_gdn_norm_g, m_sb_q_norm_g, m_sb_k_norm_g, m_mem_norm_g, m_w_mem_kv, m_mem_q_norm_g, m_mem_k_norm_g, m_w_br_gdn, m_w_br_sb, m_w_br_mem, m_w_o, m_norm2_g, m_w_up, m_w_down, v_norm1_g, v_w_in, v_conv_w, v_a_log, v_dt_bias, v_gdn_norm_g, v_sb_q_norm_g, v_sb_k_norm_g, v_mem_norm_g, v_w_mem_kv, v_mem_q_norm_g, v_mem_k_norm_g, v_w_br_gdn, v_w_br_sb, v_w_br_mem, v_w_o, v_norm2_g, v_w_up, v_w_down):
    wd = dict(norm1_g=norm1_g, w_in=w_in, conv_w=conv_w, a_log=a_log, dt_bias=dt_bias, gdn_norm_g=gdn_norm_g,
              sb_q_norm_g=sb_q_norm_g, sb_k_norm_g=sb_k_norm_g, mem_norm_g=mem_norm_g, w_mem_kv=w_mem_kv,
              mem_q_norm_g=mem_q_norm_g, mem_k_norm_g=mem_k_norm_g, w_br_gdn=w_br_gdn, w_br_sb=w_br_sb,
              w_br_mem=w_br_mem, w_o=w_o, norm2_g=norm2_g, w_up=w_up, w_down=w_down)
    md = dict(norm1_g=m_norm1_g, w_in=m_w_in, conv_w=m_conv_w, a_log=m_a_log, dt_bias=m_dt_bias,
              gdn_norm_g=m_gdn_norm_g, sb_q_norm_g=m_sb_q_norm_g, sb_k_norm_g=m_sb_k_norm_g,
              mem_norm_g=m_mem_norm_g, w_mem_kv=m_w_mem_kv, mem_q_norm_g=m_mem_q_norm_g,
              mem_k_norm_g=m_mem_k_norm_g, w_br_gdn=m_w_br_gdn, w_br_sb=m_w_br_sb, w_br_mem=m_w_br_mem, w_o=m_w_o,
              norm2_g=m_norm2_g, w_up=m_w_up, w_down=m_w_down)
    vd = dict(norm1_g=v_norm1_g, w_in=v_w_in, conv_w=v_conv_w, a_log=v_a_log, dt_bias=v_dt_bias,
              gdn_norm_g=v_gdn_norm_g, sb_q_norm_g=v_sb_q_norm_g, sb_k_norm_g=v_sb_k_norm_g,
              mem_norm_g=v_mem_norm_g, w_mem_kv=v_w_mem_kv, mem_q_norm_g=v_mem_q_norm_g,
              mem_k_norm_g=v_mem_k_norm_g, w_br_gdn=v_w_br_gdn, w_br_sb=v_w_br_sb, w_br_mem=v_w_br_mem, w_o=v_w_o,
              norm2_g=v_norm2_g, w_up=v_w_up, w_down=v_w_down)
    wd, md, vd = ({n: a[0] for n, a in d.items()} for d in (wd, md, vd))
    chip = 2 * lax.axis_index("x") + lax.axis_index("y")
    conv_shard = wd["conv_w"].shape

    dist = _Dist(wd)
    W = dict(zip(("w_main", "w_ab"), dist.gather_first()))
    P = {n: wd[n].reshape(1, -1) for n, _ in SMALL}

    sq_err, grad_x, G = _local_step(x[0], mem[0], loss_target[0], W, P, dist)
    loss = lax.psum(0.5 / D_MODEL * jnp.sum(sq_err), ("x", "y", "c"))

    g_big = dist.finish()

    spack = jnp.concatenate([_pack_small(G), G["conv_w"].reshape(CONV_ROWS, LANES)], axis=0)
    g_small = _sum_slots(_gather_all(spack, "gather_small_grads"), "small_grad_sum")
    g_conv_full = g_small[SMALL_ROWS:].reshape(conv_shard[0], 4 * conv_shard[1])
    g_conv = lax.dynamic_slice_in_dim(g_conv_full, chip * conv_shard[1], conv_shard[1], axis=1)

    grads, deltas, new_m, new_v = dict(g_big), {}, {}, {}
    for name, _, _ in BIG:
        deltas[name], new_m[name], new_v[name] = _adamw(wd[name], g_big[name], md[name], vd[name], "adamw_" + name)
    pack_sm = lambda d: jnp.concatenate([_pack_small(d), _pack_rows([d["conv_w"]], APACK_ROWS - SMALL_ROWS, F32)], axis=0)
    g_sm = jnp.concatenate([g_small[:SMALL_ROWS], _pack_rows([g_conv], APACK_ROWS - SMALL_ROWS, F32)], axis=0)
    small = (g_sm,) + _adamw(pack_sm(wd), g_sm, pack_sm(md), pack_sm(vd), "adamw_small")
    for out, pack in zip((grads, deltas, new_m, new_v), small):
        out.update(_unpack_small(pack[:SMALL_ROWS]))
        out["conv_w"] = pack[SMALL_ROWS:].reshape(-1)[:conv_shard[0] * conv_shard[1]].reshape(conv_shard)

    return (loss, grad_x[None], *[d[n][None] for d in (grads, deltas, new_m, new_v) for n in WEIGHTS])
```
